```python
import math
import jax, jax.numpy as jnp
from jax import lax
import numpy as np

D_MODEL = 1024
BATCH = 1
SEQ = 16384
DEPTH = 1
DEC_BATCH = 8
DEC_SEQ = 32
PAST_LEN = 1024

CHUNK = 64
D_MIX = D_MODEL
HG_WIDTH = D_MIX // 2
HG_HEADS = 4
HG_DIM = HG_WIDTH // HG_HEADS
DA_WIDTH = D_MIX - HG_WIDTH
DA_HEADS = 4
DA_VDIM = DA_WIDTH // DA_HEADS
DA_QKDIM = DA_VDIM // 2
N_BUCKETS = 32
MAX_DIST = 128
Q_BLOCK = 128
N_EXPERTS = 64
TOP_K = 8
N_GROUPS = 8
TOP_GROUPS = 4
D_EXPERT = D_MODEL // 4
ROUTE_SCALE = 2.5
MOE_BLOCK = 128
EPS = 1e-6
IN_COLS = 4 * HG_WIDTH + 3 * DA_WIDTH
SPLITS = (HG_WIDTH, 2 * HG_WIDTH, 3 * HG_WIDTH, 4 * HG_WIDTH,
          4 * HG_WIDTH + DA_WIDTH, 4 * HG_WIDTH + 2 * DA_WIDTH)

kernel_name = "hymba_hgrn2_diffattn_moe_stream_step"

F32 = jnp.float32


def rmsnorm(x, g):
    xf = x.astype(F32)
    y = xf * lax.rsqrt(jnp.mean(xf * xf, axis=-1, keepdims=True) + EPS)
    return (y * g.astype(F32)).astype(x.dtype)


def adaln_mod(c, w_ada_l, b_ada_l):
    mod = jax.nn.silu(c) @ w_ada_l + b_ada_l
    return jnp.split(mod[:, None, :], 6, axis=-1)


def rel_bucket(rel):
    nb = N_BUCKETS // 2
    max_exact = nb // 2
    side = jnp.where(rel > 0, nb, 0)
    n = jnp.abs(rel)
    large = max_exact + (jnp.log(jnp.maximum(n, 1).astype(F32) / max_exact)
                         / math.log(MAX_DIST / max_exact) * (nb - max_exact)).astype(jnp.int32)
    large = jnp.minimum(large, nb - 1)
    return side + jnp.where(n < max_exact, n, large)


def rel_bias(qpos, kpos, table):
    b = rel_bucket(kpos[None, :] - qpos[:, None])
    return jnp.transpose(table[b], (2, 0, 1)).astype(F32)


def diff_attend(q, k, v, qpos, kpos, lam, table):
    s = jnp.einsum('bqhmd,bkhmd->bhmqk', q.astype(F32), k.astype(F32)) * (DA_QKDIM ** -0.5)
    s = s + rel_bias(qpos, kpos, table)[None, :, None]
    visible = (kpos[None, :] // CHUNK) <= (qpos[:, None] // CHUNK)
    s = jnp.where(visible, s, -jnp.inf)
    p = jax.nn.softmax(s, axis=-1)
    a = p[:, :, 0] - lam * p[:, :, 1]
    return jnp.einsum('bhqk,bkhd->bqhd', a, v.astype(F32))


def diff_attend_blocked(q, k, v, lam, table):
    B, T = q.shape[:2]
    nq = T // Q_BLOCK
    qb = jnp.moveaxis(q.reshape(B, nq, Q_BLOCK, DA_HEADS, 2, DA_QKDIM), 1, 0)
    kpos = jnp.arange(T)

    def one(args):
        qi, i = args
        return diff_attend(qi, k, v, i * Q_BLOCK + jnp.arange(Q_BLOCK), kpos, lam, table)

    o = lax.map(one, (qb, jnp.arange(nq)))
    return jnp.moveaxis(o, 0, 1).reshape(B, T, DA_HEADS, DA_VDIM)


def hgrn2_chunk(S0, q, logf, i):
    k = -jnp.expm1(logf)
    b = jnp.cumsum(logf, axis=1)
    L = q.shape[1]
    causal = jnp.tril(jnp.ones((L, L), bool))
    dec = jnp.exp(jnp.where(causal[None, :, :, None, None],
                            b[:, :, None] - b[:, None, :], -jnp.inf))
    scores = jnp.einsum('bthc,btshc,bshc->bhts', q, dec, k)
    o = (jnp.einsum('bhts,bshv->bthv', scores, i)
         + jnp.einsum('bthc,bhcv->bthv', q * jnp.exp(b), S0))
    b_last = b[:, -1]
    S1 = (jnp.exp(b_last)[..., None] * S0
          + jnp.einsum('bshc,bshv->bhcv', k * jnp.exp(b_last[:, None] - b), i))
    return S1, o


def hgrn2_prompt(q, logf, i):
    B, T, H, dk = q.shape
    nc = T // CHUNK

    def to_chunks(a):
        return jnp.moveaxis(a.reshape(B, nc, CHUNK, H, a.shape[-1]), 1, 0)

    S0 = jnp.zeros((B, H, dk, i.shape[-1]), F32)
    S, o = lax.scan(lambda S, xs: hgrn2_chunk(S, *xs), S0,
                    (to_chunks(q), to_chunks(logf), to_chunks(i)))
    return jnp.moveaxis(o, 0, 1).reshape(B, T, H, -1), S


def token_mixers(h, w_in_l, w_out_l, lb, hg_gain, lam, lam_init, da_gain, table,
                 S0, k_past, v_past):
    B, T, _ = h.shape
    z = h @ w_in_l
    qh, fh, ih, gh, qa, ka, va = jnp.split(z, SPLITS, axis=-1)

    def heads(a):
        return a.reshape(B, T, HG_HEADS, HG_DIM).astype(F32)

    q = jax.nn.silu(heads(qh))
    lbh = lb.reshape(HG_HEADS, HG_DIM)
    logf = jnp.log(lbh + (1.0 - lbh) * jax.nn.sigmoid(heads(fh)))
    iv = heads(ih)
    if k_past is None:
        o_hg, S_new = hgrn2_prompt(q, logf, iv)
    else:
        S_new, o_hg = hgrn2_chunk(S0.astype(F32), q, logf, iv)
    o_hg = rmsnorm(o_hg, hg_gain) * jax.nn.silu(heads(gh))

    qd = qa.reshape(B, T, DA_HEADS, 2, DA_QKDIM)
    kd = ka.reshape(B, T, DA_HEADS, 2, DA_QKDIM)
    vd = va.reshape(B, T, DA_HEADS, DA_VDIM)
    if k_past is None:
        o_da = diff_attend_blocked(qd, kd, vd, lam, table)
    else:
        P = k_past.shape[1]
        k_all = jnp.concatenate(
            [k_past.reshape(B, P, DA_HEADS, 2, DA_QKDIM).astype(kd.dtype), kd], axis=1)
        v_all = jnp.concatenate([v_past.astype(vd.dtype), vd], axis=1)
        o_da = diff_attend(qd, k_all, v_all, P + jnp.arange(T), jnp.arange(P + T), lam, table)
    o_da = rmsnorm(o_da, da_gain) * (1.0 - lam_init)

    o = jnp.concatenate([o_hg.reshape(B, T, HG_WIDTH), o_da.reshape(B, T, DA_WIDTH)],
                        axis=-1).astype(h.dtype)
    return (o @ w_out_l,
            ka.reshape(B, T, DA_HEADS, 2 * DA_QKDIM),
            va.reshape(B, T, DA_HEADS, DA_VDIM),
            S_new.astype(h.dtype))


def moe_ffn(h, w_router_l, e_bias, w_gate_up_l, w_down_l, ws_gate_up_l, ws_down_l):
    B, T, D = h.shape
    xt = h.reshape(-1, D)
    N = xt.shape[0]
    scores = jax.nn.sigmoid((xt @ w_router_l).astype(F32))
    sel = scores + e_bias.astype(F32)
    gscore = lax.top_k(sel.reshape(N, N_GROUPS, N_EXPERTS // N_GROUPS), 2)[0].sum(-1)
    _, gidx = lax.top_k(gscore, TOP_GROUPS)
    gmask = jax.nn.one_hot(gidx, N_GROUPS, dtype=F32).sum(1) > 0
    emask = jnp.repeat(gmask, N_EXPERTS // N_GROUPS, axis=1)
    _, eidx = lax.top_k(jnp.where(emask, sel, -jnp.inf), TOP_K)
    w = jnp.take_along_axis(scores, eidx, axis=1)
    w = w / jnp.sum(w, axis=-1, keepdims=True) * ROUTE_SCALE

    P = N * TOP_K
    e_flat = eidx.reshape(-1)
    tok_flat = jnp.repeat(jnp.arange(N, dtype=jnp.int32), TOP_K)
    order = jnp.argsort(e_flat)
    e_sorted = e_flat[order]
    counts = jnp.zeros((N_EXPERTS,), jnp.int32).at[e_flat].add(1)
    padded = (counts + MOE_BLOCK - 1) // MOE_BLOCK * MOE_BLOCK
    start = jnp.cumsum(counts) - counts
    pend = jnp.cumsum(padded)
    pstart = pend - padded
    dest = pstart[e_sorted] + jnp.arange(P) - start[e_sorted]
    NB = -(-P // MOE_BLOCK) + N_EXPERTS
    tok_buf = jnp.full((NB * MOE_BLOCK,), N, jnp.int32).at[dest].set(tok_flat[order])
    w_buf = jnp.zeros((NB * MOE_BLOCK,), F32).at[dest].set(w.reshape(-1)[order])
    block_e = jnp.minimum(jnp.searchsorted(pend, jnp.arange(NB) * MOE_BLOCK, side='right'),
                          N_EXPERTS - 1)
    x_pad = jnp.concatenate([xt, jnp.zeros((1, D), xt.dtype)], axis=0)

    def run_block(args):
        tok, wb, e = args
        xb = x_pad[tok]
        g, u = jnp.split(xb @ w_gate_up_l[e], 2, axis=-1)
        return ((jax.nn.silu(g) * u) @ w_down_l[e]) * wb[:, None].astype(xb.dtype)

    yb = lax.map(run_block, (tok_buf.reshape(NB, MOE_BLOCK), w_buf.reshape(NB, MOE_BLOCK), block_e))
    routed = jax.ops.segment_sum(yb.reshape(-1, D), tok_buf, num_segments=N + 1)[:N]
    gs, us = jnp.split(xt @ ws_gate_up_l, 2, axis=-1)
    shared = (jax.nn.silu(gs) * us) @ ws_down_l
    return (routed + shared).reshape(B, T, D)


def block(x, c, w_ada_l, b_ada_l, g_mix, g_ffn, mixer_w, moe_w, S0, k_past, v_past):
    sh1, sc1, ga1, sh2, sc2, ga2 = adaln_mod(c, w_ada_l, b_ada_l)
    h = rmsnorm(x, g_mix) * (1.0 + sc1) + sh1
    mix, k_new, v_new, S_new = token_mixers(h, *mixer_w, S0, k_past, v_past)
    x = x + ga1 * mix
    h = rmsnorm(x, g_ffn) * (1.0 + sc2) + sh2
    x = x + ga2 * moe_ffn(h, *moe_w)
    return x, k_new, v_new, S_new


def setup_inputs(seed: int = 0) -> dict:
    key = jax.random.key(seed)
    ks = jax.random.split(key, 26)
    D = D_MODEL

    def nrm(k, shape, s):
        return jax.random.normal(k, shape, jnp.float32) * s

    return {
        "x_prompt": nrm(ks[0], (BATCH, SEQ, D), 1.0),
        "x_sample": nrm(ks[1], (DEC_BATCH, DEC_SEQ, D), 1.0),
        "cache_k": nrm(ks[2], (DEPTH, DEC_BATCH, PAST_LEN, DA_HEADS, 2 * DA_QKDIM), 1.0),
        "cache_v": nrm(ks[3], (DEPTH, DEC_BATCH, PAST_LEN, DA_HEADS, DA_VDIM), 1.0),
        "state_hgrn": nrm(ks[4], (DEPTH, DEC_BATCH, HG_HEADS, HG_DIM, HG_DIM), 0.5),
        "c_prompt": nrm(ks[5], (BATCH, D), 1.0),
        "c_sample": nrm(ks[6], (DEC_BATCH, D), 1.0),
        "w_ada": nrm(ks[7], (DEPTH, D, 6 * D), 0.5 * D ** -0.5),
        "b_ada": nrm(ks[8], (DEPTH, 6 * D), 0.01),
        "norm_mix": 1.0 + nrm(ks[9], (DEPTH, D), 0.02),
        "norm_ffn": 1.0 + nrm(ks[10], (DEPTH, D), 0.02),
        "norm_final": 1.0 + nrm(ks[11], (D,), 0.02),
        "w_in": nrm(ks[12], (DEPTH, D, IN_COLS), D ** -0.5),
        "w_out": nrm(ks[13], (DEPTH, D_MIX, D), D_MIX ** -0.5),
        "hg_lb_logits": nrm(ks[14], (DEPTH + 1, HG_WIDTH), 0.5),
        "hg_norm": 1.0 + nrm(ks[15], (DEPTH, HG_DIM), 0.02),
        "da_lambda": nrm(ks[16], (DEPTH, 4, DA_QKDIM), 0.1),
        "da_norm": 1.0 + nrm(ks[17], (DEPTH, DA_VDIM), 0.02),
        "rel_bias_table": nrm(ks[18], (N_BUCKETS, DA_HEADS), 0.5),
        "w_router": nrm(ks[19], (DEPTH, D, N_EXPERTS), D ** -0.5),
        "router_bias": nrm(ks[20], (DEPTH, N_EXPERTS), 0.01),
        "w_gate_up": nrm(ks[21], (DEPTH, N_EXPERTS, D, 2 * D_EXPERT), D ** -0.5),
        "w_down": nrm(ks[22], (DEPTH, N_EXPERTS, D_EXPERT, D), D_EXPERT ** -0.5),
        "ws_gate_up": nrm(ks[23], (DEPTH, D, 2 * D_EXPERT), D ** -0.5),
        "ws_down": nrm(ks[24], (DEPTH, D_EXPERT, D), D_EXPERT ** -0.5),
    }


def reference(x_prompt, x_sample, cache_k, cache_v, state_hgrn, c_prompt, c_sample,
              w_ada, b_ada, norm_mix, norm_ffn, norm_final, w_in, w_out,
              hg_lb_logits, hg_norm, da_lambda, da_norm, rel_bias_table,
              w_router, router_bias, w_gate_up, w_down, ws_gate_up, ws_down):
    lb_all = jnp.cumsum(jax.nn.softmax(hg_lb_logits.astype(F32), axis=0), axis=0)
    xp, xs = x_prompt, x_sample
    kp_l, vp_l, sp_l, ks_l, vs_l, ss_l = [], [], [], [], [], []
    for l in range(DEPTH):
        lam_init = 0.8 - 0.6 * math.exp(-0.3 * l)
        lam_p = da_lambda[l].astype(F32)
        lam = (jnp.exp(jnp.sum(lam_p[0] * lam_p[1])) - jnp.exp(jnp.sum(lam_p[2] * lam_p[3]))
               + lam_init)
        mixer_w = (w_in[l], w_out[l], lb_all[l], hg_norm[l], lam, lam_init, da_norm[l],
                   rel_bias_table)
        moe_w = (w_router[l], router_bias[l], w_gate_up[l], w_down[l], ws_gate_up[l], ws_down[l])
        xp, kp, vp, sp = block(xp, c_prompt, w_ada[l], b_ada[l], norm_mix[l], norm_ffn[l],
                               mixer_w, moe_w, None, None, None)
        xs, kn, vn, sn = block(xs, c_sample, w_ada[l], b_ada[l], norm_mix[l], norm_ffn[l],
                               mixer_w, moe_w, state_hgrn[l], cache_k[l], cache_v[l])
        kp_l.append(kp); vp_l.append(vp); sp_l.append(sp)
        ks_l.append(kn); vs_l.append(vn); ss_l.append(sn)
    y_prompt = rmsnorm(xp, norm_final)
    y_sample = rmsnorm(xs, norm_final)
    k_prompt = jnp.stack(kp_l)
    v_prompt = jnp.stack(vp_l)
    hgrn_prompt = jnp.stack(sp_l)
    k_sample = jnp.stack(ks_l)
    v_sample = jnp.stack(vs_l)
    hgrn_sample = jnp.stack(ss_l)
    return (y_prompt, y_sample, k_prompt, v_prompt, hgrn_prompt, k_sample, v_sample, hgrn_sample)
```

```python
import functools
import math

import numpy as np
import jax
import jax.numpy as jnp
from jax import lax
from jax.experimental import pallas as pl
from jax.experimental.pallas import tpu as pltpu

F32 = jnp.float32
BF16 = jnp.bfloat16
I32 = jnp.int32
HIGHEST = lax.Precision.HIGHEST

D_MODEL = 1024
CHUNK = 64
HG_HEADS = 4
HG_DIM = 128
HG_WIDTH = HG_HEADS * HG_DIM
DA_HEADS = 4
DA_VDIM = 128
DA_QKDIM = 64
DA_WIDTH = DA_HEADS * DA_VDIM
N_BUCKETS = 32
MAX_DIST = 128
N_EXPERTS = 64
TOP_K = 8
N_GROUPS = 8
GROUP_SIZE = N_EXPERTS // N_GROUPS
TOP_GROUPS = 4
D_EXPERT = 256
ROUTE_SCALE = 2.5
EPS = 1e-6
LAM_INIT = 0.8 - 0.6 * math.exp(-0.3 * 0)

NEG_BIG = -1e30
MASK_BUCKET = N_BUCKETS
V7X_VMEM_LIMIT = 48 * 1024 * 1024

ATT_TILE = 512
HGRN_CHUNK = 256
INPROJ_TILE = 512
POST_TILE = 256
DISPATCH_TILE = 512
COMBINE_TILE = 128
MOE_BLOCK_ROWS = 512
DMA_LAG = 16


def _sigmoid(x):
    return 1.0 / (1.0 + jnp.exp(-x))


def _silu(x):
    return x * _sigmoid(x)


def _dot(a, b, **kw):
    return jnp.dot(a, b, preferred_element_type=F32, **kw)


def _dot_nt(a, b, **kw):
    return lax.dot_general(a, b, (((1,), (1,)), ((), ())), preferred_element_type=F32, **kw)


def _dot_tn(a, b, **kw):
    return lax.dot_general(a, b, (((0,), (0,)), ((), ())), preferred_element_type=F32, **kw)


def _params(sem, vmem=V7X_VMEM_LIMIT):
    return pltpu.CompilerParams(dimension_semantics=sem, vmem_limit_bytes=vmem)


def _adaln_kernel(c_ref, w_ref, b_ref, o_ref):
    s = _silu(c_ref[...])
    o_ref[...] = _dot(s, w_ref[...], precision=HIGHEST) + b_ref[...]


def _adaln(c_all, w_ada, b_ada):
    rows, d = c_all.shape
    cols = w_ada.shape[1]
    blk = 1024
    return pl.pallas_call(
        _adaln_kernel,
        grid=(cols // blk,),
        in_specs=[
            pl.BlockSpec((rows, d), lambda j: (0, 0)),
            pl.BlockSpec((d, blk), lambda j: (0, j)),
            pl.BlockSpec((1, blk), lambda j: (0, j)),
        ],
        out_specs=pl.BlockSpec((rows, blk), lambda j: (0, j)),
        out_shape=jax.ShapeDtypeStruct((rows, cols), F32),
        compiler_params=_params(("parallel",)),
        name="adaln",
    )(c_all, w_ada, b_ada.reshape(1, cols))


def _lam_kernel(l_ref, o_ref):
    l = l_ref[...].astype(F32)
    a = jnp.sum(l[0:1] * l[1:2], axis=-1, keepdims=True)
    b = jnp.sum(l[2:3] * l[3:4], axis=-1, keepdims=True)
    lam = jnp.exp(a) - jnp.exp(b) + LAM_INIT
    o_ref[...] = jnp.broadcast_to(lam, o_ref.shape)


def _lam(da_lambda_l):
    return pl.pallas_call(
        _lam_kernel,
        out_shape=jax.ShapeDtypeStruct((8, 128), F32),
        name="lam",
    )(da_lambda_l)


def _rel_bucket(rel):
    nb = N_BUCKETS // 2
    max_exact = nb // 2
    side = jnp.where(rel > 0, nb, 0)
    n = jnp.abs(rel)
    large = max_exact + (jnp.log(jnp.maximum(n, 1).astype(F32) / max_exact)
                         / math.log(MAX_DIST / max_exact) * (nb - max_exact)).astype(I32)
    large = jnp.minimum(large, nb - 1)
    return side + jnp.where(n < max_exact, n, large)


def _bias_kernel(tab_ref, idx_ref, o_ref, *, shift_bucket, dup_lanes):
    h = pl.program_id(0)
    idx = idx_ref[...]
    shift = tab_ref[shift_bucket, h] if shift_bucket is not None else 0.0
    acc = jnp.zeros(idx.shape, F32)
    for j in range(N_BUCKETS):
        acc = jnp.where(idx == j, tab_ref[j, h] - shift, acc)
    acc = jnp.where(idx == MASK_BUCKET, NEG_BIG, acc)
    if dup_lanes:
        acc = jnp.concatenate([acc, acc], axis=1)
    o_ref[...] = acc


def _bias_tiles(table, idx, *, shift_bucket, dup_lanes):
    k, r, c = idx.shape
    cc = 2 * c if dup_lanes else c
    return pl.pallas_call(
        functools.partial(_bias_kernel, shift_bucket=shift_bucket, dup_lanes=dup_lanes),
        grid=(DA_HEADS, k),
        in_specs=[
            pl.BlockSpec(memory_space=pltpu.SMEM),
            pl.BlockSpec((None, r, c), lambda h, d: (d, 0, 0)),
        ],
        out_specs=pl.BlockSpec((None, None, r, cc), lambda h, d: (h, d, 0, 0)),
        out_shape=jax.ShapeDtypeStruct((DA_HEADS, k, r, cc), F32),
        compiler_params=_params(("parallel", "parallel")),
        name="rel_bias",
    )(table, idx)


def _inproj_kernel(x_ref, g_ref, sc_ref, sh_ref, w_ref,
                   zh_ref, q_ref, k_ref, v_ref, kb_ref, vb_ref):
    x = x_ref[...]
    ms = jnp.mean(x * x, axis=-1, keepdims=True)
    h = x * lax.rsqrt(ms + EPS) * g_ref[...]
    h = h * (1.0 + sc_ref[...]) + sh_ref[...]
    hb = h.astype(BF16)
    c0 = 4 * HG_WIDTH
    zh_ref[...] = _dot(hb, w_ref[:, 0:c0])
    zq = _dot(hb, w_ref[:, c0:c0 + DA_WIDTH])
    q_ref[...] = (zq * (DA_QKDIM ** -0.5)).astype(BF16)
    zk = _dot(hb, w_ref[:, c0 + DA_WIDTH:c0 + 2 * DA_WIDTH])
    k_ref[...] = zk
    kb_ref[...] = zk.astype(BF16)
    zv = _dot(hb, w_ref[:, c0 + 2 * DA_WIDTH:c0 + 3 * DA_WIDTH])
    v_ref[...] = zv
    vb_ref[...] = zv.astype(BF16)


def _mod_spec(mod, tm):
    if mod.shape[0] == 1:
        return pl.BlockSpec((1, mod.shape[1]), lambda i: (0, 0))
    return pl.BlockSpec((tm, mod.shape[1]), lambda i: (i, 0))


def _inproj(x, g, sc, sh, w_in_b, tm):
    n, d = x.shape
    cols = w_in_b.shape[1]
    row = lambda i: (i, 0)
    return pl.pallas_call(
        _inproj_kernel,
        grid=(n // tm,),
        in_specs=[
            pl.BlockSpec((tm, d), row),
            pl.BlockSpec((1, d), lambda i: (0, 0)),
            _mod_spec(sc, tm),
            _mod_spec(sh, tm),
            pl.BlockSpec((d, cols), lambda i: (0, 0)),
        ],
        out_specs=[
            pl.BlockSpec((tm, 4 * HG_WIDTH), row),
            pl.BlockSpec((tm, DA_WIDTH), row),
            pl.BlockSpec((tm, DA_WIDTH), row),
            pl.BlockSpec((tm, DA_WIDTH), row),
            pl.BlockSpec((tm, DA_WIDTH), row),
            pl.BlockSpec((tm, DA_WIDTH), row),
        ],
        out_shape=[
            jax.ShapeDtypeStruct((n, 4 * HG_WIDTH), F32),
            jax.ShapeDtypeStruct((n, DA_WIDTH), BF16),
            jax.ShapeDtypeStruct((n, DA_WIDTH), F32),
            jax.ShapeDtypeStruct((n, DA_WIDTH), F32),
            jax.ShapeDtypeStruct((n, DA_WIDTH), BF16),
            jax.ShapeDtypeStruct((n, DA_WIDTH), BF16),
        ],
        compiler_params=_params(("parallel",)),
        name="inproj",
    )(x, g, sc, sh, w_in_b)


def _hgrn_consts(c):
    levels = int(round(math.log2(c)))
    assert 1 << levels == c
    t = np.arange(c)[:, None]
    r = np.arange(c)[None, :]
    blocks = [r <= t]
    for l in range(levels):
        m = 1 << l
        mid = (t // (2 * m)) * (2 * m) + m - 1
        later = (t & m) != 0
        blocks.append(np.where(later, (r > mid) & (r <= t), (r > t) & (r <= mid)))
    mall = np.concatenate(blocks, axis=0).astype(np.float32)
    x = np.maximum(t ^ r, 1)
    lv = np.where(t == r, -1, np.where(t > r, np.floor(np.log2(x)).astype(np.int64), -2))
    return jnp.asarray(mall, dtype=BF16), jnp.asarray(lv, dtype=I32), levels


def _hgrn_kernel(zh_ref, s0_ref, lbl_ref, gain_ref, mall_ref, lv_ref,
                 o_ref, sout_ref, st_ref, *, c, levels):
    ci = pl.program_id(1)

    @pl.when(ci == 0)
    def _():
        for h in range(HG_HEADS):
            st_ref[h] = s0_ref[h].astype(F32).T

    lbl = lbl_ref[...].astype(F32)
    mx = jnp.maximum(lbl[0:1], lbl[1:2])
    e0 = jnp.exp(lbl[0:1] - mx)
    e1 = jnp.exp(lbl[1:2] - mx)
    lb = e0 / (e0 + e1)

    xq = zh_ref[:, 0:HG_WIDTH]
    xf = zh_ref[:, HG_WIDTH:2 * HG_WIDTH]
    q = _silu(xq)
    y = lb + (1.0 - lb) * _sigmoid(xf)
    logf = jnp.log(y)
    kk = 1.0 - y

    l1 = logf.astype(BF16)
    r1 = logf - l1.astype(F32)
    l2 = r1.astype(BF16)
    l3 = (r1 - l2.astype(F32)).astype(BF16)
    mall = mall_ref[...]
    e_all = _dot(mall, l1) + _dot(mall, l2) + _dot(mall, l3)

    lv = lv_ref[...]
    gain = gain_ref[...].astype(F32)
    for h in range(HG_HEADS):
        sl = slice(h * HG_DIM, (h + 1) * HG_DIM)
        qh = q[:, sl]
        kh = kk[:, sl]
        ih = zh_ref[:, 2 * HG_WIDTH + h * HG_DIM:2 * HG_WIDTH + (h + 1) * HG_DIM]
        gh = zh_ref[:, 3 * HG_WIDTH + h * HG_DIM:3 * HG_WIDTH + (h + 1) * HG_DIM]
        bh = e_all[0:c, sl]
        ihb = ih.astype(BF16)
        a = jnp.where(lv == -1, _dot_nt(qh.astype(BF16), kh.astype(BF16)), 0.0)
        for l in range(levels):
            f = jnp.exp(e_all[(l + 1) * c:(l + 2) * c, sl])
            p = _dot_nt((qh * f).astype(BF16), (kh * f).astype(BF16))
            a = a + jnp.where(lv == l, p, 0.0)
        st = st_ref[h]
        o = _dot(a.astype(BF16), ihb) + _dot_nt((qh * jnp.exp(bh)).astype(BF16), st.astype(BF16))
        bl = bh[c - 1:c, :]
        kd = (kh * jnp.exp(bl - bh)).astype(BF16)
        st_ref[h] = st * jnp.exp(bl) + _dot_tn(ihb, kd)
        ms = jnp.mean(o * o, axis=-1, keepdims=True)
        on = o * lax.rsqrt(ms + EPS) * gain
        o_ref[:, sl] = (on * _silu(gh)).astype(o_ref.dtype)

    @pl.when(ci == pl.num_programs(1) - 1)
    def _():
        for h in range(HG_HEADS):
            sout_ref[h] = st_ref[h].T.astype(sout_ref.dtype)


def _hgrn(zh, s0, lb_logits, gain, batch, seq, c):
    mall, lv, levels = _hgrn_consts(c)
    nc = seq // c
    return pl.pallas_call(
        functools.partial(_hgrn_kernel, c=c, levels=levels),
        grid=(batch, nc),
        in_specs=[
            pl.BlockSpec((c, 4 * HG_WIDTH), lambda b, i: (b * nc + i, 0)),
            pl.BlockSpec((None, HG_HEADS, HG_DIM, HG_DIM), lambda b, i: (b, 0, 0, 0)),
            pl.BlockSpec(lb_logits.shape, lambda b, i: (0, 0)),
            pl.BlockSpec((1, HG_DIM), lambda b, i: (0, 0)),
            pl.BlockSpec(mall.shape, lambda b, i: (0, 0)),
            pl.BlockSpec(lv.shape, lambda b, i: (0, 0)),
        ],
        out_specs=[
            pl.BlockSpec((c, HG_WIDTH), lambda b, i: (b * nc + i, 0)),
            pl.BlockSpec((None, HG_HEADS, HG_DIM, HG_DIM), lambda b, i: (b, 0, 0, 0)),
        ],
        out_shape=[
            jax.ShapeDtypeStruct((batch * seq, HG_WIDTH), BF16),
            jax.ShapeDtypeStruct((batch, HG_HEADS, HG_DIM, HG_DIM), F32),
        ],
        scratch_shapes=[pltpu.VMEM((HG_HEADS, HG_DIM, HG_DIM), F32)],
        compiler_params=_params(("parallel", "arbitrary")),
        name="hgrn2",
    )(zh, s0, lb_logits, gain.reshape(1, HG_DIM), mall, lv)


def _attn_kernel(qi_ref, kj_ref, k_ref, qt_ref, vt_ref, bias_ref, lam_ref, gain_ref,
                 o_ref, qz_ref, m_ref, l_ref, acc_ref, *, t):
    p = pl.program_id(1)
    qi = qi_ref[p]
    kj = kj_ref[p]

    @pl.when(kj == 0)
    def _():
        qt = qt_ref[...]
        row = lax.broadcasted_iota(I32, qt.shape, 0)
        zero = jnp.zeros_like(qt)
        qz_ref[:, 0:t] = jnp.where(row < DA_QKDIM, qt, zero)
        qz_ref[:, t:2 * t] = jnp.where(row >= DA_QKDIM, qt, zero)
        m_ref[...] = jnp.full(m_ref.shape, NEG_BIG, F32)
        l_ref[...] = jnp.zeros(l_ref.shape, F32)
        acc_ref[...] = jnp.zeros(acc_ref.shape, F32)

    s = _dot(k_ref[...], qz_ref[...])

    def update(s):
        m_prev = m_ref[...]
        m_new = jnp.maximum(m_prev, jnp.max(s, axis=0, keepdims=True))
        alpha = jnp.exp(m_prev - m_new)
        pr = jnp.exp(s - m_new)
        l_ref[...] = alpha * l_ref[...] + jnp.sum(pr, axis=0, keepdims=True)
        acc_ref[...] = alpha * acc_ref[...] + _dot(vt_ref[...], pr.astype(BF16))
        m_ref[...] = m_new

    @pl.when(kj + 1 < qi)
    def _():
        update(s)

    @pl.when(kj + 1 == qi)
    def _():
        update(s + bias_ref[1])

    @pl.when(kj == qi)
    def _():
        update(s + bias_ref[0])
        lam = lam_ref[0:1, 0:1]
        l = l_ref[...]
        acc = acc_ref[...]
        o = acc[:, 0:t] / l[:, 0:t] - lam * (acc[:, t:2 * t] / l[:, t:2 * t])
        ms = jnp.mean(o * o, axis=0, keepdims=True)
        on = o * lax.rsqrt(ms + EPS) * gain_ref[...].astype(F32) * (1.0 - LAM_INIT)
        o_ref[...] = on.T.astype(o_ref.dtype)


def _attn_prompt(kb, qt, vt, bias, lam, gain, t):
    n = kb.shape[0]
    nt = n // t
    qi = np.concatenate([np.full(i + 1, i) for i in range(nt)]).astype(np.int32)
    kj = np.concatenate([np.arange(i + 1) for i in range(nt)]).astype(np.int32)
    grid_spec = pltpu.PrefetchScalarGridSpec(
        num_scalar_prefetch=2,
        grid=(DA_HEADS, len(qi)),
        in_specs=[
            pl.BlockSpec((t, DA_VDIM), lambda h, p, qi, kj: (kj[p], h)),
            pl.BlockSpec((DA_VDIM, t), lambda h, p, qi, kj: (h, qi[p])),
            pl.BlockSpec((DA_VDIM, t), lambda h, p, qi, kj: (h, kj[p])),
            pl.BlockSpec((None, 2, t, 2 * t), lambda h, p, qi, kj: (h, 0, 0, 0)),
            pl.BlockSpec((8, 128), lambda h, p, qi, kj: (0, 0)),
            pl.BlockSpec((DA_VDIM, 1), lambda h, p, qi, kj: (0, 0)),
        ],
        out_specs=pl.BlockSpec((t, DA_VDIM), lambda h, p, qi, kj: (qi[p], h)),
        scratch_shapes=[
            pltpu.VMEM((DA_VDIM, 2 * t), BF16),
            pltpu.VMEM((1, 2 * t), F32),
            pltpu.VMEM((1, 2 * t), F32),
            pltpu.VMEM((DA_VDIM, 2 * t), F32),
        ],
    )
    return pl.pallas_call(
        functools.partial(_attn_kernel, t=t),
        grid_spec=grid_spec,
        out_shape=jax.ShapeDtypeStruct((n, DA_WIDTH), BF16),
        compiler_params=_params(("parallel", "arbitrary")),
        name="diff_attn_prompt",
    )(jnp.asarray(qi), jnp.asarray(kj), kb, qt, vt, bias, lam, gain.reshape(DA_VDIM, 1))


def _attn_step_kernel(q_ref, kp_ref, vp_ref, kn_ref, vn_ref, bp_ref, bn_ref, lam_ref, gain_ref,
                      o_ref, *, tq, pad):
    q = q_ref[...]
    lane = lax.broadcasted_iota(I32, q.shape, 1)
    zero = jnp.zeros_like(q)
    qz = jnp.concatenate([jnp.where(lane < DA_QKDIM, q, zero),
                          jnp.where(lane >= DA_QKDIM, q, zero)], axis=0)
    kp = kp_ref[...].astype(BF16)
    vp = vp_ref[...].astype(BF16)
    zpad = jnp.zeros((pad - tq, DA_VDIM), BF16)
    kn = jnp.concatenate([kn_ref[...], zpad], axis=0)
    vn = jnp.concatenate([vn_ref[...], zpad], axis=0)
    bp = bp_ref[...]
    bn = bn_ref[...]
    sp = _dot_nt(qz, kp) + jnp.concatenate([bp, bp], axis=0)
    sn = _dot_nt(qz, kn) + jnp.concatenate([bn, bn], axis=0)
    m = jnp.maximum(jnp.max(sp, axis=-1, keepdims=True), jnp.max(sn, axis=-1, keepdims=True))
    pp = jnp.exp(sp - m)
    pn = jnp.exp(sn - m)
    l = jnp.sum(pp, axis=-1, keepdims=True) + jnp.sum(pn, axis=-1, keepdims=True)
    acc = _dot(pp.astype(BF16), vp) + _dot(pn.astype(BF16), vn)
    on = acc / l
    lam = lam_ref[0:1, 0:1]
    o = on[0:tq] - lam * on[tq:2 * tq]
    ms = jnp.mean(o * o, axis=-1, keepdims=True)
    o = o * lax.rsqrt(ms + EPS) * gain_ref[...].astype(F32) * (1.0 - LAM_INIT)
    o_ref[...] = o.astype(o_ref.dtype)


def _attn_step(qs, cache_k_l, cache_v_l, kb, vb, bias_p, bias_n, lam, gain, batch, tq):
    past = cache_k_l.shape[1]
    pad = bias_n.shape[-1]
    return pl.pallas_call(
        functools.partial(_attn_step_kernel, tq=tq, pad=pad),
        grid=(batch, DA_HEADS),
        in_specs=[
            pl.BlockSpec((tq, DA_VDIM), lambda b, h: (b, h)),
            pl.BlockSpec((None, past, DA_VDIM), lambda b, h: (b, 0, h)),
            pl.BlockSpec((None, past, DA_VDIM), lambda b, h: (b, 0, h)),
            pl.BlockSpec((tq, DA_VDIM), lambda b, h: (b, h)),
            pl.BlockSpec((tq, DA_VDIM), lambda b, h: (b, h)),
            pl.BlockSpec((None, None, tq, past), lambda b, h: (h, 0, 0, 0)),
            pl.BlockSpec((None, None, tq, pad), lambda b, h: (h, 0, 0, 0)),
            pl.BlockSpec((8, 128), lambda b, h: (0, 0)),
            pl.BlockSpec((1, DA_VDIM), lambda b, h: (0, 0)),
        ],
        out_specs=pl.BlockSpec((tq, DA_VDIM), lambda b, h: (b, h)),
        out_shape=jax.ShapeDtypeStruct((batch * tq, DA_WIDTH), BF16),
        compiler_params=_params(("parallel", "parallel")),
        name="diff_attn_step",
    )(qs, cache_k_l.reshape(batch, past, DA_WIDTH), cache_v_l.reshape(batch, past, DA_WIDTH),
      kb, vb, bias_p, bias_n, lam, gain.reshape(1, DA_VDIM))


def _post_kernel(x_ref, ohg_ref, oda_ref, wout_ref, ga1_ref, g_ref, sc_ref, sh_ref, ga2_ref,
                 wsgu_ref, wsd_ref, wrt_ref, rb_ref, tri_ref,
                 xs_ref, h2_ref, eidx_ref, slot_ref, wl_ref, cnt_ref, carry_ref, *, tm):
    i = pl.program_id(0)

    @pl.when(i == 0)
    def _():
        carry_ref[...] = jnp.zeros(carry_ref.shape, F32)

    mix = _dot(ohg_ref[...], wout_ref[0:HG_WIDTH, :]) + _dot(oda_ref[...], wout_ref[HG_WIDTH:, :])
    x1 = x_ref[...] + ga1_ref[...] * mix
    ms = jnp.mean(x1 * x1, axis=-1, keepdims=True)
    h2 = x1 * lax.rsqrt(ms + EPS) * g_ref[...]
    h2 = h2 * (1.0 + sc_ref[...]) + sh_ref[...]
    h2_ref[...] = h2
    h2b = h2.astype(BF16)
    gu = _dot(h2b, wsgu_ref[...])
    act = (_silu(gu[:, 0:D_EXPERT]) * gu[:, D_EXPERT:]).astype(BF16)
    xs_ref[...] = x1 + ga2_ref[...] * _dot(act, wsd_ref[...])

    logits = _dot_nt(wrt_ref[...], h2, precision=HIGHEST)
    score = _sigmoid(logits)
    sel = score + rb_ref[...]
    sub = lax.broadcasted_iota(I32, (GROUP_SIZE, tm), 0)
    gscore = []
    for g in range(N_GROUPS):
        v = sel[g * GROUP_SIZE:(g + 1) * GROUP_SIZE, :]
        m1 = jnp.max(v, axis=0, keepdims=True)
        i1 = jnp.min(jnp.where(v == m1, sub, GROUP_SIZE), axis=0, keepdims=True)
        m2 = jnp.max(jnp.where(sub == i1, -jnp.inf, v), axis=0, keepdims=True)
        gscore.append(m1 + m2)
    gsel = []
    for g in range(N_GROUPS):
        ahead = jnp.zeros((1, tm), F32)
        for g2 in range(N_GROUPS):
            if g2 == g:
                continue
            tie = 1.0 if g2 < g else 0.0
            ahead = ahead + jnp.where(gscore[g2] > gscore[g], 1.0,
                                      jnp.where(gscore[g2] == gscore[g], tie, 0.0))
        gsel.append(ahead < TOP_GROUPS)
    selm = jnp.concatenate(
        [jnp.where(gsel[g], sel[g * GROUP_SIZE:(g + 1) * GROUP_SIZE, :], -jnp.inf)
         for g in range(N_GROUPS)], axis=0)
    eio = lax.broadcasted_iota(I32, (N_EXPERTS, tm), 0)
    ahead = jnp.zeros((N_EXPERTS, tm), F32)
    for e2 in range(N_EXPERTS):
        row = selm[e2:e2 + 1, :]
        tie = jnp.where(eio > e2, 1.0, 0.0)
        ahead = ahead + jnp.where(row > selm, 1.0, jnp.where(row == selm, tie, 0.0))
    chosen = jnp.where(selm > -jnp.inf, jnp.where(ahead < TOP_K, 1.0, 0.0), 0.0)
    w = chosen * score
    wn = w / jnp.sum(w, axis=0, keepdims=True) * ROUTE_SCALE

    chb = chosen.astype(BF16)
    before = _dot(chb, tri_ref[...])
    carry = carry_ref[...]
    slot = jnp.concatenate([carry] * (tm // 128), axis=1) + before
    carry_new = carry + _dot(chb, jnp.ones((tm, 128), BF16))
    carry_ref[...] = carry_new
    cnt_ref[...] = carry_new

    eiof = eio.astype(F32)
    for r in range(TOP_K):
        pick = jnp.where(ahead == r, chosen, 0.0)
        eidx_ref[r:r + 1, :] = jnp.sum(pick * eiof, axis=0, keepdims=True).astype(I32)
        slot_ref[r:r + 1, :] = jnp.sum(pick * slot, axis=0, keepdims=True).astype(I32)
        wl_ref[r:r + 1, :] = jnp.sum(pick * wn, axis=0, keepdims=True)


def _post(x, ohg, oda, w_out_b, ga1, g, sc, sh, ga2, wsgu_b, wsd_b, wr_t, rb, tm):
    n, d = x.shape
    tri = jnp.asarray(np.triu(np.ones((tm, tm), np.float32), k=1), dtype=BF16)
    row = lambda i: (i, 0)
    col = lambda i: (0, i)
    full = lambda i: (0, 0)
    return pl.pallas_call(
        functools.partial(_post_kernel, tm=tm),
        grid=(n // tm,),
        in_specs=[
            pl.BlockSpec((tm, d), row),
            pl.BlockSpec((tm, HG_WIDTH), row),
            pl.BlockSpec((tm, DA_WIDTH), row),
            pl.BlockSpec(w_out_b.shape, full),
            _mod_spec(ga1, tm),
            pl.BlockSpec((1, d), full),
            _mod_spec(sc, tm),
            _mod_spec(sh, tm),
            _mod_spec(ga2, tm),
            pl.BlockSpec(wsgu_b.shape, full),
            pl.BlockSpec(wsd_b.shape, full),
            pl.BlockSpec(wr_t.shape, full),
            pl.BlockSpec((N_EXPERTS, 1), full),
            pl.BlockSpec((tm, tm), full),
        ],
        out_specs=[
            pl.BlockSpec((tm, d), row),
            pl.BlockSpec((tm, d), row),
            pl.BlockSpec((TOP_K, tm), col),
            pl.BlockSpec((TOP_K, tm), col),
            pl.BlockSpec((TOP_K, tm), col),
            pl.BlockSpec((N_EXPERTS, 128), full),
        ],
        out_shape=[
            jax.ShapeDtypeStruct((n, d), F32),
            jax.ShapeDtypeStruct((n, d), F32),
            jax.ShapeDtypeStruct((TOP_K, n), I32),
            jax.ShapeDtypeStruct((TOP_K, n), I32),
            jax.ShapeDtypeStruct((TOP_K, n), F32),
            jax.ShapeDtypeStruct((N_EXPERTS, 128), F32),
        ],
        scratch_shapes=[pltpu.VMEM((N_EXPERTS, 128), F32)],
        compiler_params=_params(("arbitrary",)),
        name="post_mix_router",
    )(x, ohg, oda, w_out_b, ga1, g, sc, sh, ga2, wsgu_b, wsd_b, wr_t, rb.reshape(N_EXPERTS, 1), tri)


def _dispatch_kernel(eidx_ref, slot_ref, pstart_ref, pend_ref, h2_hbm, xs_hbm,
                     zero_ref, zsem, sem, *, td, bm):
    i = pl.program_id(0)
    base = i * td

    def row_copy(src_row, dst_row):
        return pltpu.make_async_copy(h2_hbm.at[pl.ds(src_row, 1)], xs_hbm.at[pl.ds(dst_row, 1)], sem)

    def zero_copy(e):
        start = pl.multiple_of(pend_ref[e] - bm, bm)
        return pltpu.make_async_copy(zero_ref, xs_hbm.at[pl.ds(start, bm)], zsem)

    @pl.when(i == 0)
    def _():
        zero_ref[...] = jnp.zeros(zero_ref.shape, zero_ref.dtype)

        def zissue(e, carry):
            @pl.when(pend_ref[e] > pstart_ref[e])
            def _():
                zero_copy(e).start()
            return carry

        def zwait(e, carry):
            @pl.when(pend_ref[e] > pstart_ref[e])
            def _():
                zero_copy(e).wait()
            return carry

        lax.fori_loop(0, N_EXPERTS, zissue, 0)
        lax.fori_loop(0, N_EXPERTS, zwait, 0)

    def issue(t, carry):
        for r in range(TOP_K):
            dst = pstart_ref[eidx_ref[r, t]] + slot_ref[r, t]
            row_copy(base + t, dst).start()

        @pl.when(t >= DMA_LAG)
        def _():
            for r in range(TOP_K):
                row_copy(0, 0).wait()
        return carry

    lax.fori_loop(0, td, issue, 0)

    def drain(t, carry):
        for r in range(TOP_K):
            row_copy(0, 0).wait()
        return carry

    lax.fori_loop(0, min(DMA_LAG, td), drain, 0)


def _dispatch(eidx, slot, pstart, pend, h2, nrows, td, bm):
    n, d = h2.shape
    return pl.pallas_call(
        functools.partial(_dispatch_kernel, td=td, bm=bm),
        grid=(n // td,),
        in_specs=[
            pl.BlockSpec((TOP_K, td), lambda i: (0, i), memory_space=pltpu.SMEM),
            pl.BlockSpec((TOP_K, td), lambda i: (0, i), memory_space=pltpu.SMEM),
            pl.BlockSpec(memory_space=pltpu.SMEM),
            pl.BlockSpec(memory_space=pltpu.SMEM),
            pl.BlockSpec(memory_space=pl.ANY),
        ],
        out_specs=pl.BlockSpec(memory_space=pl.ANY),
        out_shape=jax.ShapeDtypeStruct((nrows, d), h2.dtype),
        scratch_shapes=[
            pltpu.VMEM((bm, d), h2.dtype),
            pltpu.SemaphoreType.DMA(()),
            pltpu.SemaphoreType.DMA(()),
        ],
        compiler_params=_params(("arbitrary",)),
        name="moe_dispatch",
    )(eidx, slot, pstart, pend, h2)


def _experts_kernel(be_ref, nu_ref, x_ref, wgu_ref, wd_ref, o_ref):
    i = pl.program_id(0)

    @pl.when(i < nu_ref[0])
    def _():
        x = x_ref[...].astype(BF16)
        gu = _dot(x, wgu_ref[...].astype(BF16))
        act = (_silu(gu[:, 0:D_EXPERT]) * gu[:, D_EXPERT:]).astype(BF16)
        o_ref[...] = _dot(act, wd_ref[...].astype(BF16))

    @pl.when(i >= nu_ref[0])
    def _():
        o_ref[...] = jnp.zeros(o_ref.shape, o_ref.dtype)


def _experts(block_e, nused, xs, w_gate_up_l, w_down_l, bm):
    nrows, d = xs.shape
    nblk = nrows // bm
    grid_spec = pltpu.PrefetchScalarGridSpec(
        num_scalar_prefetch=2,
        grid=(nblk,),
        in_specs=[
            pl.BlockSpec((bm, d), lambda i, be, nu: (jnp.minimum(i, nu[0] - 1), 0)),
            pl.BlockSpec((None, d, 2 * D_EXPERT), lambda i, be, nu: (be[i], 0, 0)),
            pl.BlockSpec((None, D_EXPERT, d), lambda i, be, nu: (be[i], 0, 0)),
        ],
        out_specs=pl.BlockSpec((bm, d), lambda i, be, nu: (i, 0)),
    )
    return pl.pallas_call(
        _experts_kernel,
        grid_spec=grid_spec,
        out_shape=jax.ShapeDtypeStruct((nrows, d), F32),
        compiler_params=_params(("arbitrary",)),
        name="moe_experts",
    )(block_e, nused, xs, w_gate_up_l, w_down_l)


def _combine_kernel(e0_ref, s0_ref, e1_ref, s1_ref, pstart_ref, wl_ref, xs_ref, ga2_ref, gf_ref,
                    yb_hbm, o_ref, g_ref, sem, *, tc):
    i = pl.program_id(0)
    n = pl.num_programs(0)
    cur = lax.rem(i, 2)

    def issue(e_ref, s_ref, buf):
        def body(t, carry):
            for r in range(TOP_K):
                src = pstart_ref[e_ref[r, t]] + s_ref[r, t]
                pltpu.make_async_copy(yb_hbm.at[pl.ds(src, 1)], g_ref.at[buf, r, pl.ds(t, 1)],
                                      sem.at[buf]).start()
            return carry
        lax.fori_loop(0, tc, body, 0)

    @pl.when(i == 0)
    def _():
        issue(e0_ref, s0_ref, 0)

    @pl.when(i + 1 < n)
    def _():
        issue(e1_ref, s1_ref, 1 - cur)

    pltpu.make_async_copy(g_ref.at[cur], g_ref.at[cur], sem.at[cur]).wait()

    wl = wl_ref[...]
    routed = wl[:, 0:1] * g_ref[cur, 0]
    for r in range(1, TOP_K):
        routed = routed + wl[:, r:r + 1] * g_ref[cur, r]
    x2 = xs_ref[...] + ga2_ref[...] * routed
    ms = jnp.mean(x2 * x2, axis=-1, keepdims=True)
    o_ref[...] = x2 * lax.rsqrt(ms + EPS) * gf_ref[...]


def _combine(eidx, slot, pstart, wl_t, xs_base, ga2, gfin, yb, tc):
    n, d = xs_base.shape
    nt = n // tc
    smem_cur = pl.BlockSpec((TOP_K, tc), lambda i: (0, i), memory_space=pltpu.SMEM)
    smem_nxt = pl.BlockSpec((TOP_K, tc), lambda i: (0, jnp.minimum(i + 1, nt - 1)),
                            memory_space=pltpu.SMEM)
    return pl.pallas_call(
        functools.partial(_combine_kernel, tc=tc),
        grid=(nt,),
        in_specs=[
            smem_cur, smem_cur, smem_nxt, smem_nxt,
            pl.BlockSpec(memory_space=pltpu.SMEM),
            pl.BlockSpec((tc, TOP_K), lambda i: (i, 0)),
            pl.BlockSpec((tc, d), lambda i: (i, 0)),
            _mod_spec(ga2, tc),
            pl.BlockSpec((1, d), lambda i: (0, 0)),
            pl.BlockSpec(memory_space=pl.ANY),
        ],
        out_specs=pl.BlockSpec((tc, d), lambda i: (i, 0)),
        out_shape=jax.ShapeDtypeStruct((n, d), F32),
        scratch_shapes=[
            pltpu.VMEM((2, TOP_K, tc, d), F32),
            pltpu.SemaphoreType.DMA((2,)),
        ],
        compiler_params=_params(("arbitrary",)),
        name="moe_combine",
    )(eidx, slot, eidx, slot, pstart, wl_t, xs_base, ga2, gfin, yb)


def _moe_and_final(x, ohg, oda, mods, wts, tm, td, tc, bm):
    n, d = x.shape
    ga1, sh2, sc2, ga2 = mods
    (w_out_b, g_ffn, wsgu_b, wsd_b, wr_t, rb, w_gate_up_l, w_down_l, g_final) = wts
    xs_base, h2, eidx, slot, wl, cnt = _post(
        x, ohg, oda, w_out_b, ga1, g_ffn, sc2, sh2, ga2, wsgu_b, wsd_b, wr_t, rb, tm)
    counts = cnt[:, 0].astype(I32)
    padded = (counts + bm - 1) // bm * bm
    pend = jnp.cumsum(padded)
    pstart = pend - padded
    nblk = -(-(n * TOP_K) // bm) + N_EXPERTS
    nused = (pend[-1] // bm).astype(I32)
    be = jnp.minimum(jnp.searchsorted(pend, jnp.arange(nblk, dtype=I32) * bm, side="right"),
                     N_EXPERTS - 1).astype(I32)
    be = jnp.where(jnp.arange(nblk) < nused, be, be[jnp.maximum(nused - 1, 0)])
    xs = _dispatch(eidx, slot, pstart, pend, h2, nblk * bm, td, bm)
    yb = _experts(be, nused.reshape(1), xs, w_gate_up_l, w_down_l, bm)
    return _combine(eidx, slot, pstart, wl.T, xs_base, ga2, g_final, yb, tc)


def _expand(mod, reps):
    if mod.shape[0] == 1:
        return mod
    return jnp.repeat(mod, reps, axis=0)


def kernel(x_prompt, x_sample, cache_k, cache_v, state_hgrn, c_prompt, c_sample, w_ada, b_ada,
           norm_mix, norm_ffn, norm_final, w_in, w_out, hg_lb_logits, hg_norm, da_lambda, da_norm,
           rel_bias_table, w_router, router_bias, w_gate_up, w_down, ws_gate_up, ws_down):
    depth = w_in.shape[0]
    assert depth == 1 and hg_lb_logits.shape[0] == 2
    bp, tp, d = x_prompt.shape
    bs, ts, _ = x_sample.shape
    assert bp == 1
    past = cache_k.shape[2]
    l = 0

    rows = -(-(bp + bs) // 8) * 8
    c_all = jnp.zeros((rows, d), F32).at[:bp].set(c_prompt).at[bp:bp + bs].set(c_sample)
    mod = _adaln(c_all, w_ada[l], b_ada[l])
    mod_p = [mod[0:bp, j * d:(j + 1) * d] for j in range(6)]
    mod_s = [_expand(mod[bp:bp + bs, j * d:(j + 1) * d], ts) for j in range(6)]

    w_in_b = w_in[l].astype(BF16)
    w_out_b = w_out[l].astype(BF16)
    wsgu_b = ws_gate_up[l].astype(BF16)
    wsd_b = ws_down[l].astype(BF16)
    wr_t = w_router[l].T
    g_mix = norm_mix[l].reshape(1, d)
    g_ffn = norm_ffn[l].reshape(1, d)
    g_final = norm_final.reshape(1, d)
    moe_w = (w_out_b, g_ffn, wsgu_b, wsd_b, wr_t, router_bias[l], w_gate_up[l], w_down[l], g_final)

    lam = _lam(da_lambda[l])

    t_att = min(ATT_TILE, tp)
    kk = jnp.arange(t_att, dtype=I32)[:, None]
    qq = jnp.arange(t_att, dtype=I32)[None, :]
    idx_diag = jnp.where((kk // CHUNK) <= (qq // CHUNK), _rel_bucket(kk - qq), MASK_BUCKET)
    idx_prev = _rel_bucket(kk - qq - t_att)
    bias_p = _bias_tiles(rel_bias_table, jnp.stack([idx_diag, idx_prev]).astype(I32),
                         shift_bucket=N_BUCKETS // 2 - 1, dup_lanes=True)
    pad = 128
    qpos = past + jnp.arange(ts, dtype=I32)[:, None]
    idx_sp = _rel_bucket(jnp.arange(past, dtype=I32)[None, :] - qpos)
    kn = jnp.arange(pad, dtype=I32)[None, :]
    idx_sn = jnp.where(kn < ts, _rel_bucket(past + kn - qpos), MASK_BUCKET)
    bias_sp = _bias_tiles(rel_bias_table, idx_sp[None].astype(I32), shift_bucket=None, dup_lanes=False)
    bias_sn = _bias_tiles(rel_bias_table, idx_sn[None].astype(I32), shift_bucket=None, dup_lanes=False)

    xp = x_prompt.reshape(bp * tp, d)
    sh1, sc1, ga1, sh2, sc2, ga2 = mod_p
    zh, qs, kf, vf, kb, vb = _inproj(xp, g_mix, sc1, sh1, w_in_b, min(INPROJ_TILE, tp))
    s_zero = jnp.zeros((bp, HG_HEADS, HG_DIM, HG_DIM), F32)
    ohg_p, sp_new = _hgrn(zh, s_zero, hg_lb_logits, hg_norm[l], bp, tp, min(HGRN_CHUNK, tp))
    oda_p = _attn_prompt(kb, qs.T, vb.T, bias_p, lam, da_norm[l], t_att)
    y_p = _moe_and_final(xp, ohg_p, oda_p, (ga1, sh2, sc2, ga2), moe_w,
                         min(POST_TILE, tp), min(DISPATCH_TILE, tp), min(COMBINE_TILE, tp),
                         MOE_BLOCK_ROWS)
    k_prompt = kf.reshape(1, bp, tp, DA_HEADS, 2 * DA_QKDIM)
    v_prompt = vf.reshape(1, bp, tp, DA_HEADS, DA_VDIM)

    ns = bs * ts
    xs_ = x_sample.reshape(ns, d)
    sh1, sc1, ga1, sh2, sc2, ga2 = mod_s
    zh, qs, kf, vf, kb, vb = _inproj(xs_, g_mix, sc1, sh1, w_in_b, ns)
    ohg_s, ss_new = _hgrn(zh, state_hgrn[l], hg_lb_logits, hg_norm[l], bs, ts, ts)
    oda_s = _attn_step(qs, cache_k[l], cache_v[l], kb, vb, bias_sp, bias_sn, lam, da_norm[l], bs, ts)
    y_s = _moe_and_final(xs_, ohg_s, oda_s, (ga1, sh2, sc2, ga2), moe_w,
                         ns, ns, min(COMBINE_TILE, ns), 128)
    k_sample = kf.reshape(1, bs, ts, DA_HEADS, 2 * DA_QKDIM)
    v_sample = vf.reshape(1, bs, ts, DA_HEADS, DA_VDIM)

    return (y_p.reshape(bp, tp, d), y_s.reshape(bs, ts, d), k_prompt, v_prompt, sp_new[None],
            k_sample, v_sample, ss_new[None].astype(x_sample.dtype))
```

```python
import functools
import math

import numpy as np
import jax
import jax.numpy as jnp
from jax import lax
from jax.experimental import pallas as pl
from jax.experimental.pallas import tpu as pltpu

F32 = jnp.float32
BF16 = jnp.bfloat16
I32 = jnp.int32
U32 = jnp.uint32
HIGHEST = lax.Precision.HIGHEST

D_MODEL = 1024
CHUNK = 64
HG_HEADS = 4
HG_DIM = 128
HG_WIDTH = HG_HEADS * HG_DIM
DA_HEADS = 4
DA_VDIM = 128
DA_QKDIM = 64
DA_WIDTH = DA_HEADS * DA_VDIM
N_BUCKETS = 32
MAX_DIST = 128
N_EXPERTS = 64
TOP_K = 8
N_GROUPS = 8
GROUP_SIZE = N_EXPERTS // N_GROUPS
TOP_GROUPS = 4
D_EXPERT = 256
ROUTE_SCALE = 2.5
EPS = 1e-6
LAM_INIT = 0.8 - 0.6 * math.exp(-0.3 * 0)

LOG2E = math.log2(math.e)
HI_MASK = np.uint32(0xFFFF0000)
NEG_BIG = -1e30
MASK_BUCKET = N_BUCKETS
V7X_VMEM_LIMIT = 48 * 1024 * 1024

ATT_TILE = 512
HGRN_CHUNK = 256
INPROJ_TILE = 512
POST_TILE = 256
DISPATCH_TILE = 512
COMBINE_TILE = 256
MOE_BLOCK_ROWS = 512


def _sigmoid(x):
    return 1.0 / (1.0 + jnp.exp(-x))


def _silu(x):
    return x * _sigmoid(x)


def _dot(a, b, **kw):
    return jnp.dot(a, b, preferred_element_type=F32, **kw)


def _dot_nt(a, b, **kw):
    return lax.dot_general(a, b, (((1,), (1,)), ((), ())), preferred_element_type=F32, **kw)


def _dot_tn(a, b, **kw):
    return lax.dot_general(a, b, (((0,), (0,)), ((), ())), preferred_element_type=F32, **kw)


def _pack_bf16_pair(lo, hi):
    lo_bits = lax.bitcast_convert_type(lo.astype(BF16).astype(F32), U32)
    hi_bits = lax.bitcast_convert_type(hi.astype(BF16).astype(F32), U32)
    return (lo_bits >> 16) | (hi_bits & HI_MASK)


def _unpack_bf16_pair(w):
    lo = lax.bitcast_convert_type(w << 16, F32)
    hi = lax.bitcast_convert_type(w & HI_MASK, F32)
    return lo, hi


def _params(sem, vmem=V7X_VMEM_LIMIT, flags=None):
    return pltpu.CompilerParams(dimension_semantics=sem, vmem_limit_bytes=vmem, flags=flags)


def _adaln_kernel(c_ref, w_ref, b_ref, o_ref):
    s = _silu(c_ref[...])
    o_ref[...] = _dot(s, w_ref[...], precision=HIGHEST) + b_ref[...]


def _adaln(c_all, w_ada, b_ada):
    rows, d = c_all.shape
    cols = w_ada.shape[1]
    blk = 1024
    return pl.pallas_call(
        _adaln_kernel,
        grid=(cols // blk,),
        in_specs=[
            pl.BlockSpec((rows, d), lambda j: (0, 0)),
            pl.BlockSpec((d, blk), lambda j: (0, j)),
            pl.BlockSpec((1, blk), lambda j: (0, j)),
        ],
        out_specs=pl.BlockSpec((rows, blk), lambda j: (0, j)),
        out_shape=jax.ShapeDtypeStruct((rows, cols), F32),
        compiler_params=_params(("parallel",)),
        name="adaln",
    )(c_all, w_ada, b_ada.reshape(1, cols))


def _lam_kernel(l_ref, o_ref):
    l = l_ref[...].astype(F32)
    a = jnp.sum(l[0:1] * l[1:2], axis=-1, keepdims=True)
    b = jnp.sum(l[2:3] * l[3:4], axis=-1, keepdims=True)
    lam = jnp.exp(a) - jnp.exp(b) + LAM_INIT
    o_ref[...] = jnp.broadcast_to(lam, o_ref.shape)


def _lam(da_lambda_l):
    return pl.pallas_call(
        _lam_kernel,
        out_shape=jax.ShapeDtypeStruct((8, 128), F32),
        name="lam",
    )(da_lambda_l)


def _rel_bucket(rel):
    nb = N_BUCKETS // 2
    max_exact = nb // 2
    side = jnp.where(rel > 0, nb, 0)
    n = jnp.abs(rel)
    large = max_exact + (jnp.log(jnp.maximum(n, 1).astype(F32) / max_exact)
                         / math.log(MAX_DIST / max_exact) * (nb - max_exact)).astype(I32)
    large = jnp.minimum(large, nb - 1)
    return side + jnp.where(n < max_exact, n, large)


def _bias_kernel(tab_ref, idx_ref, o_ref, *, shift_bucket):
    h = pl.program_id(0)
    idx = idx_ref[...]
    shift = tab_ref[shift_bucket, h] if shift_bucket is not None else 0.0
    acc = jnp.zeros(idx.shape, F32)
    for j in range(N_BUCKETS):
        acc = jnp.where(idx == j, (tab_ref[j, h] - shift) * LOG2E, acc)
    o_ref[...] = jnp.where(idx == MASK_BUCKET, NEG_BIG, acc)


def _bias_tiles(table, idx, *, shift_bucket):
    k, r, c = idx.shape
    return pl.pallas_call(
        functools.partial(_bias_kernel, shift_bucket=shift_bucket),
        grid=(DA_HEADS, k),
        in_specs=[
            pl.BlockSpec(memory_space=pltpu.SMEM),
            pl.BlockSpec((None, r, c), lambda h, d: (d, 0, 0)),
        ],
        out_specs=pl.BlockSpec((None, None, r, c), lambda h, d: (h, d, 0, 0)),
        out_shape=jax.ShapeDtypeStruct((DA_HEADS, k, r, c), F32),
        compiler_params=_params(("parallel", "parallel")),
        name="rel_bias",
    )(table, idx)


def _inproj_kernel(x_ref, g_ref, sc_ref, sh_ref, w_ref,
                   zh_ref, q_ref, k_ref, v_ref, kb_ref, vb_ref):
    x = x_ref[...]
    ms = jnp.mean(x * x, axis=-1, keepdims=True)
    h = x * lax.rsqrt(ms + EPS) * g_ref[...]
    h = h * (1.0 + sc_ref[...]) + sh_ref[...]
    hb = h.astype(BF16)
    c0 = 4 * HG_WIDTH
    zh_ref[...] = _dot(hb, w_ref[:, 0:c0])
    zq = _dot(hb, w_ref[:, c0:c0 + DA_WIDTH])
    q_ref[...] = (zq * (DA_QKDIM ** -0.5 * LOG2E)).astype(BF16)
    zk = _dot(hb, w_ref[:, c0 + DA_WIDTH:c0 + 2 * DA_WIDTH])
    k_ref[...] = zk
    kb_ref[...] = zk.astype(BF16)
    zv = _dot(hb, w_ref[:, c0 + 2 * DA_WIDTH:c0 + 3 * DA_WIDTH])
    v_ref[...] = zv
    vb_ref[...] = zv.astype(BF16)


def _mod_spec(mod, tm):
    if mod.shape[0] == 1:
        return pl.BlockSpec((1, mod.shape[1]), lambda i: (0, 0))
    return pl.BlockSpec((tm, mod.shape[1]), lambda i: (i, 0))


def _inproj(x, g, sc, sh, w_in_b, tm):
    n, d = x.shape
    cols = w_in_b.shape[1]
    row = lambda i: (i, 0)
    return pl.pallas_call(
        _inproj_kernel,
        grid=(n // tm,),
        in_specs=[
            pl.BlockSpec((tm, d), row),
            pl.BlockSpec((1, d), lambda i: (0, 0)),
            _mod_spec(sc, tm),
            _mod_spec(sh, tm),
            pl.BlockSpec((d, cols), lambda i: (0, 0)),
        ],
        out_specs=[
            pl.BlockSpec((tm, 4 * HG_WIDTH), row),
            pl.BlockSpec((tm, DA_WIDTH), row),
            pl.BlockSpec((tm, DA_WIDTH), row),
            pl.BlockSpec((tm, DA_WIDTH), row),
            pl.BlockSpec((tm, DA_WIDTH), row),
            pl.BlockSpec((tm, DA_WIDTH), row),
        ],
        out_shape=[
            jax.ShapeDtypeStruct((n, 4 * HG_WIDTH), F32),
            jax.ShapeDtypeStruct((n, DA_WIDTH), BF16),
            jax.ShapeDtypeStruct((n, DA_WIDTH), F32),
            jax.ShapeDtypeStruct((n, DA_WIDTH), F32),
            jax.ShapeDtypeStruct((n, DA_WIDTH), BF16),
            jax.ShapeDtypeStruct((n, DA_WIDTH), BF16),
        ],
        compiler_params=_params(("parallel",)),
        name="inproj",
    )(x, g, sc, sh, w_in_b)


def _hgrn_consts(c):
    levels = int(round(math.log2(c)))
    assert 1 << levels == c
    t = np.arange(c)[:, None]
    r = np.arange(c)[None, :]
    blocks = [r <= t]
    for l in range(levels):
        m = 1 << l
        mid = (t // (2 * m)) * (2 * m) + m - 1
        later = (t & m) != 0
        blocks.append(np.where(later, (r > mid) & (r <= t), (r > t) & (r <= mid)))
    mall = np.concatenate(blocks, axis=0).astype(np.float32)
    x = np.maximum(t ^ r, 1)
    lv = np.where(t == r, -1, np.where(t > r, np.floor(np.log2(x)).astype(np.int64), -2))
    return jnp.asarray(mall, dtype=BF16), jnp.asarray(lv, dtype=I32), levels


def _hgrn_kernel(zh_ref, s0_ref, lbl_ref, gain_ref, mall_ref, lv_ref,
                 o_ref, sout_ref, st_ref, *, c, levels):
    ci = pl.program_id(1)

    @pl.when(ci == 0)
    def _():
        for h in range(HG_HEADS):
            st_ref[h] = s0_ref[h].astype(F32).T

    lbl = lbl_ref[...].astype(F32)
    mx = jnp.maximum(lbl[0:1], lbl[1:2])
    e0 = jnp.exp(lbl[0:1] - mx)
    e1 = jnp.exp(lbl[1:2] - mx)
    lb = e0 / (e0 + e1)

    xq = zh_ref[:, 0:HG_WIDTH]
    xf = zh_ref[:, HG_WIDTH:2 * HG_WIDTH]
    q = _silu(xq)
    y = lb + (1.0 - lb) * _sigmoid(xf)
    logf = jnp.log(y)
    kk = 1.0 - y

    l1 = logf.astype(BF16)
    r1 = logf - l1.astype(F32)
    l2 = r1.astype(BF16)
    l3 = (r1 - l2.astype(F32)).astype(BF16)
    mall = mall_ref[...]
    e_all = _dot(mall, l1) + _dot(mall, l2) + _dot(mall, l3)

    lv = lv_ref[...]
    gain = gain_ref[...].astype(F32)
    for h in range(HG_HEADS):
        sl = slice(h * HG_DIM, (h + 1) * HG_DIM)
        qh = q[:, sl]
        kh = kk[:, sl]
        ih = zh_ref[:, 2 * HG_WIDTH + h * HG_DIM:2 * HG_WIDTH + (h + 1) * HG_DIM]
        gh = zh_ref[:, 3 * HG_WIDTH + h * HG_DIM:3 * HG_WIDTH + (h + 1) * HG_DIM]
        bh = e_all[0:c, sl]
        ihb = ih.astype(BF16)
        a = jnp.where(lv == -1, _dot_nt(qh.astype(BF16), kh.astype(BF16)), 0.0)
        for l in range(levels):
            f = jnp.exp(e_all[(l + 1) * c:(l + 2) * c, sl])
            p = _dot_nt((qh * f).astype(BF16), (kh * f).astype(BF16))
            a = a + jnp.where(lv == l, p, 0.0)
        st = st_ref[h]
        o = _dot(a.astype(BF16), ihb) + _dot_nt((qh * jnp.exp(bh)).astype(BF16), st.astype(BF16))
        bl = bh[c - 1:c, :]
        kd = (kh * jnp.exp(bl - bh)).astype(BF16)
        st_ref[h] = st * jnp.exp(bl) + _dot_tn(ihb, kd)
        ms = jnp.mean(o * o, axis=-1, keepdims=True)
        on = o * lax.rsqrt(ms + EPS) * gain
        o_ref[:, sl] = (on * _silu(gh)).astype(o_ref.dtype)

    @pl.when(ci == pl.num_programs(1) - 1)
    def _():
        for h in range(HG_HEADS):
            sout_ref[h] = st_ref[h].T.astype(sout_ref.dtype)


def _hgrn(zh, s0, lb_logits, gain, batch, seq, c):
    mall, lv, levels = _hgrn_consts(c)
    nc = seq // c
    return pl.pallas_call(
        functools.partial(_hgrn_kernel, c=c, levels=levels),
        grid=(batch, nc),
        in_specs=[
            pl.BlockSpec((c, 4 * HG_WIDTH), lambda b, i: (b * nc + i, 0)),
            pl.BlockSpec((None, HG_HEADS, HG_DIM, HG_DIM), lambda b, i: (b, 0, 0, 0)),
            pl.BlockSpec(lb_logits.shape, lambda b, i: (0, 0)),
            pl.BlockSpec((1, HG_DIM), lambda b, i: (0, 0)),
            pl.BlockSpec(mall.shape, lambda b, i: (0, 0)),
            pl.BlockSpec(lv.shape, lambda b, i: (0, 0)),
        ],
        out_specs=[
            pl.BlockSpec((c, HG_WIDTH), lambda b, i: (b * nc + i, 0)),
            pl.BlockSpec((None, HG_HEADS, HG_DIM, HG_DIM), lambda b, i: (b, 0, 0, 0)),
        ],
        out_shape=[
            jax.ShapeDtypeStruct((batch * seq, HG_WIDTH), BF16),
            jax.ShapeDtypeStruct((batch, HG_HEADS, HG_DIM, HG_DIM), F32),
        ],
        scratch_shapes=[pltpu.VMEM((HG_HEADS, HG_DIM, HG_DIM), F32)],
        compiler_params=_params(("parallel", "arbitrary")),
        name="hgrn2",
    )(zh, s0, lb_logits, gain.reshape(1, HG_DIM), mall, lv)


def _attn_kernel(k_ref, qt_ref, vt_ref, bias_ref, lam_ref, gain_ref,
                 o_ref, qz_ref, m_ref, l_ref, acc_ref, s_ref, *, t):
    i = pl.program_id(1)
    qt = qt_ref[...]
    row = lax.broadcasted_iota(I32, qt.shape, 0)
    zero = jnp.zeros_like(qt)
    qz_ref[:, 0:t] = jnp.where(row < DA_QKDIM, qt, zero)
    qz_ref[:, t:2 * t] = jnp.where(row >= DA_QKDIM, qt, zero)
    m_ref[...] = jnp.full(m_ref.shape, NEG_BIG, F32)
    l_ref[...] = jnp.zeros(l_ref.shape, F32)
    acc_ref[...] = jnp.zeros(acc_ref.shape, F32)

    def scores(j, buf):
        kt = k_ref[pl.ds(pl.multiple_of(j * t, t), t), :]
        s_ref[buf] = _dot(kt, qz_ref[...])

    def consume(j, buf, bias_idx):
        s = s_ref[buf]
        if bias_idx is not None:
            b = bias_ref[bias_idx]
            s = jnp.concatenate([s[:, 0:t] + b, s[:, t:2 * t] + b], axis=1)
        m_prev = m_ref[...]
        m_new = jnp.maximum(m_prev, jnp.max(s, axis=0, keepdims=True))
        alpha = jnp.exp2(m_prev - m_new)
        pr = jnp.exp2(s - m_new)
        l_ref[...] = alpha * l_ref[...] + jnp.sum(pr, axis=0, keepdims=True)
        acc_ref[...] = alpha * acc_ref[...] + _dot(vt_ref[j], pr.astype(BF16))
        m_ref[...] = m_new

    n_far = jnp.maximum(i - 1, 0)

    @pl.when(n_far > 0)
    def _():
        scores(0, 0)

    def far_pair(p, carry):
        j = 2 * p
        scores(j + 1, 1)
        consume(j, 0, None)
        scores(jnp.minimum(j + 2, n_far - 1), 0)
        consume(j + 1, 1, None)
        return carry

    lax.fori_loop(0, n_far // 2, far_pair, 0)

    @pl.when(lax.rem(n_far, 2) == 1)
    def _():
        consume(n_far - 1, 0, None)

    @pl.when(i >= 1)
    def _():
        scores(i - 1, 0)
        consume(i - 1, 0, 1)

    scores(i, 1)
    consume(i, 1, 0)
    lam = lam_ref[0:1, 0:1]
    l = l_ref[...]
    acc = acc_ref[...]
    o = acc[:, 0:t] / l[:, 0:t] - lam * (acc[:, t:2 * t] / l[:, t:2 * t])
    ms = jnp.mean(o * o, axis=0, keepdims=True)
    on = o * lax.rsqrt(ms + EPS) * gain_ref[...].astype(F32) * (1.0 - LAM_INIT)
    o_ref[...] = on.T.astype(o_ref.dtype)


def _attn_prompt(kb, qt, vt, bias, lam, gain, t):
    n = kb.shape[0]
    nt = n // t
    return pl.pallas_call(
        functools.partial(_attn_kernel, t=t),
        grid=(DA_HEADS, nt),
        in_specs=[
            pl.BlockSpec((n, DA_VDIM), lambda h, i: (0, h)),
            pl.BlockSpec((DA_VDIM, t), lambda h, i: (h, i)),
            pl.BlockSpec((None, nt, DA_VDIM, t), lambda h, i: (h, 0, 0, 0)),
            pl.BlockSpec((None, 2, t, t), lambda h, i: (h, 0, 0, 0)),
            pl.BlockSpec((8, 128), lambda h, i: (0, 0)),
            pl.BlockSpec((DA_VDIM, 1), lambda h, i: (0, 0)),
        ],
        out_specs=pl.BlockSpec((t, DA_VDIM), lambda h, i: (i, h)),
        out_shape=jax.ShapeDtypeStruct((n, DA_WIDTH), BF16),
        scratch_shapes=[
            pltpu.VMEM((DA_VDIM, 2 * t), BF16),
            pltpu.VMEM((1, 2 * t), F32),
            pltpu.VMEM((1, 2 * t), F32),
            pltpu.VMEM((DA_VDIM, 2 * t), F32),
            pltpu.VMEM((2, t, 2 * t), F32),
        ],
        compiler_params=_params(("parallel", "parallel")),
        name="diff_attn_prompt",
    )(kb, qt, vt, bias, lam, gain.reshape(DA_VDIM, 1))


def _attn_step_kernel(q_ref, kp_ref, vp_ref, kn_ref, vn_ref, bp_ref, bn_ref, lam_ref, gain_ref,
                      o_ref, *, tq, pad):
    q = q_ref[...]
    lane = lax.broadcasted_iota(I32, q.shape, 1)
    zero = jnp.zeros_like(q)
    qz = jnp.concatenate([jnp.where(lane < DA_QKDIM, q, zero),
                          jnp.where(lane >= DA_QKDIM, q, zero)], axis=0)
    kp = kp_ref[...].astype(BF16)
    vp = vp_ref[...].astype(BF16)
    zpad = jnp.zeros((pad - tq, DA_VDIM), BF16)
    kn = jnp.concatenate([kn_ref[...], zpad], axis=0)
    vn = jnp.concatenate([vn_ref[...], zpad], axis=0)
    bp = bp_ref[...]
    bn = bn_ref[...]
    sp = _dot_nt(qz, kp) + jnp.concatenate([bp, bp], axis=0)
    sn = _dot_nt(qz, kn) + jnp.concatenate([bn, bn], axis=0)
    m = jnp.maximum(jnp.max(sp, axis=-1, keepdims=True), jnp.max(sn, axis=-1, keepdims=True))
    pp = jnp.exp2(sp - m)
    pn = jnp.exp2(sn - m)
    l = jnp.sum(pp, axis=-1, keepdims=True) + jnp.sum(pn, axis=-1, keepdims=True)
    acc = _dot(pp.astype(BF16), vp) + _dot(pn.astype(BF16), vn)
    on = acc / l
    lam = lam_ref[0:1, 0:1]
    o = on[0:tq] - lam * on[tq:2 * tq]
    ms = jnp.mean(o * o, axis=-1, keepdims=True)
    o = o * lax.rsqrt(ms + EPS) * gain_ref[...].astype(F32) * (1.0 - LAM_INIT)
    o_ref[...] = o.astype(o_ref.dtype)


def _attn_step(qs, cache_k_l, cache_v_l, kb, vb, bias_p, bias_n, lam, gain, batch, tq):
    past = cache_k_l.shape[1]
    pad = bias_n.shape[-1]
    return pl.pallas_call(
        functools.partial(_attn_step_kernel, tq=tq, pad=pad),
        grid=(batch, DA_HEADS),
        in_specs=[
            pl.BlockSpec((tq, DA_VDIM), lambda b, h: (b, h)),
            pl.BlockSpec((None, past, DA_VDIM), lambda b, h: (b, 0, h)),
            pl.BlockSpec((None, past, DA_VDIM), lambda b, h: (b, 0, h)),
            pl.BlockSpec((tq, DA_VDIM), lambda b, h: (b, h)),
            pl.BlockSpec((tq, DA_VDIM), lambda b, h: (b, h)),
            pl.BlockSpec((None, None, tq, past), lambda b, h: (h, 0, 0, 0)),
            pl.BlockSpec((None, None, tq, pad), lambda b, h: (h, 0, 0, 0)),
            pl.BlockSpec((8, 128), lambda b, h: (0, 0)),
            pl.BlockSpec((1, DA_VDIM), lambda b, h: (0, 0)),
        ],
        out_specs=pl.BlockSpec((tq, DA_VDIM), lambda b, h: (b, h)),
        out_shape=jax.ShapeDtypeStruct((batch * tq, DA_WIDTH), BF16),
        compiler_params=_params(("parallel", "parallel")),
        name="diff_attn_step",
    )(qs, cache_k_l.reshape(batch, past, DA_WIDTH), cache_v_l.reshape(batch, past, DA_WIDTH),
      kb, vb, bias_p, bias_n, lam, gain.reshape(1, DA_VDIM))


def _post_kernel(x_ref, ohg_ref, oda_ref, wout_ref, ga1_ref, g_ref, sc_ref, sh_ref, ga2_ref,
                 wsgu_ref, wsd_ref, wrt_ref, rb_ref, tri_ref,
                 xs_ref, h2_ref, eidx_ref, slot_ref, wl_ref, cnt_ref, carry_ref, *, tm):
    i = pl.program_id(0)

    @pl.when(i == 0)
    def _():
        carry_ref[...] = jnp.zeros(carry_ref.shape, F32)

    mix = _dot(ohg_ref[...], wout_ref[0:HG_WIDTH, :]) + _dot(oda_ref[...], wout_ref[HG_WIDTH:, :])
    x1 = x_ref[...] + ga1_ref[...] * mix
    ms = jnp.mean(x1 * x1, axis=-1, keepdims=True)
    h2 = x1 * lax.rsqrt(ms + EPS) * g_ref[...]
    h2 = h2 * (1.0 + sc_ref[...]) + sh_ref[...]
    h2_ref[...] = _pack_bf16_pair(h2[:, 0:D_MODEL // 2], h2[:, D_MODEL // 2:])
    h2b = h2.astype(BF16)
    gu = _dot(h2b, wsgu_ref[...])
    act = (_silu(gu[:, 0:D_EXPERT]) * gu[:, D_EXPERT:]).astype(BF16)
    xs_ref[...] = x1 + ga2_ref[...] * _dot(act, wsd_ref[...])

    logits = _dot_nt(wrt_ref[...], h2, precision=HIGHEST)
    score = _sigmoid(logits)
    sel = score + rb_ref[...]
    sub = lax.broadcasted_iota(I32, (GROUP_SIZE, tm), 0)
    gscore = []
    for g in range(N_GROUPS):
        v = sel[g * GROUP_SIZE:(g + 1) * GROUP_SIZE, :]
        m1 = jnp.max(v, axis=0, keepdims=True)
        i1 = jnp.min(jnp.where(v == m1, sub, GROUP_SIZE), axis=0, keepdims=True)
        m2 = jnp.max(jnp.where(sub == i1, -jnp.inf, v), axis=0, keepdims=True)
        gscore.append(m1 + m2)
    gsel = []
    for g in range(N_GROUPS):
        ahead = jnp.zeros((1, tm), F32)
        for g2 in range(N_GROUPS):
            if g2 == g:
                continue
            tie = 1.0 if g2 < g else 0.0
            ahead = ahead + jnp.where(gscore[g2] > gscore[g], 1.0,
                                      jnp.where(gscore[g2] == gscore[g], tie, 0.0))
        gsel.append(ahead < TOP_GROUPS)
    selm = jnp.concatenate(
        [jnp.where(gsel[g], sel[g * GROUP_SIZE:(g + 1) * GROUP_SIZE, :], -jnp.inf)
         for g in range(N_GROUPS)], axis=0)
    eio = lax.broadcasted_iota(I32, (N_EXPERTS, tm), 0)
    ahead = jnp.zeros((N_EXPERTS, tm), F32)
    for e2 in range(N_EXPERTS):
        row = selm[e2:e2 + 1, :]
        tie = jnp.where(eio > e2, 1.0, 0.0)
        ahead = ahead + jnp.where(row > selm, 1.0, jnp.where(row == selm, tie, 0.0))
    chosen = jnp.where(selm > -jnp.inf, jnp.where(ahead < TOP_K, 1.0, 0.0), 0.0)
    w = chosen * score
    wn = w / jnp.sum(w, axis=0, keepdims=True) * ROUTE_SCALE

    chb = chosen.astype(BF16)
    before = _dot(chb, tri_ref[...])
    carry = carry_ref[...]
    slot = jnp.concatenate([carry] * (tm // 128), axis=1) + before
    carry_new = carry + _dot(chb, jnp.ones((tm, 128), BF16))
    carry_ref[...] = carry_new
    cnt_ref[...] = carry_new

    eiof = eio.astype(F32)
    for r in range(TOP_K):
        pick = jnp.where(ahead == r, chosen, 0.0)
        eidx_ref[r:r + 1, :] = jnp.sum(pick * eiof, axis=0, keepdims=True).astype(I32)
        slot_ref[r:r + 1, :] = jnp.sum(pick * slot, axis=0, keepdims=True).astype(I32)
        wl_ref[r:r + 1, :] = jnp.sum(pick * wn, axis=0, keepdims=True)


def _post(x, ohg, oda, w_out_b, ga1, g, sc, sh, ga2, wsgu_b, wsd_b, wr_t, rb, tm):
    n, d = x.shape
    tri = jnp.asarray(np.triu(np.ones((tm, tm), np.float32), k=1), dtype=BF16)
    row = lambda i: (i, 0)
    col = lambda i: (0, i)
    full = lambda i: (0, 0)
    return pl.pallas_call(
        functools.partial(_post_kernel, tm=tm),
        grid=(n // tm,),
        in_specs=[
            pl.BlockSpec((tm, d), row),
            pl.BlockSpec((tm, HG_WIDTH), row),
            pl.BlockSpec((tm, DA_WIDTH), row),
            pl.BlockSpec(w_out_b.shape, full),
            _mod_spec(ga1, tm),
            pl.BlockSpec((1, d), full),
            _mod_spec(sc, tm),
            _mod_spec(sh, tm),
            _mod_spec(ga2, tm),
            pl.BlockSpec(wsgu_b.shape, full),
            pl.BlockSpec(wsd_b.shape, full),
            pl.BlockSpec(wr_t.shape, full),
            pl.BlockSpec((N_EXPERTS, 1), full),
            pl.BlockSpec((tm, tm), full),
        ],
        out_specs=[
            pl.BlockSpec((tm, d), row),
            pl.BlockSpec((tm, d // 2), row),
            pl.BlockSpec((TOP_K, tm), col),
            pl.BlockSpec((TOP_K, tm), col),
            pl.BlockSpec((TOP_K, tm), col),
            pl.BlockSpec((N_EXPERTS, 128), full),
        ],
        out_shape=[
            jax.ShapeDtypeStruct((n, d), F32),
            jax.ShapeDtypeStruct((n, d // 2), U32),
            jax.ShapeDtypeStruct((TOP_K, n), I32),
            jax.ShapeDtypeStruct((TOP_K, n), I32),
            jax.ShapeDtypeStruct((TOP_K, n), F32),
            jax.ShapeDtypeStruct((N_EXPERTS, 128), F32),
        ],
        scratch_shapes=[pltpu.VMEM((N_EXPERTS, 128), F32)],
        compiler_params=_params(("arbitrary",)),
        name="post_mix_router",
    )(x, ohg, oda, w_out_b, ga1, g, sc, sh, ga2, wsgu_b, wsd_b, wr_t, rb.reshape(N_EXPERTS, 1), tri)


def _dispatch_kernel(eidx_ref, slot_ref, pstart_ref, pend_ref, h2_ref, xs_hbm,
                     zero_ref, zsem, sem, *, td, bm):
    i = pl.program_id(0)
    nblk = xs_hbm.shape[0] // bm

    def row_copy(src_row, dst_row):
        return pltpu.make_async_copy(h2_ref.at[pl.ds(src_row, 1)], xs_hbm.at[pl.ds(dst_row, 1)], sem)

    def zero_copy(e):
        start = pl.multiple_of(pend_ref[e] - bm, bm)
        return pltpu.make_async_copy(zero_ref, xs_hbm.at[pl.ds(start, bm)], zsem)

    def tail_copy(b):
        return pltpu.make_async_copy(zero_ref, xs_hbm.at[pl.ds(pl.multiple_of(b * bm, bm), bm)], zsem)

    @pl.when(i == 0)
    def _():
        zero_ref[...] = jnp.zeros(zero_ref.shape, zero_ref.dtype)
        first_unused = pend_ref[N_EXPERTS - 1] // bm

        def zissue(e, carry):
            @pl.when(pend_ref[e] > pstart_ref[e])
            def _():
                zero_copy(e).start()
            return carry

        def zwait(e, carry):
            @pl.when(pend_ref[e] > pstart_ref[e])
            def _():
                zero_copy(e).wait()
            return carry

        def tissue(b, carry):
            tail_copy(b).start()
            return carry

        def twait(b, carry):
            tail_copy(b).wait()
            return carry

        lax.fori_loop(0, N_EXPERTS, zissue, 0)
        lax.fori_loop(first_unused, nblk, tissue, 0)
        lax.fori_loop(0, N_EXPERTS, zwait, 0)
        lax.fori_loop(first_unused, nblk, twait, 0)

    def issue(t, carry):
        for r in range(TOP_K):
            dst = pstart_ref[eidx_ref[r, t]] + slot_ref[r, t]
            row_copy(t, dst).start(priority=r % 2)
        return carry

    lax.fori_loop(0, td, issue, 0)
    for r in range(TOP_K):
        pltpu.make_async_copy(h2_ref, h2_ref, sem).wait()


def _dispatch(eidx, slot, pstart, pend, h2, nrows, td, bm):
    n, d = h2.shape
    return pl.pallas_call(
        functools.partial(_dispatch_kernel, td=td, bm=bm),
        grid=(n // td,),
        in_specs=[
            pl.BlockSpec((TOP_K, td), lambda i: (0, i), memory_space=pltpu.SMEM),
            pl.BlockSpec((TOP_K, td), lambda i: (0, i), memory_space=pltpu.SMEM),
            pl.BlockSpec(memory_space=pltpu.SMEM),
            pl.BlockSpec(memory_space=pltpu.SMEM),
            pl.BlockSpec((td, d), lambda i: (i, 0)),
        ],
        out_specs=pl.BlockSpec(memory_space=pl.ANY),
        out_shape=jax.ShapeDtypeStruct((nrows, d), h2.dtype),
        scratch_shapes=[
            pltpu.VMEM((bm, d), h2.dtype),
            pltpu.SemaphoreType.DMA(()),
            pltpu.SemaphoreType.DMA(()),
        ],
        compiler_params=_params(("arbitrary",)),
        name="moe_dispatch",
    )(eidx, slot, pstart, pend, h2)


def _experts_kernel(be_ref, nu_ref, x_ref, wgu_ref, wd_ref, o_ref):
    i = pl.program_id(0)

    @pl.when(i < nu_ref[0])
    def _():
        lo, hi = _unpack_bf16_pair(x_ref[...])
        x = jnp.concatenate([lo.astype(BF16), hi.astype(BF16)], axis=1)
        gu = _dot(x, wgu_ref[...].astype(BF16))
        act = (_silu(gu[:, 0:D_EXPERT]) * gu[:, D_EXPERT:]).astype(BF16)
        y = _dot(act, wd_ref[...].astype(BF16))
        o_ref[...] = _pack_bf16_pair(y[:, 0:D_MODEL // 2], y[:, D_MODEL // 2:])

    @pl.when(i >= nu_ref[0])
    def _():
        o_ref[...] = jnp.zeros(o_ref.shape, o_ref.dtype)


def _experts(block_e, nused, xs, w_gate_up_l, w_down_l, bm):
    nrows, dh = xs.shape
    d = 2 * dh
    nblk = nrows // bm
    grid_spec = pltpu.PrefetchScalarGridSpec(
        num_scalar_prefetch=2,
        grid=(nblk,),
        in_specs=[
            pl.BlockSpec((bm, dh), lambda i, be, nu: (jnp.minimum(i, nu[0] - 1), 0)),
            pl.BlockSpec((None, d, 2 * D_EXPERT), lambda i, be, nu: (be[i], 0, 0)),
            pl.BlockSpec((None, D_EXPERT, d), lambda i, be, nu: (be[i], 0, 0)),
        ],
        out_specs=pl.BlockSpec((bm, dh), lambda i, be, nu: (i, 0)),
    )
    return pl.pallas_call(
        _experts_kernel,
        grid_spec=grid_spec,
        out_shape=jax.ShapeDtypeStruct((nrows, dh), U32),
        compiler_params=_params(("arbitrary",)),
        name="moe_experts",
    )(block_e, nused, xs, w_gate_up_l, w_down_l)


def _combine_kernel(e0_ref, s0_ref, e1_ref, s1_ref, pstart_ref, wl_ref, xs_ref, ga2_ref, gf_ref,
                    yb_hbm, o_ref, g_ref, sem, *, tc):
    i = pl.program_id(0)
    n = pl.num_programs(0)
    cur = lax.rem(i, 2)

    def issue(e_ref, s_ref, buf):
        def body(t, carry):
            for r in range(TOP_K):
                src = pstart_ref[e_ref[r, t]] + s_ref[r, t]
                pltpu.make_async_copy(yb_hbm.at[pl.ds(src, 1)], g_ref.at[buf, r, pl.ds(t, 1)],
                                      sem.at[buf]).start(priority=r % 2)
            return carry
        lax.fori_loop(0, tc, body, 0)

    @pl.when(i == 0)
    def _():
        issue(e0_ref, s0_ref, 0)

    @pl.when(i + 1 < n)
    def _():
        issue(e1_ref, s1_ref, 1 - cur)

    pltpu.make_async_copy(g_ref.at[cur], g_ref.at[cur], sem.at[cur]).wait()

    wl = wl_ref[...]
    r_lo = None
    for r in range(TOP_K):
        lo, hi = _unpack_bf16_pair(g_ref[cur, r])
        w = wl[:, r:r + 1]
        r_lo = w * lo if r_lo is None else r_lo + w * lo
        r_hi = w * hi if r == 0 else r_hi + w * hi
    routed = jnp.concatenate([r_lo, r_hi], axis=1)
    x2 = xs_ref[...] + ga2_ref[...] * routed
    ms = jnp.mean(x2 * x2, axis=-1, keepdims=True)
    o_ref[...] = x2 * lax.rsqrt(ms + EPS) * gf_ref[...]


def _combine(eidx, slot, pstart, wl_t, xs_base, ga2, gfin, yb, tc):
    n, d = xs_base.shape
    nt = n // tc
    smem_cur = pl.BlockSpec((TOP_K, tc), lambda i: (0, i), memory_space=pltpu.SMEM)
    smem_nxt = pl.BlockSpec((TOP_K, tc), lambda i: (0, jnp.minimum(i + 1, nt - 1)),
                            memory_space=pltpu.SMEM)
    return pl.pallas_call(
        functools.partial(_combine_kernel, tc=tc),
        grid=(nt,),
        in_specs=[
            smem_cur, smem_cur, smem_nxt, smem_nxt,
            pl.BlockSpec(memory_space=pltpu.SMEM),
            pl.BlockSpec((tc, TOP_K), lambda i: (i, 0)),
            pl.BlockSpec((tc, d), lambda i: (i, 0)),
            _mod_spec(ga2, tc),
            pl.BlockSpec((1, d), lambda i: (0, 0)),
            pl.BlockSpec(memory_space=pl.ANY),
        ],
        out_specs=pl.BlockSpec((tc, d), lambda i: (i, 0)),
        out_shape=jax.ShapeDtypeStruct((n, d), F32),
        scratch_shapes=[
            pltpu.VMEM((2, TOP_K, tc, d // 2), U32),
            pltpu.SemaphoreType.DMA((2,)),
        ],
        compiler_params=_params(("arbitrary",)),
        name="moe_combine",
    )(eidx, slot, eidx, slot, pstart, wl_t, xs_base, ga2, gfin, yb)


def _moe_and_final(x, ohg, oda, mods, wts, tm, td, tc, bm):
    n, d = x.shape
    ga1, sh2, sc2, ga2 = mods
    (w_out_b, g_ffn, wsgu_b, wsd_b, wr_t, rb, w_gate_up_l, w_down_l, g_final) = wts
    xs_base, h2, eidx, slot, wl, cnt = _post(
        x, ohg, oda, w_out_b, ga1, g_ffn, sc2, sh2, ga2, wsgu_b, wsd_b, wr_t, rb, tm)
    counts = cnt[:, 0].astype(I32)
    padded = (counts + bm - 1) // bm * bm
    pend = jnp.cumsum(padded)
    pstart = pend - padded
    nblk = -(-(n * TOP_K) // bm) + N_EXPERTS
    nused = (pend[-1] // bm).astype(I32)
    blk_row = jnp.minimum(jnp.arange(nblk, dtype=I32), nused - 1) * bm
    be = jnp.sum((pend[None, :] <= blk_row[:, None]).astype(I32), axis=1)
    xs = _dispatch(eidx, slot, pstart, pend, h2, nblk * bm, td, bm)
    yb = _experts(be, nused.reshape(1), xs, w_gate_up_l, w_down_l, bm)
    return _combine(eidx, slot, pstart, wl.T, xs_base, ga2, g_final, yb, tc)


def _expand(mod, reps):
    if mod.shape[0] == 1:
        return mod
    return jnp.repeat(mod, reps, axis=0)


def kernel(x_prompt, x_sample, cache_k, cache_v, state_hgrn, c_prompt, c_sample, w_ada, b_ada,
           norm_mix, norm_ffn, norm_final, w_in, w_out, hg_lb_logits, hg_norm, da_lambda, da_norm,
           rel_bias_table, w_router, router_bias, w_gate_up, w_down, ws_gate_up, ws_down):
    depth = w_in.shape[0]
    assert depth == 1 and hg_lb_logits.shape[0] == 2
    bp, tp, d = x_prompt.shape
    bs, ts, _ = x_sample.shape
    assert bp == 1
    past = cache_k.shape[2]
    l = 0

    rows = -(-(bp + bs) // 8) * 8
    c_all = jnp.zeros((rows, d), F32).at[:bp].set(c_prompt).at[bp:bp + bs].set(c_sample)
    mod = _adaln(c_all, w_ada[l], b_ada[l])
    mod_p = [mod[0:bp, j * d:(j + 1) * d] for j in range(6)]
    mod_s = [_expand(mod[bp:bp + bs, j * d:(j + 1) * d], ts) for j in range(6)]

    w_in_b = w_in[l].astype(BF16)
    w_out_b = w_out[l].astype(BF16)
    wsgu_b = ws_gate_up[l].astype(BF16)
    wsd_b = ws_down[l].astype(BF16)
    wr_t = w_router[l].T
    g_mix = norm_mix[l].reshape(1, d)
    g_ffn = norm_ffn[l].reshape(1, d)
    g_final = norm_final.reshape(1, d)
    moe_w = (w_out_b, g_ffn, wsgu_b, wsd_b, wr_t, router_bias[l], w_gate_up[l], w_down[l], g_final)

    lam = _lam(da_lambda[l])

    t_att = min(ATT_TILE, tp)
    kk = jnp.arange(t_att, dtype=I32)[:, None]
    qq = jnp.arange(t_att, dtype=I32)[None, :]
    idx_diag = jnp.where((kk // CHUNK) <= (qq // CHUNK), _rel_bucket(kk - qq), MASK_BUCKET)
    idx_prev = _rel_bucket(kk - qq - t_att)
    bias_p = _bias_tiles(rel_bias_table, jnp.stack([idx_diag, idx_prev]).astype(I32),
                         shift_bucket=N_BUCKETS // 2 - 1)
    pad = 128
    qpos = past + jnp.arange(ts, dtype=I32)[:, None]
    idx_sp = _rel_bucket(jnp.arange(past, dtype=I32)[None, :] - qpos)
    kn = jnp.arange(pad, dtype=I32)[None, :]
    idx_sn = jnp.where(kn < ts, _rel_bucket(past + kn - qpos), MASK_BUCKET)
    bias_sp = _bias_tiles(rel_bias_table, idx_sp[None].astype(I32), shift_bucket=None)
    bias_sn = _bias_tiles(rel_bias_table, idx_sn[None].astype(I32), shift_bucket=None)

    xp = x_prompt.reshape(bp * tp, d)
    sh1, sc1, ga1, sh2, sc2, ga2 = mod_p
    zh, qs, kf, vf, kb, vb = _inproj(xp, g_mix, sc1, sh1, w_in_b, min(INPROJ_TILE, tp))
    s_zero = jnp.zeros((bp, HG_HEADS, HG_DIM, HG_DIM), F32)
    ohg_p, sp_new = _hgrn(zh, s_zero, hg_lb_logits, hg_norm[l], bp, tp, min(HGRN_CHUNK, tp))
    vt = vb.reshape(tp // t_att, t_att, DA_HEADS, DA_VDIM).transpose(2, 0, 3, 1)
    oda_p = _attn_prompt(kb, qs.T, vt, bias_p, lam, da_norm[l], t_att)
    y_p = _moe_and_final(xp, ohg_p, oda_p, (ga1, sh2, sc2, ga2), moe_w,
                         min(POST_TILE, tp), min(DISPATCH_TILE, tp), min(COMBINE_TILE, tp),
                         MOE_BLOCK_ROWS)
    k_prompt = kf.reshape(1, bp, tp, DA_HEADS, 2 * DA_QKDIM)
    v_prompt = vf.reshape(1, bp, tp, DA_HEADS, DA_VDIM)

    ns = bs * ts
    xs_ = x_sample.reshape(ns, d)
    sh1, sc1, ga1, sh2, sc2, ga2 = mod_s
    zh, qs, kf, vf, kb, vb = _inproj(xs_, g_mix, sc1, sh1, w_in_b, ns)
    ohg_s, ss_new = _hgrn(zh, state_hgrn[l], hg_lb_logits, hg_norm[l], bs, ts, ts)
    oda_s = _attn_step(qs, cache_k[l], cache_v[l], kb, vb, bias_sp, bias_sn, lam, da_norm[l], bs, ts)
    y_s = _moe_and_final(xs_, ohg_s, oda_s, (ga1, sh2, sc2, ga2), moe_w,
                         ns, ns, min(COMBINE_TILE, ns), 128)
    k_sample = kf.reshape(1, bs, ts, DA_HEADS, 2 * DA_QKDIM)
    v_sample = vf.reshape(1, bs, ts, DA_HEADS, DA_VDIM)

    return (y_p.reshape(bp, tp, d), y_s.reshape(bs, ts, d), k_prompt, v_prompt, sp_new[None],
            k_sample, v_sample, ss_new[None].astype(x_sample.dtype))
```

```python
import functools
import math

import numpy as np
import jax
import jax.numpy as jnp
from jax import lax
from jax.experimental import pallas as pl
from jax.experimental.pallas import tpu as pltpu

F32 = jnp.float32
BF16 = jnp.bfloat16
I32 = jnp.int32
U32 = jnp.uint32
HIGHEST = lax.Precision.HIGHEST

D_MODEL = 1024
CHUNK = 64
HG_HEADS = 4
HG_DIM = 128
HG_WIDTH = HG_HEADS * HG_DIM
DA_HEADS = 4
DA_VDIM = 128
DA_QKDIM = 64
DA_WIDTH = DA_HEADS * DA_VDIM
N_BUCKETS = 32
MAX_DIST = 128
N_EXPERTS = 64
TOP_K = 8
N_GROUPS = 8
GROUP_SIZE = N_EXPERTS // N_GROUPS
TOP_GROUPS = 4
D_EXPERT = 256
ROUTE_SCALE = 2.5
EPS = 1e-6
LAM_INIT = 0.8 - 0.6 * math.exp(-0.3 * 0)

LOG2E = math.log2(math.e)
HI_MASK = np.uint32(0xFFFF0000)
NEG_BIG = -1e30
MASK_BUCKET = N_BUCKETS
V7X_VMEM_LIMIT = 48 * 1024 * 1024

ATT_TILE = 512
HGRN_CHUNK = 256
INPROJ_TILE = 512
POST_TILE = 256
DISPATCH_TILE = 512
COMBINE_TILE = 256
MOE_BLOCK_ROWS = 512
ROW_GROUP = 8


def _sigmoid(x):
    return 1.0 / (1.0 + jnp.exp(-x))


def _silu(x):
    return x * _sigmoid(x)


def _dot(a, b, **kw):
    return jnp.dot(a, b, preferred_element_type=F32, **kw)


def _dot_nt(a, b, **kw):
    return lax.dot_general(a, b, (((1,), (1,)), ((), ())), preferred_element_type=F32, **kw)


def _dot_tn(a, b, **kw):
    return lax.dot_general(a, b, (((0,), (0,)), ((), ())), preferred_element_type=F32, **kw)


def _pack_bf16_pair(lo, hi):
    lo_bits = lax.bitcast_convert_type(lo.astype(BF16).astype(F32), U32)
    hi_bits = lax.bitcast_convert_type(hi.astype(BF16).astype(F32), U32)
    return (lo_bits >> 16) | (hi_bits & HI_MASK)


def _unpack_bf16_pair(w):
    lo = lax.bitcast_convert_type(w << 16, F32)
    hi = lax.bitcast_convert_type(w & HI_MASK, F32)
    return lo, hi


def _params(sem, vmem=V7X_VMEM_LIMIT, flags=None):
    return pltpu.CompilerParams(dimension_semantics=sem, vmem_limit_bytes=vmem, flags=flags)


def _adaln_kernel(c_ref, w_ref, b_ref, o_ref):
    s = _silu(c_ref[...])
    o_ref[...] = _dot(s, w_ref[...], precision=HIGHEST) + b_ref[...]


def _adaln(c_all, w_ada, b_ada):
    rows, d = c_all.shape
    cols = w_ada.shape[1]
    blk = 1024
    return pl.pallas_call(
        _adaln_kernel,
        grid=(cols // blk,),
        in_specs=[
            pl.BlockSpec((rows, d), lambda j: (0, 0)),
            pl.BlockSpec((d, blk), lambda j: (0, j)),
            pl.BlockSpec((1, blk), lambda j: (0, j)),
        ],
        out_specs=pl.BlockSpec((rows, blk), lambda j: (0, j)),
        out_shape=jax.ShapeDtypeStruct((rows, cols), F32),
        compiler_params=_params(("parallel",)),
        name="adaln",
    )(c_all, w_ada, b_ada.reshape(1, cols))


def _lam_kernel(l_ref, o_ref):
    l = l_ref[...].astype(F32)
    a = jnp.sum(l[0:1] * l[1:2], axis=-1, keepdims=True)
    b = jnp.sum(l[2:3] * l[3:4], axis=-1, keepdims=True)
    lam = jnp.exp(a) - jnp.exp(b) + LAM_INIT
    o_ref[...] = jnp.broadcast_to(lam, o_ref.shape)


def _lam(da_lambda_l):
    return pl.pallas_call(
        _lam_kernel,
        out_shape=jax.ShapeDtypeStruct((8, 128), F32),
        name="lam",
    )(da_lambda_l)


def _rel_bucket(rel):
    nb = N_BUCKETS // 2
    max_exact = nb // 2
    side = jnp.where(rel > 0, nb, 0)
    n = jnp.abs(rel)
    large = max_exact + (jnp.log(jnp.maximum(n, 1).astype(F32) / max_exact)
                         / math.log(MAX_DIST / max_exact) * (nb - max_exact)).astype(I32)
    large = jnp.minimum(large, nb - 1)
    return side + jnp.where(n < max_exact, n, large)


def _bias_kernel(tab_ref, idx_ref, o_ref, *, shift_bucket):
    h = pl.program_id(0)
    idx = idx_ref[...]
    shift = tab_ref[shift_bucket, h] if shift_bucket is not None else 0.0
    acc = jnp.zeros(idx.shape, F32)
    for j in range(N_BUCKETS):
        acc = jnp.where(idx == j, (tab_ref[j, h] - shift) * LOG2E, acc)
    o_ref[...] = jnp.where(idx == MASK_BUCKET, NEG_BIG, acc)


def _bias_tiles(table, idx, *, shift_bucket):
    k, r, c = idx.shape
    return pl.pallas_call(
        functools.partial(_bias_kernel, shift_bucket=shift_bucket),
        grid=(DA_HEADS, k),
        in_specs=[
            pl.BlockSpec(memory_space=pltpu.SMEM),
            pl.BlockSpec((None, r, c), lambda h, d: (d, 0, 0)),
        ],
        out_specs=pl.BlockSpec((None, None, r, c), lambda h, d: (h, d, 0, 0)),
        out_shape=jax.ShapeDtypeStruct((DA_HEADS, k, r, c), F32),
        compiler_params=_params(("parallel", "parallel")),
        name="rel_bias",
    )(table, idx)


def _inproj_kernel(x_ref, g_ref, sc_ref, sh_ref, w_ref,
                   zh_ref, q_ref, k_ref, v_ref, kb_ref, vb_ref):
    x = x_ref[...]
    ms = jnp.mean(x * x, axis=-1, keepdims=True)
    h = x * lax.rsqrt(ms + EPS) * g_ref[...]
    h = h * (1.0 + sc_ref[...]) + sh_ref[...]
    hb = h.astype(BF16)
    c0 = 4 * HG_WIDTH
    zh_ref[...] = _dot(hb, w_ref[:, 0:c0])
    zq = _dot(hb, w_ref[:, c0:c0 + DA_WIDTH])
    q_ref[...] = (zq * (DA_QKDIM ** -0.5 * LOG2E)).astype(BF16)
    zk = _dot(hb, w_ref[:, c0 + DA_WIDTH:c0 + 2 * DA_WIDTH])
    k_ref[...] = zk
    kb_ref[...] = zk.astype(BF16)
    zv = _dot(hb, w_ref[:, c0 + 2 * DA_WIDTH:c0 + 3 * DA_WIDTH])
    v_ref[...] = zv
    vb_ref[...] = zv.astype(BF16)


def _mod_spec(mod, tm):
    if mod.shape[0] == 1:
        return pl.BlockSpec((1, mod.shape[1]), lambda i: (0, 0))
    return pl.BlockSpec((tm, mod.shape[1]), lambda i: (i, 0))


def _inproj(x, g, sc, sh, w_in_b, tm):
    n, d = x.shape
    cols = w_in_b.shape[1]
    row = lambda i: (i, 0)
    return pl.pallas_call(
        _inproj_kernel,
        grid=(n // tm,),
        in_specs=[
            pl.BlockSpec((tm, d), row),
            pl.BlockSpec((1, d), lambda i: (0, 0)),
            _mod_spec(sc, tm),
            _mod_spec(sh, tm),
            pl.BlockSpec((d, cols), lambda i: (0, 0)),
        ],
        out_specs=[
            pl.BlockSpec((tm, 4 * HG_WIDTH), row),
            pl.BlockSpec((tm, DA_WIDTH), row),
            pl.BlockSpec((tm, DA_WIDTH), row),
            pl.BlockSpec((tm, DA_WIDTH), row),
            pl.BlockSpec((tm, DA_WIDTH), row),
            pl.BlockSpec((tm, DA_WIDTH), row),
        ],
        out_shape=[
            jax.ShapeDtypeStruct((n, 4 * HG_WIDTH), F32),
            jax.ShapeDtypeStruct((n, DA_WIDTH), BF16),
            jax.ShapeDtypeStruct((n, DA_WIDTH), F32),
            jax.ShapeDtypeStruct((n, DA_WIDTH), F32),
            jax.ShapeDtypeStruct((n, DA_WIDTH), BF16),
            jax.ShapeDtypeStruct((n, DA_WIDTH), BF16),
        ],
        compiler_params=_params(("parallel",)),
        name="inproj",
    )(x, g, sc, sh, w_in_b)


def _hgrn_consts(c):
    levels = int(round(math.log2(c)))
    assert 1 << levels == c
    t = np.arange(c)[:, None]
    r = np.arange(c)[None, :]
    blocks = [r <= t]
    for l in range(levels):
        m = 1 << l
        mid = (t // (2 * m)) * (2 * m) + m - 1
        later = (t & m) != 0
        blocks.append(np.where(later, (r > mid) & (r <= t), (r > t) & (r <= mid)))
    mall = np.concatenate(blocks, axis=0).astype(np.float32)
    x = np.maximum(t ^ r, 1)
    lv = np.where(t == r, -1, np.where(t > r, np.floor(np.log2(x)).astype(np.int64), -2))
    return jnp.asarray(mall, dtype=BF16), jnp.asarray(lv, dtype=I32), levels


def _hgrn_kernel(zh_ref, s0_ref, lbl_ref, gain_ref, mall_ref, lv_ref,
                 o_ref, sout_ref, st_ref, *, c, levels):
    ci = pl.program_id(1)

    @pl.when(ci == 0)
    def _():
        for h in range(HG_HEADS):
            st_ref[h] = s0_ref[h].astype(F32).T

    lbl = lbl_ref[...].astype(F32)
    mx = jnp.maximum(lbl[0:1], lbl[1:2])
    e0 = jnp.exp(lbl[0:1] - mx)
    e1 = jnp.exp(lbl[1:2] - mx)
    lb = e0 / (e0 + e1)

    xq = zh_ref[:, 0:HG_WIDTH]
    xf = zh_ref[:, HG_WIDTH:2 * HG_WIDTH]
    q = _silu(xq)
    y = lb + (1.0 - lb) * _sigmoid(xf)
    logf = jnp.log(y)
    kk = 1.0 - y

    l1 = logf.astype(BF16)
    r1 = logf - l1.astype(F32)
    l2 = r1.astype(BF16)
    l3 = (r1 - l2.astype(F32)).astype(BF16)
    mall = mall_ref[...]
    e_all = _dot(mall, l1) + _dot(mall, l2) + _dot(mall, l3)

    lv = lv_ref[...]
    gain = gain_ref[...].astype(F32)
    for h in range(HG_HEADS):
        sl = slice(h * HG_DIM, (h + 1) * HG_DIM)
        qh = q[:, sl]
        kh = kk[:, sl]
        ih = zh_ref[:, 2 * HG_WIDTH + h * HG_DIM:2 * HG_WIDTH + (h + 1) * HG_DIM]
        gh = zh_ref[:, 3 * HG_WIDTH + h * HG_DIM:3 * HG_WIDTH + (h + 1) * HG_DIM]
        bh = e_all[0:c, sl]
        ihb = ih.astype(BF16)
        a = jnp.where(lv == -1, _dot_nt(qh.astype(BF16), kh.astype(BF16)), 0.0)
        for l in range(levels):
            f = jnp.exp(e_all[(l + 1) * c:(l + 2) * c, sl])
            p = _dot_nt((qh * f).astype(BF16), (kh * f).astype(BF16))
            a = a + jnp.where(lv == l, p, 0.0)
        st = st_ref[h]
        o = _dot(a.astype(BF16), ihb) + _dot_nt((qh * jnp.exp(bh)).astype(BF16), st.astype(BF16))
        bl = bh[c - 1:c, :]
        kd = (kh * jnp.exp(bl - bh)).astype(BF16)
        st_ref[h] = st * jnp.exp(bl) + _dot_tn(ihb, kd)
        ms = jnp.mean(o * o, axis=-1, keepdims=True)
        on = o * lax.rsqrt(ms + EPS) * gain
        o_ref[:, sl] = (on * _silu(gh)).astype(o_ref.dtype)

    @pl.when(ci == pl.num_programs(1) - 1)
    def _():
        for h in range(HG_HEADS):
            sout_ref[h] = st_ref[h].T.astype(sout_ref.dtype)


def _hgrn(zh, s0, lb_logits, gain, batch, seq, c):
    mall, lv, levels = _hgrn_consts(c)
    nc = seq // c
    return pl.pallas_call(
        functools.partial(_hgrn_kernel, c=c, levels=levels),
        grid=(batch, nc),
        in_specs=[
            pl.BlockSpec((c, 4 * HG_WIDTH), lambda b, i: (b * nc + i, 0)),
            pl.BlockSpec((None, HG_HEADS, HG_DIM, HG_DIM), lambda b, i: (b, 0, 0, 0)),
            pl.BlockSpec(lb_logits.shape, lambda b, i: (0, 0)),
            pl.BlockSpec((1, HG_DIM), lambda b, i: (0, 0)),
            pl.BlockSpec(mall.shape, lambda b, i: (0, 0)),
            pl.BlockSpec(lv.shape, lambda b, i: (0, 0)),
        ],
        out_specs=[
            pl.BlockSpec((c, HG_WIDTH), lambda b, i: (b * nc + i, 0)),
            pl.BlockSpec((None, HG_HEADS, HG_DIM, HG_DIM), lambda b, i: (b, 0, 0, 0)),
        ],
        out_shape=[
            jax.ShapeDtypeStruct((batch * seq, HG_WIDTH), BF16),
            jax.ShapeDtypeStruct((batch, HG_HEADS, HG_DIM, HG_DIM), F32),
        ],
        scratch_shapes=[pltpu.VMEM((HG_HEADS, HG_DIM, HG_DIM), F32)],
        compiler_params=_params(("parallel", "arbitrary")),
        name="hgrn2",
    )(zh, s0, lb_logits, gain.reshape(1, HG_DIM), mall, lv)


def _attn_kernel(k_ref, qt_ref, vt_ref, bias_ref, lam_ref, gain_ref,
                 o_ref, qz_ref, m_ref, l_ref, acc_ref, s_ref, *, t):
    i = pl.program_id(1)
    qt = qt_ref[...]
    row = lax.broadcasted_iota(I32, qt.shape, 0)
    zero = jnp.zeros_like(qt)
    qz_ref[:, 0:t] = jnp.where(row < DA_QKDIM, qt, zero)
    qz_ref[:, t:2 * t] = jnp.where(row >= DA_QKDIM, qt, zero)
    m_ref[...] = jnp.full(m_ref.shape, NEG_BIG, F32)
    l_ref[...] = jnp.zeros(l_ref.shape, F32)
    acc_ref[...] = jnp.zeros(acc_ref.shape, F32)

    def scores(j, buf):
        kt = k_ref[pl.ds(pl.multiple_of(j * t, t), t), :]
        s_ref[buf] = _dot(kt, qz_ref[...])

    def consume(j, buf, bias_idx):
        s = s_ref[buf]
        if bias_idx is not None:
            b = bias_ref[bias_idx]
            s = jnp.concatenate([s[:, 0:t] + b, s[:, t:2 * t] + b], axis=1)
        m_prev = m_ref[...]
        m_new = jnp.maximum(m_prev, jnp.max(s, axis=0, keepdims=True))
        alpha = jnp.exp2(m_prev - m_new)
        pr = jnp.exp2(s - m_new)
        l_ref[...] = alpha * l_ref[...] + jnp.sum(pr, axis=0, keepdims=True)
        acc_ref[...] = alpha * acc_ref[...] + _dot(vt_ref[j], pr.astype(BF16))
        m_ref[...] = m_new

    n_far = jnp.maximum(i - 1, 0)

    @pl.when(n_far > 0)
    def _():
        scores(0, 0)

    def far_pair(p, carry):
        j = 2 * p
        scores(j + 1, 1)
        consume(j, 0, None)
        scores(jnp.minimum(j + 2, n_far - 1), 0)
        consume(j + 1, 1, None)
        return carry

    lax.fori_loop(0, n_far // 2, far_pair, 0)

    @pl.when(lax.rem(n_far, 2) == 1)
    def _():
        consume(n_far - 1, 0, None)

    @pl.when(i >= 1)
    def _():
        scores(i - 1, 0)
        scores(i, 1)
        consume(i - 1, 0, 1)
        consume(i, 1, 0)

    @pl.when(i == 0)
    def _():
        scores(i, 1)
        consume(i, 1, 0)

    lam = lam_ref[0:1, 0:1]
    l = l_ref[...]
    acc = acc_ref[...]
    o = acc[:, 0:t] / l[:, 0:t] - lam * (acc[:, t:2 * t] / l[:, t:2 * t])
    ms = jnp.mean(o * o, axis=0, keepdims=True)
    on = o * lax.rsqrt(ms + EPS) * gain_ref[...].astype(F32) * (1.0 - LAM_INIT)
    o_ref[...] = on.T.astype(o_ref.dtype)


def _attn_prompt(kb, qt, vt, bias, lam, gain, t):
    n = kb.shape[0]
    nt = n // t
    return pl.pallas_call(
        functools.partial(_attn_kernel, t=t),
        grid=(DA_HEADS, nt),
        in_specs=[
            pl.BlockSpec((n, DA_VDIM), lambda h, i: (0, h)),
            pl.BlockSpec((DA_VDIM, t), lambda h, i: (h, i)),
            pl.BlockSpec((None, nt, DA_VDIM, t), lambda h, i: (h, 0, 0, 0)),
            pl.BlockSpec((None, 2, t, t), lambda h, i: (h, 0, 0, 0)),
            pl.BlockSpec((8, 128), lambda h, i: (0, 0)),
            pl.BlockSpec((DA_VDIM, 1), lambda h, i: (0, 0)),
        ],
        out_specs=pl.BlockSpec((t, DA_VDIM), lambda h, i: (i, h)),
        out_shape=jax.ShapeDtypeStruct((n, DA_WIDTH), BF16),
        scratch_shapes=[
            pltpu.VMEM((DA_VDIM, 2 * t), BF16),
            pltpu.VMEM((1, 2 * t), F32),
            pltpu.VMEM((1, 2 * t), F32),
            pltpu.VMEM((DA_VDIM, 2 * t), F32),
            pltpu.VMEM((2, t, 2 * t), F32),
        ],
        compiler_params=_params(("parallel", "parallel")),
        name="diff_attn_prompt",
    )(kb, qt, vt, bias, lam, gain.reshape(DA_VDIM, 1))


def _attn_step_kernel(q_ref, kp_ref, vp_ref, kn_ref, vn_ref, bp_ref, bn_ref, lam_ref, gain_ref,
                      o_ref, *, tq, pad):
    q = q_ref[...]
    lane = lax.broadcasted_iota(I32, q.shape, 1)
    zero = jnp.zeros_like(q)
    qz = jnp.concatenate([jnp.where(lane < DA_QKDIM, q, zero),
                          jnp.where(lane >= DA_QKDIM, q, zero)], axis=0)
    kp = kp_ref[...].astype(BF16)
    vp = vp_ref[...].astype(BF16)
    zpad = jnp.zeros((pad - tq, DA_VDIM), BF16)
    kn = jnp.concatenate([kn_ref[...], zpad], axis=0)
    vn = jnp.concatenate([vn_ref[...], zpad], axis=0)
    bp = bp_ref[...]
    bn = bn_ref[...]
    sp = _dot_nt(qz, kp) + jnp.concatenate([bp, bp], axis=0)
    sn = _dot_nt(qz, kn) + jnp.concatenate([bn, bn], axis=0)
    m = jnp.maximum(jnp.max(sp, axis=-1, keepdims=True), jnp.max(sn, axis=-1, keepdims=True))
    pp = jnp.exp2(sp - m)
    pn = jnp.exp2(sn - m)
    l = jnp.sum(pp, axis=-1, keepdims=True) + jnp.sum(pn, axis=-1, keepdims=True)
    acc = _dot(pp.astype(BF16), vp) + _dot(pn.astype(BF16), vn)
    on = acc / l
    lam = lam_ref[0:1, 0:1]
    o = on[0:tq] - lam * on[tq:2 * tq]
    ms = jnp.mean(o * o, axis=-1, keepdims=True)
    o = o * lax.rsqrt(ms + EPS) * gain_ref[...].astype(F32) * (1.0 - LAM_INIT)
    o_ref[...] = o.astype(o_ref.dtype)


def _attn_step(qs, cache_k_l, cache_v_l, kb, vb, bias_p, bias_n, lam, gain, batch, tq):
    past = cache_k_l.shape[1]
    pad = bias_n.shape[-1]
    return pl.pallas_call(
        functools.partial(_attn_step_kernel, tq=tq, pad=pad),
        grid=(batch, DA_HEADS),
        in_specs=[
            pl.BlockSpec((tq, DA_VDIM), lambda b, h: (b, h)),
            pl.BlockSpec((None, past, DA_VDIM), lambda b, h: (b, 0, h)),
            pl.BlockSpec((None, past, DA_VDIM), lambda b, h: (b, 0, h)),
            pl.BlockSpec((tq, DA_VDIM), lambda b, h: (b, h)),
            pl.BlockSpec((tq, DA_VDIM), lambda b, h: (b, h)),
            pl.BlockSpec((None, None, tq, past), lambda b, h: (h, 0, 0, 0)),
            pl.BlockSpec((None, None, tq, pad), lambda b, h: (h, 0, 0, 0)),
            pl.BlockSpec((8, 128), lambda b, h: (0, 0)),
            pl.BlockSpec((1, DA_VDIM), lambda b, h: (0, 0)),
        ],
        out_specs=pl.BlockSpec((tq, DA_VDIM), lambda b, h: (b, h)),
        out_shape=jax.ShapeDtypeStruct((batch * tq, DA_WIDTH), BF16),
        compiler_params=_params(("parallel", "parallel")),
        name="diff_attn_step",
    )(qs, cache_k_l.reshape(batch, past, DA_WIDTH), cache_v_l.reshape(batch, past, DA_WIDTH),
      kb, vb, bias_p, bias_n, lam, gain.reshape(1, DA_VDIM))


def _post_kernel(x_ref, ohg_ref, oda_ref, wout_ref, ga1_ref, g_ref, sc_ref, sh_ref, ga2_ref,
                 wsgu_ref, wsd_ref, wrt_ref, rb_ref, tri_ref,
                 xs_ref, h2_ref, eidx_ref, slot_ref, wl_ref, cnt_ref, carry_ref, *, tm):
    i = pl.program_id(0)

    @pl.when(i == 0)
    def _():
        carry_ref[...] = jnp.zeros(carry_ref.shape, F32)

    mix = _dot(ohg_ref[...], wout_ref[0:HG_WIDTH, :]) + _dot(oda_ref[...], wout_ref[HG_WIDTH:, :])
    x1 = x_ref[...] + ga1_ref[...] * mix
    ms = jnp.mean(x1 * x1, axis=-1, keepdims=True)
    h2 = x1 * lax.rsqrt(ms + EPS) * g_ref[...]
    h2 = h2 * (1.0 + sc_ref[...]) + sh_ref[...]
    h2_ref[...] = _pack_bf16_pair(h2[:, 0:D_MODEL // 2], h2[:, D_MODEL // 2:])
    h2b = h2.astype(BF16)
    gu = _dot(h2b, wsgu_ref[...])
    act = (_silu(gu[:, 0:D_EXPERT]) * gu[:, D_EXPERT:]).astype(BF16)
    xs_ref[...] = x1 + ga2_ref[...] * _dot(act, wsd_ref[...])

    logits = _dot_nt(wrt_ref[...], h2, precision=HIGHEST)
    score = _sigmoid(logits)
    sel = score + rb_ref[...]
    sub = lax.broadcasted_iota(I32, (GROUP_SIZE, tm), 0)
    gscore = []
    for g in range(N_GROUPS):
        v = sel[g * GROUP_SIZE:(g + 1) * GROUP_SIZE, :]
        m1 = jnp.max(v, axis=0, keepdims=True)
        i1 = jnp.min(jnp.where(v == m1, sub, GROUP_SIZE), axis=0, keepdims=True)
        m2 = jnp.max(jnp.where(sub == i1, -jnp.inf, v), axis=0, keepdims=True)
        gscore.append(m1 + m2)
    gsel = []
    for g in range(N_GROUPS):
        ahead = jnp.zeros((1, tm), F32)
        for g2 in range(N_GROUPS):
            if g2 == g:
                continue
            tie = 1.0 if g2 < g else 0.0
            ahead = ahead + jnp.where(gscore[g2] > gscore[g], 1.0,
                                      jnp.where(gscore[g2] == gscore[g], tie, 0.0))
        gsel.append(ahead < TOP_GROUPS)
    selm = jnp.concatenate(
        [jnp.where(gsel[g], sel[g * GROUP_SIZE:(g + 1) * GROUP_SIZE, :], -jnp.inf)
         for g in range(N_GROUPS)], axis=0)
    eio = lax.broadcasted_iota(I32, (N_EXPERTS, tm), 0)
    ahead = jnp.zeros((N_EXPERTS, tm), F32)
    for e2 in range(N_EXPERTS):
        row = selm[e2:e2 + 1, :]
        tie = jnp.where(eio > e2, 1.0, 0.0)
        ahead = ahead + jnp.where(row > selm, 1.0, jnp.where(row == selm, tie, 0.0))
    chosen = jnp.where(selm > -jnp.inf, jnp.where(ahead < TOP_K, 1.0, 0.0), 0.0)
    w = chosen * score
    wn = w / jnp.sum(w, axis=0, keepdims=True) * ROUTE_SCALE

    chb = chosen.astype(BF16)
    before = _dot(chb, tri_ref[...])
    carry = carry_ref[...]
    slot = jnp.concatenate([carry] * (tm // 128), axis=1) + before
    carry_new = carry + _dot(chb, jnp.ones((tm, 128), BF16))
    carry_ref[...] = carry_new
    cnt_ref[...] = carry_new

    eiof = eio.astype(F32)
    for r in range(TOP_K):
        pick = jnp.where(ahead == r, chosen, 0.0)
        eidx_ref[r:r + 1, :] = jnp.sum(pick * eiof, axis=0, keepdims=True).astype(I32)
        slot_ref[r:r + 1, :] = jnp.sum(pick * slot, axis=0, keepdims=True).astype(I32)
        wl_ref[r:r + 1, :] = jnp.sum(pick * wn, axis=0, keepdims=True)


def _post(x, ohg, oda, w_out_b, ga1, g, sc, sh, ga2, wsgu_b, wsd_b, wr_t, rb, tm):
    n, d = x.shape
    tri = jnp.asarray(np.triu(np.ones((tm, tm), np.float32), k=1), dtype=BF16)
    row = lambda i: (i, 0)
    col = lambda i: (0, i)
    full = lambda i: (0, 0)
    return pl.pallas_call(
        functools.partial(_post_kernel, tm=tm),
        grid=(n // tm,),
        in_specs=[
            pl.BlockSpec((tm, d), row),
            pl.BlockSpec((tm, HG_WIDTH), row),
            pl.BlockSpec((tm, DA_WIDTH), row),
            pl.BlockSpec(w_out_b.shape, full),
            _mod_spec(ga1, tm),
            pl.BlockSpec((1, d), full),
            _mod_spec(sc, tm),
            _mod_spec(sh, tm),
            _mod_spec(ga2, tm),
            pl.BlockSpec(wsgu_b.shape, full),
            pl.BlockSpec(wsd_b.shape, full),
            pl.BlockSpec(wr_t.shape, full),
            pl.BlockSpec((N_EXPERTS, 1), full),
            pl.BlockSpec((tm, tm), full),
        ],
        out_specs=[
            pl.BlockSpec((tm, d), row),
            pl.BlockSpec((tm, d // 2), row),
            pl.BlockSpec((TOP_K, tm), col),
            pl.BlockSpec((TOP_K, tm), col),
            pl.BlockSpec((TOP_K, tm), col),
            pl.BlockSpec((N_EXPERTS, 128), full),
        ],
        out_shape=[
            jax.ShapeDtypeStruct((n, d), F32),
            jax.ShapeDtypeStruct((n, d // 2), U32),
            jax.ShapeDtypeStruct((TOP_K, n), I32),
            jax.ShapeDtypeStruct((TOP_K, n), I32),
            jax.ShapeDtypeStruct((TOP_K, n), F32),
            jax.ShapeDtypeStruct((N_EXPERTS, 128), F32),
        ],
        scratch_shapes=[pltpu.VMEM((N_EXPERTS, 128), F32)],
        compiler_params=_params(("arbitrary",)),
        name="post_mix_router",
    )(x, ohg, oda, w_out_b, ga1, g, sc, sh, ga2, wsgu_b, wsd_b, wr_t, rb.reshape(N_EXPERTS, 1), tri)


def _dispatch_kernel(dst_ref, pstart_ref, pend_ref, h2_ref, xs_hbm,
                     zero_ref, zsem, sem, *, td, bm):
    i = pl.program_id(0)
    nblk = xs_hbm.shape[0] // bm

    def zero_copy(e):
        start = pl.multiple_of(pend_ref[e] - bm, bm)
        return pltpu.make_async_copy(zero_ref, xs_hbm.at[pl.ds(start, bm)], zsem)

    def tail_copy(b):
        return pltpu.make_async_copy(zero_ref, xs_hbm.at[pl.ds(pl.multiple_of(b * bm, bm), bm)], zsem)

    @pl.when(i == 0)
    def _():
        zero_ref[...] = jnp.zeros(zero_ref.shape, zero_ref.dtype)
        first_unused = pend_ref[N_EXPERTS - 1] // bm

        def zissue(e, carry):
            @pl.when(pend_ref[e] > pstart_ref[e])
            def _():
                zero_copy(e).start()
            return carry

        def zwait(e, carry):
            @pl.when(pend_ref[e] > pstart_ref[e])
            def _():
                zero_copy(e).wait()
            return carry

        def tissue(b, carry):
            tail_copy(b).start()
            return carry

        def twait(b, carry):
            tail_copy(b).wait()
            return carry

        lax.fori_loop(0, N_EXPERTS, zissue, 0)
        lax.fori_loop(first_unused, nblk, tissue, 0)
        lax.fori_loop(0, N_EXPERTS, zwait, 0)
        lax.fori_loop(first_unused, nblk, twait, 0)

    def issue(tg, carry):
        row0 = pl.multiple_of(tg * ROW_GROUP, ROW_GROUP)
        for ts in range(ROW_GROUP):
            for r in range(TOP_K):
                dst = dst_ref[tg * (ROW_GROUP * TOP_K) + ts * TOP_K + r]
                pltpu.make_async_copy(h2_ref.at[pl.ds(row0 + ts, 1)], xs_hbm.at[pl.ds(dst, 1)],
                                      sem).start(priority=r % 2)
        return carry

    lax.fori_loop(0, td // ROW_GROUP, issue, 0)
    for r in range(TOP_K):
        pltpu.make_async_copy(h2_ref, h2_ref, sem).wait()


def _dispatch(dst_flat, pstart, pend, h2, nrows, td, bm):
    n, d = h2.shape
    return pl.pallas_call(
        functools.partial(_dispatch_kernel, td=td, bm=bm),
        grid=(n // td,),
        in_specs=[
            pl.BlockSpec((TOP_K * td,), lambda i: (i,), memory_space=pltpu.SMEM),
            pl.BlockSpec(memory_space=pltpu.SMEM),
            pl.BlockSpec(memory_space=pltpu.SMEM),
            pl.BlockSpec((td, d), lambda i: (i, 0)),
        ],
        out_specs=pl.BlockSpec(memory_space=pl.ANY),
        out_shape=jax.ShapeDtypeStruct((nrows, d), h2.dtype),
        scratch_shapes=[
            pltpu.VMEM((bm, d), h2.dtype),
            pltpu.SemaphoreType.DMA(()),
            pltpu.SemaphoreType.DMA(()),
        ],
        compiler_params=_params(("arbitrary",)),
        name="moe_dispatch",
    )(dst_flat, pstart, pend, h2)


def _experts_kernel(be_ref, nu_ref, x_ref, wgu_ref, wd_ref, o_ref):
    i = pl.program_id(0)

    @pl.when(i < nu_ref[0])
    def _():
        lo, hi = _unpack_bf16_pair(x_ref[...])
        x = jnp.concatenate([lo.astype(BF16), hi.astype(BF16)], axis=1)
        gu = _dot(x, wgu_ref[...].astype(BF16))
        act = (_silu(gu[:, 0:D_EXPERT]) * gu[:, D_EXPERT:]).astype(BF16)
        y = _dot(act, wd_ref[...].astype(BF16))
        o_ref[...] = _pack_bf16_pair(y[:, 0:D_MODEL // 2], y[:, D_MODEL // 2:])

    @pl.when(i >= nu_ref[0])
    def _():
        o_ref[...] = jnp.zeros(o_ref.shape, o_ref.dtype)


def _experts(block_e, nused, xs, w_gate_up_l, w_down_l, bm):
    nrows, dh = xs.shape
    d = 2 * dh
    nblk = nrows // bm
    grid_spec = pltpu.PrefetchScalarGridSpec(
        num_scalar_prefetch=2,
        grid=(nblk,),
        in_specs=[
            pl.BlockSpec((bm, dh), lambda i, be, nu: (jnp.minimum(i, nu[0] - 1), 0)),
            pl.BlockSpec((None, d, 2 * D_EXPERT), lambda i, be, nu: (be[i], 0, 0)),
            pl.BlockSpec((None, D_EXPERT, d), lambda i, be, nu: (be[i], 0, 0)),
        ],
        out_specs=pl.BlockSpec((bm, dh), lambda i, be, nu: (i, 0)),
    )
    return pl.pallas_call(
        _experts_kernel,
        grid_spec=grid_spec,
        out_shape=jax.ShapeDtypeStruct((nrows, dh), U32),
        compiler_params=_params(("arbitrary",)),
        name="moe_experts",
    )(block_e, nused, xs, w_gate_up_l, w_down_l)


def _combine_kernel(d0_ref, d1_ref, wl_ref, xs_ref, ga2_ref, gf_ref,
                    yb_hbm, o_ref, g_ref, sem, *, tc):
    i = pl.program_id(0)
    n = pl.num_programs(0)
    cur = lax.rem(i, 2)

    def issue(d_ref, buf):
        def body(tg, carry):
            row0 = pl.multiple_of(tg * ROW_GROUP, ROW_GROUP)
            for ts in range(ROW_GROUP):
                for r in range(TOP_K):
                    src = d_ref[tg * (ROW_GROUP * TOP_K) + ts * TOP_K + r]
                    pltpu.make_async_copy(yb_hbm.at[pl.ds(src, 1)],
                                          g_ref.at[buf, r, pl.ds(row0 + ts, 1)],
                                          sem.at[buf]).start(priority=r % 2)
            return carry
        lax.fori_loop(0, tc // ROW_GROUP, body, 0)

    @pl.when(i == 0)
    def _():
        issue(d0_ref, 0)

    for nxt in range(2):
        @pl.when((i + 1 < n) & (cur == 1 - nxt))
        def _():
            issue(d1_ref, nxt)

    pltpu.make_async_copy(g_ref.at[cur], g_ref.at[cur], sem.at[cur]).wait()

    wl = wl_ref[...]
    r_lo = None
    for r in range(TOP_K):
        lo, hi = _unpack_bf16_pair(g_ref[cur, r])
        w = wl[:, r:r + 1]
        r_lo = w * lo if r_lo is None else r_lo + w * lo
        r_hi = w * hi if r == 0 else r_hi + w * hi
    routed = jnp.concatenate([r_lo, r_hi], axis=1)
    x2 = xs_ref[...] + ga2_ref[...] * routed
    ms = jnp.mean(x2 * x2, axis=-1, keepdims=True)
    o_ref[...] = x2 * lax.rsqrt(ms + EPS) * gf_ref[...]


def _combine(dst_flat, wl_t, xs_base, ga2, gfin, yb, tc):
    n, d = xs_base.shape
    nt = n // tc
    smem_cur = pl.BlockSpec((TOP_K * tc,), lambda i: (i,), memory_space=pltpu.SMEM)
    smem_nxt = pl.BlockSpec((TOP_K * tc,), lambda i: (jnp.minimum(i + 1, nt - 1),),
                            memory_space=pltpu.SMEM)
    return pl.pallas_call(
        functools.partial(_combine_kernel, tc=tc),
        grid=(nt,),
        in_specs=[
            smem_cur, smem_nxt,
            pl.BlockSpec((tc, TOP_K), lambda i: (i, 0)),
            pl.BlockSpec((tc, d), lambda i: (i, 0)),
            _mod_spec(ga2, tc),
            pl.BlockSpec((1, d), lambda i: (0, 0)),
            pl.BlockSpec(memory_space=pl.ANY),
        ],
        out_specs=pl.BlockSpec((tc, d), lambda i: (i, 0)),
        out_shape=jax.ShapeDtypeStruct((n, d), F32),
        scratch_shapes=[
            pltpu.VMEM((2, TOP_K, tc, d // 2), U32),
            pltpu.SemaphoreType.DMA((2,)),
        ],
        compiler_params=_params(("arbitrary",)),
        name="moe_combine",
    )(dst_flat, dst_flat, wl_t, xs_base, ga2, gfin, yb)


def _moe_and_final(x, ohg, oda, mods, wts, tm, td, tc, bm):
    n, d = x.shape
    ga1, sh2, sc2, ga2 = mods
    (w_out_b, g_ffn, wsgu_b, wsd_b, wr_t, rb, w_gate_up_l, w_down_l, g_final) = wts
    xs_base, h2, eidx, slot, wl, cnt = _post(
        x, ohg, oda, w_out_b, ga1, g_ffn, sc2, sh2, ga2, wsgu_b, wsd_b, wr_t, rb, tm)
    counts = cnt[:, 0].astype(I32)
    padded = (counts + bm - 1) // bm * bm
    pend = jnp.cumsum(padded)
    pstart = pend - padded
    nblk = -(-(n * TOP_K) // bm) + N_EXPERTS
    nused = (pend[-1] // bm).astype(I32)
    blk_row = jnp.minimum(jnp.arange(nblk, dtype=I32), nused - 1) * bm
    be = jnp.sum((pend[None, :] <= blk_row[:, None]).astype(I32), axis=1)
    dst_flat = (pstart[eidx] + slot).T.reshape(-1)
    xs = _dispatch(dst_flat, pstart, pend, h2, nblk * bm, td, bm)
    yb = _experts(be, nused.reshape(1), xs, w_gate_up_l, w_down_l, bm)
    return _combine(dst_flat, wl.T, xs_base, ga2, g_final, yb, tc)


def _expand(mod, reps):
    if mod.shape[0] == 1:
        return mod
    return jnp.repeat(mod, reps, axis=0)


def kernel(x_prompt, x_sample, cache_k, cache_v, state_hgrn, c_prompt, c_sample, w_ada, b_ada,
           norm_mix, norm_ffn, norm_final, w_in, w_out, hg_lb_logits, hg_norm, da_lambda, da_norm,
           rel_bias_table, w_router, router_bias, w_gate_up, w_down, ws_gate_up, ws_down):
    depth = w_in.shape[0]
    assert depth == 1 and hg_lb_logits.shape[0] == 2
    bp, tp, d = x_prompt.shape
    bs, ts, _ = x_sample.shape
    assert bp == 1
    past = cache_k.shape[2]
    l = 0

    rows = -(-(bp + bs) // 8) * 8
    c_all = jnp.zeros((rows, d), F32).at[:bp].set(c_prompt).at[bp:bp + bs].set(c_sample)
    mod = _adaln(c_all, w_ada[l], b_ada[l])
    mod_p = [mod[0:bp, j * d:(j + 1) * d] for j in range(6)]
    mod_s = [_expand(mod[bp:bp + bs, j * d:(j + 1) * d], ts) for j in range(6)]

    w_in_b = w_in[l].astype(BF16)
    w_out_b = w_out[l].astype(BF16)
    wsgu_b = ws_gate_up[l].astype(BF16)
    wsd_b = ws_down[l].astype(BF16)
    wr_t = w_router[l].T
    g_mix = norm_mix[l].reshape(1, d)
    g_ffn = norm_ffn[l].reshape(1, d)
    g_final = norm_final.reshape(1, d)
    moe_w = (w_out_b, g_ffn, wsgu_b, wsd_b, wr_t, router_bias[l], w_gate_up[l], w_down[l], g_final)

    lam = _lam(da_lambda[l])

    t_att = min(ATT_TILE, tp)
    kk = jnp.arange(t_att, dtype=I32)[:, None]
    qq = jnp.arange(t_att, dtype=I32)[None, :]
    idx_diag = jnp.where((kk // CHUNK) <= (qq // CHUNK), _rel_bucket(kk - qq), MASK_BUCKET)
    idx_prev = _rel_bucket(kk - qq - t_att)
    bias_p = _bias_tiles(rel_bias_table, jnp.stack([idx_diag, idx_prev]).astype(I32),
                         shift_bucket=N_BUCKETS // 2 - 1)
    pad = 128
    qpos = past + jnp.arange(ts, dtype=I32)[:, None]
    idx_sp = _rel_bucket(jnp.arange(past, dtype=I32)[None, :] - qpos)
    kn = jnp.arange(pad, dtype=I32)[None, :]
    idx_sn = jnp.where(kn < ts, _rel_bucket(past + kn - qpos), MASK_BUCKET)
    bias_sp = _bias_tiles(rel_bias_table, idx_sp[None].astype(I32), shift_bucket=None)
    bias_sn = _bias_tiles(rel_bias_table, idx_sn[None].astype(I32), shift_bucket=None)

    xp = x_prompt.reshape(bp * tp, d)
    sh1, sc1, ga1, sh2, sc2, ga2 = mod_p
    zh, qs, kf, vf, kb, vb = _inproj(xp, g_mix, sc1, sh1, w_in_b, min(INPROJ_TILE, tp))
    s_zero = jnp.zeros((bp, HG_HEADS, HG_DIM, HG_DIM), F32)
    ohg_p, sp_new = _hgrn(zh, s_zero, hg_lb_logits, hg_norm[l], bp, tp, min(HGRN_CHUNK, tp))
    vt = vb.reshape(tp // t_att, t_att, DA_HEADS, DA_VDIM).transpose(2, 0, 3, 1)
    oda_p = _attn_prompt(kb, qs.T, vt, bias_p, lam, da_norm[l], t_att)
    y_p = _moe_and_final(xp, ohg_p, oda_p, (ga1, sh2, sc2, ga2), moe_w,
                         min(POST_TILE, tp), min(DISPATCH_TILE, tp), min(COMBINE_TILE, tp),
                         MOE_BLOCK_ROWS)
    k_prompt = kf.reshape(1, bp, tp, DA_HEADS, 2 * DA_QKDIM)
    v_prompt = vf.reshape(1, bp, tp, DA_HEADS, DA_VDIM)

    ns = bs * ts
    xs_ = x_sample.reshape(ns, d)
    sh1, sc1, ga1, sh2, sc2, ga2 = mod_s
    zh, qs, kf, vf, kb, vb = _inproj(xs_, g_mix, sc1, sh1, w_in_b, ns)
    ohg_s, ss_new = _hgrn(zh, state_hgrn[l], hg_lb_logits, hg_norm[l], bs, ts, ts)
    oda_s = _attn_step(qs, cache_k[l], cache_v[l], kb, vb, bias_sp, bias_sn, lam, da_norm[l], bs, ts)
    y_s = _moe_and_final(xs_, ohg_s, oda_s, (ga1, sh2, sc2, ga2), moe_w,
                         ns, ns, min(COMBINE_TILE, ns), 128)
    k_sample = kf.reshape(1, bs, ts, DA_HEADS, 2 * DA_QKDIM)
    v_sample = vf.reshape(1, bs, ts, DA_HEADS, DA_VDIM)

    return (y_p.reshape(bp, tp, d), y_s.reshape(bs, ts, d), k_prompt, v_prompt, sp_new[None],
            k_sample, v_sample, ss_new[None].astype(x_sample.dtype))
```

```python
import functools
import math

import numpy as np
import jax
import jax.numpy as jnp
from jax import lax
from jax.experimental import pallas as pl
from jax.experimental.pallas import tpu as pltpu

F32 = jnp.float32
BF16 = jnp.bfloat16
I32 = jnp.int32
U32 = jnp.uint32
HIGHEST = lax.Precision.HIGHEST

D_MODEL = 1024
CHUNK = 64
HG_HEADS = 4
HG_DIM = 128
HG_WIDTH = HG_HEADS * HG_DIM
DA_HEADS = 4
DA_VDIM = 128
DA_QKDIM = 64
DA_WIDTH = DA_HEADS * DA_VDIM
N_BUCKETS = 32
MAX_DIST = 128
N_EXPERTS = 64
TOP_K = 8
N_GROUPS = 8
GROUP_SIZE = N_EXPERTS // N_GROUPS
TOP_GROUPS = 4
D_EXPERT = 256
ROUTE_SCALE = 2.5
EPS = 1e-6
LAM_INIT = 0.8 - 0.6 * math.exp(-0.3 * 0)

LOG2E = math.log2(math.e)
HI_MASK = np.uint32(0xFFFF0000)
NEG_BIG = -1e30
MASK_BUCKET = N_BUCKETS
V7X_VMEM_LIMIT = 48 * 1024 * 1024

ATT_TILE = 512
HGRN_CHUNK = 256
INPROJ_TILE = 512
POST_TILE = 256
DISPATCH_TILE = 512
COMBINE_TILE = 256
MOE_BLOCK_ROWS = 512
ROW_GROUP = 8


def _sigmoid(x):
    return 1.0 / (1.0 + jnp.exp(-x))


def _silu(x):
    return x * _sigmoid(x)


def _dot(a, b, **kw):
    return jnp.dot(a, b, preferred_element_type=F32, **kw)


def _dot_nt(a, b, **kw):
    return lax.dot_general(a, b, (((1,), (1,)), ((), ())), preferred_element_type=F32, **kw)


def _dot_tn(a, b, **kw):
    return lax.dot_general(a, b, (((0,), (0,)), ((), ())), preferred_element_type=F32, **kw)


def _pack_bf16_pair(lo, hi):
    lo_bits = lax.bitcast_convert_type(lo.astype(BF16).astype(F32), U32)
    hi_bits = lax.bitcast_convert_type(hi.astype(BF16).astype(F32), U32)
    return (lo_bits >> 16) | (hi_bits & HI_MASK)


def _unpack_bf16_pair(w):
    lo = lax.bitcast_convert_type(w << 16, F32)
    hi = lax.bitcast_convert_type(w & HI_MASK, F32)
    return lo, hi


def _params(sem, vmem=V7X_VMEM_LIMIT, flags=None):
    return pltpu.CompilerParams(dimension_semantics=sem, vmem_limit_bytes=vmem, flags=flags)


def _adaln_kernel(c_ref, w_ref, b_ref, o_ref):
    s = _silu(c_ref[...])
    o_ref[...] = _dot(s, w_ref[...], precision=HIGHEST) + b_ref[...]


def _adaln(c_all, w_ada, b_ada):
    rows, d = c_all.shape
    cols = w_ada.shape[1]
    blk = 1024
    return pl.pallas_call(
        _adaln_kernel,
        grid=(cols // blk,),
        in_specs=[
            pl.BlockSpec((rows, d), lambda j: (0, 0)),
            pl.BlockSpec((d, blk), lambda j: (0, j)),
            pl.BlockSpec((1, blk), lambda j: (0, j)),
        ],
        out_specs=pl.BlockSpec((rows, blk), lambda j: (0, j)),
        out_shape=jax.ShapeDtypeStruct((rows, cols), F32),
        compiler_params=_params(("parallel",)),
        name="adaln",
    )(c_all, w_ada, b_ada.reshape(1, cols))


def _lam_kernel(l_ref, o_ref):
    l = l_ref[...].astype(F32)
    a = jnp.sum(l[0:1] * l[1:2], axis=-1, keepdims=True)
    b = jnp.sum(l[2:3] * l[3:4], axis=-1, keepdims=True)
    lam = jnp.exp(a) - jnp.exp(b) + LAM_INIT
    o_ref[...] = jnp.broadcast_to(lam, o_ref.shape)


def _lam(da_lambda_l):
    return pl.pallas_call(
        _lam_kernel,
        out_shape=jax.ShapeDtypeStruct((8, 128), F32),
        name="lam",
    )(da_lambda_l)


def _rel_bucket(rel):
    nb = N_BUCKETS // 2
    max_exact = nb // 2
    side = jnp.where(rel > 0, nb, 0)
    n = jnp.abs(rel)
    large = max_exact + (jnp.log(jnp.maximum(n, 1).astype(F32) / max_exact)
                         / math.log(MAX_DIST / max_exact) * (nb - max_exact)).astype(I32)
    large = jnp.minimum(large, nb - 1)
    return side + jnp.where(n < max_exact, n, large)


def _bias_kernel(tab_ref, idx_ref, o_ref, *, shift_bucket):
    h = pl.program_id(0)
    idx = idx_ref[...]
    shift = tab_ref[shift_bucket, h] if shift_bucket is not None else 0.0
    acc = jnp.zeros(idx.shape, F32)
    for j in range(N_BUCKETS):
        acc = jnp.where(idx == j, (tab_ref[j, h] - shift) * LOG2E, acc)
    o_ref[...] = jnp.where(idx == MASK_BUCKET, NEG_BIG, acc)


def _bias_tiles(table, idx, *, shift_bucket):
    k, r, c = idx.shape
    return pl.pallas_call(
        functools.partial(_bias_kernel, shift_bucket=shift_bucket),
        grid=(DA_HEADS, k),
        in_specs=[
            pl.BlockSpec(memory_space=pltpu.SMEM),
            pl.BlockSpec((None, r, c), lambda h, d: (d, 0, 0)),
        ],
        out_specs=pl.BlockSpec((None, None, r, c), lambda h, d: (h, d, 0, 0)),
        out_shape=jax.ShapeDtypeStruct((DA_HEADS, k, r, c), F32),
        compiler_params=_params(("parallel", "parallel")),
        name="rel_bias",
    )(table, idx)


def _inproj_kernel(x_ref, g_ref, sc_ref, sh_ref, w_ref,
                   zh_ref, q_ref, k_ref, v_ref, kb_ref, vb_ref, *, transposed):
    x = x_ref[...]
    ms = jnp.mean(x * x, axis=-1, keepdims=True)
    h = x * lax.rsqrt(ms + EPS) * g_ref[...]
    h = h * (1.0 + sc_ref[...]) + sh_ref[...]
    hb = h.astype(BF16)
    c0 = 4 * HG_WIDTH
    zh_ref[...] = _dot(hb, w_ref[:, 0:c0])
    zq = _dot(hb, w_ref[:, c0:c0 + DA_WIDTH]) * (DA_QKDIM ** -0.5 * LOG2E)
    zk = _dot(hb, w_ref[:, c0 + DA_WIDTH:c0 + 2 * DA_WIDTH])
    k_ref[...] = zk
    kb_ref[...] = zk.astype(BF16)
    zv = _dot(hb, w_ref[:, c0 + 2 * DA_WIDTH:c0 + 3 * DA_WIDTH])
    v_ref[...] = zv
    if transposed:
        q_ref[...] = zq.T.astype(BF16)
        vb_ref[...] = zv.T.astype(BF16).reshape(vb_ref.shape)
    else:
        q_ref[...] = zq.astype(BF16)
        vb_ref[...] = zv.astype(BF16)


def _mod_spec(mod, tm):
    if mod.shape[0] == 1:
        return pl.BlockSpec((1, mod.shape[1]), lambda i: (0, 0))
    return pl.BlockSpec((tm, mod.shape[1]), lambda i: (i, 0))


def _inproj(x, g, sc, sh, w_in_b, tm, transposed):
    n, d = x.shape
    cols = w_in_b.shape[1]
    row = lambda i: (i, 0)
    if transposed:
        q_spec = pl.BlockSpec((DA_WIDTH, tm), lambda i: (0, i))
        q_shape = jax.ShapeDtypeStruct((DA_WIDTH, n), BF16)
        vb_spec = pl.BlockSpec((DA_HEADS, None, DA_VDIM, tm), lambda i: (0, i, 0, 0))
        vb_shape = jax.ShapeDtypeStruct((DA_HEADS, n // tm, DA_VDIM, tm), BF16)
    else:
        q_spec = vb_spec = pl.BlockSpec((tm, DA_WIDTH), row)
        q_shape = vb_shape = jax.ShapeDtypeStruct((n, DA_WIDTH), BF16)
    return pl.pallas_call(
        functools.partial(_inproj_kernel, transposed=transposed),
        grid=(n // tm,),
        in_specs=[
            pl.BlockSpec((tm, d), row),
            pl.BlockSpec((1, d), lambda i: (0, 0)),
            _mod_spec(sc, tm),
            _mod_spec(sh, tm),
            pl.BlockSpec((d, cols), lambda i: (0, 0)),
        ],
        out_specs=[
            pl.BlockSpec((tm, 4 * HG_WIDTH), row),
            q_spec,
            pl.BlockSpec((tm, DA_WIDTH), row),
            pl.BlockSpec((tm, DA_WIDTH), row),
            pl.BlockSpec((tm, DA_WIDTH), row),
            vb_spec,
        ],
        out_shape=[
            jax.ShapeDtypeStruct((n, 4 * HG_WIDTH), F32),
            q_shape,
            jax.ShapeDtypeStruct((n, DA_WIDTH), F32),
            jax.ShapeDtypeStruct((n, DA_WIDTH), F32),
            jax.ShapeDtypeStruct((n, DA_WIDTH), BF16),
            vb_shape,
        ],
        compiler_params=_params(("parallel",)),
        name="inproj",
    )(x, g, sc, sh, w_in_b)


def _hgrn_consts(c):
    levels = int(round(math.log2(c)))
    assert 1 << levels == c
    t = np.arange(c)[:, None]
    r = np.arange(c)[None, :]
    blocks = [r <= t]
    for l in range(levels):
        m = 1 << l
        mid = (t // (2 * m)) * (2 * m) + m - 1
        later = (t & m) != 0
        blocks.append(np.where(later, (r > mid) & (r <= t), (r > t) & (r <= mid)))
    mall = np.concatenate(blocks, axis=0).astype(np.float32)
    x = np.maximum(t ^ r, 1)
    lv = np.where(t == r, -1, np.where(t > r, np.floor(np.log2(x)).astype(np.int64), -2))
    return jnp.asarray(mall, dtype=BF16), jnp.asarray(lv, dtype=I32), levels


def _hgrn_kernel(zh_ref, s0_ref, lbl_ref, gain_ref, mall_ref, lv_ref,
                 o_ref, sout_ref, st_ref, *, c, levels):
    ci = pl.program_id(1)

    @pl.when(ci == 0)
    def _():
        for h in range(HG_HEADS):
            st_ref[h] = s0_ref[h].astype(F32).T

    lbl = lbl_ref[...].astype(F32)
    mx = jnp.maximum(lbl[0:1], lbl[1:2])
    e0 = jnp.exp(lbl[0:1] - mx)
    e1 = jnp.exp(lbl[1:2] - mx)
    lb = e0 / (e0 + e1)

    xq = zh_ref[:, 0:HG_WIDTH]
    xf = zh_ref[:, HG_WIDTH:2 * HG_WIDTH]
    q = _silu(xq)
    y = lb + (1.0 - lb) * _sigmoid(xf)
    logf = jnp.log(y)
    kk = 1.0 - y

    l1 = logf.astype(BF16)
    r1 = logf - l1.astype(F32)
    l2 = r1.astype(BF16)
    l3 = (r1 - l2.astype(F32)).astype(BF16)
    mall = mall_ref[...]
    e_all = _dot(mall, l1) + _dot(mall, l2) + _dot(mall, l3)

    lv = lv_ref[...]
    gain = gain_ref[...].astype(F32)
    for h in range(HG_HEADS):
        sl = slice(h * HG_DIM, (h + 1) * HG_DIM)
        qh = q[:, sl]
        kh = kk[:, sl]
        ih = zh_ref[:, 2 * HG_WIDTH + h * HG_DIM:2 * HG_WIDTH + (h + 1) * HG_DIM]
        gh = zh_ref[:, 3 * HG_WIDTH + h * HG_DIM:3 * HG_WIDTH + (h + 1) * HG_DIM]
        bh = e_all[0:c, sl]
        ihb = ih.astype(BF16)
        a = jnp.where(lv == -1, _dot_nt(qh.astype(BF16), kh.astype(BF16)), 0.0)
        for l in range(levels):
            f = jnp.exp(e_all[(l + 1) * c:(l + 2) * c, sl])
            p = _dot_nt((qh * f).astype(BF16), (kh * f).astype(BF16))
            a = a + jnp.where(lv == l, p, 0.0)
        st = st_ref[h]
        o = _dot(a.astype(BF16), ihb) + _dot_nt((qh * jnp.exp(bh)).astype(BF16), st.astype(BF16))
        bl = bh[c - 1:c, :]
        kd = (kh * jnp.exp(bl - bh)).astype(BF16)
        st_ref[h] = st * jnp.exp(bl) + _dot_tn(ihb, kd)
        ms = jnp.mean(o * o, axis=-1, keepdims=True)
        on = o * lax.rsqrt(ms + EPS) * gain
        o_ref[:, sl] = (on * _silu(gh)).astype(o_ref.dtype)

    @pl.when(ci == pl.num_programs(1) - 1)
    def _():
        for h in range(HG_HEADS):
            sout_ref[h] = st_ref[h].T.astype(sout_ref.dtype)


def _hgrn(zh, s0, lb_logits, gain, batch, seq, c):
    mall, lv, levels = _hgrn_consts(c)
    nc = seq // c
    return pl.pallas_call(
        functools.partial(_hgrn_kernel, c=c, levels=levels),
        grid=(batch, nc),
        in_specs=[
            pl.BlockSpec((c, 4 * HG_WIDTH), lambda b, i: (b * nc + i, 0)),
            pl.BlockSpec((None, HG_HEADS, HG_DIM, HG_DIM), lambda b, i: (b, 0, 0, 0)),
            pl.BlockSpec(lb_logits.shape, lambda b, i: (0, 0)),
            pl.BlockSpec((1, HG_DIM), lambda b, i: (0, 0)),
            pl.BlockSpec(mall.shape, lambda b, i: (0, 0)),
            pl.BlockSpec(lv.shape, lambda b, i: (0, 0)),
        ],
        out_specs=[
            pl.BlockSpec((c, HG_WIDTH), lambda b, i: (b * nc + i, 0)),
            pl.BlockSpec((None, HG_HEADS, HG_DIM, HG_DIM), lambda b, i: (b, 0, 0, 0)),
        ],
        out_shape=[
            jax.ShapeDtypeStruct((batch * seq, HG_WIDTH), BF16),
            jax.ShapeDtypeStruct((batch, HG_HEADS, HG_DIM, HG_DIM), F32),
        ],
        scratch_shapes=[pltpu.VMEM((HG_HEADS, HG_DIM, HG_DIM), F32)],
        compiler_params=_params(("parallel", "arbitrary")),
        name="hgrn2",
    )(zh, s0, lb_logits, gain.reshape(1, HG_DIM), mall, lv)


def _attn_kernel(k_ref, qt_ref, vt_ref, bias_ref, lam_ref, gain_ref,
                 o_ref, qz_ref, m_ref, l_ref, acc_ref, s_ref, *, t):
    i = pl.program_id(1)
    qt = qt_ref[...]
    row = lax.broadcasted_iota(I32, qt.shape, 0)
    zero = jnp.zeros_like(qt)
    qz_ref[:, 0:t] = jnp.where(row < DA_QKDIM, qt, zero)
    qz_ref[:, t:2 * t] = jnp.where(row >= DA_QKDIM, qt, zero)
    m_ref[...] = jnp.full(m_ref.shape, NEG_BIG, F32)
    l_ref[...] = jnp.zeros(l_ref.shape, F32)
    acc_ref[...] = jnp.zeros(acc_ref.shape, F32)

    def scores(j, buf):
        kt = k_ref[pl.ds(pl.multiple_of(j * t, t), t), :]
        s_ref[buf] = _dot(kt, qz_ref[...])

    def consume(j, buf, bias_idx):
        s = s_ref[buf]
        if bias_idx is not None:
            b = bias_ref[bias_idx]
            s = jnp.concatenate([s[:, 0:t] + b, s[:, t:2 * t] + b], axis=1)
        m_prev = m_ref[...]
        m_new = jnp.maximum(m_prev, jnp.max(s, axis=0, keepdims=True))
        alpha = jnp.exp2(m_prev - m_new)
        pr = jnp.exp2(s - m_new)
        l_ref[...] = alpha * l_ref[...] + jnp.sum(pr, axis=0, keepdims=True)
        acc_ref[...] = alpha * acc_ref[...] + _dot(vt_ref[j], pr.astype(BF16))
        m_ref[...] = m_new

    n_far = jnp.maximum(i - 1, 0)

    @pl.when(n_far > 0)
    def _():
        scores(0, 0)

    def far_pair(p, carry):
        j = 2 * p
        scores(j + 1, 1)
        consume(j, 0, None)
        scores(jnp.minimum(j + 2, n_far - 1), 0)
        consume(j + 1, 1, None)
        return carry

    lax.fori_loop(0, n_far // 2, far_pair, 0)

    @pl.when(lax.rem(n_far, 2) == 1)
    def _():
        consume(n_far - 1, 0, None)

    @pl.when(i >= 1)
    def _():
        scores(i - 1, 0)
        scores(i, 1)
        consume(i - 1, 0, 1)
        consume(i, 1, 0)

    @pl.when(i == 0)
    def _():
        scores(i, 1)
        consume(i, 1, 0)

    lam = lam_ref[0:1, 0:1]
    l = l_ref[...]
    acc = acc_ref[...]
    o = acc[:, 0:t] / l[:, 0:t] - lam * (acc[:, t:2 * t] / l[:, t:2 * t])
    ms = jnp.mean(o * o, axis=0, keepdims=True)
    on = o * lax.rsqrt(ms + EPS) * gain_ref[...].astype(F32) * (1.0 - LAM_INIT)
    o_ref[...] = on.T.astype(o_ref.dtype)


def _attn_prompt(kb, qt, vt, bias, lam, gain, t):
    n = kb.shape[0]
    nt = n // t
    return pl.pallas_call(
        functools.partial(_attn_kernel, t=t),
        grid=(DA_HEADS, nt),
        in_specs=[
            pl.BlockSpec((n, DA_VDIM), lambda h, i: (0, h)),
            pl.BlockSpec((DA_VDIM, t), lambda h, i: (h, i)),
            pl.BlockSpec((None, nt, DA_VDIM, t), lambda h, i: (h, 0, 0, 0)),
            pl.BlockSpec((None, 2, t, t), lambda h, i: (h, 0, 0, 0)),
            pl.BlockSpec((8, 128), lambda h, i: (0, 0)),
            pl.BlockSpec((DA_VDIM, 1), lambda h, i: (0, 0)),
        ],
        out_specs=pl.BlockSpec((t, DA_VDIM), lambda h, i: (i, h)),
        out_shape=jax.ShapeDtypeStruct((n, DA_WIDTH), BF16),
        scratch_shapes=[
            pltpu.VMEM((DA_VDIM, 2 * t), BF16),
            pltpu.VMEM((1, 2 * t), F32),
            pltpu.VMEM((1, 2 * t), F32),
            pltpu.VMEM((DA_VDIM, 2 * t), F32),
            pltpu.VMEM((2, t, 2 * t), F32),
        ],
        compiler_params=_params(("parallel", "parallel")),
        name="diff_attn_prompt",
    )(kb, qt, vt, bias, lam, gain.reshape(DA_VDIM, 1))


def _attn_step_kernel(q_ref, kp_ref, vp_ref, kn_ref, vn_ref, bp_ref, bn_ref, lam_ref, gain_ref,
                      o_ref, *, tq, pad):
    q = q_ref[...]
    lane = lax.broadcasted_iota(I32, q.shape, 1)
    zero = jnp.zeros_like(q)
    qz = jnp.concatenate([jnp.where(lane < DA_QKDIM, q, zero),
                          jnp.where(lane >= DA_QKDIM, q, zero)], axis=0)
    kp = kp_ref[...].astype(BF16)
    vp = vp_ref[...].astype(BF16)
    zpad = jnp.zeros((pad - tq, DA_VDIM), BF16)
    kn = jnp.concatenate([kn_ref[...], zpad], axis=0)
    vn = jnp.concatenate([vn_ref[...], zpad], axis=0)
    bp = bp_ref[...]
    bn = bn_ref[...]
    sp = _dot_nt(qz, kp) + jnp.concatenate([bp, bp], axis=0)
    sn = _dot_nt(qz, kn) + jnp.concatenate([bn, bn], axis=0)
    m = jnp.maximum(jnp.max(sp, axis=-1, keepdims=True), jnp.max(sn, axis=-1, keepdims=True))
    pp = jnp.exp2(sp - m)
    pn = jnp.exp2(sn - m)
    l = jnp.sum(pp, axis=-1, keepdims=True) + jnp.sum(pn, axis=-1, keepdims=True)
    acc = _dot(pp.astype(BF16), vp) + _dot(pn.astype(BF16), vn)
    on = acc / l
    lam = lam_ref[0:1, 0:1]
    o = on[0:tq] - lam * on[tq:2 * tq]
    ms = jnp.mean(o * o, axis=-1, keepdims=True)
    o = o * lax.rsqrt(ms + EPS) * gain_ref[...].astype(F32) * (1.0 - LAM_INIT)
    o_ref[...] = o.astype(o_ref.dtype)


def _attn_step(qs, cache_k_l, cache_v_l, kb, vb, bias_p, bias_n, lam, gain, batch, tq):
    past = cache_k_l.shape[1]
    pad = bias_n.shape[-1]
    return pl.pallas_call(
        functools.partial(_attn_step_kernel, tq=tq, pad=pad),
        grid=(batch, DA_HEADS),
        in_specs=[
            pl.BlockSpec((tq, DA_VDIM), lambda b, h: (b, h)),
            pl.BlockSpec((None, past, DA_VDIM), lambda b, h: (b, 0, h)),
            pl.BlockSpec((None, past, DA_VDIM), lambda b, h: (b, 0, h)),
            pl.BlockSpec((tq, DA_VDIM), lambda b, h: (b, h)),
            pl.BlockSpec((tq, DA_VDIM), lambda b, h: (b, h)),
            pl.BlockSpec((None, None, tq, past), lambda b, h: (h, 0, 0, 0)),
            pl.BlockSpec((None, None, tq, pad), lambda b, h: (h, 0, 0, 0)),
            pl.BlockSpec((8, 128), lambda b, h: (0, 0)),
            pl.BlockSpec((1, DA_VDIM), lambda b, h: (0, 0)),
        ],
        out_specs=pl.BlockSpec((tq, DA_VDIM), lambda b, h: (b, h)),
        out_shape=jax.ShapeDtypeStruct((batch * tq, DA_WIDTH), BF16),
        compiler_params=_params(("parallel", "parallel")),
        name="diff_attn_step",
    )(qs, cache_k_l.reshape(batch, past, DA_WIDTH), cache_v_l.reshape(batch, past, DA_WIDTH),
      kb, vb, bias_p, bias_n, lam, gain.reshape(1, DA_VDIM))


def _post_kernel(x_ref, ohg_ref, oda_ref, wout_ref, ga1_ref, g_ref, sc_ref, sh_ref, ga2_ref,
                 wsgu_ref, wsd_ref, wrt_ref, rb_ref, tri_ref,
                 xs_ref, h2_ref, eidx_ref, slot_ref, wl_ref, cnt_ref, carry_ref, *, tm):
    i = pl.program_id(0)

    @pl.when(i == 0)
    def _():
        carry_ref[...] = jnp.zeros(carry_ref.shape, F32)

    mix = _dot(ohg_ref[...], wout_ref[0:HG_WIDTH, :]) + _dot(oda_ref[...], wout_ref[HG_WIDTH:, :])
    x1 = x_ref[...] + ga1_ref[...] * mix
    ms = jnp.mean(x1 * x1, axis=-1, keepdims=True)
    h2 = x1 * lax.rsqrt(ms + EPS) * g_ref[...]
    h2 = h2 * (1.0 + sc_ref[...]) + sh_ref[...]
    h2_ref[...] = _pack_bf16_pair(h2[:, 0:D_MODEL // 2], h2[:, D_MODEL // 2:])
    h2b = h2.astype(BF16)
    gu = _dot(h2b, wsgu_ref[...])
    act = (_silu(gu[:, 0:D_EXPERT]) * gu[:, D_EXPERT:]).astype(BF16)
    xs_ref[...] = x1 + ga2_ref[...] * _dot(act, wsd_ref[...])

    logits = _dot_nt(wrt_ref[...], h2, precision=HIGHEST)
    score = _sigmoid(logits)
    sel = score + rb_ref[...]
    sub = lax.broadcasted_iota(I32, (GROUP_SIZE, tm), 0)
    gscore = []
    for g in range(N_GROUPS):
        v = sel[g * GROUP_SIZE:(g + 1) * GROUP_SIZE, :]
        m1 = jnp.max(v, axis=0, keepdims=True)
        i1 = jnp.min(jnp.where(v == m1, sub, GROUP_SIZE), axis=0, keepdims=True)
        m2 = jnp.max(jnp.where(sub == i1, -jnp.inf, v), axis=0, keepdims=True)
        gscore.append(m1 + m2)
    gsel = []
    for g in range(N_GROUPS):
        ahead = jnp.zeros((1, tm), F32)
        for g2 in range(N_GROUPS):
            if g2 == g:
                continue
            tie = 1.0 if g2 < g else 0.0
            ahead = ahead + jnp.where(gscore[g2] > gscore[g], 1.0,
                                      jnp.where(gscore[g2] == gscore[g], tie, 0.0))
        gsel.append(ahead < TOP_GROUPS)
    selm = jnp.concatenate(
        [jnp.where(gsel[g], sel[g * GROUP_SIZE:(g + 1) * GROUP_SIZE, :], -jnp.inf)
         for g in range(N_GROUPS)], axis=0)
    eio = lax.broadcasted_iota(I32, (N_EXPERTS, tm), 0)
    ahead = jnp.zeros((N_EXPERTS, tm), F32)
    for e2 in range(N_EXPERTS):
        row = selm[e2:e2 + 1, :]
        tie = jnp.where(eio > e2, 1.0, 0.0)
        ahead = ahead + jnp.where(row > selm, 1.0, jnp.where(row == selm, tie, 0.0))
    chosen = jnp.where(selm > -jnp.inf, jnp.where(ahead < TOP_K, 1.0, 0.0), 0.0)
    w = chosen * score
    wn = w / jnp.sum(w, axis=0, keepdims=True) * ROUTE_SCALE

    chb = chosen.astype(BF16)
    before = _dot(chb, tri_ref[...])
    carry = carry_ref[...]
    slot = jnp.concatenate([carry] * (tm // 128), axis=1) + before
    carry_new = carry + _dot(chb, jnp.ones((tm, 128), BF16))
    carry_ref[...] = carry_new
    cnt_ref[...] = carry_new

    eiof = eio.astype(F32)
    for r in range(TOP_K):
        pick = jnp.where(ahead == r, chosen, 0.0)
        eidx_ref[r:r + 1, :] = jnp.sum(pick * eiof, axis=0, keepdims=True).astype(I32)
        slot_ref[r:r + 1, :] = jnp.sum(pick * slot, axis=0, keepdims=True).astype(I32)
        wl_ref[r:r + 1, :] = jnp.sum(pick * wn, axis=0, keepdims=True)


def _post(x, ohg, oda, w_out_b, ga1, g, sc, sh, ga2, wsgu_b, wsd_b, wr_t, rb, tm):
    n, d = x.shape
    tri = jnp.asarray(np.triu(np.ones((tm, tm), np.float32), k=1), dtype=BF16)
    row = lambda i: (i, 0)
    col = lambda i: (0, i)
    full = lambda i: (0, 0)
    return pl.pallas_call(
        functools.partial(_post_kernel, tm=tm),
        grid=(n // tm,),
        in_specs=[
            pl.BlockSpec((tm, d), row),
            pl.BlockSpec((tm, HG_WIDTH), row),
            pl.BlockSpec((tm, DA_WIDTH), row),
            pl.BlockSpec(w_out_b.shape, full),
            _mod_spec(ga1, tm),
            pl.BlockSpec((1, d), full),
            _mod_spec(sc, tm),
            _mod_spec(sh, tm),
            _mod_spec(ga2, tm),
            pl.BlockSpec(wsgu_b.shape, full),
            pl.BlockSpec(wsd_b.shape, full),
            pl.BlockSpec(wr_t.shape, full),
            pl.BlockSpec((N_EXPERTS, 1), full),
            pl.BlockSpec((tm, tm), full),
        ],
        out_specs=[
            pl.BlockSpec((tm, d), row),
            pl.BlockSpec((tm, d // 2), row),
            pl.BlockSpec((TOP_K, tm), col),
            pl.BlockSpec((TOP_K, tm), col),
            pl.BlockSpec((TOP_K, tm), col),
            pl.BlockSpec((N_EXPERTS, 128), full),
        ],
        out_shape=[
            jax.ShapeDtypeStruct((n, d), F32),
            jax.ShapeDtypeStruct((n, d // 2), U32),
            jax.ShapeDtypeStruct((TOP_K, n), I32),
            jax.ShapeDtypeStruct((TOP_K, n), I32),
            jax.ShapeDtypeStruct((TOP_K, n), F32),
            jax.ShapeDtypeStruct((N_EXPERTS, 128), F32),
        ],
        scratch_shapes=[pltpu.VMEM((N_EXPERTS, 128), F32)],
        compiler_params=_params(("arbitrary",)),
        name="post_mix_router",
    )(x, ohg, oda, w_out_b, ga1, g, sc, sh, ga2, wsgu_b, wsd_b, wr_t, rb.reshape(N_EXPERTS, 1), tri)


def _pair_rows_kernel(pstart_ref, eidx_ref, slot_ref, o_ref):
    eidx = eidx_ref[...]
    base = jnp.zeros(eidx.shape, I32)
    for e in range(N_EXPERTS):
        base = jnp.where(eidx == e, pstart_ref[e], base)
    o_ref[...] = base + slot_ref[...]


def _pair_rows(pstart, eidx, slot):
    k, n = eidx.shape
    tn = min(n, 2048)
    return pl.pallas_call(
        _pair_rows_kernel,
        grid=(n // tn,),
        in_specs=[
            pl.BlockSpec(memory_space=pltpu.SMEM),
            pl.BlockSpec((k, tn), lambda i: (0, i)),
            pl.BlockSpec((k, tn), lambda i: (0, i)),
        ],
        out_specs=pl.BlockSpec((k, tn), lambda i: (0, i)),
        out_shape=jax.ShapeDtypeStruct((k, n), I32),
        compiler_params=_params(("parallel",)),
        name="moe_pair_rows",
    )(pstart, eidx, slot)


def _dispatch_kernel(dst_ref, pstart_ref, pend_ref, h2_ref, xs_hbm,
                     zero_ref, zsem, sem, *, td, bm):
    i = pl.program_id(0)
    nblk = xs_hbm.shape[0] // bm

    def zero_copy(e):
        start = pl.multiple_of(pend_ref[e] - bm, bm)
        return pltpu.make_async_copy(zero_ref, xs_hbm.at[pl.ds(start, bm)], zsem)

    def tail_copy(b):
        return pltpu.make_async_copy(zero_ref, xs_hbm.at[pl.ds(pl.multiple_of(b * bm, bm), bm)], zsem)

    @pl.when(i == 0)
    def _():
        zero_ref[...] = jnp.zeros(zero_ref.shape, zero_ref.dtype)
        first_unused = pend_ref[N_EXPERTS - 1] // bm

        def zissue(e, carry):
            @pl.when(pend_ref[e] > pstart_ref[e])
            def _():
                zero_copy(e).start()
            return carry

        def zwait(e, carry):
            @pl.when(pend_ref[e] > pstart_ref[e])
            def _():
                zero_copy(e).wait()
            return carry

        def tissue(b, carry):
            tail_copy(b).start()
            return carry

        def twait(b, carry):
            tail_copy(b).wait()
            return carry

        lax.fori_loop(0, N_EXPERTS, zissue, 0)
        lax.fori_loop(first_unused, nblk, tissue, 0)
        lax.fori_loop(0, N_EXPERTS, zwait, 0)
        lax.fori_loop(first_unused, nblk, twait, 0)

    def issue(tg, carry):
        row0 = pl.multiple_of(tg * ROW_GROUP, ROW_GROUP)
        for ts in range(ROW_GROUP):
            for r in range(TOP_K):
                dst = dst_ref[tg * (ROW_GROUP * TOP_K) + ts * TOP_K + r]
                pltpu.make_async_copy(h2_ref.at[pl.ds(row0 + ts, 1)], xs_hbm.at[pl.ds(dst, 1)],
                                      sem).start(priority=r % 2)
        return carry

    lax.fori_loop(0, td // ROW_GROUP, issue, 0)
    for r in range(TOP_K):
        pltpu.make_async_copy(h2_ref, h2_ref, sem).wait()


def _dispatch(dst_flat, pstart, pend, h2, nrows, td, bm):
    n, d = h2.shape
    return pl.pallas_call(
        functools.partial(_dispatch_kernel, td=td, bm=bm),
        grid=(n // td,),
        in_specs=[
            pl.BlockSpec((TOP_K * td,), lambda i: (i,), memory_space=pltpu.SMEM),
            pl.BlockSpec(memory_space=pltpu.SMEM),
            pl.BlockSpec(memory_space=pltpu.SMEM),
            pl.BlockSpec((td, d), lambda i: (i, 0)),
        ],
        out_specs=pl.BlockSpec(memory_space=pl.ANY),
        out_shape=jax.ShapeDtypeStruct((nrows, d), h2.dtype),
        scratch_shapes=[
            pltpu.VMEM((bm, d), h2.dtype),
            pltpu.SemaphoreType.DMA(()),
            pltpu.SemaphoreType.DMA(()),
        ],
        compiler_params=_params(("arbitrary",)),
        name="moe_dispatch",
    )(dst_flat, pstart, pend, h2)


def _experts_kernel(be_ref, nu_ref, x_ref, wgu_ref, wd_ref, o_ref):
    i = pl.program_id(0)

    @pl.when(i < nu_ref[0])
    def _():
        lo, hi = _unpack_bf16_pair(x_ref[...])
        x = jnp.concatenate([lo.astype(BF16), hi.astype(BF16)], axis=1)
        gu = _dot(x, wgu_ref[...].astype(BF16))
        act = (_silu(gu[:, 0:D_EXPERT]) * gu[:, D_EXPERT:]).astype(BF16)
        y = _dot(act, wd_ref[...].astype(BF16))
        o_ref[...] = _pack_bf16_pair(y[:, 0:D_MODEL // 2], y[:, D_MODEL // 2:])

    @pl.when(i >= nu_ref[0])
    def _():
        o_ref[...] = jnp.zeros(o_ref.shape, o_ref.dtype)


def _experts(block_e, nused, xs, w_gate_up_l, w_down_l, bm):
    nrows, dh = xs.shape
    d = 2 * dh
    nblk = nrows // bm
    grid_spec = pltpu.PrefetchScalarGridSpec(
        num_scalar_prefetch=2,
        grid=(nblk,),
        in_specs=[
            pl.BlockSpec((bm, dh), lambda i, be, nu: (jnp.minimum(i, nu[0] - 1), 0)),
            pl.BlockSpec((None, d, 2 * D_EXPERT), lambda i, be, nu: (be[i], 0, 0)),
            pl.BlockSpec((None, D_EXPERT, d), lambda i, be, nu: (be[i], 0, 0)),
        ],
        out_specs=pl.BlockSpec((bm, dh), lambda i, be, nu: (i, 0)),
    )
    return pl.pallas_call(
        _experts_kernel,
        grid_spec=grid_spec,
        out_shape=jax.ShapeDtypeStruct((nrows, dh), U32),
        compiler_params=_params(("arbitrary",)),
        name="moe_experts",
    )(block_e, nused, xs, w_gate_up_l, w_down_l)


def _combine_kernel(d0_ref, d1_ref, wl_ref, xs_ref, ga2_ref, gf_ref,
                    yb_hbm, o_ref, g_ref, sem, *, tc):
    i = pl.program_id(0)
    n = pl.num_programs(0)
    cur = lax.rem(i, 2)

    def issue(d_ref, buf):
        def body(tg, carry):
            row0 = pl.multiple_of(tg * ROW_GROUP, ROW_GROUP)
            for ts in range(ROW_GROUP):
                for r in range(TOP_K):
                    src = d_ref[tg * (ROW_GROUP * TOP_K) + ts * TOP_K + r]
                    pltpu.make_async_copy(yb_hbm.at[pl.ds(src, 1)],
                                          g_ref.at[buf, r, pl.ds(row0 + ts, 1)],
                                          sem.at[buf]).start(priority=r % 2)
            return carry
        lax.fori_loop(0, tc // ROW_GROUP, body, 0)

    @pl.when(i == 0)
    def _():
        issue(d0_ref, 0)

    for nxt in range(2):
        @pl.when((i + 1 < n) & (cur == 1 - nxt))
        def _():
            issue(d1_ref, nxt)

    pltpu.make_async_copy(g_ref.at[cur], g_ref.at[cur], sem.at[cur]).wait()

    wl = wl_ref[...]
    r_lo = None
    for r in range(TOP_K):
        lo, hi = _unpack_bf16_pair(g_ref[cur, r])
        w = wl[:, r:r + 1]
        r_lo = w * lo if r_lo is None else r_lo + w * lo
        r_hi = w * hi if r == 0 else r_hi + w * hi
    routed = jnp.concatenate([r_lo, r_hi], axis=1)
    x2 = xs_ref[...] + ga2_ref[...] * routed
    ms = jnp.mean(x2 * x2, axis=-1, keepdims=True)
    o_ref[...] = x2 * lax.rsqrt(ms + EPS) * gf_ref[...]


def _combine(dst_flat, wl_t, xs_base, ga2, gfin, yb, tc):
    n, d = xs_base.shape
    nt = n // tc
    smem_cur = pl.BlockSpec((TOP_K * tc,), lambda i: (i,), memory_space=pltpu.SMEM)
    smem_nxt = pl.BlockSpec((TOP_K * tc,), lambda i: (jnp.minimum(i + 1, nt - 1),),
                            memory_space=pltpu.SMEM)
    return pl.pallas_call(
        functools.partial(_combine_kernel, tc=tc),
        grid=(nt,),
        in_specs=[
            smem_cur, smem_nxt,
            pl.BlockSpec((tc, TOP_K), lambda i: (i, 0)),
            pl.BlockSpec((tc, d), lambda i: (i, 0)),
            _mod_spec(ga2, tc),
            pl.BlockSpec((1, d), lambda i: (0, 0)),
            pl.BlockSpec(memory_space=pl.ANY),
        ],
        out_specs=pl.BlockSpec((tc, d), lambda i: (i, 0)),
        out_shape=jax.ShapeDtypeStruct((n, d), F32),
        scratch_shapes=[
            pltpu.VMEM((2, TOP_K, tc, d // 2), U32),
            pltpu.SemaphoreType.DMA((2,)),
        ],
        compiler_params=_params(("arbitrary",)),
        name="moe_combine",
    )(dst_flat, dst_flat, wl_t, xs_base, ga2, gfin, yb)


def _moe_and_final(x, ohg, oda, mods, wts, tm, td, tc, bm):
    n, d = x.shape
    ga1, sh2, sc2, ga2 = mods
    (w_out_b, g_ffn, wsgu_b, wsd_b, wr_t, rb, w_gate_up_l, w_down_l, g_final) = wts
    xs_base, h2, eidx, slot, wl, cnt = _post(
        x, ohg, oda, w_out_b, ga1, g_ffn, sc2, sh2, ga2, wsgu_b, wsd_b, wr_t, rb, tm)
    counts = cnt[:, 0].astype(I32)
    padded = (counts + bm - 1) // bm * bm
    pend = jnp.cumsum(padded)
    pstart = pend - padded
    nblk = -(-(n * TOP_K) // bm) + N_EXPERTS
    nused = (pend[-1] // bm).astype(I32)
    blk_row = jnp.minimum(jnp.arange(nblk, dtype=I32), nused - 1) * bm
    be = jnp.sum((pend[None, :] <= blk_row[:, None]).astype(I32), axis=1)
    dst_flat = _pair_rows(pstart, eidx, slot).T.reshape(-1)
    xs = _dispatch(dst_flat, pstart, pend, h2, nblk * bm, td, bm)
    yb = _experts(be, nused.reshape(1), xs, w_gate_up_l, w_down_l, bm)
    return _combine(dst_flat, wl.T, xs_base, ga2, g_final, yb, tc)


def _expand(mod, reps):
    if mod.shape[0] == 1:
        return mod
    return jnp.repeat(mod, reps, axis=0)


def kernel(x_prompt, x_sample, cache_k, cache_v, state_hgrn, c_prompt, c_sample, w_ada, b_ada,
           norm_mix, norm_ffn, norm_final, w_in, w_out, hg_lb_logits, hg_norm, da_lambda, da_norm,
           rel_bias_table, w_router, router_bias, w_gate_up, w_down, ws_gate_up, ws_down):
    depth = w_in.shape[0]
    assert depth == 1 and hg_lb_logits.shape[0] == 2
    bp, tp, d = x_prompt.shape
    bs, ts, _ = x_sample.shape
    assert bp == 1
    past = cache_k.shape[2]
    l = 0

    rows = -(-(bp + bs) // 8) * 8
    c_all = jnp.zeros((rows, d), F32).at[:bp].set(c_prompt).at[bp:bp + bs].set(c_sample)
    mod = _adaln(c_all, w_ada[l], b_ada[l])
    mod_p = [mod[0:bp, j * d:(j + 1) * d] for j in range(6)]
    mod_s = [_expand(mod[bp:bp + bs, j * d:(j + 1) * d], ts) for j in range(6)]

    w_in_b = w_in[l].astype(BF16)
    w_out_b = w_out[l].astype(BF16)
    wsgu_b = ws_gate_up[l].astype(BF16)
    wsd_b = ws_down[l].astype(BF16)
    wr_t = w_router[l].T
    g_mix = norm_mix[l].reshape(1, d)
    g_ffn = norm_ffn[l].reshape(1, d)
    g_final = norm_final.reshape(1, d)
    moe_w = (w_out_b, g_ffn, wsgu_b, wsd_b, wr_t, router_bias[l], w_gate_up[l], w_down[l], g_final)

    lam = _lam(da_lambda[l])

    t_att = min(ATT_TILE, tp)
    kk = jnp.arange(t_att, dtype=I32)[:, None]
    qq = jnp.arange(t_att, dtype=I32)[None, :]
    idx_diag = jnp.where((kk // CHUNK) <= (qq // CHUNK), _rel_bucket(kk - qq), MASK_BUCKET)
    idx_prev = _rel_bucket(kk - qq - t_att)
    bias_p = _bias_tiles(rel_bias_table, jnp.stack([idx_diag, idx_prev]).astype(I32),
                         shift_bucket=N_BUCKETS // 2 - 1)
    pad = 128
    qpos = past + jnp.arange(ts, dtype=I32)[:, None]
    idx_sp = _rel_bucket(jnp.arange(past, dtype=I32)[None, :] - qpos)
    kn = jnp.arange(pad, dtype=I32)[None, :]
    idx_sn = jnp.where(kn < ts, _rel_bucket(past + kn - qpos), MASK_BUCKET)
    bias_sp = _bias_tiles(rel_bias_table, idx_sp[None].astype(I32), shift_bucket=None)
    bias_sn = _bias_tiles(rel_bias_table, idx_sn[None].astype(I32), shift_bucket=None)

    xp = x_prompt.reshape(bp * tp, d)
    sh1, sc1, ga1, sh2, sc2, ga2 = mod_p
    assert ATT_TILE == INPROJ_TILE
    zh, qt, kf, vf, kb, vt = _inproj(xp, g_mix, sc1, sh1, w_in_b, t_att, True)
    s_zero = jnp.zeros((bp, HG_HEADS, HG_DIM, HG_DIM), F32)
    ohg_p, sp_new = _hgrn(zh, s_zero, hg_lb_logits, hg_norm[l], bp, tp, min(HGRN_CHUNK, tp))
    oda_p = _attn_prompt(kb, qt, vt, bias_p, lam, da_norm[l], t_att)
    y_p = _moe_and_final(xp, ohg_p, oda_p, (ga1, sh2, sc2, ga2), moe_w,
                         min(POST_TILE, tp), min(DISPATCH_TILE, tp), min(COMBINE_TILE, tp),
                         MOE_BLOCK_ROWS)
    k_prompt = kf.reshape(1, bp, tp, DA_HEADS, 2 * DA_QKDIM)
    v_prompt = vf.reshape(1, bp, tp, DA_HEADS, DA_VDIM)

    ns = bs * ts
    xs_ = x_sample.reshape(ns, d)
    sh1, sc1, ga1, sh2, sc2, ga2 = mod_s
    zh, qs, kf, vf, kb, vb = _inproj(xs_, g_mix, sc1, sh1, w_in_b, ns, False)
    ohg_s, ss_new = _hgrn(zh, state_hgrn[l], hg_lb_logits, hg_norm[l], bs, ts, ts)
    oda_s = _attn_step(qs, cache_k[l], cache_v[l], kb, vb, bias_sp, bias_sn, lam, da_norm[l], bs, ts)
    y_s = _moe_and_final(xs_, ohg_s, oda_s, (ga1, sh2, sc2, ga2), moe_w,
                         ns, ns, min(COMBINE_TILE, ns), 128)
    k_sample = kf.reshape(1, bs, ts, DA_HEADS, 2 * DA_QKDIM)
    v_sample = vf.reshape(1, bs, ts, DA_HEADS, DA_VDIM)

    return (y_p.reshape(bp, tp, d), y_s.reshape(bs, ts, d), k_prompt, v_prompt, sp_new[None],
            k_sample, v_sample, ss_new[None].astype(x_sample.dtype))
```

```python
import functools
import math

import numpy as np
import jax
import jax.numpy as jnp
from jax import lax
from jax.experimental import pallas as pl
from jax.experimental.pallas import tpu as pltpu

F32 = jnp.float32
BF16 = jnp.bfloat16
I32 = jnp.int32
U32 = jnp.uint32
HIGHEST = lax.Precision.HIGHEST

D_MODEL = 1024
CHUNK = 64
HG_HEADS = 4
HG_DIM = 128
HG_WIDTH = HG_HEADS * HG_DIM
DA_HEADS = 4
DA_VDIM = 128
DA_QKDIM = 64
DA_WIDTH = DA_HEADS * DA_VDIM
N_BUCKETS = 32
MAX_DIST = 128
N_EXPERTS = 64
TOP_K = 8
N_GROUPS = 8
GROUP_SIZE = N_EXPERTS // N_GROUPS
TOP_GROUPS = 4
D_EXPERT = 256
ROUTE_SCALE = 2.5
EPS = 1e-6
LAM_INIT = 0.8 - 0.6 * math.exp(-0.3 * 0)

LOG2E = math.log2(math.e)
HI_MASK = np.uint32(0xFFFF0000)
NEG_BIG = -1e30
MASK_BUCKET = N_BUCKETS
V7X_VMEM_LIMIT = 48 * 1024 * 1024

ATT_TILE = 512
HGRN_CHUNK = 256
INPROJ_TILE = 512
POST_TILE = 256
MOE_BLOCK_ROWS = 512
ROW_GROUP = 8
GROUPED_ROWS = -(-(POST_TILE * TOP_K + N_EXPERTS * (ROW_GROUP - 1)) // 256) * 256
CHUNK_SLOTS = -(-(GROUPED_ROWS // ROW_GROUP) // 128) * 128
CHUNK_EXPERT_SHIFT = 24
CHUNK_UNROLL = 4
WAIT_CHUNKS = 16


def _sigmoid(x):
    return 1.0 / (1.0 + jnp.exp(-x))


def _silu(x):
    return x * _sigmoid(x)


def _dot(a, b, **kw):
    return jnp.dot(a, b, preferred_element_type=F32, **kw)


def _dot_nt(a, b, **kw):
    return lax.dot_general(a, b, (((1,), (1,)), ((), ())), preferred_element_type=F32, **kw)


def _dot_tn(a, b, **kw):
    return lax.dot_general(a, b, (((0,), (0,)), ((), ())), preferred_element_type=F32, **kw)


def _pack_bf16_pair(lo, hi):
    lo_bits = lax.bitcast_convert_type(lo.astype(BF16).astype(F32), U32)
    hi_bits = lax.bitcast_convert_type(hi.astype(BF16).astype(F32), U32)
    return (lo_bits >> 16) | (hi_bits & HI_MASK)


def _pack_exact_bf16_pair(lo, hi):
    return (lax.bitcast_convert_type(lo, U32) >> 16) | (lax.bitcast_convert_type(hi, U32) & HI_MASK)


def _unpack_bf16_pair(w):
    lo = lax.bitcast_convert_type(w << 16, F32)
    hi = lax.bitcast_convert_type(w & HI_MASK, F32)
    return lo, hi


def _params(sem, vmem=V7X_VMEM_LIMIT, flags=None):
    return pltpu.CompilerParams(dimension_semantics=sem, vmem_limit_bytes=vmem, flags=flags)


def _adaln_kernel(c_ref, w_ref, b_ref, o_ref):
    s = _silu(c_ref[...])
    o_ref[...] = _dot(s, w_ref[...], precision=HIGHEST) + b_ref[...]


def _adaln(c_all, w_ada, b_ada):
    rows, d = c_all.shape
    cols = w_ada.shape[1]
    blk = 1024
    return pl.pallas_call(
        _adaln_kernel,
        grid=(cols // blk,),
        in_specs=[
            pl.BlockSpec((rows, d), lambda j: (0, 0)),
            pl.BlockSpec((d, blk), lambda j: (0, j)),
            pl.BlockSpec((1, blk), lambda j: (0, j)),
        ],
        out_specs=pl.BlockSpec((rows, blk), lambda j: (0, j)),
        out_shape=jax.ShapeDtypeStruct((rows, cols), F32),
        compiler_params=_params(("parallel",)),
        name="adaln",
    )(c_all, w_ada, b_ada.reshape(1, cols))


def _lam_kernel(l_ref, o_ref):
    l = l_ref[...].astype(F32)
    a = jnp.sum(l[0:1] * l[1:2], axis=-1, keepdims=True)
    b = jnp.sum(l[2:3] * l[3:4], axis=-1, keepdims=True)
    lam = jnp.exp(a) - jnp.exp(b) + LAM_INIT
    o_ref[...] = jnp.broadcast_to(lam, o_ref.shape)


def _lam(da_lambda_l):
    return pl.pallas_call(
        _lam_kernel,
        out_shape=jax.ShapeDtypeStruct((8, 128), F32),
        name="lam",
    )(da_lambda_l)


def _rel_bucket(rel):
    nb = N_BUCKETS // 2
    max_exact = nb // 2
    side = jnp.where(rel > 0, nb, 0)
    n = jnp.abs(rel)
    large = max_exact + (jnp.log(jnp.maximum(n, 1).astype(F32) / max_exact)
                         / math.log(MAX_DIST / max_exact) * (nb - max_exact)).astype(I32)
    large = jnp.minimum(large, nb - 1)
    return side + jnp.where(n < max_exact, n, large)


def _bias_kernel(tab_ref, idx_ref, o_ref, *, shift_bucket):
    h = pl.program_id(0)
    idx = idx_ref[...]
    shift = tab_ref[shift_bucket, h] if shift_bucket is not None else 0.0
    acc = jnp.zeros(idx.shape, F32)
    for j in range(N_BUCKETS):
        acc = jnp.where(idx == j, (tab_ref[j, h] - shift) * LOG2E, acc)
    o_ref[...] = jnp.where(idx == MASK_BUCKET, NEG_BIG, acc)


def _bias_tiles(table, idx, *, shift_bucket):
    k, r, c = idx.shape
    return pl.pallas_call(
        functools.partial(_bias_kernel, shift_bucket=shift_bucket),
        grid=(DA_HEADS, k),
        in_specs=[
            pl.BlockSpec(memory_space=pltpu.SMEM),
            pl.BlockSpec((None, r, c), lambda h, d: (d, 0, 0)),
        ],
        out_specs=pl.BlockSpec((None, None, r, c), lambda h, d: (h, d, 0, 0)),
        out_shape=jax.ShapeDtypeStruct((DA_HEADS, k, r, c), F32),
        compiler_params=_params(("parallel", "parallel")),
        name="rel_bias",
    )(table, idx)


def _inproj_kernel(x_ref, g_ref, sc_ref, sh_ref, w_ref,
                   zh_ref, q_ref, k_ref, v_ref, kb_ref, vb_ref, *, transposed):
    x = x_ref[...]
    ms = jnp.mean(x * x, axis=-1, keepdims=True)
    h = x * lax.rsqrt(ms + EPS) * g_ref[...]
    h = h * (1.0 + sc_ref[...]) + sh_ref[...]
    hb = h.astype(BF16)
    c0 = 4 * HG_WIDTH
    zh_ref[...] = _dot(hb, w_ref[:, 0:c0])
    zq = _dot(hb, w_ref[:, c0:c0 + DA_WIDTH]) * (DA_QKDIM ** -0.5 * LOG2E)
    zk = _dot(hb, w_ref[:, c0 + DA_WIDTH:c0 + 2 * DA_WIDTH])
    k_ref[...] = zk
    kb_ref[...] = zk.astype(BF16)
    zv = _dot(hb, w_ref[:, c0 + 2 * DA_WIDTH:c0 + 3 * DA_WIDTH])
    v_ref[...] = zv
    if transposed:
        q_ref[...] = zq.T.astype(BF16)
        vb_ref[...] = zv.T.astype(BF16).reshape(vb_ref.shape)
    else:
        q_ref[...] = zq.astype(BF16)
        vb_ref[...] = zv.astype(BF16)


def _mod_spec(mod, tm):
    if mod.shape[0] == 1:
        return pl.BlockSpec((1, mod.shape[1]), lambda i: (0, 0))
    return pl.BlockSpec((tm, mod.shape[1]), lambda i: (i, 0))


def _inproj(x, g, sc, sh, w_in_b, tm, transposed):
    n, d = x.shape
    cols = w_in_b.shape[1]
    row = lambda i: (i, 0)
    if transposed:
        q_spec = pl.BlockSpec((DA_WIDTH, tm), lambda i: (0, i))
        q_shape = jax.ShapeDtypeStruct((DA_WIDTH, n), BF16)
        vb_spec = pl.BlockSpec((DA_HEADS, None, DA_VDIM, tm), lambda i: (0, i, 0, 0))
        vb_shape = jax.ShapeDtypeStruct((DA_HEADS, n // tm, DA_VDIM, tm), BF16)
    else:
        q_spec = vb_spec = pl.BlockSpec((tm, DA_WIDTH), row)
        q_shape = vb_shape = jax.ShapeDtypeStruct((n, DA_WIDTH), BF16)
    return pl.pallas_call(
        functools.partial(_inproj_kernel, transposed=transposed),
        grid=(n // tm,),
        in_specs=[
            pl.BlockSpec((tm, d), row),
            pl.BlockSpec((1, d), lambda i: (0, 0)),
            _mod_spec(sc, tm),
            _mod_spec(sh, tm),
            pl.BlockSpec((d, cols), lambda i: (0, 0)),
        ],
        out_specs=[
            pl.BlockSpec((tm, 4 * HG_WIDTH), row),
            q_spec,
            pl.BlockSpec((tm, DA_WIDTH), row),
            pl.BlockSpec((tm, DA_WIDTH), row),
            pl.BlockSpec((tm, DA_WIDTH), row),
            vb_spec,
        ],
        out_shape=[
            jax.ShapeDtypeStruct((n, 4 * HG_WIDTH), F32),
            q_shape,
            jax.ShapeDtypeStruct((n, DA_WIDTH), F32),
            jax.ShapeDtypeStruct((n, DA_WIDTH), F32),
            jax.ShapeDtypeStruct((n, DA_WIDTH), BF16),
            vb_shape,
        ],
        compiler_params=_params(("parallel",)),
        name="inproj",
    )(x, g, sc, sh, w_in_b)


def _hgrn_consts(c):
    levels = int(round(math.log2(c)))
    assert 1 << levels == c
    t = np.arange(c)[:, None]
    r = np.arange(c)[None, :]
    blocks = [r <= t]
    for l in range(levels):
        m = 1 << l
        mid = (t // (2 * m)) * (2 * m) + m - 1
        later = (t & m) != 0
        blocks.append(np.where(later, (r > mid) & (r <= t), (r > t) & (r <= mid)))
    mall = np.concatenate(blocks, axis=0).astype(np.float32)
    x = np.maximum(t ^ r, 1)
    lv = np.where(t == r, -1, np.where(t > r, np.floor(np.log2(x)).astype(np.int64), -2))
    return jnp.asarray(mall, dtype=BF16), jnp.asarray(lv, dtype=I32), levels


def _hgrn_kernel(zh_ref, s0_ref, lbl_ref, gain_ref, mall_ref, lv_ref,
                 o_ref, sout_ref, st_ref, *, c, levels):
    ci = pl.program_id(1)

    @pl.when(ci == 0)
    def _():
        for h in range(HG_HEADS):
            st_ref[h] = s0_ref[h].astype(F32).T

    lbl = lbl_ref[...].astype(F32)
    mx = jnp.maximum(lbl[0:1], lbl[1:2])
    e0 = jnp.exp(lbl[0:1] - mx)
    e1 = jnp.exp(lbl[1:2] - mx)
    lb = e0 / (e0 + e1)

    xq = zh_ref[:, 0:HG_WIDTH]
    xf = zh_ref[:, HG_WIDTH:2 * HG_WIDTH]
    q = _silu(xq)
    y = lb + (1.0 - lb) * _sigmoid(xf)
    logf = jnp.log(y)
    kk = 1.0 - y

    l1 = logf.astype(BF16)
    r1 = logf - l1.astype(F32)
    l2 = r1.astype(BF16)
    l3 = (r1 - l2.astype(F32)).astype(BF16)
    mall = mall_ref[...]
    e_all = _dot(mall, l1) + _dot(mall, l2) + _dot(mall, l3)

    lv = lv_ref[...]
    gain = gain_ref[...].astype(F32)
    for h in range(HG_HEADS):
        sl = slice(h * HG_DIM, (h + 1) * HG_DIM)
        qh = q[:, sl]
        kh = kk[:, sl]
        ih = zh_ref[:, 2 * HG_WIDTH + h * HG_DIM:2 * HG_WIDTH + (h + 1) * HG_DIM]
        gh = zh_ref[:, 3 * HG_WIDTH + h * HG_DIM:3 * HG_WIDTH + (h + 1) * HG_DIM]
        bh = e_all[0:c, sl]
        ihb = ih.astype(BF16)
        a = jnp.where(lv == -1, _dot_nt(qh.astype(BF16), kh.astype(BF16)), 0.0)
        for l in range(levels):
            f = jnp.exp(e_all[(l + 1) * c:(l + 2) * c, sl])
            p = _dot_nt((qh * f).astype(BF16), (kh * f).astype(BF16))
            a = a + jnp.where(lv == l, p, 0.0)
        st = st_ref[h]
        o = _dot(a.astype(BF16), ihb) + _dot_nt((qh * jnp.exp(bh)).astype(BF16), st.astype(BF16))
        bl = bh[c - 1:c, :]
        kd = (kh * jnp.exp(bl - bh)).astype(BF16)
        st_ref[h] = st * jnp.exp(bl) + _dot_tn(ihb, kd)
        ms = jnp.mean(o * o, axis=-1, keepdims=True)
        on = o * lax.rsqrt(ms + EPS) * gain
        o_ref[:, sl] = (on * _silu(gh)).astype(o_ref.dtype)

    @pl.when(ci == pl.num_programs(1) - 1)
    def _():
        for h in range(HG_HEADS):
            sout_ref[h] = st_ref[h].T.astype(sout_ref.dtype)


def _hgrn(zh, s0, lb_logits, gain, batch, seq, c):
    mall, lv, levels = _hgrn_consts(c)
    nc = seq // c
    return pl.pallas_call(
        functools.partial(_hgrn_kernel, c=c, levels=levels),
        grid=(batch, nc),
        in_specs=[
            pl.BlockSpec((c, 4 * HG_WIDTH), lambda b, i: (b * nc + i, 0)),
            pl.BlockSpec((None, HG_HEADS, HG_DIM, HG_DIM), lambda b, i: (b, 0, 0, 0)),
            pl.BlockSpec(lb_logits.shape, lambda b, i: (0, 0)),
            pl.BlockSpec((1, HG_DIM), lambda b, i: (0, 0)),
            pl.BlockSpec(mall.shape, lambda b, i: (0, 0)),
            pl.BlockSpec(lv.shape, lambda b, i: (0, 0)),
        ],
        out_specs=[
            pl.BlockSpec((c, HG_WIDTH), lambda b, i: (b * nc + i, 0)),
            pl.BlockSpec((None, HG_HEADS, HG_DIM, HG_DIM), lambda b, i: (b, 0, 0, 0)),
        ],
        out_shape=[
            jax.ShapeDtypeStruct((batch * seq, HG_WIDTH), BF16),
            jax.ShapeDtypeStruct((batch, HG_HEADS, HG_DIM, HG_DIM), F32),
        ],
        scratch_shapes=[pltpu.VMEM((HG_HEADS, HG_DIM, HG_DIM), F32)],
        compiler_params=_params(("parallel", "arbitrary")),
        name="hgrn2",
    )(zh, s0, lb_logits, gain.reshape(1, HG_DIM), mall, lv)


def _attn_kernel(k_ref, qt_ref, vt_ref, bias_ref, lam_ref, gain_ref,
                 o_ref, qz_ref, m_ref, l_ref, acc_ref, s_ref, *, t):
    i = pl.program_id(1)
    qt = qt_ref[...]
    row = lax.broadcasted_iota(I32, qt.shape, 0)
    zero = jnp.zeros_like(qt)
    qz_ref[:, 0:t] = jnp.where(row < DA_QKDIM, qt, zero)
    qz_ref[:, t:2 * t] = jnp.where(row >= DA_QKDIM, qt, zero)
    m_ref[...] = jnp.full(m_ref.shape, NEG_BIG, F32)
    l_ref[...] = jnp.zeros(l_ref.shape, F32)
    acc_ref[...] = jnp.zeros(acc_ref.shape, F32)

    def scores(j, buf):
        kt = k_ref[pl.ds(pl.multiple_of(j * t, t), t), :]
        s_ref[buf] = _dot(kt, qz_ref[...])

    def consume(j, buf, bias_idx):
        s = s_ref[buf]
        if bias_idx is not None:
            b = bias_ref[bias_idx]
            s = jnp.concatenate([s[:, 0:t] + b, s[:, t:2 * t] + b], axis=1)
        m_prev = m_ref[...]
        m_new = jnp.maximum(m_prev, jnp.max(s, axis=0, keepdims=True))
        alpha = jnp.exp2(m_prev - m_new)
        pr = jnp.exp2(s - m_new)
        l_ref[...] = alpha * l_ref[...] + jnp.sum(pr, axis=0, keepdims=True)
        acc_ref[...] = alpha * acc_ref[...] + _dot(vt_ref[j], pr.astype(BF16))
        m_ref[...] = m_new

    n_far = jnp.maximum(i - 1, 0)

    @pl.when(n_far > 0)
    def _():
        scores(0, 0)

    def far_pair(p, carry):
        j = 2 * p
        scores(j + 1, 1)
        consume(j, 0, None)
        scores(jnp.minimum(j + 2, n_far - 1), 0)
        consume(j + 1, 1, None)
        return carry

    lax.fori_loop(0, n_far // 2, far_pair, 0)

    @pl.when(lax.rem(n_far, 2) == 1)
    def _():
        consume(n_far - 1, 0, None)

    @pl.when(i >= 1)
    def _():
        scores(i - 1, 0)
        scores(i, 1)
        consume(i - 1, 0, 1)
        consume(i, 1, 0)

    @pl.when(i == 0)
    def _():
        scores(i, 1)
        consume(i, 1, 0)

    lam = lam_ref[0:1, 0:1]
    l = l_ref[...]
    acc = acc_ref[...]
    o = acc[:, 0:t] / l[:, 0:t] - lam * (acc[:, t:2 * t] / l[:, t:2 * t])
    ms = jnp.mean(o * o, axis=0, keepdims=True)
    on = o * lax.rsqrt(ms + EPS) * gain_ref[...].astype(F32) * (1.0 - LAM_INIT)
    o_ref[...] = on.T.astype(o_ref.dtype)


def _attn_prompt(kb, qt, vt, bias, lam, gain, t):
    n = kb.shape[0]
    nt = n // t
    return pl.pallas_call(
        functools.partial(_attn_kernel, t=t),
        grid=(DA_HEADS, nt),
        in_specs=[
            pl.BlockSpec((n, DA_VDIM), lambda h, i: (0, h)),
            pl.BlockSpec((DA_VDIM, t), lambda h, i: (h, i)),
            pl.BlockSpec((None, nt, DA_VDIM, t), lambda h, i: (h, 0, 0, 0)),
            pl.BlockSpec((None, 2, t, t), lambda h, i: (h, 0, 0, 0)),
            pl.BlockSpec((8, 128), lambda h, i: (0, 0)),
            pl.BlockSpec((DA_VDIM, 1), lambda h, i: (0, 0)),
        ],
        out_specs=pl.BlockSpec((t, DA_VDIM), lambda h, i: (i, h)),
        out_shape=jax.ShapeDtypeStruct((n, DA_WIDTH), BF16),
        scratch_shapes=[
            pltpu.VMEM((DA_VDIM, 2 * t), BF16),
            pltpu.VMEM((1, 2 * t), F32),
            pltpu.VMEM((1, 2 * t), F32),
            pltpu.VMEM((DA_VDIM, 2 * t), F32),
            pltpu.VMEM((2, t, 2 * t), F32),
        ],
        compiler_params=_params(("parallel", "parallel")),
        name="diff_attn_prompt",
    )(kb, qt, vt, bias, lam, gain.reshape(DA_VDIM, 1))


def _attn_step_kernel(q_ref, kp_ref, vp_ref, kn_ref, vn_ref, bp_ref, bn_ref, lam_ref, gain_ref,
                      o_ref, *, tq, pad):
    q = q_ref[...]
    lane = lax.broadcasted_iota(I32, q.shape, 1)
    zero = jnp.zeros_like(q)
    qz = jnp.concatenate([jnp.where(lane < DA_QKDIM, q, zero),
                          jnp.where(lane >= DA_QKDIM, q, zero)], axis=0)
    kp = kp_ref[...].astype(BF16)
    vp = vp_ref[...].astype(BF16)
    zpad = jnp.zeros((pad - tq, DA_VDIM), BF16)
    kn = jnp.concatenate([kn_ref[...], zpad], axis=0)
    vn = jnp.concatenate([vn_ref[...], zpad], axis=0)
    bp = bp_ref[...]
    bn = bn_ref[...]
    sp = _dot_nt(qz, kp) + jnp.concatenate([bp, bp], axis=0)
    sn = _dot_nt(qz, kn) + jnp.concatenate([bn, bn], axis=0)
    m = jnp.maximum(jnp.max(sp, axis=-1, keepdims=True), jnp.max(sn, axis=-1, keepdims=True))
    pp = jnp.exp2(sp - m)
    pn = jnp.exp2(sn - m)
    l = jnp.sum(pp, axis=-1, keepdims=True) + jnp.sum(pn, axis=-1, keepdims=True)
    acc = _dot(pp.astype(BF16), vp) + _dot(pn.astype(BF16), vn)
    on = acc / l
    lam = lam_ref[0:1, 0:1]
    o = on[0:tq] - lam * on[tq:2 * tq]
    ms = jnp.mean(o * o, axis=-1, keepdims=True)
    o = o * lax.rsqrt(ms + EPS) * gain_ref[...].astype(F32) * (1.0 - LAM_INIT)
    o_ref[...] = o.astype(o_ref.dtype)


def _attn_step(qs, cache_k_l, cache_v_l, kb, vb, bias_p, bias_n, lam, gain, batch, tq):
    past = cache_k_l.shape[1]
    pad = bias_n.shape[-1]
    return pl.pallas_call(
        functools.partial(_attn_step_kernel, tq=tq, pad=pad),
        grid=(batch, DA_HEADS),
        in_specs=[
            pl.BlockSpec((tq, DA_VDIM), lambda b, h: (b, h)),
            pl.BlockSpec((None, past, DA_VDIM), lambda b, h: (b, 0, h)),
            pl.BlockSpec((None, past, DA_VDIM), lambda b, h: (b, 0, h)),
            pl.BlockSpec((tq, DA_VDIM), lambda b, h: (b, h)),
            pl.BlockSpec((tq, DA_VDIM), lambda b, h: (b, h)),
            pl.BlockSpec((None, None, tq, past), lambda b, h: (h, 0, 0, 0)),
            pl.BlockSpec((None, None, tq, pad), lambda b, h: (h, 0, 0, 0)),
            pl.BlockSpec((8, 128), lambda b, h: (0, 0)),
            pl.BlockSpec((1, DA_VDIM), lambda b, h: (0, 0)),
        ],
        out_specs=pl.BlockSpec((tq, DA_VDIM), lambda b, h: (b, h)),
        out_shape=jax.ShapeDtypeStruct((batch * tq, DA_WIDTH), BF16),
        compiler_params=_params(("parallel", "parallel")),
        name="diff_attn_step",
    )(qs, cache_k_l.reshape(batch, past, DA_WIDTH), cache_v_l.reshape(batch, past, DA_WIDTH),
      kb, vb, bias_p, bias_n, lam, gain.reshape(1, DA_VDIM))


def _post_kernel(x_ref, ohg_ref, oda_ref, wout_ref, ga1_ref, g_ref, sc_ref, sh_ref, ga2_ref,
                 wsgu_ref, wsd_ref, wrt_ref, rb_ref, tri_ref, ltri_ref,
                 xs_ref, h2_ref, pos_ref, wl_ref, chunk_ref, nch_ref, cnt_ref, carry_ref, *, tm):
    i = pl.program_id(0)

    @pl.when(i == 0)
    def _():
        carry_ref[...] = jnp.zeros(carry_ref.shape, F32)

    mix = _dot(ohg_ref[...], wout_ref[0:HG_WIDTH, :]) + _dot(oda_ref[...], wout_ref[HG_WIDTH:, :])
    x1 = x_ref[...] + ga1_ref[...] * mix
    ms = jnp.mean(x1 * x1, axis=-1, keepdims=True)
    h2 = x1 * lax.rsqrt(ms + EPS) * g_ref[...]
    h2 = h2 * (1.0 + sc_ref[...]) + sh_ref[...]
    h2b = h2.astype(BF16)
    h2_ref[...] = h2b
    gu = _dot(h2b, wsgu_ref[...])
    act = (_silu(gu[:, 0:D_EXPERT]) * gu[:, D_EXPERT:]).astype(BF16)
    xs_ref[...] = x1 + ga2_ref[...] * _dot(act, wsd_ref[...])

    logits = _dot_nt(wrt_ref[...], h2, precision=HIGHEST)
    score = _sigmoid(logits)
    sel = score + rb_ref[...]
    sub = lax.broadcasted_iota(I32, (GROUP_SIZE, tm), 0)
    gscore = []
    for g in range(N_GROUPS):
        v = sel[g * GROUP_SIZE:(g + 1) * GROUP_SIZE, :]
        m1 = jnp.max(v, axis=0, keepdims=True)
        i1 = jnp.min(jnp.where(v == m1, sub, GROUP_SIZE), axis=0, keepdims=True)
        m2 = jnp.max(jnp.where(sub == i1, -jnp.inf, v), axis=0, keepdims=True)
        gscore.append(m1 + m2)
    gsel = []
    for g in range(N_GROUPS):
        ahead = jnp.zeros((1, tm), F32)
        for g2 in range(N_GROUPS):
            if g2 == g:
                continue
            tie = 1.0 if g2 < g else 0.0
            ahead = ahead + jnp.where(gscore[g2] > gscore[g], 1.0,
                                      jnp.where(gscore[g2] == gscore[g], tie, 0.0))
        gsel.append(ahead < TOP_GROUPS)
    selm = jnp.concatenate(
        [jnp.where(gsel[g], sel[g * GROUP_SIZE:(g + 1) * GROUP_SIZE, :], -jnp.inf)
         for g in range(N_GROUPS)], axis=0)
    eio = lax.broadcasted_iota(I32, (N_EXPERTS, tm), 0)
    ahead = jnp.zeros((N_EXPERTS, tm), F32)
    for e2 in range(N_EXPERTS):
        row = selm[e2:e2 + 1, :]
        tie = jnp.where(eio > e2, 1.0, 0.0)
        ahead = ahead + jnp.where(row > selm, 1.0, jnp.where(row == selm, tie, 0.0))
    chosen = jnp.where(selm > -jnp.inf, jnp.where(ahead < TOP_K, 1.0, 0.0), 0.0)
    w = chosen * score
    wn = w / jnp.sum(w, axis=0, keepdims=True) * ROUTE_SCALE

    chb = chosen.astype(BF16)
    before = _dot(chb, tri_ref[...])
    tot = _dot(chb, jnp.ones((tm, 128), BF16))
    run = jnp.floor((tot + (ROW_GROUP - 1)) * (1.0 / ROW_GROUP)) * ROW_GROUP
    tile_base = _dot(ltri_ref[...], run.astype(BF16))
    carry = carry_ref[...]
    carry_ref[...] = carry + run
    cnt_ref[...] = carry + run
    pos = jnp.concatenate([tile_base] * (tm // 128), axis=1) + before

    widen = lambda v: jnp.concatenate([v] * (CHUNK_SLOTS // 128), axis=1)
    crow = lax.broadcasted_iota(I32, (N_EXPERTS, CHUNK_SLOTS), 1).astype(F32) * ROW_GROUP
    erow = lax.broadcasted_iota(I32, (N_EXPERTS, CHUNK_SLOTS), 0).astype(F32)
    owner = jnp.sum(jnp.where(widen(tile_base + run) <= crow, 1.0, 0.0), axis=0, keepdims=True)
    region_row = jnp.sum(jnp.where(owner == erow, widen(carry - tile_base), 0.0),
                         axis=0, keepdims=True) + crow[0:1]
    chunk_ref[...] = owner.astype(I32) * (1 << CHUNK_EXPERT_SHIFT) + region_row.astype(I32)
    nch_ref[...] = jnp.sum(run * (1.0 / ROW_GROUP), axis=0, keepdims=True).astype(I32)

    for r in range(TOP_K):
        pick = jnp.where(ahead == r, chosen, 0.0)
        pos_ref[r:r + 1, :] = jnp.sum(pick * pos, axis=0, keepdims=True).astype(I32)
        wl_ref[r:r + 1, :] = jnp.sum(pick * wn, axis=0, keepdims=True)


def _post(x, ohg, oda, w_out_b, ga1, g, sc, sh, ga2, wsgu_b, wsd_b, wr_t, rb, tm):
    n, d = x.shape
    nt = n // tm
    tri = jnp.asarray(np.triu(np.ones((tm, tm), np.float32), k=1), dtype=BF16)
    ltri = jnp.asarray(np.tril(np.ones((N_EXPERTS, N_EXPERTS), np.float32), k=-1), dtype=BF16)
    row = lambda i: (i, 0)
    col = lambda i: (0, i)
    full = lambda i: (0, 0)
    return pl.pallas_call(
        functools.partial(_post_kernel, tm=tm),
        grid=(nt,),
        in_specs=[
            pl.BlockSpec((tm, d), row),
            pl.BlockSpec((tm, HG_WIDTH), row),
            pl.BlockSpec((tm, DA_WIDTH), row),
            pl.BlockSpec(w_out_b.shape, full),
            _mod_spec(ga1, tm),
            pl.BlockSpec((1, d), full),
            _mod_spec(sc, tm),
            _mod_spec(sh, tm),
            _mod_spec(ga2, tm),
            pl.BlockSpec(wsgu_b.shape, full),
            pl.BlockSpec(wsd_b.shape, full),
            pl.BlockSpec(wr_t.shape, full),
            pl.BlockSpec((N_EXPERTS, 1), full),
            pl.BlockSpec((tm, tm), full),
            pl.BlockSpec((N_EXPERTS, N_EXPERTS), full),
        ],
        out_specs=[
            pl.BlockSpec((tm, d), row),
            pl.BlockSpec((tm, d), row),
            pl.BlockSpec((TOP_K, tm), col),
            pl.BlockSpec((TOP_K, tm), col),
            pl.BlockSpec((None, 1, CHUNK_SLOTS), lambda i: (i, 0, 0)),
            pl.BlockSpec((None, 1, 128), lambda i: (i, 0, 0)),
            pl.BlockSpec((N_EXPERTS, 128), full),
        ],
        out_shape=[
            jax.ShapeDtypeStruct((n, d), F32),
            jax.ShapeDtypeStruct((n, d), BF16),
            jax.ShapeDtypeStruct((TOP_K, n), I32),
            jax.ShapeDtypeStruct((TOP_K, n), F32),
            jax.ShapeDtypeStruct((nt, 1, CHUNK_SLOTS), I32),
            jax.ShapeDtypeStruct((nt, 1, 128), I32),
            jax.ShapeDtypeStruct((N_EXPERTS, 128), F32),
        ],
        scratch_shapes=[pltpu.VMEM((N_EXPERTS, 128), F32)],
        compiler_params=_params(("arbitrary",)),
        name="post_mix_router",
    )(x, ohg, oda, w_out_b, ga1, g, sc, sh, ga2, wsgu_b, wsd_b, wr_t, rb.reshape(N_EXPERTS, 1),
      tri, ltri)


def _start_chunks(tile, chunk_ref, nch_ref, pstart_ref, make_copy):
    n = nch_ref[tile]

    def start(c):
        word = chunk_ref[tile * CHUNK_SLOTS + c]
        expert = lax.shift_right_logical(word, CHUNK_EXPERT_SHIFT)
        region_row = word & ((1 << CHUNK_EXPERT_SHIFT) - 1)
        make_copy(pl.multiple_of(c * ROW_GROUP, ROW_GROUP),
                  pl.multiple_of(pstart_ref[expert] + region_row, ROW_GROUP), ROW_GROUP).start()

    def group(g, carry):
        for u in range(CHUNK_UNROLL):
            start(g * CHUNK_UNROLL + u)
        return carry

    def single(c, carry):
        start(c)
        return carry

    groups = n // CHUNK_UNROLL
    lax.fori_loop(0, groups, group, 0)
    lax.fori_loop(groups * CHUNK_UNROLL, n, single, 0)


def _wait_chunks(tile, nch_ref, make_copy):
    n = nch_ref[tile]
    many = n // WAIT_CHUNKS

    def wait_many(j, carry):
        make_copy(0, 0, WAIT_CHUNKS * ROW_GROUP).wait()
        return carry

    def wait_one(j, carry):
        make_copy(0, 0, ROW_GROUP).wait()
        return carry

    lax.fori_loop(0, many, wait_many, 0)
    lax.fori_loop(many * WAIT_CHUNKS, n, wait_one, 0)


def _dispatch_kernel(chunk_ref, nch_ref, pstart_ref, pend_ref, pos_ref, h2_ref, xs_hbm,
                     cbuf_ref, zero_ref, zsem, sem, *, tm, bm):
    i = pl.program_id(0)
    nt = pl.num_programs(0)
    nblk = xs_hbm.shape[0] // bm
    dh = cbuf_ref.shape[-1]

    def zero_copy(e):
        start = pl.multiple_of(pend_ref[e] - bm, bm)
        return pltpu.make_async_copy(zero_ref, xs_hbm.at[pl.ds(start, bm)], zsem)

    def tail_copy(b):
        return pltpu.make_async_copy(zero_ref, xs_hbm.at[pl.ds(pl.multiple_of(b * bm, bm), bm)], zsem)

    @pl.when(i == 0)
    def _():
        zero_ref[...] = jnp.zeros(zero_ref.shape, zero_ref.dtype)
        first_unused = pend_ref[N_EXPERTS - 1] // bm

        def zissue(e, carry):
            @pl.when(pend_ref[e] > pstart_ref[e])
            def _():
                zero_copy(e).start()
            return carry

        def zwait(e, carry):
            @pl.when(pend_ref[e] > pstart_ref[e])
            def _():
                zero_copy(e).wait()
            return carry

        def tissue(b, carry):
            tail_copy(b).start()
            return carry

        def twait(b, carry):
            tail_copy(b).wait()
            return carry

        lax.fori_loop(0, N_EXPERTS, zissue, 0)
        lax.fori_loop(first_unused, nblk, tissue, 0)
        lax.fori_loop(0, N_EXPERTS, zwait, 0)
        lax.fori_loop(first_unused, nblk, twait, 0)

    pos = pos_ref[...]
    piota = lax.broadcasted_iota(I32, (GROUPED_ROWS, tm), 0)
    perm = jnp.zeros((GROUPED_ROWS, tm), F32)
    for r in range(TOP_K):
        perm = jnp.where(piota == pos[r:r + 1, :], 1.0, perm)
    perm = perm.astype(BF16)
    cur = lax.rem(i, 2)
    cbuf_ref[cur] = _pack_exact_bf16_pair(_dot(perm, h2_ref[:, 0:dh]), _dot(perm, h2_ref[:, dh:]))

    def make_copy(buf):
        def build(tile_row, buffer_row, rows):
            return pltpu.make_async_copy(cbuf_ref.at[buf, pl.ds(tile_row, rows)],
                                         xs_hbm.at[pl.ds(buffer_row, rows)], sem.at[buf])
        return build

    _start_chunks(i, chunk_ref, nch_ref, pstart_ref, make_copy(cur))

    @pl.when(i > 0)
    def _():
        _wait_chunks(i - 1, nch_ref, make_copy(1 - cur))

    @pl.when(i == nt - 1)
    def _():
        _wait_chunks(i, nch_ref, make_copy(cur))


def _dispatch(chunks, nch, pstart, pend, pos, h2, nrows, tm, bm):
    n, d = h2.shape
    grid_spec = pltpu.PrefetchScalarGridSpec(
        num_scalar_prefetch=4,
        grid=(n // tm,),
        in_specs=[
            pl.BlockSpec((TOP_K, tm), lambda i, *_: (0, i)),
            pl.BlockSpec((tm, d), lambda i, *_: (i, 0)),
        ],
        out_specs=pl.BlockSpec(memory_space=pl.ANY),
        scratch_shapes=[
            pltpu.VMEM((2, GROUPED_ROWS, d // 2), U32),
            pltpu.VMEM((bm, d // 2), U32),
            pltpu.SemaphoreType.DMA(()),
            pltpu.SemaphoreType.DMA((2,)),
        ],
    )
    return pl.pallas_call(
        functools.partial(_dispatch_kernel, tm=tm, bm=bm),
        grid_spec=grid_spec,
        out_shape=jax.ShapeDtypeStruct((nrows, d // 2), U32),
        compiler_params=_params(("arbitrary",)),
        name="moe_dispatch",
    )(chunks, nch, pstart, pend, pos, h2)


def _experts_kernel(be_ref, nu_ref, x_ref, wgu_ref, wd_ref, o_ref):
    i = pl.program_id(0)

    @pl.when(i < nu_ref[0])
    def _():
        lo, hi = _unpack_bf16_pair(x_ref[...])
        x = jnp.concatenate([lo.astype(BF16), hi.astype(BF16)], axis=1)
        gu = _dot(x, wgu_ref[...].astype(BF16))
        act = (_silu(gu[:, 0:D_EXPERT]) * gu[:, D_EXPERT:]).astype(BF16)
        y = _dot(act, wd_ref[...].astype(BF16))
        o_ref[...] = _pack_bf16_pair(y[:, 0:D_MODEL // 2], y[:, D_MODEL // 2:])

    @pl.when(i >= nu_ref[0])
    def _():
        o_ref[...] = jnp.zeros(o_ref.shape, o_ref.dtype)


def _experts(block_e, nused, xs, w_gate_up_l, w_down_l, bm):
    nrows, dh = xs.shape
    d = 2 * dh
    nblk = nrows // bm
    grid_spec = pltpu.PrefetchScalarGridSpec(
        num_scalar_prefetch=2,
        grid=(nblk,),
        in_specs=[
            pl.BlockSpec((bm, dh), lambda i, be, nu: (jnp.minimum(i, nu[0] - 1), 0)),
            pl.BlockSpec((None, d, 2 * D_EXPERT), lambda i, be, nu: (be[i], 0, 0)),
            pl.BlockSpec((None, D_EXPERT, d), lambda i, be, nu: (be[i], 0, 0)),
        ],
        out_specs=pl.BlockSpec((bm, dh), lambda i, be, nu: (i, 0)),
    )
    return pl.pallas_call(
        _experts_kernel,
        grid_spec=grid_spec,
        out_shape=jax.ShapeDtypeStruct((nrows, dh), U32),
        compiler_params=_params(("arbitrary",)),
        name="moe_experts",
    )(block_e, nused, xs, w_gate_up_l, w_down_l)


def _combine_kernel(chunk_ref, nch_ref, pstart_ref, pos_ref, wl_ref, xs_ref, ga2_ref, gf_ref,
                    yb_hbm, o_ref, gbuf_ref, sem, *, tm):
    i = pl.program_id(0)
    nt = pl.num_programs(0)
    cur = lax.rem(i, 2)
    refs = (chunk_ref, nch_ref, pstart_ref)

    def make_copy(buf):
        def build(tile_row, buffer_row, rows):
            return pltpu.make_async_copy(yb_hbm.at[pl.ds(buffer_row, rows)],
                                         gbuf_ref.at[buf, pl.ds(tile_row, rows)], sem.at[buf])
        return build

    @pl.when(i == 0)
    def _():
        gbuf_ref[...] = jnp.zeros(gbuf_ref.shape, gbuf_ref.dtype)
        _start_chunks(0, *refs, make_copy(0))

    @pl.when(i + 1 < nt)
    def _():
        _start_chunks(i + 1, *refs, make_copy(1 - cur))

    _wait_chunks(i, nch_ref, make_copy(cur))

    lo, hi = _unpack_bf16_pair(gbuf_ref[cur])
    g = jnp.concatenate([lo.astype(BF16), hi.astype(BF16)], axis=1)
    pos = pos_ref[...]
    wl = wl_ref[...]
    liota = lax.broadcasted_iota(I32, (tm, GROUPED_ROWS), 1)
    a = jnp.zeros((tm, GROUPED_ROWS), F32)
    for r in range(TOP_K):
        a = jnp.where(liota == pos[:, r:r + 1], wl[:, r:r + 1], a)
    routed = _dot(a.astype(BF16), g)
    x2 = xs_ref[...] + ga2_ref[...] * routed
    ms = jnp.mean(x2 * x2, axis=-1, keepdims=True)
    o_ref[...] = x2 * lax.rsqrt(ms + EPS) * gf_ref[...]


def _combine(chunks, nch, pstart, pos_t, wl_t, xs_base, ga2, gfin, yb, tm):
    n, d = xs_base.shape
    ga2_spec = (pl.BlockSpec((1, d), lambda i, *_: (0, 0)) if ga2.shape[0] == 1
                else pl.BlockSpec((tm, d), lambda i, *_: (i, 0)))
    grid_spec = pltpu.PrefetchScalarGridSpec(
        num_scalar_prefetch=3,
        grid=(n // tm,),
        in_specs=[
            pl.BlockSpec((tm, TOP_K), lambda i, *_: (i, 0)),
            pl.BlockSpec((tm, TOP_K), lambda i, *_: (i, 0)),
            pl.BlockSpec((tm, d), lambda i, *_: (i, 0)),
            ga2_spec,
            pl.BlockSpec((1, d), lambda i, *_: (0, 0)),
            pl.BlockSpec(memory_space=pl.ANY),
        ],
        out_specs=pl.BlockSpec((tm, d), lambda i, *_: (i, 0)),
        scratch_shapes=[
            pltpu.VMEM((2, GROUPED_ROWS, d // 2), U32),
            pltpu.SemaphoreType.DMA((2,)),
        ],
    )
    return pl.pallas_call(
        functools.partial(_combine_kernel, tm=tm),
        grid_spec=grid_spec,
        out_shape=jax.ShapeDtypeStruct((n, d), F32),
        compiler_params=_params(("arbitrary",)),
        name="moe_combine",
    )(chunks, nch, pstart, pos_t, wl_t, xs_base, ga2, gfin, yb)


def _moe_and_final(x, ohg, oda, mods, wts, tm, bm):
    n, d = x.shape
    nt = n // tm
    ga1, sh2, sc2, ga2 = mods
    (w_out_b, g_ffn, wsgu_b, wsd_b, wr_t, rb, w_gate_up_l, w_down_l, g_final) = wts
    xs_base, h2, pos, wl, chunks, nch, cnt = _post(
        x, ohg, oda, w_out_b, ga1, g_ffn, sc2, sh2, ga2, wsgu_b, wsd_b, wr_t, rb, tm)
    counts = cnt[:, 0].astype(I32)
    padded = (counts + bm - 1) // bm * bm
    pend = jnp.cumsum(padded)
    pstart = pend - padded
    nblk = -(-(nt * GROUPED_ROWS) // bm) + N_EXPERTS
    nused = (pend[-1] // bm).astype(I32)
    blk_row = jnp.minimum(jnp.arange(nblk, dtype=I32), nused - 1) * bm
    be = jnp.sum((pend[None, :] <= blk_row[:, None]).astype(I32), axis=1)
    chunks = chunks.reshape(-1)
    nch = nch[:, 0, 0]
    xs = _dispatch(chunks, nch, pstart, pend, pos, h2, nblk * bm, tm, bm)
    yb = _experts(be, nused.reshape(1), xs, w_gate_up_l, w_down_l, bm)
    return _combine(chunks, nch, pstart, pos.T, wl.T, xs_base, ga2, g_final, yb, tm)


def _expand(mod, reps):
    if mod.shape[0] == 1:
        return mod
    return jnp.repeat(mod, reps, axis=0)


def kernel(x_prompt, x_sample, cache_k, cache_v, state_hgrn, c_prompt, c_sample, w_ada, b_ada,
           norm_mix, norm_ffn, norm_final, w_in, w_out, hg_lb_logits, hg_norm, da_lambda, da_norm,
           rel_bias_table, w_router, router_bias, w_gate_up, w_down, ws_gate_up, ws_down):
    depth = w_in.shape[0]
    assert depth == 1 and hg_lb_logits.shape[0] == 2
    bp, tp, d = x_prompt.shape
    bs, ts, _ = x_sample.shape
    assert bp == 1
    past = cache_k.shape[2]
    l = 0

    rows = -(-(bp + bs) // 8) * 8
    c_all = jnp.zeros((rows, d), F32).at[:bp].set(c_prompt).at[bp:bp + bs].set(c_sample)
    mod = _adaln(c_all, w_ada[l], b_ada[l])
    mod_p = [mod[0:bp, j * d:(j + 1) * d] for j in range(6)]
    mod_s = [_expand(mod[bp:bp + bs, j * d:(j + 1) * d], ts) for j in range(6)]

    w_in_b = w_in[l].astype(BF16)
    w_out_b = w_out[l].astype(BF16)
    wsgu_b = ws_gate_up[l].astype(BF16)
    wsd_b = ws_down[l].astype(BF16)
    wr_t = w_router[l].T
    g_mix = norm_mix[l].reshape(1, d)
    g_ffn = norm_ffn[l].reshape(1, d)
    g_final = norm_final.reshape(1, d)
    moe_w = (w_out_b, g_ffn, wsgu_b, wsd_b, wr_t, router_bias[l], w_gate_up[l], w_down[l], g_final)

    lam = _lam(da_lambda[l])

    t_att = min(ATT_TILE, tp)
    kk = jnp.arange(t_att, dtype=I32)[:, None]
    qq = jnp.arange(t_att, dtype=I32)[None, :]
    idx_diag = jnp.where((kk // CHUNK) <= (qq // CHUNK), _rel_bucket(kk - qq), MASK_BUCKET)
    idx_prev = _rel_bucket(kk - qq - t_att)
    bias_p = _bias_tiles(rel_bias_table, jnp.stack([idx_diag, idx_prev]).astype(I32),
                         shift_bucket=N_BUCKETS // 2 - 1)
    pad = 128
    qpos = past + jnp.arange(ts, dtype=I32)[:, None]
    idx_sp = _rel_bucket(jnp.arange(past, dtype=I32)[None, :] - qpos)
    kn = jnp.arange(pad, dtype=I32)[None, :]
    idx_sn = jnp.where(kn < ts, _rel_bucket(past + kn - qpos), MASK_BUCKET)
    bias_sp = _bias_tiles(rel_bias_table, idx_sp[None].astype(I32), shift_bucket=None)
    bias_sn = _bias_tiles(rel_bias_table, idx_sn[None].astype(I32), shift_bucket=None)

    xp = x_prompt.reshape(bp * tp, d)
    sh1, sc1, ga1, sh2, sc2, ga2 = mod_p
    assert ATT_TILE == INPROJ_TILE
    zh, qt, kf, vf, kb, vt = _inproj(xp, g_mix, sc1, sh1, w_in_b, t_att, True)
    s_zero = jnp.zeros((bp, HG_HEADS, HG_DIM, HG_DIM), F32)
    ohg_p, sp_new = _hgrn(zh, s_zero, hg_lb_logits, hg_norm[l], bp, tp, min(HGRN_CHUNK, tp))
    oda_p = _attn_prompt(kb, qt, vt, bias_p, lam, da_norm[l], t_att)
    y_p = _moe_and_final(xp, ohg_p, oda_p, (ga1, sh2, sc2, ga2), moe_w, POST_TILE, MOE_BLOCK_ROWS)
    k_prompt = kf.reshape(1, bp, tp, DA_HEADS, 2 * DA_QKDIM)
    v_prompt = vf.reshape(1, bp, tp, DA_HEADS, DA_VDIM)

    ns = bs * ts
    xs_ = x_sample.reshape(ns, d)
    sh1, sc1, ga1, sh2, sc2, ga2 = mod_s
    zh, qs, kf, vf, kb, vb = _inproj(xs_, g_mix, sc1, sh1, w_in_b, ns, False)
    ohg_s, ss_new = _hgrn(zh, state_hgrn[l], hg_lb_logits, hg_norm[l], bs, ts, ts)
    oda_s = _attn_step(qs, cache_k[l], cache_v[l], kb, vb, bias_sp, bias_sn, lam, da_norm[l], bs, ts)
    assert ns == POST_TILE
    y_s = _moe_and_final(xs_, ohg_s, oda_s, (ga1, sh2, sc2, ga2), moe_w, POST_TILE, 128)
    k_sample = kf.reshape(1, bs, ts, DA_HEADS, 2 * DA_QKDIM)
    v_sample = vf.reshape(1, bs, ts, DA_HEADS, DA_VDIM)

    return (y_p.reshape(bp, tp, d), y_s.reshape(bs, ts, d), k_prompt, v_prompt, sp_new[None],
            k_sample, v_sample, ss_new[None].astype(x_sample.dtype))
```

```python
import functools
import math

import numpy as np
import jax
import jax.numpy as jnp
from jax import lax
from jax.experimental import pallas as pl
from jax.experimental.pallas import tpu as pltpu

F32 = jnp.float32
BF16 = jnp.bfloat16
I32 = jnp.int32
U32 = jnp.uint32
HIGHEST = lax.Precision.HIGHEST

D_MODEL = 1024
CHUNK = 64
HG_HEADS = 4
HG_DIM = 128
HG_WIDTH = HG_HEADS * HG_DIM
DA_HEADS = 4
DA_VDIM = 128
DA_QKDIM = 64
DA_WIDTH = DA_HEADS * DA_VDIM
N_BUCKETS = 32
MAX_DIST = 128
N_EXPERTS = 64
TOP_K = 8
N_GROUPS = 8
GROUP_SIZE = N_EXPERTS // N_GROUPS
TOP_GROUPS = 4
D_EXPERT = 256
ROUTE_SCALE = 2.5
EPS = 1e-6
LAM_INIT = 0.8 - 0.6 * math.exp(-0.3 * 0)

LOG2E = math.log2(math.e)
HI_MASK = np.uint32(0xFFFF0000)
NEG_BIG = -1e30
MASK_BUCKET = N_BUCKETS
V7X_VMEM_LIMIT = 48 * 1024 * 1024

ATT_TILE = 512
HGRN_CHUNK = 256
INPROJ_TILE = 512
POST_TILE = 256
MOE_BLOCK_ROWS = 1024
ROW_GROUP = 8
GROUPED_ROWS = -(-(POST_TILE * TOP_K + N_EXPERTS * (ROW_GROUP - 1)) // 256) * 256
CHUNK_SLOTS = -(-(GROUPED_ROWS // ROW_GROUP) // 128) * 128
CHUNK_EXPERT_SHIFT = 24
CHUNK_UNROLL = 4
WAIT_CHUNKS = 16


def _sigmoid(x):
    return 1.0 / (1.0 + jnp.exp(-x))


def _silu(x):
    return x * _sigmoid(x)


def _dot(a, b, **kw):
    return jnp.dot(a, b, preferred_element_type=F32, **kw)


def _dot_nt(a, b, **kw):
    return lax.dot_general(a, b, (((1,), (1,)), ((), ())), preferred_element_type=F32, **kw)


def _dot_tn(a, b, **kw):
    return lax.dot_general(a, b, (((0,), (0,)), ((), ())), preferred_element_type=F32, **kw)


def _pack_bf16_pair(lo, hi):
    lo_bits = lax.bitcast_convert_type(lo.astype(BF16).astype(F32), U32)
    hi_bits = lax.bitcast_convert_type(hi.astype(BF16).astype(F32), U32)
    return (lo_bits >> 16) | (hi_bits & HI_MASK)


def _pack_exact_bf16_pair(lo, hi):
    return (lax.bitcast_convert_type(lo, U32) >> 16) | (lax.bitcast_convert_type(hi, U32) & HI_MASK)


def _unpack_bf16_pair(w):
    lo = lax.bitcast_convert_type(w << 16, F32)
    hi = lax.bitcast_convert_type(w & HI_MASK, F32)
    return lo, hi


def _params(sem, vmem=V7X_VMEM_LIMIT, flags=None):
    return pltpu.CompilerParams(dimension_semantics=sem, vmem_limit_bytes=vmem, flags=flags)


def _adaln_kernel(c_ref, w_ref, b_ref, o_ref):
    s = _silu(c_ref[...])
    o_ref[...] = _dot(s, w_ref[...], precision=HIGHEST) + b_ref[...]


def _adaln(c_all, w_ada, b_ada):
    rows, d = c_all.shape
    cols = w_ada.shape[1]
    blk = 1024
    return pl.pallas_call(
        _adaln_kernel,
        grid=(cols // blk,),
        in_specs=[
            pl.BlockSpec((rows, d), lambda j: (0, 0)),
            pl.BlockSpec((d, blk), lambda j: (0, j)),
            pl.BlockSpec((1, blk), lambda j: (0, j)),
        ],
        out_specs=pl.BlockSpec((rows, blk), lambda j: (0, j)),
        out_shape=jax.ShapeDtypeStruct((rows, cols), F32),
        compiler_params=_params(("parallel",)),
        name="adaln",
    )(c_all, w_ada, b_ada.reshape(1, cols))


def _lam_kernel(l_ref, o_ref):
    l = l_ref[...].astype(F32)
    a = jnp.sum(l[0:1] * l[1:2], axis=-1, keepdims=True)
    b = jnp.sum(l[2:3] * l[3:4], axis=-1, keepdims=True)
    lam = jnp.exp(a) - jnp.exp(b) + LAM_INIT
    o_ref[...] = jnp.broadcast_to(lam, o_ref.shape)


def _lam(da_lambda_l):
    return pl.pallas_call(
        _lam_kernel,
        out_shape=jax.ShapeDtypeStruct((8, 128), F32),
        name="lam",
    )(da_lambda_l)


def _rel_bucket(rel):
    nb = N_BUCKETS // 2
    max_exact = nb // 2
    side = jnp.where(rel > 0, nb, 0)
    n = jnp.abs(rel)
    large = max_exact + (jnp.log(jnp.maximum(n, 1).astype(F32) / max_exact)
                         / math.log(MAX_DIST / max_exact) * (nb - max_exact)).astype(I32)
    large = jnp.minimum(large, nb - 1)
    return side + jnp.where(n < max_exact, n, large)


def _bias_kernel(tab_ref, idx_ref, o_ref, *, shift_bucket):
    h = pl.program_id(0)
    idx = idx_ref[...]
    shift = tab_ref[shift_bucket, h] if shift_bucket is not None else 0.0
    acc = jnp.zeros(idx.shape, F32)
    for j in range(N_BUCKETS):
        acc = jnp.where(idx == j, (tab_ref[j, h] - shift) * LOG2E, acc)
    o_ref[...] = jnp.where(idx == MASK_BUCKET, NEG_BIG, acc)


def _bias_tiles(table, idx, *, shift_bucket):
    k, r, c = idx.shape
    return pl.pallas_call(
        functools.partial(_bias_kernel, shift_bucket=shift_bucket),
        grid=(DA_HEADS, k),
        in_specs=[
            pl.BlockSpec(memory_space=pltpu.SMEM),
            pl.BlockSpec((None, r, c), lambda h, d: (d, 0, 0)),
        ],
        out_specs=pl.BlockSpec((None, None, r, c), lambda h, d: (h, d, 0, 0)),
        out_shape=jax.ShapeDtypeStruct((DA_HEADS, k, r, c), F32),
        compiler_params=_params(("parallel", "parallel")),
        name="rel_bias",
    )(table, idx)


def _inproj_kernel(x_ref, g_ref, sc_ref, sh_ref, w_ref,
                   zh_ref, q_ref, k_ref, v_ref, kb_ref, vb_ref, *, transposed):
    x = x_ref[...]
    ms = jnp.mean(x * x, axis=-1, keepdims=True)
    h = x * lax.rsqrt(ms + EPS) * g_ref[...]
    h = h * (1.0 + sc_ref[...]) + sh_ref[...]
    hb = h.astype(BF16)
    c0 = 4 * HG_WIDTH
    zh_ref[...] = _dot(hb, w_ref[:, 0:c0])
    zq = _dot(hb, w_ref[:, c0:c0 + DA_WIDTH]) * (DA_QKDIM ** -0.5 * LOG2E)
    zk = _dot(hb, w_ref[:, c0 + DA_WIDTH:c0 + 2 * DA_WIDTH])
    k_ref[...] = zk
    kb_ref[...] = zk.astype(BF16)
    zv = _dot(hb, w_ref[:, c0 + 2 * DA_WIDTH:c0 + 3 * DA_WIDTH])
    v_ref[...] = zv
    if transposed:
        q_ref[...] = zq.T.astype(BF16)
        vb_ref[...] = zv.T.astype(BF16).reshape(vb_ref.shape)
    else:
        q_ref[...] = zq.astype(BF16)
        vb_ref[...] = zv.astype(BF16)


def _mod_spec(mod, tm):
    if mod.shape[0] == 1:
        return pl.BlockSpec((1, mod.shape[1]), lambda i: (0, 0))
    return pl.BlockSpec((tm, mod.shape[1]), lambda i: (i, 0))


def _inproj(x, g, sc, sh, w_in_b, tm, transposed):
    n, d = x.shape
    cols = w_in_b.shape[1]
    row = lambda i: (i, 0)
    if transposed:
        q_spec = pl.BlockSpec((DA_WIDTH, tm), lambda i: (0, i))
        q_shape = jax.ShapeDtypeStruct((DA_WIDTH, n), BF16)
        vb_spec = pl.BlockSpec((DA_HEADS, None, DA_VDIM, tm), lambda i: (0, i, 0, 0))
        vb_shape = jax.ShapeDtypeStruct((DA_HEADS, n // tm, DA_VDIM, tm), BF16)
    else:
        q_spec = vb_spec = pl.BlockSpec((tm, DA_WIDTH), row)
        q_shape = vb_shape = jax.ShapeDtypeStruct((n, DA_WIDTH), BF16)
    return pl.pallas_call(
        functools.partial(_inproj_kernel, transposed=transposed),
        grid=(n // tm,),
        in_specs=[
            pl.BlockSpec((tm, d), row),
            pl.BlockSpec((1, d), lambda i: (0, 0)),
            _mod_spec(sc, tm),
            _mod_spec(sh, tm),
            pl.BlockSpec((d, cols), lambda i: (0, 0)),
        ],
        out_specs=[
            pl.BlockSpec((tm, 4 * HG_WIDTH), row),
            q_spec,
            pl.BlockSpec((tm, DA_WIDTH), row),
            pl.BlockSpec((tm, DA_WIDTH), row),
            pl.BlockSpec((tm, DA_WIDTH), row),
            vb_spec,
        ],
        out_shape=[
            jax.ShapeDtypeStruct((n, 4 * HG_WIDTH), F32),
            q_shape,
            jax.ShapeDtypeStruct((n, DA_WIDTH), F32),
            jax.ShapeDtypeStruct((n, DA_WIDTH), F32),
            jax.ShapeDtypeStruct((n, DA_WIDTH), BF16),
            vb_shape,
        ],
        compiler_params=_params(("parallel",)),
        name="inproj",
    )(x, g, sc, sh, w_in_b)


def _hgrn_consts(c):
    levels = int(round(math.log2(c)))
    assert 1 << levels == c
    t = np.arange(c)[:, None]
    r = np.arange(c)[None, :]
    blocks = [r <= t]
    for l in range(levels):
        m = 1 << l
        mid = (t // (2 * m)) * (2 * m) + m - 1
        later = (t & m) != 0
        blocks.append(np.where(later, (r > mid) & (r <= t), (r > t) & (r <= mid)))
    mall = np.concatenate(blocks, axis=0).astype(np.float32)
    x = np.maximum(t ^ r, 1)
    lv = np.where(t == r, -1, np.where(t > r, np.floor(np.log2(x)).astype(np.int64), -2))
    return jnp.asarray(mall, dtype=BF16), jnp.asarray(lv, dtype=I32), levels


def _hgrn_kernel(zh_ref, s0_ref, lbl_ref, gain_ref, mall_ref, lv_ref,
                 o_ref, sout_ref, st_ref, *, c, levels):
    ci = pl.program_id(1)

    @pl.when(ci == 0)
    def _():
        for h in range(HG_HEADS):
            st_ref[h] = s0_ref[h].astype(F32).T

    lbl = lbl_ref[...].astype(F32)
    mx = jnp.maximum(lbl[0:1], lbl[1:2])
    e0 = jnp.exp(lbl[0:1] - mx)
    e1 = jnp.exp(lbl[1:2] - mx)
    lb = e0 / (e0 + e1)

    xq = zh_ref[:, 0:HG_WIDTH]
    xf = zh_ref[:, HG_WIDTH:2 * HG_WIDTH]
    q = _silu(xq)
    y = lb + (1.0 - lb) * _sigmoid(xf)
    logf = jnp.log(y)
    kk = 1.0 - y

    l1 = logf.astype(BF16)
    r1 = logf - l1.astype(F32)
    l2 = r1.astype(BF16)
    l3 = (r1 - l2.astype(F32)).astype(BF16)
    mall = mall_ref[...]
    e_all = _dot(mall, l1) + _dot(mall, l2) + _dot(mall, l3)

    lv = lv_ref[...]
    gain = gain_ref[...].astype(F32)
    for h in range(HG_HEADS):
        sl = slice(h * HG_DIM, (h + 1) * HG_DIM)
        qh = q[:, sl]
        kh = kk[:, sl]
        ih = zh_ref[:, 2 * HG_WIDTH + h * HG_DIM:2 * HG_WIDTH + (h + 1) * HG_DIM]
        gh = zh_ref[:, 3 * HG_WIDTH + h * HG_DIM:3 * HG_WIDTH + (h + 1) * HG_DIM]
        bh = e_all[0:c, sl]
        ihb = ih.astype(BF16)
        a = jnp.where(lv == -1, _dot_nt(qh.astype(BF16), kh.astype(BF16)), 0.0)
        for l in range(levels):
            f = jnp.exp(e_all[(l + 1) * c:(l + 2) * c, sl])
            p = _dot_nt((qh * f).astype(BF16), (kh * f).astype(BF16))
            a = a + jnp.where(lv == l, p, 0.0)
        st = st_ref[h]
        o = _dot(a.astype(BF16), ihb) + _dot_nt((qh * jnp.exp(bh)).astype(BF16), st.astype(BF16))
        bl = bh[c - 1:c, :]
        kd = (kh * jnp.exp(bl - bh)).astype(BF16)
        st_ref[h] = st * jnp.exp(bl) + _dot_tn(ihb, kd)
        ms = jnp.mean(o * o, axis=-1, keepdims=True)
        on = o * lax.rsqrt(ms + EPS) * gain
        o_ref[:, sl] = (on * _silu(gh)).astype(o_ref.dtype)

    @pl.when(ci == pl.num_programs(1) - 1)
    def _():
        for h in range(HG_HEADS):
            sout_ref[h] = st_ref[h].T.astype(sout_ref.dtype)


def _hgrn(zh, s0, lb_logits, gain, batch, seq, c):
    mall, lv, levels = _hgrn_consts(c)
    nc = seq // c
    return pl.pallas_call(
        functools.partial(_hgrn_kernel, c=c, levels=levels),
        grid=(batch, nc),
        in_specs=[
            pl.BlockSpec((c, 4 * HG_WIDTH), lambda b, i: (b * nc + i, 0)),
            pl.BlockSpec((None, HG_HEADS, HG_DIM, HG_DIM), lambda b, i: (b, 0, 0, 0)),
            pl.BlockSpec(lb_logits.shape, lambda b, i: (0, 0)),
            pl.BlockSpec((1, HG_DIM), lambda b, i: (0, 0)),
            pl.BlockSpec(mall.shape, lambda b, i: (0, 0)),
            pl.BlockSpec(lv.shape, lambda b, i: (0, 0)),
        ],
        out_specs=[
            pl.BlockSpec((c, HG_WIDTH), lambda b, i: (b * nc + i, 0)),
            pl.BlockSpec((None, HG_HEADS, HG_DIM, HG_DIM), lambda b, i: (b, 0, 0, 0)),
        ],
        out_shape=[
            jax.ShapeDtypeStruct((batch * seq, HG_WIDTH), BF16),
            jax.ShapeDtypeStruct((batch, HG_HEADS, HG_DIM, HG_DIM), F32),
        ],
        scratch_shapes=[pltpu.VMEM((HG_HEADS, HG_DIM, HG_DIM), F32)],
        compiler_params=_params(("parallel", "arbitrary")),
        name="hgrn2",
    )(zh, s0, lb_logits, gain.reshape(1, HG_DIM), mall, lv)


def _attn_kernel(k_ref, qt_ref, vt_ref, bias_ref, lam_ref, gain_ref,
                 o_ref, qz_ref, m_ref, l_ref, acc_ref, s_ref, *, t):
    i = pl.program_id(1)
    qt = qt_ref[...]
    row = lax.broadcasted_iota(I32, qt.shape, 0)
    zero = jnp.zeros_like(qt)
    qz_ref[:, 0:t] = jnp.where(row < DA_QKDIM, qt, zero)
    qz_ref[:, t:2 * t] = jnp.where(row >= DA_QKDIM, qt, zero)
    m_ref[...] = jnp.full(m_ref.shape, NEG_BIG, F32)
    l_ref[...] = jnp.zeros(l_ref.shape, F32)
    acc_ref[...] = jnp.zeros(acc_ref.shape, F32)

    def scores(j, buf):
        kt = k_ref[pl.ds(pl.multiple_of(j * t, t), t), :]
        s_ref[buf] = _dot(kt, qz_ref[...])

    def consume(j, buf, bias_idx):
        s = s_ref[buf]
        if bias_idx is not None:
            b = bias_ref[bias_idx]
            s = jnp.concatenate([s[:, 0:t] + b, s[:, t:2 * t] + b], axis=1)
        m_prev = m_ref[...]
        m_new = jnp.maximum(m_prev, jnp.max(s, axis=0, keepdims=True))
        alpha = jnp.exp2(m_prev - m_new)
        pr = jnp.exp2(s - m_new)
        l_ref[...] = alpha * l_ref[...] + jnp.sum(pr, axis=0, keepdims=True)
        acc_ref[...] = alpha * acc_ref[...] + _dot(vt_ref[j], pr.astype(BF16))
        m_ref[...] = m_new

    n_far = jnp.maximum(i - 1, 0)

    @pl.when(n_far > 0)
    def _():
        scores(0, 0)

    def far_pair(p, carry):
        j = 2 * p
        scores(j + 1, 1)
        consume(j, 0, None)
        scores(jnp.minimum(j + 2, n_far - 1), 0)
        consume(j + 1, 1, None)
        return carry

    lax.fori_loop(0, n_far // 2, far_pair, 0)

    @pl.when(lax.rem(n_far, 2) == 1)
    def _():
        consume(n_far - 1, 0, None)

    @pl.when(i >= 1)
    def _():
        scores(i - 1, 0)
        scores(i, 1)
        consume(i - 1, 0, 1)
        consume(i, 1, 0)

    @pl.when(i == 0)
    def _():
        scores(i, 1)
        consume(i, 1, 0)

    lam = lam_ref[0:1, 0:1]
    l = l_ref[...]
    acc = acc_ref[...]
    o = acc[:, 0:t] / l[:, 0:t] - lam * (acc[:, t:2 * t] / l[:, t:2 * t])
    ms = jnp.mean(o * o, axis=0, keepdims=True)
    on = o * lax.rsqrt(ms + EPS) * gain_ref[...].astype(F32) * (1.0 - LAM_INIT)
    o_ref[...] = on.T.astype(o_ref.dtype)


def _attn_prompt(kb, qt, vt, bias, lam, gain, t):
    n = kb.shape[0]
    nt = n // t
    return pl.pallas_call(
        functools.partial(_attn_kernel, t=t),
        grid=(DA_HEADS, nt),
        in_specs=[
            pl.BlockSpec((n, DA_VDIM), lambda h, i: (0, h)),
            pl.BlockSpec((DA_VDIM, t), lambda h, i: (h, i)),
            pl.BlockSpec((None, nt, DA_VDIM, t), lambda h, i: (h, 0, 0, 0)),
            pl.BlockSpec((None, 2, t, t), lambda h, i: (h, 0, 0, 0)),
            pl.BlockSpec((8, 128), lambda h, i: (0, 0)),
            pl.BlockSpec((DA_VDIM, 1), lambda h, i: (0, 0)),
        ],
        out_specs=pl.BlockSpec((t, DA_VDIM), lambda h, i: (i, h)),
        out_shape=jax.ShapeDtypeStruct((n, DA_WIDTH), BF16),
        scratch_shapes=[
            pltpu.VMEM((DA_VDIM, 2 * t), BF16),
            pltpu.VMEM((1, 2 * t), F32),
            pltpu.VMEM((1, 2 * t), F32),
            pltpu.VMEM((DA_VDIM, 2 * t), F32),
            pltpu.VMEM((2, t, 2 * t), F32),
        ],
        compiler_params=_params(("parallel", "parallel")),
        name="diff_attn_prompt",
    )(kb, qt, vt, bias, lam, gain.reshape(DA_VDIM, 1))


def _attn_step_kernel(q_ref, kp_ref, vp_ref, kn_ref, vn_ref, bp_ref, bn_ref, lam_ref, gain_ref,
                      o_ref, *, tq, pad):
    q = q_ref[...]
    lane = lax.broadcasted_iota(I32, q.shape, 1)
    zero = jnp.zeros_like(q)
    qz = jnp.concatenate([jnp.where(lane < DA_QKDIM, q, zero),
                          jnp.where(lane >= DA_QKDIM, q, zero)], axis=0)
    kp = kp_ref[...].astype(BF16)
    vp = vp_ref[...].astype(BF16)
    zpad = jnp.zeros((pad - tq, DA_VDIM), BF16)
    kn = jnp.concatenate([kn_ref[...], zpad], axis=0)
    vn = jnp.concatenate([vn_ref[...], zpad], axis=0)
    bp = bp_ref[...]
    bn = bn_ref[...]
    sp = _dot_nt(qz, kp) + jnp.concatenate([bp, bp], axis=0)
    sn = _dot_nt(qz, kn) + jnp.concatenate([bn, bn], axis=0)
    m = jnp.maximum(jnp.max(sp, axis=-1, keepdims=True), jnp.max(sn, axis=-1, keepdims=True))
    pp = jnp.exp2(sp - m)
    pn = jnp.exp2(sn - m)
    l = jnp.sum(pp, axis=-1, keepdims=True) + jnp.sum(pn, axis=-1, keepdims=True)
    acc = _dot(pp.astype(BF16), vp) + _dot(pn.astype(BF16), vn)
    on = acc / l
    lam = lam_ref[0:1, 0:1]
    o = on[0:tq] - lam * on[tq:2 * tq]
    ms = jnp.mean(o * o, axis=-1, keepdims=True)
    o = o * lax.rsqrt(ms + EPS) * gain_ref[...].astype(F32) * (1.0 - LAM_INIT)
    o_ref[...] = o.astype(o_ref.dtype)


def _attn_step(qs, cache_k_l, cache_v_l, kb, vb, bias_p, bias_n, lam, gain, batch, tq):
    past = cache_k_l.shape[1]
    pad = bias_n.shape[-1]
    return pl.pallas_call(
        functools.partial(_attn_step_kernel, tq=tq, pad=pad),
        grid=(batch, DA_HEADS),
        in_specs=[
            pl.BlockSpec((tq, DA_VDIM), lambda b, h: (b, h)),
            pl.BlockSpec((None, past, DA_VDIM), lambda b, h: (b, 0, h)),
            pl.BlockSpec((None, past, DA_VDIM), lambda b, h: (b, 0, h)),
            pl.BlockSpec((tq, DA_VDIM), lambda b, h: (b, h)),
            pl.BlockSpec((tq, DA_VDIM), lambda b, h: (b, h)),
            pl.BlockSpec((None, None, tq, past), lambda b, h: (h, 0, 0, 0)),
            pl.BlockSpec((None, None, tq, pad), lambda b, h: (h, 0, 0, 0)),
            pl.BlockSpec((8, 128), lambda b, h: (0, 0)),
            pl.BlockSpec((1, DA_VDIM), lambda b, h: (0, 0)),
        ],
        out_specs=pl.BlockSpec((tq, DA_VDIM), lambda b, h: (b, h)),
        out_shape=jax.ShapeDtypeStruct((batch * tq, DA_WIDTH), BF16),
        compiler_params=_params(("parallel", "parallel")),
        name="diff_attn_step",
    )(qs, cache_k_l.reshape(batch, past, DA_WIDTH), cache_v_l.reshape(batch, past, DA_WIDTH),
      kb, vb, bias_p, bias_n, lam, gain.reshape(1, DA_VDIM))


def _post_kernel(x_ref, ohg_ref, oda_ref, wout_ref, ga1_ref, g_ref, sc_ref, sh_ref, ga2_ref,
                 wsgu_ref, wsd_ref, wrt_ref, rb_ref, tri_ref, ltri_ref,
                 xs_ref, h2_ref, pos_ref, wl_ref, chunk_ref, nch_ref, cnt_ref, carry_ref, *, tm):
    i = pl.program_id(0)

    @pl.when(i == 0)
    def _():
        carry_ref[...] = jnp.zeros(carry_ref.shape, F32)

    mix = _dot(ohg_ref[...], wout_ref[0:HG_WIDTH, :]) + _dot(oda_ref[...], wout_ref[HG_WIDTH:, :])
    x1 = x_ref[...] + ga1_ref[...] * mix
    ms = jnp.mean(x1 * x1, axis=-1, keepdims=True)
    h2 = x1 * lax.rsqrt(ms + EPS) * g_ref[...]
    h2 = h2 * (1.0 + sc_ref[...]) + sh_ref[...]
    h2b = h2.astype(BF16)
    h2_ref[...] = h2b
    gu = _dot(h2b, wsgu_ref[...])
    act = (_silu(gu[:, 0:D_EXPERT]) * gu[:, D_EXPERT:]).astype(BF16)
    xs_ref[...] = x1 + ga2_ref[...] * _dot(act, wsd_ref[...])

    logits = _dot_nt(wrt_ref[...], h2, precision=HIGHEST)
    score = _sigmoid(logits)
    sel = score + rb_ref[...]
    sub = lax.broadcasted_iota(I32, (GROUP_SIZE, tm), 0)
    gscore = []
    for g in range(N_GROUPS):
        v = sel[g * GROUP_SIZE:(g + 1) * GROUP_SIZE, :]
        m1 = jnp.max(v, axis=0, keepdims=True)
        i1 = jnp.min(jnp.where(v == m1, sub, GROUP_SIZE), axis=0, keepdims=True)
        m2 = jnp.max(jnp.where(sub == i1, -jnp.inf, v), axis=0, keepdims=True)
        gscore.append(m1 + m2)
    gsel = []
    for g in range(N_GROUPS):
        ahead = jnp.zeros((1, tm), F32)
        for g2 in range(N_GROUPS):
            if g2 == g:
                continue
            tie = 1.0 if g2 < g else 0.0
            ahead = ahead + jnp.where(gscore[g2] > gscore[g], 1.0,
                                      jnp.where(gscore[g2] == gscore[g], tie, 0.0))
        gsel.append(ahead < TOP_GROUPS)
    selm = jnp.concatenate(
        [jnp.where(gsel[g], sel[g * GROUP_SIZE:(g + 1) * GROUP_SIZE, :], -jnp.inf)
         for g in range(N_GROUPS)], axis=0)
    eio = lax.broadcasted_iota(I32, (N_EXPERTS, tm), 0)
    ahead = jnp.zeros((N_EXPERTS, tm), F32)
    for e2 in range(N_EXPERTS):
        row = selm[e2:e2 + 1, :]
        tie = jnp.where(eio > e2, 1.0, 0.0)
        ahead = ahead + jnp.where(row > selm, 1.0, jnp.where(row == selm, tie, 0.0))
    chosen = jnp.where(selm > -jnp.inf, jnp.where(ahead < TOP_K, 1.0, 0.0), 0.0)
    w = chosen * score
    wn = w / jnp.sum(w, axis=0, keepdims=True) * ROUTE_SCALE

    chb = chosen.astype(BF16)
    before = _dot(chb, tri_ref[...])
    tot = _dot(chb, jnp.ones((tm, 128), BF16))
    run = jnp.floor((tot + (ROW_GROUP - 1)) * (1.0 / ROW_GROUP)) * ROW_GROUP
    tile_base = _dot(ltri_ref[...], run.astype(BF16))
    carry = carry_ref[...]
    carry_ref[...] = carry + run
    cnt_ref[...] = carry + run
    pos = jnp.concatenate([tile_base] * (tm // 128), axis=1) + before

    widen = lambda v: jnp.concatenate([v] * (CHUNK_SLOTS // 128), axis=1)
    crow = lax.broadcasted_iota(I32, (N_EXPERTS, CHUNK_SLOTS), 1).astype(F32) * ROW_GROUP
    erow = lax.broadcasted_iota(I32, (N_EXPERTS, CHUNK_SLOTS), 0).astype(F32)
    owner = jnp.sum(jnp.where(widen(tile_base + run) <= crow, 1.0, 0.0), axis=0, keepdims=True)
    region_row = jnp.sum(jnp.where(owner == erow, widen(carry - tile_base), 0.0),
                         axis=0, keepdims=True) + crow[0:1]
    chunk_ref[...] = owner.astype(I32) * (1 << CHUNK_EXPERT_SHIFT) + region_row.astype(I32)
    nch_ref[...] = jnp.sum(run * (1.0 / ROW_GROUP), axis=0, keepdims=True).astype(I32)

    for r in range(TOP_K):
        pick = jnp.where(ahead == r, chosen, 0.0)
        pos_ref[r:r + 1, :] = jnp.sum(pick * pos, axis=0, keepdims=True).astype(I32)
        wl_ref[r:r + 1, :] = jnp.sum(pick * wn, axis=0, keepdims=True)


def _post(x, ohg, oda, w_out_b, ga1, g, sc, sh, ga2, wsgu_b, wsd_b, wr_t, rb, tm):
    n, d = x.shape
    nt = n // tm
    tri = jnp.asarray(np.triu(np.ones((tm, tm), np.float32), k=1), dtype=BF16)
    ltri = jnp.asarray(np.tril(np.ones((N_EXPERTS, N_EXPERTS), np.float32), k=-1), dtype=BF16)
    row = lambda i: (i, 0)
    col = lambda i: (0, i)
    full = lambda i: (0, 0)
    return pl.pallas_call(
        functools.partial(_post_kernel, tm=tm),
        grid=(nt,),
        in_specs=[
            pl.BlockSpec((tm, d), row),
            pl.BlockSpec((tm, HG_WIDTH), row),
            pl.BlockSpec((tm, DA_WIDTH), row),
            pl.BlockSpec(w_out_b.shape, full),
            _mod_spec(ga1, tm),
            pl.BlockSpec((1, d), full),
            _mod_spec(sc, tm),
            _mod_spec(sh, tm),
            _mod_spec(ga2, tm),
            pl.BlockSpec(wsgu_b.shape, full),
            pl.BlockSpec(wsd_b.shape, full),
            pl.BlockSpec(wr_t.shape, full),
            pl.BlockSpec((N_EXPERTS, 1), full),
            pl.BlockSpec((tm, tm), full),
            pl.BlockSpec((N_EXPERTS, N_EXPERTS), full),
        ],
        out_specs=[
            pl.BlockSpec((tm, d), row),
            pl.BlockSpec((tm, d), row),
            pl.BlockSpec((TOP_K, tm), col),
            pl.BlockSpec((TOP_K, tm), col),
            pl.BlockSpec((None, 1, CHUNK_SLOTS), lambda i: (i, 0, 0)),
            pl.BlockSpec((None, 1, 128), lambda i: (i, 0, 0)),
            pl.BlockSpec((N_EXPERTS, 128), full),
        ],
        out_shape=[
            jax.ShapeDtypeStruct((n, d), F32),
            jax.ShapeDtypeStruct((n, d), BF16),
            jax.ShapeDtypeStruct((TOP_K, n), I32),
            jax.ShapeDtypeStruct((TOP_K, n), F32),
            jax.ShapeDtypeStruct((nt, 1, CHUNK_SLOTS), I32),
            jax.ShapeDtypeStruct((nt, 1, 128), I32),
            jax.ShapeDtypeStruct((N_EXPERTS, 128), F32),
        ],
        scratch_shapes=[pltpu.VMEM((N_EXPERTS, 128), F32)],
        compiler_params=_params(("arbitrary",)),
        name="post_mix_router",
    )(x, ohg, oda, w_out_b, ga1, g, sc, sh, ga2, wsgu_b, wsd_b, wr_t, rb.reshape(N_EXPERTS, 1),
      tri, ltri)


def _start_chunks(tile, chunk_ref, nch_ref, pstart_ref, make_copy):
    n = nch_ref[tile]

    def start(c):
        word = chunk_ref[tile * CHUNK_SLOTS + c]
        expert = lax.shift_right_logical(word, CHUNK_EXPERT_SHIFT)
        region_row = word & ((1 << CHUNK_EXPERT_SHIFT) - 1)
        make_copy(pl.multiple_of(c * ROW_GROUP, ROW_GROUP),
                  pl.multiple_of(pstart_ref[expert] + region_row, ROW_GROUP), ROW_GROUP).start()

    def group(g, carry):
        for u in range(CHUNK_UNROLL):
            start(g * CHUNK_UNROLL + u)
        return carry

    def single(c, carry):
        start(c)
        return carry

    groups = n // CHUNK_UNROLL
    lax.fori_loop(0, groups, group, 0)
    lax.fori_loop(groups * CHUNK_UNROLL, n, single, 0)


def _wait_chunks(tile, nch_ref, make_copy):
    n = nch_ref[tile]
    many = n // WAIT_CHUNKS

    def wait_many(j, carry):
        make_copy(0, 0, WAIT_CHUNKS * ROW_GROUP).wait()
        return carry

    def wait_one(j, carry):
        make_copy(0, 0, ROW_GROUP).wait()
        return carry

    lax.fori_loop(0, many, wait_many, 0)
    lax.fori_loop(many * WAIT_CHUNKS, n, wait_one, 0)


def _dispatch_kernel(chunk_ref, nch_ref, pstart_ref, pend_ref, pos_ref, h2_ref, xs_hbm,
                     cbuf_ref, zero_ref, zsem, sem, *, tm, bm):
    i = pl.program_id(0)
    nt = pl.num_programs(0)
    nblk = xs_hbm.shape[0] // bm
    dh = cbuf_ref.shape[-1]

    def zero_copy(e):
        start = pl.multiple_of(pend_ref[e] - bm, bm)
        return pltpu.make_async_copy(zero_ref, xs_hbm.at[pl.ds(start, bm)], zsem)

    def tail_copy(b):
        return pltpu.make_async_copy(zero_ref, xs_hbm.at[pl.ds(pl.multiple_of(b * bm, bm), bm)], zsem)

    @pl.when(i == 0)
    def _():
        zero_ref[...] = jnp.zeros(zero_ref.shape, zero_ref.dtype)
        first_unused = pend_ref[N_EXPERTS - 1] // bm

        def zissue(e, carry):
            @pl.when(pend_ref[e] > pstart_ref[e])
            def _():
                zero_copy(e).start()
            return carry

        def zwait(e, carry):
            @pl.when(pend_ref[e] > pstart_ref[e])
            def _():
                zero_copy(e).wait()
            return carry

        def tissue(b, carry):
            tail_copy(b).start()
            return carry

        def twait(b, carry):
            tail_copy(b).wait()
            return carry

        lax.fori_loop(0, N_EXPERTS, zissue, 0)
        lax.fori_loop(first_unused, nblk, tissue, 0)
        lax.fori_loop(0, N_EXPERTS, zwait, 0)
        lax.fori_loop(first_unused, nblk, twait, 0)

    pos = pos_ref[...]
    piota = lax.broadcasted_iota(I32, (GROUPED_ROWS, tm), 0)
    perm = jnp.zeros((GROUPED_ROWS, tm), F32)
    for r in range(TOP_K):
        perm = jnp.where(piota == pos[r:r + 1, :], 1.0, perm)
    perm = perm.astype(BF16)
    cur = lax.rem(i, 2)
    cbuf_ref[cur] = _pack_exact_bf16_pair(_dot(perm, h2_ref[:, 0:dh]), _dot(perm, h2_ref[:, dh:]))

    def make_copy(buf):
        def build(tile_row, buffer_row, rows):
            return pltpu.make_async_copy(cbuf_ref.at[buf, pl.ds(tile_row, rows)],
                                         xs_hbm.at[pl.ds(buffer_row, rows)], sem.at[buf])
        return build

    _start_chunks(i, chunk_ref, nch_ref, pstart_ref, make_copy(cur))

    @pl.when(i > 0)
    def _():
        _wait_chunks(i - 1, nch_ref, make_copy(1 - cur))

    @pl.when(i == nt - 1)
    def _():
        _wait_chunks(i, nch_ref, make_copy(cur))


def _dispatch(chunks, nch, pstart, pend, pos, h2, nrows, tm, bm):
    n, d = h2.shape
    grid_spec = pltpu.PrefetchScalarGridSpec(
        num_scalar_prefetch=4,
        grid=(n // tm,),
        in_specs=[
            pl.BlockSpec((TOP_K, tm), lambda i, *_: (0, i)),
            pl.BlockSpec((tm, d), lambda i, *_: (i, 0)),
        ],
        out_specs=pl.BlockSpec(memory_space=pl.ANY),
        scratch_shapes=[
            pltpu.VMEM((2, GROUPED_ROWS, d // 2), U32),
            pltpu.VMEM((bm, d // 2), U32),
            pltpu.SemaphoreType.DMA(()),
            pltpu.SemaphoreType.DMA((2,)),
        ],
    )
    return pl.pallas_call(
        functools.partial(_dispatch_kernel, tm=tm, bm=bm),
        grid_spec=grid_spec,
        out_shape=jax.ShapeDtypeStruct((nrows, d // 2), U32),
        compiler_params=_params(("arbitrary",)),
        name="moe_dispatch",
    )(chunks, nch, pstart, pend, pos, h2)


def _experts_kernel(be_ref, nu_ref, valid_ref, x_ref, wgu_ref, wd_ref, o_ref, wgu_b_ref, wd_b_ref,
                    *, bm):
    i = pl.program_id(0)
    valid = valid_ref[i]
    sub = bm // 2

    @pl.when((valid > 0) & ((i == 0) | (be_ref[i] != be_ref[jnp.maximum(i - 1, 0)])))
    def _():
        wgu_b_ref[...] = wgu_ref[...].astype(BF16)
        wd_b_ref[...] = wd_ref[...].astype(BF16)

    for r0 in (0, sub):
        rows = slice(r0, r0 + sub)

        @pl.when(valid > r0)
        def _():
            lo, hi = _unpack_bf16_pair(x_ref[rows, :])
            x = jnp.concatenate([lo.astype(BF16), hi.astype(BF16)], axis=1)
            gu = _dot(x, wgu_b_ref[...])
            act = (_silu(gu[:, 0:D_EXPERT]) * gu[:, D_EXPERT:]).astype(BF16)
            y = _dot(act, wd_b_ref[...])
            o_ref[rows, :] = _pack_bf16_pair(y[:, 0:D_MODEL // 2], y[:, D_MODEL // 2:])

        @pl.when(valid <= r0)
        def _():
            o_ref[rows, :] = jnp.zeros((sub, o_ref.shape[1]), o_ref.dtype)


def _experts(block_e, nused, valid, xs, w_gate_up_l, w_down_l, bm):
    nrows, dh = xs.shape
    d = 2 * dh
    nblk = nrows // bm
    grid_spec = pltpu.PrefetchScalarGridSpec(
        num_scalar_prefetch=3,
        grid=(nblk,),
        in_specs=[
            pl.BlockSpec((bm, dh), lambda i, be, nu, va: (jnp.minimum(i, nu[0] - 1), 0)),
            pl.BlockSpec((None, d, 2 * D_EXPERT), lambda i, be, nu, va: (be[i], 0, 0)),
            pl.BlockSpec((None, D_EXPERT, d), lambda i, be, nu, va: (be[i], 0, 0)),
        ],
        out_specs=pl.BlockSpec((bm, dh), lambda i, be, nu, va: (i, 0)),
        scratch_shapes=[
            pltpu.VMEM((d, 2 * D_EXPERT), BF16),
            pltpu.VMEM((D_EXPERT, d), BF16),
        ],
    )
    return pl.pallas_call(
        functools.partial(_experts_kernel, bm=bm),
        grid_spec=grid_spec,
        out_shape=jax.ShapeDtypeStruct((nrows, dh), U32),
        compiler_params=_params(("arbitrary",)),
        name="moe_experts",
    )(block_e, nused, valid, xs, w_gate_up_l, w_down_l)


def _combine_kernel(chunk_ref, nch_ref, pstart_ref, pos_ref, wl_ref, xs_ref, ga2_ref, gf_ref,
                    yb_hbm, o_ref, gbuf_ref, sem, *, tm):
    i = pl.program_id(0)
    nt = pl.num_programs(0)
    cur = lax.rem(i, 2)
    refs = (chunk_ref, nch_ref, pstart_ref)

    def make_copy(buf):
        def build(tile_row, buffer_row, rows):
            return pltpu.make_async_copy(yb_hbm.at[pl.ds(buffer_row, rows)],
                                         gbuf_ref.at[buf, pl.ds(tile_row, rows)], sem.at[buf])
        return build

    @pl.when(i == 0)
    def _():
        gbuf_ref[...] = jnp.zeros(gbuf_ref.shape, gbuf_ref.dtype)
        _start_chunks(0, *refs, make_copy(0))

    @pl.when(i + 1 < nt)
    def _():
        _start_chunks(i + 1, *refs, make_copy(1 - cur))

    _wait_chunks(i, nch_ref, make_copy(cur))

    lo, hi = _unpack_bf16_pair(gbuf_ref[cur])
    g = jnp.concatenate([lo.astype(BF16), hi.astype(BF16)], axis=1)
    pos = pos_ref[...]
    wl = wl_ref[...]
    liota = lax.broadcasted_iota(I32, (tm, GROUPED_ROWS), 1)
    a = jnp.zeros((tm, GROUPED_ROWS), F32)
    for r in range(TOP_K):
        a = jnp.where(liota == pos[:, r:r + 1], wl[:, r:r + 1], a)
    routed = _dot(a.astype(BF16), g)
    x2 = xs_ref[...] + ga2_ref[...] * routed
    ms = jnp.mean(x2 * x2, axis=-1, keepdims=True)
    o_ref[...] = x2 * lax.rsqrt(ms + EPS) * gf_ref[...]


def _combine(chunks, nch, pstart, pos_t, wl_t, xs_base, ga2, gfin, yb, tm):
    n, d = xs_base.shape
    ga2_spec = (pl.BlockSpec((1, d), lambda i, *_: (0, 0)) if ga2.shape[0] == 1
                else pl.BlockSpec((tm, d), lambda i, *_: (i, 0)))
    grid_spec = pltpu.PrefetchScalarGridSpec(
        num_scalar_prefetch=3,
        grid=(n // tm,),
        in_specs=[
            pl.BlockSpec((tm, TOP_K), lambda i, *_: (i, 0)),
            pl.BlockSpec((tm, TOP_K), lambda i, *_: (i, 0)),
            pl.BlockSpec((tm, d), lambda i, *_: (i, 0)),
            ga2_spec,
            pl.BlockSpec((1, d), lambda i, *_: (0, 0)),
            pl.BlockSpec(memory_space=pl.ANY),
        ],
        out_specs=pl.BlockSpec((tm, d), lambda i, *_: (i, 0)),
        scratch_shapes=[
            pltpu.VMEM((2, GROUPED_ROWS, d // 2), U32),
            pltpu.SemaphoreType.DMA((2,)),
        ],
    )
    return pl.pallas_call(
        functools.partial(_combine_kernel, tm=tm),
        grid_spec=grid_spec,
        out_shape=jax.ShapeDtypeStruct((n, d), F32),
        compiler_params=_params(("arbitrary",)),
        name="moe_combine",
    )(chunks, nch, pstart, pos_t, wl_t, xs_base, ga2, gfin, yb)


def _moe_and_final(x, ohg, oda, mods, wts, tm, bm):
    n, d = x.shape
    nt = n // tm
    ga1, sh2, sc2, ga2 = mods
    (w_out_b, g_ffn, wsgu_b, wsd_b, wr_t, rb, w_gate_up_l, w_down_l, g_final) = wts
    xs_base, h2, pos, wl, chunks, nch, cnt = _post(
        x, ohg, oda, w_out_b, ga1, g_ffn, sc2, sh2, ga2, wsgu_b, wsd_b, wr_t, rb, tm)
    counts = cnt[:, 0].astype(I32)
    padded = (counts + bm - 1) // bm * bm
    pend = jnp.cumsum(padded)
    pstart = pend - padded
    nblk = -(-(nt * GROUPED_ROWS) // bm) + N_EXPERTS
    nused = (pend[-1] // bm).astype(I32)
    blk_row = jnp.minimum(jnp.arange(nblk, dtype=I32), nused - 1) * bm
    be = jnp.sum((pend[None, :] <= blk_row[:, None]).astype(I32), axis=1)
    chunks = chunks.reshape(-1)
    nch = nch[:, 0, 0]
    xs = _dispatch(chunks, nch, pstart, pend, pos, h2, nblk * bm, tm, bm)
    region_end = jnp.sum(jnp.where(be[:, None] == jnp.arange(N_EXPERTS, dtype=I32)[None, :],
                                   (pstart + counts)[None, :], 0), axis=1)
    valid = jnp.clip(region_end - jnp.arange(nblk, dtype=I32) * bm, 0, bm)
    yb = _experts(be, nused.reshape(1), valid, xs, w_gate_up_l, w_down_l, bm)
    return _combine(chunks, nch, pstart, pos.T, wl.T, xs_base, ga2, g_final, yb, tm)


def _expand(mod, reps):
    if mod.shape[0] == 1:
        return mod
    return jnp.repeat(mod, reps, axis=0)


def kernel(x_prompt, x_sample, cache_k, cache_v, state_hgrn, c_prompt, c_sample, w_ada, b_ada,
           norm_mix, norm_ffn, norm_final, w_in, w_out, hg_lb_logits, hg_norm, da_lambda, da_norm,
           rel_bias_table, w_router, router_bias, w_gate_up, w_down, ws_gate_up, ws_down):
    depth = w_in.shape[0]
    assert depth == 1 and hg_lb_logits.shape[0] == 2
    bp, tp, d = x_prompt.shape
    bs, ts, _ = x_sample.shape
    assert bp == 1
    past = cache_k.shape[2]
    l = 0

    rows = -(-(bp + bs) // 8) * 8
    c_all = jnp.zeros((rows, d), F32).at[:bp].set(c_prompt).at[bp:bp + bs].set(c_sample)
    mod = _adaln(c_all, w_ada[l], b_ada[l])
    mod_p = [mod[0:bp, j * d:(j + 1) * d] for j in range(6)]
    mod_s = [_expand(mod[bp:bp + bs, j * d:(j + 1) * d], ts) for j in range(6)]

    w_in_b = w_in[l].astype(BF16)
    w_out_b = w_out[l].astype(BF16)
    wsgu_b = ws_gate_up[l].astype(BF16)
    wsd_b = ws_down[l].astype(BF16)
    wr_t = w_router[l].T
    g_mix = norm_mix[l].reshape(1, d)
    g_ffn = norm_ffn[l].reshape(1, d)
    g_final = norm_final.reshape(1, d)
    moe_w = (w_out_b, g_ffn, wsgu_b, wsd_b, wr_t, router_bias[l], w_gate_up[l], w_down[l], g_final)

    lam = _lam(da_lambda[l])

    t_att = min(ATT_TILE, tp)
    kk = jnp.arange(t_att, dtype=I32)[:, None]
    qq = jnp.arange(t_att, dtype=I32)[None, :]
    idx_diag = jnp.where((kk // CHUNK) <= (qq // CHUNK), _rel_bucket(kk - qq), MASK_BUCKET)
    idx_prev = _rel_bucket(kk - qq - t_att)
    bias_p = _bias_tiles(rel_bias_table, jnp.stack([idx_diag, idx_prev]).astype(I32),
                         shift_bucket=N_BUCKETS // 2 - 1)
    pad = 128
    qpos = past + jnp.arange(ts, dtype=I32)[:, None]
    idx_sp = _rel_bucket(jnp.arange(past, dtype=I32)[None, :] - qpos)
    kn = jnp.arange(pad, dtype=I32)[None, :]
    idx_sn = jnp.where(kn < ts, _rel_bucket(past + kn - qpos), MASK_BUCKET)
    bias_sp = _bias_tiles(rel_bias_table, idx_sp[None].astype(I32), shift_bucket=None)
    bias_sn = _bias_tiles(rel_bias_table, idx_sn[None].astype(I32), shift_bucket=None)

    xp = x_prompt.reshape(bp * tp, d)
    sh1, sc1, ga1, sh2, sc2, ga2 = mod_p
    assert ATT_TILE == INPROJ_TILE
    zh, qt, kf, vf, kb, vt = _inproj(xp, g_mix, sc1, sh1, w_in_b, t_att, True)
    s_zero = jnp.zeros((bp, HG_HEADS, HG_DIM, HG_DIM), F32)
    ohg_p, sp_new = _hgrn(zh, s_zero, hg_lb_logits, hg_norm[l], bp, tp, min(HGRN_CHUNK, tp))
    oda_p = _attn_prompt(kb, qt, vt, bias_p, lam, da_norm[l], t_att)
    y_p = _moe_and_final(xp, ohg_p, oda_p, (ga1, sh2, sc2, ga2), moe_w, POST_TILE, MOE_BLOCK_ROWS)
    k_prompt = kf.reshape(1, bp, tp, DA_HEADS, 2 * DA_QKDIM)
    v_prompt = vf.reshape(1, bp, tp, DA_HEADS, DA_VDIM)

    ns = bs * ts
    xs_ = x_sample.reshape(ns, d)
    sh1, sc1, ga1, sh2, sc2, ga2 = mod_s
    zh, qs, kf, vf, kb, vb = _inproj(xs_, g_mix, sc1, sh1, w_in_b, ns, False)
    ohg_s, ss_new = _hgrn(zh, state_hgrn[l], hg_lb_logits, hg_norm[l], bs, ts, ts)
    oda_s = _attn_step(qs, cache_k[l], cache_v[l], kb, vb, bias_sp, bias_sn, lam, da_norm[l], bs, ts)
    assert ns == POST_TILE
    y_s = _moe_and_final(xs_, ohg_s, oda_s, (ga1, sh2, sc2, ga2), moe_w, POST_TILE, 128)
    k_sample = kf.reshape(1, bs, ts, DA_HEADS, 2 * DA_QKDIM)
    v_sample = vf.reshape(1, bs, ts, DA_HEADS, DA_VDIM)

    return (y_p.reshape(bp, tp, d), y_s.reshape(bs, ts, d), k_prompt, v_prompt, sp_new[None],
            k_sample, v_sample, ss_new[None].astype(x_sample.dtype))
```

```python
import functools
import math

import numpy as np
import jax
import jax.numpy as jnp
from jax import lax
from jax.experimental import pallas as pl
from jax.experimental.pallas import tpu as pltpu

F32 = jnp.float32
BF16 = jnp.bfloat16
I32 = jnp.int32
U32 = jnp.uint32
HIGHEST = lax.Precision.HIGHEST

D_MODEL = 1024
CHUNK = 64
HG_HEADS = 4
HG_DIM = 128
HG_WIDTH = HG_HEADS * HG_DIM
DA_HEADS = 4
DA_VDIM = 128
DA_QKDIM = 64
DA_WIDTH = DA_HEADS * DA_VDIM
N_BUCKETS = 32
MAX_DIST = 128
N_EXPERTS = 64
TOP_K = 8
N_GROUPS = 8
GROUP_SIZE = N_EXPERTS // N_GROUPS
TOP_GROUPS = 4
D_EXPERT = 256
ROUTE_SCALE = 2.5
EPS = 1e-6
LAM_INIT = 0.8 - 0.6 * math.exp(-0.3 * 0)

LOG2E = math.log2(math.e)
HI_MASK = np.uint32(0xFFFF0000)
NEG_BIG = -1e30
MASK_BUCKET = N_BUCKETS
V7X_VMEM_LIMIT = 48 * 1024 * 1024

ATT_TILE = 512
HGRN_CHUNK = 256
INPROJ_TILE = 512
POST_TILE = 256
MOE_BLOCK_ROWS = 1024
ROW_GROUP = 8
GROUPED_ROWS = -(-(POST_TILE * TOP_K + N_EXPERTS * (ROW_GROUP - 1)) // 256) * 256
CHUNK_SLOTS = -(-(GROUPED_ROWS // ROW_GROUP) // 128) * 128
CHUNK_EXPERT_SHIFT = 24
CHUNK_UNROLL = 4
WAIT_CHUNKS = 16


def _sigmoid(x):
    return 1.0 / (1.0 + jnp.exp(-x))


def _silu(x):
    return x * _sigmoid(x)


def _dot(a, b, **kw):
    return jnp.dot(a, b, preferred_element_type=F32, **kw)


def _dot_nt(a, b, **kw):
    return lax.dot_general(a, b, (((1,), (1,)), ((), ())), preferred_element_type=F32, **kw)


def _dot_tn(a, b, **kw):
    return lax.dot_general(a, b, (((0,), (0,)), ((), ())), preferred_element_type=F32, **kw)


def _pack_bf16_pair(lo, hi):
    lo_bits = lax.bitcast_convert_type(lo.astype(BF16).astype(F32), U32)
    hi_bits = lax.bitcast_convert_type(hi.astype(BF16).astype(F32), U32)
    return (lo_bits >> 16) | (hi_bits & HI_MASK)


def _pack_exact_bf16_pair(lo, hi):
    return (lax.bitcast_convert_type(lo, U32) >> 16) | (lax.bitcast_convert_type(hi, U32) & HI_MASK)


def _unpack_bf16_pair(w):
    lo = lax.bitcast_convert_type(w << 16, F32)
    hi = lax.bitcast_convert_type(w & HI_MASK, F32)
    return lo, hi


def _params(sem, vmem=V7X_VMEM_LIMIT, flags=None):
    return pltpu.CompilerParams(dimension_semantics=sem, vmem_limit_bytes=vmem, flags=flags)


def _adaln_kernel(c_ref, w_ref, b_ref, o_ref):
    s = _silu(c_ref[...])
    o_ref[...] = _dot(s, w_ref[...], precision=HIGHEST) + b_ref[...]


def _adaln(c_all, w_ada, b_ada):
    rows, d = c_all.shape
    cols = w_ada.shape[1]
    blk = 1024
    return pl.pallas_call(
        _adaln_kernel,
        grid=(cols // blk,),
        in_specs=[
            pl.BlockSpec((rows, d), lambda j: (0, 0)),
            pl.BlockSpec((d, blk), lambda j: (0, j)),
            pl.BlockSpec((1, blk), lambda j: (0, j)),
        ],
        out_specs=pl.BlockSpec((rows, blk), lambda j: (0, j)),
        out_shape=jax.ShapeDtypeStruct((rows, cols), F32),
        compiler_params=_params(("parallel",)),
        name="adaln",
    )(c_all, w_ada, b_ada.reshape(1, cols))


def _lam_kernel(l_ref, o_ref):
    l = l_ref[...].astype(F32)
    a = jnp.sum(l[0:1] * l[1:2], axis=-1, keepdims=True)
    b = jnp.sum(l[2:3] * l[3:4], axis=-1, keepdims=True)
    lam = jnp.exp(a) - jnp.exp(b) + LAM_INIT
    o_ref[...] = jnp.broadcast_to(lam, o_ref.shape)


def _lam(da_lambda_l):
    return pl.pallas_call(
        _lam_kernel,
        out_shape=jax.ShapeDtypeStruct((8, 128), F32),
        name="lam",
    )(da_lambda_l)


def _rel_bucket(rel):
    nb = N_BUCKETS // 2
    max_exact = nb // 2
    side = jnp.where(rel > 0, nb, 0)
    n = jnp.abs(rel)
    large = max_exact + (jnp.log(jnp.maximum(n, 1).astype(F32) / max_exact)
                         / math.log(MAX_DIST / max_exact) * (nb - max_exact)).astype(I32)
    large = jnp.minimum(large, nb - 1)
    return side + jnp.where(n < max_exact, n, large)


def _bias_kernel(tab_ref, idx_ref, o_ref, *, shift_bucket):
    h = pl.program_id(0)
    idx = idx_ref[...]
    shift = tab_ref[shift_bucket, h] if shift_bucket is not None else 0.0
    acc = jnp.zeros(idx.shape, F32)
    for j in range(N_BUCKETS):
        acc = jnp.where(idx == j, (tab_ref[j, h] - shift) * LOG2E, acc)
    o_ref[...] = jnp.where(idx == MASK_BUCKET, NEG_BIG, acc)


def _bias_tiles(table, idx, *, shift_bucket):
    k, r, c = idx.shape
    return pl.pallas_call(
        functools.partial(_bias_kernel, shift_bucket=shift_bucket),
        grid=(DA_HEADS, k),
        in_specs=[
            pl.BlockSpec(memory_space=pltpu.SMEM),
            pl.BlockSpec((None, r, c), lambda h, d: (d, 0, 0)),
        ],
        out_specs=pl.BlockSpec((None, None, r, c), lambda h, d: (h, d, 0, 0)),
        out_shape=jax.ShapeDtypeStruct((DA_HEADS, k, r, c), F32),
        compiler_params=_params(("parallel", "parallel")),
        name="rel_bias",
    )(table, idx)


def _inproj_kernel(x_ref, g_ref, sc_ref, sh_ref, w_ref,
                   zh_ref, q_ref, k_ref, v_ref, kb_ref, vb_ref, *, transposed):
    x = x_ref[...]
    ms = jnp.mean(x * x, axis=-1, keepdims=True)
    h = x * lax.rsqrt(ms + EPS) * g_ref[...]
    h = h * (1.0 + sc_ref[...]) + sh_ref[...]
    hb = h.astype(BF16)
    c0 = 4 * HG_WIDTH
    zh_ref[...] = _dot(hb, w_ref[:, 0:c0])
    zq = _dot(hb, w_ref[:, c0:c0 + DA_WIDTH]) * (DA_QKDIM ** -0.5 * LOG2E)
    zk = _dot(hb, w_ref[:, c0 + DA_WIDTH:c0 + 2 * DA_WIDTH])
    k_ref[...] = zk
    kb_ref[...] = zk.astype(BF16)
    zv = _dot(hb, w_ref[:, c0 + 2 * DA_WIDTH:c0 + 3 * DA_WIDTH])
    v_ref[...] = zv
    if transposed:
        q_ref[...] = zq.T.astype(BF16)
        vb_ref[...] = zv.T.astype(BF16).reshape(vb_ref.shape)
    else:
        q_ref[...] = zq.astype(BF16)
        vb_ref[...] = zv.astype(BF16)


def _mod_spec(mod, tm):
    if mod.shape[0] == 1:
        return pl.BlockSpec((1, mod.shape[1]), lambda i: (0, 0))
    return pl.BlockSpec((tm, mod.shape[1]), lambda i: (i, 0))


def _inproj(x, g, sc, sh, w_in_b, tm, transposed):
    n, d = x.shape
    cols = w_in_b.shape[1]
    row = lambda i: (i, 0)
    if transposed:
        q_spec = pl.BlockSpec((DA_WIDTH, tm), lambda i: (0, i))
        q_shape = jax.ShapeDtypeStruct((DA_WIDTH, n), BF16)
        vb_spec = pl.BlockSpec((DA_HEADS, None, DA_VDIM, tm), lambda i: (0, i, 0, 0))
        vb_shape = jax.ShapeDtypeStruct((DA_HEADS, n // tm, DA_VDIM, tm), BF16)
    else:
        q_spec = vb_spec = pl.BlockSpec((tm, DA_WIDTH), row)
        q_shape = vb_shape = jax.ShapeDtypeStruct((n, DA_WIDTH), BF16)
    return pl.pallas_call(
        functools.partial(_inproj_kernel, transposed=transposed),
        grid=(n // tm,),
        in_specs=[
            pl.BlockSpec((tm, d), row),
            pl.BlockSpec((1, d), lambda i: (0, 0)),
            _mod_spec(sc, tm),
            _mod_spec(sh, tm),
            pl.BlockSpec((d, cols), lambda i: (0, 0)),
        ],
        out_specs=[
            pl.BlockSpec((tm, 4 * HG_WIDTH), row),
            q_spec,
            pl.BlockSpec((tm, DA_WIDTH), row),
            pl.BlockSpec((tm, DA_WIDTH), row),
            pl.BlockSpec((tm, DA_WIDTH), row),
            vb_spec,
        ],
        out_shape=[
            jax.ShapeDtypeStruct((n, 4 * HG_WIDTH), F32),
            q_shape,
            jax.ShapeDtypeStruct((n, DA_WIDTH), F32),
            jax.ShapeDtypeStruct((n, DA_WIDTH), F32),
            jax.ShapeDtypeStruct((n, DA_WIDTH), BF16),
            vb_shape,
        ],
        compiler_params=_params(("parallel",)),
        name="inproj",
    )(x, g, sc, sh, w_in_b)


def _hgrn_consts(c):
    levels = int(round(math.log2(c)))
    assert 1 << levels == c and levels >= 3
    t = np.arange(c)[:, None]
    r = np.arange(c)[None, :]
    tri = (r <= t).astype(np.float32)
    x = np.maximum(t ^ r, 1)
    lv = np.where(t == r, -1, np.where(t > r, np.floor(np.log2(x)).astype(np.int64), -2))
    return jnp.asarray(tri, dtype=BF16), jnp.asarray(lv, dtype=I32), levels


def _hgrn_kernel(zh_ref, s0_ref, lbl_ref, gain_ref, mall_ref, lv_ref,
                 o_ref, sout_ref, st_ref, b_ref, *, c, levels):
    ci = pl.program_id(1)

    @pl.when(ci == 0)
    def _():
        for h in range(HG_HEADS):
            st_ref[h] = s0_ref[h].astype(F32).T

    lbl = lbl_ref[...].astype(F32)
    mx = jnp.maximum(lbl[0:1], lbl[1:2])
    e0 = jnp.exp(lbl[0:1] - mx)
    e1 = jnp.exp(lbl[1:2] - mx)
    lb = e0 / (e0 + e1)

    xq = zh_ref[:, 0:HG_WIDTH]
    xf = zh_ref[:, HG_WIDTH:2 * HG_WIDTH]
    q = _silu(xq)
    y = lb + (1.0 - lb) * _sigmoid(xf)
    logf = jnp.log(y)
    kk = 1.0 - y

    l1 = logf.astype(BF16)
    r1 = logf - l1.astype(F32)
    l2 = r1.astype(BF16)
    l3 = (r1 - l2.astype(F32)).astype(BF16)
    tri = mall_ref[...]
    b = _dot(tri, l1) + _dot(tri, l2) + _dot(tri, l3)
    b_ref[...] = b
    trow = lax.broadcasted_iota(I32, b.shape, 0)

    def level_exponent(l):
        m = 1 << l
        later = (trow & m) != 0
        if l == 0:
            return jnp.where(later, logf, 0.0)
        if l == 1:
            below = pltpu.roll(logf, 1, 0)
            above = pltpu.roll(logf, c - 1, 0)
            low = (trow & 1) != 0
            return jnp.where(later, jnp.where(low, logf + below, logf), jnp.where(low, 0.0, above))
        mid = jnp.concatenate(
            [jnp.broadcast_to(b_ref[k * 2 * m + m - 1:k * 2 * m + m, :], (2 * m, b.shape[1]))
             for k in range(c // (2 * m))], axis=0)
        return jnp.where(later, b - mid, mid - b)

    factors = [jnp.exp(level_exponent(l)) for l in range(levels)]
    lv = lv_ref[...]
    gain = gain_ref[...].astype(F32)
    for h in range(HG_HEADS):
        sl = slice(h * HG_DIM, (h + 1) * HG_DIM)
        qh = q[:, sl]
        kh = kk[:, sl]
        ih = zh_ref[:, 2 * HG_WIDTH + h * HG_DIM:2 * HG_WIDTH + (h + 1) * HG_DIM]
        gh = zh_ref[:, 3 * HG_WIDTH + h * HG_DIM:3 * HG_WIDTH + (h + 1) * HG_DIM]
        bh = b[:, sl]
        ihb = ih.astype(BF16)
        a = jnp.where(lv == -1, _dot_nt(qh.astype(BF16), kh.astype(BF16)), 0.0)
        for l in range(levels):
            f = factors[l][:, sl]
            p = _dot_nt((qh * f).astype(BF16), (kh * f).astype(BF16))
            a = jnp.where(lv == l, p, a)
        st = st_ref[h]
        o = _dot(a.astype(BF16), ihb) + _dot_nt((qh * jnp.exp(bh)).astype(BF16), st.astype(BF16))
        bl = bh[c - 1:c, :]
        kd = (kh * jnp.exp(bl - bh)).astype(BF16)
        st_ref[h] = st * jnp.exp(bl) + _dot_tn(ihb, kd)
        ms = jnp.mean(o * o, axis=-1, keepdims=True)
        on = o * lax.rsqrt(ms + EPS) * gain
        o_ref[:, sl] = (on * _silu(gh)).astype(o_ref.dtype)

    @pl.when(ci == pl.num_programs(1) - 1)
    def _():
        for h in range(HG_HEADS):
            sout_ref[h] = st_ref[h].T.astype(sout_ref.dtype)


def _hgrn(zh, s0, lb_logits, gain, batch, seq, c):
    mall, lv, levels = _hgrn_consts(c)
    nc = seq // c
    return pl.pallas_call(
        functools.partial(_hgrn_kernel, c=c, levels=levels),
        grid=(batch, nc),
        in_specs=[
            pl.BlockSpec((c, 4 * HG_WIDTH), lambda b, i: (b * nc + i, 0)),
            pl.BlockSpec((None, HG_HEADS, HG_DIM, HG_DIM), lambda b, i: (b, 0, 0, 0)),
            pl.BlockSpec(lb_logits.shape, lambda b, i: (0, 0)),
            pl.BlockSpec((1, HG_DIM), lambda b, i: (0, 0)),
            pl.BlockSpec(mall.shape, lambda b, i: (0, 0)),
            pl.BlockSpec(lv.shape, lambda b, i: (0, 0)),
        ],
        out_specs=[
            pl.BlockSpec((c, HG_WIDTH), lambda b, i: (b * nc + i, 0)),
            pl.BlockSpec((None, HG_HEADS, HG_DIM, HG_DIM), lambda b, i: (b, 0, 0, 0)),
        ],
        out_shape=[
            jax.ShapeDtypeStruct((batch * seq, HG_WIDTH), BF16),
            jax.ShapeDtypeStruct((batch, HG_HEADS, HG_DIM, HG_DIM), F32),
        ],
        scratch_shapes=[pltpu.VMEM((HG_HEADS, HG_DIM, HG_DIM), F32),
                        pltpu.VMEM((c, HG_WIDTH), F32)],
        compiler_params=_params(("parallel", "arbitrary")),
        name="hgrn2",
    )(zh, s0, lb_logits, gain.reshape(1, HG_DIM), mall, lv)


def _attn_kernel(k_ref, qt_ref, vt_ref, bias_ref, lam_ref, gain_ref,
                 o_ref, qz_ref, m_ref, l_ref, acc_ref, s_ref, *, t):
    i = pl.program_id(1)
    qt = qt_ref[...]
    row = lax.broadcasted_iota(I32, qt.shape, 0)
    zero = jnp.zeros_like(qt)
    qz_ref[:, 0:t] = jnp.where(row < DA_QKDIM, qt, zero)
    qz_ref[:, t:2 * t] = jnp.where(row >= DA_QKDIM, qt, zero)
    m_ref[...] = jnp.full(m_ref.shape, NEG_BIG, F32)
    l_ref[...] = jnp.zeros(l_ref.shape, F32)
    acc_ref[...] = jnp.zeros(acc_ref.shape, F32)

    def scores(j, buf):
        kt = k_ref[pl.ds(pl.multiple_of(j * t, t), t), :]
        s_ref[buf] = _dot(kt, qz_ref[...])

    def consume(j, buf, bias_idx):
        s = s_ref[buf]
        if bias_idx is not None:
            b = bias_ref[bias_idx]
            s = jnp.concatenate([s[:, 0:t] + b, s[:, t:2 * t] + b], axis=1)
        m_prev = m_ref[...]
        m_new = jnp.maximum(m_prev, jnp.max(s, axis=0, keepdims=True))
        alpha = jnp.exp2(m_prev - m_new)
        pr = jnp.exp2(s - m_new)
        l_ref[...] = alpha * l_ref[...] + jnp.sum(pr, axis=0, keepdims=True)
        acc_ref[...] = alpha * acc_ref[...] + _dot(vt_ref[j], pr.astype(BF16))
        m_ref[...] = m_new

    n_far = jnp.maximum(i - 1, 0)

    @pl.when(n_far > 0)
    def _():
        scores(0, 0)

    def far_pair(p, carry):
        j = 2 * p
        scores(j + 1, 1)
        consume(j, 0, None)
        scores(jnp.minimum(j + 2, n_far - 1), 0)
        consume(j + 1, 1, None)
        return carry

    lax.fori_loop(0, n_far // 2, far_pair, 0)

    @pl.when(lax.rem(n_far, 2) == 1)
    def _():
        consume(n_far - 1, 0, None)

    @pl.when(i >= 1)
    def _():
        scores(i - 1, 0)
        scores(i, 1)
        consume(i - 1, 0, 1)
        consume(i, 1, 0)

    @pl.when(i == 0)
    def _():
        scores(i, 1)
        consume(i, 1, 0)

    lam = lam_ref[0:1, 0:1]
    l = l_ref[...]
    acc = acc_ref[...]
    o = acc[:, 0:t] / l[:, 0:t] - lam * (acc[:, t:2 * t] / l[:, t:2 * t])
    ms = jnp.mean(o * o, axis=0, keepdims=True)
    on = o * lax.rsqrt(ms + EPS) * gain_ref[...].astype(F32) * (1.0 - LAM_INIT)
    o_ref[...] = on.T.astype(o_ref.dtype)


def _attn_prompt(kb, qt, vt, bias, lam, gain, t):
    n = kb.shape[0]
    nt = n // t
    return pl.pallas_call(
        functools.partial(_attn_kernel, t=t),
        grid=(DA_HEADS, nt),
        in_specs=[
            pl.BlockSpec((n, DA_VDIM), lambda h, i: (0, h)),
            pl.BlockSpec((DA_VDIM, t), lambda h, i: (h, i)),
            pl.BlockSpec((None, nt, DA_VDIM, t), lambda h, i: (h, 0, 0, 0)),
            pl.BlockSpec((None, 2, t, t), lambda h, i: (h, 0, 0, 0)),
            pl.BlockSpec((8, 128), lambda h, i: (0, 0)),
            pl.BlockSpec((DA_VDIM, 1), lambda h, i: (0, 0)),
        ],
        out_specs=pl.BlockSpec((t, DA_VDIM), lambda h, i: (i, h)),
        out_shape=jax.ShapeDtypeStruct((n, DA_WIDTH), BF16),
        scratch_shapes=[
            pltpu.VMEM((DA_VDIM, 2 * t), BF16),
            pltpu.VMEM((1, 2 * t), F32),
            pltpu.VMEM((1, 2 * t), F32),
            pltpu.VMEM((DA_VDIM, 2 * t), F32),
            pltpu.VMEM((2, t, 2 * t), F32),
        ],
        compiler_params=_params(("parallel", "parallel")),
        name="diff_attn_prompt",
    )(kb, qt, vt, bias, lam, gain.reshape(DA_VDIM, 1))


def _attn_step_kernel(q_ref, kp_ref, vp_ref, kn_ref, vn_ref, bp_ref, bn_ref, lam_ref, gain_ref,
                      o_ref, *, tq, pad):
    q = q_ref[...]
    lane = lax.broadcasted_iota(I32, q.shape, 1)
    zero = jnp.zeros_like(q)
    qz = jnp.concatenate([jnp.where(lane < DA_QKDIM, q, zero),
                          jnp.where(lane >= DA_QKDIM, q, zero)], axis=0)
    kp = kp_ref[...].astype(BF16)
    vp = vp_ref[...].astype(BF16)
    zpad = jnp.zeros((pad - tq, DA_VDIM), BF16)
    kn = jnp.concatenate([kn_ref[...], zpad], axis=0)
    vn = jnp.concatenate([vn_ref[...], zpad], axis=0)
    bp = bp_ref[...]
    bn = bn_ref[...]
    sp = _dot_nt(qz, kp) + jnp.concatenate([bp, bp], axis=0)
    sn = _dot_nt(qz, kn) + jnp.concatenate([bn, bn], axis=0)
    m = jnp.maximum(jnp.max(sp, axis=-1, keepdims=True), jnp.max(sn, axis=-1, keepdims=True))
    pp = jnp.exp2(sp - m)
    pn = jnp.exp2(sn - m)
    l = jnp.sum(pp, axis=-1, keepdims=True) + jnp.sum(pn, axis=-1, keepdims=True)
    acc = _dot(pp.astype(BF16), vp) + _dot(pn.astype(BF16), vn)
    on = acc / l
    lam = lam_ref[0:1, 0:1]
    o = on[0:tq] - lam * on[tq:2 * tq]
    ms = jnp.mean(o * o, axis=-1, keepdims=True)
    o = o * lax.rsqrt(ms + EPS) * gain_ref[...].astype(F32) * (1.0 - LAM_INIT)
    o_ref[...] = o.astype(o_ref.dtype)


def _attn_step(qs, cache_k_l, cache_v_l, kb, vb, bias_p, bias_n, lam, gain, batch, tq):
    past = cache_k_l.shape[1]
    pad = bias_n.shape[-1]
    return pl.pallas_call(
        functools.partial(_attn_step_kernel, tq=tq, pad=pad),
        grid=(batch, DA_HEADS),
        in_specs=[
            pl.BlockSpec((tq, DA_VDIM), lambda b, h: (b, h)),
            pl.BlockSpec((None, past, DA_VDIM), lambda b, h: (b, 0, h)),
            pl.BlockSpec((None, past, DA_VDIM), lambda b, h: (b, 0, h)),
            pl.BlockSpec((tq, DA_VDIM), lambda b, h: (b, h)),
            pl.BlockSpec((tq, DA_VDIM), lambda b, h: (b, h)),
            pl.BlockSpec((None, None, tq, past), lambda b, h: (h, 0, 0, 0)),
            pl.BlockSpec((None, None, tq, pad), lambda b, h: (h, 0, 0, 0)),
            pl.BlockSpec((8, 128), lambda b, h: (0, 0)),
            pl.BlockSpec((1, DA_VDIM), lambda b, h: (0, 0)),
        ],
        out_specs=pl.BlockSpec((tq, DA_VDIM), lambda b, h: (b, h)),
        out_shape=jax.ShapeDtypeStruct((batch * tq, DA_WIDTH), BF16),
        compiler_params=_params(("parallel", "parallel")),
        name="diff_attn_step",
    )(qs, cache_k_l.reshape(batch, past, DA_WIDTH), cache_v_l.reshape(batch, past, DA_WIDTH),
      kb, vb, bias_p, bias_n, lam, gain.reshape(1, DA_VDIM))


def _post_kernel(x_ref, ohg_ref, oda_ref, wout_ref, ga1_ref, g_ref, sc_ref, sh_ref, ga2_ref,
                 wsgu_ref, wsd_ref, wrt_ref, rb_ref, tri_ref, ltri_ref,
                 xs_ref, h2_ref, pos_ref, wl_ref, chunk_ref, nch_ref, cnt_ref, carry_ref, *, tm):
    i = pl.program_id(0)

    @pl.when(i == 0)
    def _():
        carry_ref[...] = jnp.zeros(carry_ref.shape, F32)

    mix = _dot(ohg_ref[...], wout_ref[0:HG_WIDTH, :]) + _dot(oda_ref[...], wout_ref[HG_WIDTH:, :])
    x1 = x_ref[...] + ga1_ref[...] * mix
    ms = jnp.mean(x1 * x1, axis=-1, keepdims=True)
    h2 = x1 * lax.rsqrt(ms + EPS) * g_ref[...]
    h2 = h2 * (1.0 + sc_ref[...]) + sh_ref[...]
    h2b = h2.astype(BF16)
    h2_ref[...] = h2b
    gu = _dot(h2b, wsgu_ref[...])
    act = (_silu(gu[:, 0:D_EXPERT]) * gu[:, D_EXPERT:]).astype(BF16)
    xs_ref[...] = x1 + ga2_ref[...] * _dot(act, wsd_ref[...])

    logits = _dot_nt(wrt_ref[...], h2, precision=HIGHEST)
    score = _sigmoid(logits)
    sel = score + rb_ref[...]
    sub = lax.broadcasted_iota(I32, (GROUP_SIZE, tm), 0)
    gscore = []
    for g in range(N_GROUPS):
        v = sel[g * GROUP_SIZE:(g + 1) * GROUP_SIZE, :]
        m1 = jnp.max(v, axis=0, keepdims=True)
        i1 = jnp.min(jnp.where(v == m1, sub, GROUP_SIZE), axis=0, keepdims=True)
        m2 = jnp.max(jnp.where(sub == i1, -jnp.inf, v), axis=0, keepdims=True)
        gscore.append(m1 + m2)
    gsel = []
    for g in range(N_GROUPS):
        ahead = jnp.zeros((1, tm), F32)
        for g2 in range(N_GROUPS):
            if g2 == g:
                continue
            tie = 1.0 if g2 < g else 0.0
            ahead = ahead + jnp.where(gscore[g2] > gscore[g], 1.0,
                                      jnp.where(gscore[g2] == gscore[g], tie, 0.0))
        gsel.append(ahead < TOP_GROUPS)
    selm = jnp.concatenate(
        [jnp.where(gsel[g], sel[g * GROUP_SIZE:(g + 1) * GROUP_SIZE, :], -jnp.inf)
         for g in range(N_GROUPS)], axis=0)
    eio = lax.broadcasted_iota(I32, (N_EXPERTS, tm), 0)
    ahead = jnp.zeros((N_EXPERTS, tm), F32)
    for e2 in range(N_EXPERTS):
        row = selm[e2:e2 + 1, :]
        tie = jnp.where(eio > e2, 1.0, 0.0)
        ahead = ahead + jnp.where(row > selm, 1.0, jnp.where(row == selm, tie, 0.0))
    chosen = jnp.where(selm > -jnp.inf, jnp.where(ahead < TOP_K, 1.0, 0.0), 0.0)
    w = chosen * score
    wn = w / jnp.sum(w, axis=0, keepdims=True) * ROUTE_SCALE

    chb = chosen.astype(BF16)
    before = _dot(chb, tri_ref[...])
    tot = _dot(chb, jnp.ones((tm, 128), BF16))
    run = jnp.floor((tot + (ROW_GROUP - 1)) * (1.0 / ROW_GROUP)) * ROW_GROUP
    tile_base = _dot(ltri_ref[...], run.astype(BF16))
    carry = carry_ref[...]
    carry_ref[...] = carry + run
    cnt_ref[...] = carry + run
    pos = jnp.concatenate([tile_base] * (tm // 128), axis=1) + before

    widen = lambda v: jnp.concatenate([v] * (CHUNK_SLOTS // 128), axis=1)
    crow = lax.broadcasted_iota(I32, (N_EXPERTS, CHUNK_SLOTS), 1).astype(F32) * ROW_GROUP
    erow = lax.broadcasted_iota(I32, (N_EXPERTS, CHUNK_SLOTS), 0).astype(F32)
    owner = jnp.sum(jnp.where(widen(tile_base + run) <= crow, 1.0, 0.0), axis=0, keepdims=True)
    region_row = jnp.sum(jnp.where(owner == erow, widen(carry - tile_base), 0.0),
                         axis=0, keepdims=True) + crow[0:1]
    chunk_ref[...] = owner.astype(I32) * (1 << CHUNK_EXPERT_SHIFT) + region_row.astype(I32)
    nch_ref[...] = jnp.sum(run * (1.0 / ROW_GROUP), axis=0, keepdims=True).astype(I32)

    for r in range(TOP_K):
        pick = jnp.where(ahead == r, chosen, 0.0)
        pos_ref[r:r + 1, :] = jnp.sum(pick * pos, axis=0, keepdims=True).astype(I32)
        wl_ref[r:r + 1, :] = jnp.sum(pick * wn, axis=0, keepdims=True)


def _post(x, ohg, oda, w_out_b, ga1, g, sc, sh, ga2, wsgu_b, wsd_b, wr_t, rb, tm):
    n, d = x.shape
    nt = n // tm
    tri = jnp.asarray(np.triu(np.ones((tm, tm), np.float32), k=1), dtype=BF16)
    ltri = jnp.asarray(np.tril(np.ones((N_EXPERTS, N_EXPERTS), np.float32), k=-1), dtype=BF16)
    row = lambda i: (i, 0)
    col = lambda i: (0, i)
    full = lambda i: (0, 0)
    return pl.pallas_call(
        functools.partial(_post_kernel, tm=tm),
        grid=(nt,),
        in_specs=[
            pl.BlockSpec((tm, d), row),
            pl.BlockSpec((tm, HG_WIDTH), row),
            pl.BlockSpec((tm, DA_WIDTH), row),
            pl.BlockSpec(w_out_b.shape, full),
            _mod_spec(ga1, tm),
            pl.BlockSpec((1, d), full),
            _mod_spec(sc, tm),
            _mod_spec(sh, tm),
            _mod_spec(ga2, tm),
            pl.BlockSpec(wsgu_b.shape, full),
            pl.BlockSpec(wsd_b.shape, full),
            pl.BlockSpec(wr_t.shape, full),
            pl.BlockSpec((N_EXPERTS, 1), full),
            pl.BlockSpec((tm, tm), full),
            pl.BlockSpec((N_EXPERTS, N_EXPERTS), full),
        ],
        out_specs=[
            pl.BlockSpec((tm, d), row),
            pl.BlockSpec((tm, d), row),
            pl.BlockSpec((TOP_K, tm), col),
            pl.BlockSpec((TOP_K, tm), col),
            pl.BlockSpec((None, 1, CHUNK_SLOTS), lambda i: (i, 0, 0)),
            pl.BlockSpec((None, 1, 128), lambda i: (i, 0, 0)),
            pl.BlockSpec((N_EXPERTS, 128), full),
        ],
        out_shape=[
            jax.ShapeDtypeStruct((n, d), F32),
            jax.ShapeDtypeStruct((n, d), BF16),
            jax.ShapeDtypeStruct((TOP_K, n), I32),
            jax.ShapeDtypeStruct((TOP_K, n), F32),
            jax.ShapeDtypeStruct((nt, 1, CHUNK_SLOTS), I32),
            jax.ShapeDtypeStruct((nt, 1, 128), I32),
            jax.ShapeDtypeStruct((N_EXPERTS, 128), F32),
        ],
        scratch_shapes=[pltpu.VMEM((N_EXPERTS, 128), F32)],
        compiler_params=_params(("arbitrary",)),
        name="post_mix_router",
    )(x, ohg, oda, w_out_b, ga1, g, sc, sh, ga2, wsgu_b, wsd_b, wr_t, rb.reshape(N_EXPERTS, 1),
      tri, ltri)


def _start_chunks(tile, chunk_ref, nch_ref, pstart_ref, make_copy):
    n = nch_ref[tile]

    def start(c):
        word = chunk_ref[tile * CHUNK_SLOTS + c]
        expert = lax.shift_right_logical(word, CHUNK_EXPERT_SHIFT)
        region_row = word & ((1 << CHUNK_EXPERT_SHIFT) - 1)
        make_copy(pl.multiple_of(c * ROW_GROUP, ROW_GROUP),
                  pl.multiple_of(pstart_ref[expert] + region_row, ROW_GROUP), ROW_GROUP).start()

    def group(g, carry):
        for u in range(CHUNK_UNROLL):
            start(g * CHUNK_UNROLL + u)
        return carry

    def single(c, carry):
        start(c)
        return carry

    groups = n // CHUNK_UNROLL
    lax.fori_loop(0, groups, group, 0)
    lax.fori_loop(groups * CHUNK_UNROLL, n, single, 0)


def _wait_chunks(tile, nch_ref, make_copy):
    n = nch_ref[tile]
    many = n // WAIT_CHUNKS

    def wait_many(j, carry):
        make_copy(0, 0, WAIT_CHUNKS * ROW_GROUP).wait()
        return carry

    def wait_one(j, carry):
        make_copy(0, 0, ROW_GROUP).wait()
        return carry

    lax.fori_loop(0, many, wait_many, 0)
    lax.fori_loop(many * WAIT_CHUNKS, n, wait_one, 0)


def _dispatch_kernel(chunk_ref, nch_ref, pstart_ref, pend_ref, pos_ref, h2_ref, xs_hbm,
                     cbuf_ref, zero_ref, zsem, sem, *, tm, bm):
    i = pl.program_id(0)
    nt = pl.num_programs(0)
    nblk = xs_hbm.shape[0] // bm
    dh = cbuf_ref.shape[-1]

    def zero_copy(e):
        start = pl.multiple_of(pend_ref[e] - bm, bm)
        return pltpu.make_async_copy(zero_ref, xs_hbm.at[pl.ds(start, bm)], zsem)

    def tail_copy(b):
        return pltpu.make_async_copy(zero_ref, xs_hbm.at[pl.ds(pl.multiple_of(b * bm, bm), bm)], zsem)

    @pl.when(i == 0)
    def _():
        zero_ref[...] = jnp.zeros(zero_ref.shape, zero_ref.dtype)
        first_unused = pend_ref[N_EXPERTS - 1] // bm

        def zissue(e, carry):
            @pl.when(pend_ref[e] > pstart_ref[e])
            def _():
                zero_copy(e).start()
            return carry

        def zwait(e, carry):
            @pl.when(pend_ref[e] > pstart_ref[e])
            def _():
                zero_copy(e).wait()
            return carry

        def tissue(b, carry):
            tail_copy(b).start()
            return carry

        def twait(b, carry):
            tail_copy(b).wait()
            return carry

        lax.fori_loop(0, N_EXPERTS, zissue, 0)
        lax.fori_loop(first_unused, nblk, tissue, 0)
        lax.fori_loop(0, N_EXPERTS, zwait, 0)
        lax.fori_loop(first_unused, nblk, twait, 0)

    pos = pos_ref[...]
    piota = lax.broadcasted_iota(I32, (GROUPED_ROWS, tm), 0)
    perm = jnp.zeros((GROUPED_ROWS, tm), F32)
    for r in range(TOP_K):
        perm = jnp.where(piota == pos[r:r + 1, :], 1.0, perm)
    perm = perm.astype(BF16)
    cur = lax.rem(i, 2)
    cbuf_ref[cur] = _pack_exact_bf16_pair(_dot(perm, h2_ref[:, 0:dh]), _dot(perm, h2_ref[:, dh:]))

    def make_copy(buf):
        def build(tile_row, buffer_row, rows):
            return pltpu.make_async_copy(cbuf_ref.at[buf, pl.ds(tile_row, rows)],
                                         xs_hbm.at[pl.ds(buffer_row, rows)], sem.at[buf])
        return build

    _start_chunks(i, chunk_ref, nch_ref, pstart_ref, make_copy(cur))

    @pl.when(i > 0)
    def _():
        _wait_chunks(i - 1, nch_ref, make_copy(1 - cur))

    @pl.when(i == nt - 1)
    def _():
        _wait_chunks(i, nch_ref, make_copy(cur))


def _dispatch(chunks, nch, pstart, pend, pos, h2, nrows, tm, bm):
    n, d = h2.shape
    grid_spec = pltpu.PrefetchScalarGridSpec(
        num_scalar_prefetch=4,
        grid=(n // tm,),
        in_specs=[
            pl.BlockSpec((TOP_K, tm), lambda i, *_: (0, i)),
            pl.BlockSpec((tm, d), lambda i, *_: (i, 0)),
        ],
        out_specs=pl.BlockSpec(memory_space=pl.ANY),
        scratch_shapes=[
            pltpu.VMEM((2, GROUPED_ROWS, d // 2), U32),
            pltpu.VMEM((bm, d // 2), U32),
            pltpu.SemaphoreType.DMA(()),
            pltpu.SemaphoreType.DMA((2,)),
        ],
    )
    return pl.pallas_call(
        functools.partial(_dispatch_kernel, tm=tm, bm=bm),
        grid_spec=grid_spec,
        out_shape=jax.ShapeDtypeStruct((nrows, d // 2), U32),
        compiler_params=_params(("arbitrary",)),
        name="moe_dispatch",
    )(chunks, nch, pstart, pend, pos, h2)


def _experts_kernel(be_ref, nu_ref, valid_ref, x_ref, wgu_ref, wd_ref, o_ref, wgu_b_ref, wd_b_ref,
                    *, bm):
    i = pl.program_id(0)
    valid = valid_ref[i]
    sub = bm // 2

    @pl.when((valid > 0) & ((i == 0) | (be_ref[i] != be_ref[jnp.maximum(i - 1, 0)])))
    def _():
        wgu_b_ref[...] = wgu_ref[...].astype(BF16)
        wd_b_ref[...] = wd_ref[...].astype(BF16)

    for r0 in (0, sub):
        rows = slice(r0, r0 + sub)

        @pl.when(valid > r0)
        def _():
            lo, hi = _unpack_bf16_pair(x_ref[rows, :])
            x = jnp.concatenate([lo.astype(BF16), hi.astype(BF16)], axis=1)
            gu = _dot(x, wgu_b_ref[...])
            act = (_silu(gu[:, 0:D_EXPERT]) * gu[:, D_EXPERT:]).astype(BF16)
            y = _dot(act, wd_b_ref[...])
            o_ref[rows, :] = _pack_bf16_pair(y[:, 0:D_MODEL // 2], y[:, D_MODEL // 2:])

        @pl.when(valid <= r0)
        def _():
            o_ref[rows, :] = jnp.zeros((sub, o_ref.shape[1]), o_ref.dtype)


def _experts(block_e, nused, valid, xs, w_gate_up_l, w_down_l, bm):
    nrows, dh = xs.shape
    d = 2 * dh
    nblk = nrows // bm
    grid_spec = pltpu.PrefetchScalarGridSpec(
        num_scalar_prefetch=3,
        grid=(nblk,),
        in_specs=[
            pl.BlockSpec((bm, dh), lambda i, be, nu, va: (jnp.minimum(i, nu[0] - 1), 0)),
            pl.BlockSpec((None, d, 2 * D_EXPERT), lambda i, be, nu, va: (be[i], 0, 0)),
            pl.BlockSpec((None, D_EXPERT, d), lambda i, be, nu, va: (be[i], 0, 0)),
        ],
        out_specs=pl.BlockSpec((bm, dh), lambda i, be, nu, va: (i, 0)),
        scratch_shapes=[
            pltpu.VMEM((d, 2 * D_EXPERT), BF16),
            pltpu.VMEM((D_EXPERT, d), BF16),
        ],
    )
    return pl.pallas_call(
        functools.partial(_experts_kernel, bm=bm),
        grid_spec=grid_spec,
        out_shape=jax.ShapeDtypeStruct((nrows, dh), U32),
        compiler_params=_params(("arbitrary",)),
        name="moe_experts",
    )(block_e, nused, valid, xs, w_gate_up_l, w_down_l)


def _combine_kernel(chunk_ref, nch_ref, pstart_ref, pos_ref, wl_ref, xs_ref, ga2_ref, gf_ref,
                    yb_hbm, o_ref, gbuf_ref, sem, *, tm):
    i = pl.program_id(0)
    nt = pl.num_programs(0)
    cur = lax.rem(i, 2)
    refs = (chunk_ref, nch_ref, pstart_ref)

    def make_copy(buf):
        def build(tile_row, buffer_row, rows):
            return pltpu.make_async_copy(yb_hbm.at[pl.ds(buffer_row, rows)],
                                         gbuf_ref.at[buf, pl.ds(tile_row, rows)], sem.at[buf])
        return build

    @pl.when(i == 0)
    def _():
        gbuf_ref[...] = jnp.zeros(gbuf_ref.shape, gbuf_ref.dtype)
        _start_chunks(0, *refs, make_copy(0))

    @pl.when(i + 1 < nt)
    def _():
        _start_chunks(i + 1, *refs, make_copy(1 - cur))

    _wait_chunks(i, nch_ref, make_copy(cur))

    lo, hi = _unpack_bf16_pair(gbuf_ref[cur])
    g = jnp.concatenate([lo.astype(BF16), hi.astype(BF16)], axis=1)
    pos = pos_ref[...]
    wl = wl_ref[...]
    liota = lax.broadcasted_iota(I32, (tm, GROUPED_ROWS), 1)
    a = jnp.zeros((tm, GROUPED_ROWS), F32)
    for r in range(TOP_K):
        a = jnp.where(liota == pos[:, r:r + 1], wl[:, r:r + 1], a)
    routed = _dot(a.astype(BF16), g)
    x2 = xs_ref[...] + ga2_ref[...] * routed
    ms = jnp.mean(x2 * x2, axis=-1, keepdims=True)
    o_ref[...] = x2 * lax.rsqrt(ms + EPS) * gf_ref[...]


def _combine(chunks, nch, pstart, pos_t, wl_t, xs_base, ga2, gfin, yb, tm):
    n, d = xs_base.shape
    ga2_spec = (pl.BlockSpec((1, d), lambda i, *_: (0, 0)) if ga2.shape[0] == 1
                else pl.BlockSpec((tm, d), lambda i, *_: (i, 0)))
    grid_spec = pltpu.PrefetchScalarGridSpec(
        num_scalar_prefetch=3,
        grid=(n // tm,),
        in_specs=[
            pl.BlockSpec((tm, TOP_K), lambda i, *_: (i, 0)),
            pl.BlockSpec((tm, TOP_K), lambda i, *_: (i, 0)),
            pl.BlockSpec((tm, d), lambda i, *_: (i, 0)),
            ga2_spec,
            pl.BlockSpec((1, d), lambda i, *_: (0, 0)),
            pl.BlockSpec(memory_space=pl.ANY),
        ],
        out_specs=pl.BlockSpec((tm, d), lambda i, *_: (i, 0)),
        scratch_shapes=[
            pltpu.VMEM((2, GROUPED_ROWS, d // 2), U32),
            pltpu.SemaphoreType.DMA((2,)),
        ],
    )
    return pl.pallas_call(
        functools.partial(_combine_kernel, tm=tm),
        grid_spec=grid_spec,
        out_shape=jax.ShapeDtypeStruct((n, d), F32),
        compiler_params=_params(("arbitrary",)),
        name="moe_combine",
    )(chunks, nch, pstart, pos_t, wl_t, xs_base, ga2, gfin, yb)


def _moe_and_final(x, ohg, oda, mods, wts, tm, bm):
    n, d = x.shape
    nt = n // tm
    ga1, sh2, sc2, ga2 = mods
    (w_out_b, g_ffn, wsgu_b, wsd_b, wr_t, rb, w_gate_up_l, w_down_l, g_final) = wts
    xs_base, h2, pos, wl, chunks, nch, cnt = _post(
        x, ohg, oda, w_out_b, ga1, g_ffn, sc2, sh2, ga2, wsgu_b, wsd_b, wr_t, rb, tm)
    counts = cnt[:, 0].astype(I32)
    padded = (counts + bm - 1) // bm * bm
    pend = jnp.cumsum(padded)
    pstart = pend - padded
    nblk = -(-(nt * GROUPED_ROWS) // bm) + N_EXPERTS
    nused = (pend[-1] // bm).astype(I32)
    blk_row = jnp.minimum(jnp.arange(nblk, dtype=I32), nused - 1) * bm
    be = jnp.sum((pend[None, :] <= blk_row[:, None]).astype(I32), axis=1)
    chunks = chunks.reshape(-1)
    nch = nch[:, 0, 0]
    xs = _dispatch(chunks, nch, pstart, pend, pos, h2, nblk * bm, tm, bm)
    region_end = jnp.sum(jnp.where(be[:, None] == jnp.arange(N_EXPERTS, dtype=I32)[None, :],
                                   (pstart + counts)[None, :], 0), axis=1)
    valid = jnp.clip(region_end - jnp.arange(nblk, dtype=I32) * bm, 0, bm)
    yb = _experts(be, nused.reshape(1), valid, xs, w_gate_up_l, w_down_l, bm)
    return _combine(chunks, nch, pstart, pos.T, wl.T, xs_base, ga2, g_final, yb, tm)


def _expand(mod, reps):
    if mod.shape[0] == 1:
        return mod
    return jnp.repeat(mod, reps, axis=0)


def kernel(x_prompt, x_sample, cache_k, cache_v, state_hgrn, c_prompt, c_sample, w_ada, b_ada,
           norm_mix, norm_ffn, norm_final, w_in, w_out, hg_lb_logits, hg_norm, da_lambda, da_norm,
           rel_bias_table, w_router, router_bias, w_gate_up, w_down, ws_gate_up, ws_down):
    depth = w_in.shape[0]
    assert depth == 1 and hg_lb_logits.shape[0] == 2
    bp, tp, d = x_prompt.shape
    bs, ts, _ = x_sample.shape
    assert bp == 1
    past = cache_k.shape[2]
    l = 0

    rows = -(-(bp + bs) // 8) * 8
    c_all = jnp.zeros((rows, d), F32).at[:bp].set(c_prompt).at[bp:bp + bs].set(c_sample)
    mod = _adaln(c_all, w_ada[l], b_ada[l])
    mod_p = [mod[0:bp, j * d:(j + 1) * d] for j in range(6)]
    mod_s = [_expand(mod[bp:bp + bs, j * d:(j + 1) * d], ts) for j in range(6)]

    w_in_b = w_in[l].astype(BF16)
    w_out_b = w_out[l].astype(BF16)
    wsgu_b = ws_gate_up[l].astype(BF16)
    wsd_b = ws_down[l].astype(BF16)
    wr_t = w_router[l].T
    g_mix = norm_mix[l].reshape(1, d)
    g_ffn = norm_ffn[l].reshape(1, d)
    g_final = norm_final.reshape(1, d)
    moe_w = (w_out_b, g_ffn, wsgu_b, wsd_b, wr_t, router_bias[l], w_gate_up[l], w_down[l], g_final)

    lam = _lam(da_lambda[l])

    t_att = min(ATT_TILE, tp)
    kk = jnp.arange(t_att, dtype=I32)[:, None]
    qq = jnp.arange(t_att, dtype=I32)[None, :]
    idx_diag = jnp.where((kk // CHUNK) <= (qq // CHUNK), _rel_bucket(kk - qq), MASK_BUCKET)
    idx_prev = _rel_bucket(kk - qq - t_att)
    bias_p = _bias_tiles(rel_bias_table, jnp.stack([idx_diag, idx_prev]).astype(I32),
                         shift_bucket=N_BUCKETS // 2 - 1)
    pad = 128
    qpos = past + jnp.arange(ts, dtype=I32)[:, None]
    idx_sp = _rel_bucket(jnp.arange(past, dtype=I32)[None, :] - qpos)
    kn = jnp.arange(pad, dtype=I32)[None, :]
    idx_sn = jnp.where(kn < ts, _rel_bucket(past + kn - qpos), MASK_BUCKET)
    bias_sp = _bias_tiles(rel_bias_table, idx_sp[None].astype(I32), shift_bucket=None)
    bias_sn = _bias_tiles(rel_bias_table, idx_sn[None].astype(I32), shift_bucket=None)

    xp = x_prompt.reshape(bp * tp, d)
    sh1, sc1, ga1, sh2, sc2, ga2 = mod_p
    assert ATT_TILE == INPROJ_TILE
    zh, qt, kf, vf, kb, vt = _inproj(xp, g_mix, sc1, sh1, w_in_b, t_att, True)
    s_zero = jnp.zeros((bp, HG_HEADS, HG_DIM, HG_DIM), F32)
    ohg_p, sp_new = _hgrn(zh, s_zero, hg_lb_logits, hg_norm[l], bp, tp, min(HGRN_CHUNK, tp))
    oda_p = _attn_prompt(kb, qt, vt, bias_p, lam, da_norm[l], t_att)
    y_p = _moe_and_final(xp, ohg_p, oda_p, (ga1, sh2, sc2, ga2), moe_w, POST_TILE, MOE_BLOCK_ROWS)
    k_prompt = kf.reshape(1, bp, tp, DA_HEADS, 2 * DA_QKDIM)
    v_prompt = vf.reshape(1, bp, tp, DA_HEADS, DA_VDIM)

    ns = bs * ts
    xs_ = x_sample.reshape(ns, d)
    sh1, sc1, ga1, sh2, sc2, ga2 = mod_s
    zh, qs, kf, vf, kb, vb = _inproj(xs_, g_mix, sc1, sh1, w_in_b, ns, False)
    ohg_s, ss_new = _hgrn(zh, state_hgrn[l], hg_lb_logits, hg_norm[l], bs, ts, ts)
    oda_s = _attn_step(qs, cache_k[l], cache_v[l], kb, vb, bias_sp, bias_sn, lam, da_norm[l], bs, ts)
    assert ns == POST_TILE
    y_s = _moe_and_final(xs_, ohg_s, oda_s, (ga1, sh2, sc2, ga2), moe_w, POST_TILE, 128)
    k_sample = kf.reshape(1, bs, ts, DA_HEADS, 2 * DA_QKDIM)
    v_sample = vf.reshape(1, bs, ts, DA_HEADS, DA_VDIM)

    return (y_p.reshape(bp, tp, d), y_s.reshape(bs, ts, d), k_prompt, v_prompt, sp_new[None],
            k_sample, v_sample, ss_new[None].astype(x_sample.dtype))
```

```python
import functools
import math

import numpy as np
import jax
import jax.numpy as jnp
from jax import lax
from jax.experimental import pallas as pl
from jax.experimental.pallas import tpu as pltpu

F32 = jnp.float32
BF16 = jnp.bfloat16
I32 = jnp.int32
U32 = jnp.uint32
HIGHEST = lax.Precision.HIGHEST

D_MODEL = 1024
CHUNK = 64
HG_HEADS = 4
HG_DIM = 128
HG_WIDTH = HG_HEADS * HG_DIM
DA_HEADS = 4
DA_VDIM = 128
DA_QKDIM = 64
DA_WIDTH = DA_HEADS * DA_VDIM
N_BUCKETS = 32
MAX_DIST = 128
N_EXPERTS = 64
TOP_K = 8
N_GROUPS = 8
GROUP_SIZE = N_EXPERTS // N_GROUPS
TOP_GROUPS = 4
D_EXPERT = 256
ROUTE_SCALE = 2.5
EPS = 1e-6
LAM_INIT = 0.8 - 0.6 * math.exp(-0.3 * 0)

LOG2E = math.log2(math.e)
HI_MASK = np.uint32(0xFFFF0000)
NEG_BIG = -1e30
MASK_BUCKET = N_BUCKETS
V7X_VMEM_LIMIT = 48 * 1024 * 1024

ATT_TILE = 512
HGRN_CHUNK = 256
INPROJ_TILE = 512
POST_TILE = 256
MOE_BLOCK_ROWS = 1024
ROW_GROUP = 8
GROUPED_ROWS = -(-(POST_TILE * TOP_K + N_EXPERTS * (ROW_GROUP - 1)) // 256) * 256
CHUNK_SLOTS = -(-(GROUPED_ROWS // ROW_GROUP) // 128) * 128
CHUNK_EXPERT_SHIFT = 24
CHUNK_UNROLL = 4
WAIT_CHUNKS = 16


def _sigmoid(x):
    return 1.0 / (1.0 + jnp.exp(-x))


def _silu(x):
    return x * _sigmoid(x)


def _dot(a, b, **kw):
    return jnp.dot(a, b, preferred_element_type=F32, **kw)


def _dot_nt(a, b, **kw):
    return lax.dot_general(a, b, (((1,), (1,)), ((), ())), preferred_element_type=F32, **kw)


def _dot_tn(a, b, **kw):
    return lax.dot_general(a, b, (((0,), (0,)), ((), ())), preferred_element_type=F32, **kw)


def _pack_bf16_pair(lo, hi):
    lo_bits = lax.bitcast_convert_type(lo.astype(BF16).astype(F32), U32)
    hi_bits = lax.bitcast_convert_type(hi.astype(BF16).astype(F32), U32)
    return (lo_bits >> 16) | (hi_bits & HI_MASK)


def _pack_exact_bf16_pair(lo, hi):
    return (lax.bitcast_convert_type(lo, U32) >> 16) | (lax.bitcast_convert_type(hi, U32) & HI_MASK)


def _unpack_bf16_pair(w):
    lo = lax.bitcast_convert_type(w << 16, F32)
    hi = lax.bitcast_convert_type(w & HI_MASK, F32)
    return lo, hi


def _params(sem, vmem=V7X_VMEM_LIMIT, flags=None):
    return pltpu.CompilerParams(dimension_semantics=sem, vmem_limit_bytes=vmem, flags=flags)


def _adaln_kernel(c_ref, w_ref, b_ref, o_ref):
    s = _silu(c_ref[...])
    o_ref[...] = _dot(s, w_ref[...], precision=HIGHEST) + b_ref[...]


def _adaln(c_all, w_ada, b_ada):
    rows, d = c_all.shape
    cols = w_ada.shape[1]
    blk = 1024
    return pl.pallas_call(
        _adaln_kernel,
        grid=(cols // blk,),
        in_specs=[
            pl.BlockSpec((rows, d), lambda j: (0, 0)),
            pl.BlockSpec((d, blk), lambda j: (0, j)),
            pl.BlockSpec((1, blk), lambda j: (0, j)),
        ],
        out_specs=pl.BlockSpec((rows, blk), lambda j: (0, j)),
        out_shape=jax.ShapeDtypeStruct((rows, cols), F32),
        compiler_params=_params(("parallel",)),
        name="adaln",
    )(c_all, w_ada, b_ada.reshape(1, cols))


def _lam_kernel(l_ref, o_ref):
    l = l_ref[...].astype(F32)
    a = jnp.sum(l[0:1] * l[1:2], axis=-1, keepdims=True)
    b = jnp.sum(l[2:3] * l[3:4], axis=-1, keepdims=True)
    lam = jnp.exp(a) - jnp.exp(b) + LAM_INIT
    o_ref[...] = jnp.broadcast_to(lam, o_ref.shape)


def _lam(da_lambda_l):
    return pl.pallas_call(
        _lam_kernel,
        out_shape=jax.ShapeDtypeStruct((8, 128), F32),
        name="lam",
    )(da_lambda_l)


def _rel_bucket(rel):
    nb = N_BUCKETS // 2
    max_exact = nb // 2
    side = jnp.where(rel > 0, nb, 0)
    n = jnp.abs(rel)
    large = max_exact + (jnp.log(jnp.maximum(n, 1).astype(F32) / max_exact)
                         / math.log(MAX_DIST / max_exact) * (nb - max_exact)).astype(I32)
    large = jnp.minimum(large, nb - 1)
    return side + jnp.where(n < max_exact, n, large)


def _bias_kernel(tab_ref, idx_ref, o_ref, *, shift_bucket):
    h = pl.program_id(0)
    idx = idx_ref[...]
    shift = tab_ref[shift_bucket, h] if shift_bucket is not None else 0.0
    acc = jnp.zeros(idx.shape, F32)
    for j in range(N_BUCKETS):
        acc = jnp.where(idx == j, (tab_ref[j, h] - shift) * LOG2E, acc)
    o_ref[...] = jnp.where(idx == MASK_BUCKET, NEG_BIG, acc)


def _bias_tiles(table, idx, *, shift_bucket):
    k, r, c = idx.shape
    return pl.pallas_call(
        functools.partial(_bias_kernel, shift_bucket=shift_bucket),
        grid=(DA_HEADS, k),
        in_specs=[
            pl.BlockSpec(memory_space=pltpu.SMEM),
            pl.BlockSpec((None, r, c), lambda h, d: (d, 0, 0)),
        ],
        out_specs=pl.BlockSpec((None, None, r, c), lambda h, d: (h, d, 0, 0)),
        out_shape=jax.ShapeDtypeStruct((DA_HEADS, k, r, c), F32),
        compiler_params=_params(("parallel", "parallel")),
        name="rel_bias",
    )(table, idx)


def _inproj_kernel(x_ref, g_ref, sc_ref, sh_ref, w_ref,
                   zh_ref, q_ref, k_ref, v_ref, kb_ref, vb_ref, *, transposed):
    x = x_ref[...]
    ms = jnp.mean(x * x, axis=-1, keepdims=True)
    h = x * lax.rsqrt(ms + EPS) * g_ref[...]
    h = h * (1.0 + sc_ref[...]) + sh_ref[...]
    hb = h.astype(BF16)
    c0 = 4 * HG_WIDTH
    zh_ref[...] = _dot(hb, w_ref[:, 0:c0])
    zq = _dot(hb, w_ref[:, c0:c0 + DA_WIDTH]) * (DA_QKDIM ** -0.5 * LOG2E)
    zk = _dot(hb, w_ref[:, c0 + DA_WIDTH:c0 + 2 * DA_WIDTH])
    k_ref[...] = zk
    kb_ref[...] = zk.astype(BF16)
    zv = _dot(hb, w_ref[:, c0 + 2 * DA_WIDTH:c0 + 3 * DA_WIDTH])
    v_ref[...] = zv
    if transposed:
        q_ref[...] = zq.T.astype(BF16)
        vb_ref[...] = zv.T.astype(BF16).reshape(vb_ref.shape)
    else:
        q_ref[...] = zq.astype(BF16)
        vb_ref[...] = zv.astype(BF16)


def _mod_spec(mod, tm):
    if mod.shape[0] == 1:
        return pl.BlockSpec((1, mod.shape[1]), lambda i: (0, 0))
    return pl.BlockSpec((tm, mod.shape[1]), lambda i: (i, 0))


def _inproj(x, g, sc, sh, w_in_b, tm, transposed):
    n, d = x.shape
    cols = w_in_b.shape[1]
    row = lambda i: (i, 0)
    if transposed:
        q_spec = pl.BlockSpec((DA_WIDTH, tm), lambda i: (0, i))
        q_shape = jax.ShapeDtypeStruct((DA_WIDTH, n), BF16)
        vb_spec = pl.BlockSpec((DA_HEADS, None, DA_VDIM, tm), lambda i: (0, i, 0, 0))
        vb_shape = jax.ShapeDtypeStruct((DA_HEADS, n // tm, DA_VDIM, tm), BF16)
    else:
        q_spec = vb_spec = pl.BlockSpec((tm, DA_WIDTH), row)
        q_shape = vb_shape = jax.ShapeDtypeStruct((n, DA_WIDTH), BF16)
    return pl.pallas_call(
        functools.partial(_inproj_kernel, transposed=transposed),
        grid=(n // tm,),
        in_specs=[
            pl.BlockSpec((tm, d), row),
            pl.BlockSpec((1, d), lambda i: (0, 0)),
            _mod_spec(sc, tm),
            _mod_spec(sh, tm),
            pl.BlockSpec((d, cols), lambda i: (0, 0)),
        ],
        out_specs=[
            pl.BlockSpec((tm, 4 * HG_WIDTH), row),
            q_spec,
            pl.BlockSpec((tm, DA_WIDTH), row),
            pl.BlockSpec((tm, DA_WIDTH), row),
            pl.BlockSpec((tm, DA_WIDTH), row),
            vb_spec,
        ],
        out_shape=[
            jax.ShapeDtypeStruct((n, 4 * HG_WIDTH), F32),
            q_shape,
            jax.ShapeDtypeStruct((n, DA_WIDTH), F32),
            jax.ShapeDtypeStruct((n, DA_WIDTH), F32),
            jax.ShapeDtypeStruct((n, DA_WIDTH), BF16),
            vb_shape,
        ],
        compiler_params=_params(("parallel",)),
        name="inproj",
    )(x, g, sc, sh, w_in_b)


def _hgrn_consts(c):
    levels = int(round(math.log2(c)))
    assert 1 << levels == c and levels >= 3
    t = np.arange(c)[:, None]
    r = np.arange(c)[None, :]
    tri = (r <= t).astype(np.float32)
    x = np.maximum(t ^ r, 1)
    lv = np.where(t == r, -1, np.where(t > r, np.floor(np.log2(x)).astype(np.int64), -2))
    return jnp.asarray(tri, dtype=BF16), jnp.asarray(lv, dtype=I32), levels


def _hgrn_kernel(zh_ref, s0_ref, lbl_ref, gain_ref, mall_ref, lv_ref,
                 o_ref, sout_ref, st_ref, b_ref, *, c, levels):
    ci = pl.program_id(1)

    @pl.when(ci == 0)
    def _():
        for h in range(HG_HEADS):
            st_ref[h] = s0_ref[h].astype(F32).T

    lbl = lbl_ref[...].astype(F32)
    mx = jnp.maximum(lbl[0:1], lbl[1:2])
    e0 = jnp.exp(lbl[0:1] - mx)
    e1 = jnp.exp(lbl[1:2] - mx)
    lb = e0 / (e0 + e1)

    xq = zh_ref[:, 0:HG_WIDTH]
    xf = zh_ref[:, HG_WIDTH:2 * HG_WIDTH]
    q = _silu(xq)
    y = lb + (1.0 - lb) * _sigmoid(xf)
    logf = jnp.log(y)
    kk = 1.0 - y

    l1 = logf.astype(BF16)
    r1 = logf - l1.astype(F32)
    l2 = r1.astype(BF16)
    l3 = (r1 - l2.astype(F32)).astype(BF16)
    tri = mall_ref[...]
    b = _dot(tri, l1) + _dot(tri, l2) + _dot(tri, l3)
    b_ref[...] = b
    trow = lax.broadcasted_iota(I32, b.shape, 0)

    def level_exponent(l):
        m = 1 << l
        later = (trow & m) != 0
        if l == 0:
            return jnp.where(later, logf, 0.0)
        if l == 1:
            below = pltpu.roll(logf, 1, 0)
            above = pltpu.roll(logf, c - 1, 0)
            low = (trow & 1) != 0
            return jnp.where(later, jnp.where(low, logf + below, logf), jnp.where(low, 0.0, above))
        mid = jnp.concatenate(
            [jnp.broadcast_to(b_ref[k * 2 * m + m - 1:k * 2 * m + m, :], (2 * m, b.shape[1]))
             for k in range(c // (2 * m))], axis=0)
        return jnp.where(later, b - mid, mid - b)

    factors = [jnp.exp(level_exponent(l)) for l in range(levels)]
    lv = lv_ref[...]
    gain = gain_ref[...].astype(F32)
    for h in range(HG_HEADS):
        sl = slice(h * HG_DIM, (h + 1) * HG_DIM)
        qh = q[:, sl]
        kh = kk[:, sl]
        ih = zh_ref[:, 2 * HG_WIDTH + h * HG_DIM:2 * HG_WIDTH + (h + 1) * HG_DIM]
        gh = zh_ref[:, 3 * HG_WIDTH + h * HG_DIM:3 * HG_WIDTH + (h + 1) * HG_DIM]
        bh = b[:, sl]
        ihb = ih.astype(BF16)
        a = jnp.where(lv == -1, _dot_nt(qh.astype(BF16), kh.astype(BF16)), 0.0)
        for l in range(levels):
            f = factors[l][:, sl]
            p = _dot_nt((qh * f).astype(BF16), (kh * f).astype(BF16))
            a = jnp.where(lv == l, p, a)
        st = st_ref[h]
        o = _dot(a.astype(BF16), ihb) + _dot_nt((qh * jnp.exp(bh)).astype(BF16), st.astype(BF16))
        bl = bh[c - 1:c, :]
        kd = (kh * jnp.exp(bl - bh)).astype(BF16)
        st_ref[h] = st * jnp.exp(bl) + _dot_tn(ihb, kd)
        ms = jnp.mean(o * o, axis=-1, keepdims=True)
        on = o * lax.rsqrt(ms + EPS) * gain
        o_ref[:, sl] = (on * _silu(gh)).astype(o_ref.dtype)

    @pl.when(ci == pl.num_programs(1) - 1)
    def _():
        for h in range(HG_HEADS):
            sout_ref[h] = st_ref[h].T.astype(sout_ref.dtype)


def _hgrn(zh, s0, lb_logits, gain, batch, seq, c):
    mall, lv, levels = _hgrn_consts(c)
    nc = seq // c
    return pl.pallas_call(
        functools.partial(_hgrn_kernel, c=c, levels=levels),
        grid=(batch, nc),
        in_specs=[
            pl.BlockSpec((c, 4 * HG_WIDTH), lambda b, i: (b * nc + i, 0)),
            pl.BlockSpec((None, HG_HEADS, HG_DIM, HG_DIM), lambda b, i: (b, 0, 0, 0)),
            pl.BlockSpec(lb_logits.shape, lambda b, i: (0, 0)),
            pl.BlockSpec((1, HG_DIM), lambda b, i: (0, 0)),
            pl.BlockSpec(mall.shape, lambda b, i: (0, 0)),
            pl.BlockSpec(lv.shape, lambda b, i: (0, 0)),
        ],
        out_specs=[
            pl.BlockSpec((c, HG_WIDTH), lambda b, i: (b * nc + i, 0)),
            pl.BlockSpec((None, HG_HEADS, HG_DIM, HG_DIM), lambda b, i: (b, 0, 0, 0)),
        ],
        out_shape=[
            jax.ShapeDtypeStruct((batch * seq, HG_WIDTH), BF16),
            jax.ShapeDtypeStruct((batch, HG_HEADS, HG_DIM, HG_DIM), F32),
        ],
        scratch_shapes=[pltpu.VMEM((HG_HEADS, HG_DIM, HG_DIM), F32),
                        pltpu.VMEM((c, HG_WIDTH), F32)],
        compiler_params=_params(("parallel", "arbitrary")),
        name="hgrn2",
    )(zh, s0, lb_logits, gain.reshape(1, HG_DIM), mall, lv)


def _attn_kernel(k_ref, qt_ref, vt_ref, bias_ref, lam_ref, gain_ref,
                 o_ref, qz_ref, m_ref, l_ref, acc_ref, s_ref, *, t):
    i = pl.program_id(1)
    qt = qt_ref[...]
    row = lax.broadcasted_iota(I32, qt.shape, 0)
    zero = jnp.zeros_like(qt)
    qz_ref[:, 0:t] = jnp.where(row < DA_QKDIM, qt, zero)
    qz_ref[:, t:2 * t] = jnp.where(row >= DA_QKDIM, qt, zero)
    m_ref[...] = jnp.full(m_ref.shape, NEG_BIG, F32)
    l_ref[...] = jnp.zeros(l_ref.shape, F32)
    acc_ref[...] = jnp.zeros(acc_ref.shape, F32)

    def scores(j, buf):
        kt = k_ref[pl.ds(pl.multiple_of(j * t, t), t), :]
        s_ref[buf] = _dot(kt, qz_ref[...])

    def consume(j, buf, bias_idx):
        s = s_ref[buf]
        if bias_idx is not None:
            b = bias_ref[bias_idx]
            s = jnp.concatenate([s[:, 0:t] + b, s[:, t:2 * t] + b], axis=1)
        m_prev = m_ref[...]
        m_new = jnp.maximum(m_prev, jnp.max(s, axis=0, keepdims=True))
        alpha = jnp.exp2(m_prev - m_new)
        pr = jnp.exp2(s - m_new)
        l_ref[...] = alpha * l_ref[...] + jnp.sum(pr, axis=0, keepdims=True)
        acc_ref[...] = alpha * acc_ref[...] + _dot(vt_ref[j], pr.astype(BF16))
        m_ref[...] = m_new

    n_far = jnp.maximum(i - 1, 0)

    @pl.when(n_far > 0)
    def _():
        scores(0, 0)

    def far_pair(p, carry):
        j = 2 * p
        scores(j + 1, 1)
        consume(j, 0, None)
        scores(jnp.minimum(j + 2, n_far - 1), 0)
        consume(j + 1, 1, None)
        return carry

    lax.fori_loop(0, n_far // 2, far_pair, 0)

    @pl.when(lax.rem(n_far, 2) == 1)
    def _():
        consume(n_far - 1, 0, None)

    @pl.when(i >= 1)
    def _():
        scores(i - 1, 0)
        scores(i, 1)
        consume(i - 1, 0, 1)
        consume(i, 1, 0)

    @pl.when(i == 0)
    def _():
        scores(i, 1)
        consume(i, 1, 0)

    lam = lam_ref[0:1, 0:1]
    l = l_ref[...]
    acc = acc_ref[...]
    o = acc[:, 0:t] / l[:, 0:t] - lam * (acc[:, t:2 * t] / l[:, t:2 * t])
    ms = jnp.mean(o * o, axis=0, keepdims=True)
    on = o * lax.rsqrt(ms + EPS) * gain_ref[...].astype(F32) * (1.0 - LAM_INIT)
    o_ref[...] = on.T.astype(o_ref.dtype)


def _attn_prompt(kb, qt, vt, bias, lam, gain, t):
    n = kb.shape[0]
    nt = n // t
    return pl.pallas_call(
        functools.partial(_attn_kernel, t=t),
        grid=(DA_HEADS, nt),
        in_specs=[
            pl.BlockSpec((n, DA_VDIM), lambda h, i: (0, h)),
            pl.BlockSpec((DA_VDIM, t), lambda h, i: (h, i)),
            pl.BlockSpec((None, nt, DA_VDIM, t), lambda h, i: (h, 0, 0, 0)),
            pl.BlockSpec((None, 2, t, t), lambda h, i: (h, 0, 0, 0)),
            pl.BlockSpec((8, 128), lambda h, i: (0, 0)),
            pl.BlockSpec((DA_VDIM, 1), lambda h, i: (0, 0)),
        ],
        out_specs=pl.BlockSpec((t, DA_VDIM), lambda h, i: (i, h)),
        out_shape=jax.ShapeDtypeStruct((n, DA_WIDTH), BF16),
        scratch_shapes=[
            pltpu.VMEM((DA_VDIM, 2 * t), BF16),
            pltpu.VMEM((1, 2 * t), F32),
            pltpu.VMEM((1, 2 * t), F32),
            pltpu.VMEM((DA_VDIM, 2 * t), F32),
            pltpu.VMEM((2, t, 2 * t), F32),
        ],
        compiler_params=_params(("parallel", "parallel")),
        name="diff_attn_prompt",
    )(kb, qt, vt, bias, lam, gain.reshape(DA_VDIM, 1))


def _attn_step_kernel(q_ref, kp_ref, vp_ref, kn_ref, vn_ref, bp_ref, bn_ref, lam_ref, gain_ref,
                      o_ref, *, tq, pad):
    q = q_ref[...]
    lane = lax.broadcasted_iota(I32, q.shape, 1)
    zero = jnp.zeros_like(q)
    qz = jnp.concatenate([jnp.where(lane < DA_QKDIM, q, zero),
                          jnp.where(lane >= DA_QKDIM, q, zero)], axis=0)
    kp = kp_ref[...].astype(BF16)
    vp = vp_ref[...].astype(BF16)
    zpad = jnp.zeros((pad - tq, DA_VDIM), BF16)
    kn = jnp.concatenate([kn_ref[...], zpad], axis=0)
    vn = jnp.concatenate([vn_ref[...], zpad], axis=0)
    bp = bp_ref[...]
    bn = bn_ref[...]
    sp = _dot_nt(qz, kp) + jnp.concatenate([bp, bp], axis=0)
    sn = _dot_nt(qz, kn) + jnp.concatenate([bn, bn], axis=0)
    m = jnp.maximum(jnp.max(sp, axis=-1, keepdims=True), jnp.max(sn, axis=-1, keepdims=True))
    pp = jnp.exp2(sp - m)
    pn = jnp.exp2(sn - m)
    l = jnp.sum(pp, axis=-1, keepdims=True) + jnp.sum(pn, axis=-1, keepdims=True)
    acc = _dot(pp.astype(BF16), vp) + _dot(pn.astype(BF16), vn)
    on = acc / l
    lam = lam_ref[0:1, 0:1]
    o = on[0:tq] - lam * on[tq:2 * tq]
    ms = jnp.mean(o * o, axis=-1, keepdims=True)
    o = o * lax.rsqrt(ms + EPS) * gain_ref[...].astype(F32) * (1.0 - LAM_INIT)
    o_ref[...] = o.astype(o_ref.dtype)


def _attn_step(qs, cache_k_l, cache_v_l, kb, vb, bias_p, bias_n, lam, gain, batch, tq):
    past = cache_k_l.shape[1]
    pad = bias_n.shape[-1]
    return pl.pallas_call(
        functools.partial(_attn_step_kernel, tq=tq, pad=pad),
        grid=(batch, DA_HEADS),
        in_specs=[
            pl.BlockSpec((tq, DA_VDIM), lambda b, h: (b, h)),
            pl.BlockSpec((None, past, DA_VDIM), lambda b, h: (b, 0, h)),
            pl.BlockSpec((None, past, DA_VDIM), lambda b, h: (b, 0, h)),
            pl.BlockSpec((tq, DA_VDIM), lambda b, h: (b, h)),
            pl.BlockSpec((tq, DA_VDIM), lambda b, h: (b, h)),
            pl.BlockSpec((None, None, tq, past), lambda b, h: (h, 0, 0, 0)),
            pl.BlockSpec((None, None, tq, pad), lambda b, h: (h, 0, 0, 0)),
            pl.BlockSpec((8, 128), lambda b, h: (0, 0)),
            pl.BlockSpec((1, DA_VDIM), lambda b, h: (0, 0)),
        ],
        out_specs=pl.BlockSpec((tq, DA_VDIM), lambda b, h: (b, h)),
        out_shape=jax.ShapeDtypeStruct((batch * tq, DA_WIDTH), BF16),
        compiler_params=_params(("parallel", "parallel")),
        name="diff_attn_step",
    )(qs, cache_k_l.reshape(batch, past, DA_WIDTH), cache_v_l.reshape(batch, past, DA_WIDTH),
      kb, vb, bias_p, bias_n, lam, gain.reshape(1, DA_VDIM))


def _post_kernel(xa_ref, ohga_ref, odaa_ref, xb_ref, ohgb_ref, odab_ref,
                 wout_ref, ga1_ref, g_ref, sc_ref, sh_ref, ga2_ref,
                 wsgu_ref, wsd_ref, wrt_ref, rb_ref, tri_ref, ltri_ref,
                 xs_ref, h2_ref, pos_ref, wl_ref, chunk_ref, nch_ref, cnt_ref, carry_ref, *, tm, nta):
    i = pl.program_id(0)

    @pl.when(i == 0)
    def _():
        carry_ref[...] = jnp.zeros(carry_ref.shape, F32)

    second = i >= nta
    x = jnp.where(second, xb_ref[...], xa_ref[...])
    ohg = jnp.where(second, ohgb_ref[...], ohga_ref[...])
    oda = jnp.where(second, odab_ref[...], odaa_ref[...])
    mix = _dot(ohg, wout_ref[0:HG_WIDTH, :]) + _dot(oda, wout_ref[HG_WIDTH:, :])
    x1 = x + ga1_ref[...] * mix
    ms = jnp.mean(x1 * x1, axis=-1, keepdims=True)
    h2 = x1 * lax.rsqrt(ms + EPS) * g_ref[...]
    h2 = h2 * (1.0 + sc_ref[...]) + sh_ref[...]
    h2b = h2.astype(BF16)
    h2_ref[...] = h2b
    gu = _dot(h2b, wsgu_ref[...])
    act = (_silu(gu[:, 0:D_EXPERT]) * gu[:, D_EXPERT:]).astype(BF16)
    xs_ref[...] = x1 + ga2_ref[...] * _dot(act, wsd_ref[...])

    logits = _dot_nt(wrt_ref[...], h2, precision=HIGHEST)
    score = _sigmoid(logits)
    sel = score + rb_ref[...]
    sub = lax.broadcasted_iota(I32, (GROUP_SIZE, tm), 0)
    gscore = []
    for g in range(N_GROUPS):
        v = sel[g * GROUP_SIZE:(g + 1) * GROUP_SIZE, :]
        m1 = jnp.max(v, axis=0, keepdims=True)
        i1 = jnp.min(jnp.where(v == m1, sub, GROUP_SIZE), axis=0, keepdims=True)
        m2 = jnp.max(jnp.where(sub == i1, -jnp.inf, v), axis=0, keepdims=True)
        gscore.append(m1 + m2)
    gsel = []
    for g in range(N_GROUPS):
        ahead = jnp.zeros((1, tm), F32)
        for g2 in range(N_GROUPS):
            if g2 == g:
                continue
            tie = 1.0 if g2 < g else 0.0
            ahead = ahead + jnp.where(gscore[g2] > gscore[g], 1.0,
                                      jnp.where(gscore[g2] == gscore[g], tie, 0.0))
        gsel.append(ahead < TOP_GROUPS)
    selm = jnp.concatenate(
        [jnp.where(gsel[g], sel[g * GROUP_SIZE:(g + 1) * GROUP_SIZE, :], -jnp.inf)
         for g in range(N_GROUPS)], axis=0)
    eio = lax.broadcasted_iota(I32, (N_EXPERTS, tm), 0)
    ahead = jnp.zeros((N_EXPERTS, tm), F32)
    for e2 in range(N_EXPERTS):
        row = selm[e2:e2 + 1, :]
        tie = jnp.where(eio > e2, 1.0, 0.0)
        ahead = ahead + jnp.where(row > selm, 1.0, jnp.where(row == selm, tie, 0.0))
    chosen = jnp.where(selm > -jnp.inf, jnp.where(ahead < TOP_K, 1.0, 0.0), 0.0)
    w = chosen * score
    wn = w / jnp.sum(w, axis=0, keepdims=True) * ROUTE_SCALE

    chb = chosen.astype(BF16)
    before = _dot(chb, tri_ref[...])
    tot = _dot(chb, jnp.ones((tm, 128), BF16))
    run = jnp.floor((tot + (ROW_GROUP - 1)) * (1.0 / ROW_GROUP)) * ROW_GROUP
    tile_base = _dot(ltri_ref[...], run.astype(BF16))
    carry = carry_ref[...]
    carry_ref[...] = carry + run
    cnt_ref[...] = carry + run
    pos = jnp.concatenate([tile_base] * (tm // 128), axis=1) + before

    widen = lambda v: jnp.concatenate([v] * (CHUNK_SLOTS // 128), axis=1)
    crow = lax.broadcasted_iota(I32, (N_EXPERTS, CHUNK_SLOTS), 1).astype(F32) * ROW_GROUP
    erow = lax.broadcasted_iota(I32, (N_EXPERTS, CHUNK_SLOTS), 0).astype(F32)
    owner = jnp.sum(jnp.where(widen(tile_base + run) <= crow, 1.0, 0.0), axis=0, keepdims=True)
    region_row = jnp.sum(jnp.where(owner == erow, widen(carry - tile_base), 0.0),
                         axis=0, keepdims=True) + crow[0:1]
    chunk_ref[...] = owner.astype(I32) * (1 << CHUNK_EXPERT_SHIFT) + region_row.astype(I32)
    nch_ref[...] = jnp.sum(run * (1.0 / ROW_GROUP), axis=0, keepdims=True).astype(I32)

    for r in range(TOP_K):
        pick = jnp.where(ahead == r, chosen, 0.0)
        pos_ref[r:r + 1, :] = jnp.sum(pick * pos, axis=0, keepdims=True).astype(I32)
        wl_ref[r:r + 1, :] = jnp.sum(pick * wn, axis=0, keepdims=True)


def _post(src_a, src_b, w_out_b, ga1, g, sc, sh, ga2, wsgu_b, wsd_b, wr_t, rb, tm):
    (xa, ohga, odaa), (xb, ohgb, odab) = src_a, src_b
    d = xa.shape[1]
    nta, ntb = xa.shape[0] // tm, xb.shape[0] // tm
    nt = nta + ntb
    n = nt * tm
    tri = jnp.asarray(np.triu(np.ones((tm, tm), np.float32), k=1), dtype=BF16)
    ltri = jnp.asarray(np.tril(np.ones((N_EXPERTS, N_EXPERTS), np.float32), k=-1), dtype=BF16)
    row = lambda i: (i, 0)
    row_a = lambda i: (jnp.minimum(i, nta - 1), 0)
    row_b = lambda i: (jnp.maximum(i - nta, 0), 0)
    col = lambda i: (0, i)
    full = lambda i: (0, 0)
    mod = pl.BlockSpec((None, tm, d), lambda i: (jnp.minimum(i // nta, 1), 0, 0))
    return pl.pallas_call(
        functools.partial(_post_kernel, tm=tm, nta=nta),
        grid=(nt,),
        in_specs=[
            pl.BlockSpec((tm, d), row_a),
            pl.BlockSpec((tm, HG_WIDTH), row_a),
            pl.BlockSpec((tm, DA_WIDTH), row_a),
            pl.BlockSpec((tm, d), row_b),
            pl.BlockSpec((tm, HG_WIDTH), row_b),
            pl.BlockSpec((tm, DA_WIDTH), row_b),
            pl.BlockSpec(w_out_b.shape, full),
            mod,
            pl.BlockSpec((1, d), full),
            mod,
            mod,
            mod,
            pl.BlockSpec(wsgu_b.shape, full),
            pl.BlockSpec(wsd_b.shape, full),
            pl.BlockSpec(wr_t.shape, full),
            pl.BlockSpec((N_EXPERTS, 1), full),
            pl.BlockSpec((tm, tm), full),
            pl.BlockSpec((N_EXPERTS, N_EXPERTS), full),
        ],
        out_specs=[
            pl.BlockSpec((tm, d), row),
            pl.BlockSpec((tm, d), row),
            pl.BlockSpec((TOP_K, tm), col),
            pl.BlockSpec((TOP_K, tm), col),
            pl.BlockSpec((None, 1, CHUNK_SLOTS), lambda i: (i, 0, 0)),
            pl.BlockSpec((None, 1, 128), lambda i: (i, 0, 0)),
            pl.BlockSpec((N_EXPERTS, 128), full),
        ],
        out_shape=[
            jax.ShapeDtypeStruct((n, d), F32),
            jax.ShapeDtypeStruct((n, d), BF16),
            jax.ShapeDtypeStruct((TOP_K, n), I32),
            jax.ShapeDtypeStruct((TOP_K, n), F32),
            jax.ShapeDtypeStruct((nt, 1, CHUNK_SLOTS), I32),
            jax.ShapeDtypeStruct((nt, 1, 128), I32),
            jax.ShapeDtypeStruct((N_EXPERTS, 128), F32),
        ],
        scratch_shapes=[pltpu.VMEM((N_EXPERTS, 128), F32)],
        compiler_params=_params(("arbitrary",)),
        name="post_mix_router",
    )(xa, ohga, odaa, xb, ohgb, odab, w_out_b, ga1, g, sc, sh, ga2, wsgu_b, wsd_b, wr_t,
      rb.reshape(N_EXPERTS, 1), tri, ltri)


def _start_chunks(tile, chunk_ref, nch_ref, pstart_ref, make_copy):
    n = nch_ref[tile]

    def start(c):
        word = chunk_ref[tile * CHUNK_SLOTS + c]
        expert = lax.shift_right_logical(word, CHUNK_EXPERT_SHIFT)
        region_row = word & ((1 << CHUNK_EXPERT_SHIFT) - 1)
        make_copy(pl.multiple_of(c * ROW_GROUP, ROW_GROUP),
                  pl.multiple_of(pstart_ref[expert] + region_row, ROW_GROUP), ROW_GROUP).start()

    def group(g, carry):
        for u in range(CHUNK_UNROLL):
            start(g * CHUNK_UNROLL + u)
        return carry

    def single(c, carry):
        start(c)
        return carry

    groups = n // CHUNK_UNROLL
    lax.fori_loop(0, groups, group, 0)
    lax.fori_loop(groups * CHUNK_UNROLL, n, single, 0)


def _wait_chunks(tile, nch_ref, make_copy):
    n = nch_ref[tile]
    many = n // WAIT_CHUNKS

    def wait_many(j, carry):
        make_copy(0, 0, WAIT_CHUNKS * ROW_GROUP).wait()
        return carry

    def wait_one(j, carry):
        make_copy(0, 0, ROW_GROUP).wait()
        return carry

    lax.fori_loop(0, many, wait_many, 0)
    lax.fori_loop(many * WAIT_CHUNKS, n, wait_one, 0)


def _dispatch_kernel(chunk_ref, nch_ref, pstart_ref, pend_ref, pos_ref, h2_ref, xs_hbm,
                     cbuf_ref, zero_ref, zsem, sem, *, tm, bm):
    i = pl.program_id(0)
    nt = pl.num_programs(0)
    nblk = xs_hbm.shape[0] // bm
    dh = cbuf_ref.shape[-1]

    def zero_copy(e):
        start = pl.multiple_of(pend_ref[e] - bm, bm)
        return pltpu.make_async_copy(zero_ref, xs_hbm.at[pl.ds(start, bm)], zsem)

    def tail_copy(b):
        return pltpu.make_async_copy(zero_ref, xs_hbm.at[pl.ds(pl.multiple_of(b * bm, bm), bm)], zsem)

    @pl.when(i == 0)
    def _():
        zero_ref[...] = jnp.zeros(zero_ref.shape, zero_ref.dtype)
        first_unused = pend_ref[N_EXPERTS - 1] // bm

        def zissue(e, carry):
            @pl.when(pend_ref[e] > pstart_ref[e])
            def _():
                zero_copy(e).start()
            return carry

        def zwait(e, carry):
            @pl.when(pend_ref[e] > pstart_ref[e])
            def _():
                zero_copy(e).wait()
            return carry

        def tissue(b, carry):
            tail_copy(b).start()
            return carry

        def twait(b, carry):
            tail_copy(b).wait()
            return carry

        lax.fori_loop(0, N_EXPERTS, zissue, 0)
        lax.fori_loop(first_unused, nblk, tissue, 0)
        lax.fori_loop(0, N_EXPERTS, zwait, 0)
        lax.fori_loop(first_unused, nblk, twait, 0)

    pos = pos_ref[...]
    piota = lax.broadcasted_iota(I32, (GROUPED_ROWS, tm), 0)
    perm = jnp.zeros((GROUPED_ROWS, tm), F32)
    for r in range(TOP_K):
        perm = jnp.where(piota == pos[r:r + 1, :], 1.0, perm)
    perm = perm.astype(BF16)
    cur = lax.rem(i, 2)
    cbuf_ref[cur] = _pack_exact_bf16_pair(_dot(perm, h2_ref[:, 0:dh]), _dot(perm, h2_ref[:, dh:]))

    def make_copy(buf):
        def build(tile_row, buffer_row, rows):
            return pltpu.make_async_copy(cbuf_ref.at[buf, pl.ds(tile_row, rows)],
                                         xs_hbm.at[pl.ds(buffer_row, rows)], sem.at[buf])
        return build

    _start_chunks(i, chunk_ref, nch_ref, pstart_ref, make_copy(cur))

    @pl.when(i > 0)
    def _():
        _wait_chunks(i - 1, nch_ref, make_copy(1 - cur))

    @pl.when(i == nt - 1)
    def _():
        _wait_chunks(i, nch_ref, make_copy(cur))


def _dispatch(chunks, nch, pstart, pend, pos, h2, nrows, tm, bm):
    n, d = h2.shape
    grid_spec = pltpu.PrefetchScalarGridSpec(
        num_scalar_prefetch=4,
        grid=(n // tm,),
        in_specs=[
            pl.BlockSpec((TOP_K, tm), lambda i, *_: (0, i)),
            pl.BlockSpec((tm, d), lambda i, *_: (i, 0)),
        ],
        out_specs=pl.BlockSpec(memory_space=pl.ANY),
        scratch_shapes=[
            pltpu.VMEM((2, GROUPED_ROWS, d // 2), U32),
            pltpu.VMEM((bm, d // 2), U32),
            pltpu.SemaphoreType.DMA(()),
            pltpu.SemaphoreType.DMA((2,)),
        ],
    )
    return pl.pallas_call(
        functools.partial(_dispatch_kernel, tm=tm, bm=bm),
        grid_spec=grid_spec,
        out_shape=jax.ShapeDtypeStruct((nrows, d // 2), U32),
        compiler_params=_params(("arbitrary",)),
        name="moe_dispatch",
    )(chunks, nch, pstart, pend, pos, h2)


def _experts_kernel(be_ref, nu_ref, valid_ref, x_ref, wgu_ref, wd_ref, o_ref, wgu_b_ref, wd_b_ref,
                    *, bm):
    i = pl.program_id(0)
    valid = valid_ref[i]
    sub = bm // 2

    @pl.when((valid > 0) & ((i == 0) | (be_ref[i] != be_ref[jnp.maximum(i - 1, 0)])))
    def _():
        wgu_b_ref[...] = wgu_ref[...].astype(BF16)
        wd_b_ref[...] = wd_ref[...].astype(BF16)

    for r0 in (0, sub):
        rows = slice(r0, r0 + sub)

        @pl.when(valid > r0)
        def _():
            lo, hi = _unpack_bf16_pair(x_ref[rows, :])
            x = jnp.concatenate([lo.astype(BF16), hi.astype(BF16)], axis=1)
            gu = _dot(x, wgu_b_ref[...])
            act = (_silu(gu[:, 0:D_EXPERT]) * gu[:, D_EXPERT:]).astype(BF16)
            y = _dot(act, wd_b_ref[...])
            o_ref[rows, :] = _pack_bf16_pair(y[:, 0:D_MODEL // 2], y[:, D_MODEL // 2:])

        @pl.when(valid <= r0)
        def _():
            o_ref[rows, :] = jnp.zeros((sub, o_ref.shape[1]), o_ref.dtype)


def _experts(block_e, nused, valid, xs, w_gate_up_l, w_down_l, bm):
    nrows, dh = xs.shape
    d = 2 * dh
    nblk = nrows // bm
    grid_spec = pltpu.PrefetchScalarGridSpec(
        num_scalar_prefetch=3,
        grid=(nblk,),
        in_specs=[
            pl.BlockSpec((bm, dh), lambda i, be, nu, va: (jnp.minimum(i, nu[0] - 1), 0)),
            pl.BlockSpec((None, d, 2 * D_EXPERT), lambda i, be, nu, va: (be[i], 0, 0)),
            pl.BlockSpec((None, D_EXPERT, d), lambda i, be, nu, va: (be[i], 0, 0)),
        ],
        out_specs=pl.BlockSpec((bm, dh), lambda i, be, nu, va: (i, 0)),
        scratch_shapes=[
            pltpu.VMEM((d, 2 * D_EXPERT), BF16),
            pltpu.VMEM((D_EXPERT, d), BF16),
        ],
    )
    return pl.pallas_call(
        functools.partial(_experts_kernel, bm=bm),
        grid_spec=grid_spec,
        out_shape=jax.ShapeDtypeStruct((nrows, dh), U32),
        compiler_params=_params(("arbitrary",)),
        name="moe_experts",
    )(block_e, nused, valid, xs, w_gate_up_l, w_down_l)


def _combine_kernel(chunk_ref, nch_ref, pstart_ref, pos_ref, wl_ref, xs_ref, ga2_ref, gf_ref,
                    yb_hbm, oa_ref, ob_ref, gbuf_ref, sem, *, tm, nta):
    i = pl.program_id(0)
    nt = pl.num_programs(0)
    cur = lax.rem(i, 2)
    refs = (chunk_ref, nch_ref, pstart_ref)

    def make_copy(buf):
        def build(tile_row, buffer_row, rows):
            return pltpu.make_async_copy(yb_hbm.at[pl.ds(buffer_row, rows)],
                                         gbuf_ref.at[buf, pl.ds(tile_row, rows)], sem.at[buf])
        return build

    @pl.when(i == 0)
    def _():
        gbuf_ref[...] = jnp.zeros(gbuf_ref.shape, gbuf_ref.dtype)
        _start_chunks(0, *refs, make_copy(0))

    @pl.when(i + 1 < nt)
    def _():
        _start_chunks(i + 1, *refs, make_copy(1 - cur))

    _wait_chunks(i, nch_ref, make_copy(cur))

    lo, hi = _unpack_bf16_pair(gbuf_ref[cur])
    g = jnp.concatenate([lo.astype(BF16), hi.astype(BF16)], axis=1)
    pos = pos_ref[...]
    wl = wl_ref[...]
    liota = lax.broadcasted_iota(I32, (tm, GROUPED_ROWS), 1)
    a = jnp.zeros((tm, GROUPED_ROWS), F32)
    for r in range(TOP_K):
        a = jnp.where(liota == pos[:, r:r + 1], wl[:, r:r + 1], a)
    routed = _dot(a.astype(BF16), g)
    x2 = xs_ref[...] + ga2_ref[...] * routed
    ms = jnp.mean(x2 * x2, axis=-1, keepdims=True)
    y = x2 * lax.rsqrt(ms + EPS) * gf_ref[...]

    @pl.when(i < nta)
    def _():
        oa_ref[...] = y

    @pl.when(i >= nta)
    def _():
        ob_ref[...] = y


def _combine(chunks, nch, pstart, pos_t, wl_t, xs_base, ga2, gfin, yb, tm, nta):
    n, d = xs_base.shape
    ntb = n // tm - nta
    ga2_spec = pl.BlockSpec((None, tm, d), lambda i, *_: (jnp.minimum(i // nta, 1), 0, 0))
    grid_spec = pltpu.PrefetchScalarGridSpec(
        num_scalar_prefetch=3,
        grid=(n // tm,),
        in_specs=[
            pl.BlockSpec((tm, TOP_K), lambda i, *_: (i, 0)),
            pl.BlockSpec((tm, TOP_K), lambda i, *_: (i, 0)),
            pl.BlockSpec((tm, d), lambda i, *_: (i, 0)),
            ga2_spec,
            pl.BlockSpec((1, d), lambda i, *_: (0, 0)),
            pl.BlockSpec(memory_space=pl.ANY),
        ],
        out_specs=[
            pl.BlockSpec((tm, d), lambda i, *_: (jnp.minimum(i, nta - 1), 0)),
            pl.BlockSpec((tm, d), lambda i, *_: (jnp.maximum(i - nta, 0), 0)),
        ],
        scratch_shapes=[
            pltpu.VMEM((2, GROUPED_ROWS, d // 2), U32),
            pltpu.SemaphoreType.DMA((2,)),
        ],
    )
    return pl.pallas_call(
        functools.partial(_combine_kernel, tm=tm, nta=nta),
        grid_spec=grid_spec,
        out_shape=[jax.ShapeDtypeStruct((nta * tm, d), F32), jax.ShapeDtypeStruct((ntb * tm, d), F32)],
        compiler_params=_params(("arbitrary",)),
        name="moe_combine",
    )(chunks, nch, pstart, pos_t, wl_t, xs_base, ga2, gfin, yb)


def _moe_and_final(src_a, src_b, mods, wts, tm, bm):
    nta = src_a[0].shape[0] // tm
    ga1, sh2, sc2, ga2 = mods
    (w_out_b, g_ffn, wsgu_b, wsd_b, wr_t, rb, w_gate_up_l, w_down_l, g_final) = wts
    xs_base, h2, pos, wl, chunks, nch, cnt = _post(
        src_a, src_b, w_out_b, ga1, g_ffn, sc2, sh2, ga2, wsgu_b, wsd_b, wr_t, rb, tm)
    nt = xs_base.shape[0] // tm
    counts = cnt[:, 0].astype(I32)
    padded = (counts + bm - 1) // bm * bm
    pend = jnp.cumsum(padded)
    pstart = pend - padded
    nblk = -(-(nt * GROUPED_ROWS) // bm) + N_EXPERTS
    nused = (pend[-1] // bm).astype(I32)
    blk_row = jnp.minimum(jnp.arange(nblk, dtype=I32), nused - 1) * bm
    be = jnp.sum((pend[None, :] <= blk_row[:, None]).astype(I32), axis=1)
    chunks = chunks.reshape(-1)
    nch = nch[:, 0, 0]
    xs = _dispatch(chunks, nch, pstart, pend, pos, h2, nblk * bm, tm, bm)
    region_end = jnp.sum(jnp.where(be[:, None] == jnp.arange(N_EXPERTS, dtype=I32)[None, :],
                                   (pstart + counts)[None, :], 0), axis=1)
    valid = jnp.clip(region_end - jnp.arange(nblk, dtype=I32) * bm, 0, bm)
    yb = _experts(be, nused.reshape(1), valid, xs, w_gate_up_l, w_down_l, bm)
    return _combine(chunks, nch, pstart, pos.T, wl.T, xs_base, ga2, g_final, yb, tm, nta)


def _expand(mod, reps):
    if mod.shape[0] == 1:
        return mod
    return jnp.repeat(mod, reps, axis=0)


def kernel(x_prompt, x_sample, cache_k, cache_v, state_hgrn, c_prompt, c_sample, w_ada, b_ada,
           norm_mix, norm_ffn, norm_final, w_in, w_out, hg_lb_logits, hg_norm, da_lambda, da_norm,
           rel_bias_table, w_router, router_bias, w_gate_up, w_down, ws_gate_up, ws_down):
    depth = w_in.shape[0]
    assert depth == 1 and hg_lb_logits.shape[0] == 2
    bp, tp, d = x_prompt.shape
    bs, ts, _ = x_sample.shape
    assert bp == 1
    past = cache_k.shape[2]
    l = 0

    rows = -(-(bp + bs) // 8) * 8
    c_all = jnp.zeros((rows, d), F32).at[:bp].set(c_prompt).at[bp:bp + bs].set(c_sample)
    mod = _adaln(c_all, w_ada[l], b_ada[l])
    mod_p = [mod[0:bp, j * d:(j + 1) * d] for j in range(6)]
    mod_s = [_expand(mod[bp:bp + bs, j * d:(j + 1) * d], ts) for j in range(6)]

    w_in_b = w_in[l].astype(BF16)
    w_out_b = w_out[l].astype(BF16)
    wsgu_b = ws_gate_up[l].astype(BF16)
    wsd_b = ws_down[l].astype(BF16)
    wr_t = w_router[l].T
    g_mix = norm_mix[l].reshape(1, d)
    g_ffn = norm_ffn[l].reshape(1, d)
    g_final = norm_final.reshape(1, d)
    moe_w = (w_out_b, g_ffn, wsgu_b, wsd_b, wr_t, router_bias[l], w_gate_up[l], w_down[l], g_final)

    lam = _lam(da_lambda[l])

    t_att = min(ATT_TILE, tp)
    kk = jnp.arange(t_att, dtype=I32)[:, None]
    qq = jnp.arange(t_att, dtype=I32)[None, :]
    idx_diag = jnp.where((kk // CHUNK) <= (qq // CHUNK), _rel_bucket(kk - qq), MASK_BUCKET)
    idx_prev = _rel_bucket(kk - qq - t_att)
    bias_p = _bias_tiles(rel_bias_table, jnp.stack([idx_diag, idx_prev]).astype(I32),
                         shift_bucket=N_BUCKETS // 2 - 1)
    pad = 128
    qpos = past + jnp.arange(ts, dtype=I32)[:, None]
    idx_sp = _rel_bucket(jnp.arange(past, dtype=I32)[None, :] - qpos)
    kn = jnp.arange(pad, dtype=I32)[None, :]
    idx_sn = jnp.where(kn < ts, _rel_bucket(past + kn - qpos), MASK_BUCKET)
    bias_sp = _bias_tiles(rel_bias_table, idx_sp[None].astype(I32), shift_bucket=None)
    bias_sn = _bias_tiles(rel_bias_table, idx_sn[None].astype(I32), shift_bucket=None)

    xp = x_prompt.reshape(bp * tp, d)
    sh1, sc1, ga1, sh2, sc2, ga2 = mod_p
    assert ATT_TILE == INPROJ_TILE
    zh, qt, kf, vf, kb, vt = _inproj(xp, g_mix, sc1, sh1, w_in_b, t_att, True)
    s_zero = jnp.zeros((bp, HG_HEADS, HG_DIM, HG_DIM), F32)
    ohg_p, sp_new = _hgrn(zh, s_zero, hg_lb_logits, hg_norm[l], bp, tp, min(HGRN_CHUNK, tp))
    oda_p = _attn_prompt(kb, qt, vt, bias_p, lam, da_norm[l], t_att)
    src_p = (xp, ohg_p, oda_p)
    mods_p = (ga1, sh2, sc2, ga2)
    k_prompt = kf.reshape(1, bp, tp, DA_HEADS, 2 * DA_QKDIM)
    v_prompt = vf.reshape(1, bp, tp, DA_HEADS, DA_VDIM)

    ns = bs * ts
    xs_ = x_sample.reshape(ns, d)
    sh1, sc1, ga1, sh2, sc2, ga2 = mod_s
    zh, qs, kf, vf, kb, vb = _inproj(xs_, g_mix, sc1, sh1, w_in_b, ns, False)
    ohg_s, ss_new = _hgrn(zh, state_hgrn[l], hg_lb_logits, hg_norm[l], bs, ts, ts)
    oda_s = _attn_step(qs, cache_k[l], cache_v[l], kb, vb, bias_sp, bias_sn, lam, da_norm[l], bs, ts)
    assert ns == POST_TILE
    mods = tuple(jnp.stack([jnp.broadcast_to(mp, (POST_TILE, d)), ms_])
                 for mp, ms_ in zip(mods_p, (ga1, sh2, sc2, ga2)))
    y_p, y_s = _moe_and_final(src_p, (xs_, ohg_s, oda_s), mods, moe_w, POST_TILE, MOE_BLOCK_ROWS)
    k_sample = kf.reshape(1, bs, ts, DA_HEADS, 2 * DA_QKDIM)
    v_sample = vf.reshape(1, bs, ts, DA_HEADS, DA_VDIM)

    return (y_p.reshape(bp, tp, d), y_s.reshape(bs, ts, d), k_prompt, v_prompt, sp_new[None],
            k_sample, v_sample, ss_new[None].astype(x_sample.dtype))
```

```python
import functools
import math

import numpy as np
import jax
import jax.numpy as jnp
from jax import lax
from jax.experimental import pallas as pl
from jax.experimental.pallas import tpu as pltpu

F32 = jnp.float32
BF16 = jnp.bfloat16
I32 = jnp.int32
U32 = jnp.uint32
HIGHEST = lax.Precision.HIGHEST

D_MODEL = 1024
CHUNK = 64
HG_HEADS = 4
HG_DIM = 128
HG_WIDTH = HG_HEADS * HG_DIM
DA_HEADS = 4
DA_VDIM = 128
DA_QKDIM = 64
DA_WIDTH = DA_HEADS * DA_VDIM
N_BUCKETS = 32
MAX_DIST = 128
N_EXPERTS = 64
TOP_K = 8
N_GROUPS = 8
GROUP_SIZE = N_EXPERTS // N_GROUPS
TOP_GROUPS = 4
D_EXPERT = 256
ROUTE_SCALE = 2.5
EPS = 1e-6
LAM_INIT = 0.8 - 0.6 * math.exp(-0.3 * 0)

LOG2E = math.log2(math.e)
HI_MASK = np.uint32(0xFFFF0000)
NEG_BIG = -1e30
MASK_BUCKET = N_BUCKETS
V7X_VMEM_LIMIT = 48 * 1024 * 1024

ATT_TILE = 512
HGRN_CHUNK = 256
INPROJ_TILE = 512
POST_TILE = 256
MOE_BLOCK_ROWS = 1024
ROW_GROUP = 8
GROUPED_ROWS = -(-(POST_TILE * TOP_K + N_EXPERTS * (ROW_GROUP - 1)) // 256) * 256
CHUNK_SLOTS = -(-(GROUPED_ROWS // ROW_GROUP) // 128) * 128
CHUNK_EXPERT_SHIFT = 24
CHUNK_UNROLL = 4
WAIT_CHUNKS = 16


def _sigmoid(x):
    return 1.0 / (1.0 + jnp.exp(-x))


def _silu(x):
    return x * _sigmoid(x)


def _dot(a, b, **kw):
    return jnp.dot(a, b, preferred_element_type=F32, **kw)


def _dot_nt(a, b, **kw):
    return lax.dot_general(a, b, (((1,), (1,)), ((), ())), preferred_element_type=F32, **kw)


def _dot_tn(a, b, **kw):
    return lax.dot_general(a, b, (((0,), (0,)), ((), ())), preferred_element_type=F32, **kw)


def _pack_bf16_pair(lo, hi):
    lo_bits = lax.bitcast_convert_type(lo.astype(BF16).astype(F32), U32)
    hi_bits = lax.bitcast_convert_type(hi.astype(BF16).astype(F32), U32)
    return (lo_bits >> 16) | (hi_bits & HI_MASK)


def _pack_exact_bf16_pair(lo, hi):
    return (lax.bitcast_convert_type(lo, U32) >> 16) | (lax.bitcast_convert_type(hi, U32) & HI_MASK)


def _unpack_bf16_pair(w):
    lo = lax.bitcast_convert_type(w << 16, F32)
    hi = lax.bitcast_convert_type(w & HI_MASK, F32)
    return lo, hi


def _params(sem, vmem=V7X_VMEM_LIMIT, flags=None):
    return pltpu.CompilerParams(dimension_semantics=sem, vmem_limit_bytes=vmem, flags=flags)


def _adaln_kernel(c_ref, w_ref, b_ref, o_ref):
    s = _silu(c_ref[...])
    o_ref[...] = _dot(s, w_ref[...], precision=HIGHEST) + b_ref[...]


def _adaln(c_all, w_ada, b_ada):
    rows, d = c_all.shape
    cols = w_ada.shape[1]
    blk = 1024
    return pl.pallas_call(
        _adaln_kernel,
        grid=(cols // blk,),
        in_specs=[
            pl.BlockSpec((rows, d), lambda j: (0, 0)),
            pl.BlockSpec((d, blk), lambda j: (0, j)),
            pl.BlockSpec((1, blk), lambda j: (0, j)),
        ],
        out_specs=pl.BlockSpec((rows, blk), lambda j: (0, j)),
        out_shape=jax.ShapeDtypeStruct((rows, cols), F32),
        compiler_params=_params(("parallel",)),
        name="adaln",
    )(c_all, w_ada, b_ada.reshape(1, cols))


def _lam_kernel(l_ref, o_ref):
    l = l_ref[...].astype(F32)
    a = jnp.sum(l[0:1] * l[1:2], axis=-1, keepdims=True)
    b = jnp.sum(l[2:3] * l[3:4], axis=-1, keepdims=True)
    lam = jnp.exp(a) - jnp.exp(b) + LAM_INIT
    o_ref[...] = jnp.broadcast_to(lam, o_ref.shape)


def _lam(da_lambda_l):
    return pl.pallas_call(
        _lam_kernel,
        out_shape=jax.ShapeDtypeStruct((8, 128), F32),
        name="lam",
    )(da_lambda_l)


def _rel_bucket(rel):
    nb = N_BUCKETS // 2
    max_exact = nb // 2
    side = jnp.where(rel > 0, nb, 0)
    n = jnp.abs(rel)
    large = max_exact + (jnp.log(jnp.maximum(n, 1).astype(F32) / max_exact)
                         / math.log(MAX_DIST / max_exact) * (nb - max_exact)).astype(I32)
    large = jnp.minimum(large, nb - 1)
    return side + jnp.where(n < max_exact, n, large)


def _bias_kernel(tab_ref, idx_ref, o_ref, *, shift_bucket):
    h = pl.program_id(0)
    idx = idx_ref[...]
    shift = tab_ref[shift_bucket, h] if shift_bucket is not None else 0.0
    acc = jnp.zeros(idx.shape, F32)
    for j in range(N_BUCKETS):
        acc = jnp.where(idx == j, (tab_ref[j, h] - shift) * LOG2E, acc)
    o_ref[...] = jnp.where(idx == MASK_BUCKET, NEG_BIG, acc)


def _bias_tiles(table, idx, *, shift_bucket):
    k, r, c = idx.shape
    return pl.pallas_call(
        functools.partial(_bias_kernel, shift_bucket=shift_bucket),
        grid=(DA_HEADS, k),
        in_specs=[
            pl.BlockSpec(memory_space=pltpu.SMEM),
            pl.BlockSpec((None, r, c), lambda h, d: (d, 0, 0)),
        ],
        out_specs=pl.BlockSpec((None, None, r, c), lambda h, d: (h, d, 0, 0)),
        out_shape=jax.ShapeDtypeStruct((DA_HEADS, k, r, c), F32),
        compiler_params=_params(("parallel", "parallel")),
        name="rel_bias",
    )(table, idx)


def _inproj_kernel(x_ref, g_ref, sc_ref, sh_ref, w_ref,
                   zh_ref, q_ref, k_ref, v_ref, kb_ref, vb_ref, *, transposed):
    x = x_ref[...]
    ms = jnp.mean(x * x, axis=-1, keepdims=True)
    h = x * lax.rsqrt(ms + EPS) * g_ref[...]
    h = h * (1.0 + sc_ref[...]) + sh_ref[...]
    hb = h.astype(BF16)
    c0 = 4 * HG_WIDTH
    zh_ref[...] = _dot(hb, w_ref[:, 0:c0])
    zq = _dot(hb, w_ref[:, c0:c0 + DA_WIDTH]) * (DA_QKDIM ** -0.5 * LOG2E)
    zk = _dot(hb, w_ref[:, c0 + DA_WIDTH:c0 + 2 * DA_WIDTH])
    k_ref[...] = zk
    kb_ref[...] = zk.astype(BF16)
    zv = _dot(hb, w_ref[:, c0 + 2 * DA_WIDTH:c0 + 3 * DA_WIDTH])
    v_ref[...] = zv
    if transposed:
        q_ref[...] = zq.T.astype(BF16)
        vb_ref[...] = zv.T.astype(BF16).reshape(vb_ref.shape)
    else:
        q_ref[...] = zq.astype(BF16)
        vb_ref[...] = zv.astype(BF16)


def _mod_spec(mod, tm):
    if mod.shape[0] == 1:
        return pl.BlockSpec((1, mod.shape[1]), lambda i: (0, 0))
    return pl.BlockSpec((tm, mod.shape[1]), lambda i: (i, 0))


def _inproj(x, g, sc, sh, w_in_b, tm, transposed):
    n, d = x.shape
    cols = w_in_b.shape[1]
    row = lambda i: (i, 0)
    if transposed:
        q_spec = pl.BlockSpec((DA_WIDTH, tm), lambda i: (0, i))
        q_shape = jax.ShapeDtypeStruct((DA_WIDTH, n), BF16)
        vb_spec = pl.BlockSpec((DA_HEADS, None, DA_VDIM, tm), lambda i: (0, i, 0, 0))
        vb_shape = jax.ShapeDtypeStruct((DA_HEADS, n // tm, DA_VDIM, tm), BF16)
    else:
        q_spec = vb_spec = pl.BlockSpec((tm, DA_WIDTH), row)
        q_shape = vb_shape = jax.ShapeDtypeStruct((n, DA_WIDTH), BF16)
    return pl.pallas_call(
        functools.partial(_inproj_kernel, transposed=transposed),
        grid=(n // tm,),
        in_specs=[
            pl.BlockSpec((tm, d), row),
            pl.BlockSpec((1, d), lambda i: (0, 0)),
            _mod_spec(sc, tm),
            _mod_spec(sh, tm),
            pl.BlockSpec((d, cols), lambda i: (0, 0)),
        ],
        out_specs=[
            pl.BlockSpec((tm, 4 * HG_WIDTH), row),
            q_spec,
            pl.BlockSpec((tm, DA_WIDTH), row),
            pl.BlockSpec((tm, DA_WIDTH), row),
            pl.BlockSpec((tm, DA_WIDTH), row),
            vb_spec,
        ],
        out_shape=[
            jax.ShapeDtypeStruct((n, 4 * HG_WIDTH), F32),
            q_shape,
            jax.ShapeDtypeStruct((n, DA_WIDTH), F32),
            jax.ShapeDtypeStruct((n, DA_WIDTH), F32),
            jax.ShapeDtypeStruct((n, DA_WIDTH), BF16),
            vb_shape,
        ],
        compiler_params=_params(("parallel",)),
        name="inproj",
    )(x, g, sc, sh, w_in_b)


def _hgrn_consts(c):
    levels = int(round(math.log2(c)))
    assert 1 << levels == c and levels >= 3
    t = np.arange(c)[:, None]
    r = np.arange(c)[None, :]
    tri = (r <= t).astype(np.float32)
    x = np.maximum(t ^ r, 1)
    lv = np.where(t == r, -1, np.where(t > r, np.floor(np.log2(x)).astype(np.int64), -2))
    return jnp.asarray(tri, dtype=BF16), jnp.asarray(lv, dtype=I32), levels


def _hgrn_kernel(zh_ref, s0_ref, lbl_ref, gain_ref, mall_ref, lv_ref,
                 o_ref, sout_ref, st_ref, b_ref, *, c, levels):
    ci = pl.program_id(1)

    @pl.when(ci == 0)
    def _():
        for h in range(HG_HEADS):
            st_ref[h] = s0_ref[h].astype(F32).T

    lbl = lbl_ref[...].astype(F32)
    mx = jnp.maximum(lbl[0:1], lbl[1:2])
    e0 = jnp.exp(lbl[0:1] - mx)
    e1 = jnp.exp(lbl[1:2] - mx)
    lb = e0 / (e0 + e1)

    xq = zh_ref[:, 0:HG_WIDTH]
    xf = zh_ref[:, HG_WIDTH:2 * HG_WIDTH]
    q = _silu(xq)
    y = lb + (1.0 - lb) * _sigmoid(xf)
    logf = jnp.log(y)
    kk = 1.0 - y

    l1 = logf.astype(BF16)
    r1 = logf - l1.astype(F32)
    l2 = r1.astype(BF16)
    l3 = (r1 - l2.astype(F32)).astype(BF16)
    tri = mall_ref[...]
    b = _dot(tri, l1) + _dot(tri, l2) + _dot(tri, l3)
    b_ref[...] = b
    trow = lax.broadcasted_iota(I32, b.shape, 0)

    def level_exponent(l):
        m = 1 << l
        later = (trow & m) != 0
        if l == 0:
            return jnp.where(later, logf, 0.0)
        if l == 1:
            below = pltpu.roll(logf, 1, 0)
            above = pltpu.roll(logf, c - 1, 0)
            low = (trow & 1) != 0
            return jnp.where(later, jnp.where(low, logf + below, logf), jnp.where(low, 0.0, above))
        mid = jnp.concatenate(
            [jnp.broadcast_to(b_ref[k * 2 * m + m - 1:k * 2 * m + m, :], (2 * m, b.shape[1]))
             for k in range(c // (2 * m))], axis=0)
        return jnp.where(later, b - mid, mid - b)

    factors = [jnp.exp(level_exponent(l)) for l in range(levels)]
    lv = lv_ref[...]
    gain = gain_ref[...].astype(F32)
    for h in range(HG_HEADS):
        sl = slice(h * HG_DIM, (h + 1) * HG_DIM)
        qh = q[:, sl]
        kh = kk[:, sl]
        ih = zh_ref[:, 2 * HG_WIDTH + h * HG_DIM:2 * HG_WIDTH + (h + 1) * HG_DIM]
        gh = zh_ref[:, 3 * HG_WIDTH + h * HG_DIM:3 * HG_WIDTH + (h + 1) * HG_DIM]
        bh = b[:, sl]
        ihb = ih.astype(BF16)
        a = jnp.where(lv == -1, _dot_nt(qh.astype(BF16), kh.astype(BF16)), 0.0)
        for l in range(levels):
            f = factors[l][:, sl]
            p = _dot_nt((qh * f).astype(BF16), (kh * f).astype(BF16))
            a = jnp.where(lv == l, p, a)
        st = st_ref[h]
        o = _dot(a.astype(BF16), ihb) + _dot_nt((qh * jnp.exp(bh)).astype(BF16), st.astype(BF16))
        bl = bh[c - 1:c, :]
        kd = (kh * jnp.exp(bl - bh)).astype(BF16)
        st_ref[h] = st * jnp.exp(bl) + _dot_tn(ihb, kd)
        ms = jnp.mean(o * o, axis=-1, keepdims=True)
        on = o * lax.rsqrt(ms + EPS) * gain
        o_ref[:, sl] = (on * _silu(gh)).astype(o_ref.dtype)

    @pl.when(ci == pl.num_programs(1) - 1)
    def _():
        for h in range(HG_HEADS):
            sout_ref[h] = st_ref[h].T.astype(sout_ref.dtype)


def _hgrn(zh, s0, lb_logits, gain, batch, seq, c):
    mall, lv, levels = _hgrn_consts(c)
    nc = seq // c
    return pl.pallas_call(
        functools.partial(_hgrn_kernel, c=c, levels=levels),
        grid=(batch, nc),
        in_specs=[
            pl.BlockSpec((c, 4 * HG_WIDTH), lambda b, i: (b * nc + i, 0)),
            pl.BlockSpec((None, HG_HEADS, HG_DIM, HG_DIM), lambda b, i: (b, 0, 0, 0)),
            pl.BlockSpec(lb_logits.shape, lambda b, i: (0, 0)),
            pl.BlockSpec((1, HG_DIM), lambda b, i: (0, 0)),
            pl.BlockSpec(mall.shape, lambda b, i: (0, 0)),
            pl.BlockSpec(lv.shape, lambda b, i: (0, 0)),
        ],
        out_specs=[
            pl.BlockSpec((c, HG_WIDTH), lambda b, i: (b * nc + i, 0)),
            pl.BlockSpec((None, HG_HEADS, HG_DIM, HG_DIM), lambda b, i: (b, 0, 0, 0)),
        ],
        out_shape=[
            jax.ShapeDtypeStruct((batch * seq, HG_WIDTH), BF16),
            jax.ShapeDtypeStruct((batch, HG_HEADS, HG_DIM, HG_DIM), F32),
        ],
        scratch_shapes=[pltpu.VMEM((HG_HEADS, HG_DIM, HG_DIM), F32),
                        pltpu.VMEM((c, HG_WIDTH), F32)],
        compiler_params=_params(("parallel", "arbitrary")),
        name="hgrn2",
    )(zh, s0, lb_logits, gain.reshape(1, HG_DIM), mall, lv)


def _attn_kernel(k_ref, qt_ref, vt_ref, bias_ref, lam_ref, gain_ref,
                 o_ref, qz_ref, m_ref, l_ref, acc_ref, s_ref, *, t):
    i = pl.program_id(1)
    qt = qt_ref[...]
    row = lax.broadcasted_iota(I32, qt.shape, 0)
    zero = jnp.zeros_like(qt)
    qz_ref[:, 0:t] = jnp.where(row < DA_QKDIM, qt, zero)
    qz_ref[:, t:2 * t] = jnp.where(row >= DA_QKDIM, qt, zero)
    m_ref[...] = jnp.full(m_ref.shape, NEG_BIG, F32)
    l_ref[...] = jnp.zeros(l_ref.shape, F32)
    acc_ref[...] = jnp.zeros(acc_ref.shape, F32)

    def scores(j, buf):
        kt = k_ref[pl.ds(pl.multiple_of(j * t, t), t), :]
        s_ref[buf] = _dot(kt, qz_ref[...])

    def consume(j, buf, bias_idx):
        s = s_ref[buf]
        if bias_idx is not None:
            b = bias_ref[bias_idx]
            s = jnp.concatenate([s[:, 0:t] + b, s[:, t:2 * t] + b], axis=1)
        m_prev = m_ref[...]
        m_new = jnp.maximum(m_prev, jnp.max(s, axis=0, keepdims=True))
        alpha = jnp.exp2(m_prev - m_new)
        pr = jnp.exp2(s - m_new)
        l_ref[...] = alpha * l_ref[...] + jnp.sum(pr, axis=0, keepdims=True)
        acc_ref[...] = alpha * acc_ref[...] + _dot(vt_ref[j], pr.astype(BF16))
        m_ref[...] = m_new

    n_far = jnp.maximum(i - 1, 0)

    @pl.when(n_far > 0)
    def _():
        scores(0, 0)

    def far_pair(p, carry):
        j = 2 * p
        scores(j + 1, 1)
        consume(j, 0, None)
        scores(jnp.minimum(j + 2, n_far - 1), 0)
        consume(j + 1, 1, None)
        return carry

    lax.fori_loop(0, n_far // 2, far_pair, 0)

    @pl.when(lax.rem(n_far, 2) == 1)
    def _():
        consume(n_far - 1, 0, None)

    @pl.when(i >= 1)
    def _():
        scores(i - 1, 0)
        scores(i, 1)
        consume(i - 1, 0, 1)
        consume(i, 1, 0)

    @pl.when(i == 0)
    def _():
        scores(i, 1)
        consume(i, 1, 0)

    lam = lam_ref[0:1, 0:1]
    l = l_ref[...]
    acc = acc_ref[...]
    o = acc[:, 0:t] / l[:, 0:t] - lam * (acc[:, t:2 * t] / l[:, t:2 * t])
    ms = jnp.mean(o * o, axis=0, keepdims=True)
    on = o * lax.rsqrt(ms + EPS) * gain_ref[...].astype(F32) * (1.0 - LAM_INIT)
    o_ref[...] = on.T.astype(o_ref.dtype)


def _attn_prompt(kb, qt, vt, bias, lam, gain, t):
    n = kb.shape[0]
    nt = n // t
    return pl.pallas_call(
        functools.partial(_attn_kernel, t=t),
        grid=(DA_HEADS, nt),
        in_specs=[
            pl.BlockSpec((n, DA_VDIM), lambda h, i: (0, h)),
            pl.BlockSpec((DA_VDIM, t), lambda h, i: (h, i)),
            pl.BlockSpec((None, nt, DA_VDIM, t), lambda h, i: (h, 0, 0, 0)),
            pl.BlockSpec((None, 2, t, t), lambda h, i: (h, 0, 0, 0)),
            pl.BlockSpec((8, 128), lambda h, i: (0, 0)),
            pl.BlockSpec((DA_VDIM, 1), lambda h, i: (0, 0)),
        ],
        out_specs=pl.BlockSpec((t, DA_VDIM), lambda h, i: (i, h)),
        out_shape=jax.ShapeDtypeStruct((n, DA_WIDTH), BF16),
        scratch_shapes=[
            pltpu.VMEM((DA_VDIM, 2 * t), BF16),
            pltpu.VMEM((1, 2 * t), F32),
            pltpu.VMEM((1, 2 * t), F32),
            pltpu.VMEM((DA_VDIM, 2 * t), F32),
            pltpu.VMEM((2, t, 2 * t), F32),
        ],
        compiler_params=_params(("parallel", "parallel")),
        name="diff_attn_prompt",
    )(kb, qt, vt, bias, lam, gain.reshape(DA_VDIM, 1))


def _attn_step_kernel(q_ref, kp_ref, vp_ref, kn_ref, vn_ref, bp_ref, bn_ref, lam_ref, gain_ref,
                      o_ref, *, tq, pad):
    q = q_ref[...]
    lane = lax.broadcasted_iota(I32, q.shape, 1)
    zero = jnp.zeros_like(q)
    qz = jnp.concatenate([jnp.where(lane < DA_QKDIM, q, zero),
                          jnp.where(lane >= DA_QKDIM, q, zero)], axis=0)
    kp = kp_ref[...].astype(BF16)
    vp = vp_ref[...].astype(BF16)
    zpad = jnp.zeros((pad - tq, DA_VDIM), BF16)
    kn = jnp.concatenate([kn_ref[...], zpad], axis=0)
    vn = jnp.concatenate([vn_ref[...], zpad], axis=0)
    bp = bp_ref[...]
    bn = bn_ref[...]
    sp = _dot_nt(qz, kp) + jnp.concatenate([bp, bp], axis=0)
    sn = _dot_nt(qz, kn) + jnp.concatenate([bn, bn], axis=0)
    m = jnp.maximum(jnp.max(sp, axis=-1, keepdims=True), jnp.max(sn, axis=-1, keepdims=True))
    pp = jnp.exp2(sp - m)
    pn = jnp.exp2(sn - m)
    l = jnp.sum(pp, axis=-1, keepdims=True) + jnp.sum(pn, axis=-1, keepdims=True)
    acc = _dot(pp.astype(BF16), vp) + _dot(pn.astype(BF16), vn)
    on = acc / l
    lam = lam_ref[0:1, 0:1]
    o = on[0:tq] - lam * on[tq:2 * tq]
    ms = jnp.mean(o * o, axis=-1, keepdims=True)
    o = o * lax.rsqrt(ms + EPS) * gain_ref[...].astype(F32) * (1.0 - LAM_INIT)
    o_ref[...] = o.astype(o_ref.dtype)


def _attn_step(qs, cache_k_l, cache_v_l, kb, vb, bias_p, bias_n, lam, gain, batch, tq):
    past = cache_k_l.shape[1]
    pad = bias_n.shape[-1]
    return pl.pallas_call(
        functools.partial(_attn_step_kernel, tq=tq, pad=pad),
        grid=(batch, DA_HEADS),
        in_specs=[
            pl.BlockSpec((tq, DA_VDIM), lambda b, h: (b, h)),
            pl.BlockSpec((None, past, DA_VDIM), lambda b, h: (b, 0, h)),
            pl.BlockSpec((None, past, DA_VDIM), lambda b, h: (b, 0, h)),
            pl.BlockSpec((tq, DA_VDIM), lambda b, h: (b, h)),
            pl.BlockSpec((tq, DA_VDIM), lambda b, h: (b, h)),
            pl.BlockSpec((None, None, tq, past), lambda b, h: (h, 0, 0, 0)),
            pl.BlockSpec((None, None, tq, pad), lambda b, h: (h, 0, 0, 0)),
            pl.BlockSpec((8, 128), lambda b, h: (0, 0)),
            pl.BlockSpec((1, DA_VDIM), lambda b, h: (0, 0)),
        ],
        out_specs=pl.BlockSpec((tq, DA_VDIM), lambda b, h: (b, h)),
        out_shape=jax.ShapeDtypeStruct((batch * tq, DA_WIDTH), BF16),
        compiler_params=_params(("parallel", "parallel")),
        name="diff_attn_step",
    )(qs, cache_k_l.reshape(batch, past, DA_WIDTH), cache_v_l.reshape(batch, past, DA_WIDTH),
      kb, vb, bias_p, bias_n, lam, gain.reshape(1, DA_VDIM))


def _post_kernel(xa_ref, ohga_ref, odaa_ref, xb_ref, ohgb_ref, odab_ref,
                 wout_ref, ga1_ref, g_ref, sc_ref, sh_ref, ga2_ref,
                 wsgu_ref, wsd_ref, wrt_ref, rb_ref, tri_ref, ltri_ref,
                 xs_ref, h2_ref, pos_ref, wl_ref, chunk_ref, nch_ref, cnt_ref, carry_ref, *, tm, nta):
    i = pl.program_id(0)

    @pl.when(i == 0)
    def _():
        carry_ref[...] = jnp.zeros(carry_ref.shape, F32)

    second = i >= nta
    x = jnp.where(second, xb_ref[...], xa_ref[...])
    ohg = jnp.where(second, ohgb_ref[...], ohga_ref[...])
    oda = jnp.where(second, odab_ref[...], odaa_ref[...])
    mix = _dot(ohg, wout_ref[0:HG_WIDTH, :]) + _dot(oda, wout_ref[HG_WIDTH:, :])
    x1 = x + ga1_ref[...] * mix
    ms = jnp.mean(x1 * x1, axis=-1, keepdims=True)
    h2 = x1 * lax.rsqrt(ms + EPS) * g_ref[...]
    h2 = h2 * (1.0 + sc_ref[...]) + sh_ref[...]
    h2b = h2.astype(BF16)
    h2_ref[...] = h2b
    gu = _dot(h2b, wsgu_ref[...])
    act = (_silu(gu[:, 0:D_EXPERT]) * gu[:, D_EXPERT:]).astype(BF16)
    xs_ref[...] = x1 + ga2_ref[...] * _dot(act, wsd_ref[...])

    logits = _dot_nt(wrt_ref[...], h2, precision=HIGHEST)
    score = _sigmoid(logits)
    sel = score + rb_ref[...]
    sub = lax.broadcasted_iota(I32, (GROUP_SIZE, tm), 0)
    gscore = []
    for g in range(N_GROUPS):
        v = sel[g * GROUP_SIZE:(g + 1) * GROUP_SIZE, :]
        m1 = jnp.max(v, axis=0, keepdims=True)
        i1 = jnp.min(jnp.where(v == m1, sub, GROUP_SIZE), axis=0, keepdims=True)
        m2 = jnp.max(jnp.where(sub == i1, -jnp.inf, v), axis=0, keepdims=True)
        gscore.append(m1 + m2)
    gsel = []
    for g in range(N_GROUPS):
        ahead = jnp.zeros((1, tm), F32)
        for g2 in range(N_GROUPS):
            if g2 == g:
                continue
            tie = 1.0 if g2 < g else 0.0
            ahead = ahead + jnp.where(gscore[g2] > gscore[g], 1.0,
                                      jnp.where(gscore[g2] == gscore[g], tie, 0.0))
        gsel.append(ahead < TOP_GROUPS)
    selm = jnp.concatenate(
        [jnp.where(gsel[g], sel[g * GROUP_SIZE:(g + 1) * GROUP_SIZE, :], -jnp.inf)
         for g in range(N_GROUPS)], axis=0)
    eio = lax.broadcasted_iota(I32, (N_EXPERTS, tm), 0)
    ahead = jnp.zeros((N_EXPERTS, tm), F32)
    for e2 in range(N_EXPERTS):
        row = selm[e2:e2 + 1, :]
        tie = jnp.where(eio > e2, 1.0, 0.0)
        ahead = ahead + jnp.where(row > selm, 1.0, jnp.where(row == selm, tie, 0.0))
    chosen = jnp.where(selm > -jnp.inf, jnp.where(ahead < TOP_K, 1.0, 0.0), 0.0)
    w = chosen * score
    wn = w / jnp.sum(w, axis=0, keepdims=True) * ROUTE_SCALE

    chb = chosen.astype(BF16)
    before = _dot(chb, tri_ref[...])
    tot = _dot(chb, jnp.ones((tm, 128), BF16))
    run = jnp.floor((tot + (ROW_GROUP - 1)) * (1.0 / ROW_GROUP)) * ROW_GROUP
    tile_base = _dot(ltri_ref[...], run.astype(BF16))
    carry = carry_ref[...]
    carry_ref[...] = carry + run
    cnt_ref[...] = carry + run
    pos = jnp.concatenate([tile_base] * (tm // 128), axis=1) + before

    widen = lambda v: jnp.concatenate([v] * (CHUNK_SLOTS // 128), axis=1)
    crow = lax.broadcasted_iota(I32, (N_EXPERTS, CHUNK_SLOTS), 1).astype(F32) * ROW_GROUP
    erow = lax.broadcasted_iota(I32, (N_EXPERTS, CHUNK_SLOTS), 0).astype(F32)
    owner = jnp.sum(jnp.where(widen(tile_base + run) <= crow, 1.0, 0.0), axis=0, keepdims=True)
    region_row = jnp.sum(jnp.where(owner == erow, widen(carry - tile_base), 0.0),
                         axis=0, keepdims=True) + crow[0:1]
    region_slab = (region_row * (1.0 / ROW_GROUP)).astype(I32)
    chunk_ref[...] = owner.astype(I32) * (1 << CHUNK_EXPERT_SHIFT) + region_slab
    nch_ref[...] = jnp.sum(run * (1.0 / ROW_GROUP), axis=0, keepdims=True).astype(I32)

    for r in range(TOP_K):
        pick = jnp.where(ahead == r, chosen, 0.0)
        pos_ref[r:r + 1, :] = jnp.sum(pick * pos, axis=0, keepdims=True).astype(I32)
        wl_ref[r:r + 1, :] = jnp.sum(pick * wn, axis=0, keepdims=True)


def _post(src_a, src_b, w_out_b, ga1, g, sc, sh, ga2, wsgu_b, wsd_b, wr_t, rb, tm):
    (xa, ohga, odaa), (xb, ohgb, odab) = src_a, src_b
    d = xa.shape[1]
    nta, ntb = xa.shape[0] // tm, xb.shape[0] // tm
    nt = nta + ntb
    n = nt * tm
    tri = jnp.asarray(np.triu(np.ones((tm, tm), np.float32), k=1), dtype=BF16)
    ltri = jnp.asarray(np.tril(np.ones((N_EXPERTS, N_EXPERTS), np.float32), k=-1), dtype=BF16)
    row = lambda i: (i, 0)
    row_a = lambda i: (jnp.minimum(i, nta - 1), 0)
    row_b = lambda i: (jnp.maximum(i - nta, 0), 0)
    col = lambda i: (0, i)
    full = lambda i: (0, 0)
    mod = pl.BlockSpec((None, tm, d), lambda i: (jnp.minimum(i // nta, 1), 0, 0))
    return pl.pallas_call(
        functools.partial(_post_kernel, tm=tm, nta=nta),
        grid=(nt,),
        in_specs=[
            pl.BlockSpec((tm, d), row_a),
            pl.BlockSpec((tm, HG_WIDTH), row_a),
            pl.BlockSpec((tm, DA_WIDTH), row_a),
            pl.BlockSpec((tm, d), row_b),
            pl.BlockSpec((tm, HG_WIDTH), row_b),
            pl.BlockSpec((tm, DA_WIDTH), row_b),
            pl.BlockSpec(w_out_b.shape, full),
            mod,
            pl.BlockSpec((1, d), full),
            mod,
            mod,
            mod,
            pl.BlockSpec(wsgu_b.shape, full),
            pl.BlockSpec(wsd_b.shape, full),
            pl.BlockSpec(wr_t.shape, full),
            pl.BlockSpec((N_EXPERTS, 1), full),
            pl.BlockSpec((tm, tm), full),
            pl.BlockSpec((N_EXPERTS, N_EXPERTS), full),
        ],
        out_specs=[
            pl.BlockSpec((tm, d), row),
            pl.BlockSpec((tm, d), row),
            pl.BlockSpec((TOP_K, tm), col),
            pl.BlockSpec((TOP_K, tm), col),
            pl.BlockSpec((None, 1, CHUNK_SLOTS), lambda i: (i, 0, 0)),
            pl.BlockSpec((None, 1, 128), lambda i: (i, 0, 0)),
            pl.BlockSpec((N_EXPERTS, 128), full),
        ],
        out_shape=[
            jax.ShapeDtypeStruct((n, d), F32),
            jax.ShapeDtypeStruct((n, d), BF16),
            jax.ShapeDtypeStruct((TOP_K, n), I32),
            jax.ShapeDtypeStruct((TOP_K, n), F32),
            jax.ShapeDtypeStruct((nt, 1, CHUNK_SLOTS), I32),
            jax.ShapeDtypeStruct((nt, 1, 128), I32),
            jax.ShapeDtypeStruct((N_EXPERTS, 128), F32),
        ],
        scratch_shapes=[pltpu.VMEM((N_EXPERTS, 128), F32)],
        compiler_params=_params(("arbitrary",)),
        name="post_mix_router",
    )(xa, ohga, odaa, xb, ohgb, odab, w_out_b, ga1, g, sc, sh, ga2, wsgu_b, wsd_b, wr_t,
      rb.reshape(N_EXPERTS, 1), tri, ltri)


def _start_chunks(tile, chunk_ref, nch_ref, pslab_ref, make_copy):
    n = nch_ref[tile]

    def start(c):
        word = chunk_ref[tile * CHUNK_SLOTS + c]
        expert = lax.shift_right_logical(word, CHUNK_EXPERT_SHIFT)
        region_slab = word & ((1 << CHUNK_EXPERT_SHIFT) - 1)
        make_copy(c, pslab_ref[expert] + region_slab, 1).start()

    def group(g, carry):
        for u in range(CHUNK_UNROLL):
            start(g * CHUNK_UNROLL + u)
        return carry

    def single(c, carry):
        start(c)
        return carry

    groups = n // CHUNK_UNROLL
    lax.fori_loop(0, groups, group, 0)
    lax.fori_loop(groups * CHUNK_UNROLL, n, single, 0)


def _wait_chunks(tile, nch_ref, make_copy):
    n = nch_ref[tile]
    many = n // WAIT_CHUNKS

    def wait_many(j, carry):
        make_copy(0, 0, WAIT_CHUNKS).wait()
        return carry

    def wait_one(j, carry):
        make_copy(0, 0, 1).wait()
        return carry

    lax.fori_loop(0, many, wait_many, 0)
    lax.fori_loop(many * WAIT_CHUNKS, n, wait_one, 0)


def _dispatch_kernel(chunk_ref, nch_ref, pstart_ref, pend_ref, pos_ref, h2_ref, xs_hbm,
                     cbuf_ref, zero_ref, zsem, sem, *, tm, bm):
    i = pl.program_id(0)
    nt = pl.num_programs(0)
    bslabs = bm // ROW_GROUP
    nblk = xs_hbm.shape[0] // bslabs
    dh = cbuf_ref.shape[-1]

    def zero_copy(e):
        return pltpu.make_async_copy(zero_ref, xs_hbm.at[pl.ds(pend_ref[e] - bslabs, bslabs)], zsem)

    def tail_copy(b):
        return pltpu.make_async_copy(zero_ref, xs_hbm.at[pl.ds(b * bslabs, bslabs)], zsem)

    @pl.when(i == 0)
    def _():
        zero_ref[...] = jnp.zeros(zero_ref.shape, zero_ref.dtype)
        first_unused = pend_ref[N_EXPERTS - 1] // bslabs

        def zissue(e, carry):
            @pl.when(pend_ref[e] > pstart_ref[e])
            def _():
                zero_copy(e).start()
            return carry

        def zwait(e, carry):
            @pl.when(pend_ref[e] > pstart_ref[e])
            def _():
                zero_copy(e).wait()
            return carry

        def tissue(b, carry):
            tail_copy(b).start()
            return carry

        def twait(b, carry):
            tail_copy(b).wait()
            return carry

        lax.fori_loop(0, N_EXPERTS, zissue, 0)
        lax.fori_loop(first_unused, nblk, tissue, 0)
        lax.fori_loop(0, N_EXPERTS, zwait, 0)
        lax.fori_loop(first_unused, nblk, twait, 0)

    pos = pos_ref[...]
    piota = lax.broadcasted_iota(I32, (GROUPED_ROWS, tm), 0).astype(jnp.int16)
    pos16 = pos.astype(jnp.int16)
    one = jnp.ones((GROUPED_ROWS, tm), BF16)
    perm = jnp.zeros((GROUPED_ROWS, tm), BF16)
    for r in range(TOP_K):
        perm = jnp.where(piota == pos16[r:r + 1, :], one, perm)
    cur = lax.rem(i, 2)
    grouped = _pack_exact_bf16_pair(_dot(perm, h2_ref[:, 0:dh]), _dot(perm, h2_ref[:, dh:]))
    cbuf_ref[cur] = grouped.reshape(cbuf_ref.shape[1:])

    def make_copy(buf):
        def build(tile_slab, buffer_slab, slabs):
            return pltpu.make_async_copy(cbuf_ref.at[buf, pl.ds(tile_slab, slabs)],
                                         xs_hbm.at[pl.ds(buffer_slab, slabs)], sem.at[buf])
        return build

    _start_chunks(i, chunk_ref, nch_ref, pstart_ref, make_copy(cur))

    @pl.when(i > 0)
    def _():
        _wait_chunks(i - 1, nch_ref, make_copy(1 - cur))

    @pl.when(i == nt - 1)
    def _():
        _wait_chunks(i, nch_ref, make_copy(cur))


def _dispatch(chunks, nch, pstart, pend, pos, h2, nrows, tm, bm):
    n, d = h2.shape
    grid_spec = pltpu.PrefetchScalarGridSpec(
        num_scalar_prefetch=4,
        grid=(n // tm,),
        in_specs=[
            pl.BlockSpec((TOP_K, tm), lambda i, *_: (0, i)),
            pl.BlockSpec((tm, d), lambda i, *_: (i, 0)),
        ],
        out_specs=pl.BlockSpec(memory_space=pl.ANY),
        scratch_shapes=[
            pltpu.VMEM((2, GROUPED_ROWS // ROW_GROUP, ROW_GROUP, d // 2), U32),
            pltpu.VMEM((bm // ROW_GROUP, ROW_GROUP, d // 2), U32),
            pltpu.SemaphoreType.DMA(()),
            pltpu.SemaphoreType.DMA((2,)),
        ],
    )
    return pl.pallas_call(
        functools.partial(_dispatch_kernel, tm=tm, bm=bm),
        grid_spec=grid_spec,
        out_shape=jax.ShapeDtypeStruct((nrows // ROW_GROUP, ROW_GROUP, d // 2), U32),
        compiler_params=_params(("arbitrary",)),
        name="moe_dispatch",
    )(chunks, nch, pstart, pend, pos, h2)


def _experts_kernel(be_ref, nu_ref, valid_ref, x_ref, wgu_ref, wd_ref, o_ref, wgu_b_ref, wd_b_ref,
                    *, bm):
    i = pl.program_id(0)
    valid = valid_ref[i]
    sub = bm // 2

    @pl.when((valid > 0) & ((i == 0) | (be_ref[i] != be_ref[jnp.maximum(i - 1, 0)])))
    def _():
        wgu_b_ref[...] = wgu_ref[...].astype(BF16)
        wd_b_ref[...] = wd_ref[...].astype(BF16)

    for r0 in (0, sub):
        rows = slice(r0, r0 + sub)

        @pl.when(valid > r0)
        def _():
            lo, hi = _unpack_bf16_pair(x_ref[rows, :])
            x = jnp.concatenate([lo.astype(BF16), hi.astype(BF16)], axis=1)
            gu = _dot(x, wgu_b_ref[...])
            act = (_silu(gu[:, 0:D_EXPERT]) * gu[:, D_EXPERT:]).astype(BF16)
            y = _dot(act, wd_b_ref[...])
            o_ref[rows, :] = _pack_bf16_pair(y[:, 0:D_MODEL // 2], y[:, D_MODEL // 2:])

        @pl.when(valid <= r0)
        def _():
            o_ref[rows, :] = jnp.zeros((sub, o_ref.shape[1]), o_ref.dtype)


def _experts(block_e, nused, valid, xs, w_gate_up_l, w_down_l, bm):
    nrows, dh = xs.shape
    d = 2 * dh
    nblk = nrows // bm
    grid_spec = pltpu.PrefetchScalarGridSpec(
        num_scalar_prefetch=3,
        grid=(nblk,),
        in_specs=[
            pl.BlockSpec((bm, dh), lambda i, be, nu, va: (jnp.minimum(i, nu[0] - 1), 0)),
            pl.BlockSpec((None, d, 2 * D_EXPERT), lambda i, be, nu, va: (be[i], 0, 0)),
            pl.BlockSpec((None, D_EXPERT, d), lambda i, be, nu, va: (be[i], 0, 0)),
        ],
        out_specs=pl.BlockSpec((bm, dh), lambda i, be, nu, va: (i, 0)),
        scratch_shapes=[
            pltpu.VMEM((d, 2 * D_EXPERT), BF16),
            pltpu.VMEM((D_EXPERT, d), BF16),
        ],
    )
    return pl.pallas_call(
        functools.partial(_experts_kernel, bm=bm),
        grid_spec=grid_spec,
        out_shape=jax.ShapeDtypeStruct((nrows, dh), U32),
        compiler_params=_params(("arbitrary",)),
        name="moe_experts",
    )(block_e, nused, valid, xs, w_gate_up_l, w_down_l)


def _combine_kernel(chunk_ref, nch_ref, pstart_ref, pos_ref, wl_ref, xs_ref, ga2_ref, gf_ref,
                    yb_hbm, oa_ref, ob_ref, gbuf_ref, sem, *, tm, nta):
    i = pl.program_id(0)
    nt = pl.num_programs(0)
    cur = lax.rem(i, 2)
    refs = (chunk_ref, nch_ref, pstart_ref)

    def make_copy(buf):
        def build(tile_slab, buffer_slab, slabs):
            return pltpu.make_async_copy(yb_hbm.at[pl.ds(buffer_slab, slabs)],
                                         gbuf_ref.at[buf, pl.ds(tile_slab, slabs)], sem.at[buf])
        return build

    @pl.when(i == 0)
    def _():
        gbuf_ref[...] = jnp.zeros(gbuf_ref.shape, gbuf_ref.dtype)
        _start_chunks(0, *refs, make_copy(0))

    @pl.when(i + 1 < nt)
    def _():
        _start_chunks(i + 1, *refs, make_copy(1 - cur))

    _wait_chunks(i, nch_ref, make_copy(cur))

    lo, hi = _unpack_bf16_pair(gbuf_ref[cur].reshape(GROUPED_ROWS, gbuf_ref.shape[-1]))
    g = jnp.concatenate([lo.astype(BF16), hi.astype(BF16)], axis=1)
    pos16 = pos_ref[...].astype(jnp.int16)
    wl = wl_ref[...].astype(BF16)
    liota = lax.broadcasted_iota(I32, (tm, GROUPED_ROWS), 1).astype(jnp.int16)
    a = jnp.zeros((tm, GROUPED_ROWS), BF16)
    for r in range(TOP_K):
        a = jnp.where(liota == pos16[:, r:r + 1], jnp.broadcast_to(wl[:, r:r + 1], a.shape), a)
    routed = _dot(a, g)
    x2 = xs_ref[...] + ga2_ref[...] * routed
    ms = jnp.mean(x2 * x2, axis=-1, keepdims=True)
    y = x2 * lax.rsqrt(ms + EPS) * gf_ref[...]

    @pl.when(i < nta)
    def _():
        oa_ref[...] = y

    @pl.when(i >= nta)
    def _():
        ob_ref[...] = y


def _combine(chunks, nch, pstart, pos_t, wl_t, xs_base, ga2, gfin, yb, tm, nta):
    n, d = xs_base.shape
    ntb = n // tm - nta
    ga2_spec = pl.BlockSpec((None, tm, d), lambda i, *_: (jnp.minimum(i // nta, 1), 0, 0))
    grid_spec = pltpu.PrefetchScalarGridSpec(
        num_scalar_prefetch=3,
        grid=(n // tm,),
        in_specs=[
            pl.BlockSpec((tm, TOP_K), lambda i, *_: (i, 0)),
            pl.BlockSpec((tm, TOP_K), lambda i, *_: (i, 0)),
            pl.BlockSpec((tm, d), lambda i, *_: (i, 0)),
            ga2_spec,
            pl.BlockSpec((1, d), lambda i, *_: (0, 0)),
            pl.BlockSpec(memory_space=pl.ANY),
        ],
        out_specs=[
            pl.BlockSpec((tm, d), lambda i, *_: (jnp.minimum(i, nta - 1), 0)),
            pl.BlockSpec((tm, d), lambda i, *_: (jnp.maximum(i - nta, 0), 0)),
        ],
        scratch_shapes=[
            pltpu.VMEM((2, GROUPED_ROWS // ROW_GROUP, ROW_GROUP, d // 2), U32),
            pltpu.SemaphoreType.DMA((2,)),
        ],
    )
    return pl.pallas_call(
        functools.partial(_combine_kernel, tm=tm, nta=nta),
        grid_spec=grid_spec,
        out_shape=[jax.ShapeDtypeStruct((nta * tm, d), F32), jax.ShapeDtypeStruct((ntb * tm, d), F32)],
        compiler_params=_params(("arbitrary",)),
        name="moe_combine",
    )(chunks, nch, pstart, pos_t, wl_t, xs_base, ga2, gfin, yb)


def _moe_and_final(src_a, src_b, mods, wts, tm, bm):
    nta = src_a[0].shape[0] // tm
    ga1, sh2, sc2, ga2 = mods
    (w_out_b, g_ffn, wsgu_b, wsd_b, wr_t, rb, w_gate_up_l, w_down_l, g_final) = wts
    xs_base, h2, pos, wl, chunks, nch, cnt = _post(
        src_a, src_b, w_out_b, ga1, g_ffn, sc2, sh2, ga2, wsgu_b, wsd_b, wr_t, rb, tm)
    nt = xs_base.shape[0] // tm
    counts = cnt[:, 0].astype(I32)
    padded = (counts + bm - 1) // bm * bm
    pend = jnp.cumsum(padded)
    pstart = pend - padded
    nblk = -(-(nt * GROUPED_ROWS) // bm) + N_EXPERTS
    nused = (pend[-1] // bm).astype(I32)
    blk_row = jnp.minimum(jnp.arange(nblk, dtype=I32), nused - 1) * bm
    be = jnp.sum((pend[None, :] <= blk_row[:, None]).astype(I32), axis=1)
    chunks = chunks.reshape(-1)
    nch = nch[:, 0, 0]
    pslab = pstart // ROW_GROUP
    xs = _dispatch(chunks, nch, pslab, pend // ROW_GROUP, pos, h2, nblk * bm, tm, bm)
    xs = xs.reshape(nblk * bm, xs.shape[-1])
    region_end = jnp.sum(jnp.where(be[:, None] == jnp.arange(N_EXPERTS, dtype=I32)[None, :],
                                   (pstart + counts)[None, :], 0), axis=1)
    valid = jnp.clip(region_end - jnp.arange(nblk, dtype=I32) * bm, 0, bm)
    yb = _experts(be, nused.reshape(1), valid, xs, w_gate_up_l, w_down_l, bm)
    yb = yb.reshape(nblk * bm // ROW_GROUP, ROW_GROUP, yb.shape[-1])
    return _combine(chunks, nch, pslab, pos.T, wl.T, xs_base, ga2, g_final, yb, tm, nta)


def _expand(mod, reps):
    if mod.shape[0] == 1:
        return mod
    return jnp.repeat(mod, reps, axis=0)


def kernel(x_prompt, x_sample, cache_k, cache_v, state_hgrn, c_prompt, c_sample, w_ada, b_ada,
           norm_mix, norm_ffn, norm_final, w_in, w_out, hg_lb_logits, hg_norm, da_lambda, da_norm,
           rel_bias_table, w_router, router_bias, w_gate_up, w_down, ws_gate_up, ws_down):
    depth = w_in.shape[0]
    assert depth == 1 and hg_lb_logits.shape[0] == 2
    bp, tp, d = x_prompt.shape
    bs, ts, _ = x_sample.shape
    assert bp == 1
    past = cache_k.shape[2]
    l = 0

    rows = -(-(bp + bs) // 8) * 8
    c_all = jnp.zeros((rows, d), F32).at[:bp].set(c_prompt).at[bp:bp + bs].set(c_sample)
    mod = _adaln(c_all, w_ada[l], b_ada[l])
    mod_p = [mod[0:bp, j * d:(j + 1) * d] for j in range(6)]
    mod_s = [_expand(mod[bp:bp + bs, j * d:(j + 1) * d], ts) for j in range(6)]

    w_in_b = w_in[l].astype(BF16)
    w_out_b = w_out[l].astype(BF16)
    wsgu_b = ws_gate_up[l].astype(BF16)
    wsd_b = ws_down[l].astype(BF16)
    wr_t = w_router[l].T
    g_mix = norm_mix[l].reshape(1, d)
    g_ffn = norm_ffn[l].reshape(1, d)
    g_final = norm_final.reshape(1, d)
    moe_w = (w_out_b, g_ffn, wsgu_b, wsd_b, wr_t, router_bias[l], w_gate_up[l], w_down[l], g_final)

    lam = _lam(da_lambda[l])

    t_att = min(ATT_TILE, tp)
    kk = jnp.arange(t_att, dtype=I32)[:, None]
    qq = jnp.arange(t_att, dtype=I32)[None, :]
    idx_diag = jnp.where((kk // CHUNK) <= (qq // CHUNK), _rel_bucket(kk - qq), MASK_BUCKET)
    idx_prev = _rel_bucket(kk - qq - t_att)
    bias_p = _bias_tiles(rel_bias_table, jnp.stack([idx_diag, idx_prev]).astype(I32),
                         shift_bucket=N_BUCKETS // 2 - 1)
    pad = 128
    qpos = past + jnp.arange(ts, dtype=I32)[:, None]
    idx_sp = _rel_bucket(jnp.arange(past, dtype=I32)[None, :] - qpos)
    kn = jnp.arange(pad, dtype=I32)[None, :]
    idx_sn = jnp.where(kn < ts, _rel_bucket(past + kn - qpos), MASK_BUCKET)
    bias_sp = _bias_tiles(rel_bias_table, idx_sp[None].astype(I32), shift_bucket=None)
    bias_sn = _bias_tiles(rel_bias_table, idx_sn[None].astype(I32), shift_bucket=None)

    xp = x_prompt.reshape(bp * tp, d)
    sh1, sc1, ga1, sh2, sc2, ga2 = mod_p
    assert ATT_TILE == INPROJ_TILE
    zh, qt, kf, vf, kb, vt = _inproj(xp, g_mix, sc1, sh1, w_in_b, t_att, True)
    s_zero = jnp.zeros((bp, HG_HEADS, HG_DIM, HG_DIM), F32)
    ohg_p, sp_new = _hgrn(zh, s_zero, hg_lb_logits, hg_norm[l], bp, tp, min(HGRN_CHUNK, tp))
    oda_p = _attn_prompt(kb, qt, vt, bias_p, lam, da_norm[l], t_att)
    src_p = (xp, ohg_p, oda_p)
    mods_p = (ga1, sh2, sc2, ga2)
    k_prompt = kf.reshape(1, bp, tp, DA_HEADS, 2 * DA_QKDIM)
    v_prompt = vf.reshape(1, bp, tp, DA_HEADS, DA_VDIM)

    ns = bs * ts
    xs_ = x_sample.reshape(ns, d)
    sh1, sc1, ga1, sh2, sc2, ga2 = mod_s
    zh, qs, kf, vf, kb, vb = _inproj(xs_, g_mix, sc1, sh1, w_in_b, ns, False)
    ohg_s, ss_new = _hgrn(zh, state_hgrn[l], hg_lb_logits, hg_norm[l], bs, ts, ts)
    oda_s = _attn_step(qs, cache_k[l], cache_v[l], kb, vb, bias_sp, bias_sn, lam, da_norm[l], bs, ts)
    assert ns == POST_TILE
    mods = tuple(jnp.stack([jnp.broadcast_to(mp, (POST_TILE, d)), ms_])
                 for mp, ms_ in zip(mods_p, (ga1, sh2, sc2, ga2)))
    y_p, y_s = _moe_and_final(src_p, (xs_, ohg_s, oda_s), mods, moe_w, POST_TILE, MOE_BLOCK_ROWS)
    k_sample = kf.reshape(1, bs, ts, DA_HEADS, 2 * DA_QKDIM)
    v_sample = vf.reshape(1, bs, ts, DA_HEADS, DA_VDIM)

    return (y_p.reshape(bp, tp, d), y_s.reshape(bs, ts, d), k_prompt, v_prompt, sp_new[None],
            k_sample, v_sample, ss_new[None].astype(x_sample.dtype))
```

```python
import functools
import math

import numpy as np
import jax
import jax.numpy as jnp
from jax import lax
from jax.experimental import pallas as pl
from jax.experimental.pallas import tpu as pltpu

F32 = jnp.float32
BF16 = jnp.bfloat16
I32 = jnp.int32
U32 = jnp.uint32
HIGHEST = lax.Precision.HIGHEST

D_MODEL = 1024
CHUNK = 64
HG_HEADS = 4
HG_DIM = 128
HG_WIDTH = HG_HEADS * HG_DIM
DA_HEADS = 4
DA_VDIM = 128
DA_QKDIM = 64
DA_WIDTH = DA_HEADS * DA_VDIM
N_BUCKETS = 32
MAX_DIST = 128
N_EXPERTS = 64
TOP_K = 8
N_GROUPS = 8
GROUP_SIZE = N_EXPERTS // N_GROUPS
TOP_GROUPS = 4
D_EXPERT = 256
ROUTE_SCALE = 2.5
EPS = 1e-6
LAM_INIT = 0.8 - 0.6 * math.exp(-0.3 * 0)

LOG2E = math.log2(math.e)
HI_MASK = np.uint32(0xFFFF0000)
NEG_BIG = -1e30
MASK_BUCKET = N_BUCKETS
V7X_VMEM_LIMIT = 48 * 1024 * 1024

ATT_TILE = 512
HGRN_CHUNK = 256
INPROJ_TILE = 512
POST_TILE = 256
MOE_BLOCK_ROWS = 1024
ROW_GROUP = 8
GROUPED_ROWS = -(-(POST_TILE * TOP_K + N_EXPERTS * (ROW_GROUP - 1)) // 256) * 256
CHUNK_SLOTS = -(-(GROUPED_ROWS // ROW_GROUP) // 128) * 128
CHUNK_EXPERT_SHIFT = 24
CHUNK_UNROLL = 4
WAIT_CHUNKS = 16


def _sigmoid(x):
    return 1.0 / (1.0 + jnp.exp(-x))


def _silu(x):
    return x * _sigmoid(x)


def _dot(a, b, **kw):
    return jnp.dot(a, b, preferred_element_type=F32, **kw)


def _dot_nt(a, b, **kw):
    return lax.dot_general(a, b, (((1,), (1,)), ((), ())), preferred_element_type=F32, **kw)


def _dot_tn(a, b, **kw):
    return lax.dot_general(a, b, (((0,), (0,)), ((), ())), preferred_element_type=F32, **kw)


def _pack_bf16_pair(lo, hi):
    lo_bits = lax.bitcast_convert_type(lo.astype(BF16).astype(F32), U32)
    hi_bits = lax.bitcast_convert_type(hi.astype(BF16).astype(F32), U32)
    return (lo_bits >> 16) | (hi_bits & HI_MASK)


def _pack_exact_bf16_pair(lo, hi):
    return (lax.bitcast_convert_type(lo, U32) >> 16) | (lax.bitcast_convert_type(hi, U32) & HI_MASK)


def _unpack_bf16_pair(w):
    lo = lax.bitcast_convert_type(w << 16, F32)
    hi = lax.bitcast_convert_type(w & HI_MASK, F32)
    return lo, hi


def _params(sem, vmem=V7X_VMEM_LIMIT, flags=None):
    return pltpu.CompilerParams(dimension_semantics=sem, vmem_limit_bytes=vmem, flags=flags)


def _adaln_kernel(c_ref, w_ref, b_ref, o_ref):
    s = _silu(c_ref[...])
    o_ref[...] = _dot(s, w_ref[...], precision=HIGHEST) + b_ref[...]


def _adaln(c_all, w_ada, b_ada):
    rows, d = c_all.shape
    cols = w_ada.shape[1]
    blk = 1024
    return pl.pallas_call(
        _adaln_kernel,
        grid=(cols // blk,),
        in_specs=[
            pl.BlockSpec((rows, d), lambda j: (0, 0)),
            pl.BlockSpec((d, blk), lambda j: (0, j)),
            pl.BlockSpec((1, blk), lambda j: (0, j)),
        ],
        out_specs=pl.BlockSpec((rows, blk), lambda j: (0, j)),
        out_shape=jax.ShapeDtypeStruct((rows, cols), F32),
        compiler_params=_params(("parallel",)),
        name="adaln",
    )(c_all, w_ada, b_ada.reshape(1, cols))


def _lam_kernel(l_ref, o_ref):
    l = l_ref[...].astype(F32)
    a = jnp.sum(l[0:1] * l[1:2], axis=-1, keepdims=True)
    b = jnp.sum(l[2:3] * l[3:4], axis=-1, keepdims=True)
    lam = jnp.exp(a) - jnp.exp(b) + LAM_INIT
    o_ref[...] = jnp.broadcast_to(lam, o_ref.shape)


def _lam(da_lambda_l):
    return pl.pallas_call(
        _lam_kernel,
        out_shape=jax.ShapeDtypeStruct((8, 128), F32),
        name="lam",
    )(da_lambda_l)


def _rel_bucket(rel):
    nb = N_BUCKETS // 2
    max_exact = nb // 2
    side = jnp.where(rel > 0, nb, 0)
    n = jnp.abs(rel)
    large = max_exact + (jnp.log(jnp.maximum(n, 1).astype(F32) / max_exact)
                         / math.log(MAX_DIST / max_exact) * (nb - max_exact)).astype(I32)
    large = jnp.minimum(large, nb - 1)
    return side + jnp.where(n < max_exact, n, large)


def _bias_kernel(tab_ref, idx_ref, o_ref, *, shift_bucket):
    h = pl.program_id(0)
    idx = idx_ref[...]
    shift = tab_ref[shift_bucket, h] if shift_bucket is not None else 0.0
    acc = jnp.zeros(idx.shape, F32)
    for j in range(N_BUCKETS):
        acc = jnp.where(idx == j, (tab_ref[j, h] - shift) * LOG2E, acc)
    o_ref[...] = jnp.where(idx == MASK_BUCKET, NEG_BIG, acc)


def _bias_tiles(table, idx, *, shift_bucket):
    k, r, c = idx.shape
    return pl.pallas_call(
        functools.partial(_bias_kernel, shift_bucket=shift_bucket),
        grid=(DA_HEADS, k),
        in_specs=[
            pl.BlockSpec(memory_space=pltpu.SMEM),
            pl.BlockSpec((None, r, c), lambda h, d: (d, 0, 0)),
        ],
        out_specs=pl.BlockSpec((None, None, r, c), lambda h, d: (h, d, 0, 0)),
        out_shape=jax.ShapeDtypeStruct((DA_HEADS, k, r, c), F32),
        compiler_params=_params(("parallel", "parallel")),
        name="rel_bias",
    )(table, idx)


def _inproj_kernel(x_ref, g_ref, sc_ref, sh_ref, w_ref,
                   zh_ref, q_ref, k_ref, v_ref, kb_ref, vb_ref, *, transposed):
    x = x_ref[...]
    ms = jnp.mean(x * x, axis=-1, keepdims=True)
    h = x * lax.rsqrt(ms + EPS) * g_ref[...]
    h = h * (1.0 + sc_ref[...]) + sh_ref[...]
    hb = h.astype(BF16)
    c0 = 4 * HG_WIDTH
    zh_ref[...] = _dot(hb, w_ref[:, 0:c0])
    zq = _dot(hb, w_ref[:, c0:c0 + DA_WIDTH]) * (DA_QKDIM ** -0.5 * LOG2E)
    zk = _dot(hb, w_ref[:, c0 + DA_WIDTH:c0 + 2 * DA_WIDTH])
    k_ref[...] = zk
    kb_ref[...] = zk.astype(BF16)
    zv = _dot(hb, w_ref[:, c0 + 2 * DA_WIDTH:c0 + 3 * DA_WIDTH])
    v_ref[...] = zv
    if transposed:
        q_ref[...] = zq.T.astype(BF16)
        vb_ref[...] = zv.T.astype(BF16).reshape(vb_ref.shape)
    else:
        q_ref[...] = zq.astype(BF16)
        vb_ref[...] = zv.astype(BF16)


def _mod_spec(mod, tm):
    if mod.shape[0] == 1:
        return pl.BlockSpec((1, mod.shape[1]), lambda i: (0, 0))
    return pl.BlockSpec((tm, mod.shape[1]), lambda i: (i, 0))


def _inproj(x, g, sc, sh, w_in_b, tm, transposed):
    n, d = x.shape
    cols = w_in_b.shape[1]
    row = lambda i: (i, 0)
    if transposed:
        q_spec = pl.BlockSpec((DA_WIDTH, tm), lambda i: (0, i))
        q_shape = jax.ShapeDtypeStruct((DA_WIDTH, n), BF16)
        vb_spec = pl.BlockSpec((DA_HEADS, None, DA_VDIM, tm), lambda i: (0, i, 0, 0))
        vb_shape = jax.ShapeDtypeStruct((DA_HEADS, n // tm, DA_VDIM, tm), BF16)
    else:
        q_spec = vb_spec = pl.BlockSpec((tm, DA_WIDTH), row)
        q_shape = vb_shape = jax.ShapeDtypeStruct((n, DA_WIDTH), BF16)
    return pl.pallas_call(
        functools.partial(_inproj_kernel, transposed=transposed),
        grid=(n // tm,),
        in_specs=[
            pl.BlockSpec((tm, d), row),
            pl.BlockSpec((1, d), lambda i: (0, 0)),
            _mod_spec(sc, tm),
            _mod_spec(sh, tm),
            pl.BlockSpec((d, cols), lambda i: (0, 0)),
        ],
        out_specs=[
            pl.BlockSpec((tm, 4 * HG_WIDTH), row),
            q_spec,
            pl.BlockSpec((tm, DA_WIDTH), row),
            pl.BlockSpec((tm, DA_WIDTH), row),
            pl.BlockSpec((tm, DA_WIDTH), row),
            vb_spec,
        ],
        out_shape=[
            jax.ShapeDtypeStruct((n, 4 * HG_WIDTH), F32),
            q_shape,
            jax.ShapeDtypeStruct((n, DA_WIDTH), F32),
            jax.ShapeDtypeStruct((n, DA_WIDTH), F32),
            jax.ShapeDtypeStruct((n, DA_WIDTH), BF16),
            vb_shape,
        ],
        compiler_params=_params(("parallel",)),
        name="inproj",
    )(x, g, sc, sh, w_in_b)


def _hgrn_consts(c):
    levels = int(round(math.log2(c)))
    assert 1 << levels == c and levels >= 3
    t = np.arange(c)[:, None]
    r = np.arange(c)[None, :]
    tri = (r <= t).astype(np.float32)
    x = np.maximum(t ^ r, 1)
    lv = np.where(t == r, -1, np.where(t > r, np.floor(np.log2(x)).astype(np.int64), -2))
    return jnp.asarray(tri, dtype=BF16), jnp.asarray(lv, dtype=I32), levels


def _hgrn_kernel(zh_ref, s0_ref, lbl_ref, gain_ref, mall_ref, lv_ref,
                 o_ref, sout_ref, st_ref, b_ref, *, c, levels):
    ci = pl.program_id(1)

    @pl.when(ci == 0)
    def _():
        for h in range(HG_HEADS):
            st_ref[h] = s0_ref[h].astype(F32).T

    lbl = lbl_ref[...].astype(F32)
    mx = jnp.maximum(lbl[0:1], lbl[1:2])
    e0 = jnp.exp(lbl[0:1] - mx)
    e1 = jnp.exp(lbl[1:2] - mx)
    lb = e0 / (e0 + e1)

    xq = zh_ref[:, 0:HG_WIDTH]
    xf = zh_ref[:, HG_WIDTH:2 * HG_WIDTH]
    q = _silu(xq)
    y = lb + (1.0 - lb) * _sigmoid(xf)
    logf = jnp.log(y)
    kk = 1.0 - y

    l1 = logf.astype(BF16)
    r1 = logf - l1.astype(F32)
    l2 = r1.astype(BF16)
    l3 = (r1 - l2.astype(F32)).astype(BF16)
    tri = mall_ref[...]
    b = _dot(tri, l1) + _dot(tri, l2) + _dot(tri, l3)
    b_ref[...] = b
    trow = lax.broadcasted_iota(I32, b.shape, 0)

    def level_exponent(l):
        m = 1 << l
        later = (trow & m) != 0
        if l == 0:
            return jnp.where(later, logf, 0.0)
        if l == 1:
            below = pltpu.roll(logf, 1, 0)
            above = pltpu.roll(logf, c - 1, 0)
            low = (trow & 1) != 0
            return jnp.where(later, jnp.where(low, logf + below, logf), jnp.where(low, 0.0, above))
        mid = jnp.concatenate(
            [jnp.broadcast_to(b_ref[k * 2 * m + m - 1:k * 2 * m + m, :], (2 * m, b.shape[1]))
             for k in range(c // (2 * m))], axis=0)
        return jnp.where(later, b - mid, mid - b)

    factors = [jnp.exp(level_exponent(l)) for l in range(levels)]
    lv = lv_ref[...]
    gain = gain_ref[...].astype(F32)
    for h in range(HG_HEADS):
        sl = slice(h * HG_DIM, (h + 1) * HG_DIM)
        qh = q[:, sl]
        kh = kk[:, sl]
        ih = zh_ref[:, 2 * HG_WIDTH + h * HG_DIM:2 * HG_WIDTH + (h + 1) * HG_DIM]
        gh = zh_ref[:, 3 * HG_WIDTH + h * HG_DIM:3 * HG_WIDTH + (h + 1) * HG_DIM]
        bh = b[:, sl]
        ihb = ih.astype(BF16)
        a = jnp.where(lv == -1, _dot_nt(qh.astype(BF16), kh.astype(BF16)), 0.0)
        for l in range(levels):
            f = factors[l][:, sl]
            p = _dot_nt((qh * f).astype(BF16), (kh * f).astype(BF16))
            a = jnp.where(lv == l, p, a)
        st = st_ref[h]
        o = _dot(a.astype(BF16), ihb) + _dot_nt((qh * jnp.exp(bh)).astype(BF16), st.astype(BF16))
        bl = bh[c - 1:c, :]
        kd = (kh * jnp.exp(bl - bh)).astype(BF16)
        st_ref[h] = st * jnp.exp(bl) + _dot_tn(ihb, kd)
        ms = jnp.mean(o * o, axis=-1, keepdims=True)
        on = o * lax.rsqrt(ms + EPS) * gain
        o_ref[:, sl] = (on * _silu(gh)).astype(o_ref.dtype)

    @pl.when(ci == pl.num_programs(1) - 1)
    def _():
        for h in range(HG_HEADS):
            sout_ref[h] = st_ref[h].T.astype(sout_ref.dtype)


def _hgrn(zh, s0, lb_logits, gain, batch, seq, c):
    mall, lv, levels = _hgrn_consts(c)
    nc = seq // c
    return pl.pallas_call(
        functools.partial(_hgrn_kernel, c=c, levels=levels),
        grid=(batch, nc),
        in_specs=[
            pl.BlockSpec((c, 4 * HG_WIDTH), lambda b, i: (b * nc + i, 0)),
            pl.BlockSpec((None, HG_HEADS, HG_DIM, HG_DIM), lambda b, i: (b, 0, 0, 0)),
            pl.BlockSpec(lb_logits.shape, lambda b, i: (0, 0)),
            pl.BlockSpec((1, HG_DIM), lambda b, i: (0, 0)),
            pl.BlockSpec(mall.shape, lambda b, i: (0, 0)),
            pl.BlockSpec(lv.shape, lambda b, i: (0, 0)),
        ],
        out_specs=[
            pl.BlockSpec((c, HG_WIDTH), lambda b, i: (b * nc + i, 0)),
            pl.BlockSpec((None, HG_HEADS, HG_DIM, HG_DIM), lambda b, i: (b, 0, 0, 0)),
        ],
        out_shape=[
            jax.ShapeDtypeStruct((batch * seq, HG_WIDTH), BF16),
            jax.ShapeDtypeStruct((batch, HG_HEADS, HG_DIM, HG_DIM), F32),
        ],
        scratch_shapes=[pltpu.VMEM((HG_HEADS, HG_DIM, HG_DIM), F32),
                        pltpu.VMEM((c, HG_WIDTH), F32)],
        compiler_params=_params(("parallel", "arbitrary")),
        name="hgrn2",
    )(zh, s0, lb_logits, gain.reshape(1, HG_DIM), mall, lv)


def _attn_kernel(k_ref, qt_ref, vt_ref, bias_ref, lam_ref, gain_ref,
                 o_ref, qz_ref, m_ref, l_ref, acc_ref, s_ref, *, t):
    i = pl.program_id(1)
    qt = qt_ref[...]
    row = lax.broadcasted_iota(I32, qt.shape, 0)
    zero = jnp.zeros_like(qt)
    qz_ref[:, 0:t] = jnp.where(row < DA_QKDIM, qt, zero)
    qz_ref[:, t:2 * t] = jnp.where(row >= DA_QKDIM, qt, zero)
    m_ref[...] = jnp.full(m_ref.shape, NEG_BIG, F32)
    l_ref[...] = jnp.zeros(l_ref.shape, F32)
    acc_ref[...] = jnp.zeros(acc_ref.shape, F32)

    def scores(j, buf):
        kt = k_ref[pl.ds(pl.multiple_of(j * t, t), t), :]
        s_ref[buf] = _dot(kt, qz_ref[...])

    def consume(j, buf, bias_idx):
        s = s_ref[buf]
        if bias_idx is not None:
            b = bias_ref[bias_idx]
            s = jnp.concatenate([s[:, 0:t] + b, s[:, t:2 * t] + b], axis=1)
        m_prev = m_ref[...]
        m_new = jnp.maximum(m_prev, jnp.max(s, axis=0, keepdims=True))
        alpha = jnp.exp2(m_prev - m_new)
        pr = jnp.exp2(s - m_new)
        l_ref[...] = alpha * l_ref[...] + jnp.sum(pr, axis=0, keepdims=True)
        acc_ref[...] = alpha * acc_ref[...] + _dot(vt_ref[j], pr.astype(BF16))
        m_ref[...] = m_new

    n_far = jnp.maximum(i - 1, 0)

    @pl.when(n_far > 0)
    def _():
        scores(0, 0)

    def far_pair(p, carry):
        j = 2 * p
        scores(j + 1, 1)
        consume(j, 0, None)
        scores(jnp.minimum(j + 2, n_far - 1), 0)
        consume(j + 1, 1, None)
        return carry

    lax.fori_loop(0, n_far // 2, far_pair, 0)

    @pl.when(lax.rem(n_far, 2) == 1)
    def _():
        consume(n_far - 1, 0, None)

    @pl.when(i >= 1)
    def _():
        scores(i - 1, 0)
        scores(i, 1)
        consume(i - 1, 0, 1)
        consume(i, 1, 0)

    @pl.when(i == 0)
    def _():
        scores(i, 1)
        consume(i, 1, 0)

    lam = lam_ref[0:1, 0:1]
    l = l_ref[...]
    acc = acc_ref[...]
    o = acc[:, 0:t] / l[:, 0:t] - lam * (acc[:, t:2 * t] / l[:, t:2 * t])
    ms = jnp.mean(o * o, axis=0, keepdims=True)
    on = o * lax.rsqrt(ms + EPS) * gain_ref[...].astype(F32) * (1.0 - LAM_INIT)
    o_ref[...] = on.T.astype(o_ref.dtype)


def _attn_prompt(kb, qt, vt, bias, lam, gain, t):
    n = kb.shape[0]
    nt = n // t
    return pl.pallas_call(
        functools.partial(_attn_kernel, t=t),
        grid=(DA_HEADS, nt),
        in_specs=[
            pl.BlockSpec((n, DA_VDIM), lambda h, i: (0, h)),
            pl.BlockSpec((DA_VDIM, t), lambda h, i: (h, i)),
            pl.BlockSpec((None, nt, DA_VDIM, t), lambda h, i: (h, 0, 0, 0)),
            pl.BlockSpec((None, 2, t, t), lambda h, i: (h, 0, 0, 0)),
            pl.BlockSpec((8, 128), lambda h, i: (0, 0)),
            pl.BlockSpec((DA_VDIM, 1), lambda h, i: (0, 0)),
        ],
        out_specs=pl.BlockSpec((t, DA_VDIM), lambda h, i: (i, h)),
        out_shape=jax.ShapeDtypeStruct((n, DA_WIDTH), BF16),
        scratch_shapes=[
            pltpu.VMEM((DA_VDIM, 2 * t), BF16),
            pltpu.VMEM((1, 2 * t), F32),
            pltpu.VMEM((1, 2 * t), F32),
            pltpu.VMEM((DA_VDIM, 2 * t), F32),
            pltpu.VMEM((2, t, 2 * t), F32),
        ],
        compiler_params=_params(("parallel", "parallel")),
        name="diff_attn_prompt",
    )(kb, qt, vt, bias, lam, gain.reshape(DA_VDIM, 1))


def _attn_step_kernel(q_ref, kp_ref, vp_ref, kn_ref, vn_ref, bp_ref, bn_ref, lam_ref, gain_ref,
                      o_ref, *, tq, pad):
    q = q_ref[...]
    lane = lax.broadcasted_iota(I32, q.shape, 1)
    zero = jnp.zeros_like(q)
    qz = jnp.concatenate([jnp.where(lane < DA_QKDIM, q, zero),
                          jnp.where(lane >= DA_QKDIM, q, zero)], axis=0)
    kp = kp_ref[...].astype(BF16)
    vp = vp_ref[...].astype(BF16)
    zpad = jnp.zeros((pad - tq, DA_VDIM), BF16)
    kn = jnp.concatenate([kn_ref[...], zpad], axis=0)
    vn = jnp.concatenate([vn_ref[...], zpad], axis=0)
    bp = bp_ref[...]
    bn = bn_ref[...]
    sp = _dot_nt(qz, kp) + jnp.concatenate([bp, bp], axis=0)
    sn = _dot_nt(qz, kn) + jnp.concatenate([bn, bn], axis=0)
    m = jnp.maximum(jnp.max(sp, axis=-1, keepdims=True), jnp.max(sn, axis=-1, keepdims=True))
    pp = jnp.exp2(sp - m)
    pn = jnp.exp2(sn - m)
    l = jnp.sum(pp, axis=-1, keepdims=True) + jnp.sum(pn, axis=-1, keepdims=True)
    acc = _dot(pp.astype(BF16), vp) + _dot(pn.astype(BF16), vn)
    on = acc / l
    lam = lam_ref[0:1, 0:1]
    o = on[0:tq] - lam * on[tq:2 * tq]
    ms = jnp.mean(o * o, axis=-1, keepdims=True)
    o = o * lax.rsqrt(ms + EPS) * gain_ref[...].astype(F32) * (1.0 - LAM_INIT)
    o_ref[...] = o.astype(o_ref.dtype)


def _attn_step(qs, cache_k_l, cache_v_l, kb, vb, bias_p, bias_n, lam, gain, batch, tq):
    past = cache_k_l.shape[1]
    pad = bias_n.shape[-1]
    return pl.pallas_call(
        functools.partial(_attn_step_kernel, tq=tq, pad=pad),
        grid=(batch, DA_HEADS),
        in_specs=[
            pl.BlockSpec((tq, DA_VDIM), lambda b, h: (b, h)),
            pl.BlockSpec((None, past, DA_VDIM), lambda b, h: (b, 0, h)),
            pl.BlockSpec((None, past, DA_VDIM), lambda b, h: (b, 0, h)),
            pl.BlockSpec((tq, DA_VDIM), lambda b, h: (b, h)),
            pl.BlockSpec((tq, DA_VDIM), lambda b, h: (b, h)),
            pl.BlockSpec((None, None, tq, past), lambda b, h: (h, 0, 0, 0)),
            pl.BlockSpec((None, None, tq, pad), lambda b, h: (h, 0, 0, 0)),
            pl.BlockSpec((8, 128), lambda b, h: (0, 0)),
            pl.BlockSpec((1, DA_VDIM), lambda b, h: (0, 0)),
        ],
        out_specs=pl.BlockSpec((tq, DA_VDIM), lambda b, h: (b, h)),
        out_shape=jax.ShapeDtypeStruct((batch * tq, DA_WIDTH), BF16),
        compiler_params=_params(("parallel", "parallel")),
        name="diff_attn_step",
    )(qs, cache_k_l.reshape(batch, past, DA_WIDTH), cache_v_l.reshape(batch, past, DA_WIDTH),
      kb, vb, bias_p, bias_n, lam, gain.reshape(1, DA_VDIM))


def _post_kernel(xa_ref, ohga_ref, odaa_ref, xb_ref, ohgb_ref, odab_ref,
                 wout_ref, ga1_ref, g_ref, sc_ref, sh_ref, ga2_ref,
                 wsgu_ref, wsd_ref, wrt_ref, rb_ref, tri_ref, ltri_ref,
                 xs_ref, h2_ref, pos_ref, wl_ref, chunk_ref, nch_ref, cnt_ref, carry_ref, *, tm, nta):
    i = pl.program_id(0)

    @pl.when(i == 0)
    def _():
        carry_ref[...] = jnp.zeros(carry_ref.shape, F32)

    second = i >= nta
    x = jnp.where(second, xb_ref[...], xa_ref[...])
    ohg = jnp.where(second, ohgb_ref[...], ohga_ref[...])
    oda = jnp.where(second, odab_ref[...], odaa_ref[...])
    mix = _dot(ohg, wout_ref[0:HG_WIDTH, :]) + _dot(oda, wout_ref[HG_WIDTH:, :])
    x1 = x + ga1_ref[...] * mix
    ms = jnp.mean(x1 * x1, axis=-1, keepdims=True)
    h2 = x1 * lax.rsqrt(ms + EPS) * g_ref[...]
    h2 = h2 * (1.0 + sc_ref[...]) + sh_ref[...]
    h2b = h2.astype(BF16)
    h2_ref[...] = h2b
    gu = _dot(h2b, wsgu_ref[...])
    act = (_silu(gu[:, 0:D_EXPERT]) * gu[:, D_EXPERT:]).astype(BF16)
    xs_ref[...] = x1 + ga2_ref[...] * _dot(act, wsd_ref[...])

    logits = _dot_nt(wrt_ref[...], h2, precision=HIGHEST)
    score = _sigmoid(logits)
    sel = score + rb_ref[...]
    sub = lax.broadcasted_iota(I32, (GROUP_SIZE, tm), 0)
    gscore = []
    for g in range(N_GROUPS):
        v = sel[g * GROUP_SIZE:(g + 1) * GROUP_SIZE, :]
        m1 = jnp.max(v, axis=0, keepdims=True)
        i1 = jnp.min(jnp.where(v == m1, sub, GROUP_SIZE), axis=0, keepdims=True)
        m2 = jnp.max(jnp.where(sub == i1, -jnp.inf, v), axis=0, keepdims=True)
        gscore.append(m1 + m2)
    gsel = []
    for g in range(N_GROUPS):
        ahead = jnp.zeros((1, tm), F32)
        for g2 in range(N_GROUPS):
            if g2 == g:
                continue
            tie = 1.0 if g2 < g else 0.0
            ahead = ahead + jnp.where(gscore[g2] > gscore[g], 1.0,
                                      jnp.where(gscore[g2] == gscore[g], tie, 0.0))
        gsel.append(ahead < TOP_GROUPS)
    selm = jnp.concatenate(
        [jnp.where(gsel[g], sel[g * GROUP_SIZE:(g + 1) * GROUP_SIZE, :], -jnp.inf)
         for g in range(N_GROUPS)], axis=0)
    eio = lax.broadcasted_iota(I32, (N_EXPERTS, tm), 0)
    ahead = jnp.zeros((N_EXPERTS, tm), F32)
    for e2 in range(N_EXPERTS):
        row = selm[e2:e2 + 1, :]
        tie = jnp.where(eio > e2, 1.0, 0.0)
        ahead = ahead + jnp.where(row > selm, 1.0, jnp.where(row == selm, tie, 0.0))
    chosen = jnp.where(selm > -jnp.inf, jnp.where(ahead < TOP_K, 1.0, 0.0), 0.0)
    w = chosen * score
    wn = w / jnp.sum(w, axis=0, keepdims=True) * ROUTE_SCALE

    chb = chosen.astype(BF16)
    before = _dot(chb, tri_ref[...])
    tot = _dot(chb, jnp.ones((tm, 128), BF16))
    run = jnp.floor((tot + (ROW_GROUP - 1)) * (1.0 / ROW_GROUP)) * ROW_GROUP
    tile_base = _dot(ltri_ref[...], run.astype(BF16))
    carry = carry_ref[...]
    carry_ref[...] = carry + run
    cnt_ref[...] = carry + run
    pos = jnp.concatenate([tile_base] * (tm // 128), axis=1) + before

    widen = lambda v: jnp.concatenate([v] * (CHUNK_SLOTS // 128), axis=1)
    crow = lax.broadcasted_iota(I32, (N_EXPERTS, CHUNK_SLOTS), 1).astype(F32) * ROW_GROUP
    erow = lax.broadcasted_iota(I32, (N_EXPERTS, CHUNK_SLOTS), 0).astype(F32)
    owner = jnp.sum(jnp.where(widen(tile_base + run) <= crow, 1.0, 0.0), axis=0, keepdims=True)
    region_row = jnp.sum(jnp.where(owner == erow, widen(carry - tile_base), 0.0),
                         axis=0, keepdims=True) + crow[0:1]
    region_slab = (region_row * (1.0 / ROW_GROUP)).astype(I32)
    chunk_ref[...] = owner.astype(I32) * (1 << CHUNK_EXPERT_SHIFT) + region_slab
    nch_ref[...] = jnp.sum(run * (1.0 / ROW_GROUP), axis=0, keepdims=True).astype(I32)

    for r in range(TOP_K):
        pick = jnp.where(ahead == r, chosen, 0.0)
        pos_ref[r:r + 1, :] = jnp.sum(pick * pos, axis=0, keepdims=True).astype(I32)
        wl_ref[r:r + 1, :] = jnp.sum(pick * wn, axis=0, keepdims=True)


def _post(src_a, src_b, w_out_b, ga1, g, sc, sh, ga2, wsgu_b, wsd_b, wr_t, rb, tm):
    (xa, ohga, odaa), (xb, ohgb, odab) = src_a, src_b
    d = xa.shape[1]
    nta, ntb = xa.shape[0] // tm, xb.shape[0] // tm
    nt = nta + ntb
    n = nt * tm
    tri = jnp.asarray(np.triu(np.ones((tm, tm), np.float32), k=1), dtype=BF16)
    ltri = jnp.asarray(np.tril(np.ones((N_EXPERTS, N_EXPERTS), np.float32), k=-1), dtype=BF16)
    row = lambda i: (i, 0)
    row_a = lambda i: (jnp.minimum(i, nta - 1), 0)
    row_b = lambda i: (jnp.maximum(i - nta, 0), 0)
    col = lambda i: (0, i)
    full = lambda i: (0, 0)
    mod = pl.BlockSpec((None, tm, d), lambda i: (jnp.minimum(i // nta, 1), 0, 0))
    return pl.pallas_call(
        functools.partial(_post_kernel, tm=tm, nta=nta),
        grid=(nt,),
        in_specs=[
            pl.BlockSpec((tm, d), row_a),
            pl.BlockSpec((tm, HG_WIDTH), row_a),
            pl.BlockSpec((tm, DA_WIDTH), row_a),
            pl.BlockSpec((tm, d), row_b),
            pl.BlockSpec((tm, HG_WIDTH), row_b),
            pl.BlockSpec((tm, DA_WIDTH), row_b),
            pl.BlockSpec(w_out_b.shape, full),
            mod,
            pl.BlockSpec((1, d), full),
            mod,
            mod,
            mod,
            pl.BlockSpec(wsgu_b.shape, full),
            pl.BlockSpec(wsd_b.shape, full),
            pl.BlockSpec(wr_t.shape, full),
            pl.BlockSpec((N_EXPERTS, 1), full),
            pl.BlockSpec((tm, tm), full),
            pl.BlockSpec((N_EXPERTS, N_EXPERTS), full),
        ],
        out_specs=[
            pl.BlockSpec((tm, d), row),
            pl.BlockSpec((tm, d), row),
            pl.BlockSpec((TOP_K, tm), col),
            pl.BlockSpec((TOP_K, tm), col),
            pl.BlockSpec((None, 1, CHUNK_SLOTS), lambda i: (i, 0, 0)),
            pl.BlockSpec((None, 1, 128), lambda i: (i, 0, 0)),
            pl.BlockSpec((N_EXPERTS, 128), full),
        ],
        out_shape=[
            jax.ShapeDtypeStruct((n, d), F32),
            jax.ShapeDtypeStruct((n, d), BF16),
            jax.ShapeDtypeStruct((TOP_K, n), I32),
            jax.ShapeDtypeStruct((TOP_K, n), F32),
            jax.ShapeDtypeStruct((nt, 1, CHUNK_SLOTS), I32),
            jax.ShapeDtypeStruct((nt, 1, 128), I32),
            jax.ShapeDtypeStruct((N_EXPERTS, 128), F32),
        ],
        scratch_shapes=[pltpu.VMEM((N_EXPERTS, 128), F32)],
        compiler_params=_params(("arbitrary",)),
        name="post_mix_router",
    )(xa, ohga, odaa, xb, ohgb, odab, w_out_b, ga1, g, sc, sh, ga2, wsgu_b, wsd_b, wr_t,
      rb.reshape(N_EXPERTS, 1), tri, ltri)


def _start_chunks(tile, chunk_ref, nch_ref, pslab_ref, make_copy):
    n = nch_ref[tile]

    def start(c):
        word = chunk_ref[tile * CHUNK_SLOTS + c]
        expert = lax.shift_right_logical(word, CHUNK_EXPERT_SHIFT)
        region_slab = word & ((1 << CHUNK_EXPERT_SHIFT) - 1)
        make_copy(c, pslab_ref[expert] + region_slab, 1).start()

    def group(g, carry):
        for u in range(CHUNK_UNROLL):
            start(g * CHUNK_UNROLL + u)
        return carry

    def single(c, carry):
        start(c)
        return carry

    groups = n // CHUNK_UNROLL
    lax.fori_loop(0, groups, group, 0)
    lax.fori_loop(groups * CHUNK_UNROLL, n, single, 0)


def _wait_chunks(tile, nch_ref, make_copy):
    n = nch_ref[tile]
    many = n // WAIT_CHUNKS

    def wait_many(j, carry):
        make_copy(0, 0, WAIT_CHUNKS).wait()
        return carry

    def wait_one(j, carry):
        make_copy(0, 0, 1).wait()
        return carry

    lax.fori_loop(0, many, wait_many, 0)
    lax.fori_loop(many * WAIT_CHUNKS, n, wait_one, 0)


def _dispatch_kernel(chunk_ref, nch_ref, pstart_ref, pend_ref, pos_ref, h2_ref, xs_hbm,
                     cbuf_ref, zero_ref, zsem, sem, *, tm, bm):
    i = pl.program_id(0)
    nt = pl.num_programs(0)
    bslabs = bm // ROW_GROUP
    nblk = xs_hbm.shape[0] // bslabs
    dh = cbuf_ref.shape[-1]

    def zero_copy(e):
        return pltpu.make_async_copy(zero_ref, xs_hbm.at[pl.ds(pend_ref[e] - bslabs, bslabs)], zsem)

    def tail_copy(b):
        return pltpu.make_async_copy(zero_ref, xs_hbm.at[pl.ds(b * bslabs, bslabs)], zsem)

    @pl.when(i == 0)
    def _():
        zero_ref[...] = jnp.zeros(zero_ref.shape, zero_ref.dtype)
        first_unused = pend_ref[N_EXPERTS - 1] // bslabs

        def zissue(e, carry):
            @pl.when(pend_ref[e] > pstart_ref[e])
            def _():
                zero_copy(e).start()
            return carry

        def zwait(e, carry):
            @pl.when(pend_ref[e] > pstart_ref[e])
            def _():
                zero_copy(e).wait()
            return carry

        def tissue(b, carry):
            tail_copy(b).start()
            return carry

        def twait(b, carry):
            tail_copy(b).wait()
            return carry

        lax.fori_loop(0, N_EXPERTS, zissue, 0)
        lax.fori_loop(first_unused, nblk, tissue, 0)
        lax.fori_loop(0, N_EXPERTS, zwait, 0)
        lax.fori_loop(first_unused, nblk, twait, 0)

    pos = pos_ref[...]
    piota = lax.broadcasted_iota(I32, (GROUPED_ROWS, tm), 0).astype(jnp.int16)
    pos16 = pos.astype(jnp.int16)
    one = jnp.ones((GROUPED_ROWS, tm), BF16)
    perm = jnp.zeros((GROUPED_ROWS, tm), BF16)
    for r in range(TOP_K):
        perm = jnp.where(piota == pos16[r:r + 1, :], one, perm)
    cur = lax.rem(i, 2)
    grouped = _pack_exact_bf16_pair(_dot(perm, h2_ref[:, 0:dh]), _dot(perm, h2_ref[:, dh:]))
    cbuf_ref[cur] = grouped.reshape(cbuf_ref.shape[1:])

    def make_copy(buf):
        def build(tile_slab, buffer_slab, slabs):
            return pltpu.make_async_copy(cbuf_ref.at[buf, pl.ds(tile_slab, slabs)],
                                         xs_hbm.at[pl.ds(buffer_slab, slabs)], sem.at[buf])
        return build

    _start_chunks(i, chunk_ref, nch_ref, pstart_ref, make_copy(cur))

    @pl.when(i > 0)
    def _():
        _wait_chunks(i - 1, nch_ref, make_copy(1 - cur))

    @pl.when(i == nt - 1)
    def _():
        _wait_chunks(i, nch_ref, make_copy(cur))


def _dispatch(chunks, nch, pstart, pend, pos, h2, nrows, tm, bm):
    n, d = h2.shape
    grid_spec = pltpu.PrefetchScalarGridSpec(
        num_scalar_prefetch=4,
        grid=(n // tm,),
        in_specs=[
            pl.BlockSpec((TOP_K, tm), lambda i, *_: (0, i)),
            pl.BlockSpec((tm, d), lambda i, *_: (i, 0)),
        ],
        out_specs=pl.BlockSpec(memory_space=pl.ANY),
        scratch_shapes=[
            pltpu.VMEM((2, GROUPED_ROWS // ROW_GROUP, ROW_GROUP, d // 2), U32),
            pltpu.VMEM((bm // ROW_GROUP, ROW_GROUP, d // 2), U32),
            pltpu.SemaphoreType.DMA(()),
            pltpu.SemaphoreType.DMA((2,)),
        ],
    )
    return pl.pallas_call(
        functools.partial(_dispatch_kernel, tm=tm, bm=bm),
        grid_spec=grid_spec,
        out_shape=jax.ShapeDtypeStruct((nrows // ROW_GROUP, ROW_GROUP, d // 2), U32),
        compiler_params=_params(("arbitrary",)),
        name="moe_dispatch",
    )(chunks, nch, pstart, pend, pos, h2)


def _experts_kernel(be_ref, nu_ref, valid_ref, first_ref, slot_ref, next_ref, x_ref, wgu_hbm, wd_hbm,
                    o_ref, wgu_f_ref, wd_f_ref, wgu_b_ref, wd_b_ref, sem, *, bm):
    i = pl.program_id(0)
    valid = valid_ref[i]
    sub = bm // 2

    def weight_copies(expert, slot):
        return (pltpu.make_async_copy(wgu_hbm.at[expert], wgu_f_ref.at[slot], sem.at[slot, 0]),
                pltpu.make_async_copy(wd_hbm.at[expert], wd_f_ref.at[slot], sem.at[slot, 1]))

    @pl.when(i == 0)
    def _():
        for c in weight_copies(be_ref[0], 0):
            c.start()

    @pl.when(first_ref[i] == 1)
    def _():
        slot = slot_ref[i]
        for c in weight_copies(be_ref[i], slot):
            c.wait()

        @pl.when(next_ref[i] >= 0)
        def _():
            for c in weight_copies(next_ref[i], 1 - slot):
                c.start()

        wgu_b_ref[...] = wgu_f_ref[slot].astype(BF16)
        wd_b_ref[...] = wd_f_ref[slot].astype(BF16)

    for r0 in (0, sub):
        rows = slice(r0, r0 + sub)

        @pl.when(valid > r0)
        def _():
            lo, hi = _unpack_bf16_pair(x_ref[rows, :])
            x = jnp.concatenate([lo.astype(BF16), hi.astype(BF16)], axis=1)
            gu = _dot(x, wgu_b_ref[...])
            act = (_silu(gu[:, 0:D_EXPERT]) * gu[:, D_EXPERT:]).astype(BF16)
            y = _dot(act, wd_b_ref[...])
            o_ref[rows, :] = _pack_bf16_pair(y[:, 0:D_MODEL // 2], y[:, D_MODEL // 2:])

        @pl.when(valid <= r0)
        def _():
            o_ref[rows, :] = jnp.zeros((sub, o_ref.shape[1]), o_ref.dtype)


def _experts(block_e, nused, valid, first, slot, next_e, xs, w_gate_up_l, w_down_l, bm):
    nrows, dh = xs.shape
    d = 2 * dh
    nblk = nrows // bm
    grid_spec = pltpu.PrefetchScalarGridSpec(
        num_scalar_prefetch=6,
        grid=(nblk,),
        in_specs=[
            pl.BlockSpec((bm, dh), lambda i, be, nu, *_: (jnp.minimum(i, nu[0] - 1), 0)),
            pl.BlockSpec(memory_space=pl.ANY),
            pl.BlockSpec(memory_space=pl.ANY),
        ],
        out_specs=pl.BlockSpec((bm, dh), lambda i, *_: (i, 0)),
        scratch_shapes=[
            pltpu.VMEM((2, d, 2 * D_EXPERT), w_gate_up_l.dtype),
            pltpu.VMEM((2, D_EXPERT, d), w_down_l.dtype),
            pltpu.VMEM((d, 2 * D_EXPERT), BF16),
            pltpu.VMEM((D_EXPERT, d), BF16),
            pltpu.SemaphoreType.DMA((2, 2)),
        ],
    )
    return pl.pallas_call(
        functools.partial(_experts_kernel, bm=bm),
        grid_spec=grid_spec,
        out_shape=jax.ShapeDtypeStruct((nrows, dh), U32),
        compiler_params=_params(("arbitrary",)),
        name="moe_experts",
    )(block_e, nused, valid, first, slot, next_e, xs, w_gate_up_l, w_down_l)


def _combine_kernel(chunk_ref, nch_ref, pstart_ref, pos_ref, wl_ref, xs_ref, ga2_ref, gf_ref,
                    yb_hbm, oa_ref, ob_ref, gbuf_ref, sem, *, tm, nta):
    i = pl.program_id(0)
    nt = pl.num_programs(0)
    cur = lax.rem(i, 2)
    refs = (chunk_ref, nch_ref, pstart_ref)

    def make_copy(buf):
        def build(tile_slab, buffer_slab, slabs):
            return pltpu.make_async_copy(yb_hbm.at[pl.ds(buffer_slab, slabs)],
                                         gbuf_ref.at[buf, pl.ds(tile_slab, slabs)], sem.at[buf])
        return build

    @pl.when(i == 0)
    def _():
        gbuf_ref[...] = jnp.zeros(gbuf_ref.shape, gbuf_ref.dtype)
        _start_chunks(0, *refs, make_copy(0))

    @pl.when(i + 1 < nt)
    def _():
        _start_chunks(i + 1, *refs, make_copy(1 - cur))

    _wait_chunks(i, nch_ref, make_copy(cur))

    lo, hi = _unpack_bf16_pair(gbuf_ref[cur].reshape(GROUPED_ROWS, gbuf_ref.shape[-1]))
    g = jnp.concatenate([lo.astype(BF16), hi.astype(BF16)], axis=1)
    pos16 = pos_ref[...].astype(jnp.int16)
    wl = wl_ref[...].astype(BF16)
    liota = lax.broadcasted_iota(I32, (tm, GROUPED_ROWS), 1).astype(jnp.int16)
    a = jnp.zeros((tm, GROUPED_ROWS), BF16)
    for r in range(TOP_K):
        a = jnp.where(liota == pos16[:, r:r + 1], jnp.broadcast_to(wl[:, r:r + 1], a.shape), a)
    routed = _dot(a, g)
    x2 = xs_ref[...] + ga2_ref[...] * routed
    ms = jnp.mean(x2 * x2, axis=-1, keepdims=True)
    y = x2 * lax.rsqrt(ms + EPS) * gf_ref[...]

    @pl.when(i < nta)
    def _():
        oa_ref[...] = y

    @pl.when(i >= nta)
    def _():
        ob_ref[...] = y


def _combine(chunks, nch, pstart, pos_t, wl_t, xs_base, ga2, gfin, yb, tm, nta):
    n, d = xs_base.shape
    ntb = n // tm - nta
    ga2_spec = pl.BlockSpec((None, tm, d), lambda i, *_: (jnp.minimum(i // nta, 1), 0, 0))
    grid_spec = pltpu.PrefetchScalarGridSpec(
        num_scalar_prefetch=3,
        grid=(n // tm,),
        in_specs=[
            pl.BlockSpec((tm, TOP_K), lambda i, *_: (i, 0)),
            pl.BlockSpec((tm, TOP_K), lambda i, *_: (i, 0)),
            pl.BlockSpec((tm, d), lambda i, *_: (i, 0)),
            ga2_spec,
            pl.BlockSpec((1, d), lambda i, *_: (0, 0)),
            pl.BlockSpec(memory_space=pl.ANY),
        ],
        out_specs=[
            pl.BlockSpec((tm, d), lambda i, *_: (jnp.minimum(i, nta - 1), 0)),
            pl.BlockSpec((tm, d), lambda i, *_: (jnp.maximum(i - nta, 0), 0)),
        ],
        scratch_shapes=[
            pltpu.VMEM((2, GROUPED_ROWS // ROW_GROUP, ROW_GROUP, d // 2), U32),
            pltpu.SemaphoreType.DMA((2,)),
        ],
    )
    return pl.pallas_call(
        functools.partial(_combine_kernel, tm=tm, nta=nta),
        grid_spec=grid_spec,
        out_shape=[jax.ShapeDtypeStruct((nta * tm, d), F32), jax.ShapeDtypeStruct((ntb * tm, d), F32)],
        compiler_params=_params(("arbitrary",)),
        name="moe_combine",
    )(chunks, nch, pstart, pos_t, wl_t, xs_base, ga2, gfin, yb)


def _moe_and_final(src_a, src_b, mods, wts, tm, bm):
    nta = src_a[0].shape[0] // tm
    ga1, sh2, sc2, ga2 = mods
    (w_out_b, g_ffn, wsgu_b, wsd_b, wr_t, rb, w_gate_up_l, w_down_l, g_final) = wts
    xs_base, h2, pos, wl, chunks, nch, cnt = _post(
        src_a, src_b, w_out_b, ga1, g_ffn, sc2, sh2, ga2, wsgu_b, wsd_b, wr_t, rb, tm)
    nt = xs_base.shape[0] // tm
    counts = cnt[:, 0].astype(I32)
    padded = (counts + bm - 1) // bm * bm
    pend = jnp.cumsum(padded)
    pstart = pend - padded
    nblk = -(-(nt * GROUPED_ROWS) // bm) + N_EXPERTS
    nused = (pend[-1] // bm).astype(I32)
    blk_row = jnp.minimum(jnp.arange(nblk, dtype=I32), nused - 1) * bm
    be = jnp.sum((pend[None, :] <= blk_row[:, None]).astype(I32), axis=1)
    chunks = chunks.reshape(-1)
    nch = nch[:, 0, 0]
    pslab = pstart // ROW_GROUP
    xs = _dispatch(chunks, nch, pslab, pend // ROW_GROUP, pos, h2, nblk * bm, tm, bm)
    xs = xs.reshape(nblk * bm, xs.shape[-1])
    region_end = jnp.sum(jnp.where(be[:, None] == jnp.arange(N_EXPERTS, dtype=I32)[None, :],
                                   (pstart + counts)[None, :], 0), axis=1)
    valid = jnp.clip(region_end - jnp.arange(nblk, dtype=I32) * bm, 0, bm)
    blk = jnp.arange(nblk, dtype=I32)
    first = ((blk < nused) & ((blk == 0) | (be != jnp.roll(be, 1)))).astype(I32)
    slot = (jnp.cumsum(first) - 1) % 2
    eids = jnp.arange(N_EXPERTS, dtype=I32)
    later_nonempty = (eids[None, :] > eids[:, None]) & (counts[None, :] > 0)
    next_tab = jnp.min(jnp.where(later_nonempty, eids[None, :], N_EXPERTS), axis=1)
    next_tab = jnp.where(next_tab == N_EXPERTS, -1, next_tab)
    next_e = jnp.sum(jnp.where(be[:, None] == eids[None, :], next_tab[None, :], 0), axis=1)
    yb = _experts(be, nused.reshape(1), valid, first, slot.astype(I32), next_e.astype(I32), xs,
                  w_gate_up_l, w_down_l, bm)
    yb = yb.reshape(nblk * bm // ROW_GROUP, ROW_GROUP, yb.shape[-1])
    return _combine(chunks, nch, pslab, pos.T, wl.T, xs_base, ga2, g_final, yb, tm, nta)


def _expand(mod, reps):
    if mod.shape[0] == 1:
        return mod
    return jnp.repeat(mod, reps, axis=0)


def kernel(x_prompt, x_sample, cache_k, cache_v, state_hgrn, c_prompt, c_sample, w_ada, b_ada,
           norm_mix, norm_ffn, norm_final, w_in, w_out, hg_lb_logits, hg_norm, da_lambda, da_norm,
           rel_bias_table, w_router, router_bias, w_gate_up, w_down, ws_gate_up, ws_down):
    depth = w_in.shape[0]
    assert depth == 1 and hg_lb_logits.shape[0] == 2
    bp, tp, d = x_prompt.shape
    bs, ts, _ = x_sample.shape
    assert bp == 1
    past = cache_k.shape[2]
    l = 0

    rows = -(-(bp + bs) // 8) * 8
    c_all = jnp.zeros((rows, d), F32).at[:bp].set(c_prompt).at[bp:bp + bs].set(c_sample)
    mod = _adaln(c_all, w_ada[l], b_ada[l])
    mod_p = [mod[0:bp, j * d:(j + 1) * d] for j in range(6)]
    mod_s = [_expand(mod[bp:bp + bs, j * d:(j + 1) * d], ts) for j in range(6)]

    w_in_b = w_in[l].astype(BF16)
    w_out_b = w_out[l].astype(BF16)
    wsgu_b = ws_gate_up[l].astype(BF16)
    wsd_b = ws_down[l].astype(BF16)
    wr_t = w_router[l].T
    g_mix = norm_mix[l].reshape(1, d)
    g_ffn = norm_ffn[l].reshape(1, d)
    g_final = norm_final.reshape(1, d)
    moe_w = (w_out_b, g_ffn, wsgu_b, wsd_b, wr_t, router_bias[l], w_gate_up[l], w_down[l], g_final)

    lam = _lam(da_lambda[l])

    t_att = min(ATT_TILE, tp)
    kk = jnp.arange(t_att, dtype=I32)[:, None]
    qq = jnp.arange(t_att, dtype=I32)[None, :]
    idx_diag = jnp.where((kk // CHUNK) <= (qq // CHUNK), _rel_bucket(kk - qq), MASK_BUCKET)
    idx_prev = _rel_bucket(kk - qq - t_att)
    bias_p = _bias_tiles(rel_bias_table, jnp.stack([idx_diag, idx_prev]).astype(I32),
                         shift_bucket=N_BUCKETS // 2 - 1)
    pad = 128
    qpos = past + jnp.arange(ts, dtype=I32)[:, None]
    idx_sp = _rel_bucket(jnp.arange(past, dtype=I32)[None, :] - qpos)
    kn = jnp.arange(pad, dtype=I32)[None, :]
    idx_sn = jnp.where(kn < ts, _rel_bucket(past + kn - qpos), MASK_BUCKET)
    bias_sp = _bias_tiles(rel_bias_table, idx_sp[None].astype(I32), shift_bucket=None)
    bias_sn = _bias_tiles(rel_bias_table, idx_sn[None].astype(I32), shift_bucket=None)

    xp = x_prompt.reshape(bp * tp, d)
    sh1, sc1, ga1, sh2, sc2, ga2 = mod_p
    assert ATT_TILE == INPROJ_TILE
    zh, qt, kf, vf, kb, vt = _inproj(xp, g_mix, sc1, sh1, w_in_b, t_att, True)
    s_zero = jnp.zeros((bp, HG_HEADS, HG_DIM, HG_DIM), F32)
    ohg_p, sp_new = _hgrn(zh, s_zero, hg_lb_logits, hg_norm[l], bp, tp, min(HGRN_CHUNK, tp))
    oda_p = _attn_prompt(kb, qt, vt, bias_p, lam, da_norm[l], t_att)
    src_p = (xp, ohg_p, oda_p)
    mods_p = (ga1, sh2, sc2, ga2)
    k_prompt = kf.reshape(1, bp, tp, DA_HEADS, 2 * DA_QKDIM)
    v_prompt = vf.reshape(1, bp, tp, DA_HEADS, DA_VDIM)

    ns = bs * ts
    xs_ = x_sample.reshape(ns, d)
    sh1, sc1, ga1, sh2, sc2, ga2 = mod_s
    zh, qs, kf, vf, kb, vb = _inproj(xs_, g_mix, sc1, sh1, w_in_b, ns, False)
    ohg_s, ss_new = _hgrn(zh, state_hgrn[l], hg_lb_logits, hg_norm[l], bs, ts, ts)
    oda_s = _attn_step(qs, cache_k[l], cache_v[l], kb, vb, bias_sp, bias_sn, lam, da_norm[l], bs, ts)
    assert ns == POST_TILE
    mods = tuple(jnp.stack([jnp.broadcast_to(mp, (POST_TILE, d)), ms_])
                 for mp, ms_ in zip(mods_p, (ga1, sh2, sc2, ga2)))
    y_p, y_s = _moe_and_final(src_p, (xs_, ohg_s, oda_s), mods, moe_w, POST_TILE, MOE_BLOCK_ROWS)
    k_sample = kf.reshape(1, bs, ts, DA_HEADS, 2 * DA_QKDIM)
    v_sample = vf.reshape(1, bs, ts, DA_HEADS, DA_VDIM)

    return (y_p.reshape(bp, tp, d), y_s.reshape(bs, ts, d), k_prompt, v_prompt, sp_new[None],
            k_sample, v_sample, ss_new[None].astype(x_sample.dtype))
```

```python
import functools
import math

import numpy as np
import jax
import jax.numpy as jnp
from jax import lax
from jax.experimental import pallas as pl
from jax.experimental.pallas import tpu as pltpu

F32 = jnp.float32
BF16 = jnp.bfloat16
I32 = jnp.int32
U32 = jnp.uint32
HIGHEST = lax.Precision.HIGHEST

D_MODEL = 1024
CHUNK = 64
HG_HEADS = 4
HG_DIM = 128
HG_WIDTH = HG_HEADS * HG_DIM
DA_HEADS = 4
DA_VDIM = 128
DA_QKDIM = 64
DA_WIDTH = DA_HEADS * DA_VDIM
N_BUCKETS = 32
MAX_DIST = 128
N_EXPERTS = 64
TOP_K = 8
N_GROUPS = 8
GROUP_SIZE = N_EXPERTS // N_GROUPS
TOP_GROUPS = 4
D_EXPERT = 256
ROUTE_SCALE = 2.5
EPS = 1e-6
LAM_INIT = 0.8 - 0.6 * math.exp(-0.3 * 0)

LOG2E = math.log2(math.e)
HI_MASK = np.uint32(0xFFFF0000)
NEG_BIG = -1e30
MASK_BUCKET = N_BUCKETS
V7X_VMEM_LIMIT = 48 * 1024 * 1024

ATT_TILE = 512
HGRN_CHUNK = 256
INPROJ_TILE = 512
POST_TILE = 256
MOE_BLOCK_ROWS = 1024
ROW_GROUP = 8
GROUPED_ROWS = -(-(POST_TILE * TOP_K + N_EXPERTS * (ROW_GROUP - 1)) // 256) * 256
CHUNK_SLOTS = -(-(GROUPED_ROWS // ROW_GROUP) // 128) * 128
CHUNK_EXPERT_SHIFT = 24
CHUNK_UNROLL = 4
WAIT_CHUNKS = 16


def _sigmoid(x):
    return 1.0 / (1.0 + jnp.exp(-x))


def _silu(x):
    return x * _sigmoid(x)


def _dot(a, b, **kw):
    return jnp.dot(a, b, preferred_element_type=F32, **kw)


def _dot_nt(a, b, **kw):
    return lax.dot_general(a, b, (((1,), (1,)), ((), ())), preferred_element_type=F32, **kw)


def _dot_tn(a, b, **kw):
    return lax.dot_general(a, b, (((0,), (0,)), ((), ())), preferred_element_type=F32, **kw)


def _pack_bf16_pair(lo, hi):
    lo_bits = lax.bitcast_convert_type(lo.astype(BF16).astype(F32), U32)
    hi_bits = lax.bitcast_convert_type(hi.astype(BF16).astype(F32), U32)
    return (lo_bits >> 16) | (hi_bits & HI_MASK)


def _pack_exact_bf16_pair(lo, hi):
    return (lax.bitcast_convert_type(lo, U32) >> 16) | (lax.bitcast_convert_type(hi, U32) & HI_MASK)


def _unpack_bf16_pair(w):
    lo = lax.bitcast_convert_type(w << 16, F32)
    hi = lax.bitcast_convert_type(w & HI_MASK, F32)
    return lo, hi


def _params(sem, vmem=V7X_VMEM_LIMIT, flags=None):
    return pltpu.CompilerParams(dimension_semantics=sem, vmem_limit_bytes=vmem, flags=flags)


def _adaln_kernel(c_ref, w_ref, b_ref, o_ref):
    s = _silu(c_ref[...])
    o_ref[...] = _dot(s, w_ref[...], precision=HIGHEST) + b_ref[...]


def _adaln(c_all, w_ada, b_ada):
    rows, d = c_all.shape
    cols = w_ada.shape[1]
    blk = 1024
    return pl.pallas_call(
        _adaln_kernel,
        grid=(cols // blk,),
        in_specs=[
            pl.BlockSpec((rows, d), lambda j: (0, 0)),
            pl.BlockSpec((d, blk), lambda j: (0, j)),
            pl.BlockSpec((1, blk), lambda j: (0, j)),
        ],
        out_specs=pl.BlockSpec((rows, blk), lambda j: (0, j)),
        out_shape=jax.ShapeDtypeStruct((rows, cols), F32),
        compiler_params=_params(("parallel",)),
        name="adaln",
    )(c_all, w_ada, b_ada.reshape(1, cols))


def _lam_kernel(l_ref, o_ref):
    l = l_ref[...].astype(F32)
    a = jnp.sum(l[0:1] * l[1:2], axis=-1, keepdims=True)
    b = jnp.sum(l[2:3] * l[3:4], axis=-1, keepdims=True)
    lam = jnp.exp(a) - jnp.exp(b) + LAM_INIT
    o_ref[...] = jnp.broadcast_to(lam, o_ref.shape)


def _lam(da_lambda_l):
    return pl.pallas_call(
        _lam_kernel,
        out_shape=jax.ShapeDtypeStruct((8, 128), F32),
        name="lam",
    )(da_lambda_l)


def _rel_bucket(rel):
    nb = N_BUCKETS // 2
    max_exact = nb // 2
    side = jnp.where(rel > 0, nb, 0)
    n = jnp.abs(rel)
    large = max_exact + (jnp.log(jnp.maximum(n, 1).astype(F32) / max_exact)
                         / math.log(MAX_DIST / max_exact) * (nb - max_exact)).astype(I32)
    large = jnp.minimum(large, nb - 1)
    return side + jnp.where(n < max_exact, n, large)


def _bias_kernel(tab_ref, idx_ref, o_ref, *, shift_bucket):
    h = pl.program_id(0)
    idx = idx_ref[...]
    shift = tab_ref[shift_bucket, h] if shift_bucket is not None else 0.0
    acc = jnp.zeros(idx.shape, F32)
    for j in range(N_BUCKETS):
        acc = jnp.where(idx == j, (tab_ref[j, h] - shift) * LOG2E, acc)
    o_ref[...] = jnp.where(idx == MASK_BUCKET, NEG_BIG, acc)


def _bias_tiles(table, idx, *, shift_bucket):
    k, r, c = idx.shape
    return pl.pallas_call(
        functools.partial(_bias_kernel, shift_bucket=shift_bucket),
        grid=(DA_HEADS, k),
        in_specs=[
            pl.BlockSpec(memory_space=pltpu.SMEM),
            pl.BlockSpec((None, r, c), lambda h, d: (d, 0, 0)),
        ],
        out_specs=pl.BlockSpec((None, None, r, c), lambda h, d: (h, d, 0, 0)),
        out_shape=jax.ShapeDtypeStruct((DA_HEADS, k, r, c), F32),
        compiler_params=_params(("parallel", "parallel")),
        name="rel_bias",
    )(table, idx)


def _inproj_kernel(x_ref, g_ref, sc_ref, sh_ref, w_ref,
                   zh_ref, q_ref, k_ref, v_ref, kb_ref, vb_ref, *, transposed):
    x = x_ref[...]
    ms = jnp.mean(x * x, axis=-1, keepdims=True)
    h = x * lax.rsqrt(ms + EPS) * g_ref[...]
    h = h * (1.0 + sc_ref[...]) + sh_ref[...]
    hb = h.astype(BF16)
    c0 = 4 * HG_WIDTH
    zh_ref[...] = _dot(hb, w_ref[:, 0:c0])
    zq = _dot(hb, w_ref[:, c0:c0 + DA_WIDTH]) * (DA_QKDIM ** -0.5 * LOG2E)
    zk = _dot(hb, w_ref[:, c0 + DA_WIDTH:c0 + 2 * DA_WIDTH])
    for hd in range(DA_HEADS):
        k_ref[:, hd, :] = zk[:, hd * DA_VDIM:(hd + 1) * DA_VDIM]
    kb_ref[...] = zk.astype(BF16)
    zv = _dot(hb, w_ref[:, c0 + 2 * DA_WIDTH:c0 + 3 * DA_WIDTH])
    for hd in range(DA_HEADS):
        v_ref[:, hd, :] = zv[:, hd * DA_VDIM:(hd + 1) * DA_VDIM]
    if transposed:
        q_ref[...] = zq.T.astype(BF16)
        vb_ref[...] = zv.T.astype(BF16).reshape(vb_ref.shape)
    else:
        q_ref[...] = zq.astype(BF16)
        vb_ref[...] = zv.astype(BF16)


def _mod_spec(mod, tm):
    if mod.shape[0] == 1:
        return pl.BlockSpec((1, mod.shape[1]), lambda i: (0, 0))
    return pl.BlockSpec((tm, mod.shape[1]), lambda i: (i, 0))


def _inproj(x, g, sc, sh, w_in_b, tm, transposed):
    n, d = x.shape
    cols = w_in_b.shape[1]
    row = lambda i: (i, 0)
    if transposed:
        q_spec = pl.BlockSpec((DA_WIDTH, tm), lambda i: (0, i))
        q_shape = jax.ShapeDtypeStruct((DA_WIDTH, n), BF16)
        vb_spec = pl.BlockSpec((DA_HEADS, None, DA_VDIM, tm), lambda i: (0, i, 0, 0))
        vb_shape = jax.ShapeDtypeStruct((DA_HEADS, n // tm, DA_VDIM, tm), BF16)
    else:
        q_spec = vb_spec = pl.BlockSpec((tm, DA_WIDTH), row)
        q_shape = vb_shape = jax.ShapeDtypeStruct((n, DA_WIDTH), BF16)
    return pl.pallas_call(
        functools.partial(_inproj_kernel, transposed=transposed),
        grid=(n // tm,),
        in_specs=[
            pl.BlockSpec((tm, d), row),
            pl.BlockSpec((1, d), lambda i: (0, 0)),
            _mod_spec(sc, tm),
            _mod_spec(sh, tm),
            pl.BlockSpec((d, cols), lambda i: (0, 0)),
        ],
        out_specs=[
            pl.BlockSpec((tm, 4 * HG_WIDTH), row),
            q_spec,
            pl.BlockSpec((tm, DA_HEADS, DA_VDIM), lambda i: (i, 0, 0)),
            pl.BlockSpec((tm, DA_HEADS, DA_VDIM), lambda i: (i, 0, 0)),
            pl.BlockSpec((tm, DA_WIDTH), row),
            vb_spec,
        ],
        out_shape=[
            jax.ShapeDtypeStruct((n, 4 * HG_WIDTH), F32),
            q_shape,
            jax.ShapeDtypeStruct((n, DA_HEADS, DA_VDIM), F32),
            jax.ShapeDtypeStruct((n, DA_HEADS, DA_VDIM), F32),
            jax.ShapeDtypeStruct((n, DA_WIDTH), BF16),
            vb_shape,
        ],
        compiler_params=_params(("parallel",)),
        name="inproj",
    )(x, g, sc, sh, w_in_b)


def _hgrn_consts(c):
    levels = int(round(math.log2(c)))
    assert 1 << levels == c and levels >= 3
    t = np.arange(c)[:, None]
    r = np.arange(c)[None, :]
    tri = (r <= t).astype(np.float32)
    x = np.maximum(t ^ r, 1)
    lv = np.where(t == r, -1, np.where(t > r, np.floor(np.log2(x)).astype(np.int64), -2))
    return jnp.asarray(tri, dtype=BF16), jnp.asarray(lv, dtype=I32), levels


def _hgrn_kernel(zh_ref, s0_ref, lbl_ref, gain_ref, mall_ref, lv_ref,
                 o_ref, sout_ref, st_ref, b_ref, *, c, levels):
    ci = pl.program_id(1)

    @pl.when(ci == 0)
    def _():
        for h in range(HG_HEADS):
            st_ref[h] = s0_ref[h].astype(F32).T

    lbl = lbl_ref[...].astype(F32)
    mx = jnp.maximum(lbl[0:1], lbl[1:2])
    e0 = jnp.exp(lbl[0:1] - mx)
    e1 = jnp.exp(lbl[1:2] - mx)
    lb = e0 / (e0 + e1)

    xq = zh_ref[:, 0:HG_WIDTH]
    xf = zh_ref[:, HG_WIDTH:2 * HG_WIDTH]
    q = _silu(xq)
    y = lb + (1.0 - lb) * _sigmoid(xf)
    logf = jnp.log(y)
    kk = 1.0 - y

    l1 = logf.astype(BF16)
    r1 = logf - l1.astype(F32)
    l2 = r1.astype(BF16)
    l3 = (r1 - l2.astype(F32)).astype(BF16)
    tri = mall_ref[...]
    b = _dot(tri, l1) + _dot(tri, l2) + _dot(tri, l3)
    b_ref[...] = b
    trow = lax.broadcasted_iota(I32, b.shape, 0)

    def level_exponent(l):
        m = 1 << l
        later = (trow & m) != 0
        if l == 0:
            return jnp.where(later, logf, 0.0)
        if l == 1:
            below = pltpu.roll(logf, 1, 0)
            above = pltpu.roll(logf, c - 1, 0)
            low = (trow & 1) != 0
            return jnp.where(later, jnp.where(low, logf + below, logf), jnp.where(low, 0.0, above))
        mid = jnp.concatenate(
            [jnp.broadcast_to(b_ref[k * 2 * m + m - 1:k * 2 * m + m, :], (2 * m, b.shape[1]))
             for k in range(c // (2 * m))], axis=0)
        return jnp.where(later, b - mid, mid - b)

    factors = [jnp.exp(level_exponent(l)) for l in range(levels)]
    lv = lv_ref[...]
    gain = gain_ref[...].astype(F32)
    for h in range(HG_HEADS):
        sl = slice(h * HG_DIM, (h + 1) * HG_DIM)
        qh = q[:, sl]
        kh = kk[:, sl]
        ih = zh_ref[:, 2 * HG_WIDTH + h * HG_DIM:2 * HG_WIDTH + (h + 1) * HG_DIM]
        gh = zh_ref[:, 3 * HG_WIDTH + h * HG_DIM:3 * HG_WIDTH + (h + 1) * HG_DIM]
        bh = b[:, sl]
        ihb = ih.astype(BF16)
        a = jnp.where(lv == -1, _dot_nt(qh.astype(BF16), kh.astype(BF16)), 0.0)
        for l in range(levels):
            f = factors[l][:, sl]
            p = _dot_nt((qh * f).astype(BF16), (kh * f).astype(BF16))
            a = jnp.where(lv == l, p, a)
        st = st_ref[h]
        o = _dot(a.astype(BF16), ihb) + _dot_nt((qh * jnp.exp(bh)).astype(BF16), st.astype(BF16))
        bl = bh[c - 1:c, :]
        kd = (kh * jnp.exp(bl - bh)).astype(BF16)
        st_ref[h] = st * jnp.exp(bl) + _dot_tn(ihb, kd)
        ms = jnp.mean(o * o, axis=-1, keepdims=True)
        on = o * lax.rsqrt(ms + EPS) * gain
        o_ref[:, sl] = (on * _silu(gh)).astype(o_ref.dtype)

    @pl.when(ci == pl.num_programs(1) - 1)
    def _():
        for h in range(HG_HEADS):
            sout_ref[h] = st_ref[h].T.astype(sout_ref.dtype)


def _hgrn(zh, s0, lb_logits, gain, batch, seq, c):
    mall, lv, levels = _hgrn_consts(c)
    nc = seq // c
    return pl.pallas_call(
        functools.partial(_hgrn_kernel, c=c, levels=levels),
        grid=(batch, nc),
        in_specs=[
            pl.BlockSpec((c, 4 * HG_WIDTH), lambda b, i: (b * nc + i, 0)),
            pl.BlockSpec((None, HG_HEADS, HG_DIM, HG_DIM), lambda b, i: (b, 0, 0, 0)),
            pl.BlockSpec(lb_logits.shape, lambda b, i: (0, 0)),
            pl.BlockSpec((1, HG_DIM), lambda b, i: (0, 0)),
            pl.BlockSpec(mall.shape, lambda b, i: (0, 0)),
            pl.BlockSpec(lv.shape, lambda b, i: (0, 0)),
        ],
        out_specs=[
            pl.BlockSpec((c, HG_WIDTH), lambda b, i: (b * nc + i, 0)),
            pl.BlockSpec((None, HG_HEADS, HG_DIM, HG_DIM), lambda b, i: (b, 0, 0, 0)),
        ],
        out_shape=[
            jax.ShapeDtypeStruct((batch * seq, HG_WIDTH), BF16),
            jax.ShapeDtypeStruct((batch, HG_HEADS, HG_DIM, HG_DIM), F32),
        ],
        scratch_shapes=[pltpu.VMEM((HG_HEADS, HG_DIM, HG_DIM), F32),
                        pltpu.VMEM((c, HG_WIDTH), F32)],
        compiler_params=_params(("parallel", "arbitrary")),
        name="hgrn2",
    )(zh, s0, lb_logits, gain.reshape(1, HG_DIM), mall, lv)


def _attn_kernel(k_ref, qt_ref, vt_ref, bias_ref, lam_ref, gain_ref,
                 o_ref, qz_ref, m_ref, l_ref, acc_ref, s_ref, *, t):
    i = pl.program_id(1)
    qt = qt_ref[...]
    row = lax.broadcasted_iota(I32, qt.shape, 0)
    zero = jnp.zeros_like(qt)
    qz_ref[:, 0:t] = jnp.where(row < DA_QKDIM, qt, zero)
    qz_ref[:, t:2 * t] = jnp.where(row >= DA_QKDIM, qt, zero)
    m_ref[...] = jnp.full(m_ref.shape, NEG_BIG, F32)
    l_ref[...] = jnp.zeros(l_ref.shape, F32)
    acc_ref[...] = jnp.zeros(acc_ref.shape, F32)

    def scores(j, buf):
        kt = k_ref[pl.ds(pl.multiple_of(j * t, t), t), :]
        s_ref[buf] = _dot(kt, qz_ref[...])

    def consume(j, buf, bias_idx):
        s = s_ref[buf]
        if bias_idx is not None:
            b = bias_ref[bias_idx]
            s = jnp.concatenate([s[:, 0:t] + b, s[:, t:2 * t] + b], axis=1)
        m_prev = m_ref[...]
        m_new = jnp.maximum(m_prev, jnp.max(s, axis=0, keepdims=True))
        alpha = jnp.exp2(m_prev - m_new)
        pr = jnp.exp2(s - m_new)
        l_ref[...] = alpha * l_ref[...] + jnp.sum(pr, axis=0, keepdims=True)
        acc_ref[...] = alpha * acc_ref[...] + _dot(vt_ref[j], pr.astype(BF16))
        m_ref[...] = m_new

    n_far = jnp.maximum(i - 1, 0)

    @pl.when(n_far > 0)
    def _():
        scores(0, 0)

    def far_pair(p, carry):
        j = 2 * p
        scores(j + 1, 1)
        consume(j, 0, None)
        scores(jnp.minimum(j + 2, n_far - 1), 0)
        consume(j + 1, 1, None)
        return carry

    lax.fori_loop(0, n_far // 2, far_pair, 0)

    @pl.when(lax.rem(n_far, 2) == 1)
    def _():
        consume(n_far - 1, 0, None)

    @pl.when(i >= 1)
    def _():
        scores(i - 1, 0)
        scores(i, 1)
        consume(i - 1, 0, 1)
        consume(i, 1, 0)

    @pl.when(i == 0)
    def _():
        scores(i, 1)
        consume(i, 1, 0)

    lam = lam_ref[0:1, 0:1]
    l = l_ref[...]
    acc = acc_ref[...]
    o = acc[:, 0:t] / l[:, 0:t] - lam * (acc[:, t:2 * t] / l[:, t:2 * t])
    ms = jnp.mean(o * o, axis=0, keepdims=True)
    on = o * lax.rsqrt(ms + EPS) * gain_ref[...].astype(F32) * (1.0 - LAM_INIT)
    o_ref[...] = on.T.astype(o_ref.dtype)


def _attn_prompt(kb, qt, vt, bias, lam, gain, t):
    n = kb.shape[0]
    nt = n // t
    return pl.pallas_call(
        functools.partial(_attn_kernel, t=t),
        grid=(DA_HEADS, nt),
        in_specs=[
            pl.BlockSpec((n, DA_VDIM), lambda h, i: (0, h)),
            pl.BlockSpec((DA_VDIM, t), lambda h, i: (h, i)),
            pl.BlockSpec((None, nt, DA_VDIM, t), lambda h, i: (h, 0, 0, 0)),
            pl.BlockSpec((None, 2, t, t), lambda h, i: (h, 0, 0, 0)),
            pl.BlockSpec((8, 128), lambda h, i: (0, 0)),
            pl.BlockSpec((DA_VDIM, 1), lambda h, i: (0, 0)),
        ],
        out_specs=pl.BlockSpec((t, DA_VDIM), lambda h, i: (i, h)),
        out_shape=jax.ShapeDtypeStruct((n, DA_WIDTH), BF16),
        scratch_shapes=[
            pltpu.VMEM((DA_VDIM, 2 * t), BF16),
            pltpu.VMEM((1, 2 * t), F32),
            pltpu.VMEM((1, 2 * t), F32),
            pltpu.VMEM((DA_VDIM, 2 * t), F32),
            pltpu.VMEM((2, t, 2 * t), F32),
        ],
        compiler_params=_params(("parallel", "parallel")),
        name="diff_attn_prompt",
    )(kb, qt, vt, bias, lam, gain.reshape(DA_VDIM, 1))


def _attn_step_kernel(q_ref, kp_ref, vp_ref, kn_ref, vn_ref, bp_ref, bn_ref, lam_ref, gain_ref,
                      o_ref, *, tq, pad):
    lam = lam_ref[0:1, 0:1]
    gain = gain_ref[...].astype(F32)
    zpad = jnp.zeros((pad - tq, DA_VDIM), BF16)
    for h in range(DA_HEADS):
        hs = slice(h * DA_VDIM, (h + 1) * DA_VDIM)
        q = q_ref[:, hs]
        lane = lax.broadcasted_iota(I32, q.shape, 1)
        zero = jnp.zeros_like(q)
        qz = jnp.concatenate([jnp.where(lane < DA_QKDIM, q, zero),
                              jnp.where(lane >= DA_QKDIM, q, zero)], axis=0)
        kp = kp_ref[:, h, :].astype(BF16)
        vp = vp_ref[:, h, :].astype(BF16)
        kn = jnp.concatenate([kn_ref[:, hs], zpad], axis=0)
        vn = jnp.concatenate([vn_ref[:, hs], zpad], axis=0)
        bp = bp_ref[h, 0]
        bn = bn_ref[h, 0]
        sp = _dot_nt(qz, kp) + jnp.concatenate([bp, bp], axis=0)
        sn = _dot_nt(qz, kn) + jnp.concatenate([bn, bn], axis=0)
        m = jnp.maximum(jnp.max(sp, axis=-1, keepdims=True), jnp.max(sn, axis=-1, keepdims=True))
        pp = jnp.exp2(sp - m)
        pn = jnp.exp2(sn - m)
        l = jnp.sum(pp, axis=-1, keepdims=True) + jnp.sum(pn, axis=-1, keepdims=True)
        acc = _dot(pp.astype(BF16), vp) + _dot(pn.astype(BF16), vn)
        on = acc / l
        o = on[0:tq] - lam * on[tq:2 * tq]
        ms = jnp.mean(o * o, axis=-1, keepdims=True)
        o = o * lax.rsqrt(ms + EPS) * gain * (1.0 - LAM_INIT)
        o_ref[:, hs] = o.astype(o_ref.dtype)


def _attn_step(qs, cache_k_l, cache_v_l, kb, vb, bias_p, bias_n, lam, gain, batch, tq):
    past = cache_k_l.shape[1]
    pad = bias_n.shape[-1]
    cache_spec = pl.BlockSpec((None, past, DA_HEADS, DA_VDIM), lambda b: (b, 0, 0, 0))
    row = pl.BlockSpec((tq, DA_WIDTH), lambda b: (b, 0))
    return pl.pallas_call(
        functools.partial(_attn_step_kernel, tq=tq, pad=pad),
        grid=(batch,),
        in_specs=[
            row, cache_spec, cache_spec, row, row,
            pl.BlockSpec(bias_p.shape, lambda b: (0, 0, 0, 0)),
            pl.BlockSpec(bias_n.shape, lambda b: (0, 0, 0, 0)),
            pl.BlockSpec((8, 128), lambda b: (0, 0)),
            pl.BlockSpec((1, DA_VDIM), lambda b: (0, 0)),
        ],
        out_specs=row,
        out_shape=jax.ShapeDtypeStruct((batch * tq, DA_WIDTH), BF16),
        compiler_params=_params(("parallel",)),
        name="diff_attn_step",
    )(qs, cache_k_l, cache_v_l, kb, vb, bias_p, bias_n, lam, gain.reshape(1, DA_VDIM))


def _post_kernel(xa_ref, ohga_ref, odaa_ref, xb_ref, ohgb_ref, odab_ref,
                 wout_ref, ga1_ref, g_ref, sc_ref, sh_ref, ga2_ref,
                 wsgu_ref, wsd_ref, wrt_ref, rb_ref, tri_ref, ltri_ref,
                 xs_ref, h2_ref, pos_ref, wl_ref, chunk_ref, nch_ref, cnt_ref, carry_ref, *, tm, nta):
    i = pl.program_id(0)

    @pl.when(i == 0)
    def _():
        carry_ref[...] = jnp.zeros(carry_ref.shape, F32)

    second = i >= nta
    x = jnp.where(second, xb_ref[...], xa_ref[...])
    ohg = jnp.where(second, ohgb_ref[...], ohga_ref[...])
    oda = jnp.where(second, odab_ref[...], odaa_ref[...])
    mix = _dot(ohg, wout_ref[0:HG_WIDTH, :]) + _dot(oda, wout_ref[HG_WIDTH:, :])
    x1 = x + ga1_ref[...] * mix
    ms = jnp.mean(x1 * x1, axis=-1, keepdims=True)
    h2 = x1 * lax.rsqrt(ms + EPS) * g_ref[...]
    h2 = h2 * (1.0 + sc_ref[...]) + sh_ref[...]
    h2b = h2.astype(BF16)
    h2_ref[...] = h2b
    gu = _dot(h2b, wsgu_ref[...])
    act = (_silu(gu[:, 0:D_EXPERT]) * gu[:, D_EXPERT:]).astype(BF16)
    xs_ref[...] = x1 + ga2_ref[...] * _dot(act, wsd_ref[...])

    logits = _dot_nt(wrt_ref[...], h2, precision=HIGHEST)
    score = _sigmoid(logits)
    sel = score + rb_ref[...]
    sub = lax.broadcasted_iota(I32, (GROUP_SIZE, tm), 0)
    gscore = []
    for g in range(N_GROUPS):
        v = sel[g * GROUP_SIZE:(g + 1) * GROUP_SIZE, :]
        m1 = jnp.max(v, axis=0, keepdims=True)
        i1 = jnp.min(jnp.where(v == m1, sub, GROUP_SIZE), axis=0, keepdims=True)
        m2 = jnp.max(jnp.where(sub == i1, -jnp.inf, v), axis=0, keepdims=True)
        gscore.append(m1 + m2)
    gsel = []
    for g in range(N_GROUPS):
        ahead = jnp.zeros((1, tm), F32)
        for g2 in range(N_GROUPS):
            if g2 == g:
                continue
            tie = 1.0 if g2 < g else 0.0
            ahead = ahead + jnp.where(gscore[g2] > gscore[g], 1.0,
                                      jnp.where(gscore[g2] == gscore[g], tie, 0.0))
        gsel.append(ahead < TOP_GROUPS)
    selm = jnp.concatenate(
        [jnp.where(gsel[g], sel[g * GROUP_SIZE:(g + 1) * GROUP_SIZE, :], -jnp.inf)
         for g in range(N_GROUPS)], axis=0)
    eio = lax.broadcasted_iota(I32, (N_EXPERTS, tm), 0)
    ahead = jnp.zeros((N_EXPERTS, tm), F32)
    for e2 in range(N_EXPERTS):
        row = selm[e2:e2 + 1, :]
        tie = jnp.where(eio > e2, 1.0, 0.0)
        ahead = ahead + jnp.where(row > selm, 1.0, jnp.where(row == selm, tie, 0.0))
    chosen = jnp.where(selm > -jnp.inf, jnp.where(ahead < TOP_K, 1.0, 0.0), 0.0)
    w = chosen * score
    wn = w / jnp.sum(w, axis=0, keepdims=True) * ROUTE_SCALE

    chb = chosen.astype(BF16)
    before = _dot(chb, tri_ref[...])
    tot = _dot(chb, jnp.ones((tm, 128), BF16))
    run = jnp.floor((tot + (ROW_GROUP - 1)) * (1.0 / ROW_GROUP)) * ROW_GROUP
    tile_base = _dot(ltri_ref[...], run.astype(BF16))
    carry = carry_ref[...]
    carry_ref[...] = carry + run
    cnt_ref[...] = carry + run
    pos = jnp.concatenate([tile_base] * (tm // 128), axis=1) + before

    widen = lambda v: jnp.concatenate([v] * (CHUNK_SLOTS // 128), axis=1)
    crow = lax.broadcasted_iota(I32, (N_EXPERTS, CHUNK_SLOTS), 1).astype(F32) * ROW_GROUP
    erow = lax.broadcasted_iota(I32, (N_EXPERTS, CHUNK_SLOTS), 0).astype(F32)
    owner = jnp.sum(jnp.where(widen(tile_base + run) <= crow, 1.0, 0.0), axis=0, keepdims=True)
    region_row = jnp.sum(jnp.where(owner == erow, widen(carry - tile_base), 0.0),
                         axis=0, keepdims=True) + crow[0:1]
    region_slab = (region_row * (1.0 / ROW_GROUP)).astype(I32)
    chunk_ref[...] = owner.astype(I32) * (1 << CHUNK_EXPERT_SHIFT) + region_slab
    nch_ref[...] = jnp.sum(run * (1.0 / ROW_GROUP), axis=0, keepdims=True).astype(I32)

    for r in range(TOP_K):
        pick = jnp.where(ahead == r, chosen, 0.0)
        pos_ref[r:r + 1, :] = jnp.sum(pick * pos, axis=0, keepdims=True).astype(I32)
        wl_ref[r:r + 1, :] = jnp.sum(pick * wn, axis=0, keepdims=True)


def _post(src_a, src_b, w_out_b, ga1, g, sc, sh, ga2, wsgu_b, wsd_b, wr_t, rb, tm):
    (xa, ohga, odaa), (xb, ohgb, odab) = src_a, src_b
    d = xa.shape[1]
    nta, ntb = xa.shape[0] // tm, xb.shape[0] // tm
    nt = nta + ntb
    n = nt * tm
    tri = jnp.asarray(np.triu(np.ones((tm, tm), np.float32), k=1), dtype=BF16)
    ltri = jnp.asarray(np.tril(np.ones((N_EXPERTS, N_EXPERTS), np.float32), k=-1), dtype=BF16)
    row = lambda i: (i, 0)
    row_a = lambda i: (jnp.minimum(i, nta - 1), 0)
    row_b = lambda i: (jnp.maximum(i - nta, 0), 0)
    col = lambda i: (0, i)
    full = lambda i: (0, 0)
    mod = pl.BlockSpec((None, tm, d), lambda i: (jnp.minimum(i // nta, 1), 0, 0))
    return pl.pallas_call(
        functools.partial(_post_kernel, tm=tm, nta=nta),
        grid=(nt,),
        in_specs=[
            pl.BlockSpec((tm, d), row_a),
            pl.BlockSpec((tm, HG_WIDTH), row_a),
            pl.BlockSpec((tm, DA_WIDTH), row_a),
            pl.BlockSpec((tm, d), row_b),
            pl.BlockSpec((tm, HG_WIDTH), row_b),
            pl.BlockSpec((tm, DA_WIDTH), row_b),
            pl.BlockSpec(w_out_b.shape, full),
            mod,
            pl.BlockSpec((1, d), full),
            mod,
            mod,
            mod,
            pl.BlockSpec(wsgu_b.shape, full),
            pl.BlockSpec(wsd_b.shape, full),
            pl.BlockSpec(wr_t.shape, full),
            pl.BlockSpec((N_EXPERTS, 1), full),
            pl.BlockSpec((tm, tm), full),
            pl.BlockSpec((N_EXPERTS, N_EXPERTS), full),
        ],
        out_specs=[
            pl.BlockSpec((tm, d), row),
            pl.BlockSpec((tm, d), row),
            pl.BlockSpec((TOP_K, tm), col),
            pl.BlockSpec((TOP_K, tm), col),
            pl.BlockSpec((None, 1, CHUNK_SLOTS), lambda i: (i, 0, 0)),
            pl.BlockSpec((None, 1, 128), lambda i: (i, 0, 0)),
            pl.BlockSpec((N_EXPERTS, 128), full),
        ],
        out_shape=[
            jax.ShapeDtypeStruct((n, d), F32),
            jax.ShapeDtypeStruct((n, d), BF16),
            jax.ShapeDtypeStruct((TOP_K, n), I32),
            jax.ShapeDtypeStruct((TOP_K, n), F32),
            jax.ShapeDtypeStruct((nt, 1, CHUNK_SLOTS), I32),
            jax.ShapeDtypeStruct((nt, 1, 128), I32),
            jax.ShapeDtypeStruct((N_EXPERTS, 128), F32),
        ],
        scratch_shapes=[pltpu.VMEM((N_EXPERTS, 128), F32)],
        compiler_params=_params(("arbitrary",)),
        name="post_mix_router",
    )(xa, ohga, odaa, xb, ohgb, odab, w_out_b, ga1, g, sc, sh, ga2, wsgu_b, wsd_b, wr_t,
      rb.reshape(N_EXPERTS, 1), tri, ltri)


def _start_chunks(tile, chunk_ref, nch_ref, pslab_ref, make_copy):
    n = nch_ref[tile]

    def start(c):
        word = chunk_ref[tile * CHUNK_SLOTS + c]
        expert = lax.shift_right_logical(word, CHUNK_EXPERT_SHIFT)
        region_slab = word & ((1 << CHUNK_EXPERT_SHIFT) - 1)
        make_copy(c, pslab_ref[expert] + region_slab, 1).start()

    def group(g, carry):
        for u in range(CHUNK_UNROLL):
            start(g * CHUNK_UNROLL + u)
        return carry

    def single(c, carry):
        start(c)
        return carry

    groups = n // CHUNK_UNROLL
    lax.fori_loop(0, groups, group, 0)
    lax.fori_loop(groups * CHUNK_UNROLL, n, single, 0)


def _wait_chunks(tile, nch_ref, make_copy):
    n = nch_ref[tile]
    many = n // WAIT_CHUNKS

    def wait_many(j, carry):
        make_copy(0, 0, WAIT_CHUNKS).wait()
        return carry

    def wait_one(j, carry):
        make_copy(0, 0, 1).wait()
        return carry

    lax.fori_loop(0, many, wait_many, 0)
    lax.fori_loop(many * WAIT_CHUNKS, n, wait_one, 0)


def _dispatch_kernel(chunk_ref, nch_ref, pstart_ref, pend_ref, pos_ref, h2_ref, xs_hbm,
                     cbuf_ref, zero_ref, zsem, sem, *, tm, bm):
    i = pl.program_id(0)
    nt = pl.num_programs(0)
    bslabs = bm // ROW_GROUP
    nblk = xs_hbm.shape[0] // bslabs
    dh = cbuf_ref.shape[-1]

    def zero_copy(e):
        return pltpu.make_async_copy(zero_ref, xs_hbm.at[pl.ds(pend_ref[e] - bslabs, bslabs)], zsem)

    def tail_copy(b):
        return pltpu.make_async_copy(zero_ref, xs_hbm.at[pl.ds(b * bslabs, bslabs)], zsem)

    @pl.when(i == 0)
    def _():
        zero_ref[...] = jnp.zeros(zero_ref.shape, zero_ref.dtype)
        first_unused = pend_ref[N_EXPERTS - 1] // bslabs

        def zissue(e, carry):
            @pl.when(pend_ref[e] > pstart_ref[e])
            def _():
                zero_copy(e).start()
            return carry

        def zwait(e, carry):
            @pl.when(pend_ref[e] > pstart_ref[e])
            def _():
                zero_copy(e).wait()
            return carry

        def tissue(b, carry):
            tail_copy(b).start()
            return carry

        def twait(b, carry):
            tail_copy(b).wait()
            return carry

        lax.fori_loop(0, N_EXPERTS, zissue, 0)
        lax.fori_loop(first_unused, nblk, tissue, 0)
        lax.fori_loop(0, N_EXPERTS, zwait, 0)
        lax.fori_loop(first_unused, nblk, twait, 0)

    pos = pos_ref[...]
    piota = lax.broadcasted_iota(I32, (GROUPED_ROWS, tm), 0).astype(jnp.int16)
    pos16 = pos.astype(jnp.int16)
    one = jnp.ones((GROUPED_ROWS, tm), BF16)
    perm = jnp.zeros((GROUPED_ROWS, tm), BF16)
    for r in range(TOP_K):
        perm = jnp.where(piota == pos16[r:r + 1, :], one, perm)
    cur = lax.rem(i, 2)
    grouped = _pack_exact_bf16_pair(_dot(perm, h2_ref[:, 0:dh]), _dot(perm, h2_ref[:, dh:]))
    cbuf_ref[cur] = grouped.reshape(cbuf_ref.shape[1:])

    def make_copy(buf):
        def build(tile_slab, buffer_slab, slabs):
            return pltpu.make_async_copy(cbuf_ref.at[buf, pl.ds(tile_slab, slabs)],
                                         xs_hbm.at[pl.ds(buffer_slab, slabs)], sem.at[buf])
        return build

    _start_chunks(i, chunk_ref, nch_ref, pstart_ref, make_copy(cur))

    @pl.when(i > 0)
    def _():
        _wait_chunks(i - 1, nch_ref, make_copy(1 - cur))

    @pl.when(i == nt - 1)
    def _():
        _wait_chunks(i, nch_ref, make_copy(cur))


def _dispatch(chunks, nch, pstart, pend, pos, h2, nrows, tm, bm):
    n, d = h2.shape
    grid_spec = pltpu.PrefetchScalarGridSpec(
        num_scalar_prefetch=4,
        grid=(n // tm,),
        in_specs=[
            pl.BlockSpec((TOP_K, tm), lambda i, *_: (0, i)),
            pl.BlockSpec((tm, d), lambda i, *_: (i, 0)),
        ],
        out_specs=pl.BlockSpec(memory_space=pl.ANY),
        scratch_shapes=[
            pltpu.VMEM((2, GROUPED_ROWS // ROW_GROUP, ROW_GROUP, d // 2), U32),
            pltpu.VMEM((bm // ROW_GROUP, ROW_GROUP, d // 2), U32),
            pltpu.SemaphoreType.DMA(()),
            pltpu.SemaphoreType.DMA((2,)),
        ],
    )
    return pl.pallas_call(
        functools.partial(_dispatch_kernel, tm=tm, bm=bm),
        grid_spec=grid_spec,
        out_shape=jax.ShapeDtypeStruct((nrows // ROW_GROUP, ROW_GROUP, d // 2), U32),
        compiler_params=_params(("arbitrary",)),
        name="moe_dispatch",
    )(chunks, nch, pstart, pend, pos, h2)


def _experts_kernel(be_ref, nu_ref, valid_ref, first_ref, slot_ref, next_ref, x_ref, wgu_hbm, wd_hbm,
                    o_ref, wgu_f_ref, wd_f_ref, wgu_b_ref, wd_b_ref, sem, *, bm):
    i = pl.program_id(0)
    valid = valid_ref[i]
    sub = bm // 2

    def weight_copies(expert, slot):
        return (pltpu.make_async_copy(wgu_hbm.at[expert], wgu_f_ref.at[slot], sem.at[slot, 0]),
                pltpu.make_async_copy(wd_hbm.at[expert], wd_f_ref.at[slot], sem.at[slot, 1]))

    @pl.when(i == 0)
    def _():
        for c in weight_copies(be_ref[0], 0):
            c.start()

    @pl.when(first_ref[i] == 1)
    def _():
        slot = slot_ref[i]
        for c in weight_copies(be_ref[i], slot):
            c.wait()

        @pl.when(next_ref[i] >= 0)
        def _():
            for c in weight_copies(next_ref[i], 1 - slot):
                c.start()

        wgu_b_ref[...] = wgu_f_ref[slot].astype(BF16)
        wd_b_ref[...] = wd_f_ref[slot].astype(BF16)

    for r0 in (0, sub):
        rows = slice(r0, r0 + sub)

        @pl.when(valid > r0)
        def _():
            lo, hi = _unpack_bf16_pair(x_ref[rows, :])
            x = jnp.concatenate([lo.astype(BF16), hi.astype(BF16)], axis=1)
            gu = _dot(x, wgu_b_ref[...])
            act = (_silu(gu[:, 0:D_EXPERT]) * gu[:, D_EXPERT:]).astype(BF16)
            y = _dot(act, wd_b_ref[...])
            o_ref[rows, :] = _pack_bf16_pair(y[:, 0:D_MODEL // 2], y[:, D_MODEL // 2:])

        @pl.when(valid <= r0)
        def _():
            o_ref[rows, :] = jnp.zeros((sub, o_ref.shape[1]), o_ref.dtype)


def _experts(block_e, nused, valid, first, slot, next_e, xs, w_gate_up_l, w_down_l, bm):
    nrows, dh = xs.shape
    d = 2 * dh
    nblk = nrows // bm
    grid_spec = pltpu.PrefetchScalarGridSpec(
        num_scalar_prefetch=6,
        grid=(nblk,),
        in_specs=[
            pl.BlockSpec((bm, dh), lambda i, be, nu, *_: (jnp.minimum(i, nu[0] - 1), 0)),
            pl.BlockSpec(memory_space=pl.ANY),
            pl.BlockSpec(memory_space=pl.ANY),
        ],
        out_specs=pl.BlockSpec((bm, dh), lambda i, *_: (i, 0)),
        scratch_shapes=[
            pltpu.VMEM((2, d, 2 * D_EXPERT), w_gate_up_l.dtype),
            pltpu.VMEM((2, D_EXPERT, d), w_down_l.dtype),
            pltpu.VMEM((d, 2 * D_EXPERT), BF16),
            pltpu.VMEM((D_EXPERT, d), BF16),
            pltpu.SemaphoreType.DMA((2, 2)),
        ],
    )
    return pl.pallas_call(
        functools.partial(_experts_kernel, bm=bm),
        grid_spec=grid_spec,
        out_shape=jax.ShapeDtypeStruct((nrows, dh), U32),
        compiler_params=_params(("arbitrary",)),
        name="moe_experts",
    )(block_e, nused, valid, first, slot, next_e, xs, w_gate_up_l, w_down_l)


def _combine_kernel(chunk_ref, nch_ref, pstart_ref, pos_ref, wl_ref, xs_ref, ga2_ref, gf_ref,
                    yb_hbm, oa_ref, ob_ref, gbuf_ref, sem, *, tm, nta):
    i = pl.program_id(0)
    nt = pl.num_programs(0)
    cur = lax.rem(i, 2)
    refs = (chunk_ref, nch_ref, pstart_ref)

    def make_copy(buf):
        def build(tile_slab, buffer_slab, slabs):
            return pltpu.make_async_copy(yb_hbm.at[pl.ds(buffer_slab, slabs)],
                                         gbuf_ref.at[buf, pl.ds(tile_slab, slabs)], sem.at[buf])
        return build

    @pl.when(i == 0)
    def _():
        gbuf_ref[...] = jnp.zeros(gbuf_ref.shape, gbuf_ref.dtype)
        _start_chunks(0, *refs, make_copy(0))

    @pl.when(i + 1 < nt)
    def _():
        _start_chunks(i + 1, *refs, make_copy(1 - cur))

    _wait_chunks(i, nch_ref, make_copy(cur))

    lo, hi = _unpack_bf16_pair(gbuf_ref[cur].reshape(GROUPED_ROWS, gbuf_ref.shape[-1]))
    g = jnp.concatenate([lo.astype(BF16), hi.astype(BF16)], axis=1)
    pos16 = pos_ref[...].astype(jnp.int16)
    wl = wl_ref[...].astype(BF16)
    liota = lax.broadcasted_iota(I32, (tm, GROUPED_ROWS), 1).astype(jnp.int16)
    a = jnp.zeros((tm, GROUPED_ROWS), BF16)
    for r in range(TOP_K):
        a = jnp.where(liota == pos16[:, r:r + 1], jnp.broadcast_to(wl[:, r:r + 1], a.shape), a)
    routed = _dot(a, g)
    x2 = xs_ref[...] + ga2_ref[...] * routed
    ms = jnp.mean(x2 * x2, axis=-1, keepdims=True)
    y = x2 * lax.rsqrt(ms + EPS) * gf_ref[...]

    @pl.when(i < nta)
    def _():
        oa_ref[...] = y

    @pl.when(i >= nta)
    def _():
        ob_ref[...] = y


def _combine(chunks, nch, pstart, pos_t, wl_t, xs_base, ga2, gfin, yb, tm, nta):
    n, d = xs_base.shape
    ntb = n // tm - nta
    ga2_spec = pl.BlockSpec((None, tm, d), lambda i, *_: (jnp.minimum(i // nta, 1), 0, 0))
    grid_spec = pltpu.PrefetchScalarGridSpec(
        num_scalar_prefetch=3,
        grid=(n // tm,),
        in_specs=[
            pl.BlockSpec((tm, TOP_K), lambda i, *_: (i, 0)),
            pl.BlockSpec((tm, TOP_K), lambda i, *_: (i, 0)),
            pl.BlockSpec((tm, d), lambda i, *_: (i, 0)),
            ga2_spec,
            pl.BlockSpec((1, d), lambda i, *_: (0, 0)),
            pl.BlockSpec(memory_space=pl.ANY),
        ],
        out_specs=[
            pl.BlockSpec((tm, d), lambda i, *_: (jnp.minimum(i, nta - 1), 0)),
            pl.BlockSpec((tm, d), lambda i, *_: (jnp.maximum(i - nta, 0), 0)),
        ],
        scratch_shapes=[
            pltpu.VMEM((2, GROUPED_ROWS // ROW_GROUP, ROW_GROUP, d // 2), U32),
            pltpu.SemaphoreType.DMA((2,)),
        ],
    )
    return pl.pallas_call(
        functools.partial(_combine_kernel, tm=tm, nta=nta),
        grid_spec=grid_spec,
        out_shape=[jax.ShapeDtypeStruct((nta * tm, d), F32), jax.ShapeDtypeStruct((ntb * tm, d), F32)],
        compiler_params=_params(("arbitrary",)),
        name="moe_combine",
    )(chunks, nch, pstart, pos_t, wl_t, xs_base, ga2, gfin, yb)


def _moe_and_final(src_a, src_b, mods, wts, tm, bm):
    nta = src_a[0].shape[0] // tm
    ga1, sh2, sc2, ga2 = mods
    (w_out_b, g_ffn, wsgu_b, wsd_b, wr_t, rb, w_gate_up_l, w_down_l, g_final) = wts
    xs_base, h2, pos, wl, chunks, nch, cnt = _post(
        src_a, src_b, w_out_b, ga1, g_ffn, sc2, sh2, ga2, wsgu_b, wsd_b, wr_t, rb, tm)
    nt = xs_base.shape[0] // tm
    counts = cnt[:, 0].astype(I32)
    padded = (counts + bm - 1) // bm * bm
    pend = jnp.cumsum(padded)
    pstart = pend - padded
    nblk = -(-(nt * GROUPED_ROWS) // bm) + N_EXPERTS
    nused = (pend[-1] // bm).astype(I32)
    blk_row = jnp.minimum(jnp.arange(nblk, dtype=I32), nused - 1) * bm
    be = jnp.sum((pend[None, :] <= blk_row[:, None]).astype(I32), axis=1)
    chunks = chunks.reshape(-1)
    nch = nch[:, 0, 0]
    pslab = pstart // ROW_GROUP
    xs = _dispatch(chunks, nch, pslab, pend // ROW_GROUP, pos, h2, nblk * bm, tm, bm)
    xs = xs.reshape(nblk * bm, xs.shape[-1])
    region_end = jnp.sum(jnp.where(be[:, None] == jnp.arange(N_EXPERTS, dtype=I32)[None, :],
                                   (pstart + counts)[None, :], 0), axis=1)
    valid = jnp.clip(region_end - jnp.arange(nblk, dtype=I32) * bm, 0, bm)
    blk = jnp.arange(nblk, dtype=I32)
    first = ((blk < nused) & ((blk == 0) | (be != jnp.roll(be, 1)))).astype(I32)
    slot = (jnp.cumsum(first) - 1) % 2
    eids = jnp.arange(N_EXPERTS, dtype=I32)
    later_nonempty = (eids[None, :] > eids[:, None]) & (counts[None, :] > 0)
    next_tab = jnp.min(jnp.where(later_nonempty, eids[None, :], N_EXPERTS), axis=1)
    next_tab = jnp.where(next_tab == N_EXPERTS, -1, next_tab)
    next_e = jnp.sum(jnp.where(be[:, None] == eids[None, :], next_tab[None, :], 0), axis=1)
    yb = _experts(be, nused.reshape(1), valid, first, slot.astype(I32), next_e.astype(I32), xs,
                  w_gate_up_l, w_down_l, bm)
    yb = yb.reshape(nblk * bm // ROW_GROUP, ROW_GROUP, yb.shape[-1])
    return _combine(chunks, nch, pslab, pos.T, wl.T, xs_base, ga2, g_final, yb, tm, nta)


def _expand(mod, reps):
    if mod.shape[0] == 1:
        return mod
    return jnp.repeat(mod, reps, axis=0)


def kernel(x_prompt, x_sample, cache_k, cache_v, state_hgrn, c_prompt, c_sample, w_ada, b_ada,
           norm_mix, norm_ffn, norm_final, w_in, w_out, hg_lb_logits, hg_norm, da_lambda, da_norm,
           rel_bias_table, w_router, router_bias, w_gate_up, w_down, ws_gate_up, ws_down):
    depth = w_in.shape[0]
    assert depth == 1 and hg_lb_logits.shape[0] == 2
    bp, tp, d = x_prompt.shape
    bs, ts, _ = x_sample.shape
    assert bp == 1
    past = cache_k.shape[2]
    l = 0

    rows = -(-(bp + bs) // 8) * 8
    c_all = jnp.zeros((rows, d), F32).at[:bp].set(c_prompt).at[bp:bp + bs].set(c_sample)
    mod = _adaln(c_all, w_ada[l], b_ada[l])
    mod_p = [mod[0:bp, j * d:(j + 1) * d] for j in range(6)]
    mod_s = [_expand(mod[bp:bp + bs, j * d:(j + 1) * d], ts) for j in range(6)]

    w_in_b = w_in[l].astype(BF16)
    w_out_b = w_out[l].astype(BF16)
    wsgu_b = ws_gate_up[l].astype(BF16)
    wsd_b = ws_down[l].astype(BF16)
    wr_t = w_router[l].T
    g_mix = norm_mix[l].reshape(1, d)
    g_ffn = norm_ffn[l].reshape(1, d)
    g_final = norm_final.reshape(1, d)
    moe_w = (w_out_b, g_ffn, wsgu_b, wsd_b, wr_t, router_bias[l], w_gate_up[l], w_down[l], g_final)

    lam = _lam(da_lambda[l])

    t_att = min(ATT_TILE, tp)
    kk = jnp.arange(t_att, dtype=I32)[:, None]
    qq = jnp.arange(t_att, dtype=I32)[None, :]
    idx_diag = jnp.where((kk // CHUNK) <= (qq // CHUNK), _rel_bucket(kk - qq), MASK_BUCKET)
    idx_prev = _rel_bucket(kk - qq - t_att)
    bias_p = _bias_tiles(rel_bias_table, jnp.stack([idx_diag, idx_prev]).astype(I32),
                         shift_bucket=N_BUCKETS // 2 - 1)
    pad = 128
    qpos = past + jnp.arange(ts, dtype=I32)[:, None]
    idx_sp = _rel_bucket(jnp.arange(past, dtype=I32)[None, :] - qpos)
    kn = jnp.arange(pad, dtype=I32)[None, :]
    idx_sn = jnp.where(kn < ts, _rel_bucket(past + kn - qpos), MASK_BUCKET)
    bias_sp = _bias_tiles(rel_bias_table, idx_sp[None].astype(I32), shift_bucket=None)
    bias_sn = _bias_tiles(rel_bias_table, idx_sn[None].astype(I32), shift_bucket=None)

    xp = x_prompt.reshape(bp * tp, d)
    sh1, sc1, ga1, sh2, sc2, ga2 = mod_p
    assert ATT_TILE == INPROJ_TILE
    zh, qt, kf, vf, kb, vt = _inproj(xp, g_mix, sc1, sh1, w_in_b, t_att, True)
    s_zero = jnp.zeros((bp, HG_HEADS, HG_DIM, HG_DIM), F32)
    ohg_p, sp_new = _hgrn(zh, s_zero, hg_lb_logits, hg_norm[l], bp, tp, min(HGRN_CHUNK, tp))
    oda_p = _attn_prompt(kb, qt, vt, bias_p, lam, da_norm[l], t_att)
    src_p = (xp, ohg_p, oda_p)
    mods_p = (ga1, sh2, sc2, ga2)
    k_prompt = kf.reshape(1, bp, tp, DA_HEADS, 2 * DA_QKDIM)
    v_prompt = vf.reshape(1, bp, tp, DA_HEADS, DA_VDIM)

    ns = bs * ts
    xs_ = x_sample.reshape(ns, d)
    sh1, sc1, ga1, sh2, sc2, ga2 = mod_s
    zh, qs, kf, vf, kb, vb = _inproj(xs_, g_mix, sc1, sh1, w_in_b, ns, False)
    ohg_s, ss_new = _hgrn(zh, state_hgrn[l], hg_lb_logits, hg_norm[l], bs, ts, ts)
    oda_s = _attn_step(qs, cache_k[l], cache_v[l], kb, vb, bias_sp, bias_sn, lam, da_norm[l], bs, ts)
    assert ns == POST_TILE
    mods = tuple(jnp.stack([jnp.broadcast_to(mp, (POST_TILE, d)), ms_])
                 for mp, ms_ in zip(mods_p, (ga1, sh2, sc2, ga2)))
    y_p, y_s = _moe_and_final(src_p, (xs_, ohg_s, oda_s), mods, moe_w, POST_TILE, MOE_BLOCK_ROWS)
    k_sample = kf.reshape(1, bs, ts, DA_HEADS, 2 * DA_QKDIM)
    v_sample = vf.reshape(1, bs, ts, DA_HEADS, DA_VDIM)

    return (y_p.reshape(bp, tp, d), y_s.reshape(bs, ts, d), k_prompt, v_prompt, sp_new[None],
            k_sample, v_sample, ss_new[None].astype(x_sample.dtype))
```

```python
import functools
import math

import numpy as np
import jax
import jax.numpy as jnp
from jax import lax
from jax.experimental import pallas as pl
from jax.experimental.pallas import tpu as pltpu

F32 = jnp.float32
BF16 = jnp.bfloat16
I32 = jnp.int32
U32 = jnp.uint32
HIGHEST = lax.Precision.HIGHEST

D_MODEL = 1024
CHUNK = 64
HG_HEADS = 4
HG_DIM = 128
HG_WIDTH = HG_HEADS * HG_DIM
DA_HEADS = 4
DA_VDIM = 128
DA_QKDIM = 64
DA_WIDTH = DA_HEADS * DA_VDIM
N_BUCKETS = 32
MAX_DIST = 128
N_EXPERTS = 64
TOP_K = 8
N_GROUPS = 8
GROUP_SIZE = N_EXPERTS // N_GROUPS
TOP_GROUPS = 4
D_EXPERT = 256
ROUTE_SCALE = 2.5
EPS = 1e-6
LAM_INIT = 0.8 - 0.6 * math.exp(-0.3 * 0)

LOG2E = math.log2(math.e)
HI_MASK = np.uint32(0xFFFF0000)
NEG_BIG = -1e30
MASK_BUCKET = N_BUCKETS
V7X_VMEM_LIMIT = 48 * 1024 * 1024

ATT_TILE = 512
HGRN_CHUNK = 256
INPROJ_TILE = 512
POST_TILE = 256
MOE_BLOCK_ROWS = 1024
ROW_GROUP = 8
GROUPED_ROWS = -(-(POST_TILE * TOP_K + N_EXPERTS * (ROW_GROUP - 1)) // 256) * 256
CHUNK_SLOTS = -(-(GROUPED_ROWS // ROW_GROUP) // 128) * 128
CHUNK_EXPERT_SHIFT = 24
CHUNK_UNROLL = 4
WAIT_CHUNKS = 16


def _sigmoid(x):
    return 1.0 / (1.0 + jnp.exp(-x))


def _silu(x):
    return x * _sigmoid(x)


def _dot(a, b, **kw):
    return jnp.dot(a, b, preferred_element_type=F32, **kw)


def _dot_nt(a, b, **kw):
    return lax.dot_general(a, b, (((1,), (1,)), ((), ())), preferred_element_type=F32, **kw)


def _dot_tn(a, b, **kw):
    return lax.dot_general(a, b, (((0,), (0,)), ((), ())), preferred_element_type=F32, **kw)


def _pack_bf16_pair(lo, hi):
    lo_bits = lax.bitcast_convert_type(lo.astype(BF16).astype(F32), U32)
    hi_bits = lax.bitcast_convert_type(hi.astype(BF16).astype(F32), U32)
    return (lo_bits >> 16) | (hi_bits & HI_MASK)


def _pack_exact_bf16_pair(lo, hi):
    return (lax.bitcast_convert_type(lo, U32) >> 16) | (lax.bitcast_convert_type(hi, U32) & HI_MASK)


def _unpack_bf16_pair(w):
    lo = lax.bitcast_convert_type(w << 16, F32)
    hi = lax.bitcast_convert_type(w & HI_MASK, F32)
    return lo, hi


def _params(sem, vmem=V7X_VMEM_LIMIT, flags=None):
    return pltpu.CompilerParams(dimension_semantics=sem, vmem_limit_bytes=vmem, flags=flags)


def _adaln_kernel(c_ref, w_ref, b_ref, o_ref):
    s = _silu(c_ref[...])
    o_ref[...] = _dot(s, w_ref[...], precision=HIGHEST) + b_ref[...]


def _adaln(c_all, w_ada, b_ada):
    rows, d = c_all.shape
    cols = w_ada.shape[1]
    blk = 1024
    return pl.pallas_call(
        _adaln_kernel,
        grid=(cols // blk,),
        in_specs=[
            pl.BlockSpec((rows, d), lambda j: (0, 0)),
            pl.BlockSpec((d, blk), lambda j: (0, j)),
            pl.BlockSpec((1, blk), lambda j: (0, j)),
        ],
        out_specs=pl.BlockSpec((rows, blk), lambda j: (0, j)),
        out_shape=jax.ShapeDtypeStruct((rows, cols), F32),
        compiler_params=_params(("parallel",)),
        name="adaln",
    )(c_all, w_ada, b_ada.reshape(1, cols))


def _lam_kernel(l_ref, o_ref):
    l = l_ref[...].astype(F32)
    a = jnp.sum(l[0:1] * l[1:2], axis=-1, keepdims=True)
    b = jnp.sum(l[2:3] * l[3:4], axis=-1, keepdims=True)
    lam = jnp.exp(a) - jnp.exp(b) + LAM_INIT
    o_ref[...] = jnp.broadcast_to(lam, o_ref.shape)


def _lam(da_lambda_l):
    return pl.pallas_call(
        _lam_kernel,
        out_shape=jax.ShapeDtypeStruct((8, 128), F32),
        name="lam",
    )(da_lambda_l)


def _rel_bucket(rel):
    nb = N_BUCKETS // 2
    max_exact = nb // 2
    side = jnp.where(rel > 0, nb, 0)
    n = jnp.abs(rel)
    large = max_exact + (jnp.log(jnp.maximum(n, 1).astype(F32) / max_exact)
                         / math.log(MAX_DIST / max_exact) * (nb - max_exact)).astype(I32)
    large = jnp.minimum(large, nb - 1)
    return side + jnp.where(n < max_exact, n, large)


def _bias_kernel(tab_ref, idx_ref, o_ref, *, shift_bucket):
    h = pl.program_id(0)
    idx = idx_ref[...]
    shift = tab_ref[shift_bucket, h] if shift_bucket is not None else 0.0
    acc = jnp.zeros(idx.shape, F32)
    for j in range(N_BUCKETS):
        acc = jnp.where(idx == j, (tab_ref[j, h] - shift) * LOG2E, acc)
    o_ref[...] = jnp.where(idx == MASK_BUCKET, NEG_BIG, acc)


def _bias_tiles(table, idx, *, shift_bucket):
    k, r, c = idx.shape
    return pl.pallas_call(
        functools.partial(_bias_kernel, shift_bucket=shift_bucket),
        grid=(DA_HEADS, k),
        in_specs=[
            pl.BlockSpec(memory_space=pltpu.SMEM),
            pl.BlockSpec((None, r, c), lambda h, d: (d, 0, 0)),
        ],
        out_specs=pl.BlockSpec((None, None, r, c), lambda h, d: (h, d, 0, 0)),
        out_shape=jax.ShapeDtypeStruct((DA_HEADS, k, r, c), F32),
        compiler_params=_params(("parallel", "parallel")),
        name="rel_bias",
    )(table, idx)


def _inproj_kernel(x_ref, g_ref, sc_ref, sh_ref, w_ref,
                   zh_ref, q_ref, k_ref, v_ref, kb_ref, vb_ref, *, transposed):
    x = x_ref[...]
    ms = jnp.mean(x * x, axis=-1, keepdims=True)
    h = x * lax.rsqrt(ms + EPS) * g_ref[...]
    h = h * (1.0 + sc_ref[...]) + sh_ref[...]
    hb = h.astype(BF16)
    c0 = 4 * HG_WIDTH
    zh_ref[...] = _dot(hb, w_ref[:, 0:c0])
    zq = _dot(hb, w_ref[:, c0:c0 + DA_WIDTH]) * (DA_QKDIM ** -0.5 * LOG2E)
    zk = _dot(hb, w_ref[:, c0 + DA_WIDTH:c0 + 2 * DA_WIDTH])
    for hd in range(DA_HEADS):
        k_ref[:, hd, :] = zk[:, hd * DA_VDIM:(hd + 1) * DA_VDIM]
    kb_ref[...] = zk.astype(BF16)
    zv = _dot(hb, w_ref[:, c0 + 2 * DA_WIDTH:c0 + 3 * DA_WIDTH])
    for hd in range(DA_HEADS):
        v_ref[:, hd, :] = zv[:, hd * DA_VDIM:(hd + 1) * DA_VDIM]
    if transposed:
        q_ref[...] = zq.T.astype(BF16)
        vb_ref[...] = zv.T.astype(BF16).reshape(vb_ref.shape)
    else:
        q_ref[...] = zq.astype(BF16)
        vb_ref[...] = zv.astype(BF16)


def _mod_spec(mod, tm):
    if mod.shape[0] == 1:
        return pl.BlockSpec((1, mod.shape[1]), lambda i: (0, 0))
    return pl.BlockSpec((tm, mod.shape[1]), lambda i: (i, 0))


def _inproj(x, g, sc, sh, w_in_b, tm, transposed):
    n, d = x.shape
    cols = w_in_b.shape[1]
    row = lambda i: (i, 0)
    if transposed:
        q_spec = pl.BlockSpec((DA_WIDTH, tm), lambda i: (0, i))
        q_shape = jax.ShapeDtypeStruct((DA_WIDTH, n), BF16)
        vb_spec = pl.BlockSpec((DA_HEADS, None, DA_VDIM, tm), lambda i: (0, i, 0, 0))
        vb_shape = jax.ShapeDtypeStruct((DA_HEADS, n // tm, DA_VDIM, tm), BF16)
    else:
        q_spec = vb_spec = pl.BlockSpec((tm, DA_WIDTH), row)
        q_shape = vb_shape = jax.ShapeDtypeStruct((n, DA_WIDTH), BF16)
    return pl.pallas_call(
        functools.partial(_inproj_kernel, transposed=transposed),
        grid=(n // tm,),
        in_specs=[
            pl.BlockSpec((tm, d), row),
            pl.BlockSpec((1, d), lambda i: (0, 0)),
            _mod_spec(sc, tm),
            _mod_spec(sh, tm),
            pl.BlockSpec((d, cols), lambda i: (0, 0)),
        ],
        out_specs=[
            pl.BlockSpec((tm, 4 * HG_WIDTH), row),
            q_spec,
            pl.BlockSpec((tm, DA_HEADS, DA_VDIM), lambda i: (i, 0, 0)),
            pl.BlockSpec((tm, DA_HEADS, DA_VDIM), lambda i: (i, 0, 0)),
            pl.BlockSpec((tm, DA_WIDTH), row),
            vb_spec,
        ],
        out_shape=[
            jax.ShapeDtypeStruct((n, 4 * HG_WIDTH), F32),
            q_shape,
            jax.ShapeDtypeStruct((n, DA_HEADS, DA_VDIM), F32),
            jax.ShapeDtypeStruct((n, DA_HEADS, DA_VDIM), F32),
            jax.ShapeDtypeStruct((n, DA_WIDTH), BF16),
            vb_shape,
        ],
        compiler_params=_params(("parallel",)),
        name="inproj",
    )(x, g, sc, sh, w_in_b)


def _hgrn_consts(c):
    levels = int(round(math.log2(c)))
    assert 1 << levels == c and levels >= 3
    t = np.arange(c)[:, None]
    r = np.arange(c)[None, :]
    tri = (r <= t).astype(np.float32)
    x = np.maximum(t ^ r, 1)
    lv = np.where(t == r, -1, np.where(t > r, np.floor(np.log2(x)).astype(np.int64), -2))
    return jnp.asarray(tri, dtype=BF16), jnp.asarray(lv, dtype=I32), levels


def _hgrn_kernel(zh_ref, s0_ref, lbl_ref, gain_ref, mall_ref, lv_ref,
                 o_ref, sout_ref, st_ref, b_ref, *, c, levels):
    ci = pl.program_id(1)

    @pl.when(ci == 0)
    def _():
        for h in range(HG_HEADS):
            st_ref[h] = s0_ref[h].astype(F32).T

    lbl = lbl_ref[...].astype(F32)
    mx = jnp.maximum(lbl[0:1], lbl[1:2])
    e0 = jnp.exp(lbl[0:1] - mx)
    e1 = jnp.exp(lbl[1:2] - mx)
    lb = e0 / (e0 + e1)

    xq = zh_ref[:, 0:HG_WIDTH]
    xf = zh_ref[:, HG_WIDTH:2 * HG_WIDTH]
    q = _silu(xq)
    y = lb + (1.0 - lb) * _sigmoid(xf)
    logf = jnp.log(y)
    kk = 1.0 - y

    l1 = logf.astype(BF16)
    r1 = logf - l1.astype(F32)
    l2 = r1.astype(BF16)
    l3 = (r1 - l2.astype(F32)).astype(BF16)
    tri = mall_ref[...]
    b = _dot(tri, l1) + _dot(tri, l2) + _dot(tri, l3)
    b_ref[...] = b
    trow = lax.broadcasted_iota(I32, b.shape, 0)

    def level_exponent(l):
        m = 1 << l
        later = (trow & m) != 0
        if l == 0:
            return jnp.where(later, logf, 0.0)
        if l == 1:
            below = pltpu.roll(logf, 1, 0)
            above = pltpu.roll(logf, c - 1, 0)
            low = (trow & 1) != 0
            return jnp.where(later, jnp.where(low, logf + below, logf), jnp.where(low, 0.0, above))
        mid = jnp.concatenate(
            [jnp.broadcast_to(b_ref[k * 2 * m + m - 1:k * 2 * m + m, :], (2 * m, b.shape[1]))
             for k in range(c // (2 * m))], axis=0)
        return jnp.where(later, b - mid, mid - b)

    factors = [jnp.exp(level_exponent(l)) for l in range(levels)]
    lv = lv_ref[...]
    gain = gain_ref[...].astype(F32)
    for h in range(HG_HEADS):
        sl = slice(h * HG_DIM, (h + 1) * HG_DIM)
        qh = q[:, sl]
        kh = kk[:, sl]
        ih = zh_ref[:, 2 * HG_WIDTH + h * HG_DIM:2 * HG_WIDTH + (h + 1) * HG_DIM]
        gh = zh_ref[:, 3 * HG_WIDTH + h * HG_DIM:3 * HG_WIDTH + (h + 1) * HG_DIM]
        bh = b[:, sl]
        ihb = ih.astype(BF16)
        a = jnp.where(lv == -1, _dot_nt(qh.astype(BF16), kh.astype(BF16)), 0.0)
        for l in range(levels):
            f = factors[l][:, sl]
            p = _dot_nt((qh * f).astype(BF16), (kh * f).astype(BF16))
            a = jnp.where(lv == l, p, a)
        st = st_ref[h]
        o = _dot(a.astype(BF16), ihb) + _dot_nt((qh * jnp.exp(bh)).astype(BF16), st.astype(BF16))
        bl = bh[c - 1:c, :]
        kd = (kh * jnp.exp(bl - bh)).astype(BF16)
        st_ref[h] = st * jnp.exp(bl) + _dot_tn(ihb, kd)
        ms = jnp.mean(o * o, axis=-1, keepdims=True)
        on = o * lax.rsqrt(ms + EPS) * gain
        o_ref[:, sl] = (on * _silu(gh)).astype(o_ref.dtype)

    @pl.when(ci == pl.num_programs(1) - 1)
    def _():
        for h in range(HG_HEADS):
            sout_ref[h] = st_ref[h].T.astype(sout_ref.dtype)


def _hgrn(zh, s0, lb_logits, gain, batch, seq, c):
    mall, lv, levels = _hgrn_consts(c)
    nc = seq // c
    return pl.pallas_call(
        functools.partial(_hgrn_kernel, c=c, levels=levels),
        grid=(batch, nc),
        in_specs=[
            pl.BlockSpec((c, 4 * HG_WIDTH), lambda b, i: (b * nc + i, 0)),
            pl.BlockSpec((None, HG_HEADS, HG_DIM, HG_DIM), lambda b, i: (b, 0, 0, 0)),
            pl.BlockSpec(lb_logits.shape, lambda b, i: (0, 0)),
            pl.BlockSpec((1, HG_DIM), lambda b, i: (0, 0)),
            pl.BlockSpec(mall.shape, lambda b, i: (0, 0)),
            pl.BlockSpec(lv.shape, lambda b, i: (0, 0)),
        ],
        out_specs=[
            pl.BlockSpec((c, HG_WIDTH), lambda b, i: (b * nc + i, 0)),
            pl.BlockSpec((None, HG_HEADS, HG_DIM, HG_DIM), lambda b, i: (b, 0, 0, 0)),
        ],
        out_shape=[
            jax.ShapeDtypeStruct((batch * seq, HG_WIDTH), BF16),
            jax.ShapeDtypeStruct((batch, HG_HEADS, HG_DIM, HG_DIM), F32),
        ],
        scratch_shapes=[pltpu.VMEM((HG_HEADS, HG_DIM, HG_DIM), F32),
                        pltpu.VMEM((c, HG_WIDTH), F32)],
        compiler_params=_params(("parallel", "arbitrary")),
        name="hgrn2",
    )(zh, s0, lb_logits, gain.reshape(1, HG_DIM), mall, lv)


def _attn_kernel(k_ref, qt_ref, vt_ref, bias_ref, lam_ref, gain_ref,
                 o_ref, qz_ref, m_ref, l_ref, acc_ref, s_ref, smax_ref, *, t):
    i = pl.program_id(1)
    qt = qt_ref[...]
    row = lax.broadcasted_iota(I32, qt.shape, 0)
    zero = jnp.zeros_like(qt)
    qz_ref[:, 0:t] = jnp.where(row < DA_QKDIM, qt, zero)
    qz_ref[:, t:2 * t] = jnp.where(row >= DA_QKDIM, qt, zero)
    m_ref[...] = jnp.full(m_ref.shape, NEG_BIG, F32)
    l_ref[...] = jnp.zeros(l_ref.shape, F32)
    acc_ref[...] = jnp.zeros(acc_ref.shape, F32)

    def scores(j, buf):
        kt = k_ref[pl.ds(pl.multiple_of(j * t, t), t), :]
        s = _dot(kt, qz_ref[...])
        s_ref[buf] = s
        smax_ref[buf] = jnp.max(s, axis=0, keepdims=True)

    def consume(j, buf, bias_idx):
        s = s_ref[buf]
        if bias_idx is not None:
            b = bias_ref[bias_idx]
            s = jnp.concatenate([s[:, 0:t] + b, s[:, t:2 * t] + b], axis=1)
            s_max = jnp.max(s, axis=0, keepdims=True)
        else:
            s_max = smax_ref[buf]
        m_prev = m_ref[...]
        m_new = jnp.maximum(m_prev, s_max)
        alpha = jnp.exp2(m_prev - m_new)
        pr = jnp.exp2(s - m_new)
        l_ref[...] = alpha * l_ref[...] + jnp.sum(pr, axis=0, keepdims=True)
        acc_ref[...] = alpha * acc_ref[...] + _dot(vt_ref[j], pr.astype(BF16))
        m_ref[...] = m_new

    n_far = jnp.maximum(i - 1, 0)

    @pl.when(n_far > 0)
    def _():
        scores(0, 0)

    def far_tiles(j, count):
        for u in range(count):
            nxt = j + u + 1
            if u == count - 1:
                nxt = jnp.minimum(nxt, n_far - 1)
            scores(nxt, (u + 1) % 2)
            consume(j + u, u % 2, None)

    def far_quad(p, carry):
        far_tiles(4 * p, 4)
        return carry

    def far_pair(p, carry):
        far_tiles(4 * quads + 2 * p, 2)
        return carry

    quads = n_far // 4
    lax.fori_loop(0, quads, far_quad, 0)
    lax.fori_loop(0, (n_far - 4 * quads) // 2, far_pair, 0)

    @pl.when(lax.rem(n_far, 2) == 1)
    def _():
        consume(n_far - 1, 0, None)

    @pl.when(i >= 1)
    def _():
        scores(i - 1, 0)
        scores(i, 1)
        consume(i - 1, 0, 1)
        consume(i, 1, 0)

    @pl.when(i == 0)
    def _():
        scores(i, 1)
        consume(i, 1, 0)

    lam = lam_ref[0:1, 0:1]
    l = l_ref[...]
    acc = acc_ref[...]
    o = acc[:, 0:t] / l[:, 0:t] - lam * (acc[:, t:2 * t] / l[:, t:2 * t])
    ms = jnp.mean(o * o, axis=0, keepdims=True)
    on = o * lax.rsqrt(ms + EPS) * gain_ref[...].astype(F32) * (1.0 - LAM_INIT)
    o_ref[...] = on.T.astype(o_ref.dtype)


def _attn_prompt(kb, qt, vt, bias, lam, gain, t):
    n = kb.shape[0]
    nt = n // t
    return pl.pallas_call(
        functools.partial(_attn_kernel, t=t),
        grid=(DA_HEADS, nt),
        in_specs=[
            pl.BlockSpec((n, DA_VDIM), lambda h, i: (0, h)),
            pl.BlockSpec((DA_VDIM, t), lambda h, i: (h, i)),
            pl.BlockSpec((None, nt, DA_VDIM, t), lambda h, i: (h, 0, 0, 0)),
            pl.BlockSpec((None, 2, t, t), lambda h, i: (h, 0, 0, 0)),
            pl.BlockSpec((8, 128), lambda h, i: (0, 0)),
            pl.BlockSpec((DA_VDIM, 1), lambda h, i: (0, 0)),
        ],
        out_specs=pl.BlockSpec((t, DA_VDIM), lambda h, i: (i, h)),
        out_shape=jax.ShapeDtypeStruct((n, DA_WIDTH), BF16),
        scratch_shapes=[
            pltpu.VMEM((DA_VDIM, 2 * t), BF16),
            pltpu.VMEM((1, 2 * t), F32),
            pltpu.VMEM((1, 2 * t), F32),
            pltpu.VMEM((DA_VDIM, 2 * t), F32),
            pltpu.VMEM((2, t, 2 * t), F32),
            pltpu.VMEM((2, 1, 2 * t), F32),
        ],
        compiler_params=_params(("parallel", "parallel")),
        name="diff_attn_prompt",
    )(kb, qt, vt, bias, lam, gain.reshape(DA_VDIM, 1))


def _attn_step_kernel(q_ref, kp_ref, vp_ref, kn_ref, vn_ref, bp_ref, bn_ref, lam_ref, gain_ref,
                      o_ref, *, tq, pad):
    lam = lam_ref[0:1, 0:1]
    gain = gain_ref[...].astype(F32)
    zpad = jnp.zeros((pad - tq, DA_VDIM), BF16)
    for h in range(DA_HEADS):
        hs = slice(h * DA_VDIM, (h + 1) * DA_VDIM)
        q = q_ref[:, hs]
        lane = lax.broadcasted_iota(I32, q.shape, 1)
        zero = jnp.zeros_like(q)
        qz = jnp.concatenate([jnp.where(lane < DA_QKDIM, q, zero),
                              jnp.where(lane >= DA_QKDIM, q, zero)], axis=0)
        kp = kp_ref[:, h, :].astype(BF16)
        vp = vp_ref[:, h, :].astype(BF16)
        kn = jnp.concatenate([kn_ref[:, hs], zpad], axis=0)
        vn = jnp.concatenate([vn_ref[:, hs], zpad], axis=0)
        bp = bp_ref[h, 0]
        bn = bn_ref[h, 0]
        sp = _dot_nt(qz, kp) + jnp.concatenate([bp, bp], axis=0)
        sn = _dot_nt(qz, kn) + jnp.concatenate([bn, bn], axis=0)
        m = jnp.maximum(jnp.max(sp, axis=-1, keepdims=True), jnp.max(sn, axis=-1, keepdims=True))
        pp = jnp.exp2(sp - m)
        pn = jnp.exp2(sn - m)
        l = jnp.sum(pp, axis=-1, keepdims=True) + jnp.sum(pn, axis=-1, keepdims=True)
        acc = _dot(pp.astype(BF16), vp) + _dot(pn.astype(BF16), vn)
        on = acc / l
        o = on[0:tq] - lam * on[tq:2 * tq]
        ms = jnp.mean(o * o, axis=-1, keepdims=True)
        o = o * lax.rsqrt(ms + EPS) * gain * (1.0 - LAM_INIT)
        o_ref[:, hs] = o.astype(o_ref.dtype)


def _attn_step(qs, cache_k_l, cache_v_l, kb, vb, bias_p, bias_n, lam, gain, batch, tq):
    past = cache_k_l.shape[1]
    pad = bias_n.shape[-1]
    cache_spec = pl.BlockSpec((None, past, DA_HEADS, DA_VDIM), lambda b: (b, 0, 0, 0))
    row = pl.BlockSpec((tq, DA_WIDTH), lambda b: (b, 0))
    return pl.pallas_call(
        functools.partial(_attn_step_kernel, tq=tq, pad=pad),
        grid=(batch,),
        in_specs=[
            row, cache_spec, cache_spec, row, row,
            pl.BlockSpec(bias_p.shape, lambda b: (0, 0, 0, 0)),
            pl.BlockSpec(bias_n.shape, lambda b: (0, 0, 0, 0)),
            pl.BlockSpec((8, 128), lambda b: (0, 0)),
            pl.BlockSpec((1, DA_VDIM), lambda b: (0, 0)),
        ],
        out_specs=row,
        out_shape=jax.ShapeDtypeStruct((batch * tq, DA_WIDTH), BF16),
        compiler_params=_params(("parallel",)),
        name="diff_attn_step",
    )(qs, cache_k_l, cache_v_l, kb, vb, bias_p, bias_n, lam, gain.reshape(1, DA_VDIM))


def _post_kernel(xa_ref, ohga_ref, odaa_ref, xb_ref, ohgb_ref, odab_ref,
                 wout_ref, ga1_ref, g_ref, sc_ref, sh_ref, ga2_ref,
                 wsgu_ref, wsd_ref, wrt_ref, rb_ref, tri_ref, ltri_ref,
                 xs_ref, h2_ref, pos_ref, wl_ref, chunk_ref, nch_ref, cnt_ref, carry_ref, *, tm, nta):
    i = pl.program_id(0)

    @pl.when(i == 0)
    def _():
        carry_ref[...] = jnp.zeros(carry_ref.shape, F32)

    second = i >= nta
    x = jnp.where(second, xb_ref[...], xa_ref[...])
    ohg = jnp.where(second, ohgb_ref[...], ohga_ref[...])
    oda = jnp.where(second, odab_ref[...], odaa_ref[...])
    mix = _dot(ohg, wout_ref[0:HG_WIDTH, :]) + _dot(oda, wout_ref[HG_WIDTH:, :])
    x1 = x + ga1_ref[...] * mix
    ms = jnp.mean(x1 * x1, axis=-1, keepdims=True)
    h2 = x1 * lax.rsqrt(ms + EPS) * g_ref[...]
    h2 = h2 * (1.0 + sc_ref[...]) + sh_ref[...]
    h2b = h2.astype(BF16)
    h2_ref[...] = h2b
    gu = _dot(h2b, wsgu_ref[...])
    act = (_silu(gu[:, 0:D_EXPERT]) * gu[:, D_EXPERT:]).astype(BF16)
    xs_ref[...] = x1 + ga2_ref[...] * _dot(act, wsd_ref[...])

    logits = _dot_nt(wrt_ref[...], h2, precision=HIGHEST)
    score = _sigmoid(logits)
    sel = score + rb_ref[...]
    sub = lax.broadcasted_iota(I32, (GROUP_SIZE, tm), 0)
    gscore = []
    for g in range(N_GROUPS):
        v = sel[g * GROUP_SIZE:(g + 1) * GROUP_SIZE, :]
        m1 = jnp.max(v, axis=0, keepdims=True)
        i1 = jnp.min(jnp.where(v == m1, sub, GROUP_SIZE), axis=0, keepdims=True)
        m2 = jnp.max(jnp.where(sub == i1, -jnp.inf, v), axis=0, keepdims=True)
        gscore.append(m1 + m2)
    gsel = []
    for g in range(N_GROUPS):
        ahead = jnp.zeros((1, tm), F32)
        for g2 in range(N_GROUPS):
            if g2 == g:
                continue
            tie = 1.0 if g2 < g else 0.0
            ahead = ahead + jnp.where(gscore[g2] > gscore[g], 1.0,
                                      jnp.where(gscore[g2] == gscore[g], tie, 0.0))
        gsel.append(ahead < TOP_GROUPS)
    selm = jnp.concatenate(
        [jnp.where(gsel[g], sel[g * GROUP_SIZE:(g + 1) * GROUP_SIZE, :], -jnp.inf)
         for g in range(N_GROUPS)], axis=0)
    eio = lax.broadcasted_iota(I32, (N_EXPERTS, tm), 0)
    ahead = jnp.zeros((N_EXPERTS, tm), F32)
    for e2 in range(N_EXPERTS):
        row = selm[e2:e2 + 1, :]
        tie = jnp.where(eio > e2, 1.0, 0.0)
        ahead = ahead + jnp.where(row > selm, 1.0, jnp.where(row == selm, tie, 0.0))
    chosen = jnp.where(selm > -jnp.inf, jnp.where(ahead < TOP_K, 1.0, 0.0), 0.0)
    w = chosen * score
    wn = w / jnp.sum(w, axis=0, keepdims=True) * ROUTE_SCALE

    chb = chosen.astype(BF16)
    before = _dot(chb, tri_ref[...])
    tot = _dot(chb, jnp.ones((tm, 128), BF16))
    run = jnp.floor((tot + (ROW_GROUP - 1)) * (1.0 / ROW_GROUP)) * ROW_GROUP
    tile_base = _dot(ltri_ref[...], run.astype(BF16))
    carry = carry_ref[...]
    carry_ref[...] = carry + run
    cnt_ref[...] = carry + run
    pos = jnp.concatenate([tile_base] * (tm // 128), axis=1) + before

    widen = lambda v: jnp.concatenate([v] * (CHUNK_SLOTS // 128), axis=1)
    crow = lax.broadcasted_iota(I32, (N_EXPERTS, CHUNK_SLOTS), 1).astype(F32) * ROW_GROUP
    erow = lax.broadcasted_iota(I32, (N_EXPERTS, CHUNK_SLOTS), 0).astype(F32)
    owner = jnp.sum(jnp.where(widen(tile_base + run) <= crow, 1.0, 0.0), axis=0, keepdims=True)
    region_row = jnp.sum(jnp.where(owner == erow, widen(carry - tile_base), 0.0),
                         axis=0, keepdims=True) + crow[0:1]
    region_slab = (region_row * (1.0 / ROW_GROUP)).astype(I32)
    chunk_ref[...] = owner.astype(I32) * (1 << CHUNK_EXPERT_SHIFT) + region_slab
    nch_ref[...] = jnp.sum(run * (1.0 / ROW_GROUP), axis=0, keepdims=True).astype(I32)

    for r in range(TOP_K):
        pick = jnp.where(ahead == r, chosen, 0.0)
        pos_ref[r:r + 1, :] = jnp.sum(pick * pos, axis=0, keepdims=True).astype(I32)
        wl_ref[r:r + 1, :] = jnp.sum(pick * wn, axis=0, keepdims=True)


def _post(src_a, src_b, w_out_b, ga1, g, sc, sh, ga2, wsgu_b, wsd_b, wr_t, rb, tm):
    (xa, ohga, odaa), (xb, ohgb, odab) = src_a, src_b
    d = xa.shape[1]
    nta, ntb = xa.shape[0] // tm, xb.shape[0] // tm
    nt = nta + ntb
    n = nt * tm
    tri = jnp.asarray(np.triu(np.ones((tm, tm), np.float32), k=1), dtype=BF16)
    ltri = jnp.asarray(np.tril(np.ones((N_EXPERTS, N_EXPERTS), np.float32), k=-1), dtype=BF16)
    row = lambda i: (i, 0)
    row_a = lambda i: (jnp.minimum(i, nta - 1), 0)
    row_b = lambda i: (jnp.maximum(i - nta, 0), 0)
    col = lambda i: (0, i)
    full = lambda i: (0, 0)
    mod = pl.BlockSpec((None, tm, d), lambda i: (jnp.minimum(i // nta, 1), 0, 0))
    return pl.pallas_call(
        functools.partial(_post_kernel, tm=tm, nta=nta),
        grid=(nt,),
        in_specs=[
            pl.BlockSpec((tm, d), row_a),
            pl.BlockSpec((tm, HG_WIDTH), row_a),
            pl.BlockSpec((tm, DA_WIDTH), row_a),
            pl.BlockSpec((tm, d), row_b),
            pl.BlockSpec((tm, HG_WIDTH), row_b),
            pl.BlockSpec((tm, DA_WIDTH), row_b),
            pl.BlockSpec(w_out_b.shape, full),
            mod,
            pl.BlockSpec((1, d), full),
            mod,
            mod,
            mod,
            pl.BlockSpec(wsgu_b.shape, full),
            pl.BlockSpec(wsd_b.shape, full),
            pl.BlockSpec(wr_t.shape, full),
            pl.BlockSpec((N_EXPERTS, 1), full),
            pl.BlockSpec((tm, tm), full),
            pl.BlockSpec((N_EXPERTS, N_EXPERTS), full),
        ],
        out_specs=[
            pl.BlockSpec((tm, d), row),
            pl.BlockSpec((tm, d), row),
            pl.BlockSpec((TOP_K, tm), col),
            pl.BlockSpec((TOP_K, tm), col),
            pl.BlockSpec((None, 1, CHUNK_SLOTS), lambda i: (i, 0, 0)),
            pl.BlockSpec((None, 1, 128), lambda i: (i, 0, 0)),
            pl.BlockSpec((N_EXPERTS, 128), full),
        ],
        out_shape=[
            jax.ShapeDtypeStruct((n, d), F32),
            jax.ShapeDtypeStruct((n, d), BF16),
            jax.ShapeDtypeStruct((TOP_K, n), I32),
            jax.ShapeDtypeStruct((TOP_K, n), F32),
            jax.ShapeDtypeStruct((nt, 1, CHUNK_SLOTS), I32),
            jax.ShapeDtypeStruct((nt, 1, 128), I32),
            jax.ShapeDtypeStruct((N_EXPERTS, 128), F32),
        ],
        scratch_shapes=[pltpu.VMEM((N_EXPERTS, 128), F32)],
        compiler_params=_params(("arbitrary",)),
        name="post_mix_router",
    )(xa, ohga, odaa, xb, ohgb, odab, w_out_b, ga1, g, sc, sh, ga2, wsgu_b, wsd_b, wr_t,
      rb.reshape(N_EXPERTS, 1), tri, ltri)


def _start_chunks(tile, chunk_ref, nch_ref, pslab_ref, make_copy):
    n = nch_ref[tile]

    def start(c):
        word = chunk_ref[tile * CHUNK_SLOTS + c]
        expert = lax.shift_right_logical(word, CHUNK_EXPERT_SHIFT)
        region_slab = word & ((1 << CHUNK_EXPERT_SHIFT) - 1)
        make_copy(c, pslab_ref[expert] + region_slab, 1).start()

    def group(g, carry):
        for u in range(CHUNK_UNROLL):
            start(g * CHUNK_UNROLL + u)
        return carry

    def single(c, carry):
        start(c)
        return carry

    groups = n // CHUNK_UNROLL
    lax.fori_loop(0, groups, group, 0)
    lax.fori_loop(groups * CHUNK_UNROLL, n, single, 0)


def _wait_chunks(tile, nch_ref, make_copy):
    n = nch_ref[tile]
    many = n // WAIT_CHUNKS

    def wait_many(j, carry):
        make_copy(0, 0, WAIT_CHUNKS).wait()
        return carry

    def wait_one(j, carry):
        make_copy(0, 0, 1).wait()
        return carry

    lax.fori_loop(0, many, wait_many, 0)
    lax.fori_loop(many * WAIT_CHUNKS, n, wait_one, 0)


def _dispatch_kernel(chunk_ref, nch_ref, pstart_ref, pend_ref, pos_ref, h2_ref, xs_hbm,
                     cbuf_ref, zero_ref, zsem, sem, *, tm, bm):
    i = pl.program_id(0)
    nt = pl.num_programs(0)
    bslabs = bm // ROW_GROUP
    nblk = xs_hbm.shape[0] // bslabs
    dh = cbuf_ref.shape[-1]

    def zero_copy(e):
        return pltpu.make_async_copy(zero_ref, xs_hbm.at[pl.ds(pend_ref[e] - bslabs, bslabs)], zsem)

    def tail_copy(b):
        return pltpu.make_async_copy(zero_ref, xs_hbm.at[pl.ds(b * bslabs, bslabs)], zsem)

    @pl.when(i == 0)
    def _():
        zero_ref[...] = jnp.zeros(zero_ref.shape, zero_ref.dtype)
        first_unused = pend_ref[N_EXPERTS - 1] // bslabs

        def zissue(e, carry):
            @pl.when(pend_ref[e] > pstart_ref[e])
            def _():
                zero_copy(e).start()
            return carry

        def zwait(e, carry):
            @pl.when(pend_ref[e] > pstart_ref[e])
            def _():
                zero_copy(e).wait()
            return carry

        def tissue(b, carry):
            tail_copy(b).start()
            return carry

        def twait(b, carry):
            tail_copy(b).wait()
            return carry

        lax.fori_loop(0, N_EXPERTS, zissue, 0)
        lax.fori_loop(first_unused, nblk, tissue, 0)
        lax.fori_loop(0, N_EXPERTS, zwait, 0)
        lax.fori_loop(first_unused, nblk, twait, 0)

    pos = pos_ref[...]
    piota = lax.broadcasted_iota(I32, (GROUPED_ROWS, tm), 0).astype(jnp.int16)
    pos16 = pos.astype(jnp.int16)
    one = jnp.ones((GROUPED_ROWS, tm), BF16)
    perm = jnp.zeros((GROUPED_ROWS, tm), BF16)
    for r in range(TOP_K):
        perm = jnp.where(piota == pos16[r:r + 1, :], one, perm)
    cur = lax.rem(i, 2)
    grouped = _pack_exact_bf16_pair(_dot(perm, h2_ref[:, 0:dh]), _dot(perm, h2_ref[:, dh:]))
    cbuf_ref[cur] = grouped.reshape(cbuf_ref.shape[1:])

    def make_copy(buf):
        def build(tile_slab, buffer_slab, slabs):
            return pltpu.make_async_copy(cbuf_ref.at[buf, pl.ds(tile_slab, slabs)],
                                         xs_hbm.at[pl.ds(buffer_slab, slabs)], sem.at[buf])
        return build

    _start_chunks(i, chunk_ref, nch_ref, pstart_ref, make_copy(cur))

    @pl.when(i > 0)
    def _():
        _wait_chunks(i - 1, nch_ref, make_copy(1 - cur))

    @pl.when(i == nt - 1)
    def _():
        _wait_chunks(i, nch_ref, make_copy(cur))


def _dispatch(chunks, nch, pstart, pend, pos, h2, nrows, tm, bm):
    n, d = h2.shape
    grid_spec = pltpu.PrefetchScalarGridSpec(
        num_scalar_prefetch=4,
        grid=(n // tm,),
        in_specs=[
            pl.BlockSpec((TOP_K, tm), lambda i, *_: (0, i)),
            pl.BlockSpec((tm, d), lambda i, *_: (i, 0)),
        ],
        out_specs=pl.BlockSpec(memory_space=pl.ANY),
        scratch_shapes=[
            pltpu.VMEM((2, GROUPED_ROWS // ROW_GROUP, ROW_GROUP, d // 2), U32),
            pltpu.VMEM((bm // ROW_GROUP, ROW_GROUP, d // 2), U32),
            pltpu.SemaphoreType.DMA(()),
            pltpu.SemaphoreType.DMA((2,)),
        ],
    )
    return pl.pallas_call(
        functools.partial(_dispatch_kernel, tm=tm, bm=bm),
        grid_spec=grid_spec,
        out_shape=jax.ShapeDtypeStruct((nrows // ROW_GROUP, ROW_GROUP, d // 2), U32),
        compiler_params=_params(("arbitrary",)),
        name="moe_dispatch",
    )(chunks, nch, pstart, pend, pos, h2)


def _experts_kernel(be_ref, nu_ref, valid_ref, first_ref, slot_ref, next_ref, x_ref, wgu_hbm, wd_hbm,
                    o_ref, wgu_f_ref, wd_f_ref, wgu_b_ref, wd_b_ref, sem, *, bm):
    i = pl.program_id(0)
    valid = valid_ref[i]
    sub = bm // 2

    def weight_copies(expert, slot):
        return (pltpu.make_async_copy(wgu_hbm.at[expert], wgu_f_ref.at[slot], sem.at[slot, 0]),
                pltpu.make_async_copy(wd_hbm.at[expert], wd_f_ref.at[slot], sem.at[slot, 1]))

    @pl.when(i == 0)
    def _():
        for c in weight_copies(be_ref[0], 0):
            c.start()

    @pl.when(first_ref[i] == 1)
    def _():
        slot = slot_ref[i]
        for c in weight_copies(be_ref[i], slot):
            c.wait()

        @pl.when(next_ref[i] >= 0)
        def _():
            for c in weight_copies(next_ref[i], 1 - slot):
                c.start()

        wgu_b_ref[...] = wgu_f_ref[slot].astype(BF16)
        wd_b_ref[...] = wd_f_ref[slot].astype(BF16)

    for r0 in (0, sub):
        rows = slice(r0, r0 + sub)

        @pl.when(valid > r0)
        def _():
            lo, hi = _unpack_bf16_pair(x_ref[rows, :])
            x = jnp.concatenate([lo.astype(BF16), hi.astype(BF16)], axis=1)
            gu = _dot(x, wgu_b_ref[...])
            act = (_silu(gu[:, 0:D_EXPERT]) * gu[:, D_EXPERT:]).astype(BF16)
            y = _dot(act, wd_b_ref[...])
            o_ref[rows, :] = _pack_bf16_pair(y[:, 0:D_MODEL // 2], y[:, D_MODEL // 2:])

        @pl.when(valid <= r0)
        def _():
            o_ref[rows, :] = jnp.zeros((sub, o_ref.shape[1]), o_ref.dtype)


def _experts(block_e, nused, valid, first, slot, next_e, xs, w_gate_up_l, w_down_l, bm):
    nrows, dh = xs.shape
    d = 2 * dh
    nblk = nrows // bm
    grid_spec = pltpu.PrefetchScalarGridSpec(
        num_scalar_prefetch=6,
        grid=(nblk,),
        in_specs=[
            pl.BlockSpec((bm, dh), lambda i, be, nu, *_: (jnp.minimum(i, nu[0] - 1), 0)),
            pl.BlockSpec(memory_space=pl.ANY),
            pl.BlockSpec(memory_space=pl.ANY),
        ],
        out_specs=pl.BlockSpec((bm, dh), lambda i, *_: (i, 0)),
        scratch_shapes=[
            pltpu.VMEM((2, d, 2 * D_EXPERT), w_gate_up_l.dtype),
            pltpu.VMEM((2, D_EXPERT, d), w_down_l.dtype),
            pltpu.VMEM((d, 2 * D_EXPERT), BF16),
            pltpu.VMEM((D_EXPERT, d), BF16),
            pltpu.SemaphoreType.DMA((2, 2)),
        ],
    )
    return pl.pallas_call(
        functools.partial(_experts_kernel, bm=bm),
        grid_spec=grid_spec,
        out_shape=jax.ShapeDtypeStruct((nrows, dh), U32),
        compiler_params=_params(("arbitrary",)),
        name="moe_experts",
    )(block_e, nused, valid, first, slot, next_e, xs, w_gate_up_l, w_down_l)


def _combine_kernel(chunk_ref, nch_ref, pstart_ref, pos_ref, wl_ref, xs_ref, ga2_ref, gf_ref,
                    yb_hbm, oa_ref, ob_ref, gbuf_ref, sem, *, tm, nta):
    i = pl.program_id(0)
    nt = pl.num_programs(0)
    cur = lax.rem(i, 2)
    refs = (chunk_ref, nch_ref, pstart_ref)

    def make_copy(buf):
        def build(tile_slab, buffer_slab, slabs):
            return pltpu.make_async_copy(yb_hbm.at[pl.ds(buffer_slab, slabs)],
                                         gbuf_ref.at[buf, pl.ds(tile_slab, slabs)], sem.at[buf])
        return build

    @pl.when(i == 0)
    def _():
        gbuf_ref[...] = jnp.zeros(gbuf_ref.shape, gbuf_ref.dtype)
        _start_chunks(0, *refs, make_copy(0))

    @pl.when(i + 1 < nt)
    def _():
        _start_chunks(i + 1, *refs, make_copy(1 - cur))

    _wait_chunks(i, nch_ref, make_copy(cur))

    lo, hi = _unpack_bf16_pair(gbuf_ref[cur].reshape(GROUPED_ROWS, gbuf_ref.shape[-1]))
    g = jnp.concatenate([lo.astype(BF16), hi.astype(BF16)], axis=1)
    pos16 = pos_ref[...].astype(jnp.int16)
    wl = wl_ref[...].astype(BF16)
    liota = lax.broadcasted_iota(I32, (tm, GROUPED_ROWS), 1).astype(jnp.int16)
    a = jnp.zeros((tm, GROUPED_ROWS), BF16)
    for r in range(TOP_K):
        a = jnp.where(liota == pos16[:, r:r + 1], jnp.broadcast_to(wl[:, r:r + 1], a.shape), a)
    routed = _dot(a, g)
    x2 = xs_ref[...] + ga2_ref[...] * routed
    ms = jnp.mean(x2 * x2, axis=-1, keepdims=True)
    y = x2 * lax.rsqrt(ms + EPS) * gf_ref[...]

    @pl.when(i < nta)
    def _():
        oa_ref[...] = y

    @pl.when(i >= nta)
    def _():
        ob_ref[...] = y


def _combine(chunks, nch, pstart, pos_t, wl_t, xs_base, ga2, gfin, yb, tm, nta):
    n, d = xs_base.shape
    ntb = n // tm - nta
    ga2_spec = pl.BlockSpec((None, tm, d), lambda i, *_: (jnp.minimum(i // nta, 1), 0, 0))
    grid_spec = pltpu.PrefetchScalarGridSpec(
        num_scalar_prefetch=3,
        grid=(n // tm,),
        in_specs=[
            pl.BlockSpec((tm, TOP_K), lambda i, *_: (i, 0)),
            pl.BlockSpec((tm, TOP_K), lambda i, *_: (i, 0)),
            pl.BlockSpec((tm, d), lambda i, *_: (i, 0)),
            ga2_spec,
            pl.BlockSpec((1, d), lambda i, *_: (0, 0)),
            pl.BlockSpec(memory_space=pl.ANY),
        ],
        out_specs=[
            pl.BlockSpec((tm, d), lambda i, *_: (jnp.minimum(i, nta - 1), 0)),
            pl.BlockSpec((tm, d), lambda i, *_: (jnp.maximum(i - nta, 0), 0)),
        ],
        scratch_shapes=[
            pltpu.VMEM((2, GROUPED_ROWS // ROW_GROUP, ROW_GROUP, d // 2), U32),
            pltpu.SemaphoreType.DMA((2,)),
        ],
    )
    return pl.pallas_call(
        functools.partial(_combine_kernel, tm=tm, nta=nta),
        grid_spec=grid_spec,
        out_shape=[jax.ShapeDtypeStruct((nta * tm, d), F32), jax.ShapeDtypeStruct((ntb * tm, d), F32)],
        compiler_params=_params(("arbitrary",)),
        name="moe_combine",
    )(chunks, nch, pstart, pos_t, wl_t, xs_base, ga2, gfin, yb)


def _moe_and_final(src_a, src_b, mods, wts, tm, bm):
    nta = src_a[0].shape[0] // tm
    ga1, sh2, sc2, ga2 = mods
    (w_out_b, g_ffn, wsgu_b, wsd_b, wr_t, rb, w_gate_up_l, w_down_l, g_final) = wts
    xs_base, h2, pos, wl, chunks, nch, cnt = _post(
        src_a, src_b, w_out_b, ga1, g_ffn, sc2, sh2, ga2, wsgu_b, wsd_b, wr_t, rb, tm)
    nt = xs_base.shape[0] // tm
    counts = cnt[:, 0].astype(I32)
    padded = (counts + bm - 1) // bm * bm
    pend = jnp.cumsum(padded)
    pstart = pend - padded
    nblk = -(-(nt * GROUPED_ROWS) // bm) + N_EXPERTS
    nused = (pend[-1] // bm).astype(I32)
    blk_row = jnp.minimum(jnp.arange(nblk, dtype=I32), nused - 1) * bm
    be = jnp.sum((pend[None, :] <= blk_row[:, None]).astype(I32), axis=1)
    chunks = chunks.reshape(-1)
    nch = nch[:, 0, 0]
    pslab = pstart // ROW_GROUP
    xs = _dispatch(chunks, nch, pslab, pend // ROW_GROUP, pos, h2, nblk * bm, tm, bm)
    xs = xs.reshape(nblk * bm, xs.shape[-1])
    region_end = jnp.sum(jnp.where(be[:, None] == jnp.arange(N_EXPERTS, dtype=I32)[None, :],
                                   (pstart + counts)[None, :], 0), axis=1)
    valid = jnp.clip(region_end - jnp.arange(nblk, dtype=I32) * bm, 0, bm)
    blk = jnp.arange(nblk, dtype=I32)
    first = ((blk < nused) & ((blk == 0) | (be != jnp.roll(be, 1)))).astype(I32)
    slot = (jnp.cumsum(first) - 1) % 2
    eids = jnp.arange(N_EXPERTS, dtype=I32)
    later_nonempty = (eids[None, :] > eids[:, None]) & (counts[None, :] > 0)
    next_tab = jnp.min(jnp.where(later_nonempty, eids[None, :], N_EXPERTS), axis=1)
    next_tab = jnp.where(next_tab == N_EXPERTS, -1, next_tab)
    next_e = jnp.sum(jnp.where(be[:, None] == eids[None, :], next_tab[None, :], 0), axis=1)
    yb = _experts(be, nused.reshape(1), valid, first, slot.astype(I32), next_e.astype(I32), xs,
                  w_gate_up_l, w_down_l, bm)
    yb = yb.reshape(nblk * bm // ROW_GROUP, ROW_GROUP, yb.shape[-1])
    return _combine(chunks, nch, pslab, pos.T, wl.T, xs_base, ga2, g_final, yb, tm, nta)


def _expand(mod, reps):
    if mod.shape[0] == 1:
        return mod
    return jnp.repeat(mod, reps, axis=0)


def kernel(x_prompt, x_sample, cache_k, cache_v, state_hgrn, c_prompt, c_sample, w_ada, b_ada,
           norm_mix, norm_ffn, norm_final, w_in, w_out, hg_lb_logits, hg_norm, da_lambda, da_norm,
           rel_bias_table, w_router, router_bias, w_gate_up, w_down, ws_gate_up, ws_down):
    depth = w_in.shape[0]
    assert depth == 1 and hg_lb_logits.shape[0] == 2
    bp, tp, d = x_prompt.shape
    bs, ts, _ = x_sample.shape
    assert bp == 1
    past = cache_k.shape[2]
    l = 0

    rows = -(-(bp + bs) // 8) * 8
    c_all = jnp.zeros((rows, d), F32).at[:bp].set(c_prompt).at[bp:bp + bs].set(c_sample)
    mod = _adaln(c_all, w_ada[l], b_ada[l])
    mod_p = [mod[0:bp, j * d:(j + 1) * d] for j in range(6)]
    mod_s = [_expand(mod[bp:bp + bs, j * d:(j + 1) * d], ts) for j in range(6)]

    w_in_b = w_in[l].astype(BF16)
    w_out_b = w_out[l].astype(BF16)
    wsgu_b = ws_gate_up[l].astype(BF16)
    wsd_b = ws_down[l].astype(BF16)
    wr_t = w_router[l].T
    g_mix = norm_mix[l].reshape(1, d)
    g_ffn = norm_ffn[l].reshape(1, d)
    g_final = norm_final.reshape(1, d)
    moe_w = (w_out_b, g_ffn, wsgu_b, wsd_b, wr_t, router_bias[l], w_gate_up[l], w_down[l], g_final)

    lam = _lam(da_lambda[l])

    t_att = min(ATT_TILE, tp)
    kk = jnp.arange(t_att, dtype=I32)[:, None]
    qq = jnp.arange(t_att, dtype=I32)[None, :]
    idx_diag = jnp.where((kk // CHUNK) <= (qq // CHUNK), _rel_bucket(kk - qq), MASK_BUCKET)
    idx_prev = _rel_bucket(kk - qq - t_att)
    bias_p = _bias_tiles(rel_bias_table, jnp.stack([idx_diag, idx_prev]).astype(I32),
                         shift_bucket=N_BUCKETS // 2 - 1)
    pad = 128
    qpos = past + jnp.arange(ts, dtype=I32)[:, None]
    idx_sp = _rel_bucket(jnp.arange(past, dtype=I32)[None, :] - qpos)
    kn = jnp.arange(pad, dtype=I32)[None, :]
    idx_sn = jnp.where(kn < ts, _rel_bucket(past + kn - qpos), MASK_BUCKET)
    bias_sp = _bias_tiles(rel_bias_table, idx_sp[None].astype(I32), shift_bucket=None)
    bias_sn = _bias_tiles(rel_bias_table, idx_sn[None].astype(I32), shift_bucket=None)

    xp = x_prompt.reshape(bp * tp, d)
    sh1, sc1, ga1, sh2, sc2, ga2 = mod_p
    assert ATT_TILE == INPROJ_TILE
    zh, qt, kf, vf, kb, vt = _inproj(xp, g_mix, sc1, sh1, w_in_b, t_att, True)
    s_zero = jnp.zeros((bp, HG_HEADS, HG_DIM, HG_DIM), F32)
    ohg_p, sp_new = _hgrn(zh, s_zero, hg_lb_logits, hg_norm[l], bp, tp, min(HGRN_CHUNK, tp))
    oda_p = _attn_prompt(kb, qt, vt, bias_p, lam, da_norm[l], t_att)
    src_p = (xp, ohg_p, oda_p)
    mods_p = (ga1, sh2, sc2, ga2)
    k_prompt = kf.reshape(1, bp, tp, DA_HEADS, 2 * DA_QKDIM)
    v_prompt = vf.reshape(1, bp, tp, DA_HEADS, DA_VDIM)

    ns = bs * ts
    xs_ = x_sample.reshape(ns, d)
    sh1, sc1, ga1, sh2, sc2, ga2 = mod_s
    zh, qs, kf, vf, kb, vb = _inproj(xs_, g_mix, sc1, sh1, w_in_b, ns, False)
    ohg_s, ss_new = _hgrn(zh, state_hgrn[l], hg_lb_logits, hg_norm[l], bs, ts, ts)
    oda_s = _attn_step(qs, cache_k[l], cache_v[l], kb, vb, bias_sp, bias_sn, lam, da_norm[l], bs, ts)
    assert ns == POST_TILE
    mods = tuple(jnp.stack([jnp.broadcast_to(mp, (POST_TILE, d)), ms_])
                 for mp, ms_ in zip(mods_p, (ga1, sh2, sc2, ga2)))
    y_p, y_s = _moe_and_final(src_p, (xs_, ohg_s, oda_s), mods, moe_w, POST_TILE, MOE_BLOCK_ROWS)
    k_sample = kf.reshape(1, bs, ts, DA_HEADS, 2 * DA_QKDIM)
    v_sample = vf.reshape(1, bs, ts, DA_HEADS, DA_VDIM)

    return (y_p.reshape(bp, tp, d), y_s.reshape(bs, ts, d), k_prompt, v_prompt, sp_new[None],
            k_sample, v_sample, ss_new[None].astype(x_sample.dtype))
```

```python
import functools
import math

import numpy as np
import jax
import jax.numpy as jnp
from jax import lax
from jax.experimental import pallas as pl
from jax.experimental.pallas import tpu as pltpu

F32 = jnp.float32
BF16 = jnp.bfloat16
I32 = jnp.int32
U32 = jnp.uint32
HIGHEST = lax.Precision.HIGHEST

D_MODEL = 1024
CHUNK = 64
HG_HEADS = 4
HG_DIM = 128
HG_WIDTH = HG_HEADS * HG_DIM
DA_HEADS = 4
DA_VDIM = 128
DA_QKDIM = 64
DA_WIDTH = DA_HEADS * DA_VDIM
N_BUCKETS = 32
MAX_DIST = 128
N_EXPERTS = 64
TOP_K = 8
N_GROUPS = 8
GROUP_SIZE = N_EXPERTS // N_GROUPS
TOP_GROUPS = 4
D_EXPERT = 256
ROUTE_SCALE = 2.5
EPS = 1e-6
LAM_INIT = 0.8 - 0.6 * math.exp(-0.3 * 0)

LOG2E = math.log2(math.e)
HI_MASK = np.uint32(0xFFFF0000)
NEG_BIG = -1e30
MASK_BUCKET = N_BUCKETS
V7X_VMEM_LIMIT = 48 * 1024 * 1024

ATT_TILE = 512
VT_ROWS = DA_VDIM + 16
HGRN_CHUNK = 256
INPROJ_TILE = 512
POST_TILE = 256
MOE_BLOCK_ROWS = 1024
ROW_GROUP = 8
GROUPED_ROWS = -(-(POST_TILE * TOP_K + N_EXPERTS * (ROW_GROUP - 1)) // 256) * 256
CHUNK_SLOTS = -(-(GROUPED_ROWS // ROW_GROUP) // 128) * 128
CHUNK_EXPERT_SHIFT = 24
CHUNK_UNROLL = 4
WAIT_CHUNKS = 16


def _sigmoid(x):
    return 1.0 / (1.0 + jnp.exp(-x))


def _silu(x):
    return x * _sigmoid(x)


def _dot(a, b, **kw):
    return jnp.dot(a, b, preferred_element_type=F32, **kw)


def _dot_nt(a, b, **kw):
    return lax.dot_general(a, b, (((1,), (1,)), ((), ())), preferred_element_type=F32, **kw)


def _dot_tn(a, b, **kw):
    return lax.dot_general(a, b, (((0,), (0,)), ((), ())), preferred_element_type=F32, **kw)


def _pack_bf16_pair(lo, hi):
    lo_bits = lax.bitcast_convert_type(lo.astype(BF16).astype(F32), U32)
    hi_bits = lax.bitcast_convert_type(hi.astype(BF16).astype(F32), U32)
    return (lo_bits >> 16) | (hi_bits & HI_MASK)


def _pack_exact_bf16_pair(lo, hi):
    return (lax.bitcast_convert_type(lo, U32) >> 16) | (lax.bitcast_convert_type(hi, U32) & HI_MASK)


def _unpack_bf16_pair(w):
    lo = lax.bitcast_convert_type(w << 16, F32)
    hi = lax.bitcast_convert_type(w & HI_MASK, F32)
    return lo, hi


def _params(sem, vmem=V7X_VMEM_LIMIT, flags=None):
    return pltpu.CompilerParams(dimension_semantics=sem, vmem_limit_bytes=vmem, flags=flags)


def _adaln_kernel(c_ref, w_ref, b_ref, o_ref):
    s = _silu(c_ref[...])
    o_ref[...] = _dot(s, w_ref[...], precision=HIGHEST) + b_ref[...]


def _adaln(c_all, w_ada, b_ada):
    rows, d = c_all.shape
    cols = w_ada.shape[1]
    blk = 1024
    return pl.pallas_call(
        _adaln_kernel,
        grid=(cols // blk,),
        in_specs=[
            pl.BlockSpec((rows, d), lambda j: (0, 0)),
            pl.BlockSpec((d, blk), lambda j: (0, j)),
            pl.BlockSpec((1, blk), lambda j: (0, j)),
        ],
        out_specs=pl.BlockSpec((rows, blk), lambda j: (0, j)),
        out_shape=jax.ShapeDtypeStruct((rows, cols), F32),
        compiler_params=_params(("parallel",)),
        name="adaln",
    )(c_all, w_ada, b_ada.reshape(1, cols))


def _lam_kernel(l_ref, o_ref):
    l = l_ref[...].astype(F32)
    a = jnp.sum(l[0:1] * l[1:2], axis=-1, keepdims=True)
    b = jnp.sum(l[2:3] * l[3:4], axis=-1, keepdims=True)
    lam = jnp.exp(a) - jnp.exp(b) + LAM_INIT
    o_ref[...] = jnp.broadcast_to(lam, o_ref.shape)


def _lam(da_lambda_l):
    return pl.pallas_call(
        _lam_kernel,
        out_shape=jax.ShapeDtypeStruct((8, 128), F32),
        name="lam",
    )(da_lambda_l)


def _rel_bucket(rel):
    nb = N_BUCKETS // 2
    max_exact = nb // 2
    side = jnp.where(rel > 0, nb, 0)
    n = jnp.abs(rel)
    large = max_exact + (jnp.log(jnp.maximum(n, 1).astype(F32) / max_exact)
                         / math.log(MAX_DIST / max_exact) * (nb - max_exact)).astype(I32)
    large = jnp.minimum(large, nb - 1)
    return side + jnp.where(n < max_exact, n, large)


def _bias_kernel(tab_ref, idx_ref, o_ref, *, shift_bucket):
    h = pl.program_id(0)
    idx = idx_ref[...]
    shift = tab_ref[shift_bucket, h] if shift_bucket is not None else 0.0
    acc = jnp.zeros(idx.shape, F32)
    for j in range(N_BUCKETS):
        acc = jnp.where(idx == j, (tab_ref[j, h] - shift) * LOG2E, acc)
    o_ref[...] = jnp.where(idx == MASK_BUCKET, NEG_BIG, acc)


def _bias_tiles(table, idx, *, shift_bucket):
    k, r, c = idx.shape
    return pl.pallas_call(
        functools.partial(_bias_kernel, shift_bucket=shift_bucket),
        grid=(DA_HEADS, k),
        in_specs=[
            pl.BlockSpec(memory_space=pltpu.SMEM),
            pl.BlockSpec((None, r, c), lambda h, d: (d, 0, 0)),
        ],
        out_specs=pl.BlockSpec((None, None, r, c), lambda h, d: (h, d, 0, 0)),
        out_shape=jax.ShapeDtypeStruct((DA_HEADS, k, r, c), F32),
        compiler_params=_params(("parallel", "parallel")),
        name="rel_bias",
    )(table, idx)


def _inproj_kernel(x_ref, g_ref, sc_ref, sh_ref, w_ref,
                   zh_ref, q_ref, k_ref, v_ref, kb_ref, vb_ref, *, transposed):
    x = x_ref[...]
    ms = jnp.mean(x * x, axis=-1, keepdims=True)
    h = x * lax.rsqrt(ms + EPS) * g_ref[...]
    h = h * (1.0 + sc_ref[...]) + sh_ref[...]
    hb = h.astype(BF16)
    c0 = 4 * HG_WIDTH
    zh_ref[...] = _dot(hb, w_ref[:, 0:c0])
    zq = _dot(hb, w_ref[:, c0:c0 + DA_WIDTH]) * (DA_QKDIM ** -0.5 * LOG2E)
    zk = _dot(hb, w_ref[:, c0 + DA_WIDTH:c0 + 2 * DA_WIDTH])
    for hd in range(DA_HEADS):
        k_ref[:, hd, :] = zk[:, hd * DA_VDIM:(hd + 1) * DA_VDIM]
    kb_ref[...] = zk.astype(BF16)
    zv = _dot(hb, w_ref[:, c0 + 2 * DA_WIDTH:c0 + 3 * DA_WIDTH])
    for hd in range(DA_HEADS):
        v_ref[:, hd, :] = zv[:, hd * DA_VDIM:(hd + 1) * DA_VDIM]
    if transposed:
        q_ref[...] = zq.T.astype(BF16)
        vb_ref[:, 0:DA_VDIM, :] = zv.T.astype(BF16).reshape(DA_HEADS, DA_VDIM, zv.shape[0])
        vb_ref[:, DA_VDIM:, :] = jnp.ones((DA_HEADS, VT_ROWS - DA_VDIM, zv.shape[0]), BF16)
    else:
        q_ref[...] = zq.astype(BF16)
        vb_ref[...] = zv.astype(BF16)


def _mod_spec(mod, tm):
    if mod.shape[0] == 1:
        return pl.BlockSpec((1, mod.shape[1]), lambda i: (0, 0))
    return pl.BlockSpec((tm, mod.shape[1]), lambda i: (i, 0))


def _inproj(x, g, sc, sh, w_in_b, tm, transposed):
    n, d = x.shape
    cols = w_in_b.shape[1]
    row = lambda i: (i, 0)
    if transposed:
        q_spec = pl.BlockSpec((DA_WIDTH, tm), lambda i: (0, i))
        q_shape = jax.ShapeDtypeStruct((DA_WIDTH, n), BF16)
        vb_spec = pl.BlockSpec((DA_HEADS, None, VT_ROWS, tm), lambda i: (0, i, 0, 0))
        vb_shape = jax.ShapeDtypeStruct((DA_HEADS, n // tm, VT_ROWS, tm), BF16)
    else:
        q_spec = vb_spec = pl.BlockSpec((tm, DA_WIDTH), row)
        q_shape = vb_shape = jax.ShapeDtypeStruct((n, DA_WIDTH), BF16)
    return pl.pallas_call(
        functools.partial(_inproj_kernel, transposed=transposed),
        grid=(n // tm,),
        in_specs=[
            pl.BlockSpec((tm, d), row),
            pl.BlockSpec((1, d), lambda i: (0, 0)),
            _mod_spec(sc, tm),
            _mod_spec(sh, tm),
            pl.BlockSpec((d, cols), lambda i: (0, 0)),
        ],
        out_specs=[
            pl.BlockSpec((tm, 4 * HG_WIDTH), row),
            q_spec,
            pl.BlockSpec((tm, DA_HEADS, DA_VDIM), lambda i: (i, 0, 0)),
            pl.BlockSpec((tm, DA_HEADS, DA_VDIM), lambda i: (i, 0, 0)),
            pl.BlockSpec((tm, DA_WIDTH), row),
            vb_spec,
        ],
        out_shape=[
            jax.ShapeDtypeStruct((n, 4 * HG_WIDTH), F32),
            q_shape,
            jax.ShapeDtypeStruct((n, DA_HEADS, DA_VDIM), F32),
            jax.ShapeDtypeStruct((n, DA_HEADS, DA_VDIM), F32),
            jax.ShapeDtypeStruct((n, DA_WIDTH), BF16),
            vb_shape,
        ],
        compiler_params=_params(("parallel",)),
        name="inproj",
    )(x, g, sc, sh, w_in_b)


def _hgrn_consts(c):
    levels = int(round(math.log2(c)))
    assert 1 << levels == c and levels >= 3
    t = np.arange(c)[:, None]
    r = np.arange(c)[None, :]
    tri = (r <= t).astype(np.float32)
    x = np.maximum(t ^ r, 1)
    lv = np.where(t == r, -1, np.where(t > r, np.floor(np.log2(x)).astype(np.int64), -2))
    return jnp.asarray(tri, dtype=BF16), jnp.asarray(lv, dtype=I32), levels


def _hgrn_kernel(zh_ref, s0_ref, lbl_ref, gain_ref, mall_ref, lv_ref,
                 o_ref, sout_ref, st_ref, b_ref, *, c, levels):
    ci = pl.program_id(1)

    @pl.when(ci == 0)
    def _():
        for h in range(HG_HEADS):
            st_ref[h] = s0_ref[h].astype(F32).T

    lbl = lbl_ref[...].astype(F32)
    mx = jnp.maximum(lbl[0:1], lbl[1:2])
    e0 = jnp.exp(lbl[0:1] - mx)
    e1 = jnp.exp(lbl[1:2] - mx)
    lb = e0 / (e0 + e1)

    xq = zh_ref[:, 0:HG_WIDTH]
    xf = zh_ref[:, HG_WIDTH:2 * HG_WIDTH]
    q = _silu(xq)
    y = lb + (1.0 - lb) * _sigmoid(xf)
    logf = jnp.log(y)
    kk = 1.0 - y

    l1 = logf.astype(BF16)
    r1 = logf - l1.astype(F32)
    l2 = r1.astype(BF16)
    l3 = (r1 - l2.astype(F32)).astype(BF16)
    tri = mall_ref[...]
    b = _dot(tri, l1) + _dot(tri, l2) + _dot(tri, l3)
    b_ref[...] = b
    trow = lax.broadcasted_iota(I32, b.shape, 0)

    def level_exponent(l):
        m = 1 << l
        later = (trow & m) != 0
        if l == 0:
            return jnp.where(later, logf, 0.0)
        if l == 1:
            below = pltpu.roll(logf, 1, 0)
            above = pltpu.roll(logf, c - 1, 0)
            low = (trow & 1) != 0
            return jnp.where(later, jnp.where(low, logf + below, logf), jnp.where(low, 0.0, above))
        mid = jnp.concatenate(
            [jnp.broadcast_to(b_ref[k * 2 * m + m - 1:k * 2 * m + m, :], (2 * m, b.shape[1]))
             for k in range(c // (2 * m))], axis=0)
        return jnp.where(later, b - mid, mid - b)

    factors = [jnp.exp(level_exponent(l)) for l in range(levels)]
    lv = lv_ref[...]
    gain = gain_ref[...].astype(F32)
    for h in range(HG_HEADS):
        sl = slice(h * HG_DIM, (h + 1) * HG_DIM)
        qh = q[:, sl]
        kh = kk[:, sl]
        ih = zh_ref[:, 2 * HG_WIDTH + h * HG_DIM:2 * HG_WIDTH + (h + 1) * HG_DIM]
        gh = zh_ref[:, 3 * HG_WIDTH + h * HG_DIM:3 * HG_WIDTH + (h + 1) * HG_DIM]
        bh = b[:, sl]
        ihb = ih.astype(BF16)
        a = jnp.where(lv == -1, _dot_nt(qh.astype(BF16), kh.astype(BF16)), 0.0)
        for l in range(levels):
            f = factors[l][:, sl]
            p = _dot_nt((qh * f).astype(BF16), (kh * f).astype(BF16))
            a = jnp.where(lv == l, p, a)
        st = st_ref[h]
        o = _dot(a.astype(BF16), ihb) + _dot_nt((qh * jnp.exp(bh)).astype(BF16), st.astype(BF16))
        bl = bh[c - 1:c, :]
        kd = (kh * jnp.exp(bl - bh)).astype(BF16)
        st_ref[h] = st * jnp.exp(bl) + _dot_tn(ihb, kd)
        ms = jnp.mean(o * o, axis=-1, keepdims=True)
        on = o * lax.rsqrt(ms + EPS) * gain
        o_ref[:, sl] = (on * _silu(gh)).astype(o_ref.dtype)

    @pl.when(ci == pl.num_programs(1) - 1)
    def _():
        for h in range(HG_HEADS):
            sout_ref[h] = st_ref[h].T.astype(sout_ref.dtype)


def _hgrn(zh, s0, lb_logits, gain, batch, seq, c):
    mall, lv, levels = _hgrn_consts(c)
    nc = seq // c
    return pl.pallas_call(
        functools.partial(_hgrn_kernel, c=c, levels=levels),
        grid=(batch, nc),
        in_specs=[
            pl.BlockSpec((c, 4 * HG_WIDTH), lambda b, i: (b * nc + i, 0)),
            pl.BlockSpec((None, HG_HEADS, HG_DIM, HG_DIM), lambda b, i: (b, 0, 0, 0)),
            pl.BlockSpec(lb_logits.shape, lambda b, i: (0, 0)),
            pl.BlockSpec((1, HG_DIM), lambda b, i: (0, 0)),
            pl.BlockSpec(mall.shape, lambda b, i: (0, 0)),
            pl.BlockSpec(lv.shape, lambda b, i: (0, 0)),
        ],
        out_specs=[
            pl.BlockSpec((c, HG_WIDTH), lambda b, i: (b * nc + i, 0)),
            pl.BlockSpec((None, HG_HEADS, HG_DIM, HG_DIM), lambda b, i: (b, 0, 0, 0)),
        ],
        out_shape=[
            jax.ShapeDtypeStruct((batch * seq, HG_WIDTH), BF16),
            jax.ShapeDtypeStruct((batch, HG_HEADS, HG_DIM, HG_DIM), F32),
        ],
        scratch_shapes=[pltpu.VMEM((HG_HEADS, HG_DIM, HG_DIM), F32),
                        pltpu.VMEM((c, HG_WIDTH), F32)],
        compiler_params=_params(("parallel", "arbitrary")),
        name="hgrn2",
    )(zh, s0, lb_logits, gain.reshape(1, HG_DIM), mall, lv)


def _attn_kernel(k_ref, qt_ref, vt_ref, bias_ref, lam_ref, gain_ref,
                 o_ref, qz_ref, m_ref, acc_ref, s_ref, smax_ref, *, t):
    i = pl.program_id(1)
    qt = qt_ref[...]
    row = lax.broadcasted_iota(I32, qt.shape, 0)
    zero = jnp.zeros_like(qt)
    qz_ref[:, 0:t] = jnp.where(row < DA_QKDIM, qt, zero)
    qz_ref[:, t:2 * t] = jnp.where(row >= DA_QKDIM, qt, zero)
    m_ref[...] = jnp.full(m_ref.shape, NEG_BIG, F32)
    acc_ref[...] = jnp.zeros(acc_ref.shape, F32)

    def scores(j, buf):
        kt = k_ref[pl.ds(pl.multiple_of(j * t, t), t), :]
        s = _dot(kt, qz_ref[...])
        s_ref[buf] = s
        smax_ref[buf] = jnp.max(s, axis=0, keepdims=True)

    def consume(j, buf, bias_idx):
        s = s_ref[buf]
        if bias_idx is not None:
            b = bias_ref[bias_idx]
            s = jnp.concatenate([s[:, 0:t] + b, s[:, t:2 * t] + b], axis=1)
            s_max = jnp.max(s, axis=0, keepdims=True)
        else:
            s_max = smax_ref[buf]
        m_prev = m_ref[...]
        m_new = jnp.maximum(m_prev, s_max)
        alpha = jnp.exp2(m_prev - m_new)
        pr = jnp.exp2(s - m_new).astype(BF16)
        acc_ref[...] = alpha * acc_ref[...] + _dot(vt_ref[j], pr)
        m_ref[...] = m_new

    n_far = jnp.maximum(i - 1, 0)

    @pl.when(n_far > 0)
    def _():
        scores(0, 0)

    def far_tiles(j, count):
        for u in range(count):
            nxt = j + u + 1
            if u == count - 1:
                nxt = jnp.minimum(nxt, n_far - 1)
            scores(nxt, (u + 1) % 2)
            consume(j + u, u % 2, None)

    def far_quad(p, carry):
        far_tiles(4 * p, 4)
        return carry

    def far_pair(p, carry):
        far_tiles(4 * quads + 2 * p, 2)
        return carry

    quads = n_far // 4
    lax.fori_loop(0, quads, far_quad, 0)
    lax.fori_loop(0, (n_far - 4 * quads) // 2, far_pair, 0)

    @pl.when(lax.rem(n_far, 2) == 1)
    def _():
        consume(n_far - 1, 0, None)

    @pl.when(i >= 1)
    def _():
        scores(i - 1, 0)
        scores(i, 1)
        consume(i - 1, 0, 1)
        consume(i, 1, 0)

    @pl.when(i == 0)
    def _():
        scores(i, 1)
        consume(i, 1, 0)

    lam = lam_ref[0:1, 0:1]
    acc = acc_ref[0:DA_VDIM, :]
    l = acc_ref[DA_VDIM:DA_VDIM + 1, :]
    o = acc[:, 0:t] / l[:, 0:t] - lam * (acc[:, t:2 * t] / l[:, t:2 * t])
    ms = jnp.mean(o * o, axis=0, keepdims=True)
    on = o * lax.rsqrt(ms + EPS) * gain_ref[...].astype(F32) * (1.0 - LAM_INIT)
    o_ref[...] = on.T.astype(o_ref.dtype)


def _attn_prompt(kb, qt, vt, bias, lam, gain, t):
    n = kb.shape[0]
    nt = n // t
    return pl.pallas_call(
        functools.partial(_attn_kernel, t=t),
        grid=(DA_HEADS, nt),
        in_specs=[
            pl.BlockSpec((n, DA_VDIM), lambda h, i: (0, h)),
            pl.BlockSpec((DA_VDIM, t), lambda h, i: (h, i)),
            pl.BlockSpec((None, nt, VT_ROWS, t), lambda h, i: (h, 0, 0, 0)),
            pl.BlockSpec((None, 2, t, t), lambda h, i: (h, 0, 0, 0)),
            pl.BlockSpec((8, 128), lambda h, i: (0, 0)),
            pl.BlockSpec((DA_VDIM, 1), lambda h, i: (0, 0)),
        ],
        out_specs=pl.BlockSpec((t, DA_VDIM), lambda h, i: (i, h)),
        out_shape=jax.ShapeDtypeStruct((n, DA_WIDTH), BF16),
        scratch_shapes=[
            pltpu.VMEM((DA_VDIM, 2 * t), BF16),
            pltpu.VMEM((1, 2 * t), F32),
            pltpu.VMEM((VT_ROWS, 2 * t), F32),
            pltpu.VMEM((2, t, 2 * t), F32),
            pltpu.VMEM((2, 1, 2 * t), F32),
        ],
        compiler_params=_params(("parallel", "parallel")),
        name="diff_attn_prompt",
    )(kb, qt, vt, bias, lam, gain.reshape(DA_VDIM, 1))


def _attn_step_kernel(q_ref, kp_ref, vp_ref, kn_ref, vn_ref, bp_ref, bn_ref, lam_ref, gain_ref,
                      o_ref, *, tq, pad):
    lam = lam_ref[0:1, 0:1]
    gain = gain_ref[...].astype(F32)
    zpad = jnp.zeros((pad - tq, DA_VDIM), BF16)
    for h in range(DA_HEADS):
        hs = slice(h * DA_VDIM, (h + 1) * DA_VDIM)
        q = q_ref[:, hs]
        lane = lax.broadcasted_iota(I32, q.shape, 1)
        zero = jnp.zeros_like(q)
        qz = jnp.concatenate([jnp.where(lane < DA_QKDIM, q, zero),
                              jnp.where(lane >= DA_QKDIM, q, zero)], axis=0)
        kp = kp_ref[:, h, :].astype(BF16)
        vp = vp_ref[:, h, :].astype(BF16)
        kn = jnp.concatenate([kn_ref[:, hs], zpad], axis=0)
        vn = jnp.concatenate([vn_ref[:, hs], zpad], axis=0)
        bp = bp_ref[h, 0]
        bn = bn_ref[h, 0]
        sp = _dot_nt(qz, kp) + jnp.concatenate([bp, bp], axis=0)
        sn = _dot_nt(qz, kn) + jnp.concatenate([bn, bn], axis=0)
        m = jnp.maximum(jnp.max(sp, axis=-1, keepdims=True), jnp.max(sn, axis=-1, keepdims=True))
        pp = jnp.exp2(sp - m)
        pn = jnp.exp2(sn - m)
        l = jnp.sum(pp, axis=-1, keepdims=True) + jnp.sum(pn, axis=-1, keepdims=True)
        acc = _dot(pp.astype(BF16), vp) + _dot(pn.astype(BF16), vn)
        on = acc / l
        o = on[0:tq] - lam * on[tq:2 * tq]
        ms = jnp.mean(o * o, axis=-1, keepdims=True)
        o = o * lax.rsqrt(ms + EPS) * gain * (1.0 - LAM_INIT)
        o_ref[:, hs] = o.astype(o_ref.dtype)


def _attn_step(qs, cache_k_l, cache_v_l, kb, vb, bias_p, bias_n, lam, gain, batch, tq):
    past = cache_k_l.shape[1]
    pad = bias_n.shape[-1]
    cache_spec = pl.BlockSpec((None, past, DA_HEADS, DA_VDIM), lambda b: (b, 0, 0, 0))
    row = pl.BlockSpec((tq, DA_WIDTH), lambda b: (b, 0))
    return pl.pallas_call(
        functools.partial(_attn_step_kernel, tq=tq, pad=pad),
        grid=(batch,),
        in_specs=[
            row, cache_spec, cache_spec, row, row,
            pl.BlockSpec(bias_p.shape, lambda b: (0, 0, 0, 0)),
            pl.BlockSpec(bias_n.shape, lambda b: (0, 0, 0, 0)),
            pl.BlockSpec((8, 128), lambda b: (0, 0)),
            pl.BlockSpec((1, DA_VDIM), lambda b: (0, 0)),
        ],
        out_specs=row,
        out_shape=jax.ShapeDtypeStruct((batch * tq, DA_WIDTH), BF16),
        compiler_params=_params(("parallel",)),
        name="diff_attn_step",
    )(qs, cache_k_l, cache_v_l, kb, vb, bias_p, bias_n, lam, gain.reshape(1, DA_VDIM))


def _post_kernel(xa_ref, ohga_ref, odaa_ref, xb_ref, ohgb_ref, odab_ref,
                 wout_ref, ga1_ref, g_ref, sc_ref, sh_ref, ga2_ref,
                 wsgu_ref, wsd_ref, wrt_ref, rb_ref, tri_ref, ltri_ref,
                 xs_ref, h2_ref, pos_ref, wl_ref, chunk_ref, nch_ref, cnt_ref, carry_ref, *, tm, nta):
    i = pl.program_id(0)

    @pl.when(i == 0)
    def _():
        carry_ref[...] = jnp.zeros(carry_ref.shape, F32)

    second = i >= nta
    x = jnp.where(second, xb_ref[...], xa_ref[...])
    ohg = jnp.where(second, ohgb_ref[...], ohga_ref[...])
    oda = jnp.where(second, odab_ref[...], odaa_ref[...])
    mix = _dot(ohg, wout_ref[0:HG_WIDTH, :]) + _dot(oda, wout_ref[HG_WIDTH:, :])
    x1 = x + ga1_ref[...] * mix
    ms = jnp.mean(x1 * x1, axis=-1, keepdims=True)
    h2 = x1 * lax.rsqrt(ms + EPS) * g_ref[...]
    h2 = h2 * (1.0 + sc_ref[...]) + sh_ref[...]
    h2b = h2.astype(BF16)
    h2_ref[...] = h2b
    gu = _dot(h2b, wsgu_ref[...])
    act = (_silu(gu[:, 0:D_EXPERT]) * gu[:, D_EXPERT:]).astype(BF16)
    xs_ref[...] = x1 + ga2_ref[...] * _dot(act, wsd_ref[...])

    logits = _dot_nt(wrt_ref[...], h2, precision=HIGHEST)
    score = _sigmoid(logits)
    sel = score + rb_ref[...]
    sub = lax.broadcasted_iota(I32, (GROUP_SIZE, tm), 0)
    gscore = []
    for g in range(N_GROUPS):
        v = sel[g * GROUP_SIZE:(g + 1) * GROUP_SIZE, :]
        m1 = jnp.max(v, axis=0, keepdims=True)
        i1 = jnp.min(jnp.where(v == m1, sub, GROUP_SIZE), axis=0, keepdims=True)
        m2 = jnp.max(jnp.where(sub == i1, -jnp.inf, v), axis=0, keepdims=True)
        gscore.append(m1 + m2)
    gsel = []
    for g in range(N_GROUPS):
        ahead = jnp.zeros((1, tm), F32)
        for g2 in range(N_GROUPS):
            if g2 == g:
                continue
            tie = 1.0 if g2 < g else 0.0
            ahead = ahead + jnp.where(gscore[g2] > gscore[g], 1.0,
                                      jnp.where(gscore[g2] == gscore[g], tie, 0.0))
        gsel.append(ahead < TOP_GROUPS)
    selm = jnp.concatenate(
        [jnp.where(gsel[g], sel[g * GROUP_SIZE:(g + 1) * GROUP_SIZE, :], -jnp.inf)
         for g in range(N_GROUPS)], axis=0)
    eio = lax.broadcasted_iota(I32, (N_EXPERTS, tm), 0)
    ahead = jnp.zeros((N_EXPERTS, tm), F32)
    for e2 in range(N_EXPERTS):
        row = selm[e2:e2 + 1, :]
        tie = jnp.where(eio > e2, 1.0, 0.0)
        ahead = ahead + jnp.where(row > selm, 1.0, jnp.where(row == selm, tie, 0.0))
    chosen = jnp.where(selm > -jnp.inf, jnp.where(ahead < TOP_K, 1.0, 0.0), 0.0)
    w = chosen * score
    wn = w / jnp.sum(w, axis=0, keepdims=True) * ROUTE_SCALE

    chb = chosen.astype(BF16)
    before = _dot(chb, tri_ref[...])
    tot = _dot(chb, jnp.ones((tm, 128), BF16))
    run = jnp.floor((tot + (ROW_GROUP - 1)) * (1.0 / ROW_GROUP)) * ROW_GROUP
    tile_base = _dot(ltri_ref[...], run.astype(BF16))
    carry = carry_ref[...]
    carry_ref[...] = carry + run
    cnt_ref[...] = carry + run
    pos = jnp.concatenate([tile_base] * (tm // 128), axis=1) + before

    widen = lambda v: jnp.concatenate([v] * (CHUNK_SLOTS // 128), axis=1)
    crow = lax.broadcasted_iota(I32, (N_EXPERTS, CHUNK_SLOTS), 1).astype(F32) * ROW_GROUP
    erow = lax.broadcasted_iota(I32, (N_EXPERTS, CHUNK_SLOTS), 0).astype(F32)
    owner = jnp.sum(jnp.where(widen(tile_base + run) <= crow, 1.0, 0.0), axis=0, keepdims=True)
    region_row = jnp.sum(jnp.where(owner == erow, widen(carry - tile_base), 0.0),
                         axis=0, keepdims=True) + crow[0:1]
    region_slab = (region_row * (1.0 / ROW_GROUP)).astype(I32)
    chunk_ref[...] = owner.astype(I32) * (1 << CHUNK_EXPERT_SHIFT) + region_slab
    nch_ref[...] = jnp.sum(run * (1.0 / ROW_GROUP), axis=0, keepdims=True).astype(I32)

    for r in range(TOP_K):
        pick = jnp.where(ahead == r, chosen, 0.0)
        pos_ref[r:r + 1, :] = jnp.sum(pick * pos, axis=0, keepdims=True).astype(I32)
        wl_ref[r:r + 1, :] = jnp.sum(pick * wn, axis=0, keepdims=True)


def _post(src_a, src_b, w_out_b, ga1, g, sc, sh, ga2, wsgu_b, wsd_b, wr_t, rb, tm):
    (xa, ohga, odaa), (xb, ohgb, odab) = src_a, src_b
    d = xa.shape[1]
    nta, ntb = xa.shape[0] // tm, xb.shape[0] // tm
    nt = nta + ntb
    n = nt * tm
    tri = jnp.asarray(np.triu(np.ones((tm, tm), np.float32), k=1), dtype=BF16)
    ltri = jnp.asarray(np.tril(np.ones((N_EXPERTS, N_EXPERTS), np.float32), k=-1), dtype=BF16)
    row = lambda i: (i, 0)
    row_a = lambda i: (jnp.minimum(i, nta - 1), 0)
    row_b = lambda i: (jnp.maximum(i - nta, 0), 0)
    col = lambda i: (0, i)
    full = lambda i: (0, 0)
    mod = pl.BlockSpec((None, tm, d), lambda i: (jnp.minimum(i // nta, 1), 0, 0))
    return pl.pallas_call(
        functools.partial(_post_kernel, tm=tm, nta=nta),
        grid=(nt,),
        in_specs=[
            pl.BlockSpec((tm, d), row_a),
            pl.BlockSpec((tm, HG_WIDTH), row_a),
            pl.BlockSpec((tm, DA_WIDTH), row_a),
            pl.BlockSpec((tm, d), row_b),
            pl.BlockSpec((tm, HG_WIDTH), row_b),
            pl.BlockSpec((tm, DA_WIDTH), row_b),
            pl.BlockSpec(w_out_b.shape, full),
            mod,
            pl.BlockSpec((1, d), full),
            mod,
            mod,
            mod,
            pl.BlockSpec(wsgu_b.shape, full),
            pl.BlockSpec(wsd_b.shape, full),
            pl.BlockSpec(wr_t.shape, full),
            pl.BlockSpec((N_EXPERTS, 1), full),
            pl.BlockSpec((tm, tm), full),
            pl.BlockSpec((N_EXPERTS, N_EXPERTS), full),
        ],
        out_specs=[
            pl.BlockSpec((tm, d), row),
            pl.BlockSpec((tm, d), row),
            pl.BlockSpec((TOP_K, tm), col),
            pl.BlockSpec((TOP_K, tm), col),
            pl.BlockSpec((None, 1, CHUNK_SLOTS), lambda i: (i, 0, 0)),
            pl.BlockSpec((None, 1, 128), lambda i: (i, 0, 0)),
            pl.BlockSpec((N_EXPERTS, 128), full),
        ],
        out_shape=[
            jax.ShapeDtypeStruct((n, d), F32),
            jax.ShapeDtypeStruct((n, d), BF16),
            jax.ShapeDtypeStruct((TOP_K, n), I32),
            jax.ShapeDtypeStruct((TOP_K, n), F32),
            jax.ShapeDtypeStruct((nt, 1, CHUNK_SLOTS), I32),
            jax.ShapeDtypeStruct((nt, 1, 128), I32),
            jax.ShapeDtypeStruct((N_EXPERTS, 128), F32),
        ],
        scratch_shapes=[pltpu.VMEM((N_EXPERTS, 128), F32)],
        compiler_params=_params(("arbitrary",)),
        name="post_mix_router",
    )(xa, ohga, odaa, xb, ohgb, odab, w_out_b, ga1, g, sc, sh, ga2, wsgu_b, wsd_b, wr_t,
      rb.reshape(N_EXPERTS, 1), tri, ltri)


def _start_chunks(tile, chunk_ref, nch_ref, pslab_ref, make_copy):
    n = nch_ref[tile]

    def start(c):
        word = chunk_ref[tile * CHUNK_SLOTS + c]
        expert = lax.shift_right_logical(word, CHUNK_EXPERT_SHIFT)
        region_slab = word & ((1 << CHUNK_EXPERT_SHIFT) - 1)
        make_copy(c, pslab_ref[expert] + region_slab, 1).start()

    def group(g, carry):
        for u in range(CHUNK_UNROLL):
            start(g * CHUNK_UNROLL + u)
        return carry

    def single(c, carry):
        start(c)
        return carry

    groups = n // CHUNK_UNROLL
    lax.fori_loop(0, groups, group, 0)
    lax.fori_loop(groups * CHUNK_UNROLL, n, single, 0)


def _wait_chunks(tile, nch_ref, make_copy):
    n = nch_ref[tile]
    many = n // WAIT_CHUNKS

    def wait_many(j, carry):
        make_copy(0, 0, WAIT_CHUNKS).wait()
        return carry

    def wait_one(j, carry):
        make_copy(0, 0, 1).wait()
        return carry

    lax.fori_loop(0, many, wait_many, 0)
    lax.fori_loop(many * WAIT_CHUNKS, n, wait_one, 0)


def _dispatch_kernel(chunk_ref, nch_ref, pstart_ref, pend_ref, pos_ref, h2_ref, xs_hbm,
                     cbuf_ref, zero_ref, zsem, sem, *, tm, bm):
    i = pl.program_id(0)
    nt = pl.num_programs(0)
    bslabs = bm // ROW_GROUP
    nblk = xs_hbm.shape[0] // bslabs
    dh = cbuf_ref.shape[-1]

    def zero_copy(e):
        return pltpu.make_async_copy(zero_ref, xs_hbm.at[pl.ds(pend_ref[e] - bslabs, bslabs)], zsem)

    def tail_copy(b):
        return pltpu.make_async_copy(zero_ref, xs_hbm.at[pl.ds(b * bslabs, bslabs)], zsem)

    @pl.when(i == 0)
    def _():
        zero_ref[...] = jnp.zeros(zero_ref.shape, zero_ref.dtype)
        first_unused = pend_ref[N_EXPERTS - 1] // bslabs

        def zissue(e, carry):
            @pl.when(pend_ref[e] > pstart_ref[e])
            def _():
                zero_copy(e).start()
            return carry

        def zwait(e, carry):
            @pl.when(pend_ref[e] > pstart_ref[e])
            def _():
                zero_copy(e).wait()
            return carry

        def tissue(b, carry):
            tail_copy(b).start()
            return carry

        def twait(b, carry):
            tail_copy(b).wait()
            return carry

        lax.fori_loop(0, N_EXPERTS, zissue, 0)
        lax.fori_loop(first_unused, nblk, tissue, 0)
        lax.fori_loop(0, N_EXPERTS, zwait, 0)
        lax.fori_loop(first_unused, nblk, twait, 0)

    pos = pos_ref[...]
    piota = lax.broadcasted_iota(I32, (GROUPED_ROWS, tm), 0).astype(jnp.int16)
    pos16 = pos.astype(jnp.int16)
    one = jnp.ones((GROUPED_ROWS, tm), BF16)
    perm = jnp.zeros((GROUPED_ROWS, tm), BF16)
    for r in range(TOP_K):
        perm = jnp.where(piota == pos16[r:r + 1, :], one, perm)
    cur = lax.rem(i, 2)
    grouped = _pack_exact_bf16_pair(_dot(perm, h2_ref[:, 0:dh]), _dot(perm, h2_ref[:, dh:]))
    cbuf_ref[cur] = grouped.reshape(cbuf_ref.shape[1:])

    def make_copy(buf):
        def build(tile_slab, buffer_slab, slabs):
            return pltpu.make_async_copy(cbuf_ref.at[buf, pl.ds(tile_slab, slabs)],
                                         xs_hbm.at[pl.ds(buffer_slab, slabs)], sem.at[buf])
        return build

    _start_chunks(i, chunk_ref, nch_ref, pstart_ref, make_copy(cur))

    @pl.when(i > 0)
    def _():
        _wait_chunks(i - 1, nch_ref, make_copy(1 - cur))

    @pl.when(i == nt - 1)
    def _():
        _wait_chunks(i, nch_ref, make_copy(cur))


def _dispatch(chunks, nch, pstart, pend, pos, h2, nrows, tm, bm):
    n, d = h2.shape
    grid_spec = pltpu.PrefetchScalarGridSpec(
        num_scalar_prefetch=4,
        grid=(n // tm,),
        in_specs=[
            pl.BlockSpec((TOP_K, tm), lambda i, *_: (0, i)),
            pl.BlockSpec((tm, d), lambda i, *_: (i, 0)),
        ],
        out_specs=pl.BlockSpec(memory_space=pl.ANY),
        scratch_shapes=[
            pltpu.VMEM((2, GROUPED_ROWS // ROW_GROUP, ROW_GROUP, d // 2), U32),
            pltpu.VMEM((bm // ROW_GROUP, ROW_GROUP, d // 2), U32),
            pltpu.SemaphoreType.DMA(()),
            pltpu.SemaphoreType.DMA((2,)),
        ],
    )
    return pl.pallas_call(
        functools.partial(_dispatch_kernel, tm=tm, bm=bm),
        grid_spec=grid_spec,
        out_shape=jax.ShapeDtypeStruct((nrows // ROW_GROUP, ROW_GROUP, d // 2), U32),
        compiler_params=_params(("arbitrary",)),
        name="moe_dispatch",
    )(chunks, nch, pstart, pend, pos, h2)


def _experts_kernel(be_ref, nu_ref, valid_ref, first_ref, slot_ref, next_ref, x_ref, wgu_hbm, wd_hbm,
                    o_ref, wgu_f_ref, wd_f_ref, wgu_b_ref, wd_b_ref, sem, *, bm):
    i = pl.program_id(0)
    valid = valid_ref[i]
    sub = bm // 2

    def weight_copies(expert, slot):
        return (pltpu.make_async_copy(wgu_hbm.at[expert], wgu_f_ref.at[slot], sem.at[slot, 0]),
                pltpu.make_async_copy(wd_hbm.at[expert], wd_f_ref.at[slot], sem.at[slot, 1]))

    @pl.when(i == 0)
    def _():
        for c in weight_copies(be_ref[0], 0):
            c.start()

    @pl.when(first_ref[i] == 1)
    def _():
        slot = slot_ref[i]
        for c in weight_copies(be_ref[i], slot):
            c.wait()

        @pl.when(next_ref[i] >= 0)
        def _():
            for c in weight_copies(next_ref[i], 1 - slot):
                c.start()

        wgu_b_ref[...] = wgu_f_ref[slot].astype(BF16)
        wd_b_ref[...] = wd_f_ref[slot].astype(BF16)

    for r0 in (0, sub):
        rows = slice(r0, r0 + sub)

        @pl.when(valid > r0)
        def _():
            lo, hi = _unpack_bf16_pair(x_ref[rows, :])
            x = jnp.concatenate([lo.astype(BF16), hi.astype(BF16)], axis=1)
            gu = _dot(x, wgu_b_ref[...])
            act = (_silu(gu[:, 0:D_EXPERT]) * gu[:, D_EXPERT:]).astype(BF16)
            y = _dot(act, wd_b_ref[...])
            o_ref[rows, :] = _pack_bf16_pair(y[:, 0:D_MODEL // 2], y[:, D_MODEL // 2:])

        @pl.when(valid <= r0)
        def _():
            o_ref[rows, :] = jnp.zeros((sub, o_ref.shape[1]), o_ref.dtype)


def _experts(block_e, nused, valid, first, slot, next_e, xs, w_gate_up_l, w_down_l, bm):
    nrows, dh = xs.shape
    d = 2 * dh
    nblk = nrows // bm
    grid_spec = pltpu.PrefetchScalarGridSpec(
        num_scalar_prefetch=6,
        grid=(nblk,),
        in_specs=[
            pl.BlockSpec((bm, dh), lambda i, be, nu, *_: (jnp.minimum(i, nu[0] - 1), 0)),
            pl.BlockSpec(memory_space=pl.ANY),
            pl.BlockSpec(memory_space=pl.ANY),
        ],
        out_specs=pl.BlockSpec((bm, dh), lambda i, *_: (i, 0)),
        scratch_shapes=[
            pltpu.VMEM((2, d, 2 * D_EXPERT), w_gate_up_l.dtype),
            pltpu.VMEM((2, D_EXPERT, d), w_down_l.dtype),
            pltpu.VMEM((d, 2 * D_EXPERT), BF16),
            pltpu.VMEM((D_EXPERT, d), BF16),
            pltpu.SemaphoreType.DMA((2, 2)),
        ],
    )
    return pl.pallas_call(
        functools.partial(_experts_kernel, bm=bm),
        grid_spec=grid_spec,
        out_shape=jax.ShapeDtypeStruct((nrows, dh), U32),
        compiler_params=_params(("arbitrary",)),
        name="moe_experts",
    )(block_e, nused, valid, first, slot, next_e, xs, w_gate_up_l, w_down_l)


def _combine_kernel(chunk_ref, nch_ref, pstart_ref, pos_ref, wl_ref, xs_ref, ga2_ref, gf_ref,
                    yb_hbm, oa_ref, ob_ref, gbuf_ref, sem, *, tm, nta):
    i = pl.program_id(0)
    nt = pl.num_programs(0)
    cur = lax.rem(i, 2)
    refs = (chunk_ref, nch_ref, pstart_ref)

    def make_copy(buf):
        def build(tile_slab, buffer_slab, slabs):
            return pltpu.make_async_copy(yb_hbm.at[pl.ds(buffer_slab, slabs)],
                                         gbuf_ref.at[buf, pl.ds(tile_slab, slabs)], sem.at[buf])
        return build

    @pl.when(i == 0)
    def _():
        gbuf_ref[...] = jnp.zeros(gbuf_ref.shape, gbuf_ref.dtype)
        _start_chunks(0, *refs, make_copy(0))

    @pl.when(i + 1 < nt)
    def _():
        _start_chunks(i + 1, *refs, make_copy(1 - cur))

    _wait_chunks(i, nch_ref, make_copy(cur))

    lo, hi = _unpack_bf16_pair(gbuf_ref[cur].reshape(GROUPED_ROWS, gbuf_ref.shape[-1]))
    g = jnp.concatenate([lo.astype(BF16), hi.astype(BF16)], axis=1)
    pos16 = pos_ref[...].astype(jnp.int16)
    wl = wl_ref[...].astype(BF16)
    liota = lax.broadcasted_iota(I32, (tm, GROUPED_ROWS), 1).astype(jnp.int16)
    a = jnp.zeros((tm, GROUPED_ROWS), BF16)
    for r in range(TOP_K):
        a = jnp.where(liota == pos16[:, r:r + 1], jnp.broadcast_to(wl[:, r:r + 1], a.shape), a)
    routed = _dot(a, g)
    x2 = xs_ref[...] + ga2_ref[...] * routed
    ms = jnp.mean(x2 * x2, axis=-1, keepdims=True)
    y = x2 * lax.rsqrt(ms + EPS) * gf_ref[...]

    @pl.when(i < nta)
    def _():
        oa_ref[...] = y

    @pl.when(i >= nta)
    def _():
        ob_ref[...] = y


def _combine(chunks, nch, pstart, pos_t, wl_t, xs_base, ga2, gfin, yb, tm, nta):
    n, d = xs_base.shape
    ntb = n // tm - nta
    ga2_spec = pl.BlockSpec((None, tm, d), lambda i, *_: (jnp.minimum(i // nta, 1), 0, 0))
    grid_spec = pltpu.PrefetchScalarGridSpec(
        num_scalar_prefetch=3,
        grid=(n // tm,),
        in_specs=[
            pl.BlockSpec((tm, TOP_K), lambda i, *_: (i, 0)),
            pl.BlockSpec((tm, TOP_K), lambda i, *_: (i, 0)),
            pl.BlockSpec((tm, d), lambda i, *_: (i, 0)),
            ga2_spec,
            pl.BlockSpec((1, d), lambda i, *_: (0, 0)),
            pl.BlockSpec(memory_space=pl.ANY),
        ],
        out_specs=[
            pl.BlockSpec((tm, d), lambda i, *_: (jnp.minimum(i, nta - 1), 0)),
            pl.BlockSpec((tm, d), lambda i, *_: (jnp.maximum(i - nta, 0), 0)),
        ],
        scratch_shapes=[
            pltpu.VMEM((2, GROUPED_ROWS // ROW_GROUP, ROW_GROUP, d // 2), U32),
            pltpu.SemaphoreType.DMA((2,)),
        ],
    )
    return pl.pallas_call(
        functools.partial(_combine_kernel, tm=tm, nta=nta),
        grid_spec=grid_spec,
        out_shape=[jax.ShapeDtypeStruct((nta * tm, d), F32), jax.ShapeDtypeStruct((ntb * tm, d), F32)],
        compiler_params=_params(("arbitrary",)),
        name="moe_combine",
    )(chunks, nch, pstart, pos_t, wl_t, xs_base, ga2, gfin, yb)


def _moe_and_final(src_a, src_b, mods, wts, tm, bm):
    nta = src_a[0].shape[0] // tm
    ga1, sh2, sc2, ga2 = mods
    (w_out_b, g_ffn, wsgu_b, wsd_b, wr_t, rb, w_gate_up_l, w_down_l, g_final) = wts
    xs_base, h2, pos, wl, chunks, nch, cnt = _post(
        src_a, src_b, w_out_b, ga1, g_ffn, sc2, sh2, ga2, wsgu_b, wsd_b, wr_t, rb, tm)
    nt = xs_base.shape[0] // tm
    counts = cnt[:, 0].astype(I32)
    padded = (counts + bm - 1) // bm * bm
    pend = jnp.cumsum(padded)
    pstart = pend - padded
    nblk = -(-(nt * GROUPED_ROWS) // bm) + N_EXPERTS
    nused = (pend[-1] // bm).astype(I32)
    blk_row = jnp.minimum(jnp.arange(nblk, dtype=I32), nused - 1) * bm
    be = jnp.sum((pend[None, :] <= blk_row[:, None]).astype(I32), axis=1)
    chunks = chunks.reshape(-1)
    nch = nch[:, 0, 0]
    pslab = pstart // ROW_GROUP
    xs = _dispatch(chunks, nch, pslab, pend // ROW_GROUP, pos, h2, nblk * bm, tm, bm)
    xs = xs.reshape(nblk * bm, xs.shape[-1])
    region_end = jnp.sum(jnp.where(be[:, None] == jnp.arange(N_EXPERTS, dtype=I32)[None, :],
                                   (pstart + counts)[None, :], 0), axis=1)
    valid = jnp.clip(region_end - jnp.arange(nblk, dtype=I32) * bm, 0, bm)
    blk = jnp.arange(nblk, dtype=I32)
    first = ((blk < nused) & ((blk == 0) | (be != jnp.roll(be, 1)))).astype(I32)
    slot = (jnp.cumsum(first) - 1) % 2
    eids = jnp.arange(N_EXPERTS, dtype=I32)
    later_nonempty = (eids[None, :] > eids[:, None]) & (counts[None, :] > 0)
    next_tab = jnp.min(jnp.where(later_nonempty, eids[None, :], N_EXPERTS), axis=1)
    next_tab = jnp.where(next_tab == N_EXPERTS, -1, next_tab)
    next_e = jnp.sum(jnp.where(be[:, None] == eids[None, :], next_tab[None, :], 0), axis=1)
    yb = _experts(be, nused.reshape(1), valid, first, slot.astype(I32), next_e.astype(I32), xs,
                  w_gate_up_l, w_down_l, bm)
    yb = yb.reshape(nblk * bm // ROW_GROUP, ROW_GROUP, yb.shape[-1])
    return _combine(chunks, nch, pslab, pos.T, wl.T, xs_base, ga2, g_final, yb, tm, nta)


def _expand(mod, reps):
    if mod.shape[0] == 1:
        return mod
    return jnp.repeat(mod, reps, axis=0)


def kernel(x_prompt, x_sample, cache_k, cache_v, state_hgrn, c_prompt, c_sample, w_ada, b_ada,
           norm_mix, norm_ffn, norm_final, w_in, w_out, hg_lb_logits, hg_norm, da_lambda, da_norm,
           rel_bias_table, w_router, router_bias, w_gate_up, w_down, ws_gate_up, ws_down):
    depth = w_in.shape[0]
    assert depth == 1 and hg_lb_logits.shape[0] == 2
    bp, tp, d = x_prompt.shape
    bs, ts, _ = x_sample.shape
    assert bp == 1
    past = cache_k.shape[2]
    l = 0

    rows = -(-(bp + bs) // 8) * 8
    c_all = jnp.zeros((rows, d), F32).at[:bp].set(c_prompt).at[bp:bp + bs].set(c_sample)
    mod = _adaln(c_all, w_ada[l], b_ada[l])
    mod_p = [mod[0:bp, j * d:(j + 1) * d] for j in range(6)]
    mod_s = [_expand(mod[bp:bp + bs, j * d:(j + 1) * d], ts) for j in range(6)]

    w_in_b = w_in[l].astype(BF16)
    w_out_b = w_out[l].astype(BF16)
    wsgu_b = ws_gate_up[l].astype(BF16)
    wsd_b = ws_down[l].astype(BF16)
    wr_t = w_router[l].T
    g_mix = norm_mix[l].reshape(1, d)
    g_ffn = norm_ffn[l].reshape(1, d)
    g_final = norm_final.reshape(1, d)
    moe_w = (w_out_b, g_ffn, wsgu_b, wsd_b, wr_t, router_bias[l], w_gate_up[l], w_down[l], g_final)

    lam = _lam(da_lambda[l])

    t_att = min(ATT_TILE, tp)
    kk = jnp.arange(t_att, dtype=I32)[:, None]
    qq = jnp.arange(t_att, dtype=I32)[None, :]
    idx_diag = jnp.where((kk // CHUNK) <= (qq // CHUNK), _rel_bucket(kk - qq), MASK_BUCKET)
    idx_prev = _rel_bucket(kk - qq - t_att)
    bias_p = _bias_tiles(rel_bias_table, jnp.stack([idx_diag, idx_prev]).astype(I32),
                         shift_bucket=N_BUCKETS // 2 - 1)
    pad = 128
    qpos = past + jnp.arange(ts, dtype=I32)[:, None]
    idx_sp = _rel_bucket(jnp.arange(past, dtype=I32)[None, :] - qpos)
    kn = jnp.arange(pad, dtype=I32)[None, :]
    idx_sn = jnp.where(kn < ts, _rel_bucket(past + kn - qpos), MASK_BUCKET)
    bias_sp = _bias_tiles(rel_bias_table, idx_sp[None].astype(I32), shift_bucket=None)
    bias_sn = _bias_tiles(rel_bias_table, idx_sn[None].astype(I32), shift_bucket=None)

    xp = x_prompt.reshape(bp * tp, d)
    sh1, sc1, ga1, sh2, sc2, ga2 = mod_p
    assert ATT_TILE == INPROJ_TILE
    zh, qt, kf, vf, kb, vt = _inproj(xp, g_mix, sc1, sh1, w_in_b, t_att, True)
    s_zero = jnp.zeros((bp, HG_HEADS, HG_DIM, HG_DIM), F32)
    ohg_p, sp_new = _hgrn(zh, s_zero, hg_lb_logits, hg_norm[l], bp, tp, min(HGRN_CHUNK, tp))
    oda_p = _attn_prompt(kb, qt, vt, bias_p, lam, da_norm[l], t_att)
    src_p = (xp, ohg_p, oda_p)
    mods_p = (ga1, sh2, sc2, ga2)
    k_prompt = kf.reshape(1, bp, tp, DA_HEADS, 2 * DA_QKDIM)
    v_prompt = vf.reshape(1, bp, tp, DA_HEADS, DA_VDIM)

    ns = bs * ts
    xs_ = x_sample.reshape(ns, d)
    sh1, sc1, ga1, sh2, sc2, ga2 = mod_s
    zh, qs, kf, vf, kb, vb = _inproj(xs_, g_mix, sc1, sh1, w_in_b, ns, False)
    ohg_s, ss_new = _hgrn(zh, state_hgrn[l], hg_lb_logits, hg_norm[l], bs, ts, ts)
    oda_s = _attn_step(qs, cache_k[l], cache_v[l], kb, vb, bias_sp, bias_sn, lam, da_norm[l], bs, ts)
    assert ns == POST_TILE
    mods = tuple(jnp.stack([jnp.broadcast_to(mp, (POST_TILE, d)), ms_])
                 for mp, ms_ in zip(mods_p, (ga1, sh2, sc2, ga2)))
    y_p, y_s = _moe_and_final(src_p, (xs_, ohg_s, oda_s), mods, moe_w, POST_TILE, MOE_BLOCK_ROWS)
    k_sample = kf.reshape(1, bs, ts, DA_HEADS, 2 * DA_QKDIM)
    v_sample = vf.reshape(1, bs, ts, DA_HEADS, DA_VDIM)

    return (y_p.reshape(bp, tp, d), y_s.reshape(bs, ts, d), k_prompt, v_prompt, sp_new[None],
            k_sample, v_sample, ss_new[None].astype(x_sample.dtype))
```

```python
import functools
import math

import numpy as np
import jax
import jax.numpy as jnp
from jax import lax
from jax.experimental import pallas as pl
from jax.experimental.pallas import tpu as pltpu

F32 = jnp.float32
BF16 = jnp.bfloat16
I32 = jnp.int32
U32 = jnp.uint32
HIGHEST = lax.Precision.HIGHEST

D_MODEL = 1024
CHUNK = 64
HG_HEADS = 4
HG_DIM = 128
HG_WIDTH = HG_HEADS * HG_DIM
DA_HEADS = 4
DA_VDIM = 128
DA_QKDIM = 64
DA_WIDTH = DA_HEADS * DA_VDIM
N_BUCKETS = 32
MAX_DIST = 128
N_EXPERTS = 64
TOP_K = 8
N_GROUPS = 8
GROUP_SIZE = N_EXPERTS // N_GROUPS
TOP_GROUPS = 4
D_EXPERT = 256
ROUTE_SCALE = 2.5
EPS = 1e-6
LAM_INIT = 0.8 - 0.6 * math.exp(-0.3 * 0)

LOG2E = math.log2(math.e)
HI_MASK = np.uint32(0xFFFF0000)
NEG_BIG = -1e30
MASK_BUCKET = N_BUCKETS
V7X_VMEM_LIMIT = 48 * 1024 * 1024

ATT_TILE = 512
VT_ROWS = DA_VDIM + 16
HGRN_CHUNK = 256
INPROJ_TILE = 512
POST_TILE = 256
MOE_BLOCK_ROWS = 1024
ROW_GROUP = 8
GROUPED_ROWS = -(-(POST_TILE * TOP_K + N_EXPERTS * (ROW_GROUP - 1)) // 256) * 256
CHUNK_SLOTS = -(-(GROUPED_ROWS // ROW_GROUP) // 128) * 128
CHUNK_EXPERT_SHIFT = 24
CHUNK_UNROLL = 4
WAIT_CHUNKS = 16
ZERO_SLABS = 16


def _sigmoid(x):
    return 1.0 / (1.0 + jnp.exp(-x))


def _silu(x):
    return x * _sigmoid(x)


def _dot(a, b, **kw):
    return jnp.dot(a, b, preferred_element_type=F32, **kw)


def _dot_nt(a, b, **kw):
    return lax.dot_general(a, b, (((1,), (1,)), ((), ())), preferred_element_type=F32, **kw)


def _dot_tn(a, b, **kw):
    return lax.dot_general(a, b, (((0,), (0,)), ((), ())), preferred_element_type=F32, **kw)


def _pack_bf16_pair(lo, hi):
    lo_bits = lax.bitcast_convert_type(lo.astype(BF16).astype(F32), U32)
    hi_bits = lax.bitcast_convert_type(hi.astype(BF16).astype(F32), U32)
    return (lo_bits >> 16) | (hi_bits & HI_MASK)


def _pack_exact_bf16_pair(lo, hi):
    return (lax.bitcast_convert_type(lo, U32) >> 16) | (lax.bitcast_convert_type(hi, U32) & HI_MASK)


def _unpack_bf16_pair(w):
    lo = lax.bitcast_convert_type(w << 16, F32)
    hi = lax.bitcast_convert_type(w & HI_MASK, F32)
    return lo, hi


def _params(sem, vmem=V7X_VMEM_LIMIT, flags=None):
    return pltpu.CompilerParams(dimension_semantics=sem, vmem_limit_bytes=vmem, flags=flags)


def _adaln_kernel(c_ref, w_ref, b_ref, o_ref):
    s = _silu(c_ref[...])
    o_ref[...] = _dot(s, w_ref[...], precision=HIGHEST) + b_ref[...]


def _adaln(c_all, w_ada, b_ada):
    rows, d = c_all.shape
    cols = w_ada.shape[1]
    blk = 1024
    return pl.pallas_call(
        _adaln_kernel,
        grid=(cols // blk,),
        in_specs=[
            pl.BlockSpec((rows, d), lambda j: (0, 0)),
            pl.BlockSpec((d, blk), lambda j: (0, j)),
            pl.BlockSpec((1, blk), lambda j: (0, j)),
        ],
        out_specs=pl.BlockSpec((rows, blk), lambda j: (0, j)),
        out_shape=jax.ShapeDtypeStruct((rows, cols), F32),
        compiler_params=_params(("parallel",)),
        name="adaln",
    )(c_all, w_ada, b_ada.reshape(1, cols))


def _lam_kernel(l_ref, o_ref):
    l = l_ref[...].astype(F32)
    a = jnp.sum(l[0:1] * l[1:2], axis=-1, keepdims=True)
    b = jnp.sum(l[2:3] * l[3:4], axis=-1, keepdims=True)
    lam = jnp.exp(a) - jnp.exp(b) + LAM_INIT
    o_ref[...] = jnp.broadcast_to(lam, o_ref.shape)


def _lam(da_lambda_l):
    return pl.pallas_call(
        _lam_kernel,
        out_shape=jax.ShapeDtypeStruct((8, 128), F32),
        name="lam",
    )(da_lambda_l)


def _rel_bucket(rel):
    nb = N_BUCKETS // 2
    max_exact = nb // 2
    side = jnp.where(rel > 0, nb, 0)
    n = jnp.abs(rel)
    large = max_exact + (jnp.log(jnp.maximum(n, 1).astype(F32) / max_exact)
                         / math.log(MAX_DIST / max_exact) * (nb - max_exact)).astype(I32)
    large = jnp.minimum(large, nb - 1)
    return side + jnp.where(n < max_exact, n, large)


def _bias_kernel(tab_ref, idx_ref, o_ref, *, shift_bucket):
    h = pl.program_id(0)
    idx = idx_ref[...]
    shift = tab_ref[shift_bucket, h] if shift_bucket is not None else 0.0
    acc = jnp.zeros(idx.shape, F32)
    for j in range(N_BUCKETS):
        acc = jnp.where(idx == j, (tab_ref[j, h] - shift) * LOG2E, acc)
    o_ref[...] = jnp.where(idx == MASK_BUCKET, NEG_BIG, acc)


def _bias_tiles(table, idx, *, shift_bucket):
    k, r, c = idx.shape
    return pl.pallas_call(
        functools.partial(_bias_kernel, shift_bucket=shift_bucket),
        grid=(DA_HEADS, k),
        in_specs=[
            pl.BlockSpec(memory_space=pltpu.SMEM),
            pl.BlockSpec((None, r, c), lambda h, d: (d, 0, 0)),
        ],
        out_specs=pl.BlockSpec((None, None, r, c), lambda h, d: (h, d, 0, 0)),
        out_shape=jax.ShapeDtypeStruct((DA_HEADS, k, r, c), F32),
        compiler_params=_params(("parallel", "parallel")),
        name="rel_bias",
    )(table, idx)


def _inproj_kernel(x_ref, g_ref, sc_ref, sh_ref, w_ref,
                   zh_ref, q_ref, k_ref, v_ref, kb_ref, vb_ref, *, transposed):
    x = x_ref[...]
    ms = jnp.mean(x * x, axis=-1, keepdims=True)
    h = x * lax.rsqrt(ms + EPS) * g_ref[...]
    h = h * (1.0 + sc_ref[...]) + sh_ref[...]
    hb = h.astype(BF16)
    c0 = 4 * HG_WIDTH
    zh_ref[...] = _dot(hb, w_ref[:, 0:c0])
    zq = _dot(hb, w_ref[:, c0:c0 + DA_WIDTH]) * (DA_QKDIM ** -0.5 * LOG2E)
    zk = _dot(hb, w_ref[:, c0 + DA_WIDTH:c0 + 2 * DA_WIDTH])
    for hd in range(DA_HEADS):
        k_ref[:, hd, :] = zk[:, hd * DA_VDIM:(hd + 1) * DA_VDIM]
    kb_ref[...] = zk.astype(BF16)
    zv = _dot(hb, w_ref[:, c0 + 2 * DA_WIDTH:c0 + 3 * DA_WIDTH])
    for hd in range(DA_HEADS):
        v_ref[:, hd, :] = zv[:, hd * DA_VDIM:(hd + 1) * DA_VDIM]
    if transposed:
        q_ref[...] = zq.T.astype(BF16)
        vb_ref[:, 0:DA_VDIM, :] = zv.T.astype(BF16).reshape(DA_HEADS, DA_VDIM, zv.shape[0])
        vb_ref[:, DA_VDIM:, :] = jnp.ones((DA_HEADS, VT_ROWS - DA_VDIM, zv.shape[0]), BF16)
    else:
        q_ref[...] = zq.astype(BF16)
        vb_ref[...] = zv.astype(BF16)


def _mod_spec(mod, tm):
    if mod.shape[0] == 1:
        return pl.BlockSpec((1, mod.shape[1]), lambda i: (0, 0))
    return pl.BlockSpec((tm, mod.shape[1]), lambda i: (i, 0))


def _inproj(x, g, sc, sh, w_in_b, tm, transposed):
    n, d = x.shape
    cols = w_in_b.shape[1]
    row = lambda i: (i, 0)
    if transposed:
        q_spec = pl.BlockSpec((DA_WIDTH, tm), lambda i: (0, i))
        q_shape = jax.ShapeDtypeStruct((DA_WIDTH, n), BF16)
        vb_spec = pl.BlockSpec((DA_HEADS, None, VT_ROWS, tm), lambda i: (0, i, 0, 0))
        vb_shape = jax.ShapeDtypeStruct((DA_HEADS, n // tm, VT_ROWS, tm), BF16)
    else:
        q_spec = vb_spec = pl.BlockSpec((tm, DA_WIDTH), row)
        q_shape = vb_shape = jax.ShapeDtypeStruct((n, DA_WIDTH), BF16)
    return pl.pallas_call(
        functools.partial(_inproj_kernel, transposed=transposed),
        grid=(n // tm,),
        in_specs=[
            pl.BlockSpec((tm, d), row),
            pl.BlockSpec((1, d), lambda i: (0, 0)),
            _mod_spec(sc, tm),
            _mod_spec(sh, tm),
            pl.BlockSpec((d, cols), lambda i: (0, 0)),
        ],
        out_specs=[
            pl.BlockSpec((tm, 4 * HG_WIDTH), row),
            q_spec,
            pl.BlockSpec((tm, DA_HEADS, DA_VDIM), lambda i: (i, 0, 0)),
            pl.BlockSpec((tm, DA_HEADS, DA_VDIM), lambda i: (i, 0, 0)),
            pl.BlockSpec((tm, DA_WIDTH), row),
            vb_spec,
        ],
        out_shape=[
            jax.ShapeDtypeStruct((n, 4 * HG_WIDTH), F32),
            q_shape,
            jax.ShapeDtypeStruct((n, DA_HEADS, DA_VDIM), F32),
            jax.ShapeDtypeStruct((n, DA_HEADS, DA_VDIM), F32),
            jax.ShapeDtypeStruct((n, DA_WIDTH), BF16),
            vb_shape,
        ],
        compiler_params=_params(("parallel",)),
        name="inproj",
    )(x, g, sc, sh, w_in_b)


def _hgrn_consts(c):
    levels = int(round(math.log2(c)))
    assert 1 << levels == c and levels >= 3
    t = np.arange(c)[:, None]
    r = np.arange(c)[None, :]
    tri = (r <= t).astype(np.float32)
    x = np.maximum(t ^ r, 1)
    lv = np.where(t == r, -1, np.where(t > r, np.floor(np.log2(x)).astype(np.int64), -2))
    return jnp.asarray(tri, dtype=BF16), jnp.asarray(lv, dtype=I32), levels


def _hgrn_kernel(zh_ref, s0_ref, lbl_ref, gain_ref, mall_ref, lv_ref,
                 o_ref, sout_ref, st_ref, b_ref, *, c, levels):
    ci = pl.program_id(1)

    @pl.when(ci == 0)
    def _():
        for h in range(HG_HEADS):
            st_ref[h] = s0_ref[h].astype(F32).T

    lbl = lbl_ref[...].astype(F32)
    mx = jnp.maximum(lbl[0:1], lbl[1:2])
    e0 = jnp.exp(lbl[0:1] - mx)
    e1 = jnp.exp(lbl[1:2] - mx)
    lb = e0 / (e0 + e1)

    xq = zh_ref[:, 0:HG_WIDTH]
    xf = zh_ref[:, HG_WIDTH:2 * HG_WIDTH]
    q = _silu(xq)
    y = lb + (1.0 - lb) * _sigmoid(xf)
    logf = jnp.log(y)
    kk = 1.0 - y

    l1 = logf.astype(BF16)
    r1 = logf - l1.astype(F32)
    l2 = r1.astype(BF16)
    l3 = (r1 - l2.astype(F32)).astype(BF16)
    tri = mall_ref[...]
    b = _dot(tri, l1) + _dot(tri, l2) + _dot(tri, l3)
    b_ref[...] = b
    trow = lax.broadcasted_iota(I32, b.shape, 0)

    def level_exponent(l):
        m = 1 << l
        later = (trow & m) != 0
        if l == 0:
            return jnp.where(later, logf, 0.0)
        if l == 1:
            below = pltpu.roll(logf, 1, 0)
            above = pltpu.roll(logf, c - 1, 0)
            low = (trow & 1) != 0
            return jnp.where(later, jnp.where(low, logf + below, logf), jnp.where(low, 0.0, above))
        mid = jnp.concatenate(
            [jnp.broadcast_to(b_ref[k * 2 * m + m - 1:k * 2 * m + m, :], (2 * m, b.shape[1]))
             for k in range(c // (2 * m))], axis=0)
        return jnp.where(later, b - mid, mid - b)

    factors = [jnp.exp(level_exponent(l)) for l in range(levels)]
    lv = lv_ref[...]
    gain = gain_ref[...].astype(F32)
    for h in range(HG_HEADS):
        sl = slice(h * HG_DIM, (h + 1) * HG_DIM)
        qh = q[:, sl]
        kh = kk[:, sl]
        ih = zh_ref[:, 2 * HG_WIDTH + h * HG_DIM:2 * HG_WIDTH + (h + 1) * HG_DIM]
        gh = zh_ref[:, 3 * HG_WIDTH + h * HG_DIM:3 * HG_WIDTH + (h + 1) * HG_DIM]
        bh = b[:, sl]
        ihb = ih.astype(BF16)
        a = jnp.where(lv == -1, _dot_nt(qh.astype(BF16), kh.astype(BF16)), 0.0)
        for l in range(levels):
            f = factors[l][:, sl]
            p = _dot_nt((qh * f).astype(BF16), (kh * f).astype(BF16))
            a = jnp.where(lv == l, p, a)
        st = st_ref[h]
        o = _dot(a.astype(BF16), ihb) + _dot_nt((qh * jnp.exp(bh)).astype(BF16), st.astype(BF16))
        bl = bh[c - 1:c, :]
        kd = (kh * jnp.exp(bl - bh)).astype(BF16)
        st_ref[h] = st * jnp.exp(bl) + _dot_tn(ihb, kd)
        ms = jnp.mean(o * o, axis=-1, keepdims=True)
        on = o * lax.rsqrt(ms + EPS) * gain
        o_ref[:, sl] = (on * _silu(gh)).astype(o_ref.dtype)

    @pl.when(ci == pl.num_programs(1) - 1)
    def _():
        for h in range(HG_HEADS):
            sout_ref[h] = st_ref[h].T.astype(sout_ref.dtype)


def _hgrn(zh, s0, lb_logits, gain, batch, seq, c):
    mall, lv, levels = _hgrn_consts(c)
    nc = seq // c
    return pl.pallas_call(
        functools.partial(_hgrn_kernel, c=c, levels=levels),
        grid=(batch, nc),
        in_specs=[
            pl.BlockSpec((c, 4 * HG_WIDTH), lambda b, i: (b * nc + i, 0)),
            pl.BlockSpec((None, HG_HEADS, HG_DIM, HG_DIM), lambda b, i: (b, 0, 0, 0)),
            pl.BlockSpec(lb_logits.shape, lambda b, i: (0, 0)),
            pl.BlockSpec((1, HG_DIM), lambda b, i: (0, 0)),
            pl.BlockSpec(mall.shape, lambda b, i: (0, 0)),
            pl.BlockSpec(lv.shape, lambda b, i: (0, 0)),
        ],
        out_specs=[
            pl.BlockSpec((c, HG_WIDTH), lambda b, i: (b * nc + i, 0)),
            pl.BlockSpec((None, HG_HEADS, HG_DIM, HG_DIM), lambda b, i: (b, 0, 0, 0)),
        ],
        out_shape=[
            jax.ShapeDtypeStruct((batch * seq, HG_WIDTH), BF16),
            jax.ShapeDtypeStruct((batch, HG_HEADS, HG_DIM, HG_DIM), F32),
        ],
        scratch_shapes=[pltpu.VMEM((HG_HEADS, HG_DIM, HG_DIM), F32),
                        pltpu.VMEM((c, HG_WIDTH), F32)],
        compiler_params=_params(("parallel", "arbitrary")),
        name="hgrn2",
    )(zh, s0, lb_logits, gain.reshape(1, HG_DIM), mall, lv)


def _attn_kernel(k_ref, qt_ref, vt_ref, bias_ref, lam_ref, gain_ref,
                 o_ref, qz_ref, m_ref, acc_ref, s_ref, smax_ref, *, t):
    i = pl.program_id(1)
    qt = qt_ref[...]
    row = lax.broadcasted_iota(I32, qt.shape, 0)
    zero = jnp.zeros_like(qt)
    qz_ref[:, 0:t] = jnp.where(row < DA_QKDIM, qt, zero)
    qz_ref[:, t:2 * t] = jnp.where(row >= DA_QKDIM, qt, zero)
    m_ref[...] = jnp.full(m_ref.shape, NEG_BIG, F32)
    acc_ref[...] = jnp.zeros(acc_ref.shape, F32)

    def scores(j, buf):
        kt = k_ref[pl.ds(pl.multiple_of(j * t, t), t), :]
        s = _dot(kt, qz_ref[...])
        s_ref[buf] = s
        smax_ref[buf] = jnp.max(s, axis=0, keepdims=True)

    def consume(j, buf, bias_idx):
        s = s_ref[buf]
        if bias_idx is not None:
            b = bias_ref[bias_idx]
            s = jnp.concatenate([s[:, 0:t] + b, s[:, t:2 * t] + b], axis=1)
            s_max = jnp.max(s, axis=0, keepdims=True)
        else:
            s_max = smax_ref[buf]
        m_prev = m_ref[...]
        m_new = jnp.maximum(m_prev, s_max)
        alpha = jnp.exp2(m_prev - m_new)
        pr = jnp.exp2(s - m_new).astype(BF16)
        acc_ref[...] = alpha * acc_ref[...] + _dot(vt_ref[j], pr)
        m_ref[...] = m_new

    n_far = jnp.maximum(i - 1, 0)

    @pl.when(i >= 1)
    def _():
        scores(i - 1, 0)
        scores(i, 1)
        consume(i - 1, 0, 1)
        scores(0, 0)
        consume(i, 1, 0)

    @pl.when(i == 0)
    def _():
        scores(i, 1)
        consume(i, 1, 0)

    def far_tiles(j, count):
        for u in range(count):
            nxt = j + u + 1
            if u == count - 1:
                nxt = jnp.minimum(nxt, n_far - 1)
            scores(nxt, (u + 1) % 2)
            consume(j + u, u % 2, None)

    def far_quad(p, carry):
        far_tiles(4 * p, 4)
        return carry

    def far_pair(p, carry):
        far_tiles(4 * quads + 2 * p, 2)
        return carry

    quads = n_far // 4
    lax.fori_loop(0, quads, far_quad, 0)
    lax.fori_loop(0, (n_far - 4 * quads) // 2, far_pair, 0)

    @pl.when(lax.rem(n_far, 2) == 1)
    def _():
        consume(n_far - 1, 0, None)

    lam = lam_ref[0:1, 0:1]
    acc = acc_ref[0:DA_VDIM, :]
    l = acc_ref[DA_VDIM:DA_VDIM + 1, :]
    o = acc[:, 0:t] / l[:, 0:t] - lam * (acc[:, t:2 * t] / l[:, t:2 * t])
    ms = jnp.mean(o * o, axis=0, keepdims=True)
    on = o * lax.rsqrt(ms + EPS) * gain_ref[...].astype(F32) * (1.0 - LAM_INIT)
    o_ref[...] = on.T.astype(o_ref.dtype)


def _attn_prompt(kb, qt, vt, bias, lam, gain, t):
    n = kb.shape[0]
    nt = n // t
    return pl.pallas_call(
        functools.partial(_attn_kernel, t=t),
        grid=(DA_HEADS, nt),
        in_specs=[
            pl.BlockSpec((n, DA_VDIM), lambda h, i: (0, h)),
            pl.BlockSpec((DA_VDIM, t), lambda h, i: (h, i)),
            pl.BlockSpec((None, nt, VT_ROWS, t), lambda h, i: (h, 0, 0, 0)),
            pl.BlockSpec((None, 2, t, t), lambda h, i: (h, 0, 0, 0)),
            pl.BlockSpec((8, 128), lambda h, i: (0, 0)),
            pl.BlockSpec((DA_VDIM, 1), lambda h, i: (0, 0)),
        ],
        out_specs=pl.BlockSpec((t, DA_VDIM), lambda h, i: (i, h)),
        out_shape=jax.ShapeDtypeStruct((n, DA_WIDTH), BF16),
        scratch_shapes=[
            pltpu.VMEM((DA_VDIM, 2 * t), BF16),
            pltpu.VMEM((1, 2 * t), F32),
            pltpu.VMEM((VT_ROWS, 2 * t), F32),
            pltpu.VMEM((2, t, 2 * t), F32),
            pltpu.VMEM((2, 1, 2 * t), F32),
        ],
        compiler_params=_params(("parallel", "parallel")),
        name="diff_attn_prompt",
    )(kb, qt, vt, bias, lam, gain.reshape(DA_VDIM, 1))


def _attn_step_kernel(q_ref, kp_ref, vp_ref, kn_ref, vn_ref, bp_ref, bn_ref, lam_ref, gain_ref,
                      o_ref, *, tq, pad):
    lam = lam_ref[0:1, 0:1]
    gain = gain_ref[...].astype(F32)
    zpad = jnp.zeros((pad - tq, DA_VDIM), BF16)
    for h in range(DA_HEADS):
        hs = slice(h * DA_VDIM, (h + 1) * DA_VDIM)
        q = q_ref[:, hs]
        lane = lax.broadcasted_iota(I32, q.shape, 1)
        zero = jnp.zeros_like(q)
        qz = jnp.concatenate([jnp.where(lane < DA_QKDIM, q, zero),
                              jnp.where(lane >= DA_QKDIM, q, zero)], axis=0)
        kp = kp_ref[:, h, :].astype(BF16)
        vp = vp_ref[:, h, :].astype(BF16)
        kn = jnp.concatenate([kn_ref[:, hs], zpad], axis=0)
        vn = jnp.concatenate([vn_ref[:, hs], zpad], axis=0)
        bp = bp_ref[h, 0]
        bn = bn_ref[h, 0]
        sp = _dot_nt(qz, kp) + jnp.concatenate([bp, bp], axis=0)
        sn = _dot_nt(qz, kn) + jnp.concatenate([bn, bn], axis=0)
        m = jnp.maximum(jnp.max(sp, axis=-1, keepdims=True), jnp.max(sn, axis=-1, keepdims=True))
        pp = jnp.exp2(sp - m)
        pn = jnp.exp2(sn - m)
        l = jnp.sum(pp, axis=-1, keepdims=True) + jnp.sum(pn, axis=-1, keepdims=True)
        acc = _dot(pp.astype(BF16), vp) + _dot(pn.astype(BF16), vn)
        on = acc / l
        o = on[0:tq] - lam * on[tq:2 * tq]
        ms = jnp.mean(o * o, axis=-1, keepdims=True)
        o = o * lax.rsqrt(ms + EPS) * gain * (1.0 - LAM_INIT)
        o_ref[:, hs] = o.astype(o_ref.dtype)


def _attn_step(qs, cache_k_l, cache_v_l, kb, vb, bias_p, bias_n, lam, gain, batch, tq):
    past = cache_k_l.shape[1]
    pad = bias_n.shape[-1]
    cache_spec = pl.BlockSpec((None, past, DA_HEADS, DA_VDIM), lambda b: (b, 0, 0, 0))
    row = pl.BlockSpec((tq, DA_WIDTH), lambda b: (b, 0))
    return pl.pallas_call(
        functools.partial(_attn_step_kernel, tq=tq, pad=pad),
        grid=(batch,),
        in_specs=[
            row, cache_spec, cache_spec, row, row,
            pl.BlockSpec(bias_p.shape, lambda b: (0, 0, 0, 0)),
            pl.BlockSpec(bias_n.shape, lambda b: (0, 0, 0, 0)),
            pl.BlockSpec((8, 128), lambda b: (0, 0)),
            pl.BlockSpec((1, DA_VDIM), lambda b: (0, 0)),
        ],
        out_specs=row,
        out_shape=jax.ShapeDtypeStruct((batch * tq, DA_WIDTH), BF16),
        compiler_params=_params(("parallel",)),
        name="diff_attn_step",
    )(qs, cache_k_l, cache_v_l, kb, vb, bias_p, bias_n, lam, gain.reshape(1, DA_VDIM))


def _post_kernel(xa_ref, ohga_ref, odaa_ref, xb_ref, ohgb_ref, odab_ref,
                 wout_ref, ga1_ref, g_ref, sc_ref, sh_ref, ga2_ref,
                 wsgu_ref, wsd_ref, wrt_ref, rb_ref, tri_ref, ltri_ref,
                 xs_ref, h2_ref, pos_ref, wl_ref, chunk_ref, nch_ref, cnt_ref, carry_ref, *, tm, nta):
    i = pl.program_id(0)

    @pl.when(i == 0)
    def _():
        carry_ref[...] = jnp.zeros(carry_ref.shape, F32)

    second = i >= nta
    x = jnp.where(second, xb_ref[...], xa_ref[...])
    ohg = jnp.where(second, ohgb_ref[...], ohga_ref[...])
    oda = jnp.where(second, odab_ref[...], odaa_ref[...])
    mix = _dot(ohg, wout_ref[0:HG_WIDTH, :]) + _dot(oda, wout_ref[HG_WIDTH:, :])
    x1 = x + ga1_ref[...] * mix
    ms = jnp.mean(x1 * x1, axis=-1, keepdims=True)
    h2 = x1 * lax.rsqrt(ms + EPS) * g_ref[...]
    h2 = h2 * (1.0 + sc_ref[...]) + sh_ref[...]
    h2b = h2.astype(BF16)
    h2_ref[...] = h2b
    gu = _dot(h2b, wsgu_ref[...])
    act = (_silu(gu[:, 0:D_EXPERT]) * gu[:, D_EXPERT:]).astype(BF16)
    xs_ref[...] = x1 + ga2_ref[...] * _dot(act, wsd_ref[...])

    logits = _dot_nt(wrt_ref[...], h2, precision=HIGHEST)
    score = _sigmoid(logits)
    sel = score + rb_ref[...]
    sub = lax.broadcasted_iota(I32, (GROUP_SIZE, tm), 0)
    gscore = []
    for g in range(N_GROUPS):
        v = sel[g * GROUP_SIZE:(g + 1) * GROUP_SIZE, :]
        m1 = jnp.max(v, axis=0, keepdims=True)
        i1 = jnp.min(jnp.where(v == m1, sub, GROUP_SIZE), axis=0, keepdims=True)
        m2 = jnp.max(jnp.where(sub == i1, -jnp.inf, v), axis=0, keepdims=True)
        gscore.append(m1 + m2)
    gsel = []
    for g in range(N_GROUPS):
        ahead = jnp.zeros((1, tm), F32)
        for g2 in range(N_GROUPS):
            if g2 == g:
                continue
            tie = 1.0 if g2 < g else 0.0
            ahead = ahead + jnp.where(gscore[g2] > gscore[g], 1.0,
                                      jnp.where(gscore[g2] == gscore[g], tie, 0.0))
        gsel.append(ahead < TOP_GROUPS)
    selm = jnp.concatenate(
        [jnp.where(gsel[g], sel[g * GROUP_SIZE:(g + 1) * GROUP_SIZE, :], -jnp.inf)
         for g in range(N_GROUPS)], axis=0)
    eio = lax.broadcasted_iota(I32, (N_EXPERTS, tm), 0)
    ahead = jnp.zeros((N_EXPERTS, tm), F32)
    for e2 in range(N_EXPERTS):
        row = selm[e2:e2 + 1, :]
        tie = jnp.where(eio > e2, 1.0, 0.0)
        ahead = ahead + jnp.where(row > selm, 1.0, jnp.where(row == selm, tie, 0.0))
    chosen = jnp.where(selm > -jnp.inf, jnp.where(ahead < TOP_K, 1.0, 0.0), 0.0)
    w = chosen * score
    wn = w / jnp.sum(w, axis=0, keepdims=True) * ROUTE_SCALE

    chb = chosen.astype(BF16)
    before = _dot(chb, tri_ref[...])
    tot = _dot(chb, jnp.ones((tm, 128), BF16))
    run = jnp.floor((tot + (ROW_GROUP - 1)) * (1.0 / ROW_GROUP)) * ROW_GROUP
    tile_base = _dot(ltri_ref[...], run.astype(BF16))
    carry = carry_ref[...]
    carry_ref[...] = carry + run
    cnt_ref[...] = carry + run
    pos = jnp.concatenate([tile_base] * (tm // 128), axis=1) + before

    widen = lambda v: jnp.concatenate([v] * (CHUNK_SLOTS // 128), axis=1)
    crow = lax.broadcasted_iota(I32, (N_EXPERTS, CHUNK_SLOTS), 1).astype(F32) * ROW_GROUP
    erow = lax.broadcasted_iota(I32, (N_EXPERTS, CHUNK_SLOTS), 0).astype(F32)
    owner = jnp.sum(jnp.where(widen(tile_base + run) <= crow, 1.0, 0.0), axis=0, keepdims=True)
    region_row = jnp.sum(jnp.where(owner == erow, widen(carry - tile_base), 0.0),
                         axis=0, keepdims=True) + crow[0:1]
    region_slab = (region_row * (1.0 / ROW_GROUP)).astype(I32)
    chunk_ref[...] = owner.astype(I32) * (1 << CHUNK_EXPERT_SHIFT) + region_slab
    nch_ref[...] = jnp.sum(run * (1.0 / ROW_GROUP), axis=0, keepdims=True).astype(I32)

    for r in range(TOP_K):
        pick = jnp.where(ahead == r, chosen, 0.0)
        pos_ref[r:r + 1, :] = jnp.sum(pick * pos, axis=0, keepdims=True).astype(I32)
        wl_ref[r:r + 1, :] = jnp.sum(pick * wn, axis=0, keepdims=True)


def _post(src_a, src_b, w_out_b, ga1, g, sc, sh, ga2, wsgu_b, wsd_b, wr_t, rb, tm):
    (xa, ohga, odaa), (xb, ohgb, odab) = src_a, src_b
    d = xa.shape[1]
    nta, ntb = xa.shape[0] // tm, xb.shape[0] // tm
    nt = nta + ntb
    n = nt * tm
    tri = jnp.asarray(np.triu(np.ones((tm, tm), np.float32), k=1), dtype=BF16)
    ltri = jnp.asarray(np.tril(np.ones((N_EXPERTS, N_EXPERTS), np.float32), k=-1), dtype=BF16)
    row = lambda i: (i, 0)
    row_a = lambda i: (jnp.minimum(i, nta - 1), 0)
    row_b = lambda i: (jnp.maximum(i - nta, 0), 0)
    col = lambda i: (0, i)
    full = lambda i: (0, 0)
    mod = pl.BlockSpec((None, tm, d), lambda i: (jnp.minimum(i // nta, 1), 0, 0))
    return pl.pallas_call(
        functools.partial(_post_kernel, tm=tm, nta=nta),
        grid=(nt,),
        in_specs=[
            pl.BlockSpec((tm, d), row_a),
            pl.BlockSpec((tm, HG_WIDTH), row_a),
            pl.BlockSpec((tm, DA_WIDTH), row_a),
            pl.BlockSpec((tm, d), row_b),
            pl.BlockSpec((tm, HG_WIDTH), row_b),
            pl.BlockSpec((tm, DA_WIDTH), row_b),
            pl.BlockSpec(w_out_b.shape, full),
            mod,
            pl.BlockSpec((1, d), full),
            mod,
            mod,
            mod,
            pl.BlockSpec(wsgu_b.shape, full),
            pl.BlockSpec(wsd_b.shape, full),
            pl.BlockSpec(wr_t.shape, full),
            pl.BlockSpec((N_EXPERTS, 1), full),
            pl.BlockSpec((tm, tm), full),
            pl.BlockSpec((N_EXPERTS, N_EXPERTS), full),
        ],
        out_specs=[
            pl.BlockSpec((tm, d), row),
            pl.BlockSpec((tm, d), row),
            pl.BlockSpec((TOP_K, tm), col),
            pl.BlockSpec((TOP_K, tm), col),
            pl.BlockSpec((None, 1, CHUNK_SLOTS), lambda i: (i, 0, 0)),
            pl.BlockSpec((None, 1, 128), lambda i: (i, 0, 0)),
            pl.BlockSpec((N_EXPERTS, 128), full),
        ],
        out_shape=[
            jax.ShapeDtypeStruct((n, d), F32),
            jax.ShapeDtypeStruct((n, d), BF16),
            jax.ShapeDtypeStruct((TOP_K, n), I32),
            jax.ShapeDtypeStruct((TOP_K, n), F32),
            jax.ShapeDtypeStruct((nt, 1, CHUNK_SLOTS), I32),
            jax.ShapeDtypeStruct((nt, 1, 128), I32),
            jax.ShapeDtypeStruct((N_EXPERTS, 128), F32),
        ],
        scratch_shapes=[pltpu.VMEM((N_EXPERTS, 128), F32)],
        compiler_params=_params(("arbitrary",)),
        name="post_mix_router",
    )(xa, ohga, odaa, xb, ohgb, odab, w_out_b, ga1, g, sc, sh, ga2, wsgu_b, wsd_b, wr_t,
      rb.reshape(N_EXPERTS, 1), tri, ltri)


def _start_chunks(tile, chunk_ref, nch_ref, pslab_ref, make_copy):
    n = nch_ref[tile]

    def start(c):
        word = chunk_ref[tile * CHUNK_SLOTS + c]
        expert = lax.shift_right_logical(word, CHUNK_EXPERT_SHIFT)
        region_slab = word & ((1 << CHUNK_EXPERT_SHIFT) - 1)
        make_copy(c, pslab_ref[expert] + region_slab, 1).start()

    def group(g, carry):
        for u in range(CHUNK_UNROLL):
            start(g * CHUNK_UNROLL + u)
        return carry

    def single(c, carry):
        start(c)
        return carry

    groups = n // CHUNK_UNROLL
    lax.fori_loop(0, groups, group, 0)
    lax.fori_loop(groups * CHUNK_UNROLL, n, single, 0)


def _wait_chunks(tile, nch_ref, make_copy):
    n = nch_ref[tile]
    many = n // WAIT_CHUNKS

    def wait_many(j, carry):
        make_copy(0, 0, WAIT_CHUNKS).wait()
        return carry

    def wait_one(j, carry):
        make_copy(0, 0, 1).wait()
        return carry

    lax.fori_loop(0, many, wait_many, 0)
    lax.fori_loop(many * WAIT_CHUNKS, n, wait_one, 0)


def _dispatch_kernel(chunk_ref, nch_ref, pstart_ref, pend_ref, rend_ref, pos_ref, h2_ref, xs_hbm,
                     cbuf_ref, zero_ref, zsem, sem, *, tm):
    i = pl.program_id(0)
    nt = pl.num_programs(0)
    dh = cbuf_ref.shape[-1]

    def zero_piece(slab):
        return pltpu.make_async_copy(zero_ref, xs_hbm.at[pl.ds(slab, ZERO_SLABS)], zsem)

    def zero_fill(op):
        def pieces(start, stop):
            first = (start // ZERO_SLABS) * ZERO_SLABS

            def piece(k, carry):
                op(zero_piece(first + k * ZERO_SLABS))
                return carry

            lax.fori_loop(0, (stop - first) // ZERO_SLABS, piece, 0)

        def region(e, carry):
            pieces(rend_ref[e], pend_ref[e])
            return carry

        lax.fori_loop(0, N_EXPERTS, region, 0)
        pieces(pend_ref[N_EXPERTS - 1], xs_hbm.shape[0])

    @pl.when(i == 0)
    def _():
        zero_ref[...] = jnp.zeros(zero_ref.shape, zero_ref.dtype)
        zero_fill(lambda c: c.start())
        zero_fill(lambda c: c.wait())

    pos = pos_ref[...]
    piota = lax.broadcasted_iota(I32, (GROUPED_ROWS, tm), 0).astype(jnp.int16)
    pos16 = pos.astype(jnp.int16)
    one = jnp.ones((GROUPED_ROWS, tm), BF16)
    perm = jnp.zeros((GROUPED_ROWS, tm), BF16)
    for r in range(TOP_K):
        perm = jnp.where(piota == pos16[r:r + 1, :], one, perm)
    cur = lax.rem(i, 2)
    grouped = _pack_exact_bf16_pair(_dot(perm, h2_ref[:, 0:dh]), _dot(perm, h2_ref[:, dh:]))
    cbuf_ref[cur] = grouped.reshape(cbuf_ref.shape[1:])

    def make_copy(buf):
        def build(tile_slab, buffer_slab, slabs):
            return pltpu.make_async_copy(cbuf_ref.at[buf, pl.ds(tile_slab, slabs)],
                                         xs_hbm.at[pl.ds(buffer_slab, slabs)], sem.at[buf])
        return build

    _start_chunks(i, chunk_ref, nch_ref, pstart_ref, make_copy(cur))

    @pl.when(i > 0)
    def _():
        _wait_chunks(i - 1, nch_ref, make_copy(1 - cur))

    @pl.when(i == nt - 1)
    def _():
        _wait_chunks(i, nch_ref, make_copy(cur))


def _dispatch(chunks, nch, pstart, pend, rend, pos, h2, nrows, tm):
    n, d = h2.shape
    grid_spec = pltpu.PrefetchScalarGridSpec(
        num_scalar_prefetch=5,
        grid=(n // tm,),
        in_specs=[
            pl.BlockSpec((TOP_K, tm), lambda i, *_: (0, i)),
            pl.BlockSpec((tm, d), lambda i, *_: (i, 0)),
        ],
        out_specs=pl.BlockSpec(memory_space=pl.ANY),
        scratch_shapes=[
            pltpu.VMEM((2, GROUPED_ROWS // ROW_GROUP, ROW_GROUP, d // 2), U32),
            pltpu.VMEM((ZERO_SLABS, ROW_GROUP, d // 2), U32),
            pltpu.SemaphoreType.DMA(()),
            pltpu.SemaphoreType.DMA((2,)),
        ],
    )
    return pl.pallas_call(
        functools.partial(_dispatch_kernel, tm=tm),
        grid_spec=grid_spec,
        out_shape=jax.ShapeDtypeStruct((nrows // ROW_GROUP, ROW_GROUP, d // 2), U32),
        compiler_params=_params(("arbitrary",)),
        name="moe_dispatch",
    )(chunks, nch, pstart, pend, rend, pos, h2)


def _experts_kernel(be_ref, nu_ref, valid_ref, first_ref, slot_ref, next_ref, x_ref, wgu_hbm, wd_hbm,
                    o_ref, wgu_f_ref, wd_f_ref, wgu_b_ref, wd_b_ref, sem, *, bm):
    i = pl.program_id(0)
    valid = valid_ref[i]
    sub = bm // 2

    def weight_copies(expert, slot):
        return (pltpu.make_async_copy(wgu_hbm.at[expert], wgu_f_ref.at[slot], sem.at[slot, 0]),
                pltpu.make_async_copy(wd_hbm.at[expert], wd_f_ref.at[slot], sem.at[slot, 1]))

    @pl.when(i == 0)
    def _():
        for c in weight_copies(be_ref[0], 0):
            c.start()

    @pl.when(first_ref[i] == 1)
    def _():
        slot = slot_ref[i]
        for c in weight_copies(be_ref[i], slot):
            c.wait()

        @pl.when(next_ref[i] >= 0)
        def _():
            for c in weight_copies(next_ref[i], 1 - slot):
                c.start()

        wgu_b_ref[...] = wgu_f_ref[slot].astype(BF16)
        wd_b_ref[...] = wd_f_ref[slot].astype(BF16)

    for r0 in (0, sub):
        rows = slice(r0, r0 + sub)

        @pl.when(valid > r0)
        def _():
            lo, hi = _unpack_bf16_pair(x_ref[rows, :])
            x = jnp.concatenate([lo.astype(BF16), hi.astype(BF16)], axis=1)
            gu = _dot(x, wgu_b_ref[...])
            act = (_silu(gu[:, 0:D_EXPERT]) * gu[:, D_EXPERT:]).astype(BF16)
            y = _dot(act, wd_b_ref[...])
            o_ref[rows, :] = _pack_bf16_pair(y[:, 0:D_MODEL // 2], y[:, D_MODEL // 2:])

        @pl.when(valid <= r0)
        def _():
            o_ref[rows, :] = jnp.zeros((sub, o_ref.shape[1]), o_ref.dtype)


def _experts(block_e, nused, valid, first, slot, next_e, xs, w_gate_up_l, w_down_l, bm):
    nrows, dh = xs.shape
    d = 2 * dh
    nblk = nrows // bm
    grid_spec = pltpu.PrefetchScalarGridSpec(
        num_scalar_prefetch=6,
        grid=(nblk,),
        in_specs=[
            pl.BlockSpec((bm, dh), lambda i, be, nu, *_: (jnp.minimum(i, nu[0] - 1), 0)),
            pl.BlockSpec(memory_space=pl.ANY),
            pl.BlockSpec(memory_space=pl.ANY),
        ],
        out_specs=pl.BlockSpec((bm, dh), lambda i, *_: (i, 0)),
        scratch_shapes=[
            pltpu.VMEM((2, d, 2 * D_EXPERT), w_gate_up_l.dtype),
            pltpu.VMEM((2, D_EXPERT, d), w_down_l.dtype),
            pltpu.VMEM((d, 2 * D_EXPERT), BF16),
            pltpu.VMEM((D_EXPERT, d), BF16),
            pltpu.SemaphoreType.DMA((2, 2)),
        ],
    )
    return pl.pallas_call(
        functools.partial(_experts_kernel, bm=bm),
        grid_spec=grid_spec,
        out_shape=jax.ShapeDtypeStruct((nrows, dh), U32),
        compiler_params=_params(("arbitrary",)),
        name="moe_experts",
    )(block_e, nused, valid, first, slot, next_e, xs, w_gate_up_l, w_down_l)


def _combine_kernel(chunk_ref, nch_ref, pstart_ref, pos_ref, wl_ref, xs_ref, ga2_ref, gf_ref,
                    yb_hbm, oa_ref, ob_ref, gbuf_ref, sem, *, tm, nta):
    i = pl.program_id(0)
    nt = pl.num_programs(0)
    cur = lax.rem(i, 2)
    refs = (chunk_ref, nch_ref, pstart_ref)

    def make_copy(buf):
        def build(tile_slab, buffer_slab, slabs):
            return pltpu.make_async_copy(yb_hbm.at[pl.ds(buffer_slab, slabs)],
                                         gbuf_ref.at[buf, pl.ds(tile_slab, slabs)], sem.at[buf])
        return build

    @pl.when(i == 0)
    def _():
        gbuf_ref[...] = jnp.zeros(gbuf_ref.shape, gbuf_ref.dtype)
        _start_chunks(0, *refs, make_copy(0))

    @pl.when(i + 1 < nt)
    def _():
        _start_chunks(i + 1, *refs, make_copy(1 - cur))

    _wait_chunks(i, nch_ref, make_copy(cur))

    lo, hi = _unpack_bf16_pair(gbuf_ref[cur].reshape(GROUPED_ROWS, gbuf_ref.shape[-1]))
    g = jnp.concatenate([lo.astype(BF16), hi.astype(BF16)], axis=1)
    pos16 = pos_ref[...].astype(jnp.int16)
    wl = wl_ref[...].astype(BF16)
    liota = lax.broadcasted_iota(I32, (tm, GROUPED_ROWS), 1).astype(jnp.int16)
    a = jnp.zeros((tm, GROUPED_ROWS), BF16)
    for r in range(TOP_K):
        a = jnp.where(liota == pos16[:, r:r + 1], jnp.broadcast_to(wl[:, r:r + 1], a.shape), a)
    routed = _dot(a, g)
    x2 = xs_ref[...] + ga2_ref[...] * routed
    ms = jnp.mean(x2 * x2, axis=-1, keepdims=True)
    y = x2 * lax.rsqrt(ms + EPS) * gf_ref[...]

    @pl.when(i < nta)
    def _():
        oa_ref[...] = y

    @pl.when(i >= nta)
    def _():
        ob_ref[...] = y


def _combine(chunks, nch, pstart, pos_t, wl_t, xs_base, ga2, gfin, yb, tm, nta):
    n, d = xs_base.shape
    ntb = n // tm - nta
    ga2_spec = pl.BlockSpec((None, tm, d), lambda i, *_: (jnp.minimum(i // nta, 1), 0, 0))
    grid_spec = pltpu.PrefetchScalarGridSpec(
        num_scalar_prefetch=3,
        grid=(n // tm,),
        in_specs=[
            pl.BlockSpec((tm, TOP_K), lambda i, *_: (i, 0)),
            pl.BlockSpec((tm, TOP_K), lambda i, *_: (i, 0)),
            pl.BlockSpec((tm, d), lambda i, *_: (i, 0)),
            ga2_spec,
            pl.BlockSpec((1, d), lambda i, *_: (0, 0)),
            pl.BlockSpec(memory_space=pl.ANY),
        ],
        out_specs=[
            pl.BlockSpec((tm, d), lambda i, *_: (jnp.minimum(i, nta - 1), 0)),
            pl.BlockSpec((tm, d), lambda i, *_: (jnp.maximum(i - nta, 0), 0)),
        ],
        scratch_shapes=[
            pltpu.VMEM((2, GROUPED_ROWS // ROW_GROUP, ROW_GROUP, d // 2), U32),
            pltpu.SemaphoreType.DMA((2,)),
        ],
    )
    return pl.pallas_call(
        functools.partial(_combine_kernel, tm=tm, nta=nta),
        grid_spec=grid_spec,
        out_shape=[jax.ShapeDtypeStruct((nta * tm, d), F32), jax.ShapeDtypeStruct((ntb * tm, d), F32)],
        compiler_params=_params(("arbitrary",)),
        name="moe_combine",
    )(chunks, nch, pstart, pos_t, wl_t, xs_base, ga2, gfin, yb)


def _moe_and_final(src_a, src_b, mods, wts, tm, bm):
    nta = src_a[0].shape[0] // tm
    ga1, sh2, sc2, ga2 = mods
    (w_out_b, g_ffn, wsgu_b, wsd_b, wr_t, rb, w_gate_up_l, w_down_l, g_final) = wts
    xs_base, h2, pos, wl, chunks, nch, cnt = _post(
        src_a, src_b, w_out_b, ga1, g_ffn, sc2, sh2, ga2, wsgu_b, wsd_b, wr_t, rb, tm)
    nt = xs_base.shape[0] // tm
    counts = cnt[:, 0].astype(I32)
    padded = (counts + bm - 1) // bm * bm
    pend = jnp.cumsum(padded)
    pstart = pend - padded
    nblk = -(-(nt * GROUPED_ROWS) // bm) + N_EXPERTS
    nused = (pend[-1] // bm).astype(I32)
    blk_row = jnp.minimum(jnp.arange(nblk, dtype=I32), nused - 1) * bm
    be = jnp.sum((pend[None, :] <= blk_row[:, None]).astype(I32), axis=1)
    chunks = chunks.reshape(-1)
    nch = nch[:, 0, 0]
    pslab = pstart // ROW_GROUP
    xs = _dispatch(chunks, nch, pslab, pend // ROW_GROUP, (pstart + counts) // ROW_GROUP, pos, h2,
                   nblk * bm, tm)
    xs = xs.reshape(nblk * bm, xs.shape[-1])
    region_end = jnp.sum(jnp.where(be[:, None] == jnp.arange(N_EXPERTS, dtype=I32)[None, :],
                                   (pstart + counts)[None, :], 0), axis=1)
    valid = jnp.clip(region_end - jnp.arange(nblk, dtype=I32) * bm, 0, bm)
    blk = jnp.arange(nblk, dtype=I32)
    first = ((blk < nused) & ((blk == 0) | (be != jnp.roll(be, 1)))).astype(I32)
    slot = (jnp.cumsum(first) - 1) % 2
    eids = jnp.arange(N_EXPERTS, dtype=I32)
    later_nonempty = (eids[None, :] > eids[:, None]) & (counts[None, :] > 0)
    next_tab = jnp.min(jnp.where(later_nonempty, eids[None, :], N_EXPERTS), axis=1)
    next_tab = jnp.where(next_tab == N_EXPERTS, -1, next_tab)
    next_e = jnp.sum(jnp.where(be[:, None] == eids[None, :], next_tab[None, :], 0), axis=1)
    yb = _experts(be, nused.reshape(1), valid, first, slot.astype(I32), next_e.astype(I32), xs,
                  w_gate_up_l, w_down_l, bm)
    yb = yb.reshape(nblk * bm // ROW_GROUP, ROW_GROUP, yb.shape[-1])
    return _combine(chunks, nch, pslab, pos.T, wl.T, xs_base, ga2, g_final, yb, tm, nta)


def _expand(mod, reps):
    if mod.shape[0] == 1:
        return mod
    return jnp.repeat(mod, reps, axis=0)


def kernel(x_prompt, x_sample, cache_k, cache_v, state_hgrn, c_prompt, c_sample, w_ada, b_ada,
           norm_mix, norm_ffn, norm_final, w_in, w_out, hg_lb_logits, hg_norm, da_lambda, da_norm,
           rel_bias_table, w_router, router_bias, w_gate_up, w_down, ws_gate_up, ws_down):
    depth = w_in.shape[0]
    assert depth == 1 and hg_lb_logits.shape[0] == 2
    bp, tp, d = x_prompt.shape
    bs, ts, _ = x_sample.shape
    assert bp == 1
    past = cache_k.shape[2]
    l = 0

    rows = -(-(bp + bs) // 8) * 8
    c_all = jnp.zeros((rows, d), F32).at[:bp].set(c_prompt).at[bp:bp + bs].set(c_sample)
    mod = _adaln(c_all, w_ada[l], b_ada[l])
    mod_p = [mod[0:bp, j * d:(j + 1) * d] for j in range(6)]
    mod_s = [_expand(mod[bp:bp + bs, j * d:(j + 1) * d], ts) for j in range(6)]

    w_in_b = w_in[l].astype(BF16)
    w_out_b = w_out[l].astype(BF16)
    wsgu_b = ws_gate_up[l].astype(BF16)
    wsd_b = ws_down[l].astype(BF16)
    wr_t = w_router[l].T
    g_mix = norm_mix[l].reshape(1, d)
    g_ffn = norm_ffn[l].reshape(1, d)
    g_final = norm_final.reshape(1, d)
    moe_w = (w_out_b, g_ffn, wsgu_b, wsd_b, wr_t, router_bias[l], w_gate_up[l], w_down[l], g_final)

    lam = _lam(da_lambda[l])

    t_att = min(ATT_TILE, tp)
    kk = jnp.arange(t_att, dtype=I32)[:, None]
    qq = jnp.arange(t_att, dtype=I32)[None, :]
    idx_diag = jnp.where((kk // CHUNK) <= (qq // CHUNK), _rel_bucket(kk - qq), MASK_BUCKET)
    idx_prev = _rel_bucket(kk - qq - t_att)
    bias_p = _bias_tiles(rel_bias_table, jnp.stack([idx_diag, idx_prev]).astype(I32),
                         shift_bucket=N_BUCKETS // 2 - 1)
    pad = 128
    qpos = past + jnp.arange(ts, dtype=I32)[:, None]
    idx_sp = _rel_bucket(jnp.arange(past, dtype=I32)[None, :] - qpos)
    kn = jnp.arange(pad, dtype=I32)[None, :]
    idx_sn = jnp.where(kn < ts, _rel_bucket(past + kn - qpos), MASK_BUCKET)
    bias_sp = _bias_tiles(rel_bias_table, idx_sp[None].astype(I32), shift_bucket=None)
    bias_sn = _bias_tiles(rel_bias_table, idx_sn[None].astype(I32), shift_bucket=None)

    xp = x_prompt.reshape(bp * tp, d)
    sh1, sc1, ga1, sh2, sc2, ga2 = mod_p
    assert ATT_TILE == INPROJ_TILE
    zh, qt, kf, vf, kb, vt = _inproj(xp, g_mix, sc1, sh1, w_in_b, t_att, True)
    s_zero = jnp.zeros((bp, HG_HEADS, HG_DIM, HG_DIM), F32)
    ohg_p, sp_new = _hgrn(zh, s_zero, hg_lb_logits, hg_norm[l], bp, tp, min(HGRN_CHUNK, tp))
    oda_p = _attn_prompt(kb, qt, vt, bias_p, lam, da_norm[l], t_att)
    src_p = (xp, ohg_p, oda_p)
    mods_p = (ga1, sh2, sc2, ga2)
    k_prompt = kf.reshape(1, bp, tp, DA_HEADS, 2 * DA_QKDIM)
    v_prompt = vf.reshape(1, bp, tp, DA_HEADS, DA_VDIM)

    ns = bs * ts
    xs_ = x_sample.reshape(ns, d)
    sh1, sc1, ga1, sh2, sc2, ga2 = mod_s
    zh, qs, kf, vf, kb, vb = _inproj(xs_, g_mix, sc1, sh1, w_in_b, ns, False)
    ohg_s, ss_new = _hgrn(zh, state_hgrn[l], hg_lb_logits, hg_norm[l], bs, ts, ts)
    oda_s = _attn_step(qs, cache_k[l], cache_v[l], kb, vb, bias_sp, bias_sn, lam, da_norm[l], bs, ts)
    assert ns == POST_TILE
    mods = tuple(jnp.stack([jnp.broadcast_to(mp, (POST_TILE, d)), ms_])
                 for mp, ms_ in zip(mods_p, (ga1, sh2, sc2, ga2)))
    y_p, y_s = _moe_and_final(src_p, (xs_, ohg_s, oda_s), mods, moe_w, POST_TILE, MOE_BLOCK_ROWS)
    k_sample = kf.reshape(1, bs, ts, DA_HEADS, 2 * DA_QKDIM)
    v_sample = vf.reshape(1, bs, ts, DA_HEADS, DA_VDIM)

    return (y_p.reshape(bp, tp, d), y_s.reshape(bs, ts, d), k_prompt, v_prompt, sp_new[None],
            k_sample, v_sample, ss_new[None].astype(x_sample.dtype))
```

```python
import functools
import math

import numpy as np
import jax
import jax.numpy as jnp
from jax import lax
from jax.experimental import pallas as pl
from jax.experimental.pallas import tpu as pltpu

F32 = jnp.float32
BF16 = jnp.bfloat16
I32 = jnp.int32
U32 = jnp.uint32
HIGHEST = lax.Precision.HIGHEST

D_MODEL = 1024
CHUNK = 64
HG_HEADS = 4
HG_DIM = 128
HG_WIDTH = HG_HEADS * HG_DIM
DA_HEADS = 4
DA_VDIM = 128
DA_QKDIM = 64
DA_WIDTH = DA_HEADS * DA_VDIM
N_BUCKETS = 32
MAX_DIST = 128
N_EXPERTS = 64
TOP_K = 8
N_GROUPS = 8
GROUP_SIZE = N_EXPERTS // N_GROUPS
TOP_GROUPS = 4
D_EXPERT = 256
ROUTE_SCALE = 2.5
EPS = 1e-6
LAM_INIT = 0.8 - 0.6 * math.exp(-0.3 * 0)

LOG2E = math.log2(math.e)
HI_MASK = np.uint32(0xFFFF0000)
NEG_BIG = -1e30
MASK_BUCKET = N_BUCKETS
V7X_VMEM_LIMIT = 48 * 1024 * 1024

ATT_TILE = 512
VT_ROWS = DA_VDIM + 16
HGRN_CHUNK = 256
INPROJ_TILE = 512
POST_TILE = 256
MOE_BLOCK_ROWS = 1024
ROW_GROUP = 8
GROUPED_ROWS = -(-(POST_TILE * TOP_K + N_EXPERTS * (ROW_GROUP - 1)) // 256) * 256
CHUNK_SLOTS = -(-(GROUPED_ROWS // ROW_GROUP) // 128) * 128
CHUNK_EXPERT_SHIFT = 24
CHUNK_UNROLL = 4
WAIT_CHUNKS = 16
ZERO_SLABS = 16


def _sigmoid(x):
    return 1.0 / (1.0 + jnp.exp(-x))


def _silu(x):
    return x * _sigmoid(x)


def _dot(a, b, **kw):
    return jnp.dot(a, b, preferred_element_type=F32, **kw)


def _dot_nt(a, b, **kw):
    return lax.dot_general(a, b, (((1,), (1,)), ((), ())), preferred_element_type=F32, **kw)


def _dot_tn(a, b, **kw):
    return lax.dot_general(a, b, (((0,), (0,)), ((), ())), preferred_element_type=F32, **kw)


def _pack_bf16_pair(lo, hi):
    lo_bits = lax.bitcast_convert_type(lo.astype(BF16).astype(F32), U32)
    hi_bits = lax.bitcast_convert_type(hi.astype(BF16).astype(F32), U32)
    return (lo_bits >> 16) | (hi_bits & HI_MASK)


def _pack_exact_bf16_pair(lo, hi):
    return (lax.bitcast_convert_type(lo, U32) >> 16) | (lax.bitcast_convert_type(hi, U32) & HI_MASK)


def _unpack_bf16_pair(w):
    lo = lax.bitcast_convert_type(w << 16, F32)
    hi = lax.bitcast_convert_type(w & HI_MASK, F32)
    return lo, hi


def _params(sem, vmem=V7X_VMEM_LIMIT, flags=None):
    return pltpu.CompilerParams(dimension_semantics=sem, vmem_limit_bytes=vmem, flags=flags)


def _adaln_kernel(c_ref, w_ref, b_ref, o_ref):
    s = _silu(c_ref[...])
    o_ref[...] = _dot(s, w_ref[...], precision=HIGHEST) + b_ref[...]


def _adaln(c_all, w_ada, b_ada):
    rows, d = c_all.shape
    cols = w_ada.shape[1]
    blk = 1024
    return pl.pallas_call(
        _adaln_kernel,
        grid=(cols // blk,),
        in_specs=[
            pl.BlockSpec((rows, d), lambda j: (0, 0)),
            pl.BlockSpec((d, blk), lambda j: (0, j)),
            pl.BlockSpec((1, blk), lambda j: (0, j)),
        ],
        out_specs=pl.BlockSpec((rows, blk), lambda j: (0, j)),
        out_shape=jax.ShapeDtypeStruct((rows, cols), F32),
        compiler_params=_params(("parallel",)),
        name="adaln",
    )(c_all, w_ada, b_ada.reshape(1, cols))


def _lam_kernel(l_ref, o_ref):
    l = l_ref[...].astype(F32)
    a = jnp.sum(l[0:1] * l[1:2], axis=-1, keepdims=True)
    b = jnp.sum(l[2:3] * l[3:4], axis=-1, keepdims=True)
    lam = jnp.exp(a) - jnp.exp(b) + LAM_INIT
    o_ref[...] = jnp.broadcast_to(lam, o_ref.shape)


def _lam(da_lambda_l):
    return pl.pallas_call(
        _lam_kernel,
        out_shape=jax.ShapeDtypeStruct((8, 128), F32),
        name="lam",
    )(da_lambda_l)


def _rel_bucket(rel):
    nb = N_BUCKETS // 2
    max_exact = nb // 2
    side = jnp.where(rel > 0, nb, 0)
    n = jnp.abs(rel)
    large = max_exact + (jnp.log(jnp.maximum(n, 1).astype(F32) / max_exact)
                         / math.log(MAX_DIST / max_exact) * (nb - max_exact)).astype(I32)
    large = jnp.minimum(large, nb - 1)
    return side + jnp.where(n < max_exact, n, large)


def _bias_kernel(tab_ref, idx_ref, o_ref, *, shift_bucket):
    h = pl.program_id(0)
    idx = idx_ref[...]
    shift = tab_ref[shift_bucket, h] if shift_bucket is not None else 0.0
    acc = jnp.zeros(idx.shape, F32)
    for j in range(N_BUCKETS):
        acc = jnp.where(idx == j, (tab_ref[j, h] - shift) * LOG2E, acc)
    o_ref[...] = jnp.where(idx == MASK_BUCKET, NEG_BIG, acc)


def _bias_tiles(table, idx, *, shift_bucket):
    k, r, c = idx.shape
    return pl.pallas_call(
        functools.partial(_bias_kernel, shift_bucket=shift_bucket),
        grid=(DA_HEADS, k),
        in_specs=[
            pl.BlockSpec(memory_space=pltpu.SMEM),
            pl.BlockSpec((None, r, c), lambda h, d: (d, 0, 0)),
        ],
        out_specs=pl.BlockSpec((None, None, r, c), lambda h, d: (h, d, 0, 0)),
        out_shape=jax.ShapeDtypeStruct((DA_HEADS, k, r, c), F32),
        compiler_params=_params(("parallel", "parallel")),
        name="rel_bias",
    )(table, idx)


def _inproj_kernel(x_ref, g_ref, sc_ref, sh_ref, w_ref,
                   zh_ref, q_ref, k_ref, v_ref, kb_ref, vb_ref, *, transposed):
    x = x_ref[...]
    ms = jnp.mean(x * x, axis=-1, keepdims=True)
    h = x * lax.rsqrt(ms + EPS) * g_ref[...]
    h = h * (1.0 + sc_ref[...]) + sh_ref[...]
    hb = h.astype(BF16)
    c0 = 4 * HG_WIDTH
    zh_ref[...] = _dot(hb, w_ref[:, 0:c0])
    zq = _dot(hb, w_ref[:, c0:c0 + DA_WIDTH]) * (DA_QKDIM ** -0.5 * LOG2E)
    zk = _dot(hb, w_ref[:, c0 + DA_WIDTH:c0 + 2 * DA_WIDTH])
    for hd in range(DA_HEADS):
        k_ref[:, hd, :] = zk[:, hd * DA_VDIM:(hd + 1) * DA_VDIM]
    kb_ref[...] = zk.astype(BF16)
    zv = _dot(hb, w_ref[:, c0 + 2 * DA_WIDTH:c0 + 3 * DA_WIDTH])
    for hd in range(DA_HEADS):
        v_ref[:, hd, :] = zv[:, hd * DA_VDIM:(hd + 1) * DA_VDIM]
    if transposed:
        q_ref[...] = zq.T.astype(BF16)
        vb_ref[:, 0:DA_VDIM, :] = zv.T.astype(BF16).reshape(DA_HEADS, DA_VDIM, zv.shape[0])
        vb_ref[:, DA_VDIM:, :] = jnp.ones((DA_HEADS, VT_ROWS - DA_VDIM, zv.shape[0]), BF16)
    else:
        q_ref[...] = zq.astype(BF16)
        vb_ref[...] = zv.astype(BF16)


def _mod_spec(mod, tm):
    if mod.shape[0] == 1:
        return pl.BlockSpec((1, mod.shape[1]), lambda i: (0, 0))
    return pl.BlockSpec((tm, mod.shape[1]), lambda i: (i, 0))


def _inproj(x, g, sc, sh, w_in_b, tm, transposed):
    n, d = x.shape
    cols = w_in_b.shape[1]
    row = lambda i: (i, 0)
    if transposed:
        q_spec = pl.BlockSpec((DA_WIDTH, tm), lambda i: (0, i))
        q_shape = jax.ShapeDtypeStruct((DA_WIDTH, n), BF16)
        vb_spec = pl.BlockSpec((DA_HEADS, None, VT_ROWS, tm), lambda i: (0, i, 0, 0))
        vb_shape = jax.ShapeDtypeStruct((DA_HEADS, n // tm, VT_ROWS, tm), BF16)
    else:
        q_spec = vb_spec = pl.BlockSpec((tm, DA_WIDTH), row)
        q_shape = vb_shape = jax.ShapeDtypeStruct((n, DA_WIDTH), BF16)
    return pl.pallas_call(
        functools.partial(_inproj_kernel, transposed=transposed),
        grid=(n // tm,),
        in_specs=[
            pl.BlockSpec((tm, d), row),
            pl.BlockSpec((1, d), lambda i: (0, 0)),
            _mod_spec(sc, tm),
            _mod_spec(sh, tm),
            pl.BlockSpec((d, cols), lambda i: (0, 0)),
        ],
        out_specs=[
            pl.BlockSpec((tm, 4 * HG_WIDTH), row),
            q_spec,
            pl.BlockSpec((tm, DA_HEADS, DA_VDIM), lambda i: (i, 0, 0)),
            pl.BlockSpec((tm, DA_HEADS, DA_VDIM), lambda i: (i, 0, 0)),
            pl.BlockSpec((tm, DA_WIDTH), row),
            vb_spec,
        ],
        out_shape=[
            jax.ShapeDtypeStruct((n, 4 * HG_WIDTH), F32),
            q_shape,
            jax.ShapeDtypeStruct((n, DA_HEADS, DA_VDIM), F32),
            jax.ShapeDtypeStruct((n, DA_HEADS, DA_VDIM), F32),
            jax.ShapeDtypeStruct((n, DA_WIDTH), BF16),
            vb_shape,
        ],
        compiler_params=_params(("parallel",)),
        name="inproj",
    )(x, g, sc, sh, w_in_b)


def _hgrn_consts(c):
    levels = int(round(math.log2(c)))
    assert 1 << levels == c and levels >= 3
    t = np.arange(c)[:, None]
    r = np.arange(c)[None, :]
    tri = (r <= t).astype(np.float32)
    x = np.maximum(t ^ r, 1)
    lv = np.where(t == r, -1, np.where(t > r, np.floor(np.log2(x)).astype(np.int64), -2))
    return jnp.asarray(tri, dtype=BF16), jnp.asarray(lv, dtype=I32), levels


def _hgrn_kernel(zh_ref, s0_ref, lbl_ref, gain_ref, mall_ref, lv_ref,
                 o_ref, sout_ref, st_ref, b_ref, *, c, levels):
    ci = pl.program_id(1)

    @pl.when(ci == 0)
    def _():
        for h in range(HG_HEADS):
            st_ref[h] = s0_ref[h].astype(F32).T

    lbl = lbl_ref[...].astype(F32)
    mx = jnp.maximum(lbl[0:1], lbl[1:2])
    e0 = jnp.exp(lbl[0:1] - mx)
    e1 = jnp.exp(lbl[1:2] - mx)
    lb = e0 / (e0 + e1)

    xq = zh_ref[:, 0:HG_WIDTH]
    xf = zh_ref[:, HG_WIDTH:2 * HG_WIDTH]
    q = _silu(xq)
    y = lb + (1.0 - lb) * _sigmoid(xf)
    logf = jnp.log(y)
    kk = 1.0 - y

    l1 = logf.astype(BF16)
    r1 = logf - l1.astype(F32)
    l2 = r1.astype(BF16)
    l3 = (r1 - l2.astype(F32)).astype(BF16)
    tri = mall_ref[...]
    b = _dot(tri, l1) + _dot(tri, l2) + _dot(tri, l3)
    b_ref[...] = b
    trow = lax.broadcasted_iota(I32, (c, HG_DIM), 0)

    def level_factor(l, sl):
        m = 1 << l
        later = (trow & m) != 0
        lf = logf[:, sl]
        if l == 0:
            e = jnp.where(later, lf, 0.0)
        elif l == 1:
            below = pltpu.roll(lf, 1, 0)
            above = pltpu.roll(lf, c - 1, 0)
            low = (trow & 1) != 0
            e = jnp.where(later, jnp.where(low, lf + below, lf), jnp.where(low, 0.0, above))
        else:
            mid = jnp.concatenate(
                [jnp.broadcast_to(b_ref[k * 2 * m + m - 1:k * 2 * m + m, sl], (2 * m, HG_DIM))
                 for k in range(c // (2 * m))], axis=0)
            e = jnp.where(later, b[:, sl] - mid, mid - b[:, sl])
        return jnp.exp(e)

    lv = lv_ref[...]
    gain = gain_ref[...].astype(F32)
    for h in range(HG_HEADS):
        sl = slice(h * HG_DIM, (h + 1) * HG_DIM)
        qh = q[:, sl]
        kh = kk[:, sl]
        ih = zh_ref[:, 2 * HG_WIDTH + h * HG_DIM:2 * HG_WIDTH + (h + 1) * HG_DIM]
        gh = zh_ref[:, 3 * HG_WIDTH + h * HG_DIM:3 * HG_WIDTH + (h + 1) * HG_DIM]
        bh = b[:, sl]
        ihb = ih.astype(BF16)
        a = jnp.where(lv == -1, _dot_nt(qh.astype(BF16), kh.astype(BF16)), 0.0)
        for l in range(levels):
            f = level_factor(l, sl)
            p = _dot_nt((qh * f).astype(BF16), (kh * f).astype(BF16))
            a = jnp.where(lv == l, p, a)
        st = st_ref[h]
        o = _dot(a.astype(BF16), ihb) + _dot_nt((qh * jnp.exp(bh)).astype(BF16), st.astype(BF16))
        bl = bh[c - 1:c, :]
        kd = (kh * jnp.exp(bl - bh)).astype(BF16)
        st_ref[h] = st * jnp.exp(bl) + _dot_tn(ihb, kd)
        ms = jnp.mean(o * o, axis=-1, keepdims=True)
        on = o * lax.rsqrt(ms + EPS) * gain
        o_ref[:, sl] = (on * _silu(gh)).astype(o_ref.dtype)

    @pl.when(ci == pl.num_programs(1) - 1)
    def _():
        for h in range(HG_HEADS):
            sout_ref[h] = st_ref[h].T.astype(sout_ref.dtype)


def _hgrn(zh, s0, lb_logits, gain, batch, seq, c):
    mall, lv, levels = _hgrn_consts(c)
    nc = seq // c
    return pl.pallas_call(
        functools.partial(_hgrn_kernel, c=c, levels=levels),
        grid=(batch, nc),
        in_specs=[
            pl.BlockSpec((c, 4 * HG_WIDTH), lambda b, i: (b * nc + i, 0)),
            pl.BlockSpec((None, HG_HEADS, HG_DIM, HG_DIM), lambda b, i: (b, 0, 0, 0)),
            pl.BlockSpec(lb_logits.shape, lambda b, i: (0, 0)),
            pl.BlockSpec((1, HG_DIM), lambda b, i: (0, 0)),
            pl.BlockSpec(mall.shape, lambda b, i: (0, 0)),
            pl.BlockSpec(lv.shape, lambda b, i: (0, 0)),
        ],
        out_specs=[
            pl.BlockSpec((c, HG_WIDTH), lambda b, i: (b * nc + i, 0)),
            pl.BlockSpec((None, HG_HEADS, HG_DIM, HG_DIM), lambda b, i: (b, 0, 0, 0)),
        ],
        out_shape=[
            jax.ShapeDtypeStruct((batch * seq, HG_WIDTH), BF16),
            jax.ShapeDtypeStruct((batch, HG_HEADS, HG_DIM, HG_DIM), F32),
        ],
        scratch_shapes=[pltpu.VMEM((HG_HEADS, HG_DIM, HG_DIM), F32),
                        pltpu.VMEM((c, HG_WIDTH), F32)],
        compiler_params=_params(("parallel", "arbitrary")),
        name="hgrn2",
    )(zh, s0, lb_logits, gain.reshape(1, HG_DIM), mall, lv)


def _attn_kernel(k_ref, qt_ref, vt_ref, bias_ref, lam_ref, gain_ref,
                 o_ref, qz_ref, m_ref, acc_ref, s_ref, smax_ref, *, t):
    i = pl.program_id(1)
    qt = qt_ref[...]
    row = lax.broadcasted_iota(I32, qt.shape, 0)
    zero = jnp.zeros_like(qt)
    qz_ref[:, 0:t] = jnp.where(row < DA_QKDIM, qt, zero)
    qz_ref[:, t:2 * t] = jnp.where(row >= DA_QKDIM, qt, zero)
    m_ref[...] = jnp.full(m_ref.shape, NEG_BIG, F32)
    acc_ref[...] = jnp.zeros(acc_ref.shape, F32)

    def scores(j, buf):
        kt = k_ref[pl.ds(pl.multiple_of(j * t, t), t), :]
        s = _dot(kt, qz_ref[...])
        s_ref[buf] = s
        smax_ref[buf] = jnp.max(s, axis=0, keepdims=True)

    def consume(j, buf, bias_idx):
        s = s_ref[buf]
        if bias_idx is not None:
            b = bias_ref[bias_idx]
            s = jnp.concatenate([s[:, 0:t] + b, s[:, t:2 * t] + b], axis=1)
            s_max = jnp.max(s, axis=0, keepdims=True)
        else:
            s_max = smax_ref[buf]
        m_prev = m_ref[...]
        m_new = jnp.maximum(m_prev, s_max)
        alpha = jnp.exp2(m_prev - m_new)
        pr = jnp.exp2(s - m_new).astype(BF16)
        acc_ref[...] = alpha * acc_ref[...] + _dot(vt_ref[j], pr)
        m_ref[...] = m_new

    n_far = jnp.maximum(i - 1, 0)

    @pl.when(i >= 1)
    def _():
        scores(i - 1, 0)
        scores(i, 1)
        consume(i - 1, 0, 1)
        scores(0, 0)
        consume(i, 1, 0)

    @pl.when(i == 0)
    def _():
        scores(i, 1)
        consume(i, 1, 0)

    def far_tiles(j, count):
        for u in range(count):
            nxt = j + u + 1
            if u == count - 1:
                nxt = jnp.minimum(nxt, n_far - 1)
            scores(nxt, (u + 1) % 2)
            consume(j + u, u % 2, None)

    def far_quad(p, carry):
        far_tiles(4 * p, 4)
        return carry

    def far_pair(p, carry):
        far_tiles(4 * quads + 2 * p, 2)
        return carry

    quads = n_far // 4
    lax.fori_loop(0, quads, far_quad, 0)
    lax.fori_loop(0, (n_far - 4 * quads) // 2, far_pair, 0)

    @pl.when(lax.rem(n_far, 2) == 1)
    def _():
        consume(n_far - 1, 0, None)

    lam = lam_ref[0:1, 0:1]
    acc = acc_ref[0:DA_VDIM, :]
    l = acc_ref[DA_VDIM:DA_VDIM + 1, :]
    o = acc[:, 0:t] / l[:, 0:t] - lam * (acc[:, t:2 * t] / l[:, t:2 * t])
    ms = jnp.mean(o * o, axis=0, keepdims=True)
    on = o * lax.rsqrt(ms + EPS) * gain_ref[...].astype(F32) * (1.0 - LAM_INIT)
    o_ref[...] = on.T.astype(o_ref.dtype)


def _attn_prompt(kb, qt, vt, bias, lam, gain, t):
    n = kb.shape[0]
    nt = n // t
    return pl.pallas_call(
        functools.partial(_attn_kernel, t=t),
        grid=(DA_HEADS, nt),
        in_specs=[
            pl.BlockSpec((n, DA_VDIM), lambda h, i: (0, h)),
            pl.BlockSpec((DA_VDIM, t), lambda h, i: (h, i)),
            pl.BlockSpec((None, nt, VT_ROWS, t), lambda h, i: (h, 0, 0, 0)),
            pl.BlockSpec((None, 2, t, t), lambda h, i: (h, 0, 0, 0)),
            pl.BlockSpec((8, 128), lambda h, i: (0, 0)),
            pl.BlockSpec((DA_VDIM, 1), lambda h, i: (0, 0)),
        ],
        out_specs=pl.BlockSpec((t, DA_VDIM), lambda h, i: (i, h)),
        out_shape=jax.ShapeDtypeStruct((n, DA_WIDTH), BF16),
        scratch_shapes=[
            pltpu.VMEM((DA_VDIM, 2 * t), BF16),
            pltpu.VMEM((1, 2 * t), F32),
            pltpu.VMEM((VT_ROWS, 2 * t), F32),
            pltpu.VMEM((2, t, 2 * t), F32),
            pltpu.VMEM((2, 1, 2 * t), F32),
        ],
        compiler_params=_params(("parallel", "parallel")),
        name="diff_attn_prompt",
    )(kb, qt, vt, bias, lam, gain.reshape(DA_VDIM, 1))


def _attn_step_kernel(q_ref, kp_ref, vp_ref, kn_ref, vn_ref, bp_ref, bn_ref, lam_ref, gain_ref,
                      o_ref, *, tq, pad):
    lam = lam_ref[0:1, 0:1]
    gain = gain_ref[...].astype(F32)
    zpad = jnp.zeros((pad - tq, DA_VDIM), BF16)
    for h in range(DA_HEADS):
        hs = slice(h * DA_VDIM, (h + 1) * DA_VDIM)
        q = q_ref[:, hs]
        lane = lax.broadcasted_iota(I32, q.shape, 1)
        zero = jnp.zeros_like(q)
        qz = jnp.concatenate([jnp.where(lane < DA_QKDIM, q, zero),
                              jnp.where(lane >= DA_QKDIM, q, zero)], axis=0)
        kp = kp_ref[:, h, :].astype(BF16)
        vp = vp_ref[:, h, :].astype(BF16)
        kn = jnp.concatenate([kn_ref[:, hs], zpad], axis=0)
        vn = jnp.concatenate([vn_ref[:, hs], zpad], axis=0)
        bp = bp_ref[h, 0]
        bn = bn_ref[h, 0]
        sp = _dot_nt(qz, kp) + jnp.concatenate([bp, bp], axis=0)
        sn = _dot_nt(qz, kn) + jnp.concatenate([bn, bn], axis=0)
        m = jnp.maximum(jnp.max(sp, axis=-1, keepdims=True), jnp.max(sn, axis=-1, keepdims=True))
        pp = jnp.exp2(sp - m)
        pn = jnp.exp2(sn - m)
        l = jnp.sum(pp, axis=-1, keepdims=True) + jnp.sum(pn, axis=-1, keepdims=True)
        acc = _dot(pp.astype(BF16), vp) + _dot(pn.astype(BF16), vn)
        on = acc / l
        o = on[0:tq] - lam * on[tq:2 * tq]
        ms = jnp.mean(o * o, axis=-1, keepdims=True)
        o = o * lax.rsqrt(ms + EPS) * gain * (1.0 - LAM_INIT)
        o_ref[:, hs] = o.astype(o_ref.dtype)


def _attn_step(qs, cache_k_l, cache_v_l, kb, vb, bias_p, bias_n, lam, gain, batch, tq):
    past = cache_k_l.shape[1]
    pad = bias_n.shape[-1]
    cache_spec = pl.BlockSpec((None, past, DA_HEADS, DA_VDIM), lambda b: (b, 0, 0, 0))
    row = pl.BlockSpec((tq, DA_WIDTH), lambda b: (b, 0))
    return pl.pallas_call(
        functools.partial(_attn_step_kernel, tq=tq, pad=pad),
        grid=(batch,),
        in_specs=[
            row, cache_spec, cache_spec, row, row,
            pl.BlockSpec(bias_p.shape, lambda b: (0, 0, 0, 0)),
            pl.BlockSpec(bias_n.shape, lambda b: (0, 0, 0, 0)),
            pl.BlockSpec((8, 128), lambda b: (0, 0)),
            pl.BlockSpec((1, DA_VDIM), lambda b: (0, 0)),
        ],
        out_specs=row,
        out_shape=jax.ShapeDtypeStruct((batch * tq, DA_WIDTH), BF16),
        compiler_params=_params(("parallel",)),
        name="diff_attn_step",
    )(qs, cache_k_l, cache_v_l, kb, vb, bias_p, bias_n, lam, gain.reshape(1, DA_VDIM))


def _post_kernel(xa_ref, ohga_ref, odaa_ref, xb_ref, ohgb_ref, odab_ref,
                 wout_ref, ga1_ref, g_ref, sc_ref, sh_ref, ga2_ref,
                 wsgu_ref, wsd_ref, wrt_ref, rb_ref, tri_ref, ltri_ref,
                 xs_ref, h2_ref, pos_ref, wl_ref, chunk_ref, nch_ref, cnt_ref, carry_ref, *, tm, nta):
    i = pl.program_id(0)

    @pl.when(i == 0)
    def _():
        carry_ref[...] = jnp.zeros(carry_ref.shape, F32)

    second = i >= nta
    x = jnp.where(second, xb_ref[...], xa_ref[...])
    ohg = jnp.where(second, ohgb_ref[...], ohga_ref[...])
    oda = jnp.where(second, odab_ref[...], odaa_ref[...])
    mix = _dot(ohg, wout_ref[0:HG_WIDTH, :]) + _dot(oda, wout_ref[HG_WIDTH:, :])
    x1 = x + ga1_ref[...] * mix
    ms = jnp.mean(x1 * x1, axis=-1, keepdims=True)
    h2 = x1 * lax.rsqrt(ms + EPS) * g_ref[...]
    h2 = h2 * (1.0 + sc_ref[...]) + sh_ref[...]
    h2b = h2.astype(BF16)
    h2_ref[...] = h2b
    gu = _dot(h2b, wsgu_ref[...])
    act = (_silu(gu[:, 0:D_EXPERT]) * gu[:, D_EXPERT:]).astype(BF16)
    xs_ref[...] = x1 + ga2_ref[...] * _dot(act, wsd_ref[...])

    logits = _dot_nt(wrt_ref[...], h2, precision=HIGHEST)
    score = _sigmoid(logits)
    sel = score + rb_ref[...]
    sub = lax.broadcasted_iota(I32, (GROUP_SIZE, tm), 0)
    gscore = []
    for g in range(N_GROUPS):
        v = sel[g * GROUP_SIZE:(g + 1) * GROUP_SIZE, :]
        m1 = jnp.max(v, axis=0, keepdims=True)
        i1 = jnp.min(jnp.where(v == m1, sub, GROUP_SIZE), axis=0, keepdims=True)
        m2 = jnp.max(jnp.where(sub == i1, -jnp.inf, v), axis=0, keepdims=True)
        gscore.append(m1 + m2)
    gsel = []
    for g in range(N_GROUPS):
        ahead = jnp.zeros((1, tm), F32)
        for g2 in range(N_GROUPS):
            if g2 == g:
                continue
            tie = 1.0 if g2 < g else 0.0
            ahead = ahead + jnp.where(gscore[g2] > gscore[g], 1.0,
                                      jnp.where(gscore[g2] == gscore[g], tie, 0.0))
        gsel.append(ahead < TOP_GROUPS)
    selm = jnp.concatenate(
        [jnp.where(gsel[g], sel[g * GROUP_SIZE:(g + 1) * GROUP_SIZE, :], -jnp.inf)
         for g in range(N_GROUPS)], axis=0)
    eio = lax.broadcasted_iota(I32, (N_EXPERTS, tm), 0)
    ahead = jnp.zeros((N_EXPERTS, tm), F32)
    for e2 in range(N_EXPERTS):
        row = selm[e2:e2 + 1, :]
        tie = jnp.where(eio > e2, 1.0, 0.0)
        ahead = ahead + jnp.where(row > selm, 1.0, jnp.where(row == selm, tie, 0.0))
    chosen = jnp.where(selm > -jnp.inf, jnp.where(ahead < TOP_K, 1.0, 0.0), 0.0)
    w = chosen * score
    wn = w / jnp.sum(w, axis=0, keepdims=True) * ROUTE_SCALE

    chb = chosen.astype(BF16)
    before = _dot(chb, tri_ref[...])
    tot = _dot(chb, jnp.ones((tm, 128), BF16))
    run = jnp.floor((tot + (ROW_GROUP - 1)) * (1.0 / ROW_GROUP)) * ROW_GROUP
    tile_base = _dot(ltri_ref[...], run.astype(BF16))
    carry = carry_ref[...]
    carry_ref[...] = carry + run
    cnt_ref[...] = carry + run
    pos = jnp.concatenate([tile_base] * (tm // 128), axis=1) + before

    widen = lambda v: jnp.concatenate([v] * (CHUNK_SLOTS // 128), axis=1)
    crow = lax.broadcasted_iota(I32, (N_EXPERTS, CHUNK_SLOTS), 1).astype(F32) * ROW_GROUP
    erow = lax.broadcasted_iota(I32, (N_EXPERTS, CHUNK_SLOTS), 0).astype(F32)
    owner = jnp.sum(jnp.where(widen(tile_base + run) <= crow, 1.0, 0.0), axis=0, keepdims=True)
    region_row = jnp.sum(jnp.where(owner == erow, widen(carry - tile_base), 0.0),
                         axis=0, keepdims=True) + crow[0:1]
    region_slab = (region_row * (1.0 / ROW_GROUP)).astype(I32)
    chunk_ref[...] = owner.astype(I32) * (1 << CHUNK_EXPERT_SHIFT) + region_slab
    nch_ref[...] = jnp.sum(run * (1.0 / ROW_GROUP), axis=0, keepdims=True).astype(I32)

    for r in range(TOP_K):
        pick = jnp.where(ahead == r, chosen, 0.0)
        pos_ref[r:r + 1, :] = jnp.sum(pick * pos, axis=0, keepdims=True).astype(I32)
        wl_ref[r:r + 1, :] = jnp.sum(pick * wn, axis=0, keepdims=True)


def _post(src_a, src_b, w_out_b, ga1, g, sc, sh, ga2, wsgu_b, wsd_b, wr_t, rb, tm):
    (xa, ohga, odaa), (xb, ohgb, odab) = src_a, src_b
    d = xa.shape[1]
    nta, ntb = xa.shape[0] // tm, xb.shape[0] // tm
    nt = nta + ntb
    n = nt * tm
    tri = jnp.asarray(np.triu(np.ones((tm, tm), np.float32), k=1), dtype=BF16)
    ltri = jnp.asarray(np.tril(np.ones((N_EXPERTS, N_EXPERTS), np.float32), k=-1), dtype=BF16)
    row = lambda i: (i, 0)
    row_a = lambda i: (jnp.minimum(i, nta - 1), 0)
    row_b = lambda i: (jnp.maximum(i - nta, 0), 0)
    col = lambda i: (0, i)
    full = lambda i: (0, 0)
    mod = pl.BlockSpec((None, tm, d), lambda i: (jnp.minimum(i // nta, 1), 0, 0))
    return pl.pallas_call(
        functools.partial(_post_kernel, tm=tm, nta=nta),
        grid=(nt,),
        in_specs=[
            pl.BlockSpec((tm, d), row_a),
            pl.BlockSpec((tm, HG_WIDTH), row_a),
            pl.BlockSpec((tm, DA_WIDTH), row_a),
            pl.BlockSpec((tm, d), row_b),
            pl.BlockSpec((tm, HG_WIDTH), row_b),
            pl.BlockSpec((tm, DA_WIDTH), row_b),
            pl.BlockSpec(w_out_b.shape, full),
            mod,
            pl.BlockSpec((1, d), full),
            mod,
            mod,
            mod,
            pl.BlockSpec(wsgu_b.shape, full),
            pl.BlockSpec(wsd_b.shape, full),
            pl.BlockSpec(wr_t.shape, full),
            pl.BlockSpec((N_EXPERTS, 1), full),
            pl.BlockSpec((tm, tm), full),
            pl.BlockSpec((N_EXPERTS, N_EXPERTS), full),
        ],
        out_specs=[
            pl.BlockSpec((tm, d), row),
            pl.BlockSpec((tm, d), row),
            pl.BlockSpec((TOP_K, tm), col),
            pl.BlockSpec((TOP_K, tm), col),
            pl.BlockSpec((None, 1, CHUNK_SLOTS), lambda i: (i, 0, 0)),
            pl.BlockSpec((None, 1, 128), lambda i: (i, 0, 0)),
            pl.BlockSpec((N_EXPERTS, 128), full),
        ],
        out_shape=[
            jax.ShapeDtypeStruct((n, d), F32),
            jax.ShapeDtypeStruct((n, d), BF16),
            jax.ShapeDtypeStruct((TOP_K, n), I32),
            jax.ShapeDtypeStruct((TOP_K, n), F32),
            jax.ShapeDtypeStruct((nt, 1, CHUNK_SLOTS), I32),
            jax.ShapeDtypeStruct((nt, 1, 128), I32),
            jax.ShapeDtypeStruct((N_EXPERTS, 128), F32),
        ],
        scratch_shapes=[pltpu.VMEM((N_EXPERTS, 128), F32)],
        compiler_params=_params(("arbitrary",)),
        name="post_mix_router",
    )(xa, ohga, odaa, xb, ohgb, odab, w_out_b, ga1, g, sc, sh, ga2, wsgu_b, wsd_b, wr_t,
      rb.reshape(N_EXPERTS, 1), tri, ltri)


def _start_chunks(tile, chunk_ref, nch_ref, pslab_ref, make_copy):
    n = nch_ref[tile]

    def start(c):
        word = chunk_ref[tile * CHUNK_SLOTS + c]
        expert = lax.shift_right_logical(word, CHUNK_EXPERT_SHIFT)
        region_slab = word & ((1 << CHUNK_EXPERT_SHIFT) - 1)
        make_copy(c, pslab_ref[expert] + region_slab, 1).start()

    def group(g, carry):
        for u in range(CHUNK_UNROLL):
            start(g * CHUNK_UNROLL + u)
        return carry

    def single(c, carry):
        start(c)
        return carry

    groups = n // CHUNK_UNROLL
    lax.fori_loop(0, groups, group, 0)
    lax.fori_loop(groups * CHUNK_UNROLL, n, single, 0)


def _wait_chunks(tile, nch_ref, make_copy):
    n = nch_ref[tile]
    many = n // WAIT_CHUNKS

    def wait_many(j, carry):
        make_copy(0, 0, WAIT_CHUNKS).wait()
        return carry

    def wait_one(j, carry):
        make_copy(0, 0, 1).wait()
        return carry

    lax.fori_loop(0, many, wait_many, 0)
    lax.fori_loop(many * WAIT_CHUNKS, n, wait_one, 0)


def _dispatch_kernel(chunk_ref, nch_ref, pstart_ref, pend_ref, rend_ref, pos_ref, h2_ref, xs_hbm,
                     cbuf_ref, zero_ref, zsem, sem, *, tm):
    i = pl.program_id(0)
    nt = pl.num_programs(0)
    dh = cbuf_ref.shape[-1]

    def zero_piece(slab):
        return pltpu.make_async_copy(zero_ref, xs_hbm.at[pl.ds(slab, ZERO_SLABS)], zsem)

    def zero_fill(op):
        def pieces(start, stop):
            first = (start // ZERO_SLABS) * ZERO_SLABS

            def piece(k, carry):
                op(zero_piece(first + k * ZERO_SLABS))
                return carry

            lax.fori_loop(0, (stop - first) // ZERO_SLABS, piece, 0)

        def region(e, carry):
            pieces(rend_ref[e], pend_ref[e])
            return carry

        lax.fori_loop(0, N_EXPERTS, region, 0)
        pieces(pend_ref[N_EXPERTS - 1], xs_hbm.shape[0])

    @pl.when(i == 0)
    def _():
        zero_ref[...] = jnp.zeros(zero_ref.shape, zero_ref.dtype)
        zero_fill(lambda c: c.start())
        zero_fill(lambda c: c.wait())

    pos = pos_ref[...]
    piota = lax.broadcasted_iota(I32, (GROUPED_ROWS, tm), 0).astype(jnp.int16)
    pos16 = pos.astype(jnp.int16)
    one = jnp.ones((GROUPED_ROWS, tm), BF16)
    perm = jnp.zeros((GROUPED_ROWS, tm), BF16)
    for r in range(TOP_K):
        perm = jnp.where(piota == pos16[r:r + 1, :], one, perm)
    cur = lax.rem(i, 2)
    grouped = _pack_exact_bf16_pair(_dot(perm, h2_ref[:, 0:dh]), _dot(perm, h2_ref[:, dh:]))
    cbuf_ref[cur] = grouped.reshape(cbuf_ref.shape[1:])

    def make_copy(buf):
        def build(tile_slab, buffer_slab, slabs):
            return pltpu.make_async_copy(cbuf_ref.at[buf, pl.ds(tile_slab, slabs)],
                                         xs_hbm.at[pl.ds(buffer_slab, slabs)], sem.at[buf])
        return build

    _start_chunks(i, chunk_ref, nch_ref, pstart_ref, make_copy(cur))

    @pl.when(i > 0)
    def _():
        _wait_chunks(i - 1, nch_ref, make_copy(1 - cur))

    @pl.when(i == nt - 1)
    def _():
        _wait_chunks(i, nch_ref, make_copy(cur))


def _dispatch(chunks, nch, pstart, pend, rend, pos, h2, nrows, tm):
    n, d = h2.shape
    grid_spec = pltpu.PrefetchScalarGridSpec(
        num_scalar_prefetch=5,
        grid=(n // tm,),
        in_specs=[
            pl.BlockSpec((TOP_K, tm), lambda i, *_: (0, i)),
            pl.BlockSpec((tm, d), lambda i, *_: (i, 0)),
        ],
        out_specs=pl.BlockSpec(memory_space=pl.ANY),
        scratch_shapes=[
            pltpu.VMEM((2, GROUPED_ROWS // ROW_GROUP, ROW_GROUP, d // 2), U32),
            pltpu.VMEM((ZERO_SLABS, ROW_GROUP, d // 2), U32),
            pltpu.SemaphoreType.DMA(()),
            pltpu.SemaphoreType.DMA((2,)),
        ],
    )
    return pl.pallas_call(
        functools.partial(_dispatch_kernel, tm=tm),
        grid_spec=grid_spec,
        out_shape=jax.ShapeDtypeStruct((nrows // ROW_GROUP, ROW_GROUP, d // 2), U32),
        compiler_params=_params(("arbitrary",)),
        name="moe_dispatch",
    )(chunks, nch, pstart, pend, rend, pos, h2)


def _experts_kernel(be_ref, nu_ref, valid_ref, first_ref, slot_ref, next_ref, x_ref, wgu_hbm, wd_hbm,
                    o_ref, wgu_f_ref, wd_f_ref, wgu_b_ref, wd_b_ref, sem, *, bm):
    i = pl.program_id(0)
    valid = valid_ref[i]
    sub = bm // 2

    def weight_copies(expert, slot):
        return (pltpu.make_async_copy(wgu_hbm.at[expert], wgu_f_ref.at[slot], sem.at[slot, 0]),
                pltpu.make_async_copy(wd_hbm.at[expert], wd_f_ref.at[slot], sem.at[slot, 1]))

    @pl.when(i == 0)
    def _():
        for c in weight_copies(be_ref[0], 0):
            c.start()

    @pl.when(first_ref[i] == 1)
    def _():
        slot = slot_ref[i]
        for c in weight_copies(be_ref[i], slot):
            c.wait()

        @pl.when(next_ref[i] >= 0)
        def _():
            for c in weight_copies(next_ref[i], 1 - slot):
                c.start()

        wgu_b_ref[...] = wgu_f_ref[slot].astype(BF16)
        wd_b_ref[...] = wd_f_ref[slot].astype(BF16)

    for r0 in (0, sub):
        rows = slice(r0, r0 + sub)

        @pl.when(valid > r0)
        def _():
            lo, hi = _unpack_bf16_pair(x_ref[rows, :])
            x = jnp.concatenate([lo.astype(BF16), hi.astype(BF16)], axis=1)
            gu = _dot(x, wgu_b_ref[...])
            act = (_silu(gu[:, 0:D_EXPERT]) * gu[:, D_EXPERT:]).astype(BF16)
            y = _dot(act, wd_b_ref[...])
            o_ref[rows, :] = _pack_bf16_pair(y[:, 0:D_MODEL // 2], y[:, D_MODEL // 2:])

        @pl.when(valid <= r0)
        def _():
            o_ref[rows, :] = jnp.zeros((sub, o_ref.shape[1]), o_ref.dtype)


def _experts(block_e, nused, valid, first, slot, next_e, xs, w_gate_up_l, w_down_l, bm):
    nrows, dh = xs.shape
    d = 2 * dh
    nblk = nrows // bm
    grid_spec = pltpu.PrefetchScalarGridSpec(
        num_scalar_prefetch=6,
        grid=(nblk,),
        in_specs=[
            pl.BlockSpec((bm, dh), lambda i, be, nu, *_: (jnp.minimum(i, nu[0] - 1), 0)),
            pl.BlockSpec(memory_space=pl.ANY),
            pl.BlockSpec(memory_space=pl.ANY),
        ],
        out_specs=pl.BlockSpec((bm, dh), lambda i, *_: (i, 0)),
        scratch_shapes=[
            pltpu.VMEM((2, d, 2 * D_EXPERT), w_gate_up_l.dtype),
            pltpu.VMEM((2, D_EXPERT, d), w_down_l.dtype),
            pltpu.VMEM((d, 2 * D_EXPERT), BF16),
            pltpu.VMEM((D_EXPERT, d), BF16),
            pltpu.SemaphoreType.DMA((2, 2)),
        ],
    )
    return pl.pallas_call(
        functools.partial(_experts_kernel, bm=bm),
        grid_spec=grid_spec,
        out_shape=jax.ShapeDtypeStruct((nrows, dh), U32),
        compiler_params=_params(("arbitrary",)),
        name="moe_experts",
    )(block_e, nused, valid, first, slot, next_e, xs, w_gate_up_l, w_down_l)


def _combine_kernel(chunk_ref, nch_ref, pstart_ref, pos_ref, wl_ref, xs_ref, ga2_ref, gf_ref,
                    yb_hbm, oa_ref, ob_ref, gbuf_ref, sem, *, tm, nta):
    i = pl.program_id(0)
    nt = pl.num_programs(0)
    cur = lax.rem(i, 2)
    refs = (chunk_ref, nch_ref, pstart_ref)

    def make_copy(buf):
        def build(tile_slab, buffer_slab, slabs):
            return pltpu.make_async_copy(yb_hbm.at[pl.ds(buffer_slab, slabs)],
                                         gbuf_ref.at[buf, pl.ds(tile_slab, slabs)], sem.at[buf])
        return build

    @pl.when(i == 0)
    def _():
        gbuf_ref[...] = jnp.zeros(gbuf_ref.shape, gbuf_ref.dtype)
        _start_chunks(0, *refs, make_copy(0))

    @pl.when(i + 1 < nt)
    def _():
        _start_chunks(i + 1, *refs, make_copy(1 - cur))

    _wait_chunks(i, nch_ref, make_copy(cur))

    lo, hi = _unpack_bf16_pair(gbuf_ref[cur].reshape(GROUPED_ROWS, gbuf_ref.shape[-1]))
    g = jnp.concatenate([lo.astype(BF16), hi.astype(BF16)], axis=1)
    pos16 = pos_ref[...].astype(jnp.int16)
    wl = wl_ref[...].astype(BF16)
    liota = lax.broadcasted_iota(I32, (tm, GROUPED_ROWS), 1).astype(jnp.int16)
    a = jnp.zeros((tm, GROUPED_ROWS), BF16)
    for r in range(TOP_K):
        a = jnp.where(liota == pos16[:, r:r + 1], jnp.broadcast_to(wl[:, r:r + 1], a.shape), a)
    routed = _dot(a, g)
    x2 = xs_ref[...] + ga2_ref[...] * routed
    ms = jnp.mean(x2 * x2, axis=-1, keepdims=True)
    y = x2 * lax.rsqrt(ms + EPS) * gf_ref[...]

    @pl.when(i < nta)
    def _():
        oa_ref[...] = y

    @pl.when(i >= nta)
    def _():
        ob_ref[...] = y


def _combine(chunks, nch, pstart, pos_t, wl_t, xs_base, ga2, gfin, yb, tm, nta):
    n, d = xs_base.shape
    ntb = n // tm - nta
    ga2_spec = pl.BlockSpec((None, tm, d), lambda i, *_: (jnp.minimum(i // nta, 1), 0, 0))
    grid_spec = pltpu.PrefetchScalarGridSpec(
        num_scalar_prefetch=3,
        grid=(n // tm,),
        in_specs=[
            pl.BlockSpec((tm, TOP_K), lambda i, *_: (i, 0)),
            pl.BlockSpec((tm, TOP_K), lambda i, *_: (i, 0)),
            pl.BlockSpec((tm, d), lambda i, *_: (i, 0)),
            ga2_spec,
            pl.BlockSpec((1, d), lambda i, *_: (0, 0)),
            pl.BlockSpec(memory_space=pl.ANY),
        ],
        out_specs=[
            pl.BlockSpec((tm, d), lambda i, *_: (jnp.minimum(i, nta - 1), 0)),
            pl.BlockSpec((tm, d), lambda i, *_: (jnp.maximum(i - nta, 0), 0)),
        ],
        scratch_shapes=[
            pltpu.VMEM((2, GROUPED_ROWS // ROW_GROUP, ROW_GROUP, d // 2), U32),
            pltpu.SemaphoreType.DMA((2,)),
        ],
    )
    return pl.pallas_call(
        functools.partial(_combine_kernel, tm=tm, nta=nta),
        grid_spec=grid_spec,
        out_shape=[jax.ShapeDtypeStruct((nta * tm, d), F32), jax.ShapeDtypeStruct((ntb * tm, d), F32)],
        compiler_params=_params(("arbitrary",)),
        name="moe_combine",
    )(chunks, nch, pstart, pos_t, wl_t, xs_base, ga2, gfin, yb)


def _moe_and_final(src_a, src_b, mods, wts, tm, bm):
    nta = src_a[0].shape[0] // tm
    ga1, sh2, sc2, ga2 = mods
    (w_out_b, g_ffn, wsgu_b, wsd_b, wr_t, rb, w_gate_up_l, w_down_l, g_final) = wts
    xs_base, h2, pos, wl, chunks, nch, cnt = _post(
        src_a, src_b, w_out_b, ga1, g_ffn, sc2, sh2, ga2, wsgu_b, wsd_b, wr_t, rb, tm)
    nt = xs_base.shape[0] // tm
    counts = cnt[:, 0].astype(I32)
    padded = (counts + bm - 1) // bm * bm
    pend = jnp.cumsum(padded)
    pstart = pend - padded
    max_rows = nt * (tm * TOP_K + N_EXPERTS * (ROW_GROUP - 1))
    nblk = -(-max_rows // bm) + N_EXPERTS
    nused = (pend[-1] // bm).astype(I32)
    blk_row = jnp.minimum(jnp.arange(nblk, dtype=I32), nused - 1) * bm
    be = jnp.sum((pend[None, :] <= blk_row[:, None]).astype(I32), axis=1)
    chunks = chunks.reshape(-1)
    nch = nch[:, 0, 0]
    pslab = pstart // ROW_GROUP
    xs = _dispatch(chunks, nch, pslab, pend // ROW_GROUP, (pstart + counts) // ROW_GROUP, pos, h2,
                   nblk * bm, tm)
    xs = xs.reshape(nblk * bm, xs.shape[-1])
    region_end = jnp.sum(jnp.where(be[:, None] == jnp.arange(N_EXPERTS, dtype=I32)[None, :],
                                   (pstart + counts)[None, :], 0), axis=1)
    valid = jnp.clip(region_end - jnp.arange(nblk, dtype=I32) * bm, 0, bm)
    blk = jnp.arange(nblk, dtype=I32)
    first = ((blk < nused) & ((blk == 0) | (be != jnp.roll(be, 1)))).astype(I32)
    slot = (jnp.cumsum(first) - 1) % 2
    eids = jnp.arange(N_EXPERTS, dtype=I32)
    later_nonempty = (eids[None, :] > eids[:, None]) & (counts[None, :] > 0)
    next_tab = jnp.min(jnp.where(later_nonempty, eids[None, :], N_EXPERTS), axis=1)
    next_tab = jnp.where(next_tab == N_EXPERTS, -1, next_tab)
    next_e = jnp.sum(jnp.where(be[:, None] == eids[None, :], next_tab[None, :], 0), axis=1)
    yb = _experts(be, nused.reshape(1), valid, first, slot.astype(I32), next_e.astype(I32), xs,
                  w_gate_up_l, w_down_l, bm)
    yb = yb.reshape(nblk * bm // ROW_GROUP, ROW_GROUP, yb.shape[-1])
    return _combine(chunks, nch, pslab, pos.T, wl.T, xs_base, ga2, g_final, yb, tm, nta)


def _expand(mod, reps):
    if mod.shape[0] == 1:
        return mod
    return jnp.repeat(mod, reps, axis=0)


def kernel(x_prompt, x_sample, cache_k, cache_v, state_hgrn, c_prompt, c_sample, w_ada, b_ada,
           norm_mix, norm_ffn, norm_final, w_in, w_out, hg_lb_logits, hg_norm, da_lambda, da_norm,
           rel_bias_table, w_router, router_bias, w_gate_up, w_down, ws_gate_up, ws_down):
    depth = w_in.shape[0]
    assert depth == 1 and hg_lb_logits.shape[0] == 2
    bp, tp, d = x_prompt.shape
    bs, ts, _ = x_sample.shape
    assert bp == 1
    past = cache_k.shape[2]
    l = 0

    rows = -(-(bp + bs) // 8) * 8
    c_all = jnp.zeros((rows, d), F32).at[:bp].set(c_prompt).at[bp:bp + bs].set(c_sample)
    mod = _adaln(c_all, w_ada[l], b_ada[l])
    mod_p = [mod[0:bp, j * d:(j + 1) * d] for j in range(6)]
    mod_s = [_expand(mod[bp:bp + bs, j * d:(j + 1) * d], ts) for j in range(6)]

    w_in_b = w_in[l].astype(BF16)
    w_out_b = w_out[l].astype(BF16)
    wsgu_b = ws_gate_up[l].astype(BF16)
    wsd_b = ws_down[l].astype(BF16)
    wr_t = w_router[l].T
    g_mix = norm_mix[l].reshape(1, d)
    g_ffn = norm_ffn[l].reshape(1, d)
    g_final = norm_final.reshape(1, d)
    moe_w = (w_out_b, g_ffn, wsgu_b, wsd_b, wr_t, router_bias[l], w_gate_up[l], w_down[l], g_final)

    lam = _lam(da_lambda[l])

    t_att = min(ATT_TILE, tp)
    kk = jnp.arange(t_att, dtype=I32)[:, None]
    qq = jnp.arange(t_att, dtype=I32)[None, :]
    idx_diag = jnp.where((kk // CHUNK) <= (qq // CHUNK), _rel_bucket(kk - qq), MASK_BUCKET)
    idx_prev = _rel_bucket(kk - qq - t_att)
    bias_p = _bias_tiles(rel_bias_table, jnp.stack([idx_diag, idx_prev]).astype(I32),
                         shift_bucket=N_BUCKETS // 2 - 1)
    pad = 128
    qpos = past + jnp.arange(ts, dtype=I32)[:, None]
    idx_sp = _rel_bucket(jnp.arange(past, dtype=I32)[None, :] - qpos)
    kn = jnp.arange(pad, dtype=I32)[None, :]
    idx_sn = jnp.where(kn < ts, _rel_bucket(past + kn - qpos), MASK_BUCKET)
    bias_sp = _bias_tiles(rel_bias_table, idx_sp[None].astype(I32), shift_bucket=None)
    bias_sn = _bias_tiles(rel_bias_table, idx_sn[None].astype(I32), shift_bucket=None)

    xp = x_prompt.reshape(bp * tp, d)
    sh1, sc1, ga1, sh2, sc2, ga2 = mod_p
    assert ATT_TILE == INPROJ_TILE
    zh, qt, kf, vf, kb, vt = _inproj(xp, g_mix, sc1, sh1, w_in_b, t_att, True)
    s_zero = jnp.zeros((bp, HG_HEADS, HG_DIM, HG_DIM), F32)
    ohg_p, sp_new = _hgrn(zh, s_zero, hg_lb_logits, hg_norm[l], bp, tp, min(HGRN_CHUNK, tp))
    oda_p = _attn_prompt(kb, qt, vt, bias_p, lam, da_norm[l], t_att)
    src_p = (xp, ohg_p, oda_p)
    mods_p = (ga1, sh2, sc2, ga2)
    k_prompt = kf.reshape(1, bp, tp, DA_HEADS, 2 * DA_QKDIM)
    v_prompt = vf.reshape(1, bp, tp, DA_HEADS, DA_VDIM)

    ns = bs * ts
    xs_ = x_sample.reshape(ns, d)
    sh1, sc1, ga1, sh2, sc2, ga2 = mod_s
    zh, qs, kf, vf, kb, vb = _inproj(xs_, g_mix, sc1, sh1, w_in_b, ns, False)
    ohg_s, ss_new = _hgrn(zh, state_hgrn[l], hg_lb_logits, hg_norm[l], bs, ts, ts)
    oda_s = _attn_step(qs, cache_k[l], cache_v[l], kb, vb, bias_sp, bias_sn, lam, da_norm[l], bs, ts)
    assert ns == POST_TILE
    mods = tuple(jnp.stack([jnp.broadcast_to(mp, (POST_TILE, d)), ms_])
                 for mp, ms_ in zip(mods_p, (ga1, sh2, sc2, ga2)))
    y_p, y_s = _moe_and_final(src_p, (xs_, ohg_s, oda_s), mods, moe_w, POST_TILE, MOE_BLOCK_ROWS)
    k_sample = kf.reshape(1, bs, ts, DA_HEADS, 2 * DA_QKDIM)
    v_sample = vf.reshape(1, bs, ts, DA_HEADS, DA_VDIM)

    return (y_p.reshape(bp, tp, d), y_s.reshape(bs, ts, d), k_prompt, v_prompt, sp_new[None],
            k_sample, v_sample, ss_new[None].astype(x_sample.dtype))
```

```python
import functools
import math

import numpy as np
import jax
import jax.numpy as jnp
from jax import lax
from jax.experimental import pallas as pl
from jax.experimental.pallas import tpu as pltpu

F32 = jnp.float32
BF16 = jnp.bfloat16
I32 = jnp.int32
U32 = jnp.uint32
HIGHEST = lax.Precision.HIGHEST

D_MODEL = 1024
CHUNK = 64
HG_HEADS = 4
HG_DIM = 128
HG_WIDTH = HG_HEADS * HG_DIM
DA_HEADS = 4
DA_VDIM = 128
DA_QKDIM = 64
DA_WIDTH = DA_HEADS * DA_VDIM
N_BUCKETS = 32
MAX_DIST = 128
N_EXPERTS = 64
TOP_K = 8
N_GROUPS = 8
GROUP_SIZE = N_EXPERTS // N_GROUPS
TOP_GROUPS = 4
D_EXPERT = 256
ROUTE_SCALE = 2.5
EPS = 1e-6
LAM_INIT = 0.8 - 0.6 * math.exp(-0.3 * 0)

LOG2E = math.log2(math.e)
HI_MASK = np.uint32(0xFFFF0000)
NEG_BIG = -1e30
MASK_BUCKET = N_BUCKETS
V7X_VMEM_LIMIT = 48 * 1024 * 1024

ATT_TILE = 512
VT_ROWS = DA_VDIM + 16
HGRN_CHUNK = 256
INPROJ_TILE = 512
POST_TILE = 256
MOE_BLOCK_ROWS = 1024
ROW_GROUP = 8
GROUPED_ROWS = -(-(POST_TILE * TOP_K + N_EXPERTS * (ROW_GROUP - 1)) // 256) * 256
CHUNK_SLOTS = -(-(GROUPED_ROWS // ROW_GROUP) // 128) * 128
CHUNK_EXPERT_SHIFT = 24
CHUNK_UNROLL = 4
WAIT_CHUNKS = 16
ZERO_SLABS = 16


def _sigmoid(x):
    return 1.0 / (1.0 + jnp.exp(-x))


def _silu(x):
    return x * _sigmoid(x)


def _dot(a, b, **kw):
    return jnp.dot(a, b, preferred_element_type=F32, **kw)


def _dot_nt(a, b, **kw):
    return lax.dot_general(a, b, (((1,), (1,)), ((), ())), preferred_element_type=F32, **kw)


def _dot_tn(a, b, **kw):
    return lax.dot_general(a, b, (((0,), (0,)), ((), ())), preferred_element_type=F32, **kw)


def _pack_bf16_pair(lo, hi):
    lo_bits = lax.bitcast_convert_type(lo.astype(BF16).astype(F32), U32)
    hi_bits = lax.bitcast_convert_type(hi.astype(BF16).astype(F32), U32)
    return (lo_bits >> 16) | (hi_bits & HI_MASK)


def _pack_exact_bf16_pair(lo, hi):
    return (lax.bitcast_convert_type(lo, U32) >> 16) | (lax.bitcast_convert_type(hi, U32) & HI_MASK)


def _unpack_bf16_pair(w):
    lo = lax.bitcast_convert_type(w << 16, F32)
    hi = lax.bitcast_convert_type(w & HI_MASK, F32)
    return lo, hi


def _params(sem, vmem=V7X_VMEM_LIMIT, flags=None):
    return pltpu.CompilerParams(dimension_semantics=sem, vmem_limit_bytes=vmem, flags=flags)


def _adaln_kernel(c_ref, w_ref, b_ref, o_ref):
    s = _silu(c_ref[...])
    o_ref[...] = _dot(s, w_ref[...], precision=HIGHEST) + b_ref[...]


def _adaln(c_all, w_ada, b_ada):
    rows, d = c_all.shape
    cols = w_ada.shape[1]
    blk = 1024
    return pl.pallas_call(
        _adaln_kernel,
        grid=(cols // blk,),
        in_specs=[
            pl.BlockSpec((rows, d), lambda j: (0, 0)),
            pl.BlockSpec((d, blk), lambda j: (0, j)),
            pl.BlockSpec((1, blk), lambda j: (0, j)),
        ],
        out_specs=pl.BlockSpec((rows, blk), lambda j: (0, j)),
        out_shape=jax.ShapeDtypeStruct((rows, cols), F32),
        compiler_params=_params(("parallel",)),
        name="adaln",
    )(c_all, w_ada, b_ada.reshape(1, cols))


def _lam_kernel(l_ref, o_ref):
    l = l_ref[...].astype(F32)
    a = jnp.sum(l[0:1] * l[1:2], axis=-1, keepdims=True)
    b = jnp.sum(l[2:3] * l[3:4], axis=-1, keepdims=True)
    lam = jnp.exp(a) - jnp.exp(b) + LAM_INIT
    o_ref[...] = jnp.broadcast_to(lam, o_ref.shape)


def _lam(da_lambda_l):
    return pl.pallas_call(
        _lam_kernel,
        out_shape=jax.ShapeDtypeStruct((8, 128), F32),
        name="lam",
    )(da_lambda_l)


def _rel_bucket(rel):
    nb = N_BUCKETS // 2
    max_exact = nb // 2
    side = jnp.where(rel > 0, nb, 0)
    n = jnp.abs(rel)
    large = max_exact + (jnp.log(jnp.maximum(n, 1).astype(F32) / max_exact)
                         / math.log(MAX_DIST / max_exact) * (nb - max_exact)).astype(I32)
    large = jnp.minimum(large, nb - 1)
    return side + jnp.where(n < max_exact, n, large)


def _bias_kernel(tab_ref, idx_ref, o_ref, *, shift_bucket):
    h = pl.program_id(0)
    idx = idx_ref[...]
    shift = tab_ref[shift_bucket, h] if shift_bucket is not None else 0.0
    acc = jnp.zeros(idx.shape, F32)
    for j in range(N_BUCKETS):
        acc = jnp.where(idx == j, (tab_ref[j, h] - shift) * LOG2E, acc)
    o_ref[...] = jnp.where(idx == MASK_BUCKET, NEG_BIG, acc)


def _bias_tiles(table, idx, *, shift_bucket):
    k, r, c = idx.shape
    return pl.pallas_call(
        functools.partial(_bias_kernel, shift_bucket=shift_bucket),
        grid=(DA_HEADS, k),
        in_specs=[
            pl.BlockSpec(memory_space=pltpu.SMEM),
            pl.BlockSpec((None, r, c), lambda h, d: (d, 0, 0)),
        ],
        out_specs=pl.BlockSpec((None, None, r, c), lambda h, d: (h, d, 0, 0)),
        out_shape=jax.ShapeDtypeStruct((DA_HEADS, k, r, c), F32),
        compiler_params=_params(("parallel", "parallel")),
        name="rel_bias",
    )(table, idx)


def _inproj_kernel(x_ref, g_ref, sc_ref, sh_ref, w_ref,
                   zh_ref, q_ref, k_ref, v_ref, kb_ref, vb_ref, *, transposed):
    x = x_ref[...]
    ms = jnp.mean(x * x, axis=-1, keepdims=True)
    h = x * lax.rsqrt(ms + EPS) * g_ref[...]
    h = h * (1.0 + sc_ref[...]) + sh_ref[...]
    hb = h.astype(BF16)
    c0 = 4 * HG_WIDTH
    zh_ref[...] = _dot(hb, w_ref[:, 0:c0])
    zq = _dot(hb, w_ref[:, c0:c0 + DA_WIDTH]) * (DA_QKDIM ** -0.5 * LOG2E)
    zk = _dot(hb, w_ref[:, c0 + DA_WIDTH:c0 + 2 * DA_WIDTH])
    for hd in range(DA_HEADS):
        k_ref[:, hd, :] = zk[:, hd * DA_VDIM:(hd + 1) * DA_VDIM]
    kb_ref[...] = zk.astype(BF16)
    zv = _dot(hb, w_ref[:, c0 + 2 * DA_WIDTH:c0 + 3 * DA_WIDTH])
    for hd in range(DA_HEADS):
        v_ref[:, hd, :] = zv[:, hd * DA_VDIM:(hd + 1) * DA_VDIM]
    if transposed:
        q_ref[...] = zq.T.astype(BF16)
        vb_ref[:, 0:DA_VDIM, :] = zv.T.astype(BF16).reshape(DA_HEADS, DA_VDIM, zv.shape[0])
        vb_ref[:, DA_VDIM:, :] = jnp.ones((DA_HEADS, VT_ROWS - DA_VDIM, zv.shape[0]), BF16)
    else:
        q_ref[...] = zq.astype(BF16)
        vb_ref[...] = zv.astype(BF16)


def _mod_spec(mod, tm):
    if mod.shape[0] == 1:
        return pl.BlockSpec((1, mod.shape[1]), lambda i: (0, 0))
    return pl.BlockSpec((tm, mod.shape[1]), lambda i: (i, 0))


def _inproj(x, g, sc, sh, w_in_b, tm, transposed):
    n, d = x.shape
    cols = w_in_b.shape[1]
    row = lambda i: (i, 0)
    if transposed:
        q_spec = pl.BlockSpec((DA_WIDTH, tm), lambda i: (0, i))
        q_shape = jax.ShapeDtypeStruct((DA_WIDTH, n), BF16)
        vb_spec = pl.BlockSpec((DA_HEADS, None, VT_ROWS, tm), lambda i: (0, i, 0, 0))
        vb_shape = jax.ShapeDtypeStruct((DA_HEADS, n // tm, VT_ROWS, tm), BF16)
    else:
        q_spec = vb_spec = pl.BlockSpec((tm, DA_WIDTH), row)
        q_shape = vb_shape = jax.ShapeDtypeStruct((n, DA_WIDTH), BF16)
    return pl.pallas_call(
        functools.partial(_inproj_kernel, transposed=transposed),
        grid=(n // tm,),
        in_specs=[
            pl.BlockSpec((tm, d), row),
            pl.BlockSpec((1, d), lambda i: (0, 0)),
            _mod_spec(sc, tm),
            _mod_spec(sh, tm),
            pl.BlockSpec((d, cols), lambda i: (0, 0)),
        ],
        out_specs=[
            pl.BlockSpec((tm, 4 * HG_WIDTH), row),
            q_spec,
            pl.BlockSpec((tm, DA_HEADS, DA_VDIM), lambda i: (i, 0, 0)),
            pl.BlockSpec((tm, DA_HEADS, DA_VDIM), lambda i: (i, 0, 0)),
            pl.BlockSpec((tm, DA_WIDTH), row),
            vb_spec,
        ],
        out_shape=[
            jax.ShapeDtypeStruct((n, 4 * HG_WIDTH), F32),
            q_shape,
            jax.ShapeDtypeStruct((n, DA_HEADS, DA_VDIM), F32),
            jax.ShapeDtypeStruct((n, DA_HEADS, DA_VDIM), F32),
            jax.ShapeDtypeStruct((n, DA_WIDTH), BF16),
            vb_shape,
        ],
        compiler_params=_params(("parallel",)),
        name="inproj",
    )(x, g, sc, sh, w_in_b)


def _hgrn_consts(c):
    levels = int(round(math.log2(c)))
    assert 1 << levels == c and levels >= 3
    t = np.arange(c)[:, None]
    r = np.arange(c)[None, :]
    tri = (r <= t).astype(np.float32)
    x = np.maximum(t ^ r, 1)
    lv = np.where(t == r, -1, np.where(t > r, np.floor(np.log2(x)).astype(np.int64), -2))
    return jnp.asarray(tri, dtype=BF16), jnp.asarray(lv, dtype=I32), levels


def _hgrn_kernel(zh_ref, s0_ref, lbl_ref, gain_ref, mall_ref, lv_ref,
                 o_ref, sout_ref, st_ref, b_ref, *, c, levels):
    ci = pl.program_id(1)

    @pl.when(ci == 0)
    def _():
        for h in range(HG_HEADS):
            st_ref[h] = s0_ref[h].astype(F32).T

    lbl = lbl_ref[...].astype(F32)
    mx = jnp.maximum(lbl[0:1], lbl[1:2])
    e0 = jnp.exp(lbl[0:1] - mx)
    e1 = jnp.exp(lbl[1:2] - mx)
    lb = e0 / (e0 + e1)

    xq = zh_ref[:, 0:HG_WIDTH]
    xf = zh_ref[:, HG_WIDTH:2 * HG_WIDTH]
    q = _silu(xq)
    y = lb + (1.0 - lb) * _sigmoid(xf)
    logf = jnp.log(y)
    kk = 1.0 - y

    l1 = logf.astype(BF16)
    r1 = logf - l1.astype(F32)
    l2 = r1.astype(BF16)
    l3 = (r1 - l2.astype(F32)).astype(BF16)
    tri = mall_ref[...]
    b = _dot(tri, l1) + _dot(tri, l2) + _dot(tri, l3)
    b_ref[...] = b
    trow = lax.broadcasted_iota(I32, (c, HG_DIM), 0)

    def level_factor(l, sl):
        m = 1 << l
        later = (trow & m) != 0
        lf = logf[:, sl]
        if l == 0:
            e = jnp.where(later, lf, 0.0)
        elif l == 1:
            below = pltpu.roll(lf, 1, 0)
            above = pltpu.roll(lf, c - 1, 0)
            low = (trow & 1) != 0
            e = jnp.where(later, jnp.where(low, lf + below, lf), jnp.where(low, 0.0, above))
        else:
            mid = jnp.concatenate(
                [jnp.broadcast_to(b_ref[k * 2 * m + m - 1:k * 2 * m + m, sl], (2 * m, HG_DIM))
                 for k in range(c // (2 * m))], axis=0)
            e = jnp.where(later, b[:, sl] - mid, mid - b[:, sl])
        return jnp.exp(e)

    lv = lv_ref[...].astype(jnp.int16)
    gain = gain_ref[...].astype(F32)
    for h in range(HG_HEADS):
        sl = slice(h * HG_DIM, (h + 1) * HG_DIM)
        qh = q[:, sl]
        kh = kk[:, sl]
        ih = zh_ref[:, 2 * HG_WIDTH + h * HG_DIM:2 * HG_WIDTH + (h + 1) * HG_DIM]
        gh = zh_ref[:, 3 * HG_WIDTH + h * HG_DIM:3 * HG_WIDTH + (h + 1) * HG_DIM]
        bh = b[:, sl]
        ihb = ih.astype(BF16)
        a = jnp.where(lv == -1, _dot_nt(qh.astype(BF16), kh.astype(BF16)).astype(BF16),
                      jnp.zeros((c, c), BF16))
        for l in range(levels):
            f = level_factor(l, sl)
            p = _dot_nt((qh * f).astype(BF16), (kh * f).astype(BF16))
            a = jnp.where(lv == l, p.astype(BF16), a)
        st = st_ref[h]
        o = _dot(a, ihb) + _dot_nt((qh * jnp.exp(bh)).astype(BF16), st.astype(BF16))
        bl = bh[c - 1:c, :]
        kd = (kh * jnp.exp(bl - bh)).astype(BF16)
        st_ref[h] = st * jnp.exp(bl) + _dot_tn(ihb, kd)
        ms = jnp.mean(o * o, axis=-1, keepdims=True)
        on = o * lax.rsqrt(ms + EPS) * gain
        o_ref[:, sl] = (on * _silu(gh)).astype(o_ref.dtype)

    @pl.when(ci == pl.num_programs(1) - 1)
    def _():
        for h in range(HG_HEADS):
            sout_ref[h] = st_ref[h].T.astype(sout_ref.dtype)


def _hgrn(zh, s0, lb_logits, gain, batch, seq, c):
    mall, lv, levels = _hgrn_consts(c)
    nc = seq // c
    return pl.pallas_call(
        functools.partial(_hgrn_kernel, c=c, levels=levels),
        grid=(batch, nc),
        in_specs=[
            pl.BlockSpec((c, 4 * HG_WIDTH), lambda b, i: (b * nc + i, 0)),
            pl.BlockSpec((None, HG_HEADS, HG_DIM, HG_DIM), lambda b, i: (b, 0, 0, 0)),
            pl.BlockSpec(lb_logits.shape, lambda b, i: (0, 0)),
            pl.BlockSpec((1, HG_DIM), lambda b, i: (0, 0)),
            pl.BlockSpec(mall.shape, lambda b, i: (0, 0)),
            pl.BlockSpec(lv.shape, lambda b, i: (0, 0)),
        ],
        out_specs=[
            pl.BlockSpec((c, HG_WIDTH), lambda b, i: (b * nc + i, 0)),
            pl.BlockSpec((None, HG_HEADS, HG_DIM, HG_DIM), lambda b, i: (b, 0, 0, 0)),
        ],
        out_shape=[
            jax.ShapeDtypeStruct((batch * seq, HG_WIDTH), BF16),
            jax.ShapeDtypeStruct((batch, HG_HEADS, HG_DIM, HG_DIM), F32),
        ],
        scratch_shapes=[pltpu.VMEM((HG_HEADS, HG_DIM, HG_DIM), F32),
                        pltpu.VMEM((c, HG_WIDTH), F32)],
        compiler_params=_params(("parallel", "arbitrary")),
        name="hgrn2",
    )(zh, s0, lb_logits, gain.reshape(1, HG_DIM), mall, lv)


def _attn_kernel(k_ref, qt_ref, vt_ref, bias_ref, lam_ref, gain_ref,
                 o_ref, qz_ref, m_ref, acc_ref, s_ref, smax_ref, *, t):
    i = pl.program_id(1)
    qt = qt_ref[...]
    row = lax.broadcasted_iota(I32, qt.shape, 0)
    zero = jnp.zeros_like(qt)
    qz_ref[:, 0:t] = jnp.where(row < DA_QKDIM, qt, zero)
    qz_ref[:, t:2 * t] = jnp.where(row >= DA_QKDIM, qt, zero)
    m_ref[...] = jnp.full(m_ref.shape, NEG_BIG, F32)
    acc_ref[...] = jnp.zeros(acc_ref.shape, F32)

    def scores(j, buf):
        kt = k_ref[pl.ds(pl.multiple_of(j * t, t), t), :]
        s = _dot(kt, qz_ref[...])
        s_ref[buf] = s
        smax_ref[buf] = jnp.max(s, axis=0, keepdims=True)

    def consume(j, buf, bias_idx):
        s = s_ref[buf]
        if bias_idx is not None:
            b = bias_ref[bias_idx]
            s = jnp.concatenate([s[:, 0:t] + b, s[:, t:2 * t] + b], axis=1)
            s_max = jnp.max(s, axis=0, keepdims=True)
        else:
            s_max = smax_ref[buf]
        m_prev = m_ref[...]
        m_new = jnp.maximum(m_prev, s_max)
        alpha = jnp.exp2(m_prev - m_new)
        pr = jnp.exp2(s - m_new).astype(BF16)
        acc_ref[...] = alpha * acc_ref[...] + _dot(vt_ref[j], pr)
        m_ref[...] = m_new

    n_far = jnp.maximum(i - 1, 0)

    @pl.when(i >= 1)
    def _():
        scores(i - 1, 0)
        scores(i, 1)
        consume(i - 1, 0, 1)
        scores(0, 0)
        consume(i, 1, 0)

    @pl.when(i == 0)
    def _():
        scores(i, 1)
        consume(i, 1, 0)

    def far_tiles(j, count):
        for u in range(count):
            nxt = j + u + 1
            if u == count - 1:
                nxt = jnp.minimum(nxt, n_far - 1)
            scores(nxt, (u + 1) % 2)
            consume(j + u, u % 2, None)

    def far_quad(p, carry):
        far_tiles(4 * p, 4)
        return carry

    def far_pair(p, carry):
        far_tiles(4 * quads + 2 * p, 2)
        return carry

    quads = n_far // 4
    lax.fori_loop(0, quads, far_quad, 0)
    lax.fori_loop(0, (n_far - 4 * quads) // 2, far_pair, 0)

    @pl.when(lax.rem(n_far, 2) == 1)
    def _():
        consume(n_far - 1, 0, None)

    lam = lam_ref[0:1, 0:1]
    acc = acc_ref[0:DA_VDIM, :]
    l = acc_ref[DA_VDIM:DA_VDIM + 1, :]
    o = acc[:, 0:t] / l[:, 0:t] - lam * (acc[:, t:2 * t] / l[:, t:2 * t])
    ms = jnp.mean(o * o, axis=0, keepdims=True)
    on = o * lax.rsqrt(ms + EPS) * gain_ref[...].astype(F32) * (1.0 - LAM_INIT)
    o_ref[...] = on.T.astype(o_ref.dtype)


def _attn_prompt(kb, qt, vt, bias, lam, gain, t):
    n = kb.shape[0]
    nt = n // t
    return pl.pallas_call(
        functools.partial(_attn_kernel, t=t),
        grid=(DA_HEADS, nt),
        in_specs=[
            pl.BlockSpec((n, DA_VDIM), lambda h, i: (0, h)),
            pl.BlockSpec((DA_VDIM, t), lambda h, i: (h, i)),
            pl.BlockSpec((None, nt, VT_ROWS, t), lambda h, i: (h, 0, 0, 0)),
            pl.BlockSpec((None, 2, t, t), lambda h, i: (h, 0, 0, 0)),
            pl.BlockSpec((8, 128), lambda h, i: (0, 0)),
            pl.BlockSpec((DA_VDIM, 1), lambda h, i: (0, 0)),
        ],
        out_specs=pl.BlockSpec((t, DA_VDIM), lambda h, i: (i, h)),
        out_shape=jax.ShapeDtypeStruct((n, DA_WIDTH), BF16),
        scratch_shapes=[
            pltpu.VMEM((DA_VDIM, 2 * t), BF16),
            pltpu.VMEM((1, 2 * t), F32),
            pltpu.VMEM((VT_ROWS, 2 * t), F32),
            pltpu.VMEM((2, t, 2 * t), F32),
            pltpu.VMEM((2, 1, 2 * t), F32),
        ],
        compiler_params=_params(("parallel", "parallel")),
        name="diff_attn_prompt",
    )(kb, qt, vt, bias, lam, gain.reshape(DA_VDIM, 1))


def _attn_step_kernel(q_ref, kp_ref, vp_ref, kn_ref, vn_ref, bp_ref, bn_ref, lam_ref, gain_ref,
                      o_ref, *, tq, pad):
    lam = lam_ref[0:1, 0:1]
    gain = gain_ref[...].astype(F32)
    zpad = jnp.zeros((pad - tq, DA_VDIM), BF16)
    for h in range(DA_HEADS):
        hs = slice(h * DA_VDIM, (h + 1) * DA_VDIM)
        q = q_ref[:, hs]
        lane = lax.broadcasted_iota(I32, q.shape, 1)
        zero = jnp.zeros_like(q)
        qz = jnp.concatenate([jnp.where(lane < DA_QKDIM, q, zero),
                              jnp.where(lane >= DA_QKDIM, q, zero)], axis=0)
        kp = kp_ref[:, h, :].astype(BF16)
        vp = vp_ref[:, h, :].astype(BF16)
        kn = jnp.concatenate([kn_ref[:, hs], zpad], axis=0)
        vn = jnp.concatenate([vn_ref[:, hs], zpad], axis=0)
        bp = bp_ref[h, 0]
        bn = bn_ref[h, 0]
        sp = _dot_nt(qz, kp) + jnp.concatenate([bp, bp], axis=0)
        sn = _dot_nt(qz, kn) + jnp.concatenate([bn, bn], axis=0)
        m = jnp.maximum(jnp.max(sp, axis=-1, keepdims=True), jnp.max(sn, axis=-1, keepdims=True))
        pp = jnp.exp2(sp - m)
        pn = jnp.exp2(sn - m)
        l = jnp.sum(pp, axis=-1, keepdims=True) + jnp.sum(pn, axis=-1, keepdims=True)
        acc = _dot(pp.astype(BF16), vp) + _dot(pn.astype(BF16), vn)
        on = acc / l
        o = on[0:tq] - lam * on[tq:2 * tq]
        ms = jnp.mean(o * o, axis=-1, keepdims=True)
        o = o * lax.rsqrt(ms + EPS) * gain * (1.0 - LAM_INIT)
        o_ref[:, hs] = o.astype(o_ref.dtype)


def _attn_step(qs, cache_k_l, cache_v_l, kb, vb, bias_p, bias_n, lam, gain, batch, tq):
    past = cache_k_l.shape[1]
    pad = bias_n.shape[-1]
    cache_spec = pl.BlockSpec((None, past, DA_HEADS, DA_VDIM), lambda b: (b, 0, 0, 0))
    row = pl.BlockSpec((tq, DA_WIDTH), lambda b: (b, 0))
    return pl.pallas_call(
        functools.partial(_attn_step_kernel, tq=tq, pad=pad),
        grid=(batch,),
        in_specs=[
            row, cache_spec, cache_spec, row, row,
            pl.BlockSpec(bias_p.shape, lambda b: (0, 0, 0, 0)),
            pl.BlockSpec(bias_n.shape, lambda b: (0, 0, 0, 0)),
            pl.BlockSpec((8, 128), lambda b: (0, 0)),
            pl.BlockSpec((1, DA_VDIM), lambda b: (0, 0)),
        ],
        out_specs=row,
        out_shape=jax.ShapeDtypeStruct((batch * tq, DA_WIDTH), BF16),
        compiler_params=_params(("parallel",)),
        name="diff_attn_step",
    )(qs, cache_k_l, cache_v_l, kb, vb, bias_p, bias_n, lam, gain.reshape(1, DA_VDIM))


def _post_kernel(xa_ref, ohga_ref, odaa_ref, xb_ref, ohgb_ref, odab_ref,
                 wout_ref, ga1_ref, g_ref, sc_ref, sh_ref, ga2_ref,
                 wsgu_ref, wsd_ref, wrt_ref, rb_ref, tri_ref, ltri_ref,
                 xs_ref, h2_ref, pos_ref, wl_ref, chunk_ref, nch_ref, cnt_ref, carry_ref, *, tm, nta):
    i = pl.program_id(0)

    @pl.when(i == 0)
    def _():
        carry_ref[...] = jnp.zeros(carry_ref.shape, F32)

    second = i >= nta
    x = jnp.where(second, xb_ref[...], xa_ref[...])
    ohg = jnp.where(second, ohgb_ref[...], ohga_ref[...])
    oda = jnp.where(second, odab_ref[...], odaa_ref[...])
    mix = _dot(ohg, wout_ref[0:HG_WIDTH, :]) + _dot(oda, wout_ref[HG_WIDTH:, :])
    x1 = x + ga1_ref[...] * mix
    ms = jnp.mean(x1 * x1, axis=-1, keepdims=True)
    h2 = x1 * lax.rsqrt(ms + EPS) * g_ref[...]
    h2 = h2 * (1.0 + sc_ref[...]) + sh_ref[...]
    h2b = h2.astype(BF16)
    h2_ref[...] = h2b
    gu = _dot(h2b, wsgu_ref[...])
    act = (_silu(gu[:, 0:D_EXPERT]) * gu[:, D_EXPERT:]).astype(BF16)
    xs_ref[...] = x1 + ga2_ref[...] * _dot(act, wsd_ref[...])

    logits = _dot_nt(wrt_ref[...], h2, precision=HIGHEST)
    score = _sigmoid(logits)
    sel = score + rb_ref[...]
    sub = lax.broadcasted_iota(I32, (GROUP_SIZE, tm), 0)
    gscore = []
    for g in range(N_GROUPS):
        v = sel[g * GROUP_SIZE:(g + 1) * GROUP_SIZE, :]
        m1 = jnp.max(v, axis=0, keepdims=True)
        i1 = jnp.min(jnp.where(v == m1, sub, GROUP_SIZE), axis=0, keepdims=True)
        m2 = jnp.max(jnp.where(sub == i1, -jnp.inf, v), axis=0, keepdims=True)
        gscore.append(m1 + m2)
    gsel = []
    for g in range(N_GROUPS):
        ahead = jnp.zeros((1, tm), F32)
        for g2 in range(N_GROUPS):
            if g2 == g:
                continue
            tie = 1.0 if g2 < g else 0.0
            ahead = ahead + jnp.where(gscore[g2] > gscore[g], 1.0,
                                      jnp.where(gscore[g2] == gscore[g], tie, 0.0))
        gsel.append(ahead < TOP_GROUPS)
    selm = jnp.concatenate(
        [jnp.where(gsel[g], sel[g * GROUP_SIZE:(g + 1) * GROUP_SIZE, :], -jnp.inf)
         for g in range(N_GROUPS)], axis=0)
    eio = lax.broadcasted_iota(I32, (N_EXPERTS, tm), 0)
    ahead = jnp.zeros((N_EXPERTS, tm), F32)
    for e2 in range(N_EXPERTS):
        row = selm[e2:e2 + 1, :]
        tie = jnp.where(eio > e2, 1.0, 0.0)
        ahead = ahead + jnp.where(row > selm, 1.0, jnp.where(row == selm, tie, 0.0))
    chosen = jnp.where(selm > -jnp.inf, jnp.where(ahead < TOP_K, 1.0, 0.0), 0.0)
    w = chosen * score
    wn = w / jnp.sum(w, axis=0, keepdims=True) * ROUTE_SCALE

    chb = chosen.astype(BF16)
    before = _dot(chb, tri_ref[...])
    tot = _dot(chb, jnp.ones((tm, 128), BF16))
    run = jnp.floor((tot + (ROW_GROUP - 1)) * (1.0 / ROW_GROUP)) * ROW_GROUP
    tile_base = _dot(ltri_ref[...], run.astype(BF16))
    carry = carry_ref[...]
    carry_ref[...] = carry + run
    cnt_ref[...] = carry + run
    pos = jnp.concatenate([tile_base] * (tm // 128), axis=1) + before

    widen = lambda v: jnp.concatenate([v] * (CHUNK_SLOTS // 128), axis=1)
    crow = lax.broadcasted_iota(I32, (N_EXPERTS, CHUNK_SLOTS), 1).astype(F32) * ROW_GROUP
    erow = lax.broadcasted_iota(I32, (N_EXPERTS, CHUNK_SLOTS), 0).astype(F32)
    owner = jnp.sum(jnp.where(widen(tile_base + run) <= crow, 1.0, 0.0), axis=0, keepdims=True)
    region_row = jnp.sum(jnp.where(owner == erow, widen(carry - tile_base), 0.0),
                         axis=0, keepdims=True) + crow[0:1]
    region_slab = (region_row * (1.0 / ROW_GROUP)).astype(I32)
    chunk_ref[...] = owner.astype(I32) * (1 << CHUNK_EXPERT_SHIFT) + region_slab
    nch_ref[...] = jnp.sum(run * (1.0 / ROW_GROUP), axis=0, keepdims=True).astype(I32)

    for r in range(TOP_K):
        pick = jnp.where(ahead == r, chosen, 0.0)
        pos_ref[r:r + 1, :] = jnp.sum(pick * pos, axis=0, keepdims=True).astype(I32)
        wl_ref[r:r + 1, :] = jnp.sum(pick * wn, axis=0, keepdims=True)


def _post(src_a, src_b, w_out_b, ga1, g, sc, sh, ga2, wsgu_b, wsd_b, wr_t, rb, tm):
    (xa, ohga, odaa), (xb, ohgb, odab) = src_a, src_b
    d = xa.shape[1]
    nta, ntb = xa.shape[0] // tm, xb.shape[0] // tm
    nt = nta + ntb
    n = nt * tm
    tri = jnp.asarray(np.triu(np.ones((tm, tm), np.float32), k=1), dtype=BF16)
    ltri = jnp.asarray(np.tril(np.ones((N_EXPERTS, N_EXPERTS), np.float32), k=-1), dtype=BF16)
    row = lambda i: (i, 0)
    row_a = lambda i: (jnp.minimum(i, nta - 1), 0)
    row_b = lambda i: (jnp.maximum(i - nta, 0), 0)
    col = lambda i: (0, i)
    full = lambda i: (0, 0)
    mod = pl.BlockSpec((None, tm, d), lambda i: (jnp.minimum(i // nta, 1), 0, 0))
    return pl.pallas_call(
        functools.partial(_post_kernel, tm=tm, nta=nta),
        grid=(nt,),
        in_specs=[
            pl.BlockSpec((tm, d), row_a),
            pl.BlockSpec((tm, HG_WIDTH), row_a),
            pl.BlockSpec((tm, DA_WIDTH), row_a),
            pl.BlockSpec((tm, d), row_b),
            pl.BlockSpec((tm, HG_WIDTH), row_b),
            pl.BlockSpec((tm, DA_WIDTH), row_b),
            pl.BlockSpec(w_out_b.shape, full),
            mod,
            pl.BlockSpec((1, d), full),
            mod,
            mod,
            mod,
            pl.BlockSpec(wsgu_b.shape, full),
            pl.BlockSpec(wsd_b.shape, full),
            pl.BlockSpec(wr_t.shape, full),
            pl.BlockSpec((N_EXPERTS, 1), full),
            pl.BlockSpec((tm, tm), full),
            pl.BlockSpec((N_EXPERTS, N_EXPERTS), full),
        ],
        out_specs=[
            pl.BlockSpec((tm, d), row),
            pl.BlockSpec((tm, d), row),
            pl.BlockSpec((TOP_K, tm), col),
            pl.BlockSpec((TOP_K, tm), col),
            pl.BlockSpec((None, 1, CHUNK_SLOTS), lambda i: (i, 0, 0)),
            pl.BlockSpec((None, 1, 128), lambda i: (i, 0, 0)),
            pl.BlockSpec((N_EXPERTS, 128), full),
        ],
        out_shape=[
            jax.ShapeDtypeStruct((n, d), F32),
            jax.ShapeDtypeStruct((n, d), BF16),
            jax.ShapeDtypeStruct((TOP_K, n), I32),
            jax.ShapeDtypeStruct((TOP_K, n), F32),
            jax.ShapeDtypeStruct((nt, 1, CHUNK_SLOTS), I32),
            jax.ShapeDtypeStruct((nt, 1, 128), I32),
            jax.ShapeDtypeStruct((N_EXPERTS, 128), F32),
        ],
        scratch_shapes=[pltpu.VMEM((N_EXPERTS, 128), F32)],
        compiler_params=_params(("arbitrary",)),
        name="post_mix_router",
    )(xa, ohga, odaa, xb, ohgb, odab, w_out_b, ga1, g, sc, sh, ga2, wsgu_b, wsd_b, wr_t,
      rb.reshape(N_EXPERTS, 1), tri, ltri)


def _start_chunks(tile, chunk_ref, nch_ref, pslab_ref, make_copy):
    n = nch_ref[tile]

    def start(c):
        word = chunk_ref[tile * CHUNK_SLOTS + c]
        expert = lax.shift_right_logical(word, CHUNK_EXPERT_SHIFT)
        region_slab = word & ((1 << CHUNK_EXPERT_SHIFT) - 1)
        make_copy(c, pslab_ref[expert] + region_slab, 1).start()

    def group(g, carry):
        for u in range(CHUNK_UNROLL):
            start(g * CHUNK_UNROLL + u)
        return carry

    def single(c, carry):
        start(c)
        return carry

    groups = n // CHUNK_UNROLL
    lax.fori_loop(0, groups, group, 0)
    lax.fori_loop(groups * CHUNK_UNROLL, n, single, 0)


def _wait_chunks(tile, nch_ref, make_copy):
    n = nch_ref[tile]
    many = n // WAIT_CHUNKS

    def wait_many(j, carry):
        make_copy(0, 0, WAIT_CHUNKS).wait()
        return carry

    def wait_one(j, carry):
        make_copy(0, 0, 1).wait()
        return carry

    lax.fori_loop(0, many, wait_many, 0)
    lax.fori_loop(many * WAIT_CHUNKS, n, wait_one, 0)


def _dispatch_kernel(chunk_ref, nch_ref, pstart_ref, pend_ref, rend_ref, pos_ref, h2_ref, xs_hbm,
                     cbuf_ref, zero_ref, zsem, sem, *, tm):
    i = pl.program_id(0)
    nt = pl.num_programs(0)
    dh = cbuf_ref.shape[-1]

    def zero_piece(slab):
        return pltpu.make_async_copy(zero_ref, xs_hbm.at[pl.ds(slab, ZERO_SLABS)], zsem)

    def zero_fill(op):
        def pieces(start, stop):
            first = (start // ZERO_SLABS) * ZERO_SLABS

            def piece(k, carry):
                op(zero_piece(first + k * ZERO_SLABS))
                return carry

            lax.fori_loop(0, (stop - first) // ZERO_SLABS, piece, 0)

        def region(e, carry):
            pieces(rend_ref[e], pend_ref[e])
            return carry

        lax.fori_loop(0, N_EXPERTS, region, 0)
        pieces(pend_ref[N_EXPERTS - 1], xs_hbm.shape[0])

    @pl.when(i == 0)
    def _():
        zero_ref[...] = jnp.zeros(zero_ref.shape, zero_ref.dtype)
        zero_fill(lambda c: c.start())
        zero_fill(lambda c: c.wait())

    pos = pos_ref[...]
    piota = lax.broadcasted_iota(I32, (GROUPED_ROWS, tm), 0).astype(jnp.int16)
    pos16 = pos.astype(jnp.int16)
    one = jnp.ones((GROUPED_ROWS, tm), BF16)
    perm = jnp.zeros((GROUPED_ROWS, tm), BF16)
    for r in range(TOP_K):
        perm = jnp.where(piota == pos16[r:r + 1, :], one, perm)
    cur = lax.rem(i, 2)
    grouped = _pack_exact_bf16_pair(_dot(perm, h2_ref[:, 0:dh]), _dot(perm, h2_ref[:, dh:]))
    cbuf_ref[cur] = grouped.reshape(cbuf_ref.shape[1:])

    def make_copy(buf):
        def build(tile_slab, buffer_slab, slabs):
            return pltpu.make_async_copy(cbuf_ref.at[buf, pl.ds(tile_slab, slabs)],
                                         xs_hbm.at[pl.ds(buffer_slab, slabs)], sem.at[buf])
        return build

    _start_chunks(i, chunk_ref, nch_ref, pstart_ref, make_copy(cur))

    @pl.when(i > 0)
    def _():
        _wait_chunks(i - 1, nch_ref, make_copy(1 - cur))

    @pl.when(i == nt - 1)
    def _():
        _wait_chunks(i, nch_ref, make_copy(cur))


def _dispatch(chunks, nch, pstart, pend, rend, pos, h2, nrows, tm):
    n, d = h2.shape
    grid_spec = pltpu.PrefetchScalarGridSpec(
        num_scalar_prefetch=5,
        grid=(n // tm,),
        in_specs=[
            pl.BlockSpec((TOP_K, tm), lambda i, *_: (0, i)),
            pl.BlockSpec((tm, d), lambda i, *_: (i, 0)),
        ],
        out_specs=pl.BlockSpec(memory_space=pl.ANY),
        scratch_shapes=[
            pltpu.VMEM((2, GROUPED_ROWS // ROW_GROUP, ROW_GROUP, d // 2), U32),
            pltpu.VMEM((ZERO_SLABS, ROW_GROUP, d // 2), U32),
            pltpu.SemaphoreType.DMA(()),
            pltpu.SemaphoreType.DMA((2,)),
        ],
    )
    return pl.pallas_call(
        functools.partial(_dispatch_kernel, tm=tm),
        grid_spec=grid_spec,
        out_shape=jax.ShapeDtypeStruct((nrows // ROW_GROUP, ROW_GROUP, d // 2), U32),
        compiler_params=_params(("arbitrary",)),
        name="moe_dispatch",
    )(chunks, nch, pstart, pend, rend, pos, h2)


def _experts_kernel(be_ref, nu_ref, valid_ref, first_ref, slot_ref, next_ref, x_ref, wgu_hbm, wd_hbm,
                    o_ref, wgu_f_ref, wd_f_ref, wgu_b_ref, wd_b_ref, sem, *, bm):
    i = pl.program_id(0)
    valid = valid_ref[i]
    sub = bm // 2

    def weight_copies(expert, slot):
        return (pltpu.make_async_copy(wgu_hbm.at[expert], wgu_f_ref.at[slot], sem.at[slot, 0]),
                pltpu.make_async_copy(wd_hbm.at[expert], wd_f_ref.at[slot], sem.at[slot, 1]))

    @pl.when(i == 0)
    def _():
        for c in weight_copies(be_ref[0], 0):
            c.start()

    @pl.when(first_ref[i] == 1)
    def _():
        slot = slot_ref[i]
        for c in weight_copies(be_ref[i], slot):
            c.wait()

        @pl.when(next_ref[i] >= 0)
        def _():
            for c in weight_copies(next_ref[i], 1 - slot):
                c.start()

        wgu_b_ref[...] = wgu_f_ref[slot].astype(BF16)
        wd_b_ref[...] = wd_f_ref[slot].astype(BF16)

    for r0 in (0, sub):
        rows = slice(r0, r0 + sub)

        @pl.when(valid > r0)
        def _():
            lo, hi = _unpack_bf16_pair(x_ref[rows, :])
            x = jnp.concatenate([lo.astype(BF16), hi.astype(BF16)], axis=1)
            gu = _dot(x, wgu_b_ref[...])
            act = (_silu(gu[:, 0:D_EXPERT]) * gu[:, D_EXPERT:]).astype(BF16)
            y = _dot(act, wd_b_ref[...])
            o_ref[rows, :] = _pack_bf16_pair(y[:, 0:D_MODEL // 2], y[:, D_MODEL // 2:])

        @pl.when(valid <= r0)
        def _():
            o_ref[rows, :] = jnp.zeros((sub, o_ref.shape[1]), o_ref.dtype)


def _experts(block_e, nused, valid, first, slot, next_e, xs, w_gate_up_l, w_down_l, bm):
    nrows, dh = xs.shape
    d = 2 * dh
    nblk = nrows // bm
    grid_spec = pltpu.PrefetchScalarGridSpec(
        num_scalar_prefetch=6,
        grid=(nblk,),
        in_specs=[
            pl.BlockSpec((bm, dh), lambda i, be, nu, *_: (jnp.minimum(i, nu[0] - 1), 0)),
            pl.BlockSpec(memory_space=pl.ANY),
            pl.BlockSpec(memory_space=pl.ANY),
        ],
        out_specs=pl.BlockSpec((bm, dh), lambda i, *_: (i, 0)),
        scratch_shapes=[
            pltpu.VMEM((2, d, 2 * D_EXPERT), w_gate_up_l.dtype),
            pltpu.VMEM((2, D_EXPERT, d), w_down_l.dtype),
            pltpu.VMEM((d, 2 * D_EXPERT), BF16),
            pltpu.VMEM((D_EXPERT, d), BF16),
            pltpu.SemaphoreType.DMA((2, 2)),
        ],
    )
    return pl.pallas_call(
        functools.partial(_experts_kernel, bm=bm),
        grid_spec=grid_spec,
        out_shape=jax.ShapeDtypeStruct((nrows, dh), U32),
        compiler_params=_params(("arbitrary",)),
        name="moe_experts",
    )(block_e, nused, valid, first, slot, next_e, xs, w_gate_up_l, w_down_l)


def _combine_kernel(chunk_ref, nch_ref, pstart_ref, pos_ref, wl_ref, xs_ref, ga2_ref, gf_ref,
                    yb_hbm, oa_ref, ob_ref, gbuf_ref, sem, *, tm, nta):
    i = pl.program_id(0)
    nt = pl.num_programs(0)
    cur = lax.rem(i, 2)
    refs = (chunk_ref, nch_ref, pstart_ref)

    def make_copy(buf):
        def build(tile_slab, buffer_slab, slabs):
            return pltpu.make_async_copy(yb_hbm.at[pl.ds(buffer_slab, slabs)],
                                         gbuf_ref.at[buf, pl.ds(tile_slab, slabs)], sem.at[buf])
        return build

    @pl.when(i == 0)
    def _():
        gbuf_ref[...] = jnp.zeros(gbuf_ref.shape, gbuf_ref.dtype)
        _start_chunks(0, *refs, make_copy(0))

    @pl.when(i + 1 < nt)
    def _():
        _start_chunks(i + 1, *refs, make_copy(1 - cur))

    _wait_chunks(i, nch_ref, make_copy(cur))

    lo, hi = _unpack_bf16_pair(gbuf_ref[cur].reshape(GROUPED_ROWS, gbuf_ref.shape[-1]))
    g = jnp.concatenate([lo.astype(BF16), hi.astype(BF16)], axis=1)
    pos16 = pos_ref[...].astype(jnp.int16)
    wl = wl_ref[...].astype(BF16)
    liota = lax.broadcasted_iota(I32, (tm, GROUPED_ROWS), 1).astype(jnp.int16)
    a = jnp.zeros((tm, GROUPED_ROWS), BF16)
    for r in range(TOP_K):
        a = jnp.where(liota == pos16[:, r:r + 1], jnp.broadcast_to(wl[:, r:r + 1], a.shape), a)
    routed = _dot(a, g)
    x2 = xs_ref[...] + ga2_ref[...] * routed
    ms = jnp.mean(x2 * x2, axis=-1, keepdims=True)
    y = x2 * lax.rsqrt(ms + EPS) * gf_ref[...]

    @pl.when(i < nta)
    def _():
        oa_ref[...] = y

    @pl.when(i >= nta)
    def _():
        ob_ref[...] = y


def _combine(chunks, nch, pstart, pos_t, wl_t, xs_base, ga2, gfin, yb, tm, nta):
    n, d = xs_base.shape
    ntb = n // tm - nta
    ga2_spec = pl.BlockSpec((None, tm, d), lambda i, *_: (jnp.minimum(i // nta, 1), 0, 0))
    grid_spec = pltpu.PrefetchScalarGridSpec(
        num_scalar_prefetch=3,
        grid=(n // tm,),
        in_specs=[
            pl.BlockSpec((tm, TOP_K), lambda i, *_: (i, 0)),
            pl.BlockSpec((tm, TOP_K), lambda i, *_: (i, 0)),
            pl.BlockSpec((tm, d), lambda i, *_: (i, 0)),
            ga2_spec,
            pl.BlockSpec((1, d), lambda i, *_: (0, 0)),
            pl.BlockSpec(memory_space=pl.ANY),
        ],
        out_specs=[
            pl.BlockSpec((tm, d), lambda i, *_: (jnp.minimum(i, nta - 1), 0)),
            pl.BlockSpec((tm, d), lambda i, *_: (jnp.maximum(i - nta, 0), 0)),
        ],
        scratch_shapes=[
            pltpu.VMEM((2, GROUPED_ROWS // ROW_GROUP, ROW_GROUP, d // 2), U32),
            pltpu.SemaphoreType.DMA((2,)),
        ],
    )
    return pl.pallas_call(
        functools.partial(_combine_kernel, tm=tm, nta=nta),
        grid_spec=grid_spec,
        out_shape=[jax.ShapeDtypeStruct((nta * tm, d), F32), jax.ShapeDtypeStruct((ntb * tm, d), F32)],
        compiler_params=_params(("arbitrary",)),
        name="moe_combine",
    )(chunks, nch, pstart, pos_t, wl_t, xs_base, ga2, gfin, yb)


def _moe_and_final(src_a, src_b, mods, wts, tm, bm):
    nta = src_a[0].shape[0] // tm
    ga1, sh2, sc2, ga2 = mods
    (w_out_b, g_ffn, wsgu_b, wsd_b, wr_t, rb, w_gate_up_l, w_down_l, g_final) = wts
    xs_base, h2, pos, wl, chunks, nch, cnt = _post(
        src_a, src_b, w_out_b, ga1, g_ffn, sc2, sh2, ga2, wsgu_b, wsd_b, wr_t, rb, tm)
    nt = xs_base.shape[0] // tm
    counts = cnt[:, 0].astype(I32)
    padded = (counts + bm - 1) // bm * bm
    pend = jnp.cumsum(padded)
    pstart = pend - padded
    max_rows = nt * (tm * TOP_K + N_EXPERTS * (ROW_GROUP - 1))
    nblk = -(-max_rows // bm) + N_EXPERTS
    nused = (pend[-1] // bm).astype(I32)
    blk_row = jnp.minimum(jnp.arange(nblk, dtype=I32), nused - 1) * bm
    be = jnp.sum((pend[None, :] <= blk_row[:, None]).astype(I32), axis=1)
    chunks = chunks.reshape(-1)
    nch = nch[:, 0, 0]
    pslab = pstart // ROW_GROUP
    xs = _dispatch(chunks, nch, pslab, pend // ROW_GROUP, (pstart + counts) // ROW_GROUP, pos, h2,
                   nblk * bm, tm)
    xs = xs.reshape(nblk * bm, xs.shape[-1])
    region_end = jnp.sum(jnp.where(be[:, None] == jnp.arange(N_EXPERTS, dtype=I32)[None, :],
                                   (pstart + counts)[None, :], 0), axis=1)
    valid = jnp.clip(region_end - jnp.arange(nblk, dtype=I32) * bm, 0, bm)
    blk = jnp.arange(nblk, dtype=I32)
    first = ((blk < nused) & ((blk == 0) | (be != jnp.roll(be, 1)))).astype(I32)
    slot = (jnp.cumsum(first) - 1) % 2
    eids = jnp.arange(N_EXPERTS, dtype=I32)
    later_nonempty = (eids[None, :] > eids[:, None]) & (counts[None, :] > 0)
    next_tab = jnp.min(jnp.where(later_nonempty, eids[None, :], N_EXPERTS), axis=1)
    next_tab = jnp.where(next_tab == N_EXPERTS, -1, next_tab)
    next_e = jnp.sum(jnp.where(be[:, None] == eids[None, :], next_tab[None, :], 0), axis=1)
    yb = _experts(be, nused.reshape(1), valid, first, slot.astype(I32), next_e.astype(I32), xs,
                  w_gate_up_l, w_down_l, bm)
    yb = yb.reshape(nblk * bm // ROW_GROUP, ROW_GROUP, yb.shape[-1])
    return _combine(chunks, nch, pslab, pos.T, wl.T, xs_base, ga2, g_final, yb, tm, nta)


def _expand(mod, reps):
    if mod.shape[0] == 1:
        return mod
    return jnp.repeat(mod, reps, axis=0)


def kernel(x_prompt, x_sample, cache_k, cache_v, state_hgrn, c_prompt, c_sample, w_ada, b_ada,
           norm_mix, norm_ffn, norm_final, w_in, w_out, hg_lb_logits, hg_norm, da_lambda, da_norm,
           rel_bias_table, w_router, router_bias, w_gate_up, w_down, ws_gate_up, ws_down):
    depth = w_in.shape[0]
    assert depth == 1 and hg_lb_logits.shape[0] == 2
    bp, tp, d = x_prompt.shape
    bs, ts, _ = x_sample.shape
    assert bp == 1
    past = cache_k.shape[2]
    l = 0

    rows = -(-(bp + bs) // 8) * 8
    c_all = jnp.zeros((rows, d), F32).at[:bp].set(c_prompt).at[bp:bp + bs].set(c_sample)
    mod = _adaln(c_all, w_ada[l], b_ada[l])
    mod_p = [mod[0:bp, j * d:(j + 1) * d] for j in range(6)]
    mod_s = [_expand(mod[bp:bp + bs, j * d:(j + 1) * d], ts) for j in range(6)]

    w_in_b = w_in[l].astype(BF16)
    w_out_b = w_out[l].astype(BF16)
    wsgu_b = ws_gate_up[l].astype(BF16)
    wsd_b = ws_down[l].astype(BF16)
    wr_t = w_router[l].T
    g_mix = norm_mix[l].reshape(1, d)
    g_ffn = norm_ffn[l].reshape(1, d)
    g_final = norm_final.reshape(1, d)
    moe_w = (w_out_b, g_ffn, wsgu_b, wsd_b, wr_t, router_bias[l], w_gate_up[l], w_down[l], g_final)

    lam = _lam(da_lambda[l])

    t_att = min(ATT_TILE, tp)
    kk = jnp.arange(t_att, dtype=I32)[:, None]
    qq = jnp.arange(t_att, dtype=I32)[None, :]
    idx_diag = jnp.where((kk // CHUNK) <= (qq // CHUNK), _rel_bucket(kk - qq), MASK_BUCKET)
    idx_prev = _rel_bucket(kk - qq - t_att)
    bias_p = _bias_tiles(rel_bias_table, jnp.stack([idx_diag, idx_prev]).astype(I32),
                         shift_bucket=N_BUCKETS // 2 - 1)
    pad = 128
    qpos = past + jnp.arange(ts, dtype=I32)[:, None]
    idx_sp = _rel_bucket(jnp.arange(past, dtype=I32)[None, :] - qpos)
    kn = jnp.arange(pad, dtype=I32)[None, :]
    idx_sn = jnp.where(kn < ts, _rel_bucket(past + kn - qpos), MASK_BUCKET)
    bias_sp = _bias_tiles(rel_bias_table, idx_sp[None].astype(I32), shift_bucket=None)
    bias_sn = _bias_tiles(rel_bias_table, idx_sn[None].astype(I32), shift_bucket=None)

    xp = x_prompt.reshape(bp * tp, d)
    sh1, sc1, ga1, sh2, sc2, ga2 = mod_p
    assert ATT_TILE == INPROJ_TILE
    zh, qt, kf, vf, kb, vt = _inproj(xp, g_mix, sc1, sh1, w_in_b, t_att, True)
    s_zero = jnp.zeros((bp, HG_HEADS, HG_DIM, HG_DIM), F32)
    ohg_p, sp_new = _hgrn(zh, s_zero, hg_lb_logits, hg_norm[l], bp, tp, min(HGRN_CHUNK, tp))
    oda_p = _attn_prompt(kb, qt, vt, bias_p, lam, da_norm[l], t_att)
    src_p = (xp, ohg_p, oda_p)
    mods_p = (ga1, sh2, sc2, ga2)
    k_prompt = kf.reshape(1, bp, tp, DA_HEADS, 2 * DA_QKDIM)
    v_prompt = vf.reshape(1, bp, tp, DA_HEADS, DA_VDIM)

    ns = bs * ts
    xs_ = x_sample.reshape(ns, d)
    sh1, sc1, ga1, sh2, sc2, ga2 = mod_s
    zh, qs, kf, vf, kb, vb = _inproj(xs_, g_mix, sc1, sh1, w_in_b, ns, False)
    ohg_s, ss_new = _hgrn(zh, state_hgrn[l], hg_lb_logits, hg_norm[l], bs, ts, ts)
    oda_s = _attn_step(qs, cache_k[l], cache_v[l], kb, vb, bias_sp, bias_sn, lam, da_norm[l], bs, ts)
    assert ns == POST_TILE
    mods = tuple(jnp.stack([jnp.broadcast_to(mp, (POST_TILE, d)), ms_])
                 for mp, ms_ in zip(mods_p, (ga1, sh2, sc2, ga2)))
    y_p, y_s = _moe_and_final(src_p, (xs_, ohg_s, oda_s), mods, moe_w, POST_TILE, MOE_BLOCK_ROWS)
    k_sample = kf.reshape(1, bs, ts, DA_HEADS, 2 * DA_QKDIM)
    v_sample = vf.reshape(1, bs, ts, DA_HEADS, DA_VDIM)

    return (y_p.reshape(bp, tp, d), y_s.reshape(bs, ts, d), k_prompt, v_prompt, sp_new[None],
            k_sample, v_sample, ss_new[None].astype(x_sample.dtype))
```

```python
import functools
import math

import numpy as np
import jax
import jax.numpy as jnp
from jax import lax
from jax.experimental import pallas as pl
from jax.experimental.pallas import tpu as pltpu

F32 = jnp.float32
BF16 = jnp.bfloat16
I32 = jnp.int32
U32 = jnp.uint32
HIGHEST = lax.Precision.HIGHEST

D_MODEL = 1024
CHUNK = 64
HG_HEADS = 4
HG_DIM = 128
HG_WIDTH = HG_HEADS * HG_DIM
DA_HEADS = 4
DA_VDIM = 128
DA_QKDIM = 64
DA_WIDTH = DA_HEADS * DA_VDIM
N_BUCKETS = 32
MAX_DIST = 128
N_EXPERTS = 64
TOP_K = 8
N_GROUPS = 8
GROUP_SIZE = N_EXPERTS // N_GROUPS
TOP_GROUPS = 4
D_EXPERT = 256
ROUTE_SCALE = 2.5
EPS = 1e-6
LAM_INIT = 0.8 - 0.6 * math.exp(-0.3 * 0)

LOG2E = math.log2(math.e)
HI_MASK = np.uint32(0xFFFF0000)
NEG_BIG = -1e30
MASK_BUCKET = N_BUCKETS
V7X_VMEM_LIMIT = 48 * 1024 * 1024

ATT_TILE = 512
VT_ROWS = DA_VDIM + 16
HGRN_CHUNK = 256
INPROJ_TILE = 512
POST_TILE = 256
MOE_BLOCK_ROWS = 1024
ROW_GROUP = 8
GROUPED_ROWS = -(-(POST_TILE * TOP_K + N_EXPERTS * (ROW_GROUP - 1)) // 256) * 256
CHUNK_SLOTS = -(-(GROUPED_ROWS // ROW_GROUP) // 128) * 128
CHUNK_EXPERT_SHIFT = 24
CHUNK_UNROLL = 8
WAIT_CHUNKS = 16
ZERO_SLABS = 16


def _sigmoid(x):
    return 1.0 / (1.0 + jnp.exp(-x))


def _silu(x):
    return x * _sigmoid(x)


def _dot(a, b, **kw):
    return jnp.dot(a, b, preferred_element_type=F32, **kw)


def _dot_nt(a, b, **kw):
    return lax.dot_general(a, b, (((1,), (1,)), ((), ())), preferred_element_type=F32, **kw)


def _dot_tn(a, b, **kw):
    return lax.dot_general(a, b, (((0,), (0,)), ((), ())), preferred_element_type=F32, **kw)


def _pack_bf16_pair(lo, hi):
    lo_bits = lax.bitcast_convert_type(lo.astype(BF16).astype(F32), U32)
    hi_bits = lax.bitcast_convert_type(hi.astype(BF16).astype(F32), U32)
    return (lo_bits >> 16) | (hi_bits & HI_MASK)


def _pack_exact_bf16_pair(lo, hi):
    return (lax.bitcast_convert_type(lo, U32) >> 16) | (lax.bitcast_convert_type(hi, U32) & HI_MASK)


def _unpack_bf16_pair(w):
    lo = lax.bitcast_convert_type(w << 16, F32)
    hi = lax.bitcast_convert_type(w & HI_MASK, F32)
    return lo, hi


def _params(sem, vmem=V7X_VMEM_LIMIT, flags=None):
    return pltpu.CompilerParams(dimension_semantics=sem, vmem_limit_bytes=vmem, flags=flags)


def _adaln_kernel(c_ref, w_ref, b_ref, o_ref):
    s = _silu(c_ref[...])
    o_ref[...] = _dot(s, w_ref[...], precision=HIGHEST) + b_ref[...]


def _adaln(c_all, w_ada, b_ada):
    rows, d = c_all.shape
    cols = w_ada.shape[1]
    blk = 1024
    return pl.pallas_call(
        _adaln_kernel,
        grid=(cols // blk,),
        in_specs=[
            pl.BlockSpec((rows, d), lambda j: (0, 0)),
            pl.BlockSpec((d, blk), lambda j: (0, j)),
            pl.BlockSpec((1, blk), lambda j: (0, j)),
        ],
        out_specs=pl.BlockSpec((rows, blk), lambda j: (0, j)),
        out_shape=jax.ShapeDtypeStruct((rows, cols), F32),
        compiler_params=_params(("parallel",)),
        name="adaln",
    )(c_all, w_ada, b_ada.reshape(1, cols))


def _lam_kernel(l_ref, o_ref):
    l = l_ref[...].astype(F32)
    a = jnp.sum(l[0:1] * l[1:2], axis=-1, keepdims=True)
    b = jnp.sum(l[2:3] * l[3:4], axis=-1, keepdims=True)
    lam = jnp.exp(a) - jnp.exp(b) + LAM_INIT
    o_ref[...] = jnp.broadcast_to(lam, o_ref.shape)


def _lam(da_lambda_l):
    return pl.pallas_call(
        _lam_kernel,
        out_shape=jax.ShapeDtypeStruct((8, 128), F32),
        name="lam",
    )(da_lambda_l)


def _rel_bucket(rel):
    nb = N_BUCKETS // 2
    max_exact = nb // 2
    side = jnp.where(rel > 0, nb, 0)
    n = jnp.abs(rel)
    large = max_exact + (jnp.log(jnp.maximum(n, 1).astype(F32) / max_exact)
                         / math.log(MAX_DIST / max_exact) * (nb - max_exact)).astype(I32)
    large = jnp.minimum(large, nb - 1)
    return side + jnp.where(n < max_exact, n, large)


def _bias_kernel(tab_ref, idx_ref, o_ref, *, shift_bucket):
    h = pl.program_id(0)
    idx = idx_ref[...]
    shift = tab_ref[shift_bucket, h] if shift_bucket is not None else 0.0
    acc = jnp.zeros(idx.shape, F32)
    for j in range(N_BUCKETS):
        acc = jnp.where(idx == j, (tab_ref[j, h] - shift) * LOG2E, acc)
    o_ref[...] = jnp.where(idx == MASK_BUCKET, NEG_BIG, acc)


def _bias_tiles(table, idx, *, shift_bucket):
    k, r, c = idx.shape
    return pl.pallas_call(
        functools.partial(_bias_kernel, shift_bucket=shift_bucket),
        grid=(DA_HEADS, k),
        in_specs=[
            pl.BlockSpec(memory_space=pltpu.SMEM),
            pl.BlockSpec((None, r, c), lambda h, d: (d, 0, 0)),
        ],
        out_specs=pl.BlockSpec((None, None, r, c), lambda h, d: (h, d, 0, 0)),
        out_shape=jax.ShapeDtypeStruct((DA_HEADS, k, r, c), F32),
        compiler_params=_params(("parallel", "parallel")),
        name="rel_bias",
    )(table, idx)


def _inproj_kernel(x_ref, g_ref, sc_ref, sh_ref, w_ref,
                   zh_ref, q_ref, k_ref, v_ref, kb_ref, vb_ref, *, transposed):
    x = x_ref[...]
    ms = jnp.mean(x * x, axis=-1, keepdims=True)
    h = x * lax.rsqrt(ms + EPS) * g_ref[...]
    h = h * (1.0 + sc_ref[...]) + sh_ref[...]
    hb = h.astype(BF16)
    c0 = 4 * HG_WIDTH
    zh_ref[...] = _dot(hb, w_ref[:, 0:c0])
    zq = _dot(hb, w_ref[:, c0:c0 + DA_WIDTH]) * (DA_QKDIM ** -0.5 * LOG2E)
    zk = _dot(hb, w_ref[:, c0 + DA_WIDTH:c0 + 2 * DA_WIDTH])
    for hd in range(DA_HEADS):
        k_ref[:, hd, :] = zk[:, hd * DA_VDIM:(hd + 1) * DA_VDIM]
    kb_ref[...] = zk.astype(BF16)
    zv = _dot(hb, w_ref[:, c0 + 2 * DA_WIDTH:c0 + 3 * DA_WIDTH])
    for hd in range(DA_HEADS):
        v_ref[:, hd, :] = zv[:, hd * DA_VDIM:(hd + 1) * DA_VDIM]
    if transposed:
        q_ref[...] = zq.T.astype(BF16)
        vb_ref[:, 0:DA_VDIM, :] = zv.T.astype(BF16).reshape(DA_HEADS, DA_VDIM, zv.shape[0])
        vb_ref[:, DA_VDIM:, :] = jnp.ones((DA_HEADS, VT_ROWS - DA_VDIM, zv.shape[0]), BF16)
    else:
        q_ref[...] = zq.astype(BF16)
        vb_ref[...] = zv.astype(BF16)


def _mod_spec(mod, tm):
    if mod.shape[0] == 1:
        return pl.BlockSpec((1, mod.shape[1]), lambda i: (0, 0))
    return pl.BlockSpec((tm, mod.shape[1]), lambda i: (i, 0))


def _inproj(x, g, sc, sh, w_in_b, tm, transposed):
    n, d = x.shape
    cols = w_in_b.shape[1]
    row = lambda i: (i, 0)
    if transposed:
        q_spec = pl.BlockSpec((DA_WIDTH, tm), lambda i: (0, i))
        q_shape = jax.ShapeDtypeStruct((DA_WIDTH, n), BF16)
        vb_spec = pl.BlockSpec((DA_HEADS, None, VT_ROWS, tm), lambda i: (0, i, 0, 0))
        vb_shape = jax.ShapeDtypeStruct((DA_HEADS, n // tm, VT_ROWS, tm), BF16)
    else:
        q_spec = vb_spec = pl.BlockSpec((tm, DA_WIDTH), row)
        q_shape = vb_shape = jax.ShapeDtypeStruct((n, DA_WIDTH), BF16)
    return pl.pallas_call(
        functools.partial(_inproj_kernel, transposed=transposed),
        grid=(n // tm,),
        in_specs=[
            pl.BlockSpec((tm, d), row),
            pl.BlockSpec((1, d), lambda i: (0, 0)),
            _mod_spec(sc, tm),
            _mod_spec(sh, tm),
            pl.BlockSpec((d, cols), lambda i: (0, 0)),
        ],
        out_specs=[
            pl.BlockSpec((tm, 4 * HG_WIDTH), row),
            q_spec,
            pl.BlockSpec((tm, DA_HEADS, DA_VDIM), lambda i: (i, 0, 0)),
            pl.BlockSpec((tm, DA_HEADS, DA_VDIM), lambda i: (i, 0, 0)),
            pl.BlockSpec((tm, DA_WIDTH), row),
            vb_spec,
        ],
        out_shape=[
            jax.ShapeDtypeStruct((n, 4 * HG_WIDTH), F32),
            q_shape,
            jax.ShapeDtypeStruct((n, DA_HEADS, DA_VDIM), F32),
            jax.ShapeDtypeStruct((n, DA_HEADS, DA_VDIM), F32),
            jax.ShapeDtypeStruct((n, DA_WIDTH), BF16),
            vb_shape,
        ],
        compiler_params=_params(("parallel",)),
        name="inproj",
    )(x, g, sc, sh, w_in_b)


def _hgrn_consts(c):
    levels = int(round(math.log2(c)))
    assert 1 << levels == c and levels >= 3
    t = np.arange(c)[:, None]
    r = np.arange(c)[None, :]
    tri = (r <= t).astype(np.float32)
    x = np.maximum(t ^ r, 1)
    lv = np.where(t == r, -1, np.where(t > r, np.floor(np.log2(x)).astype(np.int64), -2))
    return jnp.asarray(tri, dtype=BF16), jnp.asarray(lv, dtype=I32), levels


def _hgrn_kernel(zh_ref, s0_ref, lbl_ref, gain_ref, mall_ref, lv_ref,
                 o_ref, sout_ref, st_ref, b_ref, *, c, levels):
    ci = pl.program_id(1)

    @pl.when(ci == 0)
    def _():
        for h in range(HG_HEADS):
            st_ref[h] = s0_ref[h].astype(F32).T

    lbl = lbl_ref[...].astype(F32)
    mx = jnp.maximum(lbl[0:1], lbl[1:2])
    e0 = jnp.exp(lbl[0:1] - mx)
    e1 = jnp.exp(lbl[1:2] - mx)
    lb = e0 / (e0 + e1)

    xq = zh_ref[:, 0:HG_WIDTH]
    xf = zh_ref[:, HG_WIDTH:2 * HG_WIDTH]
    q = _silu(xq)
    y = lb + (1.0 - lb) * _sigmoid(xf)
    logf = jnp.log(y)
    kk = 1.0 - y

    l1 = logf.astype(BF16)
    r1 = logf - l1.astype(F32)
    l2 = r1.astype(BF16)
    l3 = (r1 - l2.astype(F32)).astype(BF16)
    tri = mall_ref[...]
    b = _dot(tri, l1) + _dot(tri, l2) + _dot(tri, l3)
    b_ref[...] = b
    trow = lax.broadcasted_iota(I32, (c, HG_DIM), 0)

    def level_factor(l, sl):
        m = 1 << l
        later = (trow & m) != 0
        lf = logf[:, sl]
        if l == 0:
            e = jnp.where(later, lf, 0.0)
        elif l == 1:
            below = pltpu.roll(lf, 1, 0)
            above = pltpu.roll(lf, c - 1, 0)
            low = (trow & 1) != 0
            e = jnp.where(later, jnp.where(low, lf + below, lf), jnp.where(low, 0.0, above))
        else:
            mid = jnp.concatenate(
                [jnp.broadcast_to(b_ref[k * 2 * m + m - 1:k * 2 * m + m, sl], (2 * m, HG_DIM))
                 for k in range(c // (2 * m))], axis=0)
            e = jnp.where(later, b[:, sl] - mid, mid - b[:, sl])
        return jnp.exp(e)

    lv = lv_ref[...].astype(jnp.int16)
    gain = gain_ref[...].astype(F32)
    for h in range(HG_HEADS):
        sl = slice(h * HG_DIM, (h + 1) * HG_DIM)
        qh = q[:, sl]
        kh = kk[:, sl]
        ih = zh_ref[:, 2 * HG_WIDTH + h * HG_DIM:2 * HG_WIDTH + (h + 1) * HG_DIM]
        gh = zh_ref[:, 3 * HG_WIDTH + h * HG_DIM:3 * HG_WIDTH + (h + 1) * HG_DIM]
        bh = b[:, sl]
        ihb = ih.astype(BF16)
        a = jnp.where(lv == -1, _dot_nt(qh.astype(BF16), kh.astype(BF16)).astype(BF16),
                      jnp.zeros((c, c), BF16))
        for l in range(levels):
            f = level_factor(l, sl)
            p = _dot_nt((qh * f).astype(BF16), (kh * f).astype(BF16))
            a = jnp.where(lv == l, p.astype(BF16), a)
        st = st_ref[h]
        o = _dot(a, ihb) + _dot_nt((qh * jnp.exp(bh)).astype(BF16), st.astype(BF16))
        bl = bh[c - 1:c, :]
        kd = (kh * jnp.exp(bl - bh)).astype(BF16)
        st_ref[h] = st * jnp.exp(bl) + _dot_tn(ihb, kd)
        ms = jnp.mean(o * o, axis=-1, keepdims=True)
        on = o * lax.rsqrt(ms + EPS) * gain
        o_ref[:, sl] = (on * _silu(gh)).astype(o_ref.dtype)

    @pl.when(ci == pl.num_programs(1) - 1)
    def _():
        for h in range(HG_HEADS):
            sout_ref[h] = st_ref[h].T.astype(sout_ref.dtype)


def _hgrn(zh, s0, lb_logits, gain, batch, seq, c):
    mall, lv, levels = _hgrn_consts(c)
    nc = seq // c
    return pl.pallas_call(
        functools.partial(_hgrn_kernel, c=c, levels=levels),
        grid=(batch, nc),
        in_specs=[
            pl.BlockSpec((c, 4 * HG_WIDTH), lambda b, i: (b * nc + i, 0)),
            pl.BlockSpec((None, HG_HEADS, HG_DIM, HG_DIM), lambda b, i: (b, 0, 0, 0)),
            pl.BlockSpec(lb_logits.shape, lambda b, i: (0, 0)),
            pl.BlockSpec((1, HG_DIM), lambda b, i: (0, 0)),
            pl.BlockSpec(mall.shape, lambda b, i: (0, 0)),
            pl.BlockSpec(lv.shape, lambda b, i: (0, 0)),
        ],
        out_specs=[
            pl.BlockSpec((c, HG_WIDTH), lambda b, i: (b * nc + i, 0)),
            pl.BlockSpec((None, HG_HEADS, HG_DIM, HG_DIM), lambda b, i: (b, 0, 0, 0)),
        ],
        out_shape=[
            jax.ShapeDtypeStruct((batch * seq, HG_WIDTH), BF16),
            jax.ShapeDtypeStruct((batch, HG_HEADS, HG_DIM, HG_DIM), F32),
        ],
        scratch_shapes=[pltpu.VMEM((HG_HEADS, HG_DIM, HG_DIM), F32),
                        pltpu.VMEM((c, HG_WIDTH), F32)],
        compiler_params=_params(("parallel", "arbitrary")),
        name="hgrn2",
    )(zh, s0, lb_logits, gain.reshape(1, HG_DIM), mall, lv)


def _attn_kernel(k_ref, qt_ref, vt_ref, bias_ref, lam_ref, gain_ref,
                 o_ref, qz_ref, m_ref, acc_ref, s_ref, smax_ref, *, t):
    i = pl.program_id(1)
    qt = qt_ref[...]
    row = lax.broadcasted_iota(I32, qt.shape, 0)
    zero = jnp.zeros_like(qt)
    qz_ref[:, 0:t] = jnp.where(row < DA_QKDIM, qt, zero)
    qz_ref[:, t:2 * t] = jnp.where(row >= DA_QKDIM, qt, zero)
    m_ref[...] = jnp.full(m_ref.shape, NEG_BIG, F32)
    acc_ref[...] = jnp.zeros(acc_ref.shape, F32)

    def scores(j, buf):
        kt = k_ref[pl.ds(pl.multiple_of(j * t, t), t), :]
        s = _dot(kt, qz_ref[...])
        s_ref[buf] = s
        smax_ref[buf] = jnp.max(s, axis=0, keepdims=True)

    def consume(j, buf, bias_idx):
        s = s_ref[buf]
        if bias_idx is not None:
            b = bias_ref[bias_idx]
            s = jnp.concatenate([s[:, 0:t] + b, s[:, t:2 * t] + b], axis=1)
            s_max = jnp.max(s, axis=0, keepdims=True)
        else:
            s_max = smax_ref[buf]
        m_prev = m_ref[...]
        m_new = jnp.maximum(m_prev, s_max)
        alpha = jnp.exp2(m_prev - m_new)
        pr = jnp.exp2(s - m_new).astype(BF16)
        acc_ref[...] = alpha * acc_ref[...] + _dot(vt_ref[j], pr)
        m_ref[...] = m_new

    n_far = jnp.maximum(i - 1, 0)

    @pl.when(i >= 1)
    def _():
        scores(i - 1, 0)
        scores(i, 1)
        consume(i - 1, 0, 1)
        scores(0, 0)
        consume(i, 1, 0)

    @pl.when(i == 0)
    def _():
        scores(i, 1)
        consume(i, 1, 0)

    def far_tiles(j, count):
        for u in range(count):
            nxt = j + u + 1
            if u == count - 1:
                nxt = jnp.minimum(nxt, n_far - 1)
            scores(nxt, (u + 1) % 2)
            consume(j + u, u % 2, None)

    def far_quad(p, carry):
        far_tiles(4 * p, 4)
        return carry

    def far_pair(p, carry):
        far_tiles(4 * quads + 2 * p, 2)
        return carry

    quads = n_far // 4
    lax.fori_loop(0, quads, far_quad, 0)
    lax.fori_loop(0, (n_far - 4 * quads) // 2, far_pair, 0)

    @pl.when(lax.rem(n_far, 2) == 1)
    def _():
        consume(n_far - 1, 0, None)

    lam = lam_ref[0:1, 0:1]
    acc = acc_ref[0:DA_VDIM, :]
    l = acc_ref[DA_VDIM:DA_VDIM + 1, :]
    o = acc[:, 0:t] / l[:, 0:t] - lam * (acc[:, t:2 * t] / l[:, t:2 * t])
    ms = jnp.mean(o * o, axis=0, keepdims=True)
    on = o * lax.rsqrt(ms + EPS) * gain_ref[...].astype(F32) * (1.0 - LAM_INIT)
    o_ref[...] = on.T.astype(o_ref.dtype)


def _attn_prompt(kb, qt, vt, bias, lam, gain, t):
    n = kb.shape[0]
    nt = n // t
    return pl.pallas_call(
        functools.partial(_attn_kernel, t=t),
        grid=(DA_HEADS, nt),
        in_specs=[
            pl.BlockSpec((n, DA_VDIM), lambda h, i: (0, h)),
            pl.BlockSpec((DA_VDIM, t), lambda h, i: (h, i)),
            pl.BlockSpec((None, nt, VT_ROWS, t), lambda h, i: (h, 0, 0, 0)),
            pl.BlockSpec((None, 2, t, t), lambda h, i: (h, 0, 0, 0)),
            pl.BlockSpec((8, 128), lambda h, i: (0, 0)),
            pl.BlockSpec((DA_VDIM, 1), lambda h, i: (0, 0)),
        ],
        out_specs=pl.BlockSpec((t, DA_VDIM), lambda h, i: (i, h)),
        out_shape=jax.ShapeDtypeStruct((n, DA_WIDTH), BF16),
        scratch_shapes=[
            pltpu.VMEM((DA_VDIM, 2 * t), BF16),
            pltpu.VMEM((1, 2 * t), F32),
            pltpu.VMEM((VT_ROWS, 2 * t), F32),
            pltpu.VMEM((2, t, 2 * t), F32),
            pltpu.VMEM((2, 1, 2 * t), F32),
        ],
        compiler_params=_params(("parallel", "parallel")),
        name="diff_attn_prompt",
    )(kb, qt, vt, bias, lam, gain.reshape(DA_VDIM, 1))


def _attn_step_kernel(q_ref, kp_ref, vp_ref, kn_ref, vn_ref, bp_ref, bn_ref, lam_ref, gain_ref,
                      o_ref, *, tq, pad):
    lam = lam_ref[0:1, 0:1]
    gain = gain_ref[...].astype(F32)
    zpad = jnp.zeros((pad - tq, DA_VDIM), BF16)
    for h in range(DA_HEADS):
        hs = slice(h * DA_VDIM, (h + 1) * DA_VDIM)
        q = q_ref[:, hs]
        lane = lax.broadcasted_iota(I32, q.shape, 1)
        zero = jnp.zeros_like(q)
        qz = jnp.concatenate([jnp.where(lane < DA_QKDIM, q, zero),
                              jnp.where(lane >= DA_QKDIM, q, zero)], axis=0)
        kp = kp_ref[:, h, :].astype(BF16)
        vp = vp_ref[:, h, :].astype(BF16)
        kn = jnp.concatenate([kn_ref[:, hs], zpad], axis=0)
        vn = jnp.concatenate([vn_ref[:, hs], zpad], axis=0)
        bp = bp_ref[h, 0]
        bn = bn_ref[h, 0]
        sp = _dot_nt(qz, kp) + jnp.concatenate([bp, bp], axis=0)
        sn = _dot_nt(qz, kn) + jnp.concatenate([bn, bn], axis=0)
        m = jnp.maximum(jnp.max(sp, axis=-1, keepdims=True), jnp.max(sn, axis=-1, keepdims=True))
        pp = jnp.exp2(sp - m)
        pn = jnp.exp2(sn - m)
        l = jnp.sum(pp, axis=-1, keepdims=True) + jnp.sum(pn, axis=-1, keepdims=True)
        acc = _dot(pp.astype(BF16), vp) + _dot(pn.astype(BF16), vn)
        on = acc / l
        o = on[0:tq] - lam * on[tq:2 * tq]
        ms = jnp.mean(o * o, axis=-1, keepdims=True)
        o = o * lax.rsqrt(ms + EPS) * gain * (1.0 - LAM_INIT)
        o_ref[:, hs] = o.astype(o_ref.dtype)


def _attn_step(qs, cache_k_l, cache_v_l, kb, vb, bias_p, bias_n, lam, gain, batch, tq):
    past = cache_k_l.shape[1]
    pad = bias_n.shape[-1]
    cache_spec = pl.BlockSpec((None, past, DA_HEADS, DA_VDIM), lambda b: (b, 0, 0, 0))
    row = pl.BlockSpec((tq, DA_WIDTH), lambda b: (b, 0))
    return pl.pallas_call(
        functools.partial(_attn_step_kernel, tq=tq, pad=pad),
        grid=(batch,),
        in_specs=[
            row, cache_spec, cache_spec, row, row,
            pl.BlockSpec(bias_p.shape, lambda b: (0, 0, 0, 0)),
            pl.BlockSpec(bias_n.shape, lambda b: (0, 0, 0, 0)),
            pl.BlockSpec((8, 128), lambda b: (0, 0)),
            pl.BlockSpec((1, DA_VDIM), lambda b: (0, 0)),
        ],
        out_specs=row,
        out_shape=jax.ShapeDtypeStruct((batch * tq, DA_WIDTH), BF16),
        compiler_params=_params(("parallel",)),
        name="diff_attn_step",
    )(qs, cache_k_l, cache_v_l, kb, vb, bias_p, bias_n, lam, gain.reshape(1, DA_VDIM))


def _post_kernel(xa_ref, ohga_ref, odaa_ref, xb_ref, ohgb_ref, odab_ref,
                 wout_ref, ga1_ref, g_ref, sc_ref, sh_ref, ga2_ref,
                 wsgu_ref, wsd_ref, wrt_ref, rb_ref, tri_ref, ltri_ref,
                 xs_ref, h2_ref, pos_ref, wl_ref, chunk_ref, nch_ref, cnt_ref, carry_ref, *, tm, nta):
    i = pl.program_id(0)

    @pl.when(i == 0)
    def _():
        carry_ref[...] = jnp.zeros(carry_ref.shape, F32)

    second = i >= nta
    x = jnp.where(second, xb_ref[...], xa_ref[...])
    ohg = jnp.where(second, ohgb_ref[...], ohga_ref[...])
    oda = jnp.where(second, odab_ref[...], odaa_ref[...])
    mix = _dot(ohg, wout_ref[0:HG_WIDTH, :]) + _dot(oda, wout_ref[HG_WIDTH:, :])
    x1 = x + ga1_ref[...] * mix
    ms = jnp.mean(x1 * x1, axis=-1, keepdims=True)
    h2 = x1 * lax.rsqrt(ms + EPS) * g_ref[...]
    h2 = h2 * (1.0 + sc_ref[...]) + sh_ref[...]
    h2b = h2.astype(BF16)
    h2_ref[...] = h2b
    gu = _dot(h2b, wsgu_ref[...])
    act = (_silu(gu[:, 0:D_EXPERT]) * gu[:, D_EXPERT:]).astype(BF16)
    xs_ref[...] = x1 + ga2_ref[...] * _dot(act, wsd_ref[...])

    logits = _dot_nt(wrt_ref[...], h2, precision=HIGHEST)
    score = _sigmoid(logits)
    sel = score + rb_ref[...]
    sub = lax.broadcasted_iota(I32, (GROUP_SIZE, tm), 0)
    gscore = []
    for g in range(N_GROUPS):
        v = sel[g * GROUP_SIZE:(g + 1) * GROUP_SIZE, :]
        m1 = jnp.max(v, axis=0, keepdims=True)
        i1 = jnp.min(jnp.where(v == m1, sub, GROUP_SIZE), axis=0, keepdims=True)
        m2 = jnp.max(jnp.where(sub == i1, -jnp.inf, v), axis=0, keepdims=True)
        gscore.append(m1 + m2)
    gsel = []
    for g in range(N_GROUPS):
        ahead = jnp.zeros((1, tm), F32)
        for g2 in range(N_GROUPS):
            if g2 == g:
                continue
            tie = 1.0 if g2 < g else 0.0
            ahead = ahead + jnp.where(gscore[g2] > gscore[g], 1.0,
                                      jnp.where(gscore[g2] == gscore[g], tie, 0.0))
        gsel.append(ahead < TOP_GROUPS)
    selm = jnp.concatenate(
        [jnp.where(gsel[g], sel[g * GROUP_SIZE:(g + 1) * GROUP_SIZE, :], -jnp.inf)
         for g in range(N_GROUPS)], axis=0)
    eio = lax.broadcasted_iota(I32, (N_EXPERTS, tm), 0)
    ahead = jnp.zeros((N_EXPERTS, tm), F32)
    for e2 in range(N_EXPERTS):
        row = selm[e2:e2 + 1, :]
        tie = jnp.where(eio > e2, 1.0, 0.0)
        ahead = ahead + jnp.where(row > selm, 1.0, jnp.where(row == selm, tie, 0.0))
    chosen = jnp.where(selm > -jnp.inf, jnp.where(ahead < TOP_K, 1.0, 0.0), 0.0)
    w = chosen * score
    wn = w / jnp.sum(w, axis=0, keepdims=True) * ROUTE_SCALE

    chb = chosen.astype(BF16)
    before = _dot(chb, tri_ref[...])
    tot = _dot(chb, jnp.ones((tm, 128), BF16))
    run = jnp.floor((tot + (ROW_GROUP - 1)) * (1.0 / ROW_GROUP)) * ROW_GROUP
    tile_base = _dot(ltri_ref[...], run.astype(BF16))
    carry = carry_ref[...]
    carry_ref[...] = carry + run
    cnt_ref[...] = carry + run
    pos = jnp.concatenate([tile_base] * (tm // 128), axis=1) + before

    widen = lambda v: jnp.concatenate([v] * (CHUNK_SLOTS // 128), axis=1)
    crow = lax.broadcasted_iota(I32, (N_EXPERTS, CHUNK_SLOTS), 1).astype(F32) * ROW_GROUP
    erow = lax.broadcasted_iota(I32, (N_EXPERTS, CHUNK_SLOTS), 0).astype(F32)
    owner = jnp.sum(jnp.where(widen(tile_base + run) <= crow, 1.0, 0.0), axis=0, keepdims=True)
    region_row = jnp.sum(jnp.where(owner == erow, widen(carry - tile_base), 0.0),
                         axis=0, keepdims=True) + crow[0:1]
    region_slab = (region_row * (1.0 / ROW_GROUP)).astype(I32)
    chunk_ref[...] = owner.astype(I32) * (1 << CHUNK_EXPERT_SHIFT) + region_slab
    nch_ref[...] = jnp.sum(run * (1.0 / ROW_GROUP), axis=0, keepdims=True).astype(I32)

    for r in range(TOP_K):
        pick = jnp.where(ahead == r, chosen, 0.0)
        pos_ref[r:r + 1, :] = jnp.sum(pick * pos, axis=0, keepdims=True).astype(I32)
        wl_ref[r:r + 1, :] = jnp.sum(pick * wn, axis=0, keepdims=True)


def _post(src_a, src_b, w_out_b, ga1, g, sc, sh, ga2, wsgu_b, wsd_b, wr_t, rb, tm):
    (xa, ohga, odaa), (xb, ohgb, odab) = src_a, src_b
    d = xa.shape[1]
    nta, ntb = xa.shape[0] // tm, xb.shape[0] // tm
    nt = nta + ntb
    n = nt * tm
    tri = jnp.asarray(np.triu(np.ones((tm, tm), np.float32), k=1), dtype=BF16)
    ltri = jnp.asarray(np.tril(np.ones((N_EXPERTS, N_EXPERTS), np.float32), k=-1), dtype=BF16)
    row = lambda i: (i, 0)
    row_a = lambda i: (jnp.minimum(i, nta - 1), 0)
    row_b = lambda i: (jnp.maximum(i - nta, 0), 0)
    col = lambda i: (0, i)
    full = lambda i: (0, 0)
    mod = pl.BlockSpec((None, tm, d), lambda i: (jnp.minimum(i // nta, 1), 0, 0))
    return pl.pallas_call(
        functools.partial(_post_kernel, tm=tm, nta=nta),
        grid=(nt,),
        in_specs=[
            pl.BlockSpec((tm, d), row_a),
            pl.BlockSpec((tm, HG_WIDTH), row_a),
            pl.BlockSpec((tm, DA_WIDTH), row_a),
            pl.BlockSpec((tm, d), row_b),
            pl.BlockSpec((tm, HG_WIDTH), row_b),
            pl.BlockSpec((tm, DA_WIDTH), row_b),
            pl.BlockSpec(w_out_b.shape, full),
            mod,
            pl.BlockSpec((1, d), full),
            mod,
            mod,
            mod,
            pl.BlockSpec(wsgu_b.shape, full),
            pl.BlockSpec(wsd_b.shape, full),
            pl.BlockSpec(wr_t.shape, full),
            pl.BlockSpec((N_EXPERTS, 1), full),
            pl.BlockSpec((tm, tm), full),
            pl.BlockSpec((N_EXPERTS, N_EXPERTS), full),
        ],
        out_specs=[
            pl.BlockSpec((tm, d), row),
            pl.BlockSpec((tm, d), row),
            pl.BlockSpec((TOP_K, tm), col),
            pl.BlockSpec((TOP_K, tm), col),
            pl.BlockSpec((None, 1, CHUNK_SLOTS), lambda i: (i, 0, 0)),
            pl.BlockSpec((None, 1, 128), lambda i: (i, 0, 0)),
            pl.BlockSpec((N_EXPERTS, 128), full),
        ],
        out_shape=[
            jax.ShapeDtypeStruct((n, d), F32),
            jax.ShapeDtypeStruct((n, d), BF16),
            jax.ShapeDtypeStruct((TOP_K, n), I32),
            jax.ShapeDtypeStruct((TOP_K, n), F32),
            jax.ShapeDtypeStruct((nt, 1, CHUNK_SLOTS), I32),
            jax.ShapeDtypeStruct((nt, 1, 128), I32),
            jax.ShapeDtypeStruct((N_EXPERTS, 128), F32),
        ],
        scratch_shapes=[pltpu.VMEM((N_EXPERTS, 128), F32)],
        compiler_params=_params(("arbitrary",)),
        name="post_mix_router",
    )(xa, ohga, odaa, xb, ohgb, odab, w_out_b, ga1, g, sc, sh, ga2, wsgu_b, wsd_b, wr_t,
      rb.reshape(N_EXPERTS, 1), tri, ltri)


def _start_chunks(tile, chunk_ref, nch_ref, pslab_ref, make_copy):
    n = nch_ref[tile]

    def start(c):
        word = chunk_ref[tile * CHUNK_SLOTS + c]
        expert = lax.shift_right_logical(word, CHUNK_EXPERT_SHIFT)
        region_slab = word & ((1 << CHUNK_EXPERT_SHIFT) - 1)
        make_copy(c, pslab_ref[expert] + region_slab, 1).start()

    def group(g, carry):
        for u in range(CHUNK_UNROLL):
            start(g * CHUNK_UNROLL + u)
        return carry

    def single(c, carry):
        start(c)
        return carry

    groups = n // CHUNK_UNROLL
    lax.fori_loop(0, groups, group, 0)
    lax.fori_loop(groups * CHUNK_UNROLL, n, single, 0)


def _wait_chunks(tile, nch_ref, make_copy):
    n = nch_ref[tile]
    many = n // WAIT_CHUNKS

    def wait_many(j, carry):
        make_copy(0, 0, WAIT_CHUNKS).wait()
        return carry

    def wait_one(j, carry):
        make_copy(0, 0, 1).wait()
        return carry

    lax.fori_loop(0, many, wait_many, 0)
    lax.fori_loop(many * WAIT_CHUNKS, n, wait_one, 0)


def _dispatch_kernel(chunk_ref, nch_ref, pstart_ref, pend_ref, rend_ref, pos_ref, h2_ref, xs_hbm,
                     cbuf_ref, zero_ref, zsem, sem, *, tm):
    i = pl.program_id(0)
    nt = pl.num_programs(0)
    dh = cbuf_ref.shape[-1]

    def zero_piece(slab):
        return pltpu.make_async_copy(zero_ref, xs_hbm.at[pl.ds(slab, ZERO_SLABS)], zsem)

    def zero_fill(op):
        def pieces(start, stop):
            first = (start // ZERO_SLABS) * ZERO_SLABS

            def piece(k, carry):
                op(zero_piece(first + k * ZERO_SLABS))
                return carry

            lax.fori_loop(0, (stop - first) // ZERO_SLABS, piece, 0)

        def region(e, carry):
            pieces(rend_ref[e], pend_ref[e])
            return carry

        lax.fori_loop(0, N_EXPERTS, region, 0)
        pieces(pend_ref[N_EXPERTS - 1], xs_hbm.shape[0])

    @pl.when(i == 0)
    def _():
        zero_ref[...] = jnp.zeros(zero_ref.shape, zero_ref.dtype)
        zero_fill(lambda c: c.start())
        zero_fill(lambda c: c.wait())

    pos = pos_ref[...]
    piota = lax.broadcasted_iota(I32, (GROUPED_ROWS, tm), 0).astype(jnp.int16)
    pos16 = pos.astype(jnp.int16)
    one = jnp.ones((GROUPED_ROWS, tm), BF16)
    perm = jnp.zeros((GROUPED_ROWS, tm), BF16)
    for r in range(TOP_K):
        perm = jnp.where(piota == pos16[r:r + 1, :], one, perm)
    cur = lax.rem(i, 2)
    grouped = _pack_exact_bf16_pair(_dot(perm, h2_ref[:, 0:dh]), _dot(perm, h2_ref[:, dh:]))
    cbuf_ref[cur] = grouped.reshape(cbuf_ref.shape[1:])

    def make_copy(buf):
        def build(tile_slab, buffer_slab, slabs):
            return pltpu.make_async_copy(cbuf_ref.at[buf, pl.ds(tile_slab, slabs)],
                                         xs_hbm.at[pl.ds(buffer_slab, slabs)], sem.at[buf])
        return build

    _start_chunks(i, chunk_ref, nch_ref, pstart_ref, make_copy(cur))

    @pl.when(i > 0)
    def _():
        _wait_chunks(i - 1, nch_ref, make_copy(1 - cur))

    @pl.when(i == nt - 1)
    def _():
        _wait_chunks(i, nch_ref, make_copy(cur))


def _dispatch(chunks, nch, pstart, pend, rend, pos, h2, nrows, tm):
    n, d = h2.shape
    grid_spec = pltpu.PrefetchScalarGridSpec(
        num_scalar_prefetch=5,
        grid=(n // tm,),
        in_specs=[
            pl.BlockSpec((TOP_K, tm), lambda i, *_: (0, i)),
            pl.BlockSpec((tm, d), lambda i, *_: (i, 0)),
        ],
        out_specs=pl.BlockSpec(memory_space=pl.ANY),
        scratch_shapes=[
            pltpu.VMEM((2, GROUPED_ROWS // ROW_GROUP, ROW_GROUP, d // 2), U32),
            pltpu.VMEM((ZERO_SLABS, ROW_GROUP, d // 2), U32),
            pltpu.SemaphoreType.DMA(()),
            pltpu.SemaphoreType.DMA((2,)),
        ],
    )
    return pl.pallas_call(
        functools.partial(_dispatch_kernel, tm=tm),
        grid_spec=grid_spec,
        out_shape=jax.ShapeDtypeStruct((nrows // ROW_GROUP, ROW_GROUP, d // 2), U32),
        compiler_params=_params(("arbitrary",)),
        name="moe_dispatch",
    )(chunks, nch, pstart, pend, rend, pos, h2)


def _experts_kernel(be_ref, nu_ref, valid_ref, first_ref, slot_ref, next_ref, x_ref, wgu_hbm, wd_hbm,
                    o_ref, wgu_f_ref, wd_f_ref, wgu_b_ref, wd_b_ref, sem, *, bm):
    i = pl.program_id(0)
    valid = valid_ref[i]
    sub = bm // 2

    def weight_copies(expert, slot):
        return (pltpu.make_async_copy(wgu_hbm.at[expert], wgu_f_ref.at[slot], sem.at[slot, 0]),
                pltpu.make_async_copy(wd_hbm.at[expert], wd_f_ref.at[slot], sem.at[slot, 1]))

    @pl.when(i == 0)
    def _():
        for c in weight_copies(be_ref[0], 0):
            c.start()

    @pl.when(first_ref[i] == 1)
    def _():
        slot = slot_ref[i]
        for c in weight_copies(be_ref[i], slot):
            c.wait()

        @pl.when(next_ref[i] >= 0)
        def _():
            for c in weight_copies(next_ref[i], 1 - slot):
                c.start()

        wgu_b_ref[...] = wgu_f_ref[slot].astype(BF16)
        wd_b_ref[...] = wd_f_ref[slot].astype(BF16)

    for r0 in (0, sub):
        rows = slice(r0, r0 + sub)

        @pl.when(valid > r0)
        def _():
            lo, hi = _unpack_bf16_pair(x_ref[rows, :])
            x = jnp.concatenate([lo.astype(BF16), hi.astype(BF16)], axis=1)
            gu = _dot(x, wgu_b_ref[...])
            act = (_silu(gu[:, 0:D_EXPERT]) * gu[:, D_EXPERT:]).astype(BF16)
            y = _dot(act, wd_b_ref[...])
            o_ref[rows, :] = _pack_bf16_pair(y[:, 0:D_MODEL // 2], y[:, D_MODEL // 2:])

        @pl.when(valid <= r0)
        def _():
            o_ref[rows, :] = jnp.zeros((sub, o_ref.shape[1]), o_ref.dtype)


def _experts(block_e, nused, valid, first, slot, next_e, xs, w_gate_up_l, w_down_l, bm):
    nrows, dh = xs.shape
    d = 2 * dh
    nblk = nrows // bm
    grid_spec = pltpu.PrefetchScalarGridSpec(
        num_scalar_prefetch=6,
        grid=(nblk,),
        in_specs=[
            pl.BlockSpec((bm, dh), lambda i, be, nu, *_: (jnp.minimum(i, nu[0] - 1), 0)),
            pl.BlockSpec(memory_space=pl.ANY),
            pl.BlockSpec(memory_space=pl.ANY),
        ],
        out_specs=pl.BlockSpec((bm, dh), lambda i, *_: (i, 0)),
        scratch_shapes=[
            pltpu.VMEM((2, d, 2 * D_EXPERT), w_gate_up_l.dtype),
            pltpu.VMEM((2, D_EXPERT, d), w_down_l.dtype),
            pltpu.VMEM((d, 2 * D_EXPERT), BF16),
            pltpu.VMEM((D_EXPERT, d), BF16),
            pltpu.SemaphoreType.DMA((2, 2)),
        ],
    )
    return pl.pallas_call(
        functools.partial(_experts_kernel, bm=bm),
        grid_spec=grid_spec,
        out_shape=jax.ShapeDtypeStruct((nrows, dh), U32),
        compiler_params=_params(("arbitrary",)),
        name="moe_experts",
    )(block_e, nused, valid, first, slot, next_e, xs, w_gate_up_l, w_down_l)


def _combine_kernel(chunk_ref, nch_ref, pstart_ref, pos_ref, wl_ref, xs_ref, ga2_ref, gf_ref,
                    yb_hbm, oa_ref, ob_ref, gbuf_ref, sem, *, tm, nta):
    i = pl.program_id(0)
    nt = pl.num_programs(0)
    cur = lax.rem(i, 2)
    refs = (chunk_ref, nch_ref, pstart_ref)

    def make_copy(buf):
        def build(tile_slab, buffer_slab, slabs):
            return pltpu.make_async_copy(yb_hbm.at[pl.ds(buffer_slab, slabs)],
                                         gbuf_ref.at[buf, pl.ds(tile_slab, slabs)], sem.at[buf])
        return build

    @pl.when(i == 0)
    def _():
        gbuf_ref[...] = jnp.zeros(gbuf_ref.shape, gbuf_ref.dtype)
        _start_chunks(0, *refs, make_copy(0))

    @pl.when(i + 1 < nt)
    def _():
        _start_chunks(i + 1, *refs, make_copy(1 - cur))

    _wait_chunks(i, nch_ref, make_copy(cur))

    lo, hi = _unpack_bf16_pair(gbuf_ref[cur].reshape(GROUPED_ROWS, gbuf_ref.shape[-1]))
    g = jnp.concatenate([lo.astype(BF16), hi.astype(BF16)], axis=1)
    pos16 = pos_ref[...].astype(jnp.int16)
    wl = wl_ref[...].astype(BF16)
    liota = lax.broadcasted_iota(I32, (tm, GROUPED_ROWS), 1).astype(jnp.int16)
    a = jnp.zeros((tm, GROUPED_ROWS), BF16)
    for r in range(TOP_K):
        a = jnp.where(liota == pos16[:, r:r + 1], jnp.broadcast_to(wl[:, r:r + 1], a.shape), a)
    routed = _dot(a, g)
    x2 = xs_ref[...] + ga2_ref[...] * routed
    ms = jnp.mean(x2 * x2, axis=-1, keepdims=True)
    y = x2 * lax.rsqrt(ms + EPS) * gf_ref[...]

    @pl.when(i < nta)
    def _():
        oa_ref[...] = y

    @pl.when(i >= nta)
    def _():
        ob_ref[...] = y


def _combine(chunks, nch, pstart, pos_t, wl_t, xs_base, ga2, gfin, yb, tm, nta):
    n, d = xs_base.shape
    ntb = n // tm - nta
    ga2_spec = pl.BlockSpec((None, tm, d), lambda i, *_: (jnp.minimum(i // nta, 1), 0, 0))
    grid_spec = pltpu.PrefetchScalarGridSpec(
        num_scalar_prefetch=3,
        grid=(n // tm,),
        in_specs=[
            pl.BlockSpec((tm, TOP_K), lambda i, *_: (i, 0)),
            pl.BlockSpec((tm, TOP_K), lambda i, *_: (i, 0)),
            pl.BlockSpec((tm, d), lambda i, *_: (i, 0)),
            ga2_spec,
            pl.BlockSpec((1, d), lambda i, *_: (0, 0)),
            pl.BlockSpec(memory_space=pl.ANY),
        ],
        out_specs=[
            pl.BlockSpec((tm, d), lambda i, *_: (jnp.minimum(i, nta - 1), 0)),
            pl.BlockSpec((tm, d), lambda i, *_: (jnp.maximum(i - nta, 0), 0)),
        ],
        scratch_shapes=[
            pltpu.VMEM((2, GROUPED_ROWS // ROW_GROUP, ROW_GROUP, d // 2), U32),
            pltpu.SemaphoreType.DMA((2,)),
        ],
    )
    return pl.pallas_call(
        functools.partial(_combine_kernel, tm=tm, nta=nta),
        grid_spec=grid_spec,
        out_shape=[jax.ShapeDtypeStruct((nta * tm, d), F32), jax.ShapeDtypeStruct((ntb * tm, d), F32)],
        compiler_params=_params(("arbitrary",)),
        name="moe_combine",
    )(chunks, nch, pstart, pos_t, wl_t, xs_base, ga2, gfin, yb)


def _moe_and_final(src_a, src_b, mods, wts, tm, bm):
    nta = src_a[0].shape[0] // tm
    ga1, sh2, sc2, ga2 = mods
    (w_out_b, g_ffn, wsgu_b, wsd_b, wr_t, rb, w_gate_up_l, w_down_l, g_final) = wts
    xs_base, h2, pos, wl, chunks, nch, cnt = _post(
        src_a, src_b, w_out_b, ga1, g_ffn, sc2, sh2, ga2, wsgu_b, wsd_b, wr_t, rb, tm)
    nt = xs_base.shape[0] // tm
    counts = cnt[:, 0].astype(I32)
    padded = (counts + bm - 1) // bm * bm
    pend = jnp.cumsum(padded)
    pstart = pend - padded
    max_rows = nt * (tm * TOP_K + N_EXPERTS * (ROW_GROUP - 1))
    nblk = -(-max_rows // bm) + N_EXPERTS
    nused = (pend[-1] // bm).astype(I32)
    blk_row = jnp.minimum(jnp.arange(nblk, dtype=I32), nused - 1) * bm
    be = jnp.sum((pend[None, :] <= blk_row[:, None]).astype(I32), axis=1)
    chunks = chunks.reshape(-1)
    nch = nch[:, 0, 0]
    pslab = pstart // ROW_GROUP
    xs = _dispatch(chunks, nch, pslab, pend // ROW_GROUP, (pstart + counts) // ROW_GROUP, pos, h2,
                   nblk * bm, tm)
    xs = xs.reshape(nblk * bm, xs.shape[-1])
    region_end = jnp.sum(jnp.where(be[:, None] == jnp.arange(N_EXPERTS, dtype=I32)[None, :],
                                   (pstart + counts)[None, :], 0), axis=1)
    valid = jnp.clip(region_end - jnp.arange(nblk, dtype=I32) * bm, 0, bm)
    blk = jnp.arange(nblk, dtype=I32)
    first = ((blk < nused) & ((blk == 0) | (be != jnp.roll(be, 1)))).astype(I32)
    slot = (jnp.cumsum(first) - 1) % 2
    eids = jnp.arange(N_EXPERTS, dtype=I32)
    later_nonempty = (eids[None, :] > eids[:, None]) & (counts[None, :] > 0)
    next_tab = jnp.min(jnp.where(later_nonempty, eids[None, :], N_EXPERTS), axis=1)
    next_tab = jnp.where(next_tab == N_EXPERTS, -1, next_tab)
    next_e = jnp.sum(jnp.where(be[:, None] == eids[None, :], next_tab[None, :], 0), axis=1)
    yb = _experts(be, nused.reshape(1), valid, first, slot.astype(I32), next_e.astype(I32), xs,
                  w_gate_up_l, w_down_l, bm)
    yb = yb.reshape(nblk * bm // ROW_GROUP, ROW_GROUP, yb.shape[-1])
    return _combine(chunks, nch, pslab, pos.T, wl.T, xs_base, ga2, g_final, yb, tm, nta)


def _expand(mod, reps):
    if mod.shape[0] == 1:
        return mod
    return jnp.repeat(mod, reps, axis=0)


def kernel(x_prompt, x_sample, cache_k, cache_v, state_hgrn, c_prompt, c_sample, w_ada, b_ada,
           norm_mix, norm_ffn, norm_final, w_in, w_out, hg_lb_logits, hg_norm, da_lambda, da_norm,
           rel_bias_table, w_router, router_bias, w_gate_up, w_down, ws_gate_up, ws_down):
    depth = w_in.shape[0]
    assert depth == 1 and hg_lb_logits.shape[0] == 2
    bp, tp, d = x_prompt.shape
    bs, ts, _ = x_sample.shape
    assert bp == 1
    past = cache_k.shape[2]
    l = 0

    rows = -(-(bp + bs) // 8) * 8
    c_all = jnp.zeros((rows, d), F32).at[:bp].set(c_prompt).at[bp:bp + bs].set(c_sample)
    mod = _adaln(c_all, w_ada[l], b_ada[l])
    mod_p = [mod[0:bp, j * d:(j + 1) * d] for j in range(6)]
    mod_s = [_expand(mod[bp:bp + bs, j * d:(j + 1) * d], ts) for j in range(6)]

    w_in_b = w_in[l].astype(BF16)
    w_out_b = w_out[l].astype(BF16)
    wsgu_b = ws_gate_up[l].astype(BF16)
    wsd_b = ws_down[l].astype(BF16)
    wr_t = w_router[l].T
    g_mix = norm_mix[l].reshape(1, d)
    g_ffn = norm_ffn[l].reshape(1, d)
    g_final = norm_final.reshape(1, d)
    moe_w = (w_out_b, g_ffn, wsgu_b, wsd_b, wr_t, router_bias[l], w_gate_up[l], w_down[l], g_final)

    lam = _lam(da_lambda[l])

    t_att = min(ATT_TILE, tp)
    kk = jnp.arange(t_att, dtype=I32)[:, None]
    qq = jnp.arange(t_att, dtype=I32)[None, :]
    idx_diag = jnp.where((kk // CHUNK) <= (qq // CHUNK), _rel_bucket(kk - qq), MASK_BUCKET)
    idx_prev = _rel_bucket(kk - qq - t_att)
    bias_p = _bias_tiles(rel_bias_table, jnp.stack([idx_diag, idx_prev]).astype(I32),
                         shift_bucket=N_BUCKETS // 2 - 1)
    pad = 128
    qpos = past + jnp.arange(ts, dtype=I32)[:, None]
    idx_sp = _rel_bucket(jnp.arange(past, dtype=I32)[None, :] - qpos)
    kn = jnp.arange(pad, dtype=I32)[None, :]
    idx_sn = jnp.where(kn < ts, _rel_bucket(past + kn - qpos), MASK_BUCKET)
    bias_sp = _bias_tiles(rel_bias_table, idx_sp[None].astype(I32), shift_bucket=None)
    bias_sn = _bias_tiles(rel_bias_table, idx_sn[None].astype(I32), shift_bucket=None)

    xp = x_prompt.reshape(bp * tp, d)
    sh1, sc1, ga1, sh2, sc2, ga2 = mod_p
    assert ATT_TILE == INPROJ_TILE
    zh, qt, kf, vf, kb, vt = _inproj(xp, g_mix, sc1, sh1, w_in_b, t_att, True)
    s_zero = jnp.zeros((bp, HG_HEADS, HG_DIM, HG_DIM), F32)
    ohg_p, sp_new = _hgrn(zh, s_zero, hg_lb_logits, hg_norm[l], bp, tp, min(HGRN_CHUNK, tp))
    oda_p = _attn_prompt(kb, qt, vt, bias_p, lam, da_norm[l], t_att)
    src_p = (xp, ohg_p, oda_p)
    mods_p = (ga1, sh2, sc2, ga2)
    k_prompt = kf.reshape(1, bp, tp, DA_HEADS, 2 * DA_QKDIM)
    v_prompt = vf.reshape(1, bp, tp, DA_HEADS, DA_VDIM)

    ns = bs * ts
    xs_ = x_sample.reshape(ns, d)
    sh1, sc1, ga1, sh2, sc2, ga2 = mod_s
    zh, qs, kf, vf, kb, vb = _inproj(xs_, g_mix, sc1, sh1, w_in_b, ns, False)
    ohg_s, ss_new = _hgrn(zh, state_hgrn[l], hg_lb_logits, hg_norm[l], bs, ts, ts)
    oda_s = _attn_step(qs, cache_k[l], cache_v[l], kb, vb, bias_sp, bias_sn, lam, da_norm[l], bs, ts)
    assert ns == POST_TILE
    mods = tuple(jnp.stack([jnp.broadcast_to(mp, (POST_TILE, d)), ms_])
                 for mp, ms_ in zip(mods_p, (ga1, sh2, sc2, ga2)))
    y_p, y_s = _moe_and_final(src_p, (xs_, ohg_s, oda_s), mods, moe_w, POST_TILE, MOE_BLOCK_ROWS)
    k_sample = kf.reshape(1, bs, ts, DA_HEADS, 2 * DA_QKDIM)
    v_sample = vf.reshape(1, bs, ts, DA_HEADS, DA_VDIM)

    return (y_p.reshape(bp, tp, d), y_s.reshape(bs, ts, d), k_prompt, v_prompt, sp_new[None],
            k_sample, v_sample, ss_new[None].astype(x_sample.dtype))
```

```python
import functools
import math

import numpy as np
import jax
import jax.numpy as jnp
from jax import lax
from jax.experimental import pallas as pl
from jax.experimental.pallas import tpu as pltpu

F32 = jnp.float32
BF16 = jnp.bfloat16
I32 = jnp.int32
U32 = jnp.uint32
HIGHEST = lax.Precision.HIGHEST

D_MODEL = 1024
CHUNK = 64
HG_HEADS = 4
HG_DIM = 128
HG_WIDTH = HG_HEADS * HG_DIM
DA_HEADS = 4
DA_VDIM = 128
DA_QKDIM = 64
DA_WIDTH = DA_HEADS * DA_VDIM
N_BUCKETS = 32
MAX_DIST = 128
N_EXPERTS = 64
TOP_K = 8
N_GROUPS = 8
GROUP_SIZE = N_EXPERTS // N_GROUPS
TOP_GROUPS = 4
D_EXPERT = 256
ROUTE_SCALE = 2.5
EPS = 1e-6
LAM_INIT = 0.8 - 0.6 * math.exp(-0.3 * 0)

LOG2E = math.log2(math.e)
HI_MASK = np.uint32(0xFFFF0000)
NEG_BIG = -1e30
MASK_BUCKET = N_BUCKETS
V7X_VMEM_LIMIT = 48 * 1024 * 1024

ATT_TILE = 512
VT_ROWS = DA_VDIM + 16
HGRN_CHUNK = 256
INPROJ_TILE = 512
POST_TILE = 256
MOE_BLOCK_ROWS = 1024
ROW_GROUP = 8
GROUPED_ROWS = -(-(POST_TILE * TOP_K + N_EXPERTS * (ROW_GROUP - 1)) // 256) * 256
CHUNK_SLOTS = -(-(GROUPED_ROWS // ROW_GROUP) // 128) * 128
CHUNK_EXPERT_SHIFT = 24
CHUNK_UNROLL = 8
WAIT_CHUNKS = 16
ZERO_SLABS = 16


def _sigmoid(x):
    return 1.0 / (1.0 + jnp.exp(-x))


def _silu(x):
    return x * _sigmoid(x)


def _dot(a, b, **kw):
    return jnp.dot(a, b, preferred_element_type=F32, **kw)


def _dot_nt(a, b, **kw):
    return lax.dot_general(a, b, (((1,), (1,)), ((), ())), preferred_element_type=F32, **kw)


def _dot_tn(a, b, **kw):
    return lax.dot_general(a, b, (((0,), (0,)), ((), ())), preferred_element_type=F32, **kw)


def _pack_bf16_pair(lo, hi):
    lo_bits = lax.bitcast_convert_type(lo.astype(BF16).astype(F32), U32)
    hi_bits = lax.bitcast_convert_type(hi.astype(BF16).astype(F32), U32)
    return (lo_bits >> 16) | (hi_bits & HI_MASK)


def _pack_exact_bf16_pair(lo, hi):
    return (lax.bitcast_convert_type(lo, U32) >> 16) | (lax.bitcast_convert_type(hi, U32) & HI_MASK)


def _unpack_bf16_pair(w):
    lo = lax.bitcast_convert_type(w << 16, F32)
    hi = lax.bitcast_convert_type(w & HI_MASK, F32)
    return lo, hi


def _params(sem, vmem=V7X_VMEM_LIMIT, flags=None):
    return pltpu.CompilerParams(dimension_semantics=sem, vmem_limit_bytes=vmem, flags=flags)


def _adaln_kernel(c_ref, w_ref, b_ref, o_ref):
    s = _silu(c_ref[...])
    o_ref[...] = _dot(s, w_ref[...], precision=HIGHEST) + b_ref[...]


def _adaln(c_all, w_ada, b_ada):
    rows, d = c_all.shape
    cols = w_ada.shape[1]
    blk = 1024
    return pl.pallas_call(
        _adaln_kernel,
        grid=(cols // blk,),
        in_specs=[
            pl.BlockSpec((rows, d), lambda j: (0, 0)),
            pl.BlockSpec((d, blk), lambda j: (0, j)),
            pl.BlockSpec((1, blk), lambda j: (0, j)),
        ],
        out_specs=pl.BlockSpec((rows, blk), lambda j: (0, j)),
        out_shape=jax.ShapeDtypeStruct((rows, cols), F32),
        compiler_params=_params(("parallel",)),
        name="adaln",
    )(c_all, w_ada, b_ada.reshape(1, cols))


def _lam_kernel(l_ref, o_ref):
    l = l_ref[...].astype(F32)
    a = jnp.sum(l[0:1] * l[1:2], axis=-1, keepdims=True)
    b = jnp.sum(l[2:3] * l[3:4], axis=-1, keepdims=True)
    lam = jnp.exp(a) - jnp.exp(b) + LAM_INIT
    o_ref[...] = jnp.broadcast_to(lam, o_ref.shape)


def _lam(da_lambda_l):
    return pl.pallas_call(
        _lam_kernel,
        out_shape=jax.ShapeDtypeStruct((8, 128), F32),
        name="lam",
    )(da_lambda_l)


def _rel_bucket(rel):
    nb = N_BUCKETS // 2
    max_exact = nb // 2
    side = jnp.where(rel > 0, nb, 0)
    n = jnp.abs(rel)
    large = max_exact + (jnp.log(jnp.maximum(n, 1).astype(F32) / max_exact)
                         / math.log(MAX_DIST / max_exact) * (nb - max_exact)).astype(I32)
    large = jnp.minimum(large, nb - 1)
    return side + jnp.where(n < max_exact, n, large)


def _bias_kernel(tab_ref, idx_ref, o_ref, *, shift_bucket):
    h = pl.program_id(0)
    idx = idx_ref[...]
    shift = tab_ref[shift_bucket, h] if shift_bucket is not None else 0.0
    acc = jnp.zeros(idx.shape, F32)
    for j in range(N_BUCKETS):
        acc = jnp.where(idx == j, (tab_ref[j, h] - shift) * LOG2E, acc)
    o_ref[...] = jnp.where(idx == MASK_BUCKET, NEG_BIG, acc)


def _bias_tiles(table, idx, *, shift_bucket):
    k, r, c = idx.shape
    return pl.pallas_call(
        functools.partial(_bias_kernel, shift_bucket=shift_bucket),
        grid=(DA_HEADS, k),
        in_specs=[
            pl.BlockSpec(memory_space=pltpu.SMEM),
            pl.BlockSpec((None, r, c), lambda h, d: (d, 0, 0)),
        ],
        out_specs=pl.BlockSpec((None, None, r, c), lambda h, d: (h, d, 0, 0)),
        out_shape=jax.ShapeDtypeStruct((DA_HEADS, k, r, c), F32),
        compiler_params=_params(("parallel", "parallel")),
        name="rel_bias",
    )(table, idx)


def _inproj_kernel(x_ref, g_ref, sc_ref, sh_ref, w_ref,
                   zh_ref, q_ref, k_ref, v_ref, kb_ref, vb_ref, *, transposed):
    x = x_ref[...]
    ms = jnp.mean(x * x, axis=-1, keepdims=True)
    h = x * lax.rsqrt(ms + EPS) * g_ref[...]
    h = h * (1.0 + sc_ref[...]) + sh_ref[...]
    hb = h.astype(BF16)
    c0 = 4 * HG_WIDTH
    zh_ref[...] = _dot(hb, w_ref[:, 0:c0])
    zq = _dot(hb, w_ref[:, c0:c0 + DA_WIDTH]) * (DA_QKDIM ** -0.5 * LOG2E)
    zk = _dot(hb, w_ref[:, c0 + DA_WIDTH:c0 + 2 * DA_WIDTH])
    for hd in range(DA_HEADS):
        k_ref[:, hd, :] = zk[:, hd * DA_VDIM:(hd + 1) * DA_VDIM]
    kb_ref[...] = zk.astype(BF16)
    zv = _dot(hb, w_ref[:, c0 + 2 * DA_WIDTH:c0 + 3 * DA_WIDTH])
    for hd in range(DA_HEADS):
        v_ref[:, hd, :] = zv[:, hd * DA_VDIM:(hd + 1) * DA_VDIM]
    if transposed:
        q_ref[...] = zq.T.astype(BF16)
        vb_ref[:, 0:DA_VDIM, :] = zv.T.astype(BF16).reshape(DA_HEADS, DA_VDIM, zv.shape[0])
        vb_ref[:, DA_VDIM:, :] = jnp.ones((DA_HEADS, VT_ROWS - DA_VDIM, zv.shape[0]), BF16)
    else:
        q_ref[...] = zq.astype(BF16)
        vb_ref[...] = zv.astype(BF16)


def _mod_spec(mod, tm):
    if mod.shape[0] == 1:
        return pl.BlockSpec((1, mod.shape[1]), lambda i: (0, 0))
    return pl.BlockSpec((tm, mod.shape[1]), lambda i: (i, 0))


def _inproj(x, g, sc, sh, w_in_b, tm, transposed):
    n, d = x.shape
    cols = w_in_b.shape[1]
    row = lambda i: (i, 0)
    if transposed:
        q_spec = pl.BlockSpec((DA_WIDTH, tm), lambda i: (0, i))
        q_shape = jax.ShapeDtypeStruct((DA_WIDTH, n), BF16)
        vb_spec = pl.BlockSpec((DA_HEADS, None, VT_ROWS, tm), lambda i: (0, i, 0, 0))
        vb_shape = jax.ShapeDtypeStruct((DA_HEADS, n // tm, VT_ROWS, tm), BF16)
    else:
        q_spec = vb_spec = pl.BlockSpec((tm, DA_WIDTH), row)
        q_shape = vb_shape = jax.ShapeDtypeStruct((n, DA_WIDTH), BF16)
    return pl.pallas_call(
        functools.partial(_inproj_kernel, transposed=transposed),
        grid=(n // tm,),
        in_specs=[
            pl.BlockSpec((tm, d), row),
            pl.BlockSpec((1, d), lambda i: (0, 0)),
            _mod_spec(sc, tm),
            _mod_spec(sh, tm),
            pl.BlockSpec((d, cols), lambda i: (0, 0)),
        ],
        out_specs=[
            pl.BlockSpec((tm, 4 * HG_WIDTH), row),
            q_spec,
            pl.BlockSpec((tm, DA_HEADS, DA_VDIM), lambda i: (i, 0, 0)),
            pl.BlockSpec((tm, DA_HEADS, DA_VDIM), lambda i: (i, 0, 0)),
            pl.BlockSpec((tm, DA_WIDTH), row),
            vb_spec,
        ],
        out_shape=[
            jax.ShapeDtypeStruct((n, 4 * HG_WIDTH), F32),
            q_shape,
            jax.ShapeDtypeStruct((n, DA_HEADS, DA_VDIM), F32),
            jax.ShapeDtypeStruct((n, DA_HEADS, DA_VDIM), F32),
            jax.ShapeDtypeStruct((n, DA_WIDTH), BF16),
            vb_shape,
        ],
        compiler_params=_params(("parallel",)),
        name="inproj",
    )(x, g, sc, sh, w_in_b)


def _hgrn_consts(c):
    levels = int(round(math.log2(c)))
    assert 1 << levels == c and levels >= 3
    t = np.arange(c)[:, None]
    r = np.arange(c)[None, :]
    tri = (r <= t).astype(np.float32)
    x = np.maximum(t ^ r, 1)
    lv = np.where(t == r, -1, np.where(t > r, np.floor(np.log2(x)).astype(np.int64), -2))
    return jnp.asarray(tri, dtype=BF16), jnp.asarray(lv, dtype=I32), levels


def _hgrn_kernel(zh_ref, s0_ref, lbl_ref, gain_ref, mall_ref, lv_ref,
                 o_ref, sout_ref, st_ref, b_ref, *, c, levels):
    ci = pl.program_id(1)

    @pl.when(ci == 0)
    def _():
        for h in range(HG_HEADS):
            st_ref[h] = s0_ref[h].astype(F32).T

    lbl = lbl_ref[...].astype(F32)
    mx = jnp.maximum(lbl[0:1], lbl[1:2])
    e0 = jnp.exp(lbl[0:1] - mx)
    e1 = jnp.exp(lbl[1:2] - mx)
    lb = e0 / (e0 + e1)

    xq = zh_ref[:, 0:HG_WIDTH]
    xf = zh_ref[:, HG_WIDTH:2 * HG_WIDTH]
    q = _silu(xq)
    y = lb + (1.0 - lb) * _sigmoid(xf)
    logf = jnp.log(y)
    kk = 1.0 - y

    l1 = logf.astype(BF16)
    r1 = logf - l1.astype(F32)
    l2 = r1.astype(BF16)
    l3 = (r1 - l2.astype(F32)).astype(BF16)
    tri = mall_ref[...]
    b = _dot(tri, l1) + _dot(tri, l2) + _dot(tri, l3)
    b_ref[...] = b
    trow = lax.broadcasted_iota(I32, (c, HG_DIM), 0)

    def level_factor(l, sl):
        m = 1 << l
        later = (trow & m) != 0
        lf = logf[:, sl]
        if l == 0:
            e = jnp.where(later, lf, 0.0)
        elif l == 1:
            below = pltpu.roll(lf, 1, 0)
            above = pltpu.roll(lf, c - 1, 0)
            low = (trow & 1) != 0
            e = jnp.where(later, jnp.where(low, lf + below, lf), jnp.where(low, 0.0, above))
        else:
            mid = jnp.concatenate(
                [jnp.broadcast_to(b_ref[k * 2 * m + m - 1:k * 2 * m + m, sl], (2 * m, HG_DIM))
                 for k in range(c // (2 * m))], axis=0)
            e = jnp.where(later, b[:, sl] - mid, mid - b[:, sl])
        return jnp.exp(e)

    lv = lv_ref[...].astype(jnp.int16)
    gain = gain_ref[...].astype(F32)
    for h in range(HG_HEADS):
        sl = slice(h * HG_DIM, (h + 1) * HG_DIM)
        qh = q[:, sl]
        kh = kk[:, sl]
        ih = zh_ref[:, 2 * HG_WIDTH + h * HG_DIM:2 * HG_WIDTH + (h + 1) * HG_DIM]
        gh = zh_ref[:, 3 * HG_WIDTH + h * HG_DIM:3 * HG_WIDTH + (h + 1) * HG_DIM]
        bh = b[:, sl]
        ihb = ih.astype(BF16)
        a = jnp.where(lv == -1, _dot_nt(qh.astype(BF16), kh.astype(BF16)).astype(BF16),
                      jnp.zeros((c, c), BF16))
        for l in range(levels):
            f = level_factor(l, sl)
            p = _dot_nt((qh * f).astype(BF16), (kh * f).astype(BF16))
            a = jnp.where(lv == l, p.astype(BF16), a)
        st = st_ref[h]
        o = _dot(a, ihb) + _dot_nt((qh * jnp.exp(bh)).astype(BF16), st.astype(BF16))
        bl = bh[c - 1:c, :]
        kd = (kh * jnp.exp(bl - bh)).astype(BF16)
        st_ref[h] = st * jnp.exp(bl) + _dot_tn(ihb, kd)
        ms = jnp.mean(o * o, axis=-1, keepdims=True)
        on = o * lax.rsqrt(ms + EPS) * gain
        o_ref[:, sl] = (on * _silu(gh)).astype(o_ref.dtype)

    @pl.when(ci == pl.num_programs(1) - 1)
    def _():
        for h in range(HG_HEADS):
            sout_ref[h] = st_ref[h].T.astype(sout_ref.dtype)


def _hgrn(zh, s0, lb_logits, gain, batch, seq, c):
    mall, lv, levels = _hgrn_consts(c)
    nc = seq // c
    return pl.pallas_call(
        functools.partial(_hgrn_kernel, c=c, levels=levels),
        grid=(batch, nc),
        in_specs=[
            pl.BlockSpec((c, 4 * HG_WIDTH), lambda b, i: (b * nc + i, 0)),
            pl.BlockSpec((None, HG_HEADS, HG_DIM, HG_DIM), lambda b, i: (b, 0, 0, 0)),
            pl.BlockSpec(lb_logits.shape, lambda b, i: (0, 0)),
            pl.BlockSpec((1, HG_DIM), lambda b, i: (0, 0)),
            pl.BlockSpec(mall.shape, lambda b, i: (0, 0)),
            pl.BlockSpec(lv.shape, lambda b, i: (0, 0)),
        ],
        out_specs=[
            pl.BlockSpec((c, HG_WIDTH), lambda b, i: (b * nc + i, 0)),
            pl.BlockSpec((None, HG_HEADS, HG_DIM, HG_DIM), lambda b, i: (b, 0, 0, 0)),
        ],
        out_shape=[
            jax.ShapeDtypeStruct((batch * seq, HG_WIDTH), BF16),
            jax.ShapeDtypeStruct((batch, HG_HEADS, HG_DIM, HG_DIM), F32),
        ],
        scratch_shapes=[pltpu.VMEM((HG_HEADS, HG_DIM, HG_DIM), F32),
                        pltpu.VMEM((c, HG_WIDTH), F32)],
        compiler_params=_params(("parallel", "arbitrary")),
        name="hgrn2",
    )(zh, s0, lb_logits, gain.reshape(1, HG_DIM), mall, lv)


def _attn_kernel(k_ref, qt_ref, vt_ref, bias_ref, lam_ref, gain_ref,
                 o_ref, qz_ref, m_ref, acc_ref, s_ref, smax_ref, *, t):
    i = pl.program_id(1)
    qt = qt_ref[...]
    row = lax.broadcasted_iota(I32, qt.shape, 0)
    zero = jnp.zeros_like(qt)
    qz_ref[:, 0:t] = jnp.where(row < DA_QKDIM, qt, zero)
    qz_ref[:, t:2 * t] = jnp.where(row >= DA_QKDIM, qt, zero)
    m_ref[...] = jnp.full(m_ref.shape, NEG_BIG, F32)
    acc_ref[...] = jnp.zeros(acc_ref.shape, F32)

    def scores(j, buf):
        kt = k_ref[pl.ds(pl.multiple_of(j * t, t), t), :]
        s = _dot(kt, qz_ref[...])
        s_ref[buf] = s
        smax_ref[buf] = jnp.max(s, axis=0, keepdims=True)

    def consume(j, buf, bias_idx):
        s = s_ref[buf]
        if bias_idx is not None:
            b = bias_ref[bias_idx]
            s = jnp.concatenate([s[:, 0:t] + b, s[:, t:2 * t] + b], axis=1)
            s_max = jnp.max(s, axis=0, keepdims=True)
        else:
            s_max = smax_ref[buf]
        m_prev = m_ref[...]
        m_new = jnp.maximum(m_prev, s_max)
        alpha = jnp.exp2(m_prev - m_new)
        pr = jnp.exp2(s - m_new).astype(BF16)
        acc_ref[...] = alpha * acc_ref[...] + _dot(vt_ref[j], pr)
        m_ref[...] = m_new

    n_far = jnp.maximum(i - 1, 0)

    @pl.when(i >= 1)
    def _():
        scores(i - 1, 0)
        scores(i, 1)
        consume(i - 1, 0, 1)
        scores(0, 0)
        consume(i, 1, 0)

    @pl.when(i == 0)
    def _():
        scores(i, 1)
        consume(i, 1, 0)

    def far_tiles(j, count):
        for u in range(count):
            nxt = j + u + 1
            if u == count - 1:
                nxt = jnp.minimum(nxt, n_far - 1)
            scores(nxt, (u + 1) % 2)
            consume(j + u, u % 2, None)

    def far_quad(p, carry):
        far_tiles(4 * p, 4)
        return carry

    def far_pair(p, carry):
        far_tiles(4 * quads + 2 * p, 2)
        return carry

    quads = n_far // 4
    lax.fori_loop(0, quads, far_quad, 0)
    lax.fori_loop(0, (n_far - 4 * quads) // 2, far_pair, 0)

    @pl.when(lax.rem(n_far, 2) == 1)
    def _():
        consume(n_far - 1, 0, None)

    lam = lam_ref[0:1, 0:1]
    acc = acc_ref[0:DA_VDIM, :]
    l = acc_ref[DA_VDIM:DA_VDIM + 1, :]
    o = acc[:, 0:t] / l[:, 0:t] - lam * (acc[:, t:2 * t] / l[:, t:2 * t])
    ms = jnp.mean(o * o, axis=0, keepdims=True)
    on = o * lax.rsqrt(ms + EPS) * gain_ref[...].astype(F32) * (1.0 - LAM_INIT)
    o_ref[...] = on.T.astype(o_ref.dtype)


def _attn_prompt(kb, qt, vt, bias, lam, gain, t):
    n = kb.shape[0]
    nt = n // t
    return pl.pallas_call(
        functools.partial(_attn_kernel, t=t),
        grid=(DA_HEADS, nt),
        in_specs=[
            pl.BlockSpec((n, DA_VDIM), lambda h, i: (0, h)),
            pl.BlockSpec((DA_VDIM, t), lambda h, i: (h, i)),
            pl.BlockSpec((None, nt, VT_ROWS, t), lambda h, i: (h, 0, 0, 0)),
            pl.BlockSpec((None, 2, t, t), lambda h, i: (h, 0, 0, 0)),
            pl.BlockSpec((8, 128), lambda h, i: (0, 0)),
            pl.BlockSpec((DA_VDIM, 1), lambda h, i: (0, 0)),
        ],
        out_specs=pl.BlockSpec((t, DA_VDIM), lambda h, i: (i, h)),
        out_shape=jax.ShapeDtypeStruct((n, DA_WIDTH), BF16),
        scratch_shapes=[
            pltpu.VMEM((DA_VDIM, 2 * t), BF16),
            pltpu.VMEM((1, 2 * t), F32),
            pltpu.VMEM((VT_ROWS, 2 * t), F32),
            pltpu.VMEM((2, t, 2 * t), F32),
            pltpu.VMEM((2, 1, 2 * t), F32),
        ],
        compiler_params=_params(("parallel", "parallel")),
        name="diff_attn_prompt",
    )(kb, qt, vt, bias, lam, gain.reshape(DA_VDIM, 1))


def _attn_step_kernel(q_ref, kp_ref, vp_ref, kn_ref, vn_ref, bp_ref, bn_ref, lam_ref, gain_ref,
                      o_ref, *, tq, pad):
    lam = lam_ref[0:1, 0:1]
    gain = gain_ref[...].astype(F32)
    zpad = jnp.zeros((pad - tq, DA_VDIM), BF16)
    for h in range(DA_HEADS):
        hs = slice(h * DA_VDIM, (h + 1) * DA_VDIM)
        q = q_ref[:, hs]
        lane = lax.broadcasted_iota(I32, q.shape, 1)
        zero = jnp.zeros_like(q)
        qz = jnp.concatenate([jnp.where(lane < DA_QKDIM, q, zero),
                              jnp.where(lane >= DA_QKDIM, q, zero)], axis=0)
        kp = kp_ref[:, h, :].astype(BF16)
        vp = vp_ref[:, h, :].astype(BF16)
        kn = jnp.concatenate([kn_ref[:, hs], zpad], axis=0)
        vn = jnp.concatenate([vn_ref[:, hs], zpad], axis=0)
        bp = bp_ref[h, 0]
        bn = bn_ref[h, 0]
        sp = _dot_nt(qz, kp) + jnp.concatenate([bp, bp], axis=0)
        sn = _dot_nt(qz, kn) + jnp.concatenate([bn, bn], axis=0)
        m = jnp.maximum(jnp.max(sp, axis=-1, keepdims=True), jnp.max(sn, axis=-1, keepdims=True))
        pp = jnp.exp2(sp - m)
        pn = jnp.exp2(sn - m)
        l = jnp.sum(pp, axis=-1, keepdims=True) + jnp.sum(pn, axis=-1, keepdims=True)
        acc = _dot(pp.astype(BF16), vp) + _dot(pn.astype(BF16), vn)
        on = acc / l
        o = on[0:tq] - lam * on[tq:2 * tq]
        ms = jnp.mean(o * o, axis=-1, keepdims=True)
        o = o * lax.rsqrt(ms + EPS) * gain * (1.0 - LAM_INIT)
        o_ref[:, hs] = o.astype(o_ref.dtype)


def _attn_step(qs, cache_k_l, cache_v_l, kb, vb, bias_p, bias_n, lam, gain, batch, tq):
    past = cache_k_l.shape[1]
    pad = bias_n.shape[-1]
    cache_spec = pl.BlockSpec((None, past, DA_HEADS, DA_VDIM), lambda b: (b, 0, 0, 0))
    row = pl.BlockSpec((tq, DA_WIDTH), lambda b: (b, 0))
    return pl.pallas_call(
        functools.partial(_attn_step_kernel, tq=tq, pad=pad),
        grid=(batch,),
        in_specs=[
            row, cache_spec, cache_spec, row, row,
            pl.BlockSpec(bias_p.shape, lambda b: (0, 0, 0, 0)),
            pl.BlockSpec(bias_n.shape, lambda b: (0, 0, 0, 0)),
            pl.BlockSpec((8, 128), lambda b: (0, 0)),
            pl.BlockSpec((1, DA_VDIM), lambda b: (0, 0)),
        ],
        out_specs=row,
        out_shape=jax.ShapeDtypeStruct((batch * tq, DA_WIDTH), BF16),
        compiler_params=_params(("parallel",)),
        name="diff_attn_step",
    )(qs, cache_k_l, cache_v_l, kb, vb, bias_p, bias_n, lam, gain.reshape(1, DA_VDIM))


def _post_kernel(xa_ref, ohga_ref, odaa_ref, xb_ref, ohgb_ref, odab_ref,
                 wout_ref, ga1_ref, g_ref, sc_ref, sh_ref, ga2_ref,
                 wsgu_ref, wsd_ref, wrt_ref, rb_ref, tri_ref, ltri_ref,
                 xs_ref, h2_ref, pos_ref, wl_ref, chunk_ref, nch_ref, cnt_ref, carry_ref, *, tm, nta):
    i = pl.program_id(0)

    @pl.when(i == 0)
    def _():
        carry_ref[...] = jnp.zeros(carry_ref.shape, F32)

    second = i >= nta
    x = jnp.where(second, xb_ref[...], xa_ref[...])
    ohg = jnp.where(second, ohgb_ref[...], ohga_ref[...])
    oda = jnp.where(second, odab_ref[...], odaa_ref[...])
    mix = _dot(ohg, wout_ref[0:HG_WIDTH, :]) + _dot(oda, wout_ref[HG_WIDTH:, :])
    x1 = x + ga1_ref[...] * mix
    ms = jnp.mean(x1 * x1, axis=-1, keepdims=True)
    h2 = x1 * lax.rsqrt(ms + EPS) * g_ref[...]
    h2 = h2 * (1.0 + sc_ref[...]) + sh_ref[...]
    h2b = h2.astype(BF16)
    h2_ref[...] = h2b
    gu = _dot(h2b, wsgu_ref[...])
    act = (_silu(gu[:, 0:D_EXPERT]) * gu[:, D_EXPERT:]).astype(BF16)
    xs_ref[...] = x1 + ga2_ref[...] * _dot(act, wsd_ref[...])

    logits = _dot_nt(wrt_ref[...], h2, precision=HIGHEST)
    score = _sigmoid(logits)
    sel = score + rb_ref[...]
    sub = lax.broadcasted_iota(I32, (GROUP_SIZE, tm), 0)
    gscore = []
    for g in range(N_GROUPS):
        v = sel[g * GROUP_SIZE:(g + 1) * GROUP_SIZE, :]
        m1 = jnp.max(v, axis=0, keepdims=True)
        i1 = jnp.min(jnp.where(v == m1, sub, GROUP_SIZE), axis=0, keepdims=True)
        m2 = jnp.max(jnp.where(sub == i1, -jnp.inf, v), axis=0, keepdims=True)
        gscore.append(m1 + m2)
    gsel = []
    for g in range(N_GROUPS):
        ahead = jnp.zeros((1, tm), F32)
        for g2 in range(N_GROUPS):
            if g2 == g:
                continue
            tie = 1.0 if g2 < g else 0.0
            ahead = ahead + jnp.where(gscore[g2] > gscore[g], 1.0,
                                      jnp.where(gscore[g2] == gscore[g], tie, 0.0))
        gsel.append(ahead < TOP_GROUPS)
    selm = jnp.concatenate(
        [jnp.where(gsel[g], sel[g * GROUP_SIZE:(g + 1) * GROUP_SIZE, :], -jnp.inf)
         for g in range(N_GROUPS)], axis=0)
    eio = lax.broadcasted_iota(I32, (N_EXPERTS, tm), 0)
    ahead = jnp.zeros((N_EXPERTS, tm), F32)
    for e2 in range(N_EXPERTS):
        row = selm[e2:e2 + 1, :]
        tie = jnp.where(eio > e2, 1.0, 0.0)
        ahead = ahead + jnp.where(row > selm, 1.0, jnp.where(row == selm, tie, 0.0))
    chosen = jnp.where(selm > -jnp.inf, jnp.where(ahead < TOP_K, 1.0, 0.0), 0.0)
    w = chosen * score
    wn = w / jnp.sum(w, axis=0, keepdims=True) * ROUTE_SCALE

    chb = chosen.astype(BF16)
    before = _dot(chb, tri_ref[...])
    tot = _dot(chb, jnp.ones((tm, 128), BF16))
    run = jnp.floor((tot + (ROW_GROUP - 1)) * (1.0 / ROW_GROUP)) * ROW_GROUP
    tile_base = _dot(ltri_ref[...], run.astype(BF16))
    carry = carry_ref[...]
    carry_ref[...] = carry + run
    cnt_ref[...] = carry + run
    pos = jnp.concatenate([tile_base] * (tm // 128), axis=1) + before

    widen = lambda v: jnp.concatenate([v] * (CHUNK_SLOTS // 128), axis=1)
    crow = lax.broadcasted_iota(I32, (N_EXPERTS, CHUNK_SLOTS), 1).astype(F32) * ROW_GROUP
    erow = lax.broadcasted_iota(I32, (N_EXPERTS, CHUNK_SLOTS), 0).astype(F32)
    owner = jnp.sum(jnp.where(widen(tile_base + run) <= crow, 1.0, 0.0), axis=0, keepdims=True)
    region_row = jnp.sum(jnp.where(owner == erow, widen(carry - tile_base), 0.0),
                         axis=0, keepdims=True) + crow[0:1]
    region_slab = (region_row * (1.0 / ROW_GROUP)).astype(I32)
    chunk_ref[...] = owner.astype(I32) * (1 << CHUNK_EXPERT_SHIFT) + region_slab
    nch_ref[...] = jnp.sum(run * (1.0 / ROW_GROUP), axis=0, keepdims=True).astype(I32)

    for r in range(TOP_K):
        pick = jnp.where(ahead == r, chosen, 0.0)
        pos_ref[r:r + 1, :] = jnp.sum(pick * pos, axis=0, keepdims=True).astype(I32)
        wl_ref[r:r + 1, :] = jnp.sum(pick * wn, axis=0, keepdims=True)


def _post(src_a, src_b, w_out_b, ga1, g, sc, sh, ga2, wsgu_b, wsd_b, wr_t, rb, tm):
    (xa, ohga, odaa), (xb, ohgb, odab) = src_a, src_b
    d = xa.shape[1]
    nta, ntb = xa.shape[0] // tm, xb.shape[0] // tm
    nt = nta + ntb
    n = nt * tm
    tri = jnp.asarray(np.triu(np.ones((tm, tm), np.float32), k=1), dtype=BF16)
    ltri = jnp.asarray(np.tril(np.ones((N_EXPERTS, N_EXPERTS), np.float32), k=-1), dtype=BF16)
    row = lambda i: (i, 0)
    row_a = lambda i: (jnp.minimum(i, nta - 1), 0)
    row_b = lambda i: (jnp.maximum(i - nta, 0), 0)
    col = lambda i: (0, i)
    full = lambda i: (0, 0)
    mod = pl.BlockSpec((None, tm, d), lambda i: (jnp.minimum(i // nta, 1), 0, 0))
    return pl.pallas_call(
        functools.partial(_post_kernel, tm=tm, nta=nta),
        grid=(nt,),
        in_specs=[
            pl.BlockSpec((tm, d), row_a),
            pl.BlockSpec((tm, HG_WIDTH), row_a),
            pl.BlockSpec((tm, DA_WIDTH), row_a),
            pl.BlockSpec((tm, d), row_b),
            pl.BlockSpec((tm, HG_WIDTH), row_b),
            pl.BlockSpec((tm, DA_WIDTH), row_b),
            pl.BlockSpec(w_out_b.shape, full),
            mod,
            pl.BlockSpec((1, d), full),
            mod,
            mod,
            mod,
            pl.BlockSpec(wsgu_b.shape, full),
            pl.BlockSpec(wsd_b.shape, full),
            pl.BlockSpec(wr_t.shape, full),
            pl.BlockSpec((N_EXPERTS, 1), full),
            pl.BlockSpec((tm, tm), full),
            pl.BlockSpec((N_EXPERTS, N_EXPERTS), full),
        ],
        out_specs=[
            pl.BlockSpec((tm, d), row),
            pl.BlockSpec((tm, d), row),
            pl.BlockSpec((TOP_K, tm), col),
            pl.BlockSpec((TOP_K, tm), col),
            pl.BlockSpec((None, 1, CHUNK_SLOTS), lambda i: (i, 0, 0)),
            pl.BlockSpec((None, 1, 128), lambda i: (i, 0, 0)),
            pl.BlockSpec((N_EXPERTS, 128), full),
        ],
        out_shape=[
            jax.ShapeDtypeStruct((n, d), F32),
            jax.ShapeDtypeStruct((n, d), BF16),
            jax.ShapeDtypeStruct((TOP_K, n), I32),
            jax.ShapeDtypeStruct((TOP_K, n), F32),
            jax.ShapeDtypeStruct((nt, 1, CHUNK_SLOTS), I32),
            jax.ShapeDtypeStruct((nt, 1, 128), I32),
            jax.ShapeDtypeStruct((N_EXPERTS, 128), F32),
        ],
        scratch_shapes=[pltpu.VMEM((N_EXPERTS, 128), F32)],
        compiler_params=_params(("arbitrary",)),
        name="post_mix_router",
    )(xa, ohga, odaa, xb, ohgb, odab, w_out_b, ga1, g, sc, sh, ga2, wsgu_b, wsd_b, wr_t,
      rb.reshape(N_EXPERTS, 1), tri, ltri)


def _chunk_slab_kernel(pslab_ref, w_ref, o_ref):
    w = w_ref[...]
    expert = lax.shift_right_logical(w, CHUNK_EXPERT_SHIFT)
    base = jnp.zeros(w.shape, I32)
    for e in range(N_EXPERTS):
        base = jnp.where(expert == e, pslab_ref[e], base)
    o_ref[...] = base + (w & ((1 << CHUNK_EXPERT_SHIFT) - 1))


def _chunk_slabs(pslab, chunks):
    return pl.pallas_call(
        _chunk_slab_kernel,
        in_specs=[pl.BlockSpec(memory_space=pltpu.SMEM), pl.BlockSpec(memory_space=pltpu.VMEM)],
        out_specs=pl.BlockSpec(memory_space=pltpu.VMEM),
        out_shape=jax.ShapeDtypeStruct(chunks.shape, I32),
        name="moe_chunk_slabs",
    )(pslab, chunks)


def _start_chunks(tile, chunk_ref, nch_ref, make_copy):
    n = nch_ref[tile]

    def start(c):
        make_copy(c, chunk_ref[tile * CHUNK_SLOTS + c], 1).start()

    def group(g, carry):
        for u in range(CHUNK_UNROLL):
            start(g * CHUNK_UNROLL + u)
        return carry

    def single(c, carry):
        start(c)
        return carry

    groups = n // CHUNK_UNROLL
    lax.fori_loop(0, groups, group, 0)
    lax.fori_loop(groups * CHUNK_UNROLL, n, single, 0)


def _wait_chunks(tile, nch_ref, make_copy):
    n = nch_ref[tile]
    many = n // WAIT_CHUNKS

    def wait_many(j, carry):
        make_copy(0, 0, WAIT_CHUNKS).wait()
        return carry

    def wait_one(j, carry):
        make_copy(0, 0, 1).wait()
        return carry

    lax.fori_loop(0, many, wait_many, 0)
    lax.fori_loop(many * WAIT_CHUNKS, n, wait_one, 0)


def _dispatch_kernel(chunk_ref, nch_ref, pstart_ref, pend_ref, rend_ref, pos_ref, h2_ref, xs_hbm,
                     cbuf_ref, zero_ref, zsem, sem, *, tm):
    i = pl.program_id(0)
    nt = pl.num_programs(0)
    dh = cbuf_ref.shape[-1]

    def zero_piece(slab):
        return pltpu.make_async_copy(zero_ref, xs_hbm.at[pl.ds(slab, ZERO_SLABS)], zsem)

    def zero_fill(op):
        def pieces(start, stop):
            first = (start // ZERO_SLABS) * ZERO_SLABS

            def piece(k, carry):
                op(zero_piece(first + k * ZERO_SLABS))
                return carry

            lax.fori_loop(0, (stop - first) // ZERO_SLABS, piece, 0)

        def region(e, carry):
            pieces(rend_ref[e], pend_ref[e])
            return carry

        lax.fori_loop(0, N_EXPERTS, region, 0)
        pieces(pend_ref[N_EXPERTS - 1], xs_hbm.shape[0])

    @pl.when(i == 0)
    def _():
        zero_ref[...] = jnp.zeros(zero_ref.shape, zero_ref.dtype)
        zero_fill(lambda c: c.start())
        zero_fill(lambda c: c.wait())

    pos = pos_ref[...]
    piota = lax.broadcasted_iota(I32, (GROUPED_ROWS, tm), 0).astype(jnp.int16)
    pos16 = pos.astype(jnp.int16)
    one = jnp.ones((GROUPED_ROWS, tm), BF16)
    perm = jnp.zeros((GROUPED_ROWS, tm), BF16)
    for r in range(TOP_K):
        perm = jnp.where(piota == pos16[r:r + 1, :], one, perm)
    cur = lax.rem(i, 2)
    grouped = _pack_exact_bf16_pair(_dot(perm, h2_ref[:, 0:dh]), _dot(perm, h2_ref[:, dh:]))
    cbuf_ref[cur] = grouped.reshape(cbuf_ref.shape[1:])

    def make_copy(buf):
        def build(tile_slab, buffer_slab, slabs):
            return pltpu.make_async_copy(cbuf_ref.at[buf, pl.ds(tile_slab, slabs)],
                                         xs_hbm.at[pl.ds(buffer_slab, slabs)], sem.at[buf])
        return build

    _start_chunks(i, chunk_ref, nch_ref, make_copy(cur))

    @pl.when(i > 0)
    def _():
        _wait_chunks(i - 1, nch_ref, make_copy(1 - cur))

    @pl.when(i == nt - 1)
    def _():
        _wait_chunks(i, nch_ref, make_copy(cur))


def _dispatch(chunks, nch, pstart, pend, rend, pos, h2, nrows, tm):
    n, d = h2.shape
    grid_spec = pltpu.PrefetchScalarGridSpec(
        num_scalar_prefetch=5,
        grid=(n // tm,),
        in_specs=[
            pl.BlockSpec((TOP_K, tm), lambda i, *_: (0, i)),
            pl.BlockSpec((tm, d), lambda i, *_: (i, 0)),
        ],
        out_specs=pl.BlockSpec(memory_space=pl.ANY),
        scratch_shapes=[
            pltpu.VMEM((2, GROUPED_ROWS // ROW_GROUP, ROW_GROUP, d // 2), U32),
            pltpu.VMEM((ZERO_SLABS, ROW_GROUP, d // 2), U32),
            pltpu.SemaphoreType.DMA(()),
            pltpu.SemaphoreType.DMA((2,)),
        ],
    )
    return pl.pallas_call(
        functools.partial(_dispatch_kernel, tm=tm),
        grid_spec=grid_spec,
        out_shape=jax.ShapeDtypeStruct((nrows // ROW_GROUP, ROW_GROUP, d // 2), U32),
        compiler_params=_params(("arbitrary",)),
        name="moe_dispatch",
    )(chunks, nch, pstart, pend, rend, pos, h2)


def _experts_kernel(be_ref, nu_ref, valid_ref, first_ref, slot_ref, next_ref, x_ref, wgu_hbm, wd_hbm,
                    o_ref, wgu_f_ref, wd_f_ref, wgu_b_ref, wd_b_ref, sem, *, bm):
    i = pl.program_id(0)
    valid = valid_ref[i]
    sub = bm // 2

    def weight_copies(expert, slot):
        return (pltpu.make_async_copy(wgu_hbm.at[expert], wgu_f_ref.at[slot], sem.at[slot, 0]),
                pltpu.make_async_copy(wd_hbm.at[expert], wd_f_ref.at[slot], sem.at[slot, 1]))

    @pl.when(i == 0)
    def _():
        for c in weight_copies(be_ref[0], 0):
            c.start()

    @pl.when(first_ref[i] == 1)
    def _():
        slot = slot_ref[i]
        for c in weight_copies(be_ref[i], slot):
            c.wait()

        @pl.when(next_ref[i] >= 0)
        def _():
            for c in weight_copies(next_ref[i], 1 - slot):
                c.start()

        wgu_b_ref[...] = wgu_f_ref[slot].astype(BF16)
        wd_b_ref[...] = wd_f_ref[slot].astype(BF16)

    for r0 in (0, sub):
        rows = slice(r0, r0 + sub)

        @pl.when(valid > r0)
        def _():
            lo, hi = _unpack_bf16_pair(x_ref[rows, :])
            x = jnp.concatenate([lo.astype(BF16), hi.astype(BF16)], axis=1)
            gu = _dot(x, wgu_b_ref[...])
            act = (_silu(gu[:, 0:D_EXPERT]) * gu[:, D_EXPERT:]).astype(BF16)
            y = _dot(act, wd_b_ref[...])
            o_ref[rows, :] = _pack_bf16_pair(y[:, 0:D_MODEL // 2], y[:, D_MODEL // 2:])

        @pl.when(valid <= r0)
        def _():
            o_ref[rows, :] = jnp.zeros((sub, o_ref.shape[1]), o_ref.dtype)


def _experts(block_e, nused, valid, first, slot, next_e, xs, w_gate_up_l, w_down_l, bm):
    nrows, dh = xs.shape
    d = 2 * dh
    nblk = nrows // bm
    grid_spec = pltpu.PrefetchScalarGridSpec(
        num_scalar_prefetch=6,
        grid=(nblk,),
        in_specs=[
            pl.BlockSpec((bm, dh), lambda i, be, nu, *_: (jnp.minimum(i, nu[0] - 1), 0)),
            pl.BlockSpec(memory_space=pl.ANY),
            pl.BlockSpec(memory_space=pl.ANY),
        ],
        out_specs=pl.BlockSpec((bm, dh), lambda i, *_: (i, 0)),
        scratch_shapes=[
            pltpu.VMEM((2, d, 2 * D_EXPERT), w_gate_up_l.dtype),
            pltpu.VMEM((2, D_EXPERT, d), w_down_l.dtype),
            pltpu.VMEM((d, 2 * D_EXPERT), BF16),
            pltpu.VMEM((D_EXPERT, d), BF16),
            pltpu.SemaphoreType.DMA((2, 2)),
        ],
    )
    return pl.pallas_call(
        functools.partial(_experts_kernel, bm=bm),
        grid_spec=grid_spec,
        out_shape=jax.ShapeDtypeStruct((nrows, dh), U32),
        compiler_params=_params(("arbitrary",)),
        name="moe_experts",
    )(block_e, nused, valid, first, slot, next_e, xs, w_gate_up_l, w_down_l)


def _combine_kernel(chunk_ref, nch_ref, pstart_ref, pos_ref, wl_ref, xs_ref, ga2_ref, gf_ref,
                    yb_hbm, oa_ref, ob_ref, gbuf_ref, sem, *, tm, nta):
    i = pl.program_id(0)
    nt = pl.num_programs(0)
    cur = lax.rem(i, 2)
    refs = (chunk_ref, nch_ref)

    def make_copy(buf):
        def build(tile_slab, buffer_slab, slabs):
            return pltpu.make_async_copy(yb_hbm.at[pl.ds(buffer_slab, slabs)],
                                         gbuf_ref.at[buf, pl.ds(tile_slab, slabs)], sem.at[buf])
        return build

    @pl.when(i == 0)
    def _():
        gbuf_ref[...] = jnp.zeros(gbuf_ref.shape, gbuf_ref.dtype)
        _start_chunks(0, *refs, make_copy(0))

    @pl.when(i + 1 < nt)
    def _():
        _start_chunks(i + 1, *refs, make_copy(1 - cur))

    _wait_chunks(i, nch_ref, make_copy(cur))

    lo, hi = _unpack_bf16_pair(gbuf_ref[cur].reshape(GROUPED_ROWS, gbuf_ref.shape[-1]))
    g = jnp.concatenate([lo.astype(BF16), hi.astype(BF16)], axis=1)
    pos16 = pos_ref[...].astype(jnp.int16)
    wl = wl_ref[...].astype(BF16)
    liota = lax.broadcasted_iota(I32, (tm, GROUPED_ROWS), 1).astype(jnp.int16)
    a = jnp.zeros((tm, GROUPED_ROWS), BF16)
    for r in range(TOP_K):
        a = jnp.where(liota == pos16[:, r:r + 1], jnp.broadcast_to(wl[:, r:r + 1], a.shape), a)
    routed = _dot(a, g)
    x2 = xs_ref[...] + ga2_ref[...] * routed
    ms = jnp.mean(x2 * x2, axis=-1, keepdims=True)
    y = x2 * lax.rsqrt(ms + EPS) * gf_ref[...]

    @pl.when(i < nta)
    def _():
        oa_ref[...] = y

    @pl.when(i >= nta)
    def _():
        ob_ref[...] = y


def _combine(chunks, nch, pstart, pos_t, wl_t, xs_base, ga2, gfin, yb, tm, nta):
    n, d = xs_base.shape
    ntb = n // tm - nta
    ga2_spec = pl.BlockSpec((None, tm, d), lambda i, *_: (jnp.minimum(i // nta, 1), 0, 0))
    grid_spec = pltpu.PrefetchScalarGridSpec(
        num_scalar_prefetch=3,
        grid=(n // tm,),
        in_specs=[
            pl.BlockSpec((tm, TOP_K), lambda i, *_: (i, 0)),
            pl.BlockSpec((tm, TOP_K), lambda i, *_: (i, 0)),
            pl.BlockSpec((tm, d), lambda i, *_: (i, 0)),
            ga2_spec,
            pl.BlockSpec((1, d), lambda i, *_: (0, 0)),
            pl.BlockSpec(memory_space=pl.ANY),
        ],
        out_specs=[
            pl.BlockSpec((tm, d), lambda i, *_: (jnp.minimum(i, nta - 1), 0)),
            pl.BlockSpec((tm, d), lambda i, *_: (jnp.maximum(i - nta, 0), 0)),
        ],
        scratch_shapes=[
            pltpu.VMEM((2, GROUPED_ROWS // ROW_GROUP, ROW_GROUP, d // 2), U32),
            pltpu.SemaphoreType.DMA((2,)),
        ],
    )
    return pl.pallas_call(
        functools.partial(_combine_kernel, tm=tm, nta=nta),
        grid_spec=grid_spec,
        out_shape=[jax.ShapeDtypeStruct((nta * tm, d), F32), jax.ShapeDtypeStruct((ntb * tm, d), F32)],
        compiler_params=_params(("arbitrary",)),
        name="moe_combine",
    )(chunks, nch, pstart, pos_t, wl_t, xs_base, ga2, gfin, yb)


def _moe_and_final(src_a, src_b, mods, wts, tm, bm):
    nta = src_a[0].shape[0] // tm
    ga1, sh2, sc2, ga2 = mods
    (w_out_b, g_ffn, wsgu_b, wsd_b, wr_t, rb, w_gate_up_l, w_down_l, g_final) = wts
    xs_base, h2, pos, wl, chunks, nch, cnt = _post(
        src_a, src_b, w_out_b, ga1, g_ffn, sc2, sh2, ga2, wsgu_b, wsd_b, wr_t, rb, tm)
    nt = xs_base.shape[0] // tm
    counts = cnt[:, 0].astype(I32)
    padded = (counts + bm - 1) // bm * bm
    pend = jnp.cumsum(padded)
    pstart = pend - padded
    max_rows = nt * (tm * TOP_K + N_EXPERTS * (ROW_GROUP - 1))
    nblk = -(-max_rows // bm) + N_EXPERTS
    nused = (pend[-1] // bm).astype(I32)
    blk_row = jnp.minimum(jnp.arange(nblk, dtype=I32), nused - 1) * bm
    be = jnp.sum((pend[None, :] <= blk_row[:, None]).astype(I32), axis=1)
    nch = nch[:, 0, 0]
    pslab = pstart // ROW_GROUP
    chunks = _chunk_slabs(pslab, chunks[:, 0, :]).reshape(-1)
    xs = _dispatch(chunks, nch, pslab, pend // ROW_GROUP, (pstart + counts) // ROW_GROUP, pos, h2,
                   nblk * bm, tm)
    xs = xs.reshape(nblk * bm, xs.shape[-1])
    region_end = jnp.sum(jnp.where(be[:, None] == jnp.arange(N_EXPERTS, dtype=I32)[None, :],
                                   (pstart + counts)[None, :], 0), axis=1)
    valid = jnp.clip(region_end - jnp.arange(nblk, dtype=I32) * bm, 0, bm)
    blk = jnp.arange(nblk, dtype=I32)
    first = ((blk < nused) & ((blk == 0) | (be != jnp.roll(be, 1)))).astype(I32)
    slot = (jnp.cumsum(first) - 1) % 2
    eids = jnp.arange(N_EXPERTS, dtype=I32)
    later_nonempty = (eids[None, :] > eids[:, None]) & (counts[None, :] > 0)
    next_tab = jnp.min(jnp.where(later_nonempty, eids[None, :], N_EXPERTS), axis=1)
    next_tab = jnp.where(next_tab == N_EXPERTS, -1, next_tab)
    next_e = jnp.sum(jnp.where(be[:, None] == eids[None, :], next_tab[None, :], 0), axis=1)
    yb = _experts(be, nused.reshape(1), valid, first, slot.astype(I32), next_e.astype(I32), xs,
                  w_gate_up_l, w_down_l, bm)
    yb = yb.reshape(nblk * bm // ROW_GROUP, ROW_GROUP, yb.shape[-1])
    return _combine(chunks, nch, pslab, pos.T, wl.T, xs_base, ga2, g_final, yb, tm, nta)


def _expand(mod, reps):
    if mod.shape[0] == 1:
        return mod
    return jnp.repeat(mod, reps, axis=0)


def kernel(x_prompt, x_sample, cache_k, cache_v, state_hgrn, c_prompt, c_sample, w_ada, b_ada,
           norm_mix, norm_ffn, norm_final, w_in, w_out, hg_lb_logits, hg_norm, da_lambda, da_norm,
           rel_bias_table, w_router, router_bias, w_gate_up, w_down, ws_gate_up, ws_down):
    depth = w_in.shape[0]
    assert depth == 1 and hg_lb_logits.shape[0] == 2
    bp, tp, d = x_prompt.shape
    bs, ts, _ = x_sample.shape
    assert bp == 1
    past = cache_k.shape[2]
    l = 0

    rows = -(-(bp + bs) // 8) * 8
    c_all = jnp.zeros((rows, d), F32).at[:bp].set(c_prompt).at[bp:bp + bs].set(c_sample)
    mod = _adaln(c_all, w_ada[l], b_ada[l])
    mod_p = [mod[0:bp, j * d:(j + 1) * d] for j in range(6)]
    mod_s = [_expand(mod[bp:bp + bs, j * d:(j + 1) * d], ts) for j in range(6)]

    w_in_b = w_in[l].astype(BF16)
    w_out_b = w_out[l].astype(BF16)
    wsgu_b = ws_gate_up[l].astype(BF16)
    wsd_b = ws_down[l].astype(BF16)
    wr_t = w_router[l].T
    g_mix = norm_mix[l].reshape(1, d)
    g_ffn = norm_ffn[l].reshape(1, d)
    g_final = norm_final.reshape(1, d)
    moe_w = (w_out_b, g_ffn, wsgu_b, wsd_b, wr_t, router_bias[l], w_gate_up[l], w_down[l], g_final)

    lam = _lam(da_lambda[l])

    t_att = min(ATT_TILE, tp)
    kk = jnp.arange(t_att, dtype=I32)[:, None]
    qq = jnp.arange(t_att, dtype=I32)[None, :]
    idx_diag = jnp.where((kk // CHUNK) <= (qq // CHUNK), _rel_bucket(kk - qq), MASK_BUCKET)
    idx_prev = _rel_bucket(kk - qq - t_att)
    bias_p = _bias_tiles(rel_bias_table, jnp.stack([idx_diag, idx_prev]).astype(I32),
                         shift_bucket=N_BUCKETS // 2 - 1)
    pad = 128
    qpos = past + jnp.arange(ts, dtype=I32)[:, None]
    idx_sp = _rel_bucket(jnp.arange(past, dtype=I32)[None, :] - qpos)
    kn = jnp.arange(pad, dtype=I32)[None, :]
    idx_sn = jnp.where(kn < ts, _rel_bucket(past + kn - qpos), MASK_BUCKET)
    bias_sp = _bias_tiles(rel_bias_table, idx_sp[None].astype(I32), shift_bucket=None)
    bias_sn = _bias_tiles(rel_bias_table, idx_sn[None].astype(I32), shift_bucket=None)

    xp = x_prompt.reshape(bp * tp, d)
    sh1, sc1, ga1, sh2, sc2, ga2 = mod_p
    assert ATT_TILE == INPROJ_TILE
    zh, qt, kf, vf, kb, vt = _inproj(xp, g_mix, sc1, sh1, w_in_b, t_att, True)
    s_zero = jnp.zeros((bp, HG_HEADS, HG_DIM, HG_DIM), F32)
    ohg_p, sp_new = _hgrn(zh, s_zero, hg_lb_logits, hg_norm[l], bp, tp, min(HGRN_CHUNK, tp))
    oda_p = _attn_prompt(kb, qt, vt, bias_p, lam, da_norm[l], t_att)
    src_p = (xp, ohg_p, oda_p)
    mods_p = (ga1, sh2, sc2, ga2)
    k_prompt = kf.reshape(1, bp, tp, DA_HEADS, 2 * DA_QKDIM)
    v_prompt = vf.reshape(1, bp, tp, DA_HEADS, DA_VDIM)

    ns = bs * ts
    xs_ = x_sample.reshape(ns, d)
    sh1, sc1, ga1, sh2, sc2, ga2 = mod_s
    zh, qs, kf, vf, kb, vb = _inproj(xs_, g_mix, sc1, sh1, w_in_b, ns, False)
    ohg_s, ss_new = _hgrn(zh, state_hgrn[l], hg_lb_logits, hg_norm[l], bs, ts, ts)
    oda_s = _attn_step(qs, cache_k[l], cache_v[l], kb, vb, bias_sp, bias_sn, lam, da_norm[l], bs, ts)
    assert ns == POST_TILE
    mods = tuple(jnp.stack([jnp.broadcast_to(mp, (POST_TILE, d)), ms_])
                 for mp, ms_ in zip(mods_p, (ga1, sh2, sc2, ga2)))
    y_p, y_s = _moe_and_final(src_p, (xs_, ohg_s, oda_s), mods, moe_w, POST_TILE, MOE_BLOCK_ROWS)
    k_sample = kf.reshape(1, bs, ts, DA_HEADS, 2 * DA_QKDIM)
    v_sample = vf.reshape(1, bs, ts, DA_HEADS, DA_VDIM)

    return (y_p.reshape(bp, tp, d), y_s.reshape(bs, ts, d), k_prompt, v_prompt, sp_new[None],
            k_sample, v_sample, ss_new[None].astype(x_sample.dtype))
```

```python
import functools
import math

import numpy as np
import jax
import jax.numpy as jnp
from jax import lax
from jax.experimental import pallas as pl
from jax.experimental.pallas import tpu as pltpu

F32 = jnp.float32
BF16 = jnp.bfloat16
I32 = jnp.int32
U32 = jnp.uint32
HIGHEST = lax.Precision.HIGHEST

D_MODEL = 1024
CHUNK = 64
HG_HEADS = 4
HG_DIM = 128
HG_WIDTH = HG_HEADS * HG_DIM
DA_HEADS = 4
DA_VDIM = 128
DA_QKDIM = 64
DA_WIDTH = DA_HEADS * DA_VDIM
N_BUCKETS = 32
MAX_DIST = 128
N_EXPERTS = 64
TOP_K = 8
N_GROUPS = 8
GROUP_SIZE = N_EXPERTS // N_GROUPS
TOP_GROUPS = 4
D_EXPERT = 256
ROUTE_SCALE = 2.5
EPS = 1e-6
LAM_INIT = 0.8 - 0.6 * math.exp(-0.3 * 0)

LOG2E = math.log2(math.e)
HI_MASK = np.uint32(0xFFFF0000)
NEG_BIG = -1e30
MASK_BUCKET = N_BUCKETS
V7X_VMEM_LIMIT = 48 * 1024 * 1024

ATT_TILE = 512
VT_ROWS = DA_VDIM + 16
HGRN_CHUNK = 256
INPROJ_TILE = 512
POST_TILE = 256
MOE_BLOCK_ROWS = 1024
ROW_GROUP = 8
GROUPED_ROWS = -(-(POST_TILE * TOP_K + N_EXPERTS * (ROW_GROUP - 1)) // 256) * 256
CHUNK_SLOTS = -(-(GROUPED_ROWS // ROW_GROUP) // 128) * 128
CHUNK_EXPERT_SHIFT = 24
CHUNK_UNROLL = 8
WAIT_CHUNKS = 16
ZERO_SLABS = 16


def _sigmoid(x):
    return 1.0 / (1.0 + jnp.exp(-x))


def _silu(x):
    return x * _sigmoid(x)


def _dot(a, b, **kw):
    return jnp.dot(a, b, preferred_element_type=F32, **kw)


def _dot_nt(a, b, **kw):
    return lax.dot_general(a, b, (((1,), (1,)), ((), ())), preferred_element_type=F32, **kw)


def _dot_tn(a, b, **kw):
    return lax.dot_general(a, b, (((0,), (0,)), ((), ())), preferred_element_type=F32, **kw)


def _pack_bf16_pair(lo, hi):
    lo_bits = lax.bitcast_convert_type(lo.astype(BF16).astype(F32), U32)
    hi_bits = lax.bitcast_convert_type(hi.astype(BF16).astype(F32), U32)
    return (lo_bits >> 16) | (hi_bits & HI_MASK)


def _pack_exact_bf16_pair(lo, hi):
    return (lax.bitcast_convert_type(lo, U32) >> 16) | (lax.bitcast_convert_type(hi, U32) & HI_MASK)


def _unpack_bf16_pair(w):
    lo = lax.bitcast_convert_type(w << 16, F32)
    hi = lax.bitcast_convert_type(w & HI_MASK, F32)
    return lo, hi


def _params(sem, vmem=V7X_VMEM_LIMIT, flags=None):
    return pltpu.CompilerParams(dimension_semantics=sem, vmem_limit_bytes=vmem, flags=flags)


def _adaln_kernel(c_ref, w_ref, b_ref, o_ref):
    s = _silu(c_ref[...])
    o_ref[...] = _dot(s, w_ref[...], precision=HIGHEST) + b_ref[...]


def _adaln(c_all, w_ada, b_ada):
    rows, d = c_all.shape
    cols = w_ada.shape[1]
    blk = 1024
    return pl.pallas_call(
        _adaln_kernel,
        grid=(cols // blk,),
        in_specs=[
            pl.BlockSpec((rows, d), lambda j: (0, 0)),
            pl.BlockSpec((d, blk), lambda j: (0, j)),
            pl.BlockSpec((1, blk), lambda j: (0, j)),
        ],
        out_specs=pl.BlockSpec((rows, blk), lambda j: (0, j)),
        out_shape=jax.ShapeDtypeStruct((rows, cols), F32),
        compiler_params=_params(("parallel",)),
        name="adaln",
    )(c_all, w_ada, b_ada.reshape(1, cols))


def _lam_kernel(l_ref, o_ref):
    l = l_ref[...].astype(F32)
    a = jnp.sum(l[0:1] * l[1:2], axis=-1, keepdims=True)
    b = jnp.sum(l[2:3] * l[3:4], axis=-1, keepdims=True)
    lam = jnp.exp(a) - jnp.exp(b) + LAM_INIT
    o_ref[...] = jnp.broadcast_to(lam, o_ref.shape)


def _lam(da_lambda_l):
    return pl.pallas_call(
        _lam_kernel,
        out_shape=jax.ShapeDtypeStruct((8, 128), F32),
        name="lam",
    )(da_lambda_l)


def _rel_bucket(rel):
    nb = N_BUCKETS // 2
    max_exact = nb // 2
    side = jnp.where(rel > 0, nb, 0)
    n = jnp.abs(rel)
    large = max_exact + (jnp.log(jnp.maximum(n, 1).astype(F32) / max_exact)
                         / math.log(MAX_DIST / max_exact) * (nb - max_exact)).astype(I32)
    large = jnp.minimum(large, nb - 1)
    return side + jnp.where(n < max_exact, n, large)


def _bias_kernel(tab_ref, idx_ref, o_ref, *, shift_bucket):
    h = pl.program_id(0)
    idx = idx_ref[...]
    shift = tab_ref[shift_bucket, h] if shift_bucket is not None else 0.0
    acc = jnp.zeros(idx.shape, F32)
    for j in range(N_BUCKETS):
        acc = jnp.where(idx == j, (tab_ref[j, h] - shift) * LOG2E, acc)
    o_ref[...] = jnp.where(idx == MASK_BUCKET, NEG_BIG, acc)


def _bias_tiles(table, idx, *, shift_bucket):
    k, r, c = idx.shape
    return pl.pallas_call(
        functools.partial(_bias_kernel, shift_bucket=shift_bucket),
        grid=(DA_HEADS, k),
        in_specs=[
            pl.BlockSpec(memory_space=pltpu.SMEM),
            pl.BlockSpec((None, r, c), lambda h, d: (d, 0, 0)),
        ],
        out_specs=pl.BlockSpec((None, None, r, c), lambda h, d: (h, d, 0, 0)),
        out_shape=jax.ShapeDtypeStruct((DA_HEADS, k, r, c), F32),
        compiler_params=_params(("parallel", "parallel")),
        name="rel_bias",
    )(table, idx)


def _inproj_kernel(x_ref, g_ref, sc_ref, sh_ref, w_ref,
                   zh_ref, q_ref, k_ref, v_ref, kb_ref, vb_ref, *, transposed):
    x = x_ref[...]
    ms = jnp.mean(x * x, axis=-1, keepdims=True)
    h = x * lax.rsqrt(ms + EPS) * g_ref[...]
    h = h * (1.0 + sc_ref[...]) + sh_ref[...]
    hb = h.astype(BF16)
    c0 = 4 * HG_WIDTH
    zh_ref[...] = _dot(hb, w_ref[:, 0:c0])
    zq = _dot(hb, w_ref[:, c0:c0 + DA_WIDTH]) * (DA_QKDIM ** -0.5 * LOG2E)
    zk = _dot(hb, w_ref[:, c0 + DA_WIDTH:c0 + 2 * DA_WIDTH])
    for hd in range(DA_HEADS):
        k_ref[:, hd, :] = zk[:, hd * DA_VDIM:(hd + 1) * DA_VDIM]
    kb_ref[...] = zk.astype(BF16)
    zv = _dot(hb, w_ref[:, c0 + 2 * DA_WIDTH:c0 + 3 * DA_WIDTH])
    for hd in range(DA_HEADS):
        v_ref[:, hd, :] = zv[:, hd * DA_VDIM:(hd + 1) * DA_VDIM]
    if transposed:
        q_ref[...] = zq.T.astype(BF16)
        vb_ref[:, 0:DA_VDIM, :] = zv.T.astype(BF16).reshape(DA_HEADS, DA_VDIM, zv.shape[0])
        vb_ref[:, DA_VDIM:, :] = jnp.ones((DA_HEADS, VT_ROWS - DA_VDIM, zv.shape[0]), BF16)
    else:
        q_ref[...] = zq.astype(BF16)
        vb_ref[...] = zv.astype(BF16)


def _mod_spec(mod, tm):
    if mod.shape[0] == 1:
        return pl.BlockSpec((1, mod.shape[1]), lambda i: (0, 0))
    return pl.BlockSpec((tm, mod.shape[1]), lambda i: (i, 0))


def _inproj(x, g, sc, sh, w_in_b, tm, transposed):
    n, d = x.shape
    cols = w_in_b.shape[1]
    row = lambda i: (i, 0)
    if transposed:
        q_spec = pl.BlockSpec((DA_WIDTH, tm), lambda i: (0, i))
        q_shape = jax.ShapeDtypeStruct((DA_WIDTH, n), BF16)
        vb_spec = pl.BlockSpec((DA_HEADS, None, VT_ROWS, tm), lambda i: (0, i, 0, 0))
        vb_shape = jax.ShapeDtypeStruct((DA_HEADS, n // tm, VT_ROWS, tm), BF16)
    else:
        q_spec = vb_spec = pl.BlockSpec((tm, DA_WIDTH), row)
        q_shape = vb_shape = jax.ShapeDtypeStruct((n, DA_WIDTH), BF16)
    return pl.pallas_call(
        functools.partial(_inproj_kernel, transposed=transposed),
        grid=(n // tm,),
        in_specs=[
            pl.BlockSpec((tm, d), row),
            pl.BlockSpec((1, d), lambda i: (0, 0)),
            _mod_spec(sc, tm),
            _mod_spec(sh, tm),
            pl.BlockSpec((d, cols), lambda i: (0, 0)),
        ],
        out_specs=[
            pl.BlockSpec((tm, 4 * HG_WIDTH), row),
            q_spec,
            pl.BlockSpec((tm, DA_HEADS, DA_VDIM), lambda i: (i, 0, 0)),
            pl.BlockSpec((tm, DA_HEADS, DA_VDIM), lambda i: (i, 0, 0)),
            pl.BlockSpec((tm, DA_WIDTH), row),
            vb_spec,
        ],
        out_shape=[
            jax.ShapeDtypeStruct((n, 4 * HG_WIDTH), F32),
            q_shape,
            jax.ShapeDtypeStruct((n, DA_HEADS, DA_VDIM), F32),
            jax.ShapeDtypeStruct((n, DA_HEADS, DA_VDIM), F32),
            jax.ShapeDtypeStruct((n, DA_WIDTH), BF16),
            vb_shape,
        ],
        compiler_params=_params(("parallel",)),
        name="inproj",
    )(x, g, sc, sh, w_in_b)


def _hgrn_consts(c):
    levels = int(round(math.log2(c)))
    assert 1 << levels == c and levels >= 3
    t = np.arange(c)[:, None]
    r = np.arange(c)[None, :]
    tri = (r <= t).astype(np.float32)
    x = np.maximum(t ^ r, 1)
    lv = np.where(t == r, -1, np.where(t > r, np.floor(np.log2(x)).astype(np.int64), -2))
    return jnp.asarray(tri, dtype=BF16), jnp.asarray(lv, dtype=I32), levels


def _hgrn_kernel(zh_ref, s0_ref, lbl_ref, gain_ref, mall_ref, lv_ref,
                 o_ref, sout_ref, st_ref, b_ref, *, c, levels):
    ci = pl.program_id(1)

    @pl.when(ci == 0)
    def _():
        for h in range(HG_HEADS):
            st_ref[h] = s0_ref[h].astype(F32).T

    lbl = lbl_ref[...].astype(F32)
    mx = jnp.maximum(lbl[0:1], lbl[1:2])
    e0 = jnp.exp(lbl[0:1] - mx)
    e1 = jnp.exp(lbl[1:2] - mx)
    lb = e0 / (e0 + e1)

    xq = zh_ref[:, 0:HG_WIDTH]
    xf = zh_ref[:, HG_WIDTH:2 * HG_WIDTH]
    q = _silu(xq)
    y = lb + (1.0 - lb) * _sigmoid(xf)
    logf = jnp.log(y)
    kk = 1.0 - y

    l1 = logf.astype(BF16)
    r1 = logf - l1.astype(F32)
    l2 = r1.astype(BF16)
    l3 = (r1 - l2.astype(F32)).astype(BF16)
    tri = mall_ref[...]
    b = _dot(tri, l1) + _dot(tri, l2) + _dot(tri, l3)
    b_ref[...] = b
    trow = lax.broadcasted_iota(I32, (c, HG_DIM), 0)

    def level_factor(l, sl):
        m = 1 << l
        later = (trow & m) != 0
        lf = logf[:, sl]
        if l == 0:
            e = jnp.where(later, lf, 0.0)
        elif l == 1:
            below = pltpu.roll(lf, 1, 0)
            above = pltpu.roll(lf, c - 1, 0)
            low = (trow & 1) != 0
            e = jnp.where(later, jnp.where(low, lf + below, lf), jnp.where(low, 0.0, above))
        else:
            mid = jnp.concatenate(
                [jnp.broadcast_to(b_ref[k * 2 * m + m - 1:k * 2 * m + m, sl], (2 * m, HG_DIM))
                 for k in range(c // (2 * m))], axis=0)
            e = jnp.where(later, b[:, sl] - mid, mid - b[:, sl])
        return jnp.exp(e)

    lv = lv_ref[...].astype(jnp.int16)
    gain = gain_ref[...].astype(F32)
    for h in range(HG_HEADS):
        sl = slice(h * HG_DIM, (h + 1) * HG_DIM)
        qh = q[:, sl]
        kh = kk[:, sl]
        ih = zh_ref[:, 2 * HG_WIDTH + h * HG_DIM:2 * HG_WIDTH + (h + 1) * HG_DIM]
        gh = zh_ref[:, 3 * HG_WIDTH + h * HG_DIM:3 * HG_WIDTH + (h + 1) * HG_DIM]
        bh = b[:, sl]
        ihb = ih.astype(BF16)
        a = jnp.where(lv == -1, _dot_nt(qh.astype(BF16), kh.astype(BF16)).astype(BF16),
                      jnp.zeros((c, c), BF16))
        for l in range(levels):
            f = level_factor(l, sl)
            p = _dot_nt((qh * f).astype(BF16), (kh * f).astype(BF16))
            a = jnp.where(lv == l, p.astype(BF16), a)
        st = st_ref[h]
        o = _dot(a, ihb) + _dot_nt((qh * jnp.exp(bh)).astype(BF16), st.astype(BF16))
        bl = bh[c - 1:c, :]
        kd = (kh * jnp.exp(bl - bh)).astype(BF16)
        st_ref[h] = st * jnp.exp(bl) + _dot_tn(ihb, kd)
        ms = jnp.mean(o * o, axis=-1, keepdims=True)
        on = o * lax.rsqrt(ms + EPS) * gain
        o_ref[:, sl] = (on * _silu(gh)).astype(o_ref.dtype)

    @pl.when(ci == pl.num_programs(1) - 1)
    def _():
        for h in range(HG_HEADS):
            sout_ref[h] = st_ref[h].T.astype(sout_ref.dtype)


def _hgrn(zh, s0, lb_logits, gain, batch, seq, c):
    mall, lv, levels = _hgrn_consts(c)
    nc = seq // c
    return pl.pallas_call(
        functools.partial(_hgrn_kernel, c=c, levels=levels),
        grid=(batch, nc),
        in_specs=[
            pl.BlockSpec((c, 4 * HG_WIDTH), lambda b, i: (b * nc + i, 0)),
            pl.BlockSpec((None, HG_HEADS, HG_DIM, HG_DIM), lambda b, i: (b, 0, 0, 0)),
            pl.BlockSpec(lb_logits.shape, lambda b, i: (0, 0)),
            pl.BlockSpec((1, HG_DIM), lambda b, i: (0, 0)),
            pl.BlockSpec(mall.shape, lambda b, i: (0, 0)),
            pl.BlockSpec(lv.shape, lambda b, i: (0, 0)),
        ],
        out_specs=[
            pl.BlockSpec((c, HG_WIDTH), lambda b, i: (b * nc + i, 0)),
            pl.BlockSpec((None, HG_HEADS, HG_DIM, HG_DIM), lambda b, i: (b, 0, 0, 0)),
        ],
        out_shape=[
            jax.ShapeDtypeStruct((batch * seq, HG_WIDTH), BF16),
            jax.ShapeDtypeStruct((batch, HG_HEADS, HG_DIM, HG_DIM), F32),
        ],
        scratch_shapes=[pltpu.VMEM((HG_HEADS, HG_DIM, HG_DIM), F32),
                        pltpu.VMEM((c, HG_WIDTH), F32)],
        compiler_params=_params(("parallel", "arbitrary")),
        name="hgrn2",
    )(zh, s0, lb_logits, gain.reshape(1, HG_DIM), mall, lv)


def _attn_kernel(k_ref, qt_ref, vt_ref, bias_ref, lam_ref, gain_ref,
                 o_ref, qz_ref, m_ref, acc_ref, s_ref, smax_ref, *, t):
    i = pl.program_id(1)
    qt = qt_ref[...]
    row = lax.broadcasted_iota(I32, qt.shape, 0)
    zero = jnp.zeros_like(qt)
    qz_ref[:, 0:t] = jnp.where(row < DA_QKDIM, qt, zero)
    qz_ref[:, t:2 * t] = jnp.where(row >= DA_QKDIM, qt, zero)
    m_ref[...] = jnp.full(m_ref.shape, NEG_BIG, F32)
    acc_ref[...] = jnp.zeros(acc_ref.shape, F32)

    def scores(j, buf):
        kt = k_ref[pl.ds(pl.multiple_of(j * t, t), t), :]
        s = _dot(kt, qz_ref[...])
        s_ref[buf] = s
        smax_ref[buf] = jnp.max(s, axis=0, keepdims=True)

    def consume(j, buf, bias_idx):
        s = s_ref[buf]
        if bias_idx is not None:
            b = bias_ref[bias_idx]
            s = jnp.concatenate([s[:, 0:t] + b, s[:, t:2 * t] + b], axis=1)
            s_max = jnp.max(s, axis=0, keepdims=True)
        else:
            s_max = smax_ref[buf]
        m_prev = m_ref[...]
        m_new = jnp.maximum(m_prev, s_max)
        alpha = jnp.exp2(m_prev - m_new)
        pr = jnp.exp2(s - m_new).astype(BF16)
        acc_ref[...] = alpha * acc_ref[...] + _dot(vt_ref[j], pr)
        m_ref[...] = m_new

    n_far = jnp.maximum(i - 1, 0)

    @pl.when(i >= 1)
    def _():
        scores(i - 1, 0)
        scores(i, 1)
        consume(i - 1, 0, 1)
        scores(0, 0)
        consume(i, 1, 0)

    @pl.when(i == 0)
    def _():
        scores(i, 1)
        consume(i, 1, 0)

    def far_tiles(j, count):
        for u in range(count):
            nxt = j + u + 1
            if u == count - 1:
                nxt = jnp.minimum(nxt, n_far - 1)
            scores(nxt, (u + 1) % 2)
            consume(j + u, u % 2, None)

    def far_quad(p, carry):
        far_tiles(4 * p, 4)
        return carry

    def far_pair(p, carry):
        far_tiles(4 * quads + 2 * p, 2)
        return carry

    quads = n_far // 4
    lax.fori_loop(0, quads, far_quad, 0)
    lax.fori_loop(0, (n_far - 4 * quads) // 2, far_pair, 0)

    @pl.when(lax.rem(n_far, 2) == 1)
    def _():
        consume(n_far - 1, 0, None)

    lam = lam_ref[0:1, 0:1]
    acc = acc_ref[0:DA_VDIM, :]
    l = acc_ref[DA_VDIM:DA_VDIM + 1, :]
    o = acc[:, 0:t] / l[:, 0:t] - lam * (acc[:, t:2 * t] / l[:, t:2 * t])
    ms = jnp.mean(o * o, axis=0, keepdims=True)
    on = o * lax.rsqrt(ms + EPS) * gain_ref[...].astype(F32) * (1.0 - LAM_INIT)
    o_ref[...] = on.T.astype(o_ref.dtype)


def _attn_prompt(kb, qt, vt, bias, lam, gain, t):
    n = kb.shape[0]
    nt = n // t
    return pl.pallas_call(
        functools.partial(_attn_kernel, t=t),
        grid=(DA_HEADS, nt),
        in_specs=[
            pl.BlockSpec((n, DA_VDIM), lambda h, i: (0, h)),
            pl.BlockSpec((DA_VDIM, t), lambda h, i: (h, i)),
            pl.BlockSpec((None, nt, VT_ROWS, t), lambda h, i: (h, 0, 0, 0)),
            pl.BlockSpec((None, 2, t, t), lambda h, i: (h, 0, 0, 0)),
            pl.BlockSpec((8, 128), lambda h, i: (0, 0)),
            pl.BlockSpec((DA_VDIM, 1), lambda h, i: (0, 0)),
        ],
        out_specs=pl.BlockSpec((t, DA_VDIM), lambda h, i: (i, h)),
        out_shape=jax.ShapeDtypeStruct((n, DA_WIDTH), BF16),
        scratch_shapes=[
            pltpu.VMEM((DA_VDIM, 2 * t), BF16),
            pltpu.VMEM((1, 2 * t), F32),
            pltpu.VMEM((VT_ROWS, 2 * t), F32),
            pltpu.VMEM((2, t, 2 * t), F32),
            pltpu.VMEM((2, 1, 2 * t), F32),
        ],
        compiler_params=_params(("parallel", "parallel")),
        name="diff_attn_prompt",
    )(kb, qt, vt, bias, lam, gain.reshape(DA_VDIM, 1))


def _attn_step_kernel(q_ref, kp_ref, vp_ref, kn_ref, vn_ref, bp_ref, bn_ref, lam_ref, gain_ref,
                      o_ref, *, tq, pad):
    lam = lam_ref[0:1, 0:1]
    gain = gain_ref[...].astype(F32)
    zpad = jnp.zeros((pad - tq, DA_VDIM), BF16)
    for h in range(DA_HEADS):
        hs = slice(h * DA_VDIM, (h + 1) * DA_VDIM)
        q = q_ref[:, hs]
        lane = lax.broadcasted_iota(I32, q.shape, 1)
        zero = jnp.zeros_like(q)
        qz = jnp.concatenate([jnp.where(lane < DA_QKDIM, q, zero),
                              jnp.where(lane >= DA_QKDIM, q, zero)], axis=0)
        kp = kp_ref[:, h, :].astype(BF16)
        vp = vp_ref[:, h, :].astype(BF16)
        kn = jnp.concatenate([kn_ref[:, hs], zpad], axis=0)
        vn = jnp.concatenate([vn_ref[:, hs], zpad], axis=0)
        bp = bp_ref[h, 0]
        bn = bn_ref[h, 0]
        sp = _dot_nt(qz, kp) + jnp.concatenate([bp, bp], axis=0)
        sn = _dot_nt(qz, kn) + jnp.concatenate([bn, bn], axis=0)
        m = jnp.maximum(jnp.max(sp, axis=-1, keepdims=True), jnp.max(sn, axis=-1, keepdims=True))
        pp = jnp.exp2(sp - m)
        pn = jnp.exp2(sn - m)
        l = jnp.sum(pp, axis=-1, keepdims=True) + jnp.sum(pn, axis=-1, keepdims=True)
        acc = _dot(pp.astype(BF16), vp) + _dot(pn.astype(BF16), vn)
        on = acc / l
        o = on[0:tq] - lam * on[tq:2 * tq]
        ms = jnp.mean(o * o, axis=-1, keepdims=True)
        o = o * lax.rsqrt(ms + EPS) * gain * (1.0 - LAM_INIT)
        o_ref[:, hs] = o.astype(o_ref.dtype)


def _attn_step(qs, cache_k_l, cache_v_l, kb, vb, bias_p, bias_n, lam, gain, batch, tq):
    past = cache_k_l.shape[1]
    pad = bias_n.shape[-1]
    cache_spec = pl.BlockSpec((None, past, DA_HEADS, DA_VDIM), lambda b: (b, 0, 0, 0))
    row = pl.BlockSpec((tq, DA_WIDTH), lambda b: (b, 0))
    return pl.pallas_call(
        functools.partial(_attn_step_kernel, tq=tq, pad=pad),
        grid=(batch,),
        in_specs=[
            row, cache_spec, cache_spec, row, row,
            pl.BlockSpec(bias_p.shape, lambda b: (0, 0, 0, 0)),
            pl.BlockSpec(bias_n.shape, lambda b: (0, 0, 0, 0)),
            pl.BlockSpec((8, 128), lambda b: (0, 0)),
            pl.BlockSpec((1, DA_VDIM), lambda b: (0, 0)),
        ],
        out_specs=row,
        out_shape=jax.ShapeDtypeStruct((batch * tq, DA_WIDTH), BF16),
        compiler_params=_params(("parallel",)),
        name="diff_attn_step",
    )(qs, cache_k_l, cache_v_l, kb, vb, bias_p, bias_n, lam, gain.reshape(1, DA_VDIM))


def _post_kernel(xa_ref, ohga_ref, odaa_ref, xb_ref, ohgb_ref, odab_ref,
                 wout_ref, ga1_ref, g_ref, sc_ref, sh_ref, ga2_ref,
                 wsgu_ref, wsd_ref, wrt_ref, rb_ref, tri_ref, ltri_ref,
                 xs_ref, h2_ref, pos_ref, wl_ref, chunk_ref, nch_ref, cnt_ref, carry_ref, *, tm, nta):
    i = pl.program_id(0)

    @pl.when(i == 0)
    def _():
        carry_ref[...] = jnp.zeros(carry_ref.shape, F32)

    second = i >= nta
    x = jnp.where(second, xb_ref[...], xa_ref[...])
    ohg = jnp.where(second, ohgb_ref[...], ohga_ref[...])
    oda = jnp.where(second, odab_ref[...], odaa_ref[...])
    mix = _dot(ohg, wout_ref[0:HG_WIDTH, :]) + _dot(oda, wout_ref[HG_WIDTH:, :])
    x1 = x + ga1_ref[...] * mix
    ms = jnp.mean(x1 * x1, axis=-1, keepdims=True)
    h2 = x1 * lax.rsqrt(ms + EPS) * g_ref[...]
    h2 = h2 * (1.0 + sc_ref[...]) + sh_ref[...]
    h2b = h2.astype(BF16)
    h2_ref[...] = h2b
    gu = _dot(h2b, wsgu_ref[...])
    act = (_silu(gu[:, 0:D_EXPERT]) * gu[:, D_EXPERT:]).astype(BF16)
    xs_ref[...] = x1 + ga2_ref[...] * _dot(act, wsd_ref[...])

    logits = _dot_nt(wrt_ref[...], h2, precision=HIGHEST)
    score = _sigmoid(logits)
    sel = score + rb_ref[...]
    sub = lax.broadcasted_iota(I32, (GROUP_SIZE, tm), 0)
    gscore = []
    for g in range(N_GROUPS):
        v = sel[g * GROUP_SIZE:(g + 1) * GROUP_SIZE, :]
        m1 = jnp.max(v, axis=0, keepdims=True)
        i1 = jnp.min(jnp.where(v == m1, sub, GROUP_SIZE), axis=0, keepdims=True)
        m2 = jnp.max(jnp.where(sub == i1, -jnp.inf, v), axis=0, keepdims=True)
        gscore.append(m1 + m2)
    gsel = []
    for g in range(N_GROUPS):
        ahead = jnp.zeros((1, tm), F32)
        for g2 in range(N_GROUPS):
            if g2 == g:
                continue
            tie = 1.0 if g2 < g else 0.0
            ahead = ahead + jnp.where(gscore[g2] > gscore[g], 1.0,
                                      jnp.where(gscore[g2] == gscore[g], tie, 0.0))
        gsel.append(ahead < TOP_GROUPS)
    selm = jnp.concatenate(
        [jnp.where(gsel[g], sel[g * GROUP_SIZE:(g + 1) * GROUP_SIZE, :], -jnp.inf)
         for g in range(N_GROUPS)], axis=0)
    eio = lax.broadcasted_iota(I32, (N_EXPERTS, tm), 0)
    ahead = jnp.zeros((N_EXPERTS, tm), F32)
    for e2 in range(N_EXPERTS):
        row = selm[e2:e2 + 1, :]
        tie = jnp.where(eio > e2, 1.0, 0.0)
        ahead = ahead + jnp.where(row > selm, 1.0, jnp.where(row == selm, tie, 0.0))
    chosen = jnp.where(selm > -jnp.inf, jnp.where(ahead < TOP_K, 1.0, 0.0), 0.0)
    w = chosen * score
    wn = w / jnp.sum(w, axis=0, keepdims=True) * ROUTE_SCALE

    chb = chosen.astype(BF16)
    before = _dot(chb, tri_ref[...])
    tot = _dot(chb, jnp.ones((tm, 128), BF16))
    run = jnp.floor((tot + (ROW_GROUP - 1)) * (1.0 / ROW_GROUP)) * ROW_GROUP
    tile_base = _dot(ltri_ref[...], run.astype(BF16))
    carry = carry_ref[...]
    carry_ref[...] = carry + run
    cnt_ref[...] = carry + run
    pos = jnp.concatenate([tile_base] * (tm // 128), axis=1) + before

    widen = lambda v: jnp.concatenate([v] * (CHUNK_SLOTS // 128), axis=1)
    crow = lax.broadcasted_iota(I32, (N_EXPERTS, CHUNK_SLOTS), 1).astype(F32) * ROW_GROUP
    erow = lax.broadcasted_iota(I32, (N_EXPERTS, CHUNK_SLOTS), 0).astype(F32)
    owner = jnp.sum(jnp.where(widen(tile_base + run) <= crow, 1.0, 0.0), axis=0, keepdims=True)
    region_row = jnp.sum(jnp.where(owner == erow, widen(carry - tile_base), 0.0),
                         axis=0, keepdims=True) + crow[0:1]
    region_slab = (region_row * (1.0 / ROW_GROUP)).astype(I32)
    chunk_ref[...] = owner.astype(I32) * (1 << CHUNK_EXPERT_SHIFT) + region_slab
    nch_ref[...] = jnp.sum(run * (1.0 / ROW_GROUP), axis=0, keepdims=True).astype(I32)

    for r in range(TOP_K):
        pick = jnp.where(ahead == r, chosen, 0.0)
        pos_ref[r:r + 1, :] = jnp.sum(pick * pos, axis=0, keepdims=True).astype(I32)
        wl_ref[r:r + 1, :] = jnp.sum(pick * wn, axis=0, keepdims=True)


def _post(src_a, src_b, w_out_b, ga1, g, sc, sh, ga2, wsgu_b, wsd_b, wr_t, rb, tm):
    (xa, ohga, odaa), (xb, ohgb, odab) = src_a, src_b
    d = xa.shape[1]
    nta, ntb = xa.shape[0] // tm, xb.shape[0] // tm
    nt = nta + ntb
    n = nt * tm
    tri = jnp.asarray(np.triu(np.ones((tm, tm), np.float32), k=1), dtype=BF16)
    ltri = jnp.asarray(np.tril(np.ones((N_EXPERTS, N_EXPERTS), np.float32), k=-1), dtype=BF16)
    row = lambda i: (i, 0)
    row_a = lambda i: (jnp.minimum(i, nta - 1), 0)
    row_b = lambda i: (jnp.maximum(i - nta, 0), 0)
    col = lambda i: (0, i)
    full = lambda i: (0, 0)
    mod = pl.BlockSpec((None, tm, d), lambda i: (jnp.minimum(i // nta, 1), 0, 0))
    return pl.pallas_call(
        functools.partial(_post_kernel, tm=tm, nta=nta),
        grid=(nt,),
        in_specs=[
            pl.BlockSpec((tm, d), row_a),
            pl.BlockSpec((tm, HG_WIDTH), row_a),
            pl.BlockSpec((tm, DA_WIDTH), row_a),
            pl.BlockSpec((tm, d), row_b),
            pl.BlockSpec((tm, HG_WIDTH), row_b),
            pl.BlockSpec((tm, DA_WIDTH), row_b),
            pl.BlockSpec(w_out_b.shape, full),
            mod,
            pl.BlockSpec((1, d), full),
            mod,
            mod,
            mod,
            pl.BlockSpec(wsgu_b.shape, full),
            pl.BlockSpec(wsd_b.shape, full),
            pl.BlockSpec(wr_t.shape, full),
            pl.BlockSpec((N_EXPERTS, 1), full),
            pl.BlockSpec((tm, tm), full),
            pl.BlockSpec((N_EXPERTS, N_EXPERTS), full),
        ],
        out_specs=[
            pl.BlockSpec((tm, d), row),
            pl.BlockSpec((tm, d), row),
            pl.BlockSpec((TOP_K, tm), col),
            pl.BlockSpec((TOP_K, tm), col),
            pl.BlockSpec((None, 1, CHUNK_SLOTS), lambda i: (i, 0, 0)),
            pl.BlockSpec((None, 1, 128), lambda i: (i, 0, 0)),
            pl.BlockSpec((N_EXPERTS, 128), full),
        ],
        out_shape=[
            jax.ShapeDtypeStruct((n, d), F32),
            jax.ShapeDtypeStruct((n, d), BF16),
            jax.ShapeDtypeStruct((TOP_K, n), I32),
            jax.ShapeDtypeStruct((TOP_K, n), F32),
            jax.ShapeDtypeStruct((nt, 1, CHUNK_SLOTS), I32),
            jax.ShapeDtypeStruct((nt, 1, 128), I32),
            jax.ShapeDtypeStruct((N_EXPERTS, 128), F32),
        ],
        scratch_shapes=[pltpu.VMEM((N_EXPERTS, 128), F32)],
        compiler_params=_params(("arbitrary",)),
        name="post_mix_router",
    )(xa, ohga, odaa, xb, ohgb, odab, w_out_b, ga1, g, sc, sh, ga2, wsgu_b, wsd_b, wr_t,
      rb.reshape(N_EXPERTS, 1), tri, ltri)


def _chunk_slab_kernel(pslab_ref, w_ref, o_ref):
    w = w_ref[...]
    expert = lax.shift_right_logical(w, CHUNK_EXPERT_SHIFT)
    base = jnp.zeros(w.shape, I32)
    for e in range(N_EXPERTS):
        base = jnp.where(expert == e, pslab_ref[e], base)
    o_ref[...] = base + (w & ((1 << CHUNK_EXPERT_SHIFT) - 1))


def _chunk_slabs(pslab, chunks):
    return pl.pallas_call(
        _chunk_slab_kernel,
        in_specs=[pl.BlockSpec(memory_space=pltpu.SMEM), pl.BlockSpec(memory_space=pltpu.VMEM)],
        out_specs=pl.BlockSpec(memory_space=pltpu.VMEM),
        out_shape=jax.ShapeDtypeStruct(chunks.shape, I32),
        name="moe_chunk_slabs",
    )(pslab, chunks)


def _start_chunks(tile, chunk_ref, nch_ref, make_copy):
    n = nch_ref[tile]

    def start(c, priority):
        make_copy(c, chunk_ref[tile * CHUNK_SLOTS + c], 1).start(priority=priority)

    def group(g, carry):
        for u in range(CHUNK_UNROLL):
            start(g * CHUNK_UNROLL + u, u % 2)
        return carry

    def single(c, carry):
        start(c, 0)
        return carry

    groups = n // CHUNK_UNROLL
    lax.fori_loop(0, groups, group, 0)
    lax.fori_loop(groups * CHUNK_UNROLL, n, single, 0)


def _wait_chunks(tile, nch_ref, make_copy):
    n = nch_ref[tile]
    many = n // WAIT_CHUNKS

    def wait_many(j, carry):
        make_copy(0, 0, WAIT_CHUNKS).wait()
        return carry

    def wait_one(j, carry):
        make_copy(0, 0, 1).wait()
        return carry

    lax.fori_loop(0, many, wait_many, 0)
    lax.fori_loop(many * WAIT_CHUNKS, n, wait_one, 0)


def _dispatch_kernel(chunk_ref, nch_ref, pstart_ref, pend_ref, rend_ref, pos_ref, h2_ref, xs_hbm,
                     cbuf_ref, zero_ref, zsem, sem, *, tm):
    i = pl.program_id(0)
    nt = pl.num_programs(0)
    dh = cbuf_ref.shape[-1]

    def zero_piece(slab):
        return pltpu.make_async_copy(zero_ref, xs_hbm.at[pl.ds(slab, ZERO_SLABS)], zsem)

    def zero_fill(op):
        def pieces(start, stop):
            first = (start // ZERO_SLABS) * ZERO_SLABS

            def piece(k, carry):
                op(zero_piece(first + k * ZERO_SLABS))
                return carry

            lax.fori_loop(0, (stop - first) // ZERO_SLABS, piece, 0)

        def region(e, carry):
            pieces(rend_ref[e], pend_ref[e])
            return carry

        lax.fori_loop(0, N_EXPERTS, region, 0)
        pieces(pend_ref[N_EXPERTS - 1], xs_hbm.shape[0])

    @pl.when(i == 0)
    def _():
        zero_ref[...] = jnp.zeros(zero_ref.shape, zero_ref.dtype)
        zero_fill(lambda c: c.start())
        zero_fill(lambda c: c.wait())

    pos = pos_ref[...]
    piota = lax.broadcasted_iota(I32, (GROUPED_ROWS, tm), 0).astype(jnp.int16)
    pos16 = pos.astype(jnp.int16)
    one = jnp.ones((GROUPED_ROWS, tm), BF16)
    perm = jnp.zeros((GROUPED_ROWS, tm), BF16)
    for r in range(TOP_K):
        perm = jnp.where(piota == pos16[r:r + 1, :], one, perm)
    cur = lax.rem(i, 2)
    grouped = _pack_exact_bf16_pair(_dot(perm, h2_ref[:, 0:dh]), _dot(perm, h2_ref[:, dh:]))
    cbuf_ref[cur] = grouped.reshape(cbuf_ref.shape[1:])

    def make_copy(buf):
        def build(tile_slab, buffer_slab, slabs):
            return pltpu.make_async_copy(cbuf_ref.at[buf, pl.ds(tile_slab, slabs)],
                                         xs_hbm.at[pl.ds(buffer_slab, slabs)], sem.at[buf])
        return build

    _start_chunks(i, chunk_ref, nch_ref, make_copy(cur))

    @pl.when(i > 0)
    def _():
        _wait_chunks(i - 1, nch_ref, make_copy(1 - cur))

    @pl.when(i == nt - 1)
    def _():
        _wait_chunks(i, nch_ref, make_copy(cur))


def _dispatch(chunks, nch, pstart, pend, rend, pos, h2, nrows, tm):
    n, d = h2.shape
    grid_spec = pltpu.PrefetchScalarGridSpec(
        num_scalar_prefetch=5,
        grid=(n // tm,),
        in_specs=[
            pl.BlockSpec((TOP_K, tm), lambda i, *_: (0, i)),
            pl.BlockSpec((tm, d), lambda i, *_: (i, 0)),
        ],
        out_specs=pl.BlockSpec(memory_space=pl.ANY),
        scratch_shapes=[
            pltpu.VMEM((2, GROUPED_ROWS // ROW_GROUP, ROW_GROUP, d // 2), U32),
            pltpu.VMEM((ZERO_SLABS, ROW_GROUP, d // 2), U32),
            pltpu.SemaphoreType.DMA(()),
            pltpu.SemaphoreType.DMA((2,)),
        ],
    )
    return pl.pallas_call(
        functools.partial(_dispatch_kernel, tm=tm),
        grid_spec=grid_spec,
        out_shape=jax.ShapeDtypeStruct((nrows // ROW_GROUP, ROW_GROUP, d // 2), U32),
        compiler_params=_params(("arbitrary",)),
        name="moe_dispatch",
    )(chunks, nch, pstart, pend, rend, pos, h2)


def _experts_kernel(be_ref, nu_ref, valid_ref, first_ref, slot_ref, next_ref, x_ref, wgu_hbm, wd_hbm,
                    o_ref, wgu_f_ref, wd_f_ref, wgu_b_ref, wd_b_ref, sem, *, bm):
    i = pl.program_id(0)
    valid = valid_ref[i]
    sub = bm // 2

    def weight_copies(expert, slot):
        return (pltpu.make_async_copy(wgu_hbm.at[expert], wgu_f_ref.at[slot], sem.at[slot, 0]),
                pltpu.make_async_copy(wd_hbm.at[expert], wd_f_ref.at[slot], sem.at[slot, 1]))

    @pl.when(i == 0)
    def _():
        for c in weight_copies(be_ref[0], 0):
            c.start()

    @pl.when(first_ref[i] == 1)
    def _():
        slot = slot_ref[i]
        for c in weight_copies(be_ref[i], slot):
            c.wait()

        @pl.when(next_ref[i] >= 0)
        def _():
            for c in weight_copies(next_ref[i], 1 - slot):
                c.start()

        wgu_b_ref[...] = wgu_f_ref[slot].astype(BF16)
        wd_b_ref[...] = wd_f_ref[slot].astype(BF16)

    for r0 in (0, sub):
        rows = slice(r0, r0 + sub)

        @pl.when(valid > r0)
        def _():
            lo, hi = _unpack_bf16_pair(x_ref[rows, :])
            x = jnp.concatenate([lo.astype(BF16), hi.astype(BF16)], axis=1)
            gu = _dot(x, wgu_b_ref[...])
            act = (_silu(gu[:, 0:D_EXPERT]) * gu[:, D_EXPERT:]).astype(BF16)
            y = _dot(act, wd_b_ref[...])
            o_ref[rows, :] = _pack_bf16_pair(y[:, 0:D_MODEL // 2], y[:, D_MODEL // 2:])

        @pl.when(valid <= r0)
        def _():
            o_ref[rows, :] = jnp.zeros((sub, o_ref.shape[1]), o_ref.dtype)


def _experts(block_e, nused, valid, first, slot, next_e, xs, w_gate_up_l, w_down_l, bm):
    nrows, dh = xs.shape
    d = 2 * dh
    nblk = nrows // bm
    grid_spec = pltpu.PrefetchScalarGridSpec(
        num_scalar_prefetch=6,
        grid=(nblk,),
        in_specs=[
            pl.BlockSpec((bm, dh), lambda i, be, nu, *_: (jnp.minimum(i, nu[0] - 1), 0)),
            pl.BlockSpec(memory_space=pl.ANY),
            pl.BlockSpec(memory_space=pl.ANY),
        ],
        out_specs=pl.BlockSpec((bm, dh), lambda i, *_: (i, 0)),
        scratch_shapes=[
            pltpu.VMEM((2, d, 2 * D_EXPERT), w_gate_up_l.dtype),
            pltpu.VMEM((2, D_EXPERT, d), w_down_l.dtype),
            pltpu.VMEM((d, 2 * D_EXPERT), BF16),
            pltpu.VMEM((D_EXPERT, d), BF16),
            pltpu.SemaphoreType.DMA((2, 2)),
        ],
    )
    return pl.pallas_call(
        functools.partial(_experts_kernel, bm=bm),
        grid_spec=grid_spec,
        out_shape=jax.ShapeDtypeStruct((nrows, dh), U32),
        compiler_params=_params(("arbitrary",)),
        name="moe_experts",
    )(block_e, nused, valid, first, slot, next_e, xs, w_gate_up_l, w_down_l)


def _combine_kernel(chunk_ref, nch_ref, pstart_ref, pos_ref, wl_ref, xs_ref, ga2_ref, gf_ref,
                    yb_hbm, oa_ref, ob_ref, gbuf_ref, sem, *, tm, nta):
    i = pl.program_id(0)
    nt = pl.num_programs(0)
    cur = lax.rem(i, 2)
    refs = (chunk_ref, nch_ref)

    def make_copy(buf):
        def build(tile_slab, buffer_slab, slabs):
            return pltpu.make_async_copy(yb_hbm.at[pl.ds(buffer_slab, slabs)],
                                         gbuf_ref.at[buf, pl.ds(tile_slab, slabs)], sem.at[buf])
        return build

    @pl.when(i == 0)
    def _():
        gbuf_ref[...] = jnp.zeros(gbuf_ref.shape, gbuf_ref.dtype)
        _start_chunks(0, *refs, make_copy(0))

    @pl.when(i + 1 < nt)
    def _():
        _start_chunks(i + 1, *refs, make_copy(1 - cur))

    _wait_chunks(i, nch_ref, make_copy(cur))

    lo, hi = _unpack_bf16_pair(gbuf_ref[cur].reshape(GROUPED_ROWS, gbuf_ref.shape[-1]))
    g = jnp.concatenate([lo.astype(BF16), hi.astype(BF16)], axis=1)
    pos16 = pos_ref[...].astype(jnp.int16)
    wl = wl_ref[...].astype(BF16)
    liota = lax.broadcasted_iota(I32, (tm, GROUPED_ROWS), 1).astype(jnp.int16)
    a = jnp.zeros((tm, GROUPED_ROWS), BF16)
    for r in range(TOP_K):
        a = jnp.where(liota == pos16[:, r:r + 1], jnp.broadcast_to(wl[:, r:r + 1], a.shape), a)
    routed = _dot(a, g)
    x2 = xs_ref[...] + ga2_ref[...] * routed
    ms = jnp.mean(x2 * x2, axis=-1, keepdims=True)
    y = x2 * lax.rsqrt(ms + EPS) * gf_ref[...]

    @pl.when(i < nta)
    def _():
        oa_ref[...] = y

    @pl.when(i >= nta)
    def _():
        ob_ref[...] = y


def _combine(chunks, nch, pstart, pos_t, wl_t, xs_base, ga2, gfin, yb, tm, nta):
    n, d = xs_base.shape
    ntb = n // tm - nta
    ga2_spec = pl.BlockSpec((None, tm, d), lambda i, *_: (jnp.minimum(i // nta, 1), 0, 0))
    grid_spec = pltpu.PrefetchScalarGridSpec(
        num_scalar_prefetch=3,
        grid=(n // tm,),
        in_specs=[
            pl.BlockSpec((tm, TOP_K), lambda i, *_: (i, 0)),
            pl.BlockSpec((tm, TOP_K), lambda i, *_: (i, 0)),
            pl.BlockSpec((tm, d), lambda i, *_: (i, 0)),
            ga2_spec,
            pl.BlockSpec((1, d), lambda i, *_: (0, 0)),
            pl.BlockSpec(memory_space=pl.ANY),
        ],
        out_specs=[
            pl.BlockSpec((tm, d), lambda i, *_: (jnp.minimum(i, nta - 1), 0)),
            pl.BlockSpec((tm, d), lambda i, *_: (jnp.maximum(i - nta, 0), 0)),
        ],
        scratch_shapes=[
            pltpu.VMEM((2, GROUPED_ROWS // ROW_GROUP, ROW_GROUP, d // 2), U32),
            pltpu.SemaphoreType.DMA((2,)),
        ],
    )
    return pl.pallas_call(
        functools.partial(_combine_kernel, tm=tm, nta=nta),
        grid_spec=grid_spec,
        out_shape=[jax.ShapeDtypeStruct((nta * tm, d), F32), jax.ShapeDtypeStruct((ntb * tm, d), F32)],
        compiler_params=_params(("arbitrary",)),
        name="moe_combine",
    )(chunks, nch, pstart, pos_t, wl_t, xs_base, ga2, gfin, yb)


def _moe_and_final(src_a, src_b, mods, wts, tm, bm):
    nta = src_a[0].shape[0] // tm
    ga1, sh2, sc2, ga2 = mods
    (w_out_b, g_ffn, wsgu_b, wsd_b, wr_t, rb, w_gate_up_l, w_down_l, g_final) = wts
    xs_base, h2, pos, wl, chunks, nch, cnt = _post(
        src_a, src_b, w_out_b, ga1, g_ffn, sc2, sh2, ga2, wsgu_b, wsd_b, wr_t, rb, tm)
    nt = xs_base.shape[0] // tm
    counts = cnt[:, 0].astype(I32)
    padded = (counts + bm - 1) // bm * bm
    pend = jnp.cumsum(padded)
    pstart = pend - padded
    max_rows = nt * (tm * TOP_K + N_EXPERTS * (ROW_GROUP - 1))
    nblk = -(-max_rows // bm) + N_EXPERTS
    nused = (pend[-1] // bm).astype(I32)
    blk_row = jnp.minimum(jnp.arange(nblk, dtype=I32), nused - 1) * bm
    be = jnp.sum((pend[None, :] <= blk_row[:, None]).astype(I32), axis=1)
    nch = nch[:, 0, 0]
    pslab = pstart // ROW_GROUP
    chunks = _chunk_slabs(pslab, chunks[:, 0, :]).reshape(-1)
    xs = _dispatch(chunks, nch, pslab, pend // ROW_GROUP, (pstart + counts) // ROW_GROUP, pos, h2,
                   nblk * bm, tm)
    xs = xs.reshape(nblk * bm, xs.shape[-1])
    region_end = jnp.sum(jnp.where(be[:, None] == jnp.arange(N_EXPERTS, dtype=I32)[None, :],
                                   (pstart + counts)[None, :], 0), axis=1)
    valid = jnp.clip(region_end - jnp.arange(nblk, dtype=I32) * bm, 0, bm)
    blk = jnp.arange(nblk, dtype=I32)
    first = ((blk < nused) & ((blk == 0) | (be != jnp.roll(be, 1)))).astype(I32)
    slot = (jnp.cumsum(first) - 1) % 2
    eids = jnp.arange(N_EXPERTS, dtype=I32)
    later_nonempty = (eids[None, :] > eids[:, None]) & (counts[None, :] > 0)
    next_tab = jnp.min(jnp.where(later_nonempty, eids[None, :], N_EXPERTS), axis=1)
    next_tab = jnp.where(next_tab == N_EXPERTS, -1, next_tab)
    next_e = jnp.sum(jnp.where(be[:, None] == eids[None, :], next_tab[None, :], 0), axis=1)
    yb = _experts(be, nused.reshape(1), valid, first, slot.astype(I32), next_e.astype(I32), xs,
                  w_gate_up_l, w_down_l, bm)
    yb = yb.reshape(nblk * bm // ROW_GROUP, ROW_GROUP, yb.shape[-1])
    return _combine(chunks, nch, pslab, pos.T, wl.T, xs_base, ga2, g_final, yb, tm, nta)


def _expand(mod, reps):
    if mod.shape[0] == 1:
        return mod
    return jnp.repeat(mod, reps, axis=0)


def kernel(x_prompt, x_sample, cache_k, cache_v, state_hgrn, c_prompt, c_sample, w_ada, b_ada,
           norm_mix, norm_ffn, norm_final, w_in, w_out, hg_lb_logits, hg_norm, da_lambda, da_norm,
           rel_bias_table, w_router, router_bias, w_gate_up, w_down, ws_gate_up, ws_down):
    depth = w_in.shape[0]
    assert depth == 1 and hg_lb_logits.shape[0] == 2
    bp, tp, d = x_prompt.shape
    bs, ts, _ = x_sample.shape
    assert bp == 1
    past = cache_k.shape[2]
    l = 0

    rows = -(-(bp + bs) // 8) * 8
    c_all = jnp.zeros((rows, d), F32).at[:bp].set(c_prompt).at[bp:bp + bs].set(c_sample)
    mod = _adaln(c_all, w_ada[l], b_ada[l])
    mod_p = [mod[0:bp, j * d:(j + 1) * d] for j in range(6)]
    mod_s = [_expand(mod[bp:bp + bs, j * d:(j + 1) * d], ts) for j in range(6)]

    w_in_b = w_in[l].astype(BF16)
    w_out_b = w_out[l].astype(BF16)
    wsgu_b = ws_gate_up[l].astype(BF16)
    wsd_b = ws_down[l].astype(BF16)
    wr_t = w_router[l].T
    g_mix = norm_mix[l].reshape(1, d)
    g_ffn = norm_ffn[l].reshape(1, d)
    g_final = norm_final.reshape(1, d)
    moe_w = (w_out_b, g_ffn, wsgu_b, wsd_b, wr_t, router_bias[l], w_gate_up[l], w_down[l], g_final)

    lam = _lam(da_lambda[l])

    t_att = min(ATT_TILE, tp)
    kk = jnp.arange(t_att, dtype=I32)[:, None]
    qq = jnp.arange(t_att, dtype=I32)[None, :]
    idx_diag = jnp.where((kk // CHUNK) <= (qq // CHUNK), _rel_bucket(kk - qq), MASK_BUCKET)
    idx_prev = _rel_bucket(kk - qq - t_att)
    bias_p = _bias_tiles(rel_bias_table, jnp.stack([idx_diag, idx_prev]).astype(I32),
                         shift_bucket=N_BUCKETS // 2 - 1)
    pad = 128
    qpos = past + jnp.arange(ts, dtype=I32)[:, None]
    idx_sp = _rel_bucket(jnp.arange(past, dtype=I32)[None, :] - qpos)
    kn = jnp.arange(pad, dtype=I32)[None, :]
    idx_sn = jnp.where(kn < ts, _rel_bucket(past + kn - qpos), MASK_BUCKET)
    bias_sp = _bias_tiles(rel_bias_table, idx_sp[None].astype(I32), shift_bucket=None)
    bias_sn = _bias_tiles(rel_bias_table, idx_sn[None].astype(I32), shift_bucket=None)

    xp = x_prompt.reshape(bp * tp, d)
    sh1, sc1, ga1, sh2, sc2, ga2 = mod_p
    assert ATT_TILE == INPROJ_TILE
    zh, qt, kf, vf, kb, vt = _inproj(xp, g_mix, sc1, sh1, w_in_b, t_att, True)
    s_zero = jnp.zeros((bp, HG_HEADS, HG_DIM, HG_DIM), F32)
    ohg_p, sp_new = _hgrn(zh, s_zero, hg_lb_logits, hg_norm[l], bp, tp, min(HGRN_CHUNK, tp))
    oda_p = _attn_prompt(kb, qt, vt, bias_p, lam, da_norm[l], t_att)
    src_p = (xp, ohg_p, oda_p)
    mods_p = (ga1, sh2, sc2, ga2)
    k_prompt = kf.reshape(1, bp, tp, DA_HEADS, 2 * DA_QKDIM)
    v_prompt = vf.reshape(1, bp, tp, DA_HEADS, DA_VDIM)

    ns = bs * ts
    xs_ = x_sample.reshape(ns, d)
    sh1, sc1, ga1, sh2, sc2, ga2 = mod_s
    zh, qs, kf, vf, kb, vb = _inproj(xs_, g_mix, sc1, sh1, w_in_b, ns, False)
    ohg_s, ss_new = _hgrn(zh, state_hgrn[l], hg_lb_logits, hg_norm[l], bs, ts, ts)
    oda_s = _attn_step(qs, cache_k[l], cache_v[l], kb, vb, bias_sp, bias_sn, lam, da_norm[l], bs, ts)
    assert ns == POST_TILE
    mods = tuple(jnp.stack([jnp.broadcast_to(mp, (POST_TILE, d)), ms_])
                 for mp, ms_ in zip(mods_p, (ga1, sh2, sc2, ga2)))
    y_p, y_s = _moe_and_final(src_p, (xs_, ohg_s, oda_s), mods, moe_w, POST_TILE, MOE_BLOCK_ROWS)
    k_sample = kf.reshape(1, bs, ts, DA_HEADS, 2 * DA_QKDIM)
    v_sample = vf.reshape(1, bs, ts, DA_HEADS, DA_VDIM)

    return (y_p.reshape(bp, tp, d), y_s.reshape(bs, ts, d), k_prompt, v_prompt, sp_new[None],
            k_sample, v_sample, ss_new[None].astype(x_sample.dtype))
```
